```python
import jax, jax.numpy as jnp
from jax import lax
import numpy as np

D_MODEL = 1024
BATCH = 8
SEQ = 2048
DEPTH = 4

HEAD_DIM = 64
CONV_HEADS = 4
ATTN_HEADS = 8
SGU_HEADS = 4
CONV_W = CONV_HEADS * HEAD_DIM
ATTN_W = ATTN_HEADS * HEAD_DIM
SGU_W = SGU_HEADS * HEAD_DIM
D_MIX = CONV_W + ATTN_W + SGU_W
D_IN_PROJ = 3 * CONV_W + 3 * ATTN_W + 2 * SGU_W
CONV_WIDTH = 3
Q_BLOCK = 128
CHUNK = 128
D_FF = 4 * D_MODEL
PLE_DIM = 256
EPS = 1e-6

kernel_name = "hybrid_conv_stickbreak_sgu_block"


def rms_norm(x, g):
    xf = x.astype(jnp.float32)
    y = xf * lax.rsqrt(jnp.mean(xf * xf, axis=-1, keepdims=True) + EPS)
    return (y * g.astype(jnp.float32)).astype(x.dtype)


def short_conv(x, w):
    s = x.shape[1]
    xp = jnp.pad(x, ((0, 0), (CONV_WIDTH - 1, 0), (0, 0)))
    out = w[0] * xp[:, 0:s]
    for j in range(1, CONV_WIDTH):
        out = out + w[j] * xp[:, j:j + s]
    return out


def stick_breaking_attention(q, k, v):
    s_len = q.shape[1]
    scale = HEAD_DIM ** -0.5
    outs = []
    for qb in range(s_len // Q_BLOCK):
        start = qb * Q_BLOCK
        end = start + Q_BLOCK
        qi = q[:, start:end].astype(jnp.float32)
        kj = k[:, :end].astype(jnp.float32)
        vj = v[:, :end].astype(jnp.float32)
        z = jnp.einsum('bqhd,bkhd->bhqk', qi, kj) * scale
        t_pos = start + jnp.arange(Q_BLOCK)[:, None]
        s_pos = jnp.arange(end)[None, :]
        causal = s_pos < t_pos
        log_beta = jax.nn.log_sigmoid(z)
        log_rem = jnp.where(causal, jax.nn.log_sigmoid(-z), 0.0)
        suffix = lax.cumsum(log_rem, axis=3, reverse=True) - log_rem
        weights = jnp.where(causal, jnp.exp(log_beta + suffix), 0.0)
        o = jnp.einsum('bhqk,bkhd->bqhd', weights, vj)
        outs.append(o.astype(v.dtype))
    return jnp.concatenate(outs, axis=1)


def spatial_gating(u, v, g_v, w_s, b_s):
    bsz, s_len, _ = u.shape
    v = v.reshape(bsz, s_len, SGU_HEADS, HEAD_DIM)
    v = rms_norm(v, g_v.reshape(SGU_HEADS, HEAD_DIM))
    v = v.reshape(bsz, s_len // CHUNK, CHUNK, SGU_HEADS, HEAD_DIM)
    mask = jnp.tril(jnp.ones((CHUNK, CHUNK), dtype=w_s.dtype))
    w = w_s * mask
    sv = jnp.einsum('gts,bcsge->bctge', w, v) + b_s.T[:, :, None]
    return u * sv.reshape(bsz, s_len, SGU_W)


def _fwd_setup_inputs(seed: int = 0) -> dict:
    key = jax.random.key(seed)
    ks = jax.random.split(key, 17)

    def nrm(k, shape, scale):
        return jax.random.normal(k, shape, jnp.float32) * scale

    def gain(k, shape):
        return 1.0 + 0.05 * jax.random.normal(k, shape, jnp.float32)

    return {
        "x": nrm(ks[0], (BATCH, SEQ, D_MODEL), 1.0),
        "p": nrm(ks[1], (DEPTH, BATCH, SEQ, PLE_DIM), 1.0),
        "norm1_g": gain(ks[2], (DEPTH, D_MODEL)),
        "w_in": nrm(ks[3], (DEPTH, D_MODEL, D_IN_PROJ), D_MODEL ** -0.5),
        "conv_w": nrm(ks[4], (DEPTH, CONV_WIDTH, CONV_W), CONV_WIDTH ** -0.5),
        "q_norm_g": gain(ks[5], (DEPTH, HEAD_DIM)),
        "k_norm_g": gain(ks[6], (DEPTH, HEAD_DIM)),
        "sgu_norm_g": gain(ks[7], (DEPTH, SGU_W)),
        "sgu_w": nrm(ks[8], (DEPTH, SGU_HEADS, CHUNK, CHUNK), CHUNK ** -0.5),
        "sgu_b": gain(ks[9], (DEPTH, SGU_HEADS, CHUNK)),
        "w_out": nrm(ks[10], (DEPTH, D_MIX, D_MODEL), D_MIX ** -0.5),
        "norm2_g": gain(ks[11], (DEPTH, D_MODEL)),
        "w_ff1": nrm(ks[12], (DEPTH, D_MODEL, D_FF), D_MODEL ** -0.5),
        "w_ff2": nrm(ks[13], (DEPTH, D_FF, D_MODEL), D_FF ** -0.5),
        "norm3_g": gain(ks[14], (DEPTH, D_MODEL)),
        "w_ple_gate": nrm(ks[15], (DEPTH, D_MODEL, D_MODEL), D_MODEL ** -0.5),
        "w_ple_proj": nrm(ks[16], (DEPTH, PLE_DIM, D_MODEL), PLE_DIM ** -0.5),
    }


def _fwd_reference(x, p, norm1_g, w_in, conv_w, q_norm_g, k_norm_g, sgu_norm_g, sgu_w, sgu_b,
              w_out, norm2_g, w_ff1, w_ff2, norm3_g, w_ple_gate, w_ple_proj):
    bsz, s_len, _ = x.shape
    split_idx = list(np.cumsum([CONV_W, CONV_W, CONV_W, ATTN_W, ATTN_W, ATTN_W, SGU_W]))
    h = x
    for i in range(DEPTH):
        hn = rms_norm(h, norm1_g[i])
        proj = hn @ w_in[i]
        a_b, a_c, a_h, q, k, v, c_u, c_v = jnp.split(proj, split_idx, axis=-1)
        y_a = a_b * short_conv(a_c * a_h, conv_w[i])
        q = rms_norm(q.reshape(bsz, s_len, ATTN_HEADS, HEAD_DIM), q_norm_g[i])
        k = rms_norm(k.reshape(bsz, s_len, ATTN_HEADS, HEAD_DIM), k_norm_g[i])
        v = v.reshape(bsz, s_len, ATTN_HEADS, HEAD_DIM)
        y_b = stick_breaking_attention(q, k, v).reshape(bsz, s_len, ATTN_W)
        y_c = spatial_gating(jax.nn.gelu(c_u, approximate=False),
                             jax.nn.gelu(c_v, approximate=False),
                             sgu_norm_g[i], sgu_w[i], sgu_b[i])
        h = h + jnp.concatenate([y_a, y_b, y_c], axis=-1) @ w_out[i]
        f = jnp.square(jax.nn.relu(rms_norm(h, norm2_g[i]) @ w_ff1[i]))
        h = h + f @ w_ff2[i]
        gate = jax.nn.sigmoid(rms_norm(h, norm3_g[i]) @ w_ple_gate[i])
        h = h + gate * (p[i] @ w_ple_proj[i])
    return h


import jax as _jax
import jax.numpy as _jnp

TWIN_FORMAT = 'train_step'
FWD_PARAMS = ['x', 'p', 'norm1_g', 'w_in', 'conv_w', 'q_norm_g', 'k_norm_g', 'sgu_norm_g', 'sgu_w', 'sgu_b', 'w_out', 'norm2_g', 'w_ff1', 'w_ff2', 'norm3_g', 'w_ple_gate', 'w_ple_proj']
TWIN_WEIGHTS = ['norm1_g', 'w_in', 'conv_w', 'q_norm_g', 'k_norm_g', 'sgu_norm_g', 'sgu_w', 'sgu_b', 'w_out', 'norm2_g', 'w_ff1', 'w_ff2', 'norm3_g', 'w_ple_gate', 'w_ple_proj']
TWIN_DIFF_INPUT = 'x'
TWIN_INPUTS = ['x', 'p', 'norm1_g', 'w_in', 'conv_w', 'q_norm_g', 'k_norm_g', 'sgu_norm_g', 'sgu_w', 'sgu_b', 'w_out', 'norm2_g', 'w_ff1', 'w_ff2', 'norm3_g', 'w_ple_gate', 'w_ple_proj', 'loss_target', 'm_norm1_g', 'm_w_in', 'm_conv_w', 'm_q_norm_g', 'm_k_norm_g', 'm_sgu_norm_g', 'm_sgu_w', 'm_sgu_b', 'm_w_out', 'm_norm2_g', 'm_w_ff1', 'm_w_ff2', 'm_norm3_g', 'm_w_ple_gate', 'm_w_ple_proj', 'v_norm1_g', 'v_w_in', 'v_conv_w', 'v_q_norm_g', 'v_k_norm_g', 'v_sgu_norm_g', 'v_sgu_w', 'v_sgu_b', 'v_w_out', 'v_norm2_g', 'v_w_ff1', 'v_w_ff2', 'v_norm3_g', 'v_w_ple_gate', 'v_w_ple_proj']
TWIN_OUTPUTS = ['loss', 'grad_x', 'grad_norm1_g', 'grad_w_in', 'grad_conv_w', 'grad_q_norm_g', 'grad_k_norm_g', 'grad_sgu_norm_g', 'grad_sgu_w', 'grad_sgu_b', 'grad_w_out', 'grad_norm2_g', 'grad_w_ff1', 'grad_w_ff2', 'grad_norm3_g', 'grad_w_ple_gate', 'grad_w_ple_proj', 'delta_norm1_g', 'delta_w_in', 'delta_conv_w', 'delta_q_norm_g', 'delta_k_norm_g', 'delta_sgu_norm_g', 'delta_sgu_w', 'delta_sgu_b', 'delta_w_out', 'delta_norm2_g', 'delta_w_ff1', 'delta_w_ff2', 'delta_norm3_g', 'delta_w_ple_gate', 'delta_w_ple_proj', 'new_m_norm1_g', 'new_m_w_in', 'new_m_conv_w', 'new_m_q_norm_g', 'new_m_k_norm_g', 'new_m_sgu_norm_g', 'new_m_sgu_w', 'new_m_sgu_b', 'new_m_w_out', 'new_m_norm2_g', 'new_m_w_ff1', 'new_m_w_ff2', 'new_m_norm3_g', 'new_m_w_ple_gate', 'new_m_w_ple_proj', 'new_v_norm1_g', 'new_v_w_in', 'new_v_conv_w', 'new_v_q_norm_g', 'new_v_k_norm_g', 'new_v_sgu_norm_g', 'new_v_sgu_w', 'new_v_sgu_b', 'new_v_w_out', 'new_v_norm2_g', 'new_v_w_ff1', 'new_v_w_ff2', 'new_v_norm3_g', 'new_v_w_ple_gate', 'new_v_w_ple_proj']
TWIN_LEAF_KINDS = {'loss': 'loss', 'grad_x': 'grad_x', 'grad_norm1_g': 'grad_w', 'grad_w_in': 'grad_w', 'grad_conv_w': 'grad_w', 'grad_q_norm_g': 'grad_w', 'grad_k_norm_g': 'grad_w', 'grad_sgu_norm_g': 'grad_w', 'grad_sgu_w': 'grad_w', 'grad_sgu_b': 'grad_w', 'grad_w_out': 'grad_w', 'grad_norm2_g': 'grad_w', 'grad_w_ff1': 'grad_w', 'grad_w_ff2': 'grad_w', 'grad_norm3_g': 'grad_w', 'grad_w_ple_gate': 'grad_w', 'grad_w_ple_proj': 'grad_w', 'delta_norm1_g': 'delta_w', 'delta_w_in': 'delta_w', 'delta_conv_w': 'delta_w', 'delta_q_norm_g': 'delta_w', 'delta_k_norm_g': 'delta_w', 'delta_sgu_norm_g': 'delta_w', 'delta_sgu_w': 'delta_w', 'delta_sgu_b': 'delta_w', 'delta_w_out': 'delta_w', 'delta_norm2_g': 'delta_w', 'delta_w_ff1': 'delta_w', 'delta_w_ff2': 'delta_w', 'delta_norm3_g': 'delta_w', 'delta_w_ple_gate': 'delta_w', 'delta_w_ple_proj': 'delta_w', 'new_m_norm1_g': 'new_m', 'new_m_w_in': 'new_m', 'new_m_conv_w': 'new_m', 'new_m_q_norm_g': 'new_m', 'new_m_k_norm_g': 'new_m', 'new_m_sgu_norm_g': 'new_m', 'new_m_sgu_w': 'new_m', 'new_m_sgu_b': 'new_m', 'new_m_w_out': 'new_m', 'new_m_norm2_g': 'new_m', 'new_m_w_ff1': 'new_m', 'new_m_w_ff2': 'new_m', 'new_m_norm3_g': 'new_m', 'new_m_w_ple_gate': 'new_m', 'new_m_w_ple_proj': 'new_m', 'new_v_norm1_g': 'new_v', 'new_v_w_in': 'new_v', 'new_v_conv_w': 'new_v', 'new_v_q_norm_g': 'new_v', 'new_v_k_norm_g': 'new_v', 'new_v_sgu_norm_g': 'new_v', 'new_v_sgu_w': 'new_v', 'new_v_sgu_b': 'new_v', 'new_v_w_out': 'new_v', 'new_v_norm2_g': 'new_v', 'new_v_w_ff1': 'new_v', 'new_v_w_ff2': 'new_v', 'new_v_norm3_g': 'new_v', 'new_v_w_ple_gate': 'new_v', 'new_v_w_ple_proj': 'new_v'}


def _forward(args):
    return _fwd_reference(*[args[k] for k in FWD_PARAMS])


def _output_shape():
    out = _jax.eval_shape(lambda: _forward(_fwd_setup_inputs(0)))
    return out.shape, out.dtype

N_MICROBATCH = 1
ADAM_LR = 0.001
ADAM_B1 = 0.9
ADAM_B2 = 0.999
ADAM_EPS = 1e-08
ADAM_WD = 0.01
ADAM_STEP = 10
PER_EXAMPLE_BATCH_AXIS = {'x': 0, 'p': 1, 'loss_target': 0}
SHARED_INPUTS = []
_WEIGHT_DTYPES = {'norm1_g': _jnp.float32, 'w_in': _jnp.float32, 'conv_w': _jnp.float32, 'q_norm_g': _jnp.float32, 'k_norm_g': _jnp.float32, 'sgu_norm_g': _jnp.float32, 'sgu_w': _jnp.float32, 'sgu_b': _jnp.float32, 'w_out': _jnp.float32, 'norm2_g': _jnp.float32, 'w_ff1': _jnp.float32, 'w_ff2': _jnp.float32, 'norm3_g': _jnp.float32, 'w_ple_gate': _jnp.float32, 'w_ple_proj': _jnp.float32}
MOMENT_SCALE = {'norm1_g': 1.942504e+01, 'w_in': 3.103410e+00, 'conv_w': 9.076067e+00, 'q_norm_g': 6.784872e+00, 'k_norm_g': 6.744747e+00, 'sgu_norm_g': 4.379555e+00, 'sgu_w': 9.764505e-01, 'sgu_b': 3.105969e+00, 'w_out': 6.231215e+00, 'norm2_g': 4.876011e+01, 'w_ff1': 3.309685e+00, 'w_ff2': 1.331359e+01, 'norm3_g': 6.828633e-01, 'w_ple_gate': 4.943412e-01, 'w_ple_proj': 4.800615e-01}


def _to_microbatches(a, axis):
    t = _jnp.moveaxis(a, axis, 0)
    t = t.reshape((N_MICROBATCH, t.shape[0] // N_MICROBATCH) + t.shape[1:])
    return _jnp.moveaxis(t, 1, axis + 1)


def setup_inputs(seed: int = 0) -> dict:
    inp = _fwd_setup_inputs(seed)
    key = _jax.random.fold_in(_jax.random.key(seed), 7919)
    shape, _ = _output_shape()
    out = dict(inp)
    out["loss_target"] = _jax.random.normal(_jax.random.fold_in(key, 0), shape, _jnp.float32)
    for i, name in enumerate(TWIN_WEIGHTS):
        w = inp[name].astype(_jnp.float32)
        if MOMENT_SCALE is None:
            s = _jnp.sqrt(_jnp.mean(_jnp.square(w)) + 1e-30)
        else:
            s = MOMENT_SCALE[name]
        km, kv = _jax.random.split(_jax.random.fold_in(key, i + 1))
        out[name] = w
        out["m_" + name] = s * _jax.random.normal(km, w.shape, _jnp.float32)
        out["v_" + name] = (s * s) * _jax.random.uniform(kv, w.shape, _jnp.float32, 0.5, 1.5)
    if N_MICROBATCH > 1:
        for name, axis in PER_EXAMPLE_BATCH_AXIS.items():
            out[name] = _to_microbatches(out[name], axis)
    return {'x': out['x'], 'p': out['p'], 'norm1_g': out['norm1_g'], 'w_in': out['w_in'], 'conv_w': out['conv_w'], 'q_norm_g': out['q_norm_g'], 'k_norm_g': out['k_norm_g'], 'sgu_norm_g': out['sgu_norm_g'], 'sgu_w': out['sgu_w'], 'sgu_b': out['sgu_b'], 'w_out': out['w_out'], 'norm2_g': out['norm2_g'], 'w_ff1': out['w_ff1'], 'w_ff2': out['w_ff2'], 'norm3_g': out['norm3_g'], 'w_ple_gate': out['w_ple_gate'], 'w_ple_proj': out['w_ple_proj'], 'loss_target': out['loss_target'], 'm_norm1_g': out['m_norm1_g'], 'm_w_in': out['m_w_in'], 'm_conv_w': out['m_conv_w'], 'm_q_norm_g': out['m_q_norm_g'], 'm_k_norm_g': out['m_k_norm_g'], 'm_sgu_norm_g': out['m_sgu_norm_g'], 'm_sgu_w': out['m_sgu_w'], 'm_sgu_b': out['m_sgu_b'], 'm_w_out': out['m_w_out'], 'm_norm2_g': out['m_norm2_g'], 'm_w_ff1': out['m_w_ff1'], 'm_w_ff2': out['m_w_ff2'], 'm_norm3_g': out['m_norm3_g'], 'm_w_ple_gate': out['m_w_ple_gate'], 'm_w_ple_proj': out['m_w_ple_proj'], 'v_norm1_g': out['v_norm1_g'], 'v_w_in': out['v_w_in'], 'v_conv_w': out['v_conv_w'], 'v_q_norm_g': out['v_q_norm_g'], 'v_k_norm_g': out['v_k_norm_g'], 'v_sgu_norm_g': out['v_sgu_norm_g'], 'v_sgu_w': out['v_sgu_w'], 'v_sgu_b': out['v_sgu_b'], 'v_w_out': out['v_w_out'], 'v_norm2_g': out['v_norm2_g'], 'v_w_ff1': out['v_w_ff1'], 'v_w_ff2': out['v_w_ff2'], 'v_norm3_g': out['v_norm3_g'], 'v_w_ple_gate': out['v_w_ple_gate'], 'v_w_ple_proj': out['v_w_ple_proj']}


def _loss(weights, diff, rest, loss_target):
    with _jax.named_scope("forward"):
        args = {**rest, TWIN_DIFF_INPUT: diff, **{k: w.astype(_WEIGHT_DTYPES[k]) for k, w in weights.items()}}
        y = _forward(args)
    with _jax.named_scope("loss_head"):
        err = _jnp.square(y.astype(_jnp.float32) - loss_target)
        return 0.5 * _jnp.sum(_jnp.mean(err, axis=-1)) if err.ndim else 0.5 * err


def _adamw(w, g, m, v):
    m = ADAM_B1 * m + (1.0 - ADAM_B1) * g
    v = ADAM_B2 * v + (1.0 - ADAM_B2) * _jnp.square(g)
    m_hat = m / (1.0 - ADAM_B1 ** ADAM_STEP)
    v_hat = v / (1.0 - ADAM_B2 ** ADAM_STEP)
    delta = -ADAM_LR * (m_hat / (_jnp.sqrt(v_hat) + ADAM_EPS) + ADAM_WD * w)
    return delta, m, v


def reference(x, p, norm1_g, w_in, conv_w, q_norm_g, k_norm_g, sgu_norm_g, sgu_w, sgu_b, w_out, norm2_g, w_ff1, w_ff2, norm3_g, w_ple_gate, w_ple_proj, loss_target, m_norm1_g, m_w_in, m_conv_w, m_q_norm_g, m_k_norm_g, m_sgu_norm_g, m_sgu_w, m_sgu_b, m_w_out, m_norm2_g, m_w_ff1, m_w_ff2, m_norm3_g, m_w_ple_gate, m_w_ple_proj, v_norm1_g, v_w_in, v_conv_w, v_q_norm_g, v_k_norm_g, v_sgu_norm_g, v_sgu_w, v_sgu_b, v_w_out, v_norm2_g, v_w_ff1, v_w_ff2, v_norm3_g, v_w_ple_gate, v_w_ple_proj):
    given = dict(x=x, p=p, norm1_g=norm1_g, w_in=w_in, conv_w=conv_w, q_norm_g=q_norm_g, k_norm_g=k_norm_g, sgu_norm_g=sgu_norm_g, sgu_w=sgu_w, sgu_b=sgu_b, w_out=w_out, norm2_g=norm2_g, w_ff1=w_ff1, w_ff2=w_ff2, norm3_g=norm3_g, w_ple_gate=w_ple_gate, w_ple_proj=w_ple_proj, loss_target=loss_target, m_norm1_g=m_norm1_g, m_w_in=m_w_in, m_conv_w=m_conv_w, m_q_norm_g=m_q_norm_g, m_k_norm_g=m_k_norm_g, m_sgu_norm_g=m_sgu_norm_g, m_sgu_w=m_sgu_w, m_sgu_b=m_sgu_b, m_w_out=m_w_out, m_norm2_g=m_norm2_g, m_w_ff1=m_w_ff1, m_w_ff2=m_w_ff2, m_norm3_g=m_norm3_g, m_w_ple_gate=m_w_ple_gate, m_w_ple_proj=m_w_ple_proj, v_norm1_g=v_norm1_g, v_w_in=v_w_in, v_conv_w=v_conv_w, v_q_norm_g=v_q_norm_g, v_k_norm_g=v_k_norm_g, v_sgu_norm_g=v_sgu_norm_g, v_sgu_w=v_sgu_w, v_sgu_b=v_sgu_b, v_w_out=v_w_out, v_norm2_g=v_norm2_g, v_w_ff1=v_w_ff1, v_w_ff2=v_w_ff2, v_norm3_g=v_norm3_g, v_w_ple_gate=v_w_ple_gate, v_w_ple_proj=v_w_ple_proj)
    weights = {n: given[n] for n in TWIN_WEIGHTS}
    shared = {n: given[n] for n in SHARED_INPUTS}
    per_example = {n: given[n] for n in ['x', 'p']}
    grad_fn = _jax.value_and_grad(_loss, argnums=(0, 1))

    def one_microbatch(ex, loss_target):
        ex = dict(ex)
        diff = ex.pop(TWIN_DIFF_INPUT)
        return grad_fn(weights, diff, {**shared, **ex}, loss_target)

    if N_MICROBATCH == 1:
        loss, (grad_w, grad_x) = one_microbatch(per_example, given["loss_target"])
    else:
        def body(carry, xs):
            loss_sum, grad_sum = carry
            l_k, (gw_k, gx_k) = one_microbatch(xs[0], xs[1])
            with _jax.named_scope("update"):
                return (loss_sum + l_k, _jax.tree.map(_jnp.add, grad_sum, gw_k)), gx_k

        init = (_jnp.zeros((), _jnp.float32), _jax.tree.map(_jnp.zeros_like, weights))
        (loss, grad_w), grad_x = _jax.lax.scan(body, init, (per_example, given["loss_target"]))
    with _jax.named_scope("update"):
        delta_w, new_m, new_v = {}, {}, {}
        for n in TWIN_WEIGHTS:
            delta_w[n], new_m[n], new_v[n] = _adamw(weights[n], grad_w[n], given["m_" + n], given["v_" + n])
    return (loss, grad_x, *[grad_w[n] for n in TWIN_WEIGHTS], *[delta_w[n] for n in TWIN_WEIGHTS],
            *[new_m[n] for n in TWIN_WEIGHTS], *[new_v[n] for n in TWIN_WEIGHTS])
```

```python
import functools
import math

import jax
import jax.numpy as jnp
from jax import lax
from jax.experimental import pallas as pl
from jax.experimental.pallas import tpu as pltpu

F32 = jnp.float32
BF = jnp.bfloat16
MESH = pl.DeviceIdType.MESH
HIGHEST = lax.Precision.HIGHEST

EPS = 1e-6
HEAD_DIM = 64
CONV_W = 256
ATTN_W = 512
SGU_W = 256
CHUNK = 128
N_CHIPS = 4
SCALE = HEAD_DIM ** -0.5
LANES = 128
VMEM_LIMIT = 56 * 1024 * 1024

ADAM_LR = 0.001
ADAM_B1 = 0.9
ADAM_B2 = 0.999
ADAM_EPS = 1e-08
ADAM_WD = 0.01
ADAM_STEP = 10

NT_DIMS = (((1,), (1,)), ((), ()))
TN_DIMS = (((0,), (0,)), ((), ()))


def _params(*sem):
    return pltpu.CompilerParams(dimension_semantics=sem if sem else None, vmem_limit_bytes=VMEM_LIMIT)


def _sds(shape, dtype):
    return jax.ShapeDtypeStruct(shape, dtype)


def _erf(x):
    return lax.erf(x)


def _gelu(x):
    return 0.5 * x * (1.0 + _erf(x * (2.0 ** -0.5)))


def _gelu_grad(x):
    return 0.5 * (1.0 + _erf(x * (2.0 ** -0.5))) + x * jnp.exp(-0.5 * x * x) * (1.0 / math.sqrt(2.0 * math.pi))


def _log_sigmoid(z):
    return jnp.minimum(z, 0.0) - jnp.log1p(jnp.exp(-jnp.abs(z)))


def _head_mean_matrix(width):
    r = lax.broadcasted_iota(jnp.int32, (width, width), 0) // HEAD_DIM
    c = lax.broadcasted_iota(jnp.int32, (width, width), 1) // HEAD_DIM
    return jnp.where(r == c, 1.0 / HEAD_DIM, 0.0).astype(F32)


def _head_mean(x, m):
    return jnp.dot(x, m, precision=HIGHEST, preferred_element_type=F32)


def _split_dot(a, u):
    hi = a.astype(BF)
    lo = (a - hi.astype(F32)).astype(BF)
    return jnp.dot(hi, u, preferred_element_type=F32) + jnp.dot(lo, u, preferred_element_type=F32)


def rmsnorm(name, h, g):
    S, D = h.shape
    tm = min(S, 512)

    def body(h_ref, g_ref, o_ref):
        x = h_ref[...]
        r = lax.rsqrt(jnp.mean(x * x, axis=-1, keepdims=True) + EPS)
        o_ref[...] = ((x * r) * g_ref[...]).astype(BF)

    return pl.pallas_call(
        body, name=name, grid=(S // tm,),
        in_specs=[pl.BlockSpec((tm, D), lambda i: (i, 0)), pl.BlockSpec((1, D), lambda i: (0, 0))],
        out_specs=pl.BlockSpec((tm, D), lambda i: (i, 0)),
        out_shape=_sds((S, D), BF), compiler_params=_params("parallel"),
    )(h, g.reshape(1, D))


def mm_nn(name, x, w, *, extras=(), pro=None, epi=None, out_dtypes=None, tm=512, tn=512):
    S, K = x.shape
    if w.ndim == 3:
        J, _, tn = w.shape
        N = J * tn
        w_spec = pl.BlockSpec((None, K, tn), lambda n, m: (n, 0, 0))
    else:
        N = w.shape[1]
        tn = min(tn, N)
        w_spec = pl.BlockSpec((K, tn), lambda n, m: (0, n))
    tm = min(tm, S)
    out_dtypes = (BF,) if out_dtypes is None else out_dtypes
    n_ex, n_out = len(extras), len(out_dtypes)

    def body(x_ref, w_ref, *rest):
        xv = x_ref[...]
        if pro is not None:
            xv = pro(xv)
        acc = jnp.dot(xv.astype(BF), w_ref[...], preferred_element_type=F32)
        outs = (acc,) if epi is None else epi(acc, *[e[...] for e in rest[:n_ex]])
        for o_ref, o in zip(rest[n_ex:], outs):
            o_ref[...] = o.astype(o_ref.dtype)

    tile = pl.BlockSpec((tm, tn), lambda n, m: (m, n))
    out = pl.pallas_call(
        body, name=name, grid=(N // tn, S // tm),
        in_specs=[pl.BlockSpec((tm, K), lambda n, m: (m, 0)), w_spec] + [tile] * n_ex,
        out_specs=[tile] * n_out,
        out_shape=[_sds((S, N), d) for d in out_dtypes],
        compiler_params=_params("parallel", "parallel"),
    )(x, w, *extras)
    return out[0] if n_out == 1 else out


def mm_nt(name, dy, w, *, extras=(), epi=None, tm=256):
    S, N = dy.shape
    K = w.shape[0]
    tm = min(tm, S)
    n_ex = len(extras)

    def body(dy_ref, w_ref, *rest):
        acc = lax.dot_general(dy_ref[...].astype(BF), w_ref[...], NT_DIMS, preferred_element_type=F32)
        if epi is not None:
            acc = epi(acc, *[e[...] for e in rest[:n_ex]])
        rest[n_ex][...] = acc.astype(BF)

    row = pl.BlockSpec((tm, K), lambda i: (i, 0))
    return pl.pallas_call(
        body, name=name, grid=(S // tm,),
        in_specs=[pl.BlockSpec((tm, N), lambda i: (i, 0)), pl.BlockSpec((K, N), lambda i: (0, 0))] + [row] * n_ex,
        out_specs=row, out_shape=_sds((S, K), BF), compiler_params=_params("parallel"),
    )(dy, w, *extras)


def mm_nt_rmsbwd(name, dy, w, h, g, dres, *, tm=256):
    S, N = dy.shape
    D = h.shape[1]
    tm = min(tm, S)
    blocked = w.ndim == 3
    nj = w.shape[2] if blocked else N

    def body(dy_ref, w_ref, h_ref, g_ref, dres_ref, dh_ref, dg_ref):
        i = pl.program_id(0)
        if blocked:
            dyn = None
            for j in range(w.shape[0]):
                part = lax.dot_general(dy_ref[:, j * nj:(j + 1) * nj].astype(BF), w_ref[j], NT_DIMS,
                                       preferred_element_type=F32)
                dyn = part if dyn is None else dyn + part
        else:
            dyn = lax.dot_general(dy_ref[...].astype(BF), w_ref[...], NT_DIMS, preferred_element_type=F32)
        x = h_ref[...]
        r = lax.rsqrt(jnp.mean(x * x, axis=-1, keepdims=True) + EPS)
        t = dyn * g_ref[...]
        dh_ref[...] = dres_ref[...] + r * t - x * (r * r * r) * jnp.mean(t * x, axis=-1, keepdims=True)
        part = jnp.sum(dyn * (x * r), axis=0, keepdims=True)

        @pl.when(i == 0)
        def _():
            dg_ref[...] = part

        @pl.when(i > 0)
        def _():
            dg_ref[...] += part

    w_spec = pl.BlockSpec(w.shape, (lambda i: (0, 0, 0)) if blocked else (lambda i: (0, 0)))
    row = pl.BlockSpec((tm, D), lambda i: (i, 0))
    vec = pl.BlockSpec((1, D), lambda i: (0, 0))
    return pl.pallas_call(
        body, name=name, grid=(S // tm,),
        in_specs=[pl.BlockSpec((tm, N), lambda i: (i, 0)), w_spec, row, vec, row],
        out_specs=[row, vec], out_shape=[_sds((S, D), F32), _sds((1, D), F32)],
        compiler_params=_params("arbitrary"),
    )(dy, w, h, g.reshape(1, D), dres)


def mm_tn(name, x, dy, *, pro_x=None, col_blocks=None, tk=1024, tn=1024):
    S, K = x.shape
    N = dy.shape[1]
    tk = min(tk, K)
    if col_blocks is not None:
        tn = N // col_blocks
        out_shape = _sds((col_blocks, K, tn), BF)
        out_spec = pl.BlockSpec((None, tk, tn), lambda k, n: (n, k, 0))
    else:
        tn = min(tn, N)
        out_shape = _sds((K, N), BF)
        out_spec = pl.BlockSpec((tk, tn), lambda k, n: (k, n))

    def body(x_ref, dy_ref, o_ref):
        xv = x_ref[...]
        if pro_x is not None:
            xv = pro_x(xv)
        o_ref[...] = lax.dot_general(xv.astype(BF), dy_ref[...].astype(BF), TN_DIMS,
                                     preferred_element_type=F32).astype(BF)

    return pl.pallas_call(
        body, name=name, grid=(K // tk, N // tn),
        in_specs=[pl.BlockSpec((S, tk), lambda k, n: (0, k)), pl.BlockSpec((S, tn), lambda k, n: (0, n))],
        out_specs=out_spec, out_shape=out_shape, compiler_params=_params("parallel", "parallel"),
    )(x, dy)


def _conv_parts(ac_ref, ah_ref, cw):
    a_c = ac_ref[...].astype(F32)
    a_h = ah_ref[...].astype(F32)
    x = a_c * a_h
    row = lax.broadcasted_iota(jnp.int32, x.shape, 0)
    x1 = jnp.where(row >= 1, pltpu.roll(x, 1, 0), 0.0)
    x2 = jnp.where(row >= 2, pltpu.roll(x, 2, 0), 0.0)
    cv = cw[0:1] * x2 + cw[1:2] * x1 + cw[2:3] * x
    return a_c, a_h, x, x1, x2, cv, row


def conv_fwd(name, proj, cw):
    S = proj.shape[0]

    def body(ab_ref, ac_ref, ah_ref, cw_ref, o_ref):
        cv = _conv_parts(ac_ref, ah_ref, cw_ref[...])[5]
        o_ref[...] = (ab_ref[...].astype(F32) * cv).astype(BF)

    col = lambda j: pl.BlockSpec((S, CONV_W), lambda i, j=j: (0, j))
    return pl.pallas_call(
        body, name=name, grid=(1,),
        in_specs=[col(0), col(1), col(2), pl.BlockSpec((3, CONV_W), lambda i: (0, 0))],
        out_specs=pl.BlockSpec((S, CONV_W), lambda i: (0, 0)),
        out_shape=_sds((S, CONV_W), BF), compiler_params=_params("arbitrary"),
    )(proj, proj, proj, cw)


def conv_bwd(name, dy, proj, cw):
    S = proj.shape[0]

    def body(dy_ref, ab_ref, ac_ref, ah_ref, cw_ref, dab_ref, dac_ref, dah_ref, dcw_ref):
        w = cw_ref[...]
        a_c, a_h, x, x1, x2, cv, row = _conv_parts(ac_ref, ah_ref, w)
        d = dy_ref[...].astype(F32)
        dab_ref[...] = (d * cv).astype(BF)
        dcv = d * ab_ref[...].astype(F32)
        d1 = jnp.where(row < S - 1, pltpu.roll(dcv, S - 1, 0), 0.0)
        d2 = jnp.where(row < S - 2, pltpu.roll(dcv, S - 2, 0), 0.0)
        dx = w[2:3] * dcv + w[1:2] * d1 + w[0:1] * d2
        dac_ref[...] = (dx * a_h).astype(BF)
        dah_ref[...] = (dx * a_c).astype(BF)
        dcw_ref[0:1, :] = jnp.sum(dcv * x2, axis=0, keepdims=True)
        dcw_ref[1:2, :] = jnp.sum(dcv * x1, axis=0, keepdims=True)
        dcw_ref[2:3, :] = jnp.sum(dcv * x, axis=0, keepdims=True)

    col = lambda j: pl.BlockSpec((S, CONV_W), lambda i, j=j: (0, j))
    one = pl.BlockSpec((S, CONV_W), lambda i: (0, 0))
    small = pl.BlockSpec((3, CONV_W), lambda i: (0, 0))
    return pl.pallas_call(
        body, name=name, grid=(1,),
        in_specs=[col(0), col(0), col(1), col(2), small],
        out_specs=[one, one, one, small],
        out_shape=[_sds((S, CONV_W), BF)] * 3 + [_sds((3, CONV_W), F32)],
        compiler_params=_params("arbitrary"),
    )(dy, proj, proj, proj, cw)


SGU_HEADS = SGU_W // HEAD_DIM
CU_BLOCK = 2304 // SGU_W
CV_BLOCK = 2560 // SGU_W


def _sgu_common(cu_ref, cv_ref, gv_ref, tm):
    c_u = cu_ref[...].astype(F32)
    c_v = cv_ref[...].astype(F32)
    hm = _head_mean_matrix(SGU_W)
    u = _gelu(c_u)
    vg = _gelu(c_v)
    r = lax.rsqrt(_head_mean(vg * vg, hm) + EPS)
    vv = (vg * r) * gv_ref[...]
    head = lax.broadcasted_iota(jnp.int32, (CHUNK, SGU_W), 1) // HEAD_DIM
    tri = (lax.broadcasted_iota(jnp.int32, (CHUNK, CHUNK), 0) >=
           lax.broadcasted_iota(jnp.int32, (CHUNK, CHUNK), 1))
    return c_u, c_v, hm, u, vg, r, vv, head, tri


def _sgu_mix(w_ref, tri, head, vvc, bias):
    sv = bias
    for g in range(SGU_HEADS):
        wg = jnp.where(tri, w_ref[g], 0.0).astype(BF)
        sv = sv + jnp.where(head == g, jnp.dot(wg, vvc, preferred_element_type=F32), 0.0)
    return sv


def sgu_fwd(name, proj, gv, w, bias):
    S = proj.shape[0]
    tm = min(S, 512)

    def body(cu_ref, cv_ref, gv_ref, w_ref, b_ref, o_ref):
        _, _, _, u, _, _, vv, head, tri = _sgu_common(cu_ref, cv_ref, gv_ref, tm)
        vvb = vv.astype(BF)
        for ch in range(tm // CHUNK):
            rows = slice(ch * CHUNK, (ch + 1) * CHUNK)
            sv = _sgu_mix(w_ref, tri, head, vvb[rows], b_ref[...])
            o_ref[rows, :] = (u[rows] * sv).astype(BF)

    const = lambda shape: pl.BlockSpec(shape, lambda i: (0,) * len(shape))
    return pl.pallas_call(
        body, name=name, grid=(S // tm,),
        in_specs=[pl.BlockSpec((tm, SGU_W), lambda i: (i, CU_BLOCK)), pl.BlockSpec((tm, SGU_W), lambda i: (i, CV_BLOCK)),
                  const((1, SGU_W)), const((SGU_HEADS, CHUNK, CHUNK)), const((CHUNK, SGU_W))],
        out_specs=pl.BlockSpec((tm, SGU_W), lambda i: (i, 0)),
        out_shape=_sds((S, SGU_W), BF), compiler_params=_params("parallel"),
    )(proj, proj, gv, w, bias)


def sgu_bwd(name, dy, proj, gv, w, bias):
    S = proj.shape[0]
    tm = min(S, 512)

    def body(dy_ref, cu_ref, cv_ref, gv_ref, w_ref, b_ref, dcu_ref, dcv_ref, dw_ref, db_ref, dgv_ref, dvv_s):
        i = pl.program_id(0)
        c_u, c_v, hm, u, vg, r, vv, head, tri = _sgu_common(cu_ref, cv_ref, gv_ref, tm)
        vvb = vv.astype(BF)
        d = dy_ref[...].astype(F32)
        ind = (lax.broadcasted_iota(jnp.int32, (SGU_W, LANES), 0) // HEAD_DIM ==
               lax.broadcasted_iota(jnp.int32, (SGU_W, LANES), 1)).astype(F32)
        dw_acc = [jnp.zeros((CHUNK, CHUNK), F32) for _ in range(SGU_HEADS)]
        db_acc = jnp.zeros((CHUNK, LANES), F32)
        for ch in range(tm // CHUNK):
            rows = slice(ch * CHUNK, (ch + 1) * CHUNK)
            sv = _sgu_mix(w_ref, tri, head, vvb[rows], b_ref[...])
            dcu_ref[rows, :] = (d[rows] * sv * _gelu_grad(c_u[rows])).astype(BF)
            dsv = d[rows] * u[rows]
            db_acc = db_acc + jnp.dot(dsv, ind, precision=HIGHEST, preferred_element_type=F32)
            dvv = jnp.zeros((CHUNK, SGU_W), F32)
            for g in range(SGU_HEADS):
                dsv_g = jnp.where(head == g, dsv, 0.0).astype(BF)
                wg = jnp.where(tri, w_ref[g], 0.0).astype(BF)
                dvv = dvv + lax.dot_general(wg, dsv_g, TN_DIMS, preferred_element_type=F32)
                dw_acc[g] = dw_acc[g] + lax.dot_general(dsv_g, vvb[rows], NT_DIMS, preferred_element_type=F32)
            dvv_s[rows, :] = dvv
        dvv = dvv_s[...]
        gvv = gv_ref[...]
        t = dvv * gvv
        dvg = r * t - vg * (r * r * r) * _head_mean(t * vg, hm)
        dcv_ref[...] = (dvg * _gelu_grad(c_v)).astype(BF)
        dgv = jnp.sum(dvv * (vg * r), axis=0, keepdims=True)

        @pl.when(i == 0)
        def _():
            for g in range(SGU_HEADS):
                dw_ref[g] = jnp.where(tri, dw_acc[g], 0.0)
            db_ref[...] = db_acc
            dgv_ref[...] = dgv

        @pl.when(i > 0)
        def _():
            for g in range(SGU_HEADS):
                dw_ref[g] += jnp.where(tri, dw_acc[g], 0.0)
            db_ref[...] += db_acc
            dgv_ref[...] += dgv

    const = lambda shape: pl.BlockSpec(shape, lambda i: (0,) * len(shape))
    tile = pl.BlockSpec((tm, SGU_W), lambda i: (i, 0))
    return pl.pallas_call(
        body, name=name, grid=(S // tm,),
        in_specs=[pl.BlockSpec((tm, SGU_W), lambda i: (i, 3)),
                  pl.BlockSpec((tm, SGU_W), lambda i: (i, CU_BLOCK)), pl.BlockSpec((tm, SGU_W), lambda i: (i, CV_BLOCK)),
                  const((1, SGU_W)), const((SGU_HEADS, CHUNK, CHUNK)), const((CHUNK, SGU_W))],
        out_specs=[tile, tile, const((SGU_HEADS, CHUNK, CHUNK)), const((CHUNK, LANES)), const((1, SGU_W))],
        out_shape=[_sds((S, SGU_W), BF), _sds((S, SGU_W), BF), _sds((SGU_HEADS, CHUNK, CHUNK), F32),
                   _sds((CHUNK, LANES), F32), _sds((1, SGU_W), F32)],
        scratch_shapes=[pltpu.VMEM((tm, SGU_W), F32)],
        compiler_params=_params("arbitrary"),
    )(dy, proj, proj, gv, w, bias)


HEAD_PAIRS = ATTN_W // LANES
Q_BLOCK0 = 768 // LANES
K_BLOCK0 = 1280 // LANES
V_BLOCK0 = 1792 // LANES


def _attn_tile(S):
    return min(S, 256)


def _qk_norm(x, g, hm):
    r = lax.rsqrt(_head_mean(x * x, hm) + EPS)
    return r, (x * r) * g


def _logits(qh, kb, causal):
    z = lax.dot_general(qh, kb, NT_DIMS, preferred_element_type=F32)
    lb = _log_sigmoid(z)
    return lb, jnp.where(causal, lb - z, 0.0)


def attn_fwd(name, proj, gq, gk):
    S = proj.shape[0]
    T = _attn_tile(S)
    nq = S // T

    def body(q_ref, k_ref, v_ref, gq_ref, gk_ref, o_ref, tot_ref, kn_s):
        qi = pl.program_id(1)
        hm = _head_mean_matrix(LANES)

        @pl.when(qi == 0)
        def _():
            kn_s[...] = _qk_norm(k_ref[...].astype(F32), gk_ref[...], hm)[1].astype(BF)

        qn = _qk_norm(q_ref[...].astype(F32), gq_ref[...], hm)[1]
        lane = lax.broadcasted_iota(jnp.int32, (T, LANES), 1)
        qh = [jnp.where(lane < HEAD_DIM, qn, 0.0).astype(BF), jnp.where(lane >= HEAD_DIM, qn, 0.0).astype(BF)]
        rowi = lax.broadcasted_iota(jnp.int32, (T, T), 0)
        coli = lax.broadcasted_iota(jnp.int32, (T, T), 1)
        u_excl = (rowi > coli).astype(BF)

        def step(it, carry):
            j = qi - it
            off = pl.multiple_of(j * T, T)
            kb = kn_s[pl.ds(off, T), :]
            vb = v_ref[pl.ds(off, T), :].astype(BF)
            causal = (coli + j * T) < (rowi + qi * T)
            out = []
            for h in range(2):
                o, run = carry[2 * h], carry[2 * h + 1]
                lb, lr = _logits(qh[h], kb, causal)
                a = jnp.where(causal, jnp.exp(lb + _split_dot(lr, u_excl) + run), 0.0)
                out += [o + jnp.dot(a.astype(BF), vb, preferred_element_type=F32),
                        run + jnp.sum(lr, axis=-1, keepdims=True)]
            return tuple(out)

        zero = (jnp.zeros((T, LANES), F32), jnp.zeros((T, 1), F32))
        res = lax.fori_loop(0, qi + 1, step, zero + zero)
        o_ref[...] = jnp.where(lane < HEAD_DIM, res[0], res[2]).astype(BF)
        tot_ref[...] = jnp.where(lane < HEAD_DIM, res[1], res[3])

    gain = pl.BlockSpec((1, LANES), lambda hp, qi: (0, 0))
    full = lambda b0: pl.BlockSpec((S, LANES), lambda hp, qi, b0=b0: (0, b0 + hp))
    tile = pl.BlockSpec((T, LANES), lambda hp, qi: (qi, hp))
    return pl.pallas_call(
        body, name=name, grid=(HEAD_PAIRS, nq),
        in_specs=[pl.BlockSpec((T, LANES), lambda hp, qi: (qi, Q_BLOCK0 + hp)), full(K_BLOCK0), full(V_BLOCK0), gain, gain],
        out_specs=[tile, tile],
        out_shape=[_sds((S, ATTN_W), BF), _sds((S, ATTN_W), F32)],
        scratch_shapes=[pltpu.VMEM((S, LANES), BF)],
        compiler_params=_params("arbitrary", "arbitrary"),
    )(proj, proj, proj, gq, gk)


def attn_bwd(name, dy, proj, tot, gq, gk):
    S = proj.shape[0]
    T = _attn_tile(S)
    nq = S // T

    def body(q_ref, k_ref, v_ref, tot_ref, do_ref, gq_ref, gk_ref,
             dq_ref, dk_ref, dv_ref, dgq_ref, dgk_ref, kn_s, dkn_s, dv_s):
        hp = pl.program_id(0)
        qi = pl.program_id(1)
        hm = _head_mean_matrix(LANES)

        @pl.when(qi == 0)
        def _():
            kn_s[...] = _qk_norm(k_ref[...].astype(F32), gk_ref[...], hm)[1].astype(BF)
            dkn_s[...] = jnp.zeros_like(dkn_s)
            dv_s[...] = jnp.zeros_like(dv_s)

        q = q_ref[...].astype(F32)
        rq, qn = _qk_norm(q, gq_ref[...], hm)
        lane = lax.broadcasted_iota(jnp.int32, (T, LANES), 1)
        lo, hi = lane < HEAD_DIM, lane >= HEAD_DIM
        qh = [jnp.where(lo, qn, 0.0).astype(BF), jnp.where(hi, qn, 0.0).astype(BF)]
        do = do_ref[...].astype(F32)
        doh = [jnp.where(lo, do, 0.0).astype(BF), jnp.where(hi, do, 0.0).astype(BF)]
        total = [tot_ref[:, 0:1], tot_ref[:, HEAD_DIM:HEAD_DIM + 1]]
        rowi = lax.broadcasted_iota(jnp.int32, (T, T), 0)
        coli = lax.broadcasted_iota(jnp.int32, (T, T), 1)
        u_upto = (rowi <= coli).astype(BF)
        u_before = (rowi < coli).astype(BF)

        def step(j, carry):
            off = pl.multiple_of(j * T, T)
            kb = kn_s[pl.ds(off, T), :]
            vb = v_ref[pl.ds(off, T), :].astype(BF)
            causal = (coli + j * T) < (rowi + qi * T)
            out = []
            dkn = jnp.zeros((T, LANES), F32)
            dvb = jnp.zeros((T, LANES), F32)
            for h in range(2):
                dq, run, grun = carry[3 * h], carry[3 * h + 1], carry[3 * h + 2]
                lb, lr = _logits(qh[h], kb, causal)
                after = total[h] - run - _split_dot(lr, u_upto)
                a = jnp.where(causal, jnp.exp(lb + after), 0.0)
                g = lax.dot_general(doh[h], vb, NT_DIMS, preferred_element_type=F32) * a
                dvb = dvb + lax.dot_general(a.astype(BF), doh[h], TN_DIMS, preferred_element_type=F32)
                c = grun + _split_dot(g, u_before)
                sig = jnp.exp(lb)
                dz = jnp.where(causal, g * (1.0 - sig) - c * sig, 0.0).astype(BF)
                dkn = dkn + lax.dot_general(dz, qh[h], TN_DIMS, preferred_element_type=F32)
                out += [dq + jnp.dot(dz, kb, preferred_element_type=F32),
                        run + jnp.sum(lr, axis=-1, keepdims=True),
                        grun + jnp.sum(g, axis=-1, keepdims=True)]
            dkn_s[pl.ds(off, T), :] += dkn
            dv_s[pl.ds(off, T), :] += dvb
            return tuple(out)

        zero = (jnp.zeros((T, LANES), F32), jnp.zeros((T, 1), F32), jnp.zeros((T, 1), F32))
        res = lax.fori_loop(0, qi + 1, step, zero + zero)
        dqn = jnp.where(lo, res[0], res[3])
        gq_v = gq_ref[...]
        t = dqn * gq_v
        dq_ref[...] = (rq * t - q * (rq * rq * rq) * _head_mean(t * q, hm)).astype(BF)
        dgq = jnp.sum(dqn * (q * rq), axis=0, keepdims=True) * SCALE
        first = jnp.logical_and(hp == 0, qi == 0)

        @pl.when(first)
        def _():
            dgq_ref[...] = dgq

        @pl.when(jnp.logical_not(first))
        def _():
            dgq_ref[...] += dgq

        @pl.when(qi == nq - 1)
        def _():
            k = k_ref[...].astype(F32)
            rk = _qk_norm(k, gk_ref[...], hm)[0]
            dkn = dkn_s[...]
            tk = dkn * gk_ref[...]
            dk_ref[...] = (rk * tk - k * (rk * rk * rk) * _head_mean(tk * k, hm)).astype(BF)
            dgk = jnp.sum(dkn * (k * rk), axis=0, keepdims=True)
            dv_ref[...] = dv_s[...].astype(BF)

            @pl.when(hp == 0)
            def _():
                dgk_ref[...] = dgk

            @pl.when(hp > 0)
            def _():
                dgk_ref[...] += dgk

            @pl.when(hp == HEAD_PAIRS - 1)
            def _():
                fold = (lax.broadcasted_iota(jnp.int32, (LANES, LANES), 0) % HEAD_DIM ==
                        lax.broadcasted_iota(jnp.int32, (LANES, LANES), 1) % HEAD_DIM).astype(F32)
                dgq_ref[...] = jnp.dot(dgq_ref[...], fold, precision=HIGHEST, preferred_element_type=F32)
                dgk_ref[...] = jnp.dot(dgk_ref[...], fold, precision=HIGHEST, preferred_element_type=F32)

    gain = pl.BlockSpec((1, LANES), lambda hp, qi: (0, 0))
    full = lambda b0: pl.BlockSpec((S, LANES), lambda hp, qi, b0=b0: (0, b0 + hp))
    tile = pl.BlockSpec((T, LANES), lambda hp, qi: (qi, hp))
    col = pl.BlockSpec((S, LANES), lambda hp, qi: (0, hp))
    dgain = pl.BlockSpec((1, LANES), lambda hp, qi: (0, 0))
    return pl.pallas_call(
        body, name=name, grid=(HEAD_PAIRS, nq),
        in_specs=[pl.BlockSpec((T, LANES), lambda hp, qi: (qi, Q_BLOCK0 + hp)), full(K_BLOCK0), full(V_BLOCK0),
                  tile, pl.BlockSpec((T, LANES), lambda hp, qi: (qi, 2 + hp)), gain, gain],
        out_specs=[tile, col, col, dgain, dgain],
        out_shape=[_sds((S, ATTN_W), BF)] * 3 + [_sds((1, LANES), F32)] * 2,
        scratch_shapes=[pltpu.VMEM((S, LANES), BF), pltpu.VMEM((S, LANES), F32), pltpu.VMEM((S, LANES), F32)],
        compiler_params=_params("arbitrary", "arbitrary"),
    )(proj, proj, proj, tot, dy, gq, gk)


def ple_bwd_elem(name, dh, gp, pp):
    S, D = dh.shape
    tm = min(S, 512)

    def body(dh_ref, gp_ref, pp_ref, dgp_ref, dpp_ref):
        d = dh_ref[...]
        gate = jax.nn.sigmoid(gp_ref[...].astype(F32))
        dpp_ref[...] = (d * gate).astype(BF)
        dgp_ref[...] = (d * pp_ref[...].astype(F32) * gate * (1.0 - gate)).astype(BF)

    tile = pl.BlockSpec((tm, D), lambda i: (i, 0))
    return pl.pallas_call(
        body, name=name, grid=(S // tm,), in_specs=[tile] * 3, out_specs=[tile] * 2,
        out_shape=[_sds((S, D), BF)] * 2, compiler_params=_params("parallel"),
    )(dh, gp, pp)


def loss_head(name, h, target):
    S, D = h.shape
    tm = min(S, 512)

    def body(h_ref, t_ref, loss_ref, dh_ref):
        i = pl.program_id(0)
        e = h_ref[...] - t_ref[...]
        dh_ref[...] = e * (1.0 / D)
        part = jnp.zeros((8, LANES), F32) + 0.5 * jnp.sum(jnp.mean(e * e, axis=-1, keepdims=True))

        @pl.when(i == 0)
        def _():
            loss_ref[...] = part

        @pl.when(i > 0)
        def _():
            loss_ref[...] += part

    tile = pl.BlockSpec((tm, D), lambda i: (i, 0))
    return pl.pallas_call(
        body, name=name, grid=(S // tm,), in_specs=[tile, tile],
        out_specs=[pl.BlockSpec((8, LANES), lambda i: (0, 0)), tile],
        out_shape=[_sds((8, LANES), F32), _sds((S, D), F32)], compiler_params=_params("arbitrary"),
    )(h, target)


def adamw(name, w, g, m, v):
    R, C = w.shape
    tr = R
    for cand in (512, 256, 128, 64, 32, 16, 8):
        if R % cand == 0:
            tr = cand
            break
    c1 = 1.0 - ADAM_B1 ** ADAM_STEP
    c2 = 1.0 - ADAM_B2 ** ADAM_STEP

    def body(w_ref, g_ref, m_ref, v_ref, d_ref, nm_ref, nv_ref):
        gg = g_ref[...]
        nm = ADAM_B1 * m_ref[...] + (1.0 - ADAM_B1) * gg
        nv = ADAM_B2 * v_ref[...] + (1.0 - ADAM_B2) * (gg * gg)
        nm_ref[...] = nm
        nv_ref[...] = nv
        d_ref[...] = -ADAM_LR * ((nm / c1) / (jnp.sqrt(nv / c2) + ADAM_EPS) + ADAM_WD * w_ref[...])

    tile = pl.BlockSpec((tr, C), lambda i: (i, 0))
    return pl.pallas_call(
        body, name=name, grid=(R // tr,), in_specs=[tile] * 4, out_specs=[tile] * 3,
        out_shape=[_sds((R, C), F32)] * 3, compiler_params=_params("parallel"),
    )(w, g, m, v)


def _relu2(u):
    r = jnp.maximum(u.astype(F32), 0.0)
    return r * r


def layer_fwd(tag, h0, p_bf, wt):
    hn1 = rmsnorm(f"{tag}_norm1", h0, wt["norm1_g"])
    proj = mm_nn(f"{tag}_proj", hn1, wt["w_in"], tn=1408)
    ya = conv_fwd(f"{tag}_conv", proj, wt["conv_w"])
    yb, yb_tot = attn_fwd(f"{tag}_attn", proj, wt["gq"], wt["gk"])
    yc = sgu_fwd(f"{tag}_sgu", proj, wt["gv"], wt["sgu_w"], wt["sgu_bias"])
    y = jnp.concatenate([ya, yb, yc], axis=-1)
    h1 = mm_nn(f"{tag}_out", y, wt["w_out"], extras=(h0,), epi=lambda acc, h: (h + acc,), out_dtypes=(F32,))
    hn2 = rmsnorm(f"{tag}_norm2", h1, wt["norm2_g"])
    uu = mm_nn(f"{tag}_ff1", hn2, wt["w_ff1"])
    h2 = mm_nn(f"{tag}_ff2", uu, wt["w_ff2"], pro=_relu2, extras=(h1,), epi=lambda acc, h: (h + acc,),
               out_dtypes=(F32,))
    hn3 = rmsnorm(f"{tag}_norm3", h2, wt["norm3_g"])
    gp = mm_nn(f"{tag}_gate", hn3, wt["w_ple_gate"])
    h3, pp = mm_nn(f"{tag}_ple", p_bf, wt["w_ple_proj"], extras=(gp, h2),
                   epi=lambda acc, g, h: (h + jax.nn.sigmoid(g.astype(F32)) * acc, acc), out_dtypes=(F32, BF))
    saved = dict(h0=h0, h1=h1, h2=h2, hn1=hn1, hn2=hn2, hn3=hn3, proj=proj, yb_tot=yb_tot, y=y, uu=uu, gp=gp, pp=pp,
                 p_bf=p_bf)
    return h3, saved


def layer_bwd(tag, dh3, sv, wt):
    dgp, dpp = ple_bwd_elem(f"{tag}_dple", dh3, sv["gp"], sv["pp"])
    g = {}
    g["w_ple_proj"] = mm_tn(f"{tag}_dwp", sv["p_bf"], dpp, col_blocks=N_CHIPS)
    g["w_ple_gate"] = mm_tn(f"{tag}_dwg", sv["hn3"], dgp)
    dh2, g["norm3_g"] = mm_nt_rmsbwd(f"{tag}_dnorm3", dgp, wt["w_ple_gate"], sv["h2"], wt["norm3_g"], dh3)

    duu = mm_nt(f"{tag}_dff2", dh2, wt["w_ff2"], extras=(sv["uu"],),
                epi=lambda acc, u: acc * (2.0 * jnp.maximum(u.astype(F32), 0.0)))
    g["w_ff2"] = mm_tn(f"{tag}_dw2", sv["uu"], dh2, pro_x=_relu2)
    g["w_ff1"] = mm_tn(f"{tag}_dw1", sv["hn2"], duu, col_blocks=N_CHIPS)
    dh1, g["norm2_g"] = mm_nt_rmsbwd(f"{tag}_dnorm2", duu, wt["w_ff1"], sv["h1"], wt["norm2_g"], dh2)

    dy = mm_nt(f"{tag}_dout", dh1, wt["w_out"])
    g["w_out"] = mm_tn(f"{tag}_dwo", sv["y"], dh1)
    dab, dac, dah, g["conv_w"] = conv_bwd(f"{tag}_dconv", dy, sv["proj"], wt["conv_w"])
    dq, dk, dv, g["gq"], g["gk"] = attn_bwd(f"{tag}_dattn", dy, sv["proj"], sv["yb_tot"], wt["gq"], wt["gk"])
    dcu, dcv, g["sgu_w"], g["sgu_bias"], g["gv"] = sgu_bwd(f"{tag}_dsgu", dy, sv["proj"], wt["gv"], wt["sgu_w"],
                                                           wt["sgu_bias"])
    dproj = jnp.concatenate([dab, dac, dah, dq, dk, dv, dcu, dcv], axis=-1)
    g["w_in"] = mm_tn(f"{tag}_dwi", sv["hn1"], dproj, tn=1408)
    dh0, g["norm1_g"] = mm_nt_rmsbwd(f"{tag}_dnorm1", dproj, wt["w_in"], sv["h0"], wt["norm1_g"], dh1)
    return dh0, g


def prep_small(norm1_g, q_norm_g, k_norm_g, sgu_norm_g, sgu_w, sgu_b, norm2_g, norm3_g, conv_w_full):
    return dict(
        norm1_g=norm1_g, norm2_g=norm2_g, norm3_g=norm3_g, conv_w=conv_w_full,
        gq=(jnp.tile(q_norm_g, 2) * SCALE).reshape(1, LANES), gk=jnp.tile(k_norm_g, 2).reshape(1, LANES),
        gv=sgu_norm_g.reshape(1, SGU_W), sgu_w=sgu_w, sgu_bias=jnp.repeat(sgu_b.T, HEAD_DIM, axis=1))


def small_grads(g):
    return dict(
        norm1_g=g["norm1_g"][0], norm2_g=g["norm2_g"][0], norm3_g=g["norm3_g"][0], conv_w=g["conv_w"],
        q_norm_g=g["gq"][0, :HEAD_DIM], k_norm_g=g["gk"][0, :HEAD_DIM], sgu_norm_g=g["gv"][0], sgu_w=g["sgu_w"],
        sgu_b=g["sgu_bias"][:, :SGU_HEADS].T)


HBM_SPEC = pl.BlockSpec(memory_space=pltpu.HBM)
BIG = ("w_in", "w_out", "w_ff1", "w_ff2", "w_ple_gate", "w_ple_proj")


def _mesh_pos():
    return lax.axis_index("x"), lax.axis_index("y"), lax.axis_index("c")


def _other_chips(x, y):
    return [(1 - x, y), (x, 1 - y), (1 - x, 1 - y)]


def _half(rows, core):
    h = rows // 2
    return pl.ds(pl.multiple_of(core * h, 16), h)


def _remote(src, dst, send_sems, recv_sems, k, to):
    return pltpu.make_async_remote_copy(src_ref=src, dst_ref=dst, send_sem=send_sems.at[k], recv_sem=recv_sems.at[k],
                                        device_id=to, device_id_type=MESH)


def allgather_weights(name, shards):
    n = len(shards)

    def body(*refs):
        ins, outs = refs[:n], refs[n:2 * n]
        send_sems, recv_sems, loc_sems = refs[2 * n:]
        x, y, c = _mesh_pos()
        me = 2 * x + y
        chips = _other_chips(x, y)
        local, sends = [], []
        for i in range(n):
            rows = shards[i].shape[0]
            cp = pltpu.make_async_copy(ins[i], outs[i].at[me], loc_sems.at[i])
            cp.start()
            local.append(cp)
            for k, chip in enumerate(chips):
                cp = _remote(ins[i].at[_half(rows, c)], outs[i].at[me, _half(rows, c)], send_sems, recv_sems,
                             6 * i + k, (*chip, c))
                cp.start()
                sends.append(cp)
        for i in range(n):
            rows = shards[i].shape[0]
            for k, chip in enumerate(chips):
                region = outs[i].at[2 * chip[0] + chip[1], _half(rows, c)]
                _remote(region, region, send_sems, recv_sems, 6 * i + k, (*chip, c)).wait_recv()
                cp = _remote(region, region, send_sems, recv_sems, 6 * i + 3 + k, (x, y, 1 - c))
                cp.start()
                sends.append(cp)
        for i in range(n):
            rows = shards[i].shape[0]
            for k, chip in enumerate(chips):
                region = outs[i].at[2 * chip[0] + chip[1], _half(rows, 1 - c)]
                _remote(region, region, send_sems, recv_sems, 6 * i + 3 + k, (x, y, 1 - c)).wait_recv()
        for cp in sends:
            cp.wait_send()
        for cp in local:
            cp.wait()

    return pl.pallas_call(
        body, name=name, in_specs=[HBM_SPEC] * n, out_specs=[HBM_SPEC] * n,
        out_shape=[_sds((N_CHIPS,) + s.shape, s.dtype) for s in shards],
        scratch_shapes=[pltpu.SemaphoreType.DMA((6 * n,)), pltpu.SemaphoreType.DMA((6 * n,)),
                        pltpu.SemaphoreType.DMA((n,))],
    )(*shards)


def exchange_other_half(name, grads):
    n = len(grads)

    def body(*refs):
        ins, outs = refs[:n], refs[n:2 * n]
        send_sems, recv_sems = refs[2 * n:]
        x, y, c = _mesh_pos()
        copies = []
        for i in range(n):
            rows = grads[i].shape[1]
            cp = _remote(ins[i].at[:, _half(rows, 1 - c)], outs[i], send_sems, recv_sems, i, (x, y, 1 - c))
            cp.start()
            copies.append(cp)
        for cp in copies:
            cp.wait()

    return pl.pallas_call(
        body, name=name, in_specs=[HBM_SPEC] * n, out_specs=[HBM_SPEC] * n,
        out_shape=[_sds((N_CHIPS, g.shape[1] // 2, g.shape[2]), g.dtype) for g in grads],
        scratch_shapes=[pltpu.SemaphoreType.DMA((n,)), pltpu.SemaphoreType.DMA((n,))],
    )(*grads)


def add_own_half(name, core, grads, got):
    n = len(grads)

    def body(core_ref, *refs):
        for i in range(n):
            refs[2 * n + i][...] = (refs[i][...].astype(F32) + refs[n + i][...].astype(F32)).astype(BF)

    def spec(g, own):
        blk = (None, g.shape[1] // 2, g.shape[2])
        return pl.BlockSpec(blk, (lambda j, core_ref: (j, core_ref[0], 0)) if own else (lambda j, core_ref: (j, 0, 0)))

    return pl.pallas_call(
        body, name=name,
        grid_spec=pltpu.PrefetchScalarGridSpec(
            num_scalar_prefetch=1, grid=(N_CHIPS,),
            in_specs=[spec(g, True) for g in grads] + [spec(g, False) for g in grads],
            out_specs=[spec(g, False) for g in grads]),
        out_shape=[_sds(r.shape, BF) for r in got], compiler_params=_params("parallel"),
    )(core, *grads, *got)


def exchange_chip_sums(name, parts):
    n = len(parts)

    def body(*refs):
        ins, outs = refs[:n], refs[n:2 * n]
        send_sems, recv_sems, loc_sems = refs[2 * n:]
        x, y, c = _mesh_pos()
        me = 2 * x + y
        chips = _other_chips(x, y)
        local, sends = [], []
        for i in range(n):
            cp = pltpu.make_async_copy(ins[i].at[me], outs[i].at[me], loc_sems.at[i])
            cp.start()
            local.append(cp)
            for k, chip in enumerate(chips):
                cp = _remote(ins[i].at[2 * chip[0] + chip[1]], outs[i].at[me], send_sems, recv_sems, 3 * i + k, (*chip, c))
                cp.start()
                sends.append(cp)
        for i in range(n):
            for k, chip in enumerate(chips):
                slot = outs[i].at[2 * chip[0] + chip[1]]
                _remote(slot, slot, send_sems, recv_sems, 3 * i + k, (*chip, c)).wait_recv()
        for cp in sends:
            cp.wait_send()
        for cp in local:
            cp.wait()

    return pl.pallas_call(
        body, name=name, in_specs=[HBM_SPEC] * n, out_specs=[HBM_SPEC] * n,
        out_shape=[_sds(p.shape, p.dtype) for p in parts],
        scratch_shapes=[pltpu.SemaphoreType.DMA((3 * n,)), pltpu.SemaphoreType.DMA((3 * n,)),
                        pltpu.SemaphoreType.DMA((n,))],
    )(*parts)


def sum_chips(name, got):
    n = len(got)

    def body(*refs):
        for i in range(n):
            acc = refs[i][0].astype(F32)
            for j in range(1, N_CHIPS):
                acc = acc + refs[i][j].astype(F32)
            refs[n + i][...] = acc

    parts = 2
    return pl.pallas_call(
        body, name=name, grid=(parts,),
        in_specs=[pl.BlockSpec((N_CHIPS, g.shape[1] // parts, g.shape[2]), lambda t: (0, t, 0)) for g in got],
        out_specs=[pl.BlockSpec((g.shape[1] // parts, g.shape[2]), lambda t: (t, 0)) for g in got],
        out_shape=[_sds(g.shape[1:], F32) for g in got], compiler_params=_params("parallel"),
    )(*got)


def exchange_final_halves(name, halves):
    n = len(halves)

    def body(*refs):
        ins, outs = refs[:n], refs[n:2 * n]
        send_sems, recv_sems, loc_sems = refs[2 * n:]
        x, y, c = _mesh_pos()
        copies, local = [], []
        for i in range(n):
            rows = 2 * halves[i].shape[0]
            cp = pltpu.make_async_copy(ins[i], outs[i].at[_half(rows, c)], loc_sems.at[i])
            cp.start()
            local.append(cp)
            cp = _remote(ins[i], outs[i].at[_half(rows, c)], send_sems, recv_sems, i, (x, y, 1 - c))
            cp.start()
            copies.append(cp)
        for i, cp in enumerate(copies):
            cp.wait_send()
            rows = 2 * halves[i].shape[0]
            region = outs[i].at[_half(rows, 1 - c)]
            _remote(region, region, send_sems, recv_sems, i, (x, y, 1 - c)).wait_recv()
        for cp in local:
            cp.wait()

    return pl.pallas_call(
        body, name=name, in_specs=[HBM_SPEC] * n, out_specs=[HBM_SPEC] * n,
        out_shape=[_sds((2 * h.shape[0], h.shape[1]), h.dtype) for h in halves],
        scratch_shapes=[pltpu.SemaphoreType.DMA((n,)), pltpu.SemaphoreType.DMA((n,)), pltpu.SemaphoreType.DMA((n,))],
    )(*halves)


def reduce_scatter_layer(tag, core, grads):
    got = exchange_other_half(f"{tag}_rs_pair", grads)
    parts = add_own_half(f"{tag}_rs_add", core, grads, got)
    sums = exchange_chip_sums(f"{tag}_rs_chips", parts)
    halves = sum_chips(f"{tag}_rs_sum", sums)
    return exchange_final_halves(f"{tag}_rs_join", halves)


def small_allreduce(name, x):
    R = x.shape[0]
    H = R // 2

    def body(x_ref, o_ref, pair_ref, chip_ref, send_sems, recv_sems):
        xx, yy, c = _mesh_pos()
        me = 2 * xx + yy
        chips = _other_chips(xx, yy)
        sibling = (xx, yy, 1 - c)
        mine = pl.ds(pl.multiple_of(c * H, 8), H)
        theirs = pl.ds(pl.multiple_of((1 - c) * H, 8), H)
        a = _remote(x_ref.at[theirs], pair_ref.at[theirs], send_sems, recv_sems, 0, sibling)
        a.start()
        a.wait_send()
        _remote(x_ref.at[mine], pair_ref.at[mine], send_sems, recv_sems, 0, sibling).wait_recv()
        chip_ref[me] = x_ref[mine, :] + pair_ref[mine, :]
        sends = []
        for k, chip in enumerate(chips):
            cp = _remote(chip_ref.at[me], chip_ref.at[me], send_sems, recv_sems, 1 + k, (*chip, c))
            cp.start()
            sends.append(cp)
        for k, chip in enumerate(chips):
            slot = chip_ref.at[2 * chip[0] + chip[1]]
            _remote(slot, slot, send_sems, recv_sems, 1 + k, (*chip, c)).wait_recv()
        o_ref[mine, :] = (chip_ref[0] + chip_ref[1]) + (chip_ref[2] + chip_ref[3])
        b = _remote(o_ref.at[mine], o_ref.at[mine], send_sems, recv_sems, 4, sibling)
        b.start()
        b.wait_send()
        _remote(o_ref.at[theirs], o_ref.at[theirs], send_sems, recv_sems, 4, sibling).wait_recv()
        for cp in sends:
            cp.wait_send()

    return pl.pallas_call(
        body, name=name,
        in_specs=[pl.BlockSpec(memory_space=pltpu.VMEM)], out_specs=pl.BlockSpec(memory_space=pltpu.VMEM),
        out_shape=_sds((R, LANES), F32),
        scratch_shapes=[pltpu.VMEM((R, LANES), F32), pltpu.VMEM((N_CHIPS, H, LANES), F32),
                        pltpu.SemaphoreType.DMA((5,)), pltpu.SemaphoreType.DMA((5,))],
        compiler_params=pltpu.CompilerParams(vmem_limit_bytes=VMEM_LIMIT),
    )(x)


WEIGHTS = ("norm1_g", "w_in", "conv_w", "q_norm_g", "k_norm_g", "sgu_norm_g", "sgu_w", "sgu_b", "w_out", "norm2_g",
           "w_ff1", "w_ff2", "norm3_g", "w_ple_gate", "w_ple_proj")
SMALL = ("norm1_g", "norm2_g", "norm3_g", "q_norm_g", "k_norm_g", "sgu_norm_g", "sgu_w", "sgu_b", "conv_w")


def _pack_rows(arrays):
    flat = []
    for a in arrays:
        v = a.reshape(-1)
        flat.append(jnp.pad(v, (0, (-v.shape[0]) % LANES)))
    v = jnp.concatenate(flat)
    v = jnp.pad(v, (0, (-v.shape[0]) % (16 * LANES)))
    return v.reshape(-1, LANES)


def _unpack_rows(packed, shapes):
    out, pos = [], 0
    flat = packed.reshape(-1)
    for shp in shapes:
        size = math.prod(shp)
        out.append(flat[pos:pos + size].reshape(shp))
        pos += size + (-size) % LANES
    return out


def kernel(x, p, norm1_g, w_in, conv_w, q_norm_g, k_norm_g, sgu_norm_g, sgu_w, sgu_b, w_out, norm2_g, w_ff1, w_ff2, norm3_g, w_ple_gate, w_ple_proj, loss_target, m_norm1_g, m_w_in, m_conv_w, m_q_norm_g, m_k_norm_g, m_sgu_norm_g, m_sgu_w, m_sgu_b, m_w_out, m_norm2_g, m_w_ff1, m_w_ff2, m_norm3_g, m_w_ple_gate, m_w_ple_proj, v_norm1_g, v_w_in, v_conv_w, v_q_norm_g, v_k_norm_g, v_sgu_norm_g, v_sgu_w, v_sgu_b, v_w_out, v_norm2_g, v_w_ff1, v_w_ff2, v_norm3_g, v_w_ple_gate, v_w_ple_proj):
    w = dict(norm1_g=norm1_g, w_in=w_in, conv_w=conv_w, q_norm_g=q_norm_g, k_norm_g=k_norm_g, sgu_norm_g=sgu_norm_g,
             sgu_w=sgu_w, sgu_b=sgu_b, w_out=w_out, norm2_g=norm2_g, w_ff1=w_ff1, w_ff2=w_ff2, norm3_g=norm3_g,
             w_ple_gate=w_ple_gate, w_ple_proj=w_ple_proj)
    m = dict(norm1_g=m_norm1_g, w_in=m_w_in, conv_w=m_conv_w, q_norm_g=m_q_norm_g, k_norm_g=m_k_norm_g,
             sgu_norm_g=m_sgu_norm_g, sgu_w=m_sgu_w, sgu_b=m_sgu_b, w_out=m_w_out, norm2_g=m_norm2_g, w_ff1=m_w_ff1,
             w_ff2=m_w_ff2, norm3_g=m_norm3_g, w_ple_gate=m_w_ple_gate, w_ple_proj=m_w_ple_proj)
    v = dict(norm1_g=v_norm1_g, w_in=v_w_in, conv_w=v_conv_w, q_norm_g=v_q_norm_g, k_norm_g=v_k_norm_g,
             sgu_norm_g=v_sgu_norm_g, sgu_w=v_sgu_w, sgu_b=v_sgu_b, w_out=v_w_out, norm2_g=v_norm2_g, w_ff1=v_w_ff1,
             w_ff2=v_w_ff2, norm3_g=v_norm3_g, w_ple_gate=v_w_ple_gate, w_ple_proj=v_w_ple_proj)
    depth = w_in.shape[0]
    d_model = x.shape[-1]
    chip = 2 * lax.axis_index("x") + lax.axis_index("y")
    core = lax.axis_index("c")
    core_arr = core.reshape(1).astype(jnp.int32)

    cw_cols = conv_w.shape[-1]
    placed = lax.dynamic_update_slice(jnp.zeros((depth, 3, CONV_W), F32), conv_w, (0, 0, chip * cw_cols))
    placed = jnp.where(core == 0, placed, 0.0)
    conv_full = _unpack_rows(small_allreduce("conv_w_gather", _pack_rows([placed])), [(depth, 3, CONV_W)])[0]

    h = x[0]
    p_bf = p[:, 0].astype(BF)
    saved, full = [], []
    for l in range(depth):
        g_in, g_out, g_ff1, g_ff2, g_gate, g_proj = allgather_weights(f"l{l}_gather", [w[n][l].astype(BF) for n in BIG])
        wt = prep_small(norm1_g[l], q_norm_g[l], k_norm_g[l], sgu_norm_g[l], sgu_w[l], sgu_b[l], norm2_g[l], norm3_g[l],
                        conv_full[l])
        wt["w_in"] = jnp.transpose(g_in, (1, 0, 2)).reshape(d_model, -1)
        wt["w_out"] = g_out.reshape(-1, d_model)
        wt["w_ff1"] = g_ff1
        wt["w_ff2"] = g_ff2.reshape(-1, d_model)
        wt["w_ple_gate"] = g_gate.reshape(-1, d_model)
        wt["w_ple_proj"] = g_proj
        h, sv = layer_fwd(f"l{l}", h, p_bf[l], wt)
        saved.append(sv)
        full.append(wt)

    loss_tile, dh = loss_head("loss", h, loss_target[0])
    loss = lax.psum(loss_tile[0, 0], ("x", "y", "c"))

    reduced = [None] * depth
    small = [None] * depth
    for l in reversed(range(depth)):
        dh, g = layer_bwd(f"l{l}", dh, saved[l], full[l])
        small[l] = small_grads(g)
        shards_in = w_in.shape[-1]
        gl = [jnp.transpose(g["w_in"].reshape(d_model, N_CHIPS, shards_in), (1, 0, 2)),
              g["w_out"].reshape(N_CHIPS, -1, d_model), g["w_ff1"], g["w_ff2"].reshape(N_CHIPS, -1, d_model),
              g["w_ple_gate"].reshape(N_CHIPS, -1, d_model), g["w_ple_proj"]]
        reduced[l] = reduce_scatter_layer(f"l{l}", core_arr, gl)

    grads = {n: jnp.stack([reduced[l][i] for l in range(depth)]) for i, n in enumerate(BIG)}
    packed = _pack_rows([small[l][n] for l in range(depth) for n in SMALL])
    shapes = [small[l][n].shape for l in range(depth) for n in SMALL]
    pieces = _unpack_rows(small_allreduce("small_grads", packed), shapes)
    for i, n in enumerate(SMALL):
        grads[n] = jnp.stack([pieces[l * len(SMALL) + i] for l in range(depth)])
    grads["conv_w"] = lax.dynamic_slice(grads["conv_w"], (0, 0, chip * cw_cols), (depth, 3, cw_cols))

    delta, new_m, new_v = {}, {}, {}
    for n in WEIGHTS:
        shp = w[n].shape
        two_d = (-1, shp[-1]) if n != "sgu_w" else (-1, LANES)
        d, nm, nv = adamw(f"adamw_{n}", w[n].reshape(two_d), grads[n].reshape(two_d), m[n].reshape(two_d),
                          v[n].reshape(two_d))
        delta[n], new_m[n], new_v[n] = d.reshape(shp), nm.reshape(shp), nv.reshape(shp)

    return (loss, dh[None], *[grads[n] for n in WEIGHTS], *[delta[n] for n in WEIGHTS], *[new_m[n] for n in WEIGHTS],
            *[new_v[n] for n in WEIGHTS])
```

```python
import functools
import math

import jax
import jax.numpy as jnp
from jax import lax
from jax.experimental import pallas as pl
from jax.experimental.pallas import tpu as pltpu

F32 = jnp.float32
BF = jnp.bfloat16
MESH = pl.DeviceIdType.MESH
HIGHEST = lax.Precision.HIGHEST

EPS = 1e-6
HEAD_DIM = 64
CONV_W = 256
ATTN_W = 512
SGU_W = 256
CHUNK = 128
N_CHIPS = 4
SCALE = HEAD_DIM ** -0.5
LANES = 128
VMEM_LIMIT = 56 * 1024 * 1024

ADAM_LR = 0.001
ADAM_B1 = 0.9
ADAM_B2 = 0.999
ADAM_EPS = 1e-08
ADAM_WD = 0.01
ADAM_STEP = 10

NT_DIMS = (((1,), (1,)), ((), ()))
TN_DIMS = (((0,), (0,)), ((), ()))


def _params(*sem):
    return pltpu.CompilerParams(dimension_semantics=sem if sem else None, vmem_limit_bytes=VMEM_LIMIT)


def _sds(shape, dtype):
    return jax.ShapeDtypeStruct(shape, dtype)


def _erf(x):
    return lax.erf(x)


def _gelu(x):
    return 0.5 * x * (1.0 + _erf(x * (2.0 ** -0.5)))


def _gelu_grad(x):
    return 0.5 * (1.0 + _erf(x * (2.0 ** -0.5))) + x * jnp.exp(-0.5 * x * x) * (1.0 / math.sqrt(2.0 * math.pi))


def _log_sigmoid(z):
    return jnp.minimum(z, 0.0) - jnp.log1p(jnp.exp(-jnp.abs(z)))


def _head_mean_matrix(width):
    r = lax.broadcasted_iota(jnp.int32, (width, width), 0) // HEAD_DIM
    c = lax.broadcasted_iota(jnp.int32, (width, width), 1) // HEAD_DIM
    return jnp.where(r == c, 1.0 / HEAD_DIM, 0.0).astype(F32)


def _head_mean(x, m):
    return jnp.dot(x, m, precision=HIGHEST, preferred_element_type=F32)


def _split_dot(a, u):
    hi = a.astype(BF)
    lo = (a - hi.astype(F32)).astype(BF)
    return jnp.dot(hi, u, preferred_element_type=F32) + jnp.dot(lo, u, preferred_element_type=F32)


def rmsnorm(name, h, g, after=()):
    S, D = h.shape
    tm = min(S, 512)

    def body(h_ref, g_ref, *rest):
        x = h_ref[...]
        r = lax.rsqrt(jnp.mean(x * x, axis=-1, keepdims=True) + EPS)
        rest[-1][...] = ((x * r) * g_ref[...]).astype(BF)

    return pl.pallas_call(
        body, name=name, grid=(S // tm,),
        in_specs=[pl.BlockSpec((tm, D), lambda i: (i, 0)), pl.BlockSpec((1, D), lambda i: (0, 0))] +
                 [pl.BlockSpec(memory_space=pl.ANY)] * len(after),
        out_specs=pl.BlockSpec((tm, D), lambda i: (i, 0)),
        out_shape=_sds((S, D), BF), compiler_params=_params("parallel"),
    )(h, g.reshape(1, D), *after)


def mm_nn(name, x, w, *, extras=(), pro=None, epi=None, out_dtypes=None, tm=512, tn=512):
    S, K = x.shape
    if w.ndim == 3:
        J, _, tn = w.shape
        N = J * tn
        w_spec = pl.BlockSpec((None, K, tn), lambda n, m: (n, 0, 0))
    else:
        N = w.shape[1]
        tn = min(tn, N)
        w_spec = pl.BlockSpec((K, tn), lambda n, m: (0, n))
    tm = min(tm, S)
    out_dtypes = (BF,) if out_dtypes is None else out_dtypes
    n_ex, n_out = len(extras), len(out_dtypes)

    def body(x_ref, w_ref, *rest):
        xv = x_ref[...]
        if pro is not None:
            xv = pro(xv)
        acc = jnp.dot(xv.astype(BF), w_ref[...], preferred_element_type=F32)
        outs = (acc,) if epi is None else epi(acc, *[e[...] for e in rest[:n_ex]])
        for o_ref, o in zip(rest[n_ex:], outs):
            o_ref[...] = o.astype(o_ref.dtype)

    tile = pl.BlockSpec((tm, tn), lambda n, m: (m, n))
    out = pl.pallas_call(
        body, name=name, grid=(N // tn, S // tm),
        in_specs=[pl.BlockSpec((tm, K), lambda n, m: (m, 0)), w_spec] + [tile] * n_ex,
        out_specs=[tile] * n_out,
        out_shape=[_sds((S, N), d) for d in out_dtypes],
        compiler_params=_params("parallel", "parallel"),
    )(x, w, *extras)
    return out[0] if n_out == 1 else out


def mm_nt(name, dy, w, *, extras=(), epi=None, tm=256):
    S, N = dy.shape
    K = w.shape[0]
    tm = min(tm, S)
    n_ex = len(extras)

    def body(dy_ref, w_ref, *rest):
        acc = lax.dot_general(dy_ref[...].astype(BF), w_ref[...], NT_DIMS, preferred_element_type=F32)
        if epi is not None:
            acc = epi(acc, *[e[...] for e in rest[:n_ex]])
        rest[n_ex][...] = acc.astype(BF)

    row = pl.BlockSpec((tm, K), lambda i: (i, 0))
    return pl.pallas_call(
        body, name=name, grid=(S // tm,),
        in_specs=[pl.BlockSpec((tm, N), lambda i: (i, 0)), pl.BlockSpec((K, N), lambda i: (0, 0))] + [row] * n_ex,
        out_specs=row, out_shape=_sds((S, K), BF), compiler_params=_params("parallel"),
    )(dy, w, *extras)


def mm_nt_rmsbwd(name, dy, w, h, g, dres, *, tm=256):
    S, N = dy.shape
    D = h.shape[1]
    tm = min(tm, S)
    blocked = w.ndim == 3
    nj = w.shape[2] if blocked else N

    def body(dy_ref, w_ref, h_ref, g_ref, dres_ref, dh_ref, dg_ref):
        i = pl.program_id(0)
        if blocked:
            dyn = None
            for j in range(w.shape[0]):
                part = lax.dot_general(dy_ref[:, j * nj:(j + 1) * nj].astype(BF), w_ref[j], NT_DIMS,
                                       preferred_element_type=F32)
                dyn = part if dyn is None else dyn + part
        else:
            dyn = lax.dot_general(dy_ref[...].astype(BF), w_ref[...], NT_DIMS, preferred_element_type=F32)
        x = h_ref[...]
        r = lax.rsqrt(jnp.mean(x * x, axis=-1, keepdims=True) + EPS)
        t = dyn * g_ref[...]
        dh_ref[...] = dres_ref[...] + r * t - x * (r * r * r) * jnp.mean(t * x, axis=-1, keepdims=True)
        part = jnp.sum(dyn * (x * r), axis=0, keepdims=True)

        @pl.when(i == 0)
        def _():
            dg_ref[...] = part

        @pl.when(i > 0)
        def _():
            dg_ref[...] += part

    w_spec = pl.BlockSpec(w.shape, (lambda i: (0, 0, 0)) if blocked else (lambda i: (0, 0)))
    row = pl.BlockSpec((tm, D), lambda i: (i, 0))
    vec = pl.BlockSpec((1, D), lambda i: (0, 0))
    return pl.pallas_call(
        body, name=name, grid=(S // tm,),
        in_specs=[pl.BlockSpec((tm, N), lambda i: (i, 0)), w_spec, row, vec, row],
        out_specs=[row, vec], out_shape=[_sds((S, D), F32), _sds((1, D), F32)],
        compiler_params=_params("arbitrary"),
    )(dy, w, h, g.reshape(1, D), dres)


def mm_tn(name, x, dy, *, pro_x=None, col_blocks=None, tk=1024, tn=1024):
    S, K = x.shape
    N = dy.shape[1]
    tk = min(tk, K)
    if col_blocks is not None:
        tn = N // col_blocks
        out_shape = _sds((col_blocks, K, tn), BF)
        out_spec = pl.BlockSpec((None, tk, tn), lambda k, n: (n, k, 0))
    else:
        tn = min(tn, N)
        out_shape = _sds((K, N), BF)
        out_spec = pl.BlockSpec((tk, tn), lambda k, n: (k, n))

    def body(x_ref, dy_ref, o_ref):
        xv = x_ref[...]
        if pro_x is not None:
            xv = pro_x(xv)
        o_ref[...] = lax.dot_general(xv.astype(BF), dy_ref[...].astype(BF), TN_DIMS,
                                     preferred_element_type=F32).astype(BF)

    return pl.pallas_call(
        body, name=name, grid=(K // tk, N // tn),
        in_specs=[pl.BlockSpec((S, tk), lambda k, n: (0, k)), pl.BlockSpec((S, tn), lambda k, n: (0, n))],
        out_specs=out_spec, out_shape=out_shape, compiler_params=_params("parallel", "parallel"),
    )(x, dy)


def _conv_parts(ac_ref, ah_ref, cw):
    a_c = ac_ref[...].astype(F32)
    a_h = ah_ref[...].astype(F32)
    x = a_c * a_h
    row = lax.broadcasted_iota(jnp.int32, x.shape, 0)
    x1 = jnp.where(row >= 1, pltpu.roll(x, 1, 0), 0.0)
    x2 = jnp.where(row >= 2, pltpu.roll(x, 2, 0), 0.0)
    cv = cw[0:1] * x2 + cw[1:2] * x1 + cw[2:3] * x
    return a_c, a_h, x, x1, x2, cv, row


def conv_fwd(name, proj, cw):
    S = proj.shape[0]

    def body(ab_ref, ac_ref, ah_ref, cw_ref, o_ref):
        cv = _conv_parts(ac_ref, ah_ref, cw_ref[...])[5]
        o_ref[...] = (ab_ref[...].astype(F32) * cv).astype(BF)

    col = lambda j: pl.BlockSpec((S, CONV_W), lambda i, j=j: (0, j))
    return pl.pallas_call(
        body, name=name, grid=(1,),
        in_specs=[col(0), col(1), col(2), pl.BlockSpec((3, CONV_W), lambda i: (0, 0))],
        out_specs=pl.BlockSpec((S, CONV_W), lambda i: (0, 0)),
        out_shape=_sds((S, CONV_W), BF), compiler_params=_params("arbitrary"),
    )(proj, proj, proj, cw)


def conv_bwd(name, dy, proj, cw):
    S = proj.shape[0]

    def body(dy_ref, ab_ref, ac_ref, ah_ref, cw_ref, dab_ref, dac_ref, dah_ref, dcw_ref):
        w = cw_ref[...]
        a_c, a_h, x, x1, x2, cv, row = _conv_parts(ac_ref, ah_ref, w)
        d = dy_ref[...].astype(F32)
        dab_ref[...] = (d * cv).astype(BF)
        dcv = d * ab_ref[...].astype(F32)
        d1 = jnp.where(row < S - 1, pltpu.roll(dcv, S - 1, 0), 0.0)
        d2 = jnp.where(row < S - 2, pltpu.roll(dcv, S - 2, 0), 0.0)
        dx = w[2:3] * dcv + w[1:2] * d1 + w[0:1] * d2
        dac_ref[...] = (dx * a_h).astype(BF)
        dah_ref[...] = (dx * a_c).astype(BF)
        dcw_ref[0:1, :] = jnp.sum(dcv * x2, axis=0, keepdims=True)
        dcw_ref[1:2, :] = jnp.sum(dcv * x1, axis=0, keepdims=True)
        dcw_ref[2:3, :] = jnp.sum(dcv * x, axis=0, keepdims=True)

    col = lambda j: pl.BlockSpec((S, CONV_W), lambda i, j=j: (0, j))
    one = pl.BlockSpec((S, CONV_W), lambda i: (0, 0))
    small = pl.BlockSpec((3, CONV_W), lambda i: (0, 0))
    return pl.pallas_call(
        body, name=name, grid=(1,),
        in_specs=[col(0), col(0), col(1), col(2), small],
        out_specs=[one, one, one, small],
        out_shape=[_sds((S, CONV_W), BF)] * 3 + [_sds((3, CONV_W), F32)],
        compiler_params=_params("arbitrary"),
    )(dy, proj, proj, proj, cw)


SGU_HEADS = SGU_W // HEAD_DIM
CU_BLOCK = 2304 // SGU_W
CV_BLOCK = 2560 // SGU_W


def _sgu_common(cu_ref, cv_ref, gv_ref, tm):
    c_u = cu_ref[...].astype(F32)
    c_v = cv_ref[...].astype(F32)
    hm = _head_mean_matrix(SGU_W)
    u = _gelu(c_u)
    vg = _gelu(c_v)
    r = lax.rsqrt(_head_mean(vg * vg, hm) + EPS)
    vv = (vg * r) * gv_ref[...]
    head = lax.broadcasted_iota(jnp.int32, (CHUNK, SGU_W), 1) // HEAD_DIM
    tri = (lax.broadcasted_iota(jnp.int32, (CHUNK, CHUNK), 0) >=
           lax.broadcasted_iota(jnp.int32, (CHUNK, CHUNK), 1))
    return c_u, c_v, hm, u, vg, r, vv, head, tri


def _sgu_mix(w_ref, tri, head, vvc, bias):
    sv = bias
    for g in range(SGU_HEADS):
        wg = jnp.where(tri, w_ref[g], 0.0).astype(BF)
        sv = sv + jnp.where(head == g, jnp.dot(wg, vvc, preferred_element_type=F32), 0.0)
    return sv


def sgu_fwd(name, proj, gv, w, bias):
    S = proj.shape[0]
    tm = min(S, 512)

    def body(cu_ref, cv_ref, gv_ref, w_ref, b_ref, o_ref):
        _, _, _, u, _, _, vv, head, tri = _sgu_common(cu_ref, cv_ref, gv_ref, tm)
        vvb = vv.astype(BF)
        for ch in range(tm // CHUNK):
            rows = slice(ch * CHUNK, (ch + 1) * CHUNK)
            sv = _sgu_mix(w_ref, tri, head, vvb[rows], b_ref[...])
            o_ref[rows, :] = (u[rows] * sv).astype(BF)

    const = lambda shape: pl.BlockSpec(shape, lambda i: (0,) * len(shape))
    return pl.pallas_call(
        body, name=name, grid=(S // tm,),
        in_specs=[pl.BlockSpec((tm, SGU_W), lambda i: (i, CU_BLOCK)), pl.BlockSpec((tm, SGU_W), lambda i: (i, CV_BLOCK)),
                  const((1, SGU_W)), const((SGU_HEADS, CHUNK, CHUNK)), const((CHUNK, SGU_W))],
        out_specs=pl.BlockSpec((tm, SGU_W), lambda i: (i, 0)),
        out_shape=_sds((S, SGU_W), BF), compiler_params=_params("parallel"),
    )(proj, proj, gv, w, bias)


def sgu_bwd(name, dy, proj, gv, w, bias):
    S = proj.shape[0]
    tm = min(S, 512)

    def body(dy_ref, cu_ref, cv_ref, gv_ref, w_ref, b_ref, dcu_ref, dcv_ref, dw_ref, db_ref, dgv_ref, dvv_s):
        i = pl.program_id(0)
        c_u, c_v, hm, u, vg, r, vv, head, tri = _sgu_common(cu_ref, cv_ref, gv_ref, tm)
        vvb = vv.astype(BF)
        d = dy_ref[...].astype(F32)
        ind = (lax.broadcasted_iota(jnp.int32, (SGU_W, LANES), 0) // HEAD_DIM ==
               lax.broadcasted_iota(jnp.int32, (SGU_W, LANES), 1)).astype(F32)
        dw_acc = [jnp.zeros((CHUNK, CHUNK), F32) for _ in range(SGU_HEADS)]
        db_acc = jnp.zeros((CHUNK, LANES), F32)
        for ch in range(tm // CHUNK):
            rows = slice(ch * CHUNK, (ch + 1) * CHUNK)
            sv = _sgu_mix(w_ref, tri, head, vvb[rows], b_ref[...])
            dcu_ref[rows, :] = (d[rows] * sv * _gelu_grad(c_u[rows])).astype(BF)
            dsv = d[rows] * u[rows]
            db_acc = db_acc + jnp.dot(dsv, ind, precision=HIGHEST, preferred_element_type=F32)
            dvv = jnp.zeros((CHUNK, SGU_W), F32)
            for g in range(SGU_HEADS):
                dsv_g = jnp.where(head == g, dsv, 0.0).astype(BF)
                wg = jnp.where(tri, w_ref[g], 0.0).astype(BF)
                dvv = dvv + lax.dot_general(wg, dsv_g, TN_DIMS, preferred_element_type=F32)
                dw_acc[g] = dw_acc[g] + lax.dot_general(dsv_g, vvb[rows], NT_DIMS, preferred_element_type=F32)
            dvv_s[rows, :] = dvv
        dvv = dvv_s[...]
        gvv = gv_ref[...]
        t = dvv * gvv
        dvg = r * t - vg * (r * r * r) * _head_mean(t * vg, hm)
        dcv_ref[...] = (dvg * _gelu_grad(c_v)).astype(BF)
        dgv = jnp.sum(dvv * (vg * r), axis=0, keepdims=True)

        @pl.when(i == 0)
        def _():
            for g in range(SGU_HEADS):
                dw_ref[g] = jnp.where(tri, dw_acc[g], 0.0)
            db_ref[...] = db_acc
            dgv_ref[...] = dgv

        @pl.when(i > 0)
        def _():
            for g in range(SGU_HEADS):
                dw_ref[g] += jnp.where(tri, dw_acc[g], 0.0)
            db_ref[...] += db_acc
            dgv_ref[...] += dgv

    const = lambda shape: pl.BlockSpec(shape, lambda i: (0,) * len(shape))
    tile = pl.BlockSpec((tm, SGU_W), lambda i: (i, 0))
    return pl.pallas_call(
        body, name=name, grid=(S // tm,),
        in_specs=[pl.BlockSpec((tm, SGU_W), lambda i: (i, 3)),
                  pl.BlockSpec((tm, SGU_W), lambda i: (i, CU_BLOCK)), pl.BlockSpec((tm, SGU_W), lambda i: (i, CV_BLOCK)),
                  const((1, SGU_W)), const((SGU_HEADS, CHUNK, CHUNK)), const((CHUNK, SGU_W))],
        out_specs=[tile, tile, const((SGU_HEADS, CHUNK, CHUNK)), const((CHUNK, LANES)), const((1, SGU_W))],
        out_shape=[_sds((S, SGU_W), BF), _sds((S, SGU_W), BF), _sds((SGU_HEADS, CHUNK, CHUNK), F32),
                   _sds((CHUNK, LANES), F32), _sds((1, SGU_W), F32)],
        scratch_shapes=[pltpu.VMEM((tm, SGU_W), F32)],
        compiler_params=_params("arbitrary"),
    )(dy, proj, proj, gv, w, bias)


HEAD_PAIRS = ATTN_W // LANES
Q_BLOCK0 = 768 // LANES
K_BLOCK0 = 1280 // LANES
V_BLOCK0 = 1792 // LANES


def _attn_tile(S):
    return min(S, 256)


def _qk_norm(x, g, hm):
    r = lax.rsqrt(_head_mean(x * x, hm) + EPS)
    return r, (x * r) * g


def _logits(qh, kb, causal):
    z = lax.dot_general(qh, kb, NT_DIMS, preferred_element_type=F32)
    lb = _log_sigmoid(z)
    return lb, jnp.where(causal, lb - z, 0.0)


def attn_fwd(name, proj, gq, gk):
    S = proj.shape[0]
    T = _attn_tile(S)
    nq = S // T

    def body(q_ref, k_ref, v_ref, gq_ref, gk_ref, o_ref, tot_ref, kn_s):
        qi = pl.program_id(1)
        hm = _head_mean_matrix(LANES)

        @pl.when(qi == 0)
        def _():
            kn_s[...] = _qk_norm(k_ref[...].astype(F32), gk_ref[...], hm)[1].astype(BF)

        qn = _qk_norm(q_ref[...].astype(F32), gq_ref[...], hm)[1]
        lane = lax.broadcasted_iota(jnp.int32, (T, LANES), 1)
        qh = [jnp.where(lane < HEAD_DIM, qn, 0.0).astype(BF), jnp.where(lane >= HEAD_DIM, qn, 0.0).astype(BF)]
        rowi = lax.broadcasted_iota(jnp.int32, (T, T), 0)
        coli = lax.broadcasted_iota(jnp.int32, (T, T), 1)
        u_excl = (rowi > coli).astype(BF)

        def step(it, carry):
            j = qi - it
            off = pl.multiple_of(j * T, T)
            kb = kn_s[pl.ds(off, T), :]
            vb = v_ref[pl.ds(off, T), :].astype(BF)
            causal = (coli + j * T) < (rowi + qi * T)
            out = []
            for h in range(2):
                o, run = carry[2 * h], carry[2 * h + 1]
                lb, lr = _logits(qh[h], kb, causal)
                a = jnp.where(causal, jnp.exp(lb + _split_dot(lr, u_excl) + run), 0.0)
                out += [o + jnp.dot(a.astype(BF), vb, preferred_element_type=F32),
                        run + jnp.sum(lr, axis=-1, keepdims=True)]
            return tuple(out)

        zero = (jnp.zeros((T, LANES), F32), jnp.zeros((T, 1), F32))
        res = lax.fori_loop(0, qi + 1, step, zero + zero)
        o_ref[...] = jnp.where(lane < HEAD_DIM, res[0], res[2]).astype(BF)
        tot_ref[...] = jnp.where(lane < HEAD_DIM, res[1], res[3])

    gain = pl.BlockSpec((1, LANES), lambda hp, qi: (0, 0))
    full = lambda b0: pl.BlockSpec((S, LANES), lambda hp, qi, b0=b0: (0, b0 + hp))
    tile = pl.BlockSpec((T, LANES), lambda hp, qi: (qi, hp))
    return pl.pallas_call(
        body, name=name, grid=(HEAD_PAIRS, nq),
        in_specs=[pl.BlockSpec((T, LANES), lambda hp, qi: (qi, Q_BLOCK0 + hp)), full(K_BLOCK0), full(V_BLOCK0), gain, gain],
        out_specs=[tile, tile],
        out_shape=[_sds((S, ATTN_W), BF), _sds((S, ATTN_W), F32)],
        scratch_shapes=[pltpu.VMEM((S, LANES), BF)],
        compiler_params=_params("arbitrary", "arbitrary"),
    )(proj, proj, proj, gq, gk)


def attn_bwd(name, dy, proj, tot, gq, gk):
    S = proj.shape[0]
    T = _attn_tile(S)
    nq = S // T

    def body(q_ref, k_ref, v_ref, tot_ref, do_ref, gq_ref, gk_ref,
             dq_ref, dk_ref, dv_ref, dgq_ref, dgk_ref, kn_s, dkn_s, dv_s):
        hp = pl.program_id(0)
        qi = pl.program_id(1)
        hm = _head_mean_matrix(LANES)

        @pl.when(qi == 0)
        def _():
            kn_s[...] = _qk_norm(k_ref[...].astype(F32), gk_ref[...], hm)[1].astype(BF)
            dkn_s[...] = jnp.zeros_like(dkn_s)
            dv_s[...] = jnp.zeros_like(dv_s)

        q = q_ref[...].astype(F32)
        rq, qn = _qk_norm(q, gq_ref[...], hm)
        lane = lax.broadcasted_iota(jnp.int32, (T, LANES), 1)
        lo, hi = lane < HEAD_DIM, lane >= HEAD_DIM
        qh = [jnp.where(lo, qn, 0.0).astype(BF), jnp.where(hi, qn, 0.0).astype(BF)]
        do = do_ref[...].astype(F32)
        doh = [jnp.where(lo, do, 0.0).astype(BF), jnp.where(hi, do, 0.0).astype(BF)]
        total = [tot_ref[:, 0:1], tot_ref[:, HEAD_DIM:HEAD_DIM + 1]]
        rowi = lax.broadcasted_iota(jnp.int32, (T, T), 0)
        coli = lax.broadcasted_iota(jnp.int32, (T, T), 1)
        u_upto = (rowi <= coli).astype(BF)
        u_before = (rowi < coli).astype(BF)

        def step(j, carry):
            off = pl.multiple_of(j * T, T)
            kb = kn_s[pl.ds(off, T), :]
            vb = v_ref[pl.ds(off, T), :].astype(BF)
            causal = (coli + j * T) < (rowi + qi * T)
            out = []
            dkn = jnp.zeros((T, LANES), F32)
            dvb = jnp.zeros((T, LANES), F32)
            for h in range(2):
                dq, run, grun = carry[3 * h], carry[3 * h + 1], carry[3 * h + 2]
                lb, lr = _logits(qh[h], kb, causal)
                after = total[h] - run - _split_dot(lr, u_upto)
                a = jnp.where(causal, jnp.exp(lb + after), 0.0)
                g = lax.dot_general(doh[h], vb, NT_DIMS, preferred_element_type=F32) * a
                dvb = dvb + lax.dot_general(a.astype(BF), doh[h], TN_DIMS, preferred_element_type=F32)
                c = grun + _split_dot(g, u_before)
                sig = jnp.exp(lb)
                dz = jnp.where(causal, g * (1.0 - sig) - c * sig, 0.0).astype(BF)
                dkn = dkn + lax.dot_general(dz, qh[h], TN_DIMS, preferred_element_type=F32)
                out += [dq + jnp.dot(dz, kb, preferred_element_type=F32),
                        run + jnp.sum(lr, axis=-1, keepdims=True),
                        grun + jnp.sum(g, axis=-1, keepdims=True)]
            dkn_s[pl.ds(off, T), :] += dkn
            dv_s[pl.ds(off, T), :] += dvb
            return tuple(out)

        zero = (jnp.zeros((T, LANES), F32), jnp.zeros((T, 1), F32), jnp.zeros((T, 1), F32))
        res = lax.fori_loop(0, qi + 1, step, zero + zero)
        dqn = jnp.where(lo, res[0], res[3])
        gq_v = gq_ref[...]
        t = dqn * gq_v
        dq_ref[...] = (rq * t - q * (rq * rq * rq) * _head_mean(t * q, hm)).astype(BF)
        dgq = jnp.sum(dqn * (q * rq), axis=0, keepdims=True) * SCALE
        first = jnp.logical_and(hp == 0, qi == 0)

        @pl.when(first)
        def _():
            dgq_ref[...] = dgq

        @pl.when(jnp.logical_not(first))
        def _():
            dgq_ref[...] += dgq

        @pl.when(qi == nq - 1)
        def _():
            k = k_ref[...].astype(F32)
            rk = _qk_norm(k, gk_ref[...], hm)[0]
            dkn = dkn_s[...]
            tk = dkn * gk_ref[...]
            dk_ref[...] = (rk * tk - k * (rk * rk * rk) * _head_mean(tk * k, hm)).astype(BF)
            dgk = jnp.sum(dkn * (k * rk), axis=0, keepdims=True)
            dv_ref[...] = dv_s[...].astype(BF)

            @pl.when(hp == 0)
            def _():
                dgk_ref[...] = dgk

            @pl.when(hp > 0)
            def _():
                dgk_ref[...] += dgk

            @pl.when(hp == HEAD_PAIRS - 1)
            def _():
                fold = (lax.broadcasted_iota(jnp.int32, (LANES, LANES), 0) % HEAD_DIM ==
                        lax.broadcasted_iota(jnp.int32, (LANES, LANES), 1) % HEAD_DIM).astype(F32)
                dgq_ref[...] = jnp.dot(dgq_ref[...], fold, precision=HIGHEST, preferred_element_type=F32)
                dgk_ref[...] = jnp.dot(dgk_ref[...], fold, precision=HIGHEST, preferred_element_type=F32)

    gain = pl.BlockSpec((1, LANES), lambda hp, qi: (0, 0))
    full = lambda b0: pl.BlockSpec((S, LANES), lambda hp, qi, b0=b0: (0, b0 + hp))
    tile = pl.BlockSpec((T, LANES), lambda hp, qi: (qi, hp))
    col = pl.BlockSpec((S, LANES), lambda hp, qi: (0, hp))
    dgain = pl.BlockSpec((1, LANES), lambda hp, qi: (0, 0))
    return pl.pallas_call(
        body, name=name, grid=(HEAD_PAIRS, nq),
        in_specs=[pl.BlockSpec((T, LANES), lambda hp, qi: (qi, Q_BLOCK0 + hp)), full(K_BLOCK0), full(V_BLOCK0),
                  tile, pl.BlockSpec((T, LANES), lambda hp, qi: (qi, 2 + hp)), gain, gain],
        out_specs=[tile, col, col, dgain, dgain],
        out_shape=[_sds((S, ATTN_W), BF)] * 3 + [_sds((1, LANES), F32)] * 2,
        scratch_shapes=[pltpu.VMEM((S, LANES), BF), pltpu.VMEM((S, LANES), F32), pltpu.VMEM((S, LANES), F32)],
        compiler_params=_params("arbitrary", "arbitrary"),
    )(proj, proj, proj, tot, dy, gq, gk)


def ple_bwd_elem(name, dh, gp, pp, after=()):
    S, D = dh.shape
    tm = min(S, 512)

    def body(dh_ref, gp_ref, pp_ref, *rest):
        dgp_ref, dpp_ref = rest[-2:]
        d = dh_ref[...]
        gate = jax.nn.sigmoid(gp_ref[...].astype(F32))
        dpp_ref[...] = (d * gate).astype(BF)
        dgp_ref[...] = (d * pp_ref[...].astype(F32) * gate * (1.0 - gate)).astype(BF)

    tile = pl.BlockSpec((tm, D), lambda i: (i, 0))
    return pl.pallas_call(
        body, name=name, grid=(S // tm,), in_specs=[tile] * 3 + [pl.BlockSpec(memory_space=pl.ANY)] * len(after),
        out_specs=[tile] * 2, out_shape=[_sds((S, D), BF)] * 2, compiler_params=_params("parallel"),
    )(dh, gp, pp, *after)


def loss_head(name, h, target):
    S, D = h.shape
    tm = min(S, 512)

    def body(h_ref, t_ref, loss_ref, dh_ref):
        i = pl.program_id(0)
        e = h_ref[...] - t_ref[...]
        dh_ref[...] = e * (1.0 / D)
        part = jnp.zeros((8, LANES), F32) + 0.5 * jnp.sum(jnp.mean(e * e, axis=-1, keepdims=True))

        @pl.when(i == 0)
        def _():
            loss_ref[...] = part

        @pl.when(i > 0)
        def _():
            loss_ref[...] += part

    tile = pl.BlockSpec((tm, D), lambda i: (i, 0))
    return pl.pallas_call(
        body, name=name, grid=(S // tm,), in_specs=[tile, tile],
        out_specs=[pl.BlockSpec((8, LANES), lambda i: (0, 0)), tile],
        out_shape=[_sds((8, LANES), F32), _sds((S, D), F32)], compiler_params=_params("arbitrary"),
    )(h, target)


def adamw(name, w, g, m, v):
    R, C = w.shape
    tr = R
    for cand in (512, 256, 128, 64, 32, 16, 8):
        if R % cand == 0:
            tr = cand
            break
    c1 = 1.0 - ADAM_B1 ** ADAM_STEP
    c2 = 1.0 - ADAM_B2 ** ADAM_STEP

    def body(w_ref, g_ref, m_ref, v_ref, d_ref, nm_ref, nv_ref):
        gg = g_ref[...]
        nm = ADAM_B1 * m_ref[...] + (1.0 - ADAM_B1) * gg
        nv = ADAM_B2 * v_ref[...] + (1.0 - ADAM_B2) * (gg * gg)
        nm_ref[...] = nm
        nv_ref[...] = nv
        d_ref[...] = -ADAM_LR * ((nm / c1) / (jnp.sqrt(nv / c2) + ADAM_EPS) + ADAM_WD * w_ref[...])

    tile = pl.BlockSpec((tr, C), lambda i: (i, 0))
    return pl.pallas_call(
        body, name=name, grid=(R // tr,), in_specs=[tile] * 4, out_specs=[tile] * 3,
        out_shape=[_sds((R, C), F32)] * 3, compiler_params=_params("parallel"),
    )(w, g, m, v)


def _relu2(u):
    r = jnp.maximum(u.astype(F32), 0.0)
    return r * r


def layer_fwd(tag, h0, p_bf, wt, after=()):
    hn1 = rmsnorm(f"{tag}_norm1", h0, wt["norm1_g"], after)
    proj = mm_nn(f"{tag}_proj", hn1, wt["w_in"], tn=1408)
    ya = conv_fwd(f"{tag}_conv", proj, wt["conv_w"])
    yb, yb_tot = attn_fwd(f"{tag}_attn", proj, wt["gq"], wt["gk"])
    yc = sgu_fwd(f"{tag}_sgu", proj, wt["gv"], wt["sgu_w"], wt["sgu_bias"])
    y = jnp.concatenate([ya, yb, yc], axis=-1)
    h1 = mm_nn(f"{tag}_out", y, wt["w_out"], extras=(h0,), epi=lambda acc, h: (h + acc,), out_dtypes=(F32,))
    hn2 = rmsnorm(f"{tag}_norm2", h1, wt["norm2_g"])
    uu = mm_nn(f"{tag}_ff1", hn2, wt["w_ff1"])
    h2 = mm_nn(f"{tag}_ff2", uu, wt["w_ff2"], pro=_relu2, extras=(h1,), epi=lambda acc, h: (h + acc,),
               out_dtypes=(F32,))
    hn3 = rmsnorm(f"{tag}_norm3", h2, wt["norm3_g"])
    gp = mm_nn(f"{tag}_gate", hn3, wt["w_ple_gate"])
    h3, pp = mm_nn(f"{tag}_ple", p_bf, wt["w_ple_proj"], extras=(gp, h2),
                   epi=lambda acc, g, h: (h + jax.nn.sigmoid(g.astype(F32)) * acc, acc), out_dtypes=(F32, BF))
    saved = dict(h0=h0, h1=h1, h2=h2, hn1=hn1, hn2=hn2, hn3=hn3, proj=proj, yb_tot=yb_tot, y=y, uu=uu, gp=gp, pp=pp,
                 p_bf=p_bf)
    return h3, saved


def layer_bwd(tag, dh3, sv, wt, after=()):
    dgp, dpp = ple_bwd_elem(f"{tag}_dple", dh3, sv["gp"], sv["pp"], after)
    g = {}
    g["w_ple_proj"] = mm_tn(f"{tag}_dwp", sv["p_bf"], dpp, col_blocks=N_CHIPS)
    g["w_ple_gate"] = mm_tn(f"{tag}_dwg", sv["hn3"], dgp)
    dh2, g["norm3_g"] = mm_nt_rmsbwd(f"{tag}_dnorm3", dgp, wt["w_ple_gate"], sv["h2"], wt["norm3_g"], dh3)

    duu = mm_nt(f"{tag}_dff2", dh2, wt["w_ff2"], extras=(sv["uu"],),
                epi=lambda acc, u: acc * (2.0 * jnp.maximum(u.astype(F32), 0.0)))
    g["w_ff2"] = mm_tn(f"{tag}_dw2", sv["uu"], dh2, pro_x=_relu2)
    g["w_ff1"] = mm_tn(f"{tag}_dw1", sv["hn2"], duu, col_blocks=N_CHIPS)
    dh1, g["norm2_g"] = mm_nt_rmsbwd(f"{tag}_dnorm2", duu, wt["w_ff1"], sv["h1"], wt["norm2_g"], dh2)

    dy = mm_nt(f"{tag}_dout", dh1, wt["w_out"])
    g["w_out"] = mm_tn(f"{tag}_dwo", sv["y"], dh1)
    dab, dac, dah, g["conv_w"] = conv_bwd(f"{tag}_dconv", dy, sv["proj"], wt["conv_w"])
    dq, dk, dv, g["gq"], g["gk"] = attn_bwd(f"{tag}_dattn", dy, sv["proj"], sv["yb_tot"], wt["gq"], wt["gk"])
    dcu, dcv, g["sgu_w"], g["sgu_bias"], g["gv"] = sgu_bwd(f"{tag}_dsgu", dy, sv["proj"], wt["gv"], wt["sgu_w"],
                                                           wt["sgu_bias"])
    dproj = jnp.concatenate([dab, dac, dah, dq, dk, dv, dcu, dcv], axis=-1)
    g["w_in"] = mm_tn(f"{tag}_dwi", sv["hn1"], dproj, tn=1408)
    dh0, g["norm1_g"] = mm_nt_rmsbwd(f"{tag}_dnorm1", dproj, wt["w_in"], sv["h0"], wt["norm1_g"], dh1)
    return dh0, g


def prep_small(norm1_g, q_norm_g, k_norm_g, sgu_norm_g, sgu_w, sgu_b, norm2_g, norm3_g, conv_w_full):
    return dict(
        norm1_g=norm1_g, norm2_g=norm2_g, norm3_g=norm3_g, conv_w=conv_w_full,
        gq=(jnp.tile(q_norm_g, 2) * SCALE).reshape(1, LANES), gk=jnp.tile(k_norm_g, 2).reshape(1, LANES),
        gv=sgu_norm_g.reshape(1, SGU_W), sgu_w=sgu_w, sgu_bias=jnp.repeat(sgu_b.T, HEAD_DIM, axis=1))


def small_grads(g):
    return dict(
        norm1_g=g["norm1_g"][0], norm2_g=g["norm2_g"][0], norm3_g=g["norm3_g"][0], conv_w=g["conv_w"],
        q_norm_g=g["gq"][0, :HEAD_DIM], k_norm_g=g["gk"][0, :HEAD_DIM], sgu_norm_g=g["gv"][0], sgu_w=g["sgu_w"],
        sgu_b=g["sgu_bias"][:, :SGU_HEADS].T)


HBM_SPEC = pl.BlockSpec(memory_space=pltpu.HBM)
BIG = ("w_in", "w_out", "w_ff1", "w_ff2", "w_ple_gate", "w_ple_proj")


def _mesh_pos():
    return lax.axis_index("x"), lax.axis_index("y"), lax.axis_index("c")


def _other_chips(x, y):
    return [(1 - x, y), (x, 1 - y), (1 - x, 1 - y)]


def _half(rows, core):
    h = rows // 2
    return pl.ds(pl.multiple_of(core * h, 16), h)


def _remote(src, dst, send_sems, recv_sems, k, to):
    return pltpu.make_async_remote_copy(src_ref=src, dst_ref=dst, send_sem=send_sems.at[k], recv_sem=recv_sems.at[k],
                                        device_id=to, device_id_type=MESH)


SEM_SPEC = pl.BlockSpec(memory_space=pltpu.SEMAPHORE)
ANY_SPEC = pl.BlockSpec(memory_space=pl.ANY)
SIDE_EFFECT = pltpu.SideEffectType.DATAFLOW_SIDE_EFFECTING


def _in_hbm(arrays):
    return [pltpu.with_memory_space_constraint(a, pltpu.HBM) for a in arrays]


def copies_start(name, srcs, lands, plan, after=()):
    n, na = len(srcs), len(after)

    def body(*refs):
        src_refs, land_refs = refs[:n], refs[n:2 * n]
        send_sem, recv_sem = refs[2 * n + na], refs[2 * n + na + 1]
        token = refs[-1]
        x, y, c = _mesh_pos()
        for i in range(n):
            for src, dst, dev in plan(i, src_refs[i], land_refs[i], x, y, c):
                pltpu.make_async_remote_copy(src_ref=src, dst_ref=dst, send_sem=send_sem, recv_sem=recv_sem,
                                             device_id=dev, device_id_type=MESH).start()
        token[...] = jnp.zeros_like(token)

    out = pl.pallas_call(
        body, name=name,
        in_specs=[HBM_SPEC] * (2 * n) + [ANY_SPEC] * na,
        out_specs=(SEM_SPEC, SEM_SPEC, *[HBM_SPEC] * (2 * n), pl.BlockSpec(memory_space=pltpu.VMEM)),
        out_shape=(pltpu.SemaphoreType.DMA(()), pltpu.SemaphoreType.DMA(()),
                   *[pltpu.HBM(a.shape, a.dtype) for a in (*srcs, *lands)], _sds((8, LANES), F32)),
        input_output_aliases={i: 2 + i for i in range(2 * n)},
        compiler_params=pltpu.CompilerParams(has_side_effects=SIDE_EFFECT),
    )(*_in_hbm(srcs), *_in_hbm(lands), *after)
    return out[0], out[1], list(out[2:2 + n]), list(out[2 + n:2 + 2 * n]), out[-1]


def copies_wait(name, started, plan, after=()):
    send_sem, recv_sem, srcs, lands, _ = started
    n, na = len(srcs), len(after)

    def body(*refs):
        src_refs, land_refs = refs[:n], refs[n:2 * n]
        send_sem, recv_sem = refs[2 * n], refs[2 * n + 1]
        x, y, c = _mesh_pos()
        for i in range(n):
            for src, dst, dev in plan(i, src_refs[i], land_refs[i], x, y, c):
                cp = pltpu.make_async_remote_copy(src_ref=src, dst_ref=dst, send_sem=send_sem, recv_sem=recv_sem,
                                                  device_id=dev, device_id_type=MESH)
                cp.wait_send()
                cp.wait_recv()

    out = pl.pallas_call(
        body, name=name,
        in_specs=[HBM_SPEC] * (2 * n) + [SEM_SPEC, SEM_SPEC] + [ANY_SPEC] * na,
        out_specs=[HBM_SPEC] * (2 * n),
        out_shape=[pltpu.HBM(a.shape, a.dtype) for a in (*srcs, *lands)],
        input_output_aliases={i: i for i in range(2 * n)},
        compiler_params=pltpu.CompilerParams(has_side_effects=SIDE_EFFECT),
    )(*srcs, *lands, send_sem, recv_sem, *after)
    return list(out[:n]), list(out[n:])


def _gather_plan(shapes):
    def plan(i, src, land, x, y, c):
        me = 2 * x + y
        rows = _half(shapes[i][0], c)
        return [(src.at[rows], land.at[me, rows], (*chip, c)) for chip in _other_chips(x, y)]
    return plan


def _gather_arrivals(shapes):
    def plan(i, src, land, x, y, c):
        rows = _half(shapes[i][0], c)
        return [(src.at[rows], land.at[2 * chip[0] + chip[1], rows], (*chip, c)) for chip in _other_chips(x, y)]
    return plan


def gather_finish(name, shards, lands):
    n = len(shards)

    def body(*refs):
        ins, outs = refs[:n], refs[2 * n:3 * n]
        send_sems, recv_sems, loc_sems = refs[3 * n:]
        x, y, c = _mesh_pos()
        me = 2 * x + y
        chips = _other_chips(x, y)
        local, sends = [], []
        for i in range(n):
            rows = shards[i].shape[0]
            cp = pltpu.make_async_copy(ins[i], outs[i].at[me], loc_sems.at[i])
            cp.start()
            local.append(cp)
            for k, chip in enumerate(chips):
                region = outs[i].at[2 * chip[0] + chip[1], _half(rows, c)]
                cp = _remote(region, region, send_sems, recv_sems, 3 * i + k, (x, y, 1 - c))
                cp.start()
                sends.append(cp)
        for i in range(n):
            rows = shards[i].shape[0]
            for k, chip in enumerate(chips):
                region = outs[i].at[2 * chip[0] + chip[1], _half(rows, 1 - c)]
                _remote(region, region, send_sems, recv_sems, 3 * i + k, (x, y, 1 - c)).wait_recv()
        for cp in sends:
            cp.wait_send()
        for cp in local:
            cp.wait()

    return pl.pallas_call(
        body, name=name, in_specs=[HBM_SPEC] * (2 * n), out_specs=[HBM_SPEC] * n,
        out_shape=[_sds(a.shape, a.dtype) for a in lands],
        input_output_aliases={n + i: i for i in range(n)},
        scratch_shapes=[pltpu.SemaphoreType.DMA((3 * n,)), pltpu.SemaphoreType.DMA((3 * n,)),
                        pltpu.SemaphoreType.DMA((n,))],
    )(*shards, *lands)


def exchange_other_half(name, grads):
    n = len(grads)

    def body(*refs):
        ins, outs = refs[:n], refs[n:2 * n]
        send_sems, recv_sems = refs[2 * n:]
        x, y, c = _mesh_pos()
        copies = []
        for i in range(n):
            rows = grads[i].shape[1]
            cp = _remote(ins[i].at[:, _half(rows, 1 - c)], outs[i], send_sems, recv_sems, i, (x, y, 1 - c))
            cp.start()
            copies.append(cp)
        for cp in copies:
            cp.wait()

    return pl.pallas_call(
        body, name=name, in_specs=[HBM_SPEC] * n, out_specs=[HBM_SPEC] * n,
        out_shape=[_sds((N_CHIPS, g.shape[1] // 2, g.shape[2]), g.dtype) for g in grads],
        scratch_shapes=[pltpu.SemaphoreType.DMA((n,)), pltpu.SemaphoreType.DMA((n,))],
    )(*grads)


def add_own_half(name, core, grads, got):
    n = len(grads)

    def body(core_ref, *refs):
        for i in range(n):
            refs[2 * n + i][...] = (refs[i][...].astype(F32) + refs[n + i][...].astype(F32)).astype(BF)

    def spec(g, own):
        blk = (None, g.shape[1] // 2, g.shape[2])
        return pl.BlockSpec(blk, (lambda j, core_ref: (j, core_ref[0], 0)) if own else (lambda j, core_ref: (j, 0, 0)))

    return pl.pallas_call(
        body, name=name,
        grid_spec=pltpu.PrefetchScalarGridSpec(
            num_scalar_prefetch=1, grid=(N_CHIPS,),
            in_specs=[spec(g, True) for g in grads] + [spec(g, False) for g in grads],
            out_specs=[spec(g, False) for g in grads]),
        out_shape=[_sds(r.shape, BF) for r in got], compiler_params=_params("parallel"),
    )(core, *grads, *got)


def _chips_plan(i, src, land, x, y, c):
    return [(src.at[2 * chip[0] + chip[1]], land.at[k], (*chip, c)) for k, chip in enumerate(_other_chips(x, y))]


def sum_chips(name, chip, parts, got):
    n = len(got)

    def body(chip_ref, *refs):
        for i in range(n):
            acc = refs[i][...].astype(F32)
            for k in range(N_CHIPS - 1):
                acc = acc + refs[n + i][k].astype(F32)
            refs[2 * n + i][...] = acc

    steps = 2
    return pl.pallas_call(
        body, name=name,
        grid_spec=pltpu.PrefetchScalarGridSpec(
            num_scalar_prefetch=1, grid=(steps,),
            in_specs=[pl.BlockSpec((None, g.shape[1] // steps, g.shape[2]), lambda t, chip_ref: (chip_ref[0], t, 0))
                      for g in parts] +
                     [pl.BlockSpec((N_CHIPS - 1, g.shape[1] // steps, g.shape[2]), lambda t, chip_ref: (0, t, 0))
                      for g in got],
            out_specs=[pl.BlockSpec((g.shape[1] // steps, g.shape[2]), lambda t, chip_ref: (t, 0)) for g in got]),
        out_shape=[_sds(g.shape[1:], F32) for g in got], compiler_params=_params("parallel"),
    )(chip, *parts, *got)


def exchange_final_halves(name, halves):
    n = len(halves)

    def body(*refs):
        ins, outs = refs[:n], refs[n:2 * n]
        send_sems, recv_sems, loc_sems = refs[2 * n:]
        x, y, c = _mesh_pos()
        copies, local = [], []
        for i in range(n):
            rows = 2 * halves[i].shape[0]
            cp = pltpu.make_async_copy(ins[i], outs[i].at[_half(rows, c)], loc_sems.at[i])
            cp.start()
            local.append(cp)
            cp = _remote(ins[i], outs[i].at[_half(rows, c)], send_sems, recv_sems, i, (x, y, 1 - c))
            cp.start()
            copies.append(cp)
        for i, cp in enumerate(copies):
            cp.wait_send()
            rows = 2 * halves[i].shape[0]
            region = outs[i].at[_half(rows, 1 - c)]
            _remote(region, region, send_sems, recv_sems, i, (x, y, 1 - c)).wait_recv()
        for cp in local:
            cp.wait()

    return pl.pallas_call(
        body, name=name, in_specs=[HBM_SPEC] * n, out_specs=[HBM_SPEC] * n,
        out_shape=[_sds((2 * h.shape[0], h.shape[1]), h.dtype) for h in halves],
        scratch_shapes=[pltpu.SemaphoreType.DMA((n,)), pltpu.SemaphoreType.DMA((n,)), pltpu.SemaphoreType.DMA((n,))],
    )(*halves)


def reduce_scatter_begin(tag, core, grads):
    got = exchange_other_half(f"{tag}_rs_pair", grads)
    parts = add_own_half(f"{tag}_rs_add", core, grads, got)
    lands = [lax.empty((N_CHIPS - 1,) + p.shape[1:], p.dtype) for p in parts]
    return copies_start(f"{tag}_rs_start", parts, lands, _chips_plan)


def reduce_scatter_end(tag, chip, started, after):
    parts, got = copies_wait(f"{tag}_rs_wait", started, _chips_plan, after)
    halves = sum_chips(f"{tag}_rs_sum", chip, parts, got)
    return exchange_final_halves(f"{tag}_rs_join", halves)


def small_allreduce(name, x):
    R = x.shape[0]
    H = R // 2

    def body(x_ref, o_ref, pair_ref, chip_ref, send_sems, recv_sems):
        xx, yy, c = _mesh_pos()
        me = 2 * xx + yy
        chips = _other_chips(xx, yy)
        sibling = (xx, yy, 1 - c)
        mine = pl.ds(pl.multiple_of(c * H, 8), H)
        theirs = pl.ds(pl.multiple_of((1 - c) * H, 8), H)
        a = _remote(x_ref.at[theirs], pair_ref.at[theirs], send_sems, recv_sems, 0, sibling)
        a.start()
        a.wait_send()
        _remote(x_ref.at[mine], pair_ref.at[mine], send_sems, recv_sems, 0, sibling).wait_recv()
        chip_ref[me] = x_ref[mine, :] + pair_ref[mine, :]
        sends = []
        for k, chip in enumerate(chips):
            cp = _remote(chip_ref.at[me], chip_ref.at[me], send_sems, recv_sems, 1 + k, (*chip, c))
            cp.start()
            sends.append(cp)
        for k, chip in enumerate(chips):
            slot = chip_ref.at[2 * chip[0] + chip[1]]
            _remote(slot, slot, send_sems, recv_sems, 1 + k, (*chip, c)).wait_recv()
        o_ref[mine, :] = (chip_ref[0] + chip_ref[1]) + (chip_ref[2] + chip_ref[3])
        b = _remote(o_ref.at[mine], o_ref.at[mine], send_sems, recv_sems, 4, sibling)
        b.start()
        b.wait_send()
        _remote(o_ref.at[theirs], o_ref.at[theirs], send_sems, recv_sems, 4, sibling).wait_recv()
        for cp in sends:
            cp.wait_send()

    return pl.pallas_call(
        body, name=name,
        in_specs=[pl.BlockSpec(memory_space=pltpu.VMEM)], out_specs=pl.BlockSpec(memory_space=pltpu.VMEM),
        out_shape=_sds((R, LANES), F32),
        scratch_shapes=[pltpu.VMEM((R, LANES), F32), pltpu.VMEM((N_CHIPS, H, LANES), F32),
                        pltpu.SemaphoreType.DMA((5,)), pltpu.SemaphoreType.DMA((5,))],
        compiler_params=pltpu.CompilerParams(vmem_limit_bytes=VMEM_LIMIT),
    )(x)


WEIGHTS = ("norm1_g", "w_in", "conv_w", "q_norm_g", "k_norm_g", "sgu_norm_g", "sgu_w", "sgu_b", "w_out", "norm2_g",
           "w_ff1", "w_ff2", "norm3_g", "w_ple_gate", "w_ple_proj")
SMALL = ("norm1_g", "norm2_g", "norm3_g", "q_norm_g", "k_norm_g", "sgu_norm_g", "sgu_w", "sgu_b", "conv_w")


def _pack_rows(arrays):
    flat = []
    for a in arrays:
        v = a.reshape(-1)
        flat.append(jnp.pad(v, (0, (-v.shape[0]) % LANES)))
    v = jnp.concatenate(flat)
    v = jnp.pad(v, (0, (-v.shape[0]) % (16 * LANES)))
    return v.reshape(-1, LANES)


def _unpack_rows(packed, shapes):
    out, pos = [], 0
    flat = packed.reshape(-1)
    for shp in shapes:
        size = math.prod(shp)
        out.append(flat[pos:pos + size].reshape(shp))
        pos += size + (-size) % LANES
    return out


def kernel(x, p, norm1_g, w_in, conv_w, q_norm_g, k_norm_g, sgu_norm_g, sgu_w, sgu_b, w_out, norm2_g, w_ff1, w_ff2, norm3_g, w_ple_gate, w_ple_proj, loss_target, m_norm1_g, m_w_in, m_conv_w, m_q_norm_g, m_k_norm_g, m_sgu_norm_g, m_sgu_w, m_sgu_b, m_w_out, m_norm2_g, m_w_ff1, m_w_ff2, m_norm3_g, m_w_ple_gate, m_w_ple_proj, v_norm1_g, v_w_in, v_conv_w, v_q_norm_g, v_k_norm_g, v_sgu_norm_g, v_sgu_w, v_sgu_b, v_w_out, v_norm2_g, v_w_ff1, v_w_ff2, v_norm3_g, v_w_ple_gate, v_w_ple_proj):
    w = dict(norm1_g=norm1_g, w_in=w_in, conv_w=conv_w, q_norm_g=q_norm_g, k_norm_g=k_norm_g, sgu_norm_g=sgu_norm_g,
             sgu_w=sgu_w, sgu_b=sgu_b, w_out=w_out, norm2_g=norm2_g, w_ff1=w_ff1, w_ff2=w_ff2, norm3_g=norm3_g,
             w_ple_gate=w_ple_gate, w_ple_proj=w_ple_proj)
    m = dict(norm1_g=m_norm1_g, w_in=m_w_in, conv_w=m_conv_w, q_norm_g=m_q_norm_g, k_norm_g=m_k_norm_g,
             sgu_norm_g=m_sgu_norm_g, sgu_w=m_sgu_w, sgu_b=m_sgu_b, w_out=m_w_out, norm2_g=m_norm2_g, w_ff1=m_w_ff1,
             w_ff2=m_w_ff2, norm3_g=m_norm3_g, w_ple_gate=m_w_ple_gate, w_ple_proj=m_w_ple_proj)
    v = dict(norm1_g=v_norm1_g, w_in=v_w_in, conv_w=v_conv_w, q_norm_g=v_q_norm_g, k_norm_g=v_k_norm_g,
             sgu_norm_g=v_sgu_norm_g, sgu_w=v_sgu_w, sgu_b=v_sgu_b, w_out=v_w_out, norm2_g=v_norm2_g, w_ff1=v_w_ff1,
             w_ff2=v_w_ff2, norm3_g=v_norm3_g, w_ple_gate=v_w_ple_gate, w_ple_proj=v_w_ple_proj)
    depth = w_in.shape[0]
    d_model = x.shape[-1]
    chip = 2 * lax.axis_index("x") + lax.axis_index("y")
    core = lax.axis_index("c")
    core_arr = core.reshape(1).astype(jnp.int32)

    cw_cols = conv_w.shape[-1]
    placed = lax.dynamic_update_slice(jnp.zeros((depth, 3, CONV_W), F32), conv_w, (0, 0, chip * cw_cols))
    placed = jnp.where(core == 0, placed, 0.0)
    conv_full = _unpack_rows(small_allreduce("conv_w_gather", _pack_rows([placed])), [(depth, 3, CONV_W)])[0]

    h = x[0]
    p_bf = p[:, 0].astype(BF)
    saved, full = [], []
    shard_shapes = [w[n].shape[1:] for n in BIG]

    def gather_start(l, after):
        shards = [w[n][l].astype(BF) for n in BIG]
        lands = [lax.empty((N_CHIPS,) + s.shape, BF) for s in shards]
        return copies_start(f"l{l}_gather_start", shards, lands, _gather_plan(shard_shapes), after)

    started = gather_start(0, ())
    for l in range(depth):
        shards, lands = copies_wait(f"l{l}_gather_wait", started, _gather_arrivals(shard_shapes), (h,))
        g_in, g_out, g_ff1, g_ff2, g_gate, g_proj = gather_finish(f"l{l}_gather_finish", shards, lands)
        token = ()
        if l + 1 < depth:
            started = gather_start(l + 1, (g_proj,))
            token = (started[-1],)
        wt = prep_small(norm1_g[l], q_norm_g[l], k_norm_g[l], sgu_norm_g[l], sgu_w[l], sgu_b[l], norm2_g[l], norm3_g[l],
                        conv_full[l])
        wt["w_in"] = jnp.transpose(g_in, (1, 0, 2)).reshape(d_model, -1)
        wt["w_out"] = g_out.reshape(-1, d_model)
        wt["w_ff1"] = g_ff1
        wt["w_ff2"] = g_ff2.reshape(-1, d_model)
        wt["w_ple_gate"] = g_gate.reshape(-1, d_model)
        wt["w_ple_proj"] = g_proj
        h, sv = layer_fwd(f"l{l}", h, p_bf[l], wt, token)
        saved.append(sv)
        full.append(wt)

    loss_tile, dh = loss_head("loss", h, loss_target[0])
    loss = lax.psum(loss_tile[0, 0], ("x", "y", "c"))

    reduced = [None] * depth
    small = [None] * depth
    chip_arr = chip.reshape(1).astype(jnp.int32)
    started = None
    for l in reversed(range(depth)):
        dh, g = layer_bwd(f"l{l}", dh, saved[l], full[l], () if started is None else (started[-1],))
        if started is not None:
            reduced[l + 1] = reduce_scatter_end(f"l{l + 1}", chip_arr, started, (dh,))
        small[l] = small_grads(g)
        shards_in = w_in.shape[-1]
        gl = [jnp.transpose(g["w_in"].reshape(d_model, N_CHIPS, shards_in), (1, 0, 2)),
              g["w_out"].reshape(N_CHIPS, -1, d_model), g["w_ff1"], g["w_ff2"].reshape(N_CHIPS, -1, d_model),
              g["w_ple_gate"].reshape(N_CHIPS, -1, d_model), g["w_ple_proj"]]
        started = reduce_scatter_begin(f"l{l}", core_arr, gl)
    reduced[0] = reduce_scatter_end("l0", chip_arr, started, (dh,))

    grads = {n: jnp.stack([reduced[l][i] for l in range(depth)]) for i, n in enumerate(BIG)}
    packed = _pack_rows([small[l][n] for l in range(depth) for n in SMALL])
    shapes = [small[l][n].shape for l in range(depth) for n in SMALL]
    pieces = _unpack_rows(small_allreduce("small_grads", packed), shapes)
    for i, n in enumerate(SMALL):
        grads[n] = jnp.stack([pieces[l * len(SMALL) + i] for l in range(depth)])
    grads["conv_w"] = lax.dynamic_slice(grads["conv_w"], (0, 0, chip * cw_cols), (depth, 3, cw_cols))

    delta, new_m, new_v = {}, {}, {}
    for n in WEIGHTS:
        shp = w[n].shape
        two_d = (-1, shp[-1]) if n != "sgu_w" else (-1, LANES)
        d, nm, nv = adamw(f"adamw_{n}", w[n].reshape(two_d), grads[n].reshape(two_d), m[n].reshape(two_d),
                          v[n].reshape(two_d))
        delta[n], new_m[n], new_v[n] = d.reshape(shp), nm.reshape(shp), nv.reshape(shp)

    return (loss, dh[None], *[grads[n] for n in WEIGHTS], *[delta[n] for n in WEIGHTS], *[new_m[n] for n in WEIGHTS],
            *[new_v[n] for n in WEIGHTS])
```

```python
import functools
import math

import jax
import jax.numpy as jnp
from jax import lax
from jax.experimental import pallas as pl
from jax.experimental.pallas import tpu as pltpu

F32 = jnp.float32
BF = jnp.bfloat16
MESH = pl.DeviceIdType.MESH
HIGHEST = lax.Precision.HIGHEST

EPS = 1e-6
HEAD_DIM = 64
CONV_W = 256
ATTN_W = 512
SGU_W = 256
CHUNK = 128
N_CHIPS = 4
SCALE = HEAD_DIM ** -0.5
LANES = 128
VMEM_LIMIT = 56 * 1024 * 1024

ADAM_LR = 0.001
ADAM_B1 = 0.9
ADAM_B2 = 0.999
ADAM_EPS = 1e-08
ADAM_WD = 0.01
ADAM_STEP = 10

NT_DIMS = (((1,), (1,)), ((), ()))
TN_DIMS = (((0,), (0,)), ((), ()))


def _params(*sem):
    return pltpu.CompilerParams(dimension_semantics=sem if sem else None, vmem_limit_bytes=VMEM_LIMIT)


def _sds(shape, dtype):
    return jax.ShapeDtypeStruct(shape, dtype)


def _erf(x):
    return lax.erf(x)


def _gelu(x):
    return 0.5 * x * (1.0 + _erf(x * (2.0 ** -0.5)))


def _gelu_grad(x):
    return 0.5 * (1.0 + _erf(x * (2.0 ** -0.5))) + x * jnp.exp(-0.5 * x * x) * (1.0 / math.sqrt(2.0 * math.pi))


def _log_sigmoid(z):
    return jnp.minimum(z, 0.0) - jnp.log1p(jnp.exp(-jnp.abs(z)))


def _head_mean_matrix(width):
    r = lax.broadcasted_iota(jnp.int32, (width, width), 0) // HEAD_DIM
    c = lax.broadcasted_iota(jnp.int32, (width, width), 1) // HEAD_DIM
    return jnp.where(r == c, 1.0 / HEAD_DIM, 0.0).astype(F32)


def _head_mean(x, m):
    return jnp.dot(x, m, precision=HIGHEST, preferred_element_type=F32)


def _split_dot(a, u):
    hi = a.astype(BF)
    lo = (a - hi.astype(F32)).astype(BF)
    return jnp.dot(hi, u, preferred_element_type=F32) + jnp.dot(lo, u, preferred_element_type=F32)


def rmsnorm(name, h, g, after=()):
    S, D = h.shape
    tm = min(S, 512)

    def body(h_ref, g_ref, *rest):
        x = h_ref[...]
        r = lax.rsqrt(jnp.mean(x * x, axis=-1, keepdims=True) + EPS)
        rest[-1][...] = ((x * r) * g_ref[...]).astype(BF)

    return pl.pallas_call(
        body, name=name, grid=(S // tm,),
        in_specs=[pl.BlockSpec((tm, D), lambda i: (i, 0)), pl.BlockSpec((1, D), lambda i: (0, 0))] +
                 [pl.BlockSpec(memory_space=pl.ANY)] * len(after),
        out_specs=pl.BlockSpec((tm, D), lambda i: (i, 0)),
        out_shape=_sds((S, D), BF), compiler_params=_params("parallel"),
    )(h, g.reshape(1, D), *after)


def mm_nn(name, x, w, *, extras=(), pro=None, epi=None, out_dtypes=None, tm=512, tn=512):
    S, K = x.shape
    if w.ndim == 3:
        J, _, tn = w.shape
        N = J * tn
        w_spec = pl.BlockSpec((None, K, tn), lambda n, m: (n, 0, 0))
    else:
        N = w.shape[1]
        tn = min(tn, N)
        w_spec = pl.BlockSpec((K, tn), lambda n, m: (0, n))
    tm = min(tm, S)
    out_dtypes = (BF,) if out_dtypes is None else out_dtypes
    n_ex, n_out = len(extras), len(out_dtypes)

    def body(x_ref, w_ref, *rest):
        xv = x_ref[...]
        if pro is not None:
            xv = pro(xv)
        acc = jnp.dot(xv.astype(BF), w_ref[...], preferred_element_type=F32)
        outs = (acc,) if epi is None else epi(acc, *[e[...] for e in rest[:n_ex]])
        for o_ref, o in zip(rest[n_ex:], outs):
            o_ref[...] = o.astype(o_ref.dtype)

    tile = pl.BlockSpec((tm, tn), lambda n, m: (m, n))
    out = pl.pallas_call(
        body, name=name, grid=(N // tn, S // tm),
        in_specs=[pl.BlockSpec((tm, K), lambda n, m: (m, 0)), w_spec] + [tile] * n_ex,
        out_specs=[tile] * n_out,
        out_shape=[_sds((S, N), d) for d in out_dtypes],
        compiler_params=_params("parallel", "parallel"),
    )(x, w, *extras)
    return out[0] if n_out == 1 else out


def mm_nt(name, dy, w, *, extras=(), epi=None, tm=256):
    S, N = dy.shape
    K = w.shape[0]
    tm = min(tm, S)
    n_ex = len(extras)

    def body(dy_ref, w_ref, *rest):
        acc = lax.dot_general(dy_ref[...].astype(BF), w_ref[...], NT_DIMS, preferred_element_type=F32)
        if epi is not None:
            acc = epi(acc, *[e[...] for e in rest[:n_ex]])
        rest[n_ex][...] = acc.astype(BF)

    row = pl.BlockSpec((tm, K), lambda i: (i, 0))
    return pl.pallas_call(
        body, name=name, grid=(S // tm,),
        in_specs=[pl.BlockSpec((tm, N), lambda i: (i, 0)), pl.BlockSpec((K, N), lambda i: (0, 0))] + [row] * n_ex,
        out_specs=row, out_shape=_sds((S, K), BF), compiler_params=_params("parallel"),
    )(dy, w, *extras)


def mm_nt_rmsbwd(name, dy, w, h, g, dres, *, tm=256):
    S, N = dy.shape
    D = h.shape[1]
    tm = min(tm, S)
    blocked = w.ndim == 3
    nj = w.shape[2] if blocked else N

    def body(dy_ref, w_ref, h_ref, g_ref, dres_ref, dh_ref, dg_ref):
        i = pl.program_id(0)
        if blocked:
            dyn = None
            for j in range(w.shape[0]):
                part = lax.dot_general(dy_ref[:, j * nj:(j + 1) * nj].astype(BF), w_ref[j], NT_DIMS,
                                       preferred_element_type=F32)
                dyn = part if dyn is None else dyn + part
        else:
            dyn = lax.dot_general(dy_ref[...].astype(BF), w_ref[...], NT_DIMS, preferred_element_type=F32)
        x = h_ref[...]
        r = lax.rsqrt(jnp.mean(x * x, axis=-1, keepdims=True) + EPS)
        t = dyn * g_ref[...]
        dh_ref[...] = dres_ref[...] + r * t - x * (r * r * r) * jnp.mean(t * x, axis=-1, keepdims=True)
        part = jnp.sum(dyn * (x * r), axis=0, keepdims=True)

        @pl.when(i == 0)
        def _():
            dg_ref[...] = part

        @pl.when(i > 0)
        def _():
            dg_ref[...] += part

    w_spec = pl.BlockSpec(w.shape, (lambda i: (0, 0, 0)) if blocked else (lambda i: (0, 0)))
    row = pl.BlockSpec((tm, D), lambda i: (i, 0))
    vec = pl.BlockSpec((1, D), lambda i: (0, 0))
    return pl.pallas_call(
        body, name=name, grid=(S // tm,),
        in_specs=[pl.BlockSpec((tm, N), lambda i: (i, 0)), w_spec, row, vec, row],
        out_specs=[row, vec], out_shape=[_sds((S, D), F32), _sds((1, D), F32)],
        compiler_params=_params("arbitrary"),
    )(dy, w, h, g.reshape(1, D), dres)


def mm_tn(name, x, dy, *, pro_x=None, col_blocks=None, tk=1024, tn=1024):
    S, K = x.shape
    N = dy.shape[1]
    tk = min(tk, K)
    if col_blocks is not None:
        tn = N // col_blocks
        out_shape = _sds((col_blocks, K, tn), BF)
        out_spec = pl.BlockSpec((None, tk, tn), lambda k, n: (n, k, 0))
    else:
        tn = min(tn, N)
        out_shape = _sds((K, N), BF)
        out_spec = pl.BlockSpec((tk, tn), lambda k, n: (k, n))

    def body(x_ref, dy_ref, o_ref):
        xv = x_ref[...]
        if pro_x is not None:
            xv = pro_x(xv)
        o_ref[...] = lax.dot_general(xv.astype(BF), dy_ref[...].astype(BF), TN_DIMS,
                                     preferred_element_type=F32).astype(BF)

    return pl.pallas_call(
        body, name=name, grid=(K // tk, N // tn),
        in_specs=[pl.BlockSpec((S, tk), lambda k, n: (0, k)), pl.BlockSpec((S, tn), lambda k, n: (0, n))],
        out_specs=out_spec, out_shape=out_shape, compiler_params=_params("parallel", "parallel"),
    )(x, dy)


def _conv_parts(ac_ref, ah_ref, cw):
    a_c = ac_ref[...].astype(F32)
    a_h = ah_ref[...].astype(F32)
    x = a_c * a_h
    row = lax.broadcasted_iota(jnp.int32, x.shape, 0)
    x1 = jnp.where(row >= 1, pltpu.roll(x, 1, 0), 0.0)
    x2 = jnp.where(row >= 2, pltpu.roll(x, 2, 0), 0.0)
    cv = cw[0:1] * x2 + cw[1:2] * x1 + cw[2:3] * x
    return a_c, a_h, x, x1, x2, cv, row


def conv_fwd(name, proj, cw):
    S = proj.shape[0]

    def body(ab_ref, ac_ref, ah_ref, cw_ref, o_ref):
        cv = _conv_parts(ac_ref, ah_ref, cw_ref[...])[5]
        o_ref[...] = (ab_ref[...].astype(F32) * cv).astype(BF)

    col = lambda j: pl.BlockSpec((S, CONV_W), lambda i, j=j: (0, j))
    return pl.pallas_call(
        body, name=name, grid=(1,),
        in_specs=[col(0), col(1), col(2), pl.BlockSpec((3, CONV_W), lambda i: (0, 0))],
        out_specs=pl.BlockSpec((S, CONV_W), lambda i: (0, 0)),
        out_shape=_sds((S, CONV_W), BF), compiler_params=_params("arbitrary"),
    )(proj, proj, proj, cw)


def conv_bwd(name, dy, proj, cw):
    S = proj.shape[0]

    def body(dy_ref, ab_ref, ac_ref, ah_ref, cw_ref, dab_ref, dac_ref, dah_ref, dcw_ref):
        w = cw_ref[...]
        a_c, a_h, x, x1, x2, cv, row = _conv_parts(ac_ref, ah_ref, w)
        d = dy_ref[...].astype(F32)
        dab_ref[...] = (d * cv).astype(BF)
        dcv = d * ab_ref[...].astype(F32)
        d1 = jnp.where(row < S - 1, pltpu.roll(dcv, S - 1, 0), 0.0)
        d2 = jnp.where(row < S - 2, pltpu.roll(dcv, S - 2, 0), 0.0)
        dx = w[2:3] * dcv + w[1:2] * d1 + w[0:1] * d2
        dac_ref[...] = (dx * a_h).astype(BF)
        dah_ref[...] = (dx * a_c).astype(BF)
        dcw_ref[0:1, :] = jnp.sum(dcv * x2, axis=0, keepdims=True)
        dcw_ref[1:2, :] = jnp.sum(dcv * x1, axis=0, keepdims=True)
        dcw_ref[2:3, :] = jnp.sum(dcv * x, axis=0, keepdims=True)

    col = lambda j: pl.BlockSpec((S, CONV_W), lambda i, j=j: (0, j))
    one = pl.BlockSpec((S, CONV_W), lambda i: (0, 0))
    small = pl.BlockSpec((3, CONV_W), lambda i: (0, 0))
    return pl.pallas_call(
        body, name=name, grid=(1,),
        in_specs=[col(0), col(0), col(1), col(2), small],
        out_specs=[one, one, one, small],
        out_shape=[_sds((S, CONV_W), BF)] * 3 + [_sds((3, CONV_W), F32)],
        compiler_params=_params("arbitrary"),
    )(dy, proj, proj, proj, cw)


SGU_HEADS = SGU_W // HEAD_DIM
CU_BLOCK = 2304 // SGU_W
CV_BLOCK = 2560 // SGU_W


def _sgu_common(cu_ref, cv_ref, gv_ref, tm):
    c_u = cu_ref[...].astype(F32)
    c_v = cv_ref[...].astype(F32)
    hm = _head_mean_matrix(SGU_W)
    u = _gelu(c_u)
    vg = _gelu(c_v)
    r = lax.rsqrt(_head_mean(vg * vg, hm) + EPS)
    vv = (vg * r) * gv_ref[...]
    head = lax.broadcasted_iota(jnp.int32, (CHUNK, SGU_W), 1) // HEAD_DIM
    tri = (lax.broadcasted_iota(jnp.int32, (CHUNK, CHUNK), 0) >=
           lax.broadcasted_iota(jnp.int32, (CHUNK, CHUNK), 1))
    return c_u, c_v, hm, u, vg, r, vv, head, tri


def _sgu_mix(w_ref, tri, head, vvc, bias):
    sv = bias
    for g in range(SGU_HEADS):
        wg = jnp.where(tri, w_ref[g], 0.0).astype(BF)
        sv = sv + jnp.where(head == g, jnp.dot(wg, vvc, preferred_element_type=F32), 0.0)
    return sv


def sgu_fwd(name, proj, gv, w, bias):
    S = proj.shape[0]
    tm = min(S, 512)

    def body(cu_ref, cv_ref, gv_ref, w_ref, b_ref, o_ref):
        _, _, _, u, _, _, vv, head, tri = _sgu_common(cu_ref, cv_ref, gv_ref, tm)
        vvb = vv.astype(BF)
        for ch in range(tm // CHUNK):
            rows = slice(ch * CHUNK, (ch + 1) * CHUNK)
            sv = _sgu_mix(w_ref, tri, head, vvb[rows], b_ref[...])
            o_ref[rows, :] = (u[rows] * sv).astype(BF)

    const = lambda shape: pl.BlockSpec(shape, lambda i: (0,) * len(shape))
    return pl.pallas_call(
        body, name=name, grid=(S // tm,),
        in_specs=[pl.BlockSpec((tm, SGU_W), lambda i: (i, CU_BLOCK)), pl.BlockSpec((tm, SGU_W), lambda i: (i, CV_BLOCK)),
                  const((1, SGU_W)), const((SGU_HEADS, CHUNK, CHUNK)), const((CHUNK, SGU_W))],
        out_specs=pl.BlockSpec((tm, SGU_W), lambda i: (i, 0)),
        out_shape=_sds((S, SGU_W), BF), compiler_params=_params("parallel"),
    )(proj, proj, gv, w, bias)


def sgu_bwd(name, dy, proj, gv, w, bias):
    S = proj.shape[0]
    tm = min(S, 512)

    def body(dy_ref, cu_ref, cv_ref, gv_ref, w_ref, b_ref, dcu_ref, dcv_ref, dw_ref, db_ref, dgv_ref, dvv_s):
        i = pl.program_id(0)
        c_u, c_v, hm, u, vg, r, vv, head, tri = _sgu_common(cu_ref, cv_ref, gv_ref, tm)
        vvb = vv.astype(BF)
        d = dy_ref[...].astype(F32)
        ind = (lax.broadcasted_iota(jnp.int32, (SGU_W, LANES), 0) // HEAD_DIM ==
               lax.broadcasted_iota(jnp.int32, (SGU_W, LANES), 1)).astype(F32)
        dw_acc = [jnp.zeros((CHUNK, CHUNK), F32) for _ in range(SGU_HEADS)]
        db_acc = jnp.zeros((CHUNK, LANES), F32)
        for ch in range(tm // CHUNK):
            rows = slice(ch * CHUNK, (ch + 1) * CHUNK)
            sv = _sgu_mix(w_ref, tri, head, vvb[rows], b_ref[...])
            dcu_ref[rows, :] = (d[rows] * sv * _gelu_grad(c_u[rows])).astype(BF)
            dsv = d[rows] * u[rows]
            db_acc = db_acc + jnp.dot(dsv, ind, precision=HIGHEST, preferred_element_type=F32)
            dvv = jnp.zeros((CHUNK, SGU_W), F32)
            for g in range(SGU_HEADS):
                dsv_g = jnp.where(head == g, dsv, 0.0).astype(BF)
                wg = jnp.where(tri, w_ref[g], 0.0).astype(BF)
                dvv = dvv + lax.dot_general(wg, dsv_g, TN_DIMS, preferred_element_type=F32)
                dw_acc[g] = dw_acc[g] + lax.dot_general(dsv_g, vvb[rows], NT_DIMS, preferred_element_type=F32)
            dvv_s[rows, :] = dvv
        dvv = dvv_s[...]
        gvv = gv_ref[...]
        t = dvv * gvv
        dvg = r * t - vg * (r * r * r) * _head_mean(t * vg, hm)
        dcv_ref[...] = (dvg * _gelu_grad(c_v)).astype(BF)
        dgv = jnp.sum(dvv * (vg * r), axis=0, keepdims=True)

        @pl.when(i == 0)
        def _():
            for g in range(SGU_HEADS):
                dw_ref[g] = jnp.where(tri, dw_acc[g], 0.0)
            db_ref[...] = db_acc
            dgv_ref[...] = dgv

        @pl.when(i > 0)
        def _():
            for g in range(SGU_HEADS):
                dw_ref[g] += jnp.where(tri, dw_acc[g], 0.0)
            db_ref[...] += db_acc
            dgv_ref[...] += dgv

    const = lambda shape: pl.BlockSpec(shape, lambda i: (0,) * len(shape))
    tile = pl.BlockSpec((tm, SGU_W), lambda i: (i, 0))
    return pl.pallas_call(
        body, name=name, grid=(S // tm,),
        in_specs=[pl.BlockSpec((tm, SGU_W), lambda i: (i, 3)),
                  pl.BlockSpec((tm, SGU_W), lambda i: (i, CU_BLOCK)), pl.BlockSpec((tm, SGU_W), lambda i: (i, CV_BLOCK)),
                  const((1, SGU_W)), const((SGU_HEADS, CHUNK, CHUNK)), const((CHUNK, SGU_W))],
        out_specs=[tile, tile, const((SGU_HEADS, CHUNK, CHUNK)), const((CHUNK, LANES)), const((1, SGU_W))],
        out_shape=[_sds((S, SGU_W), BF), _sds((S, SGU_W), BF), _sds((SGU_HEADS, CHUNK, CHUNK), F32),
                   _sds((CHUNK, LANES), F32), _sds((1, SGU_W), F32)],
        scratch_shapes=[pltpu.VMEM((tm, SGU_W), F32)],
        compiler_params=_params("arbitrary"),
    )(dy, proj, proj, gv, w, bias)


HEAD_PAIRS = ATTN_W // LANES
Q_BLOCK0 = 768 // LANES
K_BLOCK0 = 1280 // LANES
V_BLOCK0 = 1792 // LANES


def _attn_tile(S):
    return min(S, 256)


def _qk_norm(x, g, hm):
    r = lax.rsqrt(_head_mean(x * x, hm) + EPS)
    return r, (x * r) * g


def _logits(qh, kb, causal):
    z = lax.dot_general(qh, kb, NT_DIMS, preferred_element_type=F32)
    lb = _log_sigmoid(z)
    return lb, jnp.where(causal, lb - z, 0.0)


def attn_fwd(name, proj, gq, gk):
    S = proj.shape[0]
    T = _attn_tile(S)
    nq = S // T

    def body(q_ref, k_ref, v_ref, gq_ref, gk_ref, o_ref, tot_ref, kn_s):
        qi = pl.program_id(1)
        hm = _head_mean_matrix(LANES)

        @pl.when(qi == 0)
        def _():
            kn_s[...] = _qk_norm(k_ref[...].astype(F32), gk_ref[...], hm)[1].astype(BF)

        qn = _qk_norm(q_ref[...].astype(F32), gq_ref[...], hm)[1]
        lane = lax.broadcasted_iota(jnp.int32, (T, LANES), 1)
        qh = [jnp.where(lane < HEAD_DIM, qn, 0.0).astype(BF), jnp.where(lane >= HEAD_DIM, qn, 0.0).astype(BF)]
        rowi = lax.broadcasted_iota(jnp.int32, (T, T), 0)
        coli = lax.broadcasted_iota(jnp.int32, (T, T), 1)
        u_excl = (rowi > coli).astype(BF)

        def step(it, carry):
            j = qi - it
            off = pl.multiple_of(j * T, T)
            kb = kn_s[pl.ds(off, T), :]
            vb = v_ref[pl.ds(off, T), :].astype(BF)
            causal = (coli + j * T) < (rowi + qi * T)
            out = []
            for h in range(2):
                o, run = carry[2 * h], carry[2 * h + 1]
                lb, lr = _logits(qh[h], kb, causal)
                a = jnp.where(causal, jnp.exp(lb + _split_dot(lr, u_excl) + run), 0.0)
                out += [o + jnp.dot(a.astype(BF), vb, preferred_element_type=F32),
                        run + jnp.sum(lr, axis=-1, keepdims=True)]
            return tuple(out)

        zero = (jnp.zeros((T, LANES), F32), jnp.zeros((T, 1), F32))
        res = lax.fori_loop(0, qi + 1, step, zero + zero)
        o_ref[...] = jnp.where(lane < HEAD_DIM, res[0], res[2]).astype(BF)
        tot_ref[...] = jnp.where(lane < HEAD_DIM, res[1], res[3])

    gain = pl.BlockSpec((1, LANES), lambda hp, qi: (0, 0))
    full = lambda b0: pl.BlockSpec((S, LANES), lambda hp, qi, b0=b0: (0, b0 + hp))
    tile = pl.BlockSpec((T, LANES), lambda hp, qi: (qi, hp))
    return pl.pallas_call(
        body, name=name, grid=(HEAD_PAIRS, nq),
        in_specs=[pl.BlockSpec((T, LANES), lambda hp, qi: (qi, Q_BLOCK0 + hp)), full(K_BLOCK0), full(V_BLOCK0), gain, gain],
        out_specs=[tile, tile],
        out_shape=[_sds((S, ATTN_W), BF), _sds((S, ATTN_W), F32)],
        scratch_shapes=[pltpu.VMEM((S, LANES), BF)],
        compiler_params=_params("arbitrary", "arbitrary"),
    )(proj, proj, proj, gq, gk)


def attn_bwd(name, dy, proj, tot, gq, gk):
    S = proj.shape[0]
    T = _attn_tile(S)
    nq = S // T

    def body(q_ref, k_ref, v_ref, tot_ref, do_ref, gq_ref, gk_ref,
             dq_ref, dk_ref, dv_ref, dgq_ref, dgk_ref, kn_s, dkn_s, dv_s):
        hp = pl.program_id(0)
        qi = pl.program_id(1)
        hm = _head_mean_matrix(LANES)

        @pl.when(qi == 0)
        def _():
            kn_s[...] = _qk_norm(k_ref[...].astype(F32), gk_ref[...], hm)[1].astype(BF)
            dkn_s[...] = jnp.zeros_like(dkn_s)
            dv_s[...] = jnp.zeros_like(dv_s)

        q = q_ref[...].astype(F32)
        rq, qn = _qk_norm(q, gq_ref[...], hm)
        lane = lax.broadcasted_iota(jnp.int32, (T, LANES), 1)
        lo, hi = lane < HEAD_DIM, lane >= HEAD_DIM
        qh = [jnp.where(lo, qn, 0.0).astype(BF), jnp.where(hi, qn, 0.0).astype(BF)]
        do = do_ref[...].astype(F32)
        doh = [jnp.where(lo, do, 0.0).astype(BF), jnp.where(hi, do, 0.0).astype(BF)]
        total = [tot_ref[:, 0:1], tot_ref[:, HEAD_DIM:HEAD_DIM + 1]]
        rowi = lax.broadcasted_iota(jnp.int32, (T, T), 0)
        coli = lax.broadcasted_iota(jnp.int32, (T, T), 1)
        u_upto = (rowi <= coli).astype(BF)
        u_before = (rowi < coli).astype(BF)

        def step(j, carry):
            off = pl.multiple_of(j * T, T)
            kb = kn_s[pl.ds(off, T), :]
            vb = v_ref[pl.ds(off, T), :].astype(BF)
            causal = (coli + j * T) < (rowi + qi * T)
            out = []
            dkn = jnp.zeros((T, LANES), F32)
            dvb = jnp.zeros((T, LANES), F32)
            for h in range(2):
                dq, run, grun = carry[3 * h], carry[3 * h + 1], carry[3 * h + 2]
                lb, lr = _logits(qh[h], kb, causal)
                after = total[h] - run - _split_dot(lr, u_upto)
                a = jnp.where(causal, jnp.exp(lb + after), 0.0)
                g = lax.dot_general(doh[h], vb, NT_DIMS, preferred_element_type=F32) * a
                dvb = dvb + lax.dot_general(a.astype(BF), doh[h], TN_DIMS, preferred_element_type=F32)
                c = grun + _split_dot(g, u_before)
                sig = jnp.exp(lb)
                dz = jnp.where(causal, g * (1.0 - sig) - c * sig, 0.0).astype(BF)
                dkn = dkn + lax.dot_general(dz, qh[h], TN_DIMS, preferred_element_type=F32)
                out += [dq + jnp.dot(dz, kb, preferred_element_type=F32),
                        run + jnp.sum(lr, axis=-1, keepdims=True),
                        grun + jnp.sum(g, axis=-1, keepdims=True)]
            dkn_s[pl.ds(off, T), :] += dkn
            dv_s[pl.ds(off, T), :] += dvb
            return tuple(out)

        zero = (jnp.zeros((T, LANES), F32), jnp.zeros((T, 1), F32), jnp.zeros((T, 1), F32))
        res = lax.fori_loop(0, qi + 1, step, zero + zero)
        dqn = jnp.where(lo, res[0], res[3])
        gq_v = gq_ref[...]
        t = dqn * gq_v
        dq_ref[...] = (rq * t - q * (rq * rq * rq) * _head_mean(t * q, hm)).astype(BF)
        dgq = jnp.sum(dqn * (q * rq), axis=0, keepdims=True) * SCALE
        first = jnp.logical_and(hp == 0, qi == 0)

        @pl.when(first)
        def _():
            dgq_ref[...] = dgq

        @pl.when(jnp.logical_not(first))
        def _():
            dgq_ref[...] += dgq

        @pl.when(qi == nq - 1)
        def _():
            k = k_ref[...].astype(F32)
            rk = _qk_norm(k, gk_ref[...], hm)[0]
            dkn = dkn_s[...]
            tk = dkn * gk_ref[...]
            dk_ref[...] = (rk * tk - k * (rk * rk * rk) * _head_mean(tk * k, hm)).astype(BF)
            dgk = jnp.sum(dkn * (k * rk), axis=0, keepdims=True)
            dv_ref[...] = dv_s[...].astype(BF)

            @pl.when(hp == 0)
            def _():
                dgk_ref[...] = dgk

            @pl.when(hp > 0)
            def _():
                dgk_ref[...] += dgk

            @pl.when(hp == HEAD_PAIRS - 1)
            def _():
                fold = (lax.broadcasted_iota(jnp.int32, (LANES, LANES), 0) % HEAD_DIM ==
                        lax.broadcasted_iota(jnp.int32, (LANES, LANES), 1) % HEAD_DIM).astype(F32)
                dgq_ref[...] = jnp.dot(dgq_ref[...], fold, precision=HIGHEST, preferred_element_type=F32)
                dgk_ref[...] = jnp.dot(dgk_ref[...], fold, precision=HIGHEST, preferred_element_type=F32)

    gain = pl.BlockSpec((1, LANES), lambda hp, qi: (0, 0))
    full = lambda b0: pl.BlockSpec((S, LANES), lambda hp, qi, b0=b0: (0, b0 + hp))
    tile = pl.BlockSpec((T, LANES), lambda hp, qi: (qi, hp))
    col = pl.BlockSpec((S, LANES), lambda hp, qi: (0, hp))
    dgain = pl.BlockSpec((1, LANES), lambda hp, qi: (0, 0))
    return pl.pallas_call(
        body, name=name, grid=(HEAD_PAIRS, nq),
        in_specs=[pl.BlockSpec((T, LANES), lambda hp, qi: (qi, Q_BLOCK0 + hp)), full(K_BLOCK0), full(V_BLOCK0),
                  tile, pl.BlockSpec((T, LANES), lambda hp, qi: (qi, 2 + hp)), gain, gain],
        out_specs=[tile, col, col, dgain, dgain],
        out_shape=[_sds((S, ATTN_W), BF)] * 3 + [_sds((1, LANES), F32)] * 2,
        scratch_shapes=[pltpu.VMEM((S, LANES), BF), pltpu.VMEM((S, LANES), F32), pltpu.VMEM((S, LANES), F32)],
        compiler_params=_params("arbitrary", "arbitrary"),
    )(proj, proj, proj, tot, dy, gq, gk)


def ple_bwd_elem(name, dh, gp, pp, after=()):
    S, D = dh.shape
    tm = min(S, 512)

    def body(dh_ref, gp_ref, pp_ref, *rest):
        dgp_ref, dpp_ref = rest[-2:]
        d = dh_ref[...]
        gate = jax.nn.sigmoid(gp_ref[...].astype(F32))
        dpp_ref[...] = (d * gate).astype(BF)
        dgp_ref[...] = (d * pp_ref[...].astype(F32) * gate * (1.0 - gate)).astype(BF)

    tile = pl.BlockSpec((tm, D), lambda i: (i, 0))
    return pl.pallas_call(
        body, name=name, grid=(S // tm,), in_specs=[tile] * 3 + [pl.BlockSpec(memory_space=pl.ANY)] * len(after),
        out_specs=[tile] * 2, out_shape=[_sds((S, D), BF)] * 2, compiler_params=_params("parallel"),
    )(dh, gp, pp, *after)


def loss_head(name, h, target):
    S, D = h.shape
    tm = min(S, 512)

    def body(h_ref, t_ref, loss_ref, dh_ref):
        i = pl.program_id(0)
        e = h_ref[...] - t_ref[...]
        dh_ref[...] = e * (1.0 / D)
        part = jnp.zeros((8, LANES), F32) + 0.5 * jnp.sum(jnp.mean(e * e, axis=-1, keepdims=True))

        @pl.when(i == 0)
        def _():
            loss_ref[...] = part

        @pl.when(i > 0)
        def _():
            loss_ref[...] += part

    tile = pl.BlockSpec((tm, D), lambda i: (i, 0))
    return pl.pallas_call(
        body, name=name, grid=(S // tm,), in_specs=[tile, tile],
        out_specs=[pl.BlockSpec((8, LANES), lambda i: (0, 0)), tile],
        out_shape=[_sds((8, LANES), F32), _sds((S, D), F32)], compiler_params=_params("arbitrary"),
    )(h, target)


def adamw(name, w, g, m, v):
    R, C = w.shape
    tr = R
    for cand in (512, 256, 128, 64, 32, 16, 8):
        if R % cand == 0:
            tr = cand
            break
    c1 = 1.0 - ADAM_B1 ** ADAM_STEP
    c2 = 1.0 - ADAM_B2 ** ADAM_STEP

    def body(w_ref, g_ref, m_ref, v_ref, d_ref, nm_ref, nv_ref):
        gg = g_ref[...]
        nm = ADAM_B1 * m_ref[...] + (1.0 - ADAM_B1) * gg
        nv = ADAM_B2 * v_ref[...] + (1.0 - ADAM_B2) * (gg * gg)
        nm_ref[...] = nm
        nv_ref[...] = nv
        d_ref[...] = -ADAM_LR * ((nm / c1) / (jnp.sqrt(nv / c2) + ADAM_EPS) + ADAM_WD * w_ref[...])

    tile = pl.BlockSpec((tr, C), lambda i: (i, 0))
    return pl.pallas_call(
        body, name=name, grid=(R // tr,), in_specs=[tile] * 4, out_specs=[tile] * 3,
        out_shape=[_sds((R, C), F32)] * 3, compiler_params=_params("parallel"),
    )(w, g, m, v)


def _relu2(u):
    r = jnp.maximum(u.astype(F32), 0.0)
    return r * r


def layer_fwd(tag, h0, p_bf, wt, after=()):
    hn1 = rmsnorm(f"{tag}_norm1", h0, wt["norm1_g"], after)
    proj = mm_nn(f"{tag}_proj", hn1, wt["w_in"], tn=1408)
    ya = conv_fwd(f"{tag}_conv", proj, wt["conv_w"])
    yb, yb_tot = attn_fwd(f"{tag}_attn", proj, wt["gq"], wt["gk"])
    yc = sgu_fwd(f"{tag}_sgu", proj, wt["gv"], wt["sgu_w"], wt["sgu_bias"])
    y = jnp.concatenate([ya, yb, yc], axis=-1)
    h1 = mm_nn(f"{tag}_out", y, wt["w_out"], extras=(h0,), epi=lambda acc, h: (h + acc,), out_dtypes=(F32,))
    hn2 = rmsnorm(f"{tag}_norm2", h1, wt["norm2_g"])
    uu = mm_nn(f"{tag}_ff1", hn2, wt["w_ff1"])
    h2 = mm_nn(f"{tag}_ff2", uu, wt["w_ff2"], pro=_relu2, extras=(h1,), epi=lambda acc, h: (h + acc,),
               out_dtypes=(F32,))
    hn3 = rmsnorm(f"{tag}_norm3", h2, wt["norm3_g"])
    gp = mm_nn(f"{tag}_gate", hn3, wt["w_ple_gate"])
    h3, pp = mm_nn(f"{tag}_ple", p_bf, wt["w_ple_proj"], extras=(gp, h2),
                   epi=lambda acc, g, h: (h + jax.nn.sigmoid(g.astype(F32)) * acc, acc), out_dtypes=(F32, BF))
    saved = dict(h0=h0, h1=h1, h2=h2, hn1=hn1, hn2=hn2, hn3=hn3, proj=proj, yb_tot=yb_tot, y=y, uu=uu, gp=gp, pp=pp,
                 p_bf=p_bf)
    return h3, saved


def layer_bwd(tag, dh3, sv, wt, after=()):
    dgp, dpp = ple_bwd_elem(f"{tag}_dple", dh3, sv["gp"], sv["pp"], after)
    g = {}
    g["w_ple_proj"] = mm_tn(f"{tag}_dwp", sv["p_bf"], dpp, col_blocks=N_CHIPS)
    g["w_ple_gate"] = mm_tn(f"{tag}_dwg", sv["hn3"], dgp)
    dh2, g["norm3_g"] = mm_nt_rmsbwd(f"{tag}_dnorm3", dgp, wt["w_ple_gate"], sv["h2"], wt["norm3_g"], dh3)

    duu = mm_nt(f"{tag}_dff2", dh2, wt["w_ff2"], extras=(sv["uu"],),
                epi=lambda acc, u: acc * (2.0 * jnp.maximum(u.astype(F32), 0.0)))
    g["w_ff2"] = mm_tn(f"{tag}_dw2", sv["uu"], dh2, pro_x=_relu2)
    g["w_ff1"] = mm_tn(f"{tag}_dw1", sv["hn2"], duu, col_blocks=N_CHIPS)
    dh1, g["norm2_g"] = mm_nt_rmsbwd(f"{tag}_dnorm2", duu, wt["w_ff1"], sv["h1"], wt["norm2_g"], dh2)

    dy = mm_nt(f"{tag}_dout", dh1, wt["w_out"])
    g["w_out"] = mm_tn(f"{tag}_dwo", sv["y"], dh1)
    dab, dac, dah, g["conv_w"] = conv_bwd(f"{tag}_dconv", dy, sv["proj"], wt["conv_w"])
    dq, dk, dv, g["gq"], g["gk"] = attn_bwd(f"{tag}_dattn", dy, sv["proj"], sv["yb_tot"], wt["gq"], wt["gk"])
    dcu, dcv, g["sgu_w"], g["sgu_bias"], g["gv"] = sgu_bwd(f"{tag}_dsgu", dy, sv["proj"], wt["gv"], wt["sgu_w"],
                                                           wt["sgu_bias"])
    dproj = jnp.concatenate([dab, dac, dah, dq, dk, dv, dcu, dcv], axis=-1)
    g["w_in"] = mm_tn(f"{tag}_dwi", sv["hn1"], dproj, tn=1408)
    dh0, g["norm1_g"] = mm_nt_rmsbwd(f"{tag}_dnorm1", dproj, wt["w_in"], sv["h0"], wt["norm1_g"], dh1)
    return dh0, g


def prep_small(norm1_g, q_norm_g, k_norm_g, sgu_norm_g, sgu_w, sgu_b, norm2_g, norm3_g, conv_w_full):
    return dict(
        norm1_g=norm1_g, norm2_g=norm2_g, norm3_g=norm3_g, conv_w=conv_w_full,
        gq=(jnp.tile(q_norm_g, 2) * SCALE).reshape(1, LANES), gk=jnp.tile(k_norm_g, 2).reshape(1, LANES),
        gv=sgu_norm_g.reshape(1, SGU_W), sgu_w=sgu_w, sgu_bias=jnp.repeat(sgu_b.T, HEAD_DIM, axis=1))


def small_grads(g):
    return dict(
        norm1_g=g["norm1_g"][0], norm2_g=g["norm2_g"][0], norm3_g=g["norm3_g"][0], conv_w=g["conv_w"],
        q_norm_g=g["gq"][0, :HEAD_DIM], k_norm_g=g["gk"][0, :HEAD_DIM], sgu_norm_g=g["gv"][0], sgu_w=g["sgu_w"],
        sgu_b=g["sgu_bias"][:, :SGU_HEADS].T)


HBM_SPEC = pl.BlockSpec(memory_space=pltpu.HBM)
BIG = ("w_in", "w_out", "w_ff1", "w_ff2", "w_ple_gate", "w_ple_proj")


def _mesh_pos():
    return lax.axis_index("x"), lax.axis_index("y"), lax.axis_index("c")


def _other_chips(x, y):
    return [(1 - x, y), (x, 1 - y), (1 - x, 1 - y)]


def _half(rows, core):
    h = rows // 2
    return pl.ds(pl.multiple_of(core * h, 16), h)


def _remote(src, dst, send_sems, recv_sems, k, to):
    return pltpu.make_async_remote_copy(src_ref=src, dst_ref=dst, send_sem=send_sems.at[k], recv_sem=recv_sems.at[k],
                                        device_id=to, device_id_type=MESH)


SEM_SPEC = pl.BlockSpec(memory_space=pltpu.SEMAPHORE)
ANY_SPEC = pl.BlockSpec(memory_space=pl.ANY)
SIDE_EFFECT = pltpu.SideEffectType.DATAFLOW_SIDE_EFFECTING


def _in_hbm(arrays):
    return [pltpu.with_memory_space_constraint(a, pltpu.HBM) for a in arrays]


def copies_start(name, srcs, lands, plan, after=()):
    n, na = len(srcs), len(after)

    def body(*refs):
        src_refs, land_refs = refs[:n], refs[n:2 * n]
        send_sem, recv_sem = refs[2 * n + na], refs[2 * n + na + 1]
        token = refs[-1]
        x, y, c = _mesh_pos()
        for i in range(n):
            for src, dst, dev in plan(i, src_refs[i], land_refs[i], x, y, c):
                pltpu.make_async_remote_copy(src_ref=src, dst_ref=dst, send_sem=send_sem, recv_sem=recv_sem,
                                             device_id=dev, device_id_type=MESH).start()
        token[...] = jnp.zeros_like(token)

    out = pl.pallas_call(
        body, name=name,
        in_specs=[HBM_SPEC] * (2 * n) + [ANY_SPEC] * na,
        out_specs=(SEM_SPEC, SEM_SPEC, *[HBM_SPEC] * (2 * n), pl.BlockSpec(memory_space=pltpu.VMEM)),
        out_shape=(pltpu.SemaphoreType.DMA(()), pltpu.SemaphoreType.DMA(()),
                   *[pltpu.HBM(a.shape, a.dtype) for a in (*srcs, *lands)], _sds((8, LANES), F32)),
        input_output_aliases={i: 2 + i for i in range(2 * n)},
        compiler_params=pltpu.CompilerParams(has_side_effects=SIDE_EFFECT),
    )(*_in_hbm(srcs), *_in_hbm(lands), *after)
    return out[0], out[1], list(out[2:2 + n]), list(out[2 + n:2 + 2 * n]), out[-1]


def copies_wait(name, started, plan, after=()):
    send_sem, recv_sem, srcs, lands, _ = started
    n, na = len(srcs), len(after)

    def body(*refs):
        src_refs, land_refs = refs[:n], refs[n:2 * n]
        send_sem, recv_sem = refs[2 * n], refs[2 * n + 1]
        x, y, c = _mesh_pos()
        for i in range(n):
            for src, dst, dev in plan(i, src_refs[i], land_refs[i], x, y, c):
                cp = pltpu.make_async_remote_copy(src_ref=src, dst_ref=dst, send_sem=send_sem, recv_sem=recv_sem,
                                                  device_id=dev, device_id_type=MESH)
                cp.wait_send()
                cp.wait_recv()

    out = pl.pallas_call(
        body, name=name,
        in_specs=[HBM_SPEC] * (2 * n) + [SEM_SPEC, SEM_SPEC] + [ANY_SPEC] * na,
        out_specs=[HBM_SPEC] * (2 * n),
        out_shape=[pltpu.HBM(a.shape, a.dtype) for a in (*srcs, *lands)],
        input_output_aliases={i: i for i in range(2 * n)},
        compiler_params=pltpu.CompilerParams(has_side_effects=SIDE_EFFECT),
    )(*srcs, *lands, send_sem, recv_sem, *after)
    return list(out[:n]), list(out[n:])


def _gather_plan(shapes):
    def plan(i, src, land, x, y, c):
        me = 2 * x + y
        rows = _half(shapes[i][0], c)
        return [(src.at[rows], land.at[me, rows], (*chip, c)) for chip in _other_chips(x, y)]
    return plan


def _gather_arrivals(shapes):
    def plan(i, src, land, x, y, c):
        rows = _half(shapes[i][0], c)
        return [(src.at[rows], land.at[2 * chip[0] + chip[1], rows], (*chip, c)) for chip in _other_chips(x, y)]
    return plan


def gather_finish(name, shards, lands):
    n = len(shards)

    def body(*refs):
        ins, outs = refs[:n], refs[2 * n:3 * n]
        send_sems, recv_sems = refs[3 * n:]
        x, y, c = _mesh_pos()
        me = 2 * x + y
        chips = _other_chips(x, y)
        sibling = (x, y, 1 - c)
        sends = []
        for i in range(n):
            rows = shards[i].shape[0]
            cp = _remote(ins[i], outs[i].at[me], send_sems, recv_sems, 4 * i + 3, sibling)
            cp.start()
            sends.append(cp)
            for k, chip in enumerate(chips):
                region = outs[i].at[2 * chip[0] + chip[1], _half(rows, c)]
                cp = _remote(region, region, send_sems, recv_sems, 4 * i + k, sibling)
                cp.start()
                sends.append(cp)
        for i in range(n):
            rows = shards[i].shape[0]
            _remote(ins[i], outs[i].at[me], send_sems, recv_sems, 4 * i + 3, sibling).wait_recv()
            for k, chip in enumerate(chips):
                region = outs[i].at[2 * chip[0] + chip[1], _half(rows, 1 - c)]
                _remote(region, region, send_sems, recv_sems, 4 * i + k, sibling).wait_recv()
        for cp in sends:
            cp.wait_send()

    return pl.pallas_call(
        body, name=name, in_specs=[HBM_SPEC] * (2 * n), out_specs=[HBM_SPEC] * n,
        out_shape=[_sds(a.shape, a.dtype) for a in lands],
        input_output_aliases={n + i: i for i in range(n)},
        scratch_shapes=[pltpu.SemaphoreType.DMA((4 * n,)), pltpu.SemaphoreType.DMA((4 * n,))],
    )(*shards, *lands)


def exchange_other_half(name, grads):
    n = len(grads)

    def body(*refs):
        ins, outs = refs[:n], refs[n:2 * n]
        send_sems, recv_sems = refs[2 * n:]
        x, y, c = _mesh_pos()
        copies = []
        for i in range(n):
            rows = grads[i].shape[1]
            cp = _remote(ins[i].at[:, _half(rows, 1 - c)], outs[i], send_sems, recv_sems, i, (x, y, 1 - c))
            cp.start()
            copies.append(cp)
        for cp in copies:
            cp.wait()

    return pl.pallas_call(
        body, name=name, in_specs=[HBM_SPEC] * n, out_specs=[HBM_SPEC] * n,
        out_shape=[_sds((N_CHIPS, g.shape[1] // 2, g.shape[2]), g.dtype) for g in grads],
        scratch_shapes=[pltpu.SemaphoreType.DMA((n,)), pltpu.SemaphoreType.DMA((n,))],
    )(*grads)


def add_own_half(name, core, grads, got):
    n = len(grads)

    def body(core_ref, *refs):
        for i in range(n):
            refs[2 * n + i][...] = (refs[i][...].astype(F32) + refs[n + i][...].astype(F32)).astype(BF)

    def spec(g, own):
        blk = (None, g.shape[1] // 2, g.shape[2])
        return pl.BlockSpec(blk, (lambda j, core_ref: (j, core_ref[0], 0)) if own else (lambda j, core_ref: (j, 0, 0)))

    return pl.pallas_call(
        body, name=name,
        grid_spec=pltpu.PrefetchScalarGridSpec(
            num_scalar_prefetch=1, grid=(N_CHIPS,),
            in_specs=[spec(g, True) for g in grads] + [spec(g, False) for g in grads],
            out_specs=[spec(g, False) for g in grads]),
        out_shape=[_sds(r.shape, BF) for r in got], compiler_params=_params("parallel"),
    )(core, *grads, *got)


def _chips_plan(i, src, land, x, y, c):
    return [(src.at[2 * chip[0] + chip[1]], land.at[k], (*chip, c)) for k, chip in enumerate(_other_chips(x, y))]


def sum_chips(name, place, parts, got):
    n = len(got)

    def body(place_ref, *refs):
        for i in range(n):
            acc = refs[i][...].astype(F32)
            for k in range(N_CHIPS - 1):
                acc = acc + refs[n + i][k].astype(F32)
            refs[2 * n + i][...] = acc

    steps = 2
    return pl.pallas_call(
        body, name=name,
        grid_spec=pltpu.PrefetchScalarGridSpec(
            num_scalar_prefetch=1, grid=(steps,),
            in_specs=[pl.BlockSpec((None, g.shape[1] // steps, g.shape[2]), lambda t, place_ref: (place_ref[0], t, 0))
                      for g in parts] +
                     [pl.BlockSpec((N_CHIPS - 1, g.shape[1] // steps, g.shape[2]), lambda t, place_ref: (0, t, 0))
                      for g in got],
            out_specs=[pl.BlockSpec((g.shape[1] // steps, g.shape[2]),
                                    lambda t, place_ref: (place_ref[1] * steps + t, 0)) for g in got]),
        out_shape=[_sds((2 * g.shape[1], g.shape[2]), F32) for g in got], compiler_params=_params("parallel"),
    )(place, *parts, *got)


def exchange_final_halves(name, shards):
    n = len(shards)

    def body(*refs):
        outs = refs[n:2 * n]
        send_sems, recv_sems = refs[2 * n:]
        x, y, c = _mesh_pos()
        copies = []
        for i in range(n):
            mine = outs[i].at[_half(shards[i].shape[0], c)]
            cp = _remote(mine, mine, send_sems, recv_sems, i, (x, y, 1 - c))
            cp.start()
            copies.append(cp)
        for i, cp in enumerate(copies):
            cp.wait_send()
            theirs = outs[i].at[_half(shards[i].shape[0], 1 - c)]
            _remote(theirs, theirs, send_sems, recv_sems, i, (x, y, 1 - c)).wait_recv()

    return pl.pallas_call(
        body, name=name, in_specs=[HBM_SPEC] * n, out_specs=[HBM_SPEC] * n,
        out_shape=[_sds(a.shape, a.dtype) for a in shards], input_output_aliases={i: i for i in range(n)},
        scratch_shapes=[pltpu.SemaphoreType.DMA((n,)), pltpu.SemaphoreType.DMA((n,))],
    )(*shards)


def reduce_scatter_begin(tag, core, grads):
    got = exchange_other_half(f"{tag}_rs_pair", grads)
    parts = add_own_half(f"{tag}_rs_add", core, grads, got)
    lands = [lax.empty((N_CHIPS - 1,) + p.shape[1:], p.dtype) for p in parts]
    return copies_start(f"{tag}_rs_start", parts, lands, _chips_plan)


def reduce_scatter_end(tag, place, started, after):
    parts, got = copies_wait(f"{tag}_rs_wait", started, _chips_plan, after)
    return exchange_final_halves(f"{tag}_rs_join", sum_chips(f"{tag}_rs_sum", place, parts, got))


def small_allreduce(name, x):
    R = x.shape[0]
    H = R // 2

    def body(x_ref, o_ref, pair_ref, chip_ref, send_sems, recv_sems):
        xx, yy, c = _mesh_pos()
        me = 2 * xx + yy
        chips = _other_chips(xx, yy)
        sibling = (xx, yy, 1 - c)
        mine = pl.ds(pl.multiple_of(c * H, 8), H)
        theirs = pl.ds(pl.multiple_of((1 - c) * H, 8), H)
        a = _remote(x_ref.at[theirs], pair_ref.at[theirs], send_sems, recv_sems, 0, sibling)
        a.start()
        a.wait_send()
        _remote(x_ref.at[mine], pair_ref.at[mine], send_sems, recv_sems, 0, sibling).wait_recv()
        chip_ref[me] = x_ref[mine, :] + pair_ref[mine, :]
        sends = []
        for k, chip in enumerate(chips):
            cp = _remote(chip_ref.at[me], chip_ref.at[me], send_sems, recv_sems, 1 + k, (*chip, c))
            cp.start()
            sends.append(cp)
        for k, chip in enumerate(chips):
            slot = chip_ref.at[2 * chip[0] + chip[1]]
            _remote(slot, slot, send_sems, recv_sems, 1 + k, (*chip, c)).wait_recv()
        o_ref[mine, :] = (chip_ref[0] + chip_ref[1]) + (chip_ref[2] + chip_ref[3])
        b = _remote(o_ref.at[mine], o_ref.at[mine], send_sems, recv_sems, 4, sibling)
        b.start()
        b.wait_send()
        _remote(o_ref.at[theirs], o_ref.at[theirs], send_sems, recv_sems, 4, sibling).wait_recv()
        for cp in sends:
            cp.wait_send()

    return pl.pallas_call(
        body, name=name,
        in_specs=[pl.BlockSpec(memory_space=pltpu.VMEM)], out_specs=pl.BlockSpec(memory_space=pltpu.VMEM),
        out_shape=_sds((R, LANES), F32),
        scratch_shapes=[pltpu.VMEM((R, LANES), F32), pltpu.VMEM((N_CHIPS, H, LANES), F32),
                        pltpu.SemaphoreType.DMA((5,)), pltpu.SemaphoreType.DMA((5,))],
        compiler_params=pltpu.CompilerParams(vmem_limit_bytes=VMEM_LIMIT),
    )(x)


WEIGHTS = ("norm1_g", "w_in", "conv_w", "q_norm_g", "k_norm_g", "sgu_norm_g", "sgu_w", "sgu_b", "w_out", "norm2_g",
           "w_ff1", "w_ff2", "norm3_g", "w_ple_gate", "w_ple_proj")
SMALL = ("norm1_g", "norm2_g", "norm3_g", "q_norm_g", "k_norm_g", "sgu_norm_g", "sgu_w", "sgu_b", "conv_w")


def _pack_rows(arrays):
    flat = []
    for a in arrays:
        v = a.reshape(-1)
        flat.append(jnp.pad(v, (0, (-v.shape[0]) % LANES)))
    v = jnp.concatenate(flat)
    v = jnp.pad(v, (0, (-v.shape[0]) % (16 * LANES)))
    return v.reshape(-1, LANES)


def _unpack_rows(packed, shapes):
    out, pos = [], 0
    flat = packed.reshape(-1)
    for shp in shapes:
        size = math.prod(shp)
        out.append(flat[pos:pos + size].reshape(shp))
        pos += size + (-size) % LANES
    return out


def kernel(x, p, norm1_g, w_in, conv_w, q_norm_g, k_norm_g, sgu_norm_g, sgu_w, sgu_b, w_out, norm2_g, w_ff1, w_ff2, norm3_g, w_ple_gate, w_ple_proj, loss_target, m_norm1_g, m_w_in, m_conv_w, m_q_norm_g, m_k_norm_g, m_sgu_norm_g, m_sgu_w, m_sgu_b, m_w_out, m_norm2_g, m_w_ff1, m_w_ff2, m_norm3_g, m_w_ple_gate, m_w_ple_proj, v_norm1_g, v_w_in, v_conv_w, v_q_norm_g, v_k_norm_g, v_sgu_norm_g, v_sgu_w, v_sgu_b, v_w_out, v_norm2_g, v_w_ff1, v_w_ff2, v_norm3_g, v_w_ple_gate, v_w_ple_proj):
    w = dict(norm1_g=norm1_g, w_in=w_in, conv_w=conv_w, q_norm_g=q_norm_g, k_norm_g=k_norm_g, sgu_norm_g=sgu_norm_g,
             sgu_w=sgu_w, sgu_b=sgu_b, w_out=w_out, norm2_g=norm2_g, w_ff1=w_ff1, w_ff2=w_ff2, norm3_g=norm3_g,
             w_ple_gate=w_ple_gate, w_ple_proj=w_ple_proj)
    m = dict(norm1_g=m_norm1_g, w_in=m_w_in, conv_w=m_conv_w, q_norm_g=m_q_norm_g, k_norm_g=m_k_norm_g,
             sgu_norm_g=m_sgu_norm_g, sgu_w=m_sgu_w, sgu_b=m_sgu_b, w_out=m_w_out, norm2_g=m_norm2_g, w_ff1=m_w_ff1,
             w_ff2=m_w_ff2, norm3_g=m_norm3_g, w_ple_gate=m_w_ple_gate, w_ple_proj=m_w_ple_proj)
    v = dict(norm1_g=v_norm1_g, w_in=v_w_in, conv_w=v_conv_w, q_norm_g=v_q_norm_g, k_norm_g=v_k_norm_g,
             sgu_norm_g=v_sgu_norm_g, sgu_w=v_sgu_w, sgu_b=v_sgu_b, w_out=v_w_out, norm2_g=v_norm2_g, w_ff1=v_w_ff1,
             w_ff2=v_w_ff2, norm3_g=v_norm3_g, w_ple_gate=v_w_ple_gate, w_ple_proj=v_w_ple_proj)
    depth = w_in.shape[0]
    d_model = x.shape[-1]
    chip = 2 * lax.axis_index("x") + lax.axis_index("y")
    core = lax.axis_index("c")
    core_arr = core.reshape(1).astype(jnp.int32)

    cw_cols = conv_w.shape[-1]
    placed = lax.dynamic_update_slice(jnp.zeros((depth, 3, CONV_W), F32), conv_w, (0, 0, chip * cw_cols))
    placed = jnp.where(core == 0, placed, 0.0)
    conv_full = _unpack_rows(small_allreduce("conv_w_gather", _pack_rows([placed])), [(depth, 3, CONV_W)])[0]

    h = x[0]
    p_bf = p[:, 0].astype(BF)
    saved, full = [], []
    shard_shapes = [w[n].shape[1:] for n in BIG]

    def gather_start(l, after):
        shards = [w[n][l].astype(BF) for n in BIG]
        lands = [lax.empty((N_CHIPS,) + s.shape, BF) for s in shards]
        return copies_start(f"l{l}_gather_start", shards, lands, _gather_plan(shard_shapes), after)

    started = gather_start(0, ())
    for l in range(depth):
        shards, lands = copies_wait(f"l{l}_gather_wait", started, _gather_arrivals(shard_shapes), (h,))
        g_in, g_out, g_ff1, g_ff2, g_gate, g_proj = gather_finish(f"l{l}_gather_finish", shards, lands)
        token = ()
        if l + 1 < depth:
            started = gather_start(l + 1, (g_proj,))
            token = (started[-1],)
        wt = prep_small(norm1_g[l], q_norm_g[l], k_norm_g[l], sgu_norm_g[l], sgu_w[l], sgu_b[l], norm2_g[l], norm3_g[l],
                        conv_full[l])
        wt["w_in"] = jnp.transpose(g_in, (1, 0, 2)).reshape(d_model, -1)
        wt["w_out"] = g_out.reshape(-1, d_model)
        wt["w_ff1"] = g_ff1
        wt["w_ff2"] = g_ff2.reshape(-1, d_model)
        wt["w_ple_gate"] = g_gate.reshape(-1, d_model)
        wt["w_ple_proj"] = g_proj
        h, sv = layer_fwd(f"l{l}", h, p_bf[l], wt, token)
        saved.append(sv)
        full.append(wt)

    loss_tile, dh = loss_head("loss", h, loss_target[0])
    loss = lax.psum(loss_tile[0, 0], ("x", "y", "c"))

    reduced = [None] * depth
    small = [None] * depth
    chip_arr = jnp.stack([chip, core]).astype(jnp.int32)
    started = None
    for l in reversed(range(depth)):
        dh, g = layer_bwd(f"l{l}", dh, saved[l], full[l], () if started is None else (started[-1],))
        if started is not None:
            reduced[l + 1] = reduce_scatter_end(f"l{l + 1}", chip_arr, started, (dh,))
        small[l] = small_grads(g)
        shards_in = w_in.shape[-1]
        gl = [jnp.transpose(g["w_in"].reshape(d_model, N_CHIPS, shards_in), (1, 0, 2)),
              g["w_out"].reshape(N_CHIPS, -1, d_model), g["w_ff1"], g["w_ff2"].reshape(N_CHIPS, -1, d_model),
              g["w_ple_gate"].reshape(N_CHIPS, -1, d_model), g["w_ple_proj"]]
        started = reduce_scatter_begin(f"l{l}", core_arr, gl)
    reduced[0] = reduce_scatter_end("l0", chip_arr, started, (dh,))

    grads = {n: jnp.stack([reduced[l][i] for l in range(depth)]) for i, n in enumerate(BIG)}
    packed = _pack_rows([small[l][n] for l in range(depth) for n in SMALL])
    shapes = [small[l][n].shape for l in range(depth) for n in SMALL]
    pieces = _unpack_rows(small_allreduce("small_grads", packed), shapes)
    for i, n in enumerate(SMALL):
        grads[n] = jnp.stack([pieces[l * len(SMALL) + i] for l in range(depth)])
    grads["conv_w"] = lax.dynamic_slice(grads["conv_w"], (0, 0, chip * cw_cols), (depth, 3, cw_cols))

    delta, new_m, new_v = {}, {}, {}
    for n in WEIGHTS:
        shp = w[n].shape
        two_d = (-1, shp[-1]) if n != "sgu_w" else (-1, LANES)
        d, nm, nv = adamw(f"adamw_{n}", w[n].reshape(two_d), grads[n].reshape(two_d), m[n].reshape(two_d),
                          v[n].reshape(two_d))
        delta[n], new_m[n], new_v[n] = d.reshape(shp), nm.reshape(shp), nv.reshape(shp)

    return (loss, dh[None], *[grads[n] for n in WEIGHTS], *[delta[n] for n in WEIGHTS], *[new_m[n] for n in WEIGHTS],
            *[new_v[n] for n in WEIGHTS])
```

```python
import functools
import math

import jax
import jax.numpy as jnp
from jax import lax
from jax.experimental import pallas as pl
from jax.experimental.pallas import tpu as pltpu

F32 = jnp.float32
BF = jnp.bfloat16
MESH = pl.DeviceIdType.MESH
HIGHEST = lax.Precision.HIGHEST

EPS = 1e-6
HEAD_DIM = 64
CONV_W = 256
ATTN_W = 512
SGU_W = 256
CHUNK = 128
N_CHIPS = 4
SCALE = HEAD_DIM ** -0.5
LANES = 128
VMEM_LIMIT = 56 * 1024 * 1024

ADAM_LR = 0.001
ADAM_B1 = 0.9
ADAM_B2 = 0.999
ADAM_EPS = 1e-08
ADAM_WD = 0.01
ADAM_STEP = 10

NT_DIMS = (((1,), (1,)), ((), ()))
TN_DIMS = (((0,), (0,)), ((), ()))


def _params(*sem):
    return pltpu.CompilerParams(dimension_semantics=sem if sem else None, vmem_limit_bytes=VMEM_LIMIT)


def _sds(shape, dtype):
    return jax.ShapeDtypeStruct(shape, dtype)


def _erf(x):
    return lax.erf(x)


def _gelu(x):
    return 0.5 * x * (1.0 + _erf(x * (2.0 ** -0.5)))


def _gelu_grad(x):
    return 0.5 * (1.0 + _erf(x * (2.0 ** -0.5))) + x * jnp.exp(-0.5 * x * x) * (1.0 / math.sqrt(2.0 * math.pi))


def _log_sigmoid(z):
    return jnp.minimum(z, 0.0) - jnp.log(1.0 + jnp.exp(-jnp.abs(z)))


def _head_mean_matrix(width):
    r = lax.broadcasted_iota(jnp.int32, (width, width), 0) // HEAD_DIM
    c = lax.broadcasted_iota(jnp.int32, (width, width), 1) // HEAD_DIM
    return jnp.where(r == c, 1.0 / HEAD_DIM, 0.0).astype(F32)


def _head_mean(x, m):
    return jnp.dot(x, m, precision=HIGHEST, preferred_element_type=F32)


def rmsnorm(name, h, g, after=()):
    S, D = h.shape
    tm = min(S, 512)

    def body(h_ref, g_ref, *rest):
        x = h_ref[...]
        r = lax.rsqrt(jnp.mean(x * x, axis=-1, keepdims=True) + EPS)
        rest[-1][...] = ((x * r) * g_ref[...]).astype(BF)

    return pl.pallas_call(
        body, name=name, grid=(S // tm,),
        in_specs=[pl.BlockSpec((tm, D), lambda i: (i, 0)), pl.BlockSpec((1, D), lambda i: (0, 0))] +
                 [pl.BlockSpec(memory_space=pl.ANY)] * len(after),
        out_specs=pl.BlockSpec((tm, D), lambda i: (i, 0)),
        out_shape=_sds((S, D), BF), compiler_params=_params("parallel"),
    )(h, g.reshape(1, D), *after)


def mm_nn(name, x, w, *, extras=(), pro=None, epi=None, out_dtypes=None, tm=512, tn=512):
    S, K = x.shape
    if w.ndim == 3:
        J, _, tn = w.shape
        N = J * tn
        w_spec = pl.BlockSpec((None, K, tn), lambda n, m: (n, 0, 0))
    else:
        N = w.shape[1]
        tn = min(tn, N)
        w_spec = pl.BlockSpec((K, tn), lambda n, m: (0, n))
    tm = min(tm, S)
    out_dtypes = (BF,) if out_dtypes is None else out_dtypes
    n_ex, n_out = len(extras), len(out_dtypes)

    def body(x_ref, w_ref, *rest):
        xv = x_ref[...]
        if pro is not None:
            xv = pro(xv)
        acc = jnp.dot(xv.astype(BF), w_ref[...], preferred_element_type=F32)
        outs = (acc,) if epi is None else epi(acc, *[e[...] for e in rest[:n_ex]])
        for o_ref, o in zip(rest[n_ex:], outs):
            o_ref[...] = o.astype(o_ref.dtype)

    tile = pl.BlockSpec((tm, tn), lambda n, m: (m, n))
    out = pl.pallas_call(
        body, name=name, grid=(N // tn, S // tm),
        in_specs=[pl.BlockSpec((tm, K), lambda n, m: (m, 0)), w_spec] + [tile] * n_ex,
        out_specs=[tile] * n_out,
        out_shape=[_sds((S, N), d) for d in out_dtypes],
        compiler_params=_params("parallel", "parallel"),
    )(x, w, *extras)
    return out[0] if n_out == 1 else out


def mm_nt(name, dy, w, *, extras=(), epi=None, tm=256):
    S, N = dy.shape
    K = w.shape[0]
    tm = min(tm, S)
    n_ex = len(extras)

    def body(dy_ref, w_ref, *rest):
        acc = lax.dot_general(dy_ref[...].astype(BF), w_ref[...], NT_DIMS, preferred_element_type=F32)
        if epi is not None:
            acc = epi(acc, *[e[...] for e in rest[:n_ex]])
        rest[n_ex][...] = acc.astype(BF)

    row = pl.BlockSpec((tm, K), lambda i: (i, 0))
    return pl.pallas_call(
        body, name=name, grid=(S // tm,),
        in_specs=[pl.BlockSpec((tm, N), lambda i: (i, 0)), pl.BlockSpec((K, N), lambda i: (0, 0))] + [row] * n_ex,
        out_specs=row, out_shape=_sds((S, K), BF), compiler_params=_params("parallel"),
    )(dy, w, *extras)


def mm_nt_rmsbwd(name, dy, w, h, g, dres, *, tm=256):
    S, N = dy.shape
    D = h.shape[1]
    tm = min(tm, S)
    blocked = w.ndim == 3
    nj = w.shape[2] if blocked else N

    def body(dy_ref, w_ref, h_ref, g_ref, dres_ref, dh_ref, dg_ref):
        i = pl.program_id(0)
        if blocked:
            dyn = None
            for j in range(w.shape[0]):
                part = lax.dot_general(dy_ref[:, j * nj:(j + 1) * nj].astype(BF), w_ref[j], NT_DIMS,
                                       preferred_element_type=F32)
                dyn = part if dyn is None else dyn + part
        else:
            dyn = lax.dot_general(dy_ref[...].astype(BF), w_ref[...], NT_DIMS, preferred_element_type=F32)
        x = h_ref[...]
        r = lax.rsqrt(jnp.mean(x * x, axis=-1, keepdims=True) + EPS)
        t = dyn * g_ref[...]
        dh_ref[...] = dres_ref[...] + r * t - x * (r * r * r) * jnp.mean(t * x, axis=-1, keepdims=True)
        part = jnp.sum(dyn * (x * r), axis=0, keepdims=True)

        @pl.when(i == 0)
        def _():
            dg_ref[...] = part

        @pl.when(i > 0)
        def _():
            dg_ref[...] += part

    w_spec = pl.BlockSpec(w.shape, (lambda i: (0, 0, 0)) if blocked else (lambda i: (0, 0)))
    row = pl.BlockSpec((tm, D), lambda i: (i, 0))
    vec = pl.BlockSpec((1, D), lambda i: (0, 0))
    return pl.pallas_call(
        body, name=name, grid=(S // tm,),
        in_specs=[pl.BlockSpec((tm, N), lambda i: (i, 0)), w_spec, row, vec, row],
        out_specs=[row, vec], out_shape=[_sds((S, D), F32), _sds((1, D), F32)],
        compiler_params=_params("arbitrary"),
    )(dy, w, h, g.reshape(1, D), dres)


def mm_tn(name, x, dy, *, pro_x=None, col_blocks=None, tk=1024, tn=1024):
    S, K = x.shape
    N = dy.shape[1]
    tk = min(tk, K)
    if col_blocks is not None:
        tn = N // col_blocks
        out_shape = _sds((col_blocks, K, tn), BF)
        out_spec = pl.BlockSpec((None, tk, tn), lambda k, n: (n, k, 0))
    else:
        tn = min(tn, N)
        out_shape = _sds((K, N), BF)
        out_spec = pl.BlockSpec((tk, tn), lambda k, n: (k, n))

    def body(x_ref, dy_ref, o_ref):
        xv = x_ref[...]
        if pro_x is not None:
            xv = pro_x(xv)
        o_ref[...] = lax.dot_general(xv.astype(BF), dy_ref[...].astype(BF), TN_DIMS,
                                     preferred_element_type=F32).astype(BF)

    return pl.pallas_call(
        body, name=name, grid=(K // tk, N // tn),
        in_specs=[pl.BlockSpec((S, tk), lambda k, n: (0, k)), pl.BlockSpec((S, tn), lambda k, n: (0, n))],
        out_specs=out_spec, out_shape=out_shape, compiler_params=_params("parallel", "parallel"),
    )(x, dy)


def _conv_parts(ac_ref, ah_ref, cw):
    a_c = ac_ref[...].astype(F32)
    a_h = ah_ref[...].astype(F32)
    x = a_c * a_h
    row = lax.broadcasted_iota(jnp.int32, x.shape, 0)
    x1 = jnp.where(row >= 1, pltpu.roll(x, 1, 0), 0.0)
    x2 = jnp.where(row >= 2, pltpu.roll(x, 2, 0), 0.0)
    cv = cw[0:1] * x2 + cw[1:2] * x1 + cw[2:3] * x
    return a_c, a_h, x, x1, x2, cv, row


def conv_fwd(name, proj, cw):
    S = proj.shape[0]

    def body(ab_ref, ac_ref, ah_ref, cw_ref, o_ref):
        cv = _conv_parts(ac_ref, ah_ref, cw_ref[...])[5]
        o_ref[...] = (ab_ref[...].astype(F32) * cv).astype(BF)

    col = lambda j: pl.BlockSpec((S, CONV_W), lambda i, j=j: (0, j))
    return pl.pallas_call(
        body, name=name, grid=(1,),
        in_specs=[col(0), col(1), col(2), pl.BlockSpec((3, CONV_W), lambda i: (0, 0))],
        out_specs=pl.BlockSpec((S, CONV_W), lambda i: (0, 0)),
        out_shape=_sds((S, CONV_W), BF), compiler_params=_params("arbitrary"),
    )(proj, proj, proj, cw)


def conv_bwd(name, dy, proj, cw):
    S = proj.shape[0]

    def body(dy_ref, ab_ref, ac_ref, ah_ref, cw_ref, dab_ref, dac_ref, dah_ref, dcw_ref):
        w = cw_ref[...]
        a_c, a_h, x, x1, x2, cv, row = _conv_parts(ac_ref, ah_ref, w)
        d = dy_ref[...].astype(F32)
        dab_ref[...] = (d * cv).astype(BF)
        dcv = d * ab_ref[...].astype(F32)
        d1 = jnp.where(row < S - 1, pltpu.roll(dcv, S - 1, 0), 0.0)
        d2 = jnp.where(row < S - 2, pltpu.roll(dcv, S - 2, 0), 0.0)
        dx = w[2:3] * dcv + w[1:2] * d1 + w[0:1] * d2
        dac_ref[...] = (dx * a_h).astype(BF)
        dah_ref[...] = (dx * a_c).astype(BF)
        dcw_ref[0:1, :] = jnp.sum(dcv * x2, axis=0, keepdims=True)
        dcw_ref[1:2, :] = jnp.sum(dcv * x1, axis=0, keepdims=True)
        dcw_ref[2:3, :] = jnp.sum(dcv * x, axis=0, keepdims=True)

    col = lambda j: pl.BlockSpec((S, CONV_W), lambda i, j=j: (0, j))
    one = pl.BlockSpec((S, CONV_W), lambda i: (0, 0))
    small = pl.BlockSpec((3, CONV_W), lambda i: (0, 0))
    return pl.pallas_call(
        body, name=name, grid=(1,),
        in_specs=[col(0), col(0), col(1), col(2), small],
        out_specs=[one, one, one, small],
        out_shape=[_sds((S, CONV_W), BF)] * 3 + [_sds((3, CONV_W), F32)],
        compiler_params=_params("arbitrary"),
    )(dy, proj, proj, proj, cw)


SGU_HEADS = SGU_W // HEAD_DIM
CU_BLOCK = 2304 // SGU_W
CV_BLOCK = 2560 // SGU_W


def _sgu_common(cu_ref, cv_ref, gv_ref, tm):
    c_u = cu_ref[...].astype(F32)
    c_v = cv_ref[...].astype(F32)
    hm = _head_mean_matrix(SGU_W)
    u = _gelu(c_u)
    vg = _gelu(c_v)
    r = lax.rsqrt(_head_mean(vg * vg, hm) + EPS)
    vv = (vg * r) * gv_ref[...]
    head = lax.broadcasted_iota(jnp.int32, (CHUNK, SGU_W), 1) // HEAD_DIM
    tri = (lax.broadcasted_iota(jnp.int32, (CHUNK, CHUNK), 0) >=
           lax.broadcasted_iota(jnp.int32, (CHUNK, CHUNK), 1))
    return c_u, c_v, hm, u, vg, r, vv, head, tri


def _sgu_mix(w_ref, tri, head, vvc, bias):
    sv = bias
    for g in range(SGU_HEADS):
        wg = jnp.where(tri, w_ref[g], 0.0).astype(BF)
        sv = sv + jnp.where(head == g, jnp.dot(wg, vvc, preferred_element_type=F32), 0.0)
    return sv


def sgu_fwd(name, proj, gv, w, bias):
    S = proj.shape[0]
    tm = min(S, 512)

    def body(cu_ref, cv_ref, gv_ref, w_ref, b_ref, o_ref):
        _, _, _, u, _, _, vv, head, tri = _sgu_common(cu_ref, cv_ref, gv_ref, tm)
        vvb = vv.astype(BF)
        for ch in range(tm // CHUNK):
            rows = slice(ch * CHUNK, (ch + 1) * CHUNK)
            sv = _sgu_mix(w_ref, tri, head, vvb[rows], b_ref[...])
            o_ref[rows, :] = (u[rows] * sv).astype(BF)

    const = lambda shape: pl.BlockSpec(shape, lambda i: (0,) * len(shape))
    return pl.pallas_call(
        body, name=name, grid=(S // tm,),
        in_specs=[pl.BlockSpec((tm, SGU_W), lambda i: (i, CU_BLOCK)), pl.BlockSpec((tm, SGU_W), lambda i: (i, CV_BLOCK)),
                  const((1, SGU_W)), const((SGU_HEADS, CHUNK, CHUNK)), const((CHUNK, SGU_W))],
        out_specs=pl.BlockSpec((tm, SGU_W), lambda i: (i, 0)),
        out_shape=_sds((S, SGU_W), BF), compiler_params=_params("parallel"),
    )(proj, proj, gv, w, bias)


def sgu_bwd(name, dy, proj, gv, w, bias):
    S = proj.shape[0]
    tm = min(S, 512)

    def body(dy_ref, cu_ref, cv_ref, gv_ref, w_ref, b_ref, dcu_ref, dcv_ref, dw_ref, db_ref, dgv_ref, dvv_s):
        i = pl.program_id(0)
        c_u, c_v, hm, u, vg, r, vv, head, tri = _sgu_common(cu_ref, cv_ref, gv_ref, tm)
        vvb = vv.astype(BF)
        d = dy_ref[...].astype(F32)
        ind = (lax.broadcasted_iota(jnp.int32, (SGU_W, LANES), 0) // HEAD_DIM ==
               lax.broadcasted_iota(jnp.int32, (SGU_W, LANES), 1)).astype(F32)
        dw_acc = [jnp.zeros((CHUNK, CHUNK), F32) for _ in range(SGU_HEADS)]
        db_acc = jnp.zeros((CHUNK, LANES), F32)
        for ch in range(tm // CHUNK):
            rows = slice(ch * CHUNK, (ch + 1) * CHUNK)
            sv = _sgu_mix(w_ref, tri, head, vvb[rows], b_ref[...])
            dcu_ref[rows, :] = (d[rows] * sv * _gelu_grad(c_u[rows])).astype(BF)
            dsv = d[rows] * u[rows]
            db_acc = db_acc + jnp.dot(dsv, ind, precision=HIGHEST, preferred_element_type=F32)
            dvv = jnp.zeros((CHUNK, SGU_W), F32)
            for g in range(SGU_HEADS):
                dsv_g = jnp.where(head == g, dsv, 0.0).astype(BF)
                wg = jnp.where(tri, w_ref[g], 0.0).astype(BF)
                dvv = dvv + lax.dot_general(wg, dsv_g, TN_DIMS, preferred_element_type=F32)
                dw_acc[g] = dw_acc[g] + lax.dot_general(dsv_g, vvb[rows], NT_DIMS, preferred_element_type=F32)
            dvv_s[rows, :] = dvv
        dvv = dvv_s[...]
        gvv = gv_ref[...]
        t = dvv * gvv
        dvg = r * t - vg * (r * r * r) * _head_mean(t * vg, hm)
        dcv_ref[...] = (dvg * _gelu_grad(c_v)).astype(BF)
        dgv = jnp.sum(dvv * (vg * r), axis=0, keepdims=True)

        @pl.when(i == 0)
        def _():
            for g in range(SGU_HEADS):
                dw_ref[g] = jnp.where(tri, dw_acc[g], 0.0)
            db_ref[...] = db_acc
            dgv_ref[...] = dgv

        @pl.when(i > 0)
        def _():
            for g in range(SGU_HEADS):
                dw_ref[g] += jnp.where(tri, dw_acc[g], 0.0)
            db_ref[...] += db_acc
            dgv_ref[...] += dgv

    const = lambda shape: pl.BlockSpec(shape, lambda i: (0,) * len(shape))
    tile = pl.BlockSpec((tm, SGU_W), lambda i: (i, 0))
    return pl.pallas_call(
        body, name=name, grid=(S // tm,),
        in_specs=[pl.BlockSpec((tm, SGU_W), lambda i: (i, 3)),
                  pl.BlockSpec((tm, SGU_W), lambda i: (i, CU_BLOCK)), pl.BlockSpec((tm, SGU_W), lambda i: (i, CV_BLOCK)),
                  const((1, SGU_W)), const((SGU_HEADS, CHUNK, CHUNK)), const((CHUNK, SGU_W))],
        out_specs=[tile, tile, const((SGU_HEADS, CHUNK, CHUNK)), const((CHUNK, LANES)), const((1, SGU_W))],
        out_shape=[_sds((S, SGU_W), BF), _sds((S, SGU_W), BF), _sds((SGU_HEADS, CHUNK, CHUNK), F32),
                   _sds((CHUNK, LANES), F32), _sds((1, SGU_W), F32)],
        scratch_shapes=[pltpu.VMEM((tm, SGU_W), F32)],
        compiler_params=_params("arbitrary"),
    )(dy, proj, proj, gv, w, bias)


HEAD_PAIRS = ATTN_W // LANES
Q_BLOCK0 = 768 // LANES
K_BLOCK0 = 1280 // LANES
V_BLOCK0 = 1792 // LANES


def _attn_tile(S):
    return min(S, 256)


def _qk_norm(x, g, hm):
    r = lax.rsqrt(_head_mean(x * x, hm) + EPS)
    return r, (x * r) * g


MASKED = -1e30


def _logit_parts(qh, kb, bias):
    z = lax.dot_general(qh, kb, NT_DIMS, preferred_element_type=F32)
    if bias is not None:
        z = z + bias
    lb = _log_sigmoid(z)
    lr = lb - z
    hi = lr.astype(BF)
    return lb, hi, (lr - hi.astype(F32)).astype(BF)


def _stack_heads(x, lane):
    return jnp.concatenate([jnp.where(lane < HEAD_DIM, x, 0.0), jnp.where(lane >= HEAD_DIM, x, 0.0)],
                           axis=0).astype(BF)


def _dot2_stacked(hi, lo, u):
    rows = hi.shape[0]
    both = jnp.dot(jnp.concatenate([hi, lo], axis=0), u, preferred_element_type=F32)
    return both[:rows] + both[rows:]


def attn_fwd(name, proj, gq, gk):
    S = proj.shape[0]
    T = _attn_tile(S)
    nq = S // T

    def body(q_ref, k_ref, v_ref, gq_ref, gk_ref, o_ref, tot_ref, kn_s):
        qi = pl.program_id(1)
        hm = _head_mean_matrix(LANES)

        @pl.when(qi == 0)
        def _():
            kn_s[...] = _qk_norm(k_ref[...].astype(F32), gk_ref[...], hm)[1].astype(BF)

        qn = _qk_norm(q_ref[...].astype(F32), gq_ref[...], hm)[1]
        lane = lax.broadcasted_iota(jnp.int32, (T, LANES), 1)
        qst = _stack_heads(qn, lane)
        rowi = lax.broadcasted_iota(jnp.int32, (T, T), 0)
        coli = lax.broadcasted_iota(jnp.int32, (T, T), 1)
        u_excl = (rowi > coli).astype(BF)
        diagonal = jnp.where(coli < rowi, 0.0, MASKED)

        def block(j, carry, bias):
            o, run = carry
            off = pl.multiple_of(j * T, T)
            lb, hi, lo = _logit_parts(qst, kn_s[pl.ds(off, T), :], bias)
            later = _dot2_stacked(hi, lo, u_excl)
            a = jnp.exp(lb + later + run)
            first = hi[:, 0:1].astype(F32) + lo[:, 0:1].astype(F32)
            vb = v_ref[pl.ds(off, T), :].astype(BF)
            return o + jnp.dot(a.astype(BF), vb, preferred_element_type=F32), run + later[:, 0:1] + first

        res = block(qi, (jnp.zeros((2 * T, LANES), F32), jnp.zeros((2 * T, 1), F32)),
                    jnp.concatenate([diagonal, diagonal], axis=0))
        o, run = lax.fori_loop(0, qi, lambda it, carry: block(qi - 1 - it, carry, None), res)
        o_ref[...] = jnp.where(lane < HEAD_DIM, o[:T], o[T:]).astype(BF)
        tot_ref[...] = jnp.where(lane < HEAD_DIM, run[:T], run[T:])

    gain = pl.BlockSpec((1, LANES), lambda hp, qi: (0, 0))
    full = lambda b0: pl.BlockSpec((S, LANES), lambda hp, qi, b0=b0: (0, b0 + hp))
    tile = pl.BlockSpec((T, LANES), lambda hp, qi: (qi, hp))
    return pl.pallas_call(
        body, name=name, grid=(HEAD_PAIRS, nq),
        in_specs=[pl.BlockSpec((T, LANES), lambda hp, qi: (qi, Q_BLOCK0 + hp)), full(K_BLOCK0), full(V_BLOCK0), gain, gain],
        out_specs=[tile, tile],
        out_shape=[_sds((S, ATTN_W), BF), _sds((S, ATTN_W), F32)],
        scratch_shapes=[pltpu.VMEM((S, LANES), BF)],
        compiler_params=_params("arbitrary", "arbitrary"),
    )(proj, proj, proj, gq, gk)


def attn_bwd(name, dy, proj, tot, gq, gk):
    S = proj.shape[0]
    T = _attn_tile(S)
    nq = S // T

    def body(q_ref, k_ref, v_ref, tot_ref, do_ref, gq_ref, gk_ref,
             dq_ref, dk_ref, dv_ref, dgq_ref, dgk_ref, kn_s, dkn_s, dv_s):
        hp = pl.program_id(0)
        qi = pl.program_id(1)
        hm = _head_mean_matrix(LANES)

        @pl.when(qi == 0)
        def _():
            kn_s[...] = _qk_norm(k_ref[...].astype(F32), gk_ref[...], hm)[1].astype(BF)
            dkn_s[...] = jnp.zeros_like(dkn_s)
            dv_s[...] = jnp.zeros_like(dv_s)

        q = q_ref[...].astype(F32)
        rq, qn = _qk_norm(q, gq_ref[...], hm)
        lane = lax.broadcasted_iota(jnp.int32, (T, LANES), 1)
        qst = _stack_heads(qn, lane)
        dost = _stack_heads(do_ref[...].astype(F32), lane)
        total = jnp.concatenate([tot_ref[:, 0:1], tot_ref[:, HEAD_DIM:HEAD_DIM + 1]], axis=0)
        rowi = lax.broadcasted_iota(jnp.int32, (T, T), 0)
        coli = lax.broadcasted_iota(jnp.int32, (T, T), 1)
        u_upto = (rowi <= coli).astype(BF)
        u_before = (rowi < coli).astype(BF)
        diagonal = jnp.where(coli < rowi, 0.0, MASKED)

        def block(j, carry, bias):
            dq, run, grun = carry
            off = pl.multiple_of(j * T, T)
            kb = kn_s[pl.ds(off, T), :]
            vb = v_ref[pl.ds(off, T), :].astype(BF)
            lb, hi, lo = _logit_parts(qst, kb, bias)
            upto = _dot2_stacked(hi, lo, u_upto)
            a = jnp.exp(lb + (total - run - upto))
            g = lax.dot_general(dost, vb, NT_DIMS, preferred_element_type=F32) * a
            dv_s[pl.ds(off, T), :] += lax.dot_general(a.astype(BF), dost, TN_DIMS, preferred_element_type=F32)
            ghi = g.astype(BF)
            before = _dot2_stacked(ghi, (g - ghi.astype(F32)).astype(BF), u_before)
            sig = jnp.exp(lb)
            dz = (g - sig * (g + (grun + before))).astype(BF)
            dkn_s[pl.ds(off, T), :] += lax.dot_general(dz, qst, TN_DIMS, preferred_element_type=F32)
            return (dq + jnp.dot(dz, kb, preferred_element_type=F32), run + upto[:, T - 1:T],
                    grun + before[:, T - 1:T] + g[:, T - 1:T])

        zero = (jnp.zeros((2 * T, LANES), F32), jnp.zeros((2 * T, 1), F32), jnp.zeros((2 * T, 1), F32))
        res = lax.fori_loop(0, qi, lambda j, carry: block(j, carry, None), zero)
        res = block(qi, res, jnp.concatenate([diagonal, diagonal], axis=0))
        dqn = jnp.where(lane < HEAD_DIM, res[0][:T], res[0][T:])
        gq_v = gq_ref[...]
        t = dqn * gq_v
        dq_ref[...] = (rq * t - q * (rq * rq * rq) * _head_mean(t * q, hm)).astype(BF)
        dgq = jnp.sum(dqn * (q * rq), axis=0, keepdims=True) * SCALE
        first = jnp.logical_and(hp == 0, qi == 0)

        @pl.when(first)
        def _():
            dgq_ref[...] = dgq

        @pl.when(jnp.logical_not(first))
        def _():
            dgq_ref[...] += dgq

        @pl.when(qi == nq - 1)
        def _():
            k = k_ref[...].astype(F32)
            rk = _qk_norm(k, gk_ref[...], hm)[0]
            dkn = dkn_s[...]
            tk = dkn * gk_ref[...]
            dk_ref[...] = (rk * tk - k * (rk * rk * rk) * _head_mean(tk * k, hm)).astype(BF)
            dgk = jnp.sum(dkn * (k * rk), axis=0, keepdims=True)
            dv_ref[...] = dv_s[...].astype(BF)

            @pl.when(hp == 0)
            def _():
                dgk_ref[...] = dgk

            @pl.when(hp > 0)
            def _():
                dgk_ref[...] += dgk

            @pl.when(hp == HEAD_PAIRS - 1)
            def _():
                fold = (lax.broadcasted_iota(jnp.int32, (LANES, LANES), 0) % HEAD_DIM ==
                        lax.broadcasted_iota(jnp.int32, (LANES, LANES), 1) % HEAD_DIM).astype(F32)
                dgq_ref[...] = jnp.dot(dgq_ref[...], fold, precision=HIGHEST, preferred_element_type=F32)
                dgk_ref[...] = jnp.dot(dgk_ref[...], fold, precision=HIGHEST, preferred_element_type=F32)

    gain = pl.BlockSpec((1, LANES), lambda hp, qi: (0, 0))
    full = lambda b0: pl.BlockSpec((S, LANES), lambda hp, qi, b0=b0: (0, b0 + hp))
    tile = pl.BlockSpec((T, LANES), lambda hp, qi: (qi, hp))
    col = pl.BlockSpec((S, LANES), lambda hp, qi: (0, hp))
    dgain = pl.BlockSpec((1, LANES), lambda hp, qi: (0, 0))
    return pl.pallas_call(
        body, name=name, grid=(HEAD_PAIRS, nq),
        in_specs=[pl.BlockSpec((T, LANES), lambda hp, qi: (qi, Q_BLOCK0 + hp)), full(K_BLOCK0), full(V_BLOCK0),
                  tile, pl.BlockSpec((T, LANES), lambda hp, qi: (qi, 2 + hp)), gain, gain],
        out_specs=[tile, col, col, dgain, dgain],
        out_shape=[_sds((S, ATTN_W), BF)] * 3 + [_sds((1, LANES), F32)] * 2,
        scratch_shapes=[pltpu.VMEM((S, LANES), BF), pltpu.VMEM((S, LANES), F32), pltpu.VMEM((S, LANES), F32)],
        compiler_params=_params("arbitrary", "arbitrary"),
    )(proj, proj, proj, tot, dy, gq, gk)


def ple_bwd_elem(name, dh, gp, pp, after=()):
    S, D = dh.shape
    tm = min(S, 512)

    def body(dh_ref, gp_ref, pp_ref, *rest):
        dgp_ref, dpp_ref = rest[-2:]
        d = dh_ref[...]
        gate = jax.nn.sigmoid(gp_ref[...].astype(F32))
        dpp_ref[...] = (d * gate).astype(BF)
        dgp_ref[...] = (d * pp_ref[...].astype(F32) * gate * (1.0 - gate)).astype(BF)

    tile = pl.BlockSpec((tm, D), lambda i: (i, 0))
    return pl.pallas_call(
        body, name=name, grid=(S // tm,), in_specs=[tile] * 3 + [pl.BlockSpec(memory_space=pl.ANY)] * len(after),
        out_specs=[tile] * 2, out_shape=[_sds((S, D), BF)] * 2, compiler_params=_params("parallel"),
    )(dh, gp, pp, *after)


def loss_head(name, h, target):
    S, D = h.shape
    tm = min(S, 512)

    def body(h_ref, t_ref, loss_ref, dh_ref):
        i = pl.program_id(0)
        e = h_ref[...] - t_ref[...]
        dh_ref[...] = e * (1.0 / D)
        part = jnp.zeros((8, LANES), F32) + 0.5 * jnp.sum(jnp.mean(e * e, axis=-1, keepdims=True))

        @pl.when(i == 0)
        def _():
            loss_ref[...] = part

        @pl.when(i > 0)
        def _():
            loss_ref[...] += part

    tile = pl.BlockSpec((tm, D), lambda i: (i, 0))
    return pl.pallas_call(
        body, name=name, grid=(S // tm,), in_specs=[tile, tile],
        out_specs=[pl.BlockSpec((8, LANES), lambda i: (0, 0)), tile],
        out_shape=[_sds((8, LANES), F32), _sds((S, D), F32)], compiler_params=_params("arbitrary"),
    )(h, target)


def _adamw_math(w, g, m, v):
    c1 = 1.0 - ADAM_B1 ** ADAM_STEP
    c2 = 1.0 - ADAM_B2 ** ADAM_STEP
    nm = ADAM_B1 * m + (1.0 - ADAM_B1) * g
    nv = ADAM_B2 * v + (1.0 - ADAM_B2) * (g * g)
    return -ADAM_LR * ((nm / c1) / (jnp.sqrt(nv / c2) + ADAM_EPS) + ADAM_WD * w), nm, nv


def adamw(name, w, g, m, v):
    R, C = w.shape
    tr = R
    for cand in (512, 256, 128, 64, 32, 16, 8):
        if R % cand == 0:
            tr = cand
            break

    def body(w_ref, g_ref, m_ref, v_ref, d_ref, nm_ref, nv_ref):
        d_ref[...], nm_ref[...], nv_ref[...] = _adamw_math(w_ref[...], g_ref[...], m_ref[...], v_ref[...])

    tile = pl.BlockSpec((tr, C), lambda i: (i, 0))
    return pl.pallas_call(
        body, name=name, grid=(R // tr,), in_specs=[tile] * 4, out_specs=[tile] * 3,
        out_shape=[_sds((R, C), F32)] * 3, compiler_params=_params("parallel"),
    )(w, g, m, v)


def adamw_layer(name, layer, ws, gs, ms, vs, prev):
    n = len(ws)
    steps = 8

    def body(*refs):
        ins, outs = refs[:4 * n + (0 if prev is None else 4 * n)], refs[-4 * n:]
        for i in range(n):
            w_ref, g_ref, m_ref, v_ref = (ins[k * n + i] for k in range(4))
            g = g_ref[...]
            outs[i][...] = g
            outs[n + i][...], outs[2 * n + i][...], outs[3 * n + i][...] = _adamw_math(w_ref[...], g, m_ref[...],
                                                                                        v_ref[...])

    def stacked(a):
        return pl.BlockSpec((None, a.shape[1] // steps, a.shape[2]), lambda t: (layer, t, 0))

    def flat(a):
        return pl.BlockSpec((a.shape[0] // steps, a.shape[1]), lambda t: (t, 0))

    in_specs = [stacked(a) for a in ws] + [flat(a) for a in gs] + [stacked(a) for a in ms] + [stacked(a) for a in vs]
    operands = [*ws, *gs, *ms, *vs]
    aliases = {}
    if prev is not None:
        flat_prev = [a for group in prev for a in group]
        in_specs += [pl.BlockSpec(memory_space=pl.ANY)] * len(flat_prev)
        aliases = {4 * n + i: i for i in range(4 * n)}
        operands += flat_prev
    out = pl.pallas_call(
        body, name=name, grid=(steps,), in_specs=in_specs, out_specs=[stacked(a) for a in ws] * 4,
        out_shape=[_sds(a.shape, F32) for a in ws] * 4, input_output_aliases=aliases,
        compiler_params=_params("parallel"),
    )(*operands)
    return [list(out[k * n:(k + 1) * n]) for k in range(4)]


def _relu2(u):
    r = jnp.maximum(u.astype(F32), 0.0)
    return r * r


def layer_fwd(tag, h0, p_bf, wt, after=()):
    hn1 = rmsnorm(f"{tag}_norm1", h0, wt["norm1_g"], after)
    proj = mm_nn(f"{tag}_proj", hn1, wt["w_in"], tn=1408)
    ya = conv_fwd(f"{tag}_conv", proj, wt["conv_w"])
    yb, yb_tot = attn_fwd(f"{tag}_attn", proj, wt["gq"], wt["gk"])
    yc = sgu_fwd(f"{tag}_sgu", proj, wt["gv"], wt["sgu_w"], wt["sgu_bias"])
    y = jnp.concatenate([ya, yb, yc], axis=-1)
    h1 = mm_nn(f"{tag}_out", y, wt["w_out"], extras=(h0,), epi=lambda acc, h: (h + acc,), out_dtypes=(F32,))
    hn2 = rmsnorm(f"{tag}_norm2", h1, wt["norm2_g"])
    uu = mm_nn(f"{tag}_ff1", hn2, wt["w_ff1"])
    h2 = mm_nn(f"{tag}_ff2", uu, wt["w_ff2"], pro=_relu2, extras=(h1,), epi=lambda acc, h: (h + acc,),
               out_dtypes=(F32,))
    hn3 = rmsnorm(f"{tag}_norm3", h2, wt["norm3_g"])
    gp = mm_nn(f"{tag}_gate", hn3, wt["w_ple_gate"])
    h3, pp = mm_nn(f"{tag}_ple", p_bf, wt["w_ple_proj"], extras=(gp, h2),
                   epi=lambda acc, g, h: (h + jax.nn.sigmoid(g.astype(F32)) * acc, acc), out_dtypes=(F32, BF))
    saved = dict(h0=h0, h1=h1, h2=h2, hn1=hn1, hn2=hn2, hn3=hn3, proj=proj, yb_tot=yb_tot, y=y, uu=uu, gp=gp, pp=pp,
                 p_bf=p_bf)
    return h3, saved


def layer_bwd(tag, dh3, sv, wt, after=()):
    dgp, dpp = ple_bwd_elem(f"{tag}_dple", dh3, sv["gp"], sv["pp"], after)
    g = {}
    g["w_ple_proj"] = mm_tn(f"{tag}_dwp", sv["p_bf"], dpp, col_blocks=N_CHIPS)
    g["w_ple_gate"] = mm_tn(f"{tag}_dwg", sv["hn3"], dgp)
    dh2, g["norm3_g"] = mm_nt_rmsbwd(f"{tag}_dnorm3", dgp, wt["w_ple_gate"], sv["h2"], wt["norm3_g"], dh3)

    duu = mm_nt(f"{tag}_dff2", dh2, wt["w_ff2"], extras=(sv["uu"],),
                epi=lambda acc, u: acc * (2.0 * jnp.maximum(u.astype(F32), 0.0)))
    g["w_ff2"] = mm_tn(f"{tag}_dw2", sv["uu"], dh2, pro_x=_relu2)
    g["w_ff1"] = mm_tn(f"{tag}_dw1", sv["hn2"], duu, col_blocks=N_CHIPS)
    dh1, g["norm2_g"] = mm_nt_rmsbwd(f"{tag}_dnorm2", duu, wt["w_ff1"], sv["h1"], wt["norm2_g"], dh2)

    dy = mm_nt(f"{tag}_dout", dh1, wt["w_out"])
    g["w_out"] = mm_tn(f"{tag}_dwo", sv["y"], dh1)
    dab, dac, dah, g["conv_w"] = conv_bwd(f"{tag}_dconv", dy, sv["proj"], wt["conv_w"])
    dq, dk, dv, g["gq"], g["gk"] = attn_bwd(f"{tag}_dattn", dy, sv["proj"], sv["yb_tot"], wt["gq"], wt["gk"])
    dcu, dcv, g["sgu_w"], g["sgu_bias"], g["gv"] = sgu_bwd(f"{tag}_dsgu", dy, sv["proj"], wt["gv"], wt["sgu_w"],
                                                           wt["sgu_bias"])
    dproj = jnp.concatenate([dab, dac, dah, dq, dk, dv, dcu, dcv], axis=-1)
    g["w_in"] = mm_tn(f"{tag}_dwi", sv["hn1"], dproj, tn=1408)
    dh0, g["norm1_g"] = mm_nt_rmsbwd(f"{tag}_dnorm1", dproj, wt["w_in"], sv["h0"], wt["norm1_g"], dh1)
    return dh0, g


def prep_small(norm1_g, q_norm_g, k_norm_g, sgu_norm_g, sgu_w, sgu_b, norm2_g, norm3_g, conv_w_full):
    return dict(
        norm1_g=norm1_g, norm2_g=norm2_g, norm3_g=norm3_g, conv_w=conv_w_full,
        gq=(jnp.tile(q_norm_g, 2) * SCALE).reshape(1, LANES), gk=jnp.tile(k_norm_g, 2).reshape(1, LANES),
        gv=sgu_norm_g.reshape(1, SGU_W), sgu_w=sgu_w, sgu_bias=jnp.repeat(sgu_b.T, HEAD_DIM, axis=1))


def small_grads(g):
    return dict(
        norm1_g=g["norm1_g"][0], norm2_g=g["norm2_g"][0], norm3_g=g["norm3_g"][0], conv_w=g["conv_w"],
        q_norm_g=g["gq"][0, :HEAD_DIM], k_norm_g=g["gk"][0, :HEAD_DIM], sgu_norm_g=g["gv"][0], sgu_w=g["sgu_w"],
        sgu_b=g["sgu_bias"][:, :SGU_HEADS].T)


HBM_SPEC = pl.BlockSpec(memory_space=pltpu.HBM)
BIG = ("w_in", "w_out", "w_ff1", "w_ff2", "w_ple_gate", "w_ple_proj")


def _mesh_pos():
    return lax.axis_index("x"), lax.axis_index("y"), lax.axis_index("c")


def _other_chips(x, y):
    return [(1 - x, y), (x, 1 - y), (1 - x, 1 - y)]


def _half(rows, core):
    h = rows // 2
    return pl.ds(pl.multiple_of(core * h, 16), h)


def _remote(src, dst, send_sems, recv_sems, k, to):
    return pltpu.make_async_remote_copy(src_ref=src, dst_ref=dst, send_sem=send_sems.at[k], recv_sem=recv_sems.at[k],
                                        device_id=to, device_id_type=MESH)


SEM_SPEC = pl.BlockSpec(memory_space=pltpu.SEMAPHORE)
ANY_SPEC = pl.BlockSpec(memory_space=pl.ANY)
SIDE_EFFECT = pltpu.SideEffectType.DATAFLOW_SIDE_EFFECTING


def _in_hbm(arrays):
    return [pltpu.with_memory_space_constraint(a, pltpu.HBM) for a in arrays]


def copies_start(name, srcs, lands, plan, after=()):
    n, na = len(srcs), len(after)

    def body(*refs):
        src_refs, land_refs = refs[:n], refs[n:2 * n]
        send_sem, recv_sem = refs[2 * n + na], refs[2 * n + na + 1]
        token = refs[-1]
        x, y, c = _mesh_pos()
        for i in range(n):
            for src, dst, dev in plan(i, src_refs[i], land_refs[i], x, y, c):
                pltpu.make_async_remote_copy(src_ref=src, dst_ref=dst, send_sem=send_sem, recv_sem=recv_sem,
                                             device_id=dev, device_id_type=MESH).start()
        token[...] = jnp.zeros_like(token)

    out = pl.pallas_call(
        body, name=name,
        in_specs=[HBM_SPEC] * (2 * n) + [ANY_SPEC] * na,
        out_specs=(SEM_SPEC, SEM_SPEC, *[HBM_SPEC] * (2 * n), pl.BlockSpec(memory_space=pltpu.VMEM)),
        out_shape=(pltpu.SemaphoreType.DMA(()), pltpu.SemaphoreType.DMA(()),
                   *[pltpu.HBM(a.shape, a.dtype) for a in (*srcs, *lands)], _sds((8, LANES), F32)),
        input_output_aliases={i: 2 + i for i in range(2 * n)},
        compiler_params=pltpu.CompilerParams(has_side_effects=SIDE_EFFECT),
    )(*_in_hbm(srcs), *_in_hbm(lands), *after)
    return out[0], out[1], list(out[2:2 + n]), list(out[2 + n:2 + 2 * n]), out[-1]


def copies_wait(name, started, plan, after=()):
    send_sem, recv_sem, srcs, lands, _ = started
    n, na = len(srcs), len(after)

    def body(*refs):
        src_refs, land_refs = refs[:n], refs[n:2 * n]
        send_sem, recv_sem = refs[2 * n], refs[2 * n + 1]
        x, y, c = _mesh_pos()
        for i in range(n):
            for src, dst, dev in plan(i, src_refs[i], land_refs[i], x, y, c):
                cp = pltpu.make_async_remote_copy(src_ref=src, dst_ref=dst, send_sem=send_sem, recv_sem=recv_sem,
                                                  device_id=dev, device_id_type=MESH)
                cp.wait_send()
                cp.wait_recv()

    out = pl.pallas_call(
        body, name=name,
        in_specs=[HBM_SPEC] * (2 * n) + [SEM_SPEC, SEM_SPEC] + [ANY_SPEC] * na,
        out_specs=[HBM_SPEC] * (2 * n),
        out_shape=[pltpu.HBM(a.shape, a.dtype) for a in (*srcs, *lands)],
        input_output_aliases={i: i for i in range(2 * n)},
        compiler_params=pltpu.CompilerParams(has_side_effects=SIDE_EFFECT),
    )(*srcs, *lands, send_sem, recv_sem, *after)
    return list(out[:n]), list(out[n:])


def _gather_plan(shapes):
    def plan(i, src, land, x, y, c):
        me = 2 * x + y
        rows = _half(shapes[i][0], c)
        return [(src.at[rows], land.at[me, rows], (*chip, c)) for chip in _other_chips(x, y)]
    return plan


def _gather_arrivals(shapes):
    def plan(i, src, land, x, y, c):
        rows = _half(shapes[i][0], c)
        return [(src.at[rows], land.at[2 * chip[0] + chip[1], rows], (*chip, c)) for chip in _other_chips(x, y)]
    return plan


def gather_finish(name, shards, lands):
    n = len(shards)

    def body(*refs):
        ins, outs = refs[:n], refs[2 * n:3 * n]
        send_sems, recv_sems = refs[3 * n:]
        x, y, c = _mesh_pos()
        me = 2 * x + y
        chips = _other_chips(x, y)
        sibling = (x, y, 1 - c)
        sends = []
        for i in range(n):
            rows = shards[i].shape[0]
            cp = _remote(ins[i], outs[i].at[me], send_sems, recv_sems, 4 * i + 3, sibling)
            cp.start()
            sends.append(cp)
            for k, chip in enumerate(chips):
                region = outs[i].at[2 * chip[0] + chip[1], _half(rows, c)]
                cp = _remote(region, region, send_sems, recv_sems, 4 * i + k, sibling)
                cp.start()
                sends.append(cp)
        for i in range(n):
            rows = shards[i].shape[0]
            _remote(ins[i], outs[i].at[me], send_sems, recv_sems, 4 * i + 3, sibling).wait_recv()
            for k, chip in enumerate(chips):
                region = outs[i].at[2 * chip[0] + chip[1], _half(rows, 1 - c)]
                _remote(region, region, send_sems, recv_sems, 4 * i + k, sibling).wait_recv()
        for cp in sends:
            cp.wait_send()

    return pl.pallas_call(
        body, name=name, in_specs=[HBM_SPEC] * (2 * n), out_specs=[HBM_SPEC] * n,
        out_shape=[_sds(a.shape, a.dtype) for a in lands],
        input_output_aliases={n + i: i for i in range(n)},
        scratch_shapes=[pltpu.SemaphoreType.DMA((4 * n,)), pltpu.SemaphoreType.DMA((4 * n,))],
    )(*shards, *lands)


def exchange_other_half(name, grads):
    n = len(grads)

    def body(*refs):
        ins, outs = refs[:n], refs[n:2 * n]
        send_sems, recv_sems = refs[2 * n:]
        x, y, c = _mesh_pos()
        copies = []
        for i in range(n):
            rows = grads[i].shape[1]
            cp = _remote(ins[i].at[:, _half(rows, 1 - c)], outs[i], send_sems, recv_sems, i, (x, y, 1 - c))
            cp.start()
            copies.append(cp)
        for cp in copies:
            cp.wait()

    return pl.pallas_call(
        body, name=name, in_specs=[HBM_SPEC] * n, out_specs=[HBM_SPEC] * n,
        out_shape=[_sds((N_CHIPS, g.shape[1] // 2, g.shape[2]), g.dtype) for g in grads],
        scratch_shapes=[pltpu.SemaphoreType.DMA((n,)), pltpu.SemaphoreType.DMA((n,))],
    )(*grads)


def add_own_half(name, core, grads, got):
    n = len(grads)

    def body(core_ref, *refs):
        for i in range(n):
            refs[2 * n + i][...] = (refs[i][...].astype(F32) + refs[n + i][...].astype(F32)).astype(BF)

    def spec(g, own):
        blk = (None, g.shape[1] // 2, g.shape[2])
        return pl.BlockSpec(blk, (lambda j, core_ref: (j, core_ref[0], 0)) if own else (lambda j, core_ref: (j, 0, 0)))

    return pl.pallas_call(
        body, name=name,
        grid_spec=pltpu.PrefetchScalarGridSpec(
            num_scalar_prefetch=1, grid=(N_CHIPS,),
            in_specs=[spec(g, True) for g in grads] + [spec(g, False) for g in grads],
            out_specs=[spec(g, False) for g in grads]),
        out_shape=[_sds(r.shape, BF) for r in got], compiler_params=_params("parallel"),
    )(core, *grads, *got)


def _chips_plan(i, src, land, x, y, c):
    return [(src.at[2 * chip[0] + chip[1]], land.at[k], (*chip, c)) for k, chip in enumerate(_other_chips(x, y))]


def sum_chips(name, place, parts, got):
    n = len(got)

    def body(place_ref, *refs):
        for i in range(n):
            acc = refs[i][...].astype(F32)
            for k in range(N_CHIPS - 1):
                acc = acc + refs[n + i][k].astype(F32)
            refs[2 * n + i][...] = acc

    steps = 2
    return pl.pallas_call(
        body, name=name,
        grid_spec=pltpu.PrefetchScalarGridSpec(
            num_scalar_prefetch=1, grid=(steps,),
            in_specs=[pl.BlockSpec((None, g.shape[1] // steps, g.shape[2]), lambda t, place_ref: (place_ref[0], t, 0))
                      for g in parts] +
                     [pl.BlockSpec((N_CHIPS - 1, g.shape[1] // steps, g.shape[2]), lambda t, place_ref: (0, t, 0))
                      for g in got],
            out_specs=[pl.BlockSpec((g.shape[1] // steps, g.shape[2]),
                                    lambda t, place_ref: (place_ref[1] * steps + t, 0)) for g in got]),
        out_shape=[_sds((2 * g.shape[1], g.shape[2]), F32) for g in got], compiler_params=_params("parallel"),
    )(place, *parts, *got)


def exchange_final_halves(name, shards):
    n = len(shards)

    def body(*refs):
        outs = refs[n:2 * n]
        send_sems, recv_sems = refs[2 * n:]
        x, y, c = _mesh_pos()
        copies = []
        for i in range(n):
            mine = outs[i].at[_half(shards[i].shape[0], c)]
            cp = _remote(mine, mine, send_sems, recv_sems, i, (x, y, 1 - c))
            cp.start()
            copies.append(cp)
        for i, cp in enumerate(copies):
            cp.wait_send()
            theirs = outs[i].at[_half(shards[i].shape[0], 1 - c)]
            _remote(theirs, theirs, send_sems, recv_sems, i, (x, y, 1 - c)).wait_recv()

    return pl.pallas_call(
        body, name=name, in_specs=[HBM_SPEC] * n, out_specs=[HBM_SPEC] * n,
        out_shape=[_sds(a.shape, a.dtype) for a in shards], input_output_aliases={i: i for i in range(n)},
        scratch_shapes=[pltpu.SemaphoreType.DMA((n,)), pltpu.SemaphoreType.DMA((n,))],
    )(*shards)


def reduce_scatter_begin(tag, core, grads):
    got = exchange_other_half(f"{tag}_rs_pair", grads)
    parts = add_own_half(f"{tag}_rs_add", core, grads, got)
    lands = [lax.empty((N_CHIPS - 1,) + p.shape[1:], p.dtype) for p in parts]
    return copies_start(f"{tag}_rs_start", parts, lands, _chips_plan)


def reduce_scatter_end(tag, place, started, after):
    parts, got = copies_wait(f"{tag}_rs_wait", started, _chips_plan, after)
    return exchange_final_halves(f"{tag}_rs_join", sum_chips(f"{tag}_rs_sum", place, parts, got))


def small_allreduce(name, x):
    R = x.shape[0]
    H = R // 2

    def body(x_ref, o_ref, pair_ref, chip_ref, send_sems, recv_sems):
        xx, yy, c = _mesh_pos()
        me = 2 * xx + yy
        chips = _other_chips(xx, yy)
        sibling = (xx, yy, 1 - c)
        mine = pl.ds(pl.multiple_of(c * H, 8), H)
        theirs = pl.ds(pl.multiple_of((1 - c) * H, 8), H)
        a = _remote(x_ref.at[theirs], pair_ref.at[theirs], send_sems, recv_sems, 0, sibling)
        a.start()
        a.wait_send()
        _remote(x_ref.at[mine], pair_ref.at[mine], send_sems, recv_sems, 0, sibling).wait_recv()
        chip_ref[me] = x_ref[mine, :] + pair_ref[mine, :]
        sends = []
        for k, chip in enumerate(chips):
            cp = _remote(chip_ref.at[me], chip_ref.at[me], send_sems, recv_sems, 1 + k, (*chip, c))
            cp.start()
            sends.append(cp)
        for k, chip in enumerate(chips):
            slot = chip_ref.at[2 * chip[0] + chip[1]]
            _remote(slot, slot, send_sems, recv_sems, 1 + k, (*chip, c)).wait_recv()
        o_ref[mine, :] = (chip_ref[0] + chip_ref[1]) + (chip_ref[2] + chip_ref[3])
        b = _remote(o_ref.at[mine], o_ref.at[mine], send_sems, recv_sems, 4, sibling)
        b.start()
        b.wait_send()
        _remote(o_ref.at[theirs], o_ref.at[theirs], send_sems, recv_sems, 4, sibling).wait_recv()
        for cp in sends:
            cp.wait_send()

    return pl.pallas_call(
        body, name=name,
        in_specs=[pl.BlockSpec(memory_space=pltpu.VMEM)], out_specs=pl.BlockSpec(memory_space=pltpu.VMEM),
        out_shape=_sds((R, LANES), F32),
        scratch_shapes=[pltpu.VMEM((R, LANES), F32), pltpu.VMEM((N_CHIPS, H, LANES), F32),
                        pltpu.SemaphoreType.DMA((5,)), pltpu.SemaphoreType.DMA((5,))],
        compiler_params=pltpu.CompilerParams(vmem_limit_bytes=VMEM_LIMIT),
    )(x)


WEIGHTS = ("norm1_g", "w_in", "conv_w", "q_norm_g", "k_norm_g", "sgu_norm_g", "sgu_w", "sgu_b", "w_out", "norm2_g",
           "w_ff1", "w_ff2", "norm3_g", "w_ple_gate", "w_ple_proj")
SMALL = ("norm1_g", "norm2_g", "norm3_g", "q_norm_g", "k_norm_g", "sgu_norm_g", "sgu_w", "sgu_b", "conv_w")


def _pack_rows(arrays):
    flat = []
    for a in arrays:
        v = a.reshape(-1)
        flat.append(jnp.pad(v, (0, (-v.shape[0]) % LANES)))
    v = jnp.concatenate(flat)
    v = jnp.pad(v, (0, (-v.shape[0]) % (16 * LANES)))
    return v.reshape(-1, LANES)


def _unpack_rows(packed, shapes):
    out, pos = [], 0
    flat = packed.reshape(-1)
    for shp in shapes:
        size = math.prod(shp)
        out.append(flat[pos:pos + size].reshape(shp))
        pos += size + (-size) % LANES
    return out


def kernel(x, p, norm1_g, w_in, conv_w, q_norm_g, k_norm_g, sgu_norm_g, sgu_w, sgu_b, w_out, norm2_g, w_ff1, w_ff2, norm3_g, w_ple_gate, w_ple_proj, loss_target, m_norm1_g, m_w_in, m_conv_w, m_q_norm_g, m_k_norm_g, m_sgu_norm_g, m_sgu_w, m_sgu_b, m_w_out, m_norm2_g, m_w_ff1, m_w_ff2, m_norm3_g, m_w_ple_gate, m_w_ple_proj, v_norm1_g, v_w_in, v_conv_w, v_q_norm_g, v_k_norm_g, v_sgu_norm_g, v_sgu_w, v_sgu_b, v_w_out, v_norm2_g, v_w_ff1, v_w_ff2, v_norm3_g, v_w_ple_gate, v_w_ple_proj):
    w = dict(norm1_g=norm1_g, w_in=w_in, conv_w=conv_w, q_norm_g=q_norm_g, k_norm_g=k_norm_g, sgu_norm_g=sgu_norm_g,
             sgu_w=sgu_w, sgu_b=sgu_b, w_out=w_out, norm2_g=norm2_g, w_ff1=w_ff1, w_ff2=w_ff2, norm3_g=norm3_g,
             w_ple_gate=w_ple_gate, w_ple_proj=w_ple_proj)
    m = dict(norm1_g=m_norm1_g, w_in=m_w_in, conv_w=m_conv_w, q_norm_g=m_q_norm_g, k_norm_g=m_k_norm_g,
             sgu_norm_g=m_sgu_norm_g, sgu_w=m_sgu_w, sgu_b=m_sgu_b, w_out=m_w_out, norm2_g=m_norm2_g, w_ff1=m_w_ff1,
             w_ff2=m_w_ff2, norm3_g=m_norm3_g, w_ple_gate=m_w_ple_gate, w_ple_proj=m_w_ple_proj)
    v = dict(norm1_g=v_norm1_g, w_in=v_w_in, conv_w=v_conv_w, q_norm_g=v_q_norm_g, k_norm_g=v_k_norm_g,
             sgu_norm_g=v_sgu_norm_g, sgu_w=v_sgu_w, sgu_b=v_sgu_b, w_out=v_w_out, norm2_g=v_norm2_g, w_ff1=v_w_ff1,
             w_ff2=v_w_ff2, norm3_g=v_norm3_g, w_ple_gate=v_w_ple_gate, w_ple_proj=v_w_ple_proj)
    depth = w_in.shape[0]
    d_model = x.shape[-1]
    chip = 2 * lax.axis_index("x") + lax.axis_index("y")
    core = lax.axis_index("c")
    core_arr = core.reshape(1).astype(jnp.int32)

    cw_cols = conv_w.shape[-1]
    placed = lax.dynamic_update_slice(jnp.zeros((depth, 3, CONV_W), F32), conv_w, (0, 0, chip * cw_cols))
    placed = jnp.where(core == 0, placed, 0.0)
    conv_full = _unpack_rows(small_allreduce("conv_w_gather", _pack_rows([placed])), [(depth, 3, CONV_W)])[0]

    h = x[0]
    p_bf = p[:, 0].astype(BF)
    saved, full = [], []
    shard_shapes = [w[n].shape[1:] for n in BIG]

    def gather_start(l, after):
        shards = [w[n][l].astype(BF) for n in BIG]
        lands = [lax.empty((N_CHIPS,) + s.shape, BF) for s in shards]
        return copies_start(f"l{l}_gather_start", shards, lands, _gather_plan(shard_shapes), after)

    started = gather_start(0, ())
    for l in range(depth):
        shards, lands = copies_wait(f"l{l}_gather_wait", started, _gather_arrivals(shard_shapes), (h,))
        g_in, g_out, g_ff1, g_ff2, g_gate, g_proj = gather_finish(f"l{l}_gather_finish", shards, lands)
        token = ()
        if l + 1 < depth:
            started = gather_start(l + 1, (g_proj,))
            token = (started[-1],)
        wt = prep_small(norm1_g[l], q_norm_g[l], k_norm_g[l], sgu_norm_g[l], sgu_w[l], sgu_b[l], norm2_g[l], norm3_g[l],
                        conv_full[l])
        wt["w_in"] = jnp.transpose(g_in, (1, 0, 2)).reshape(d_model, -1)
        wt["w_out"] = g_out.reshape(-1, d_model)
        wt["w_ff1"] = g_ff1
        wt["w_ff2"] = g_ff2.reshape(-1, d_model)
        wt["w_ple_gate"] = g_gate.reshape(-1, d_model)
        wt["w_ple_proj"] = g_proj
        h, sv = layer_fwd(f"l{l}", h, p_bf[l], wt, token)
        saved.append(sv)
        full.append(wt)

    loss_tile, dh = loss_head("loss", h, loss_target[0])
    loss = lax.psum(loss_tile[0, 0], ("x", "y", "c"))

    small = [None] * depth
    chip_arr = jnp.stack([chip, core]).astype(jnp.int32)
    big_w, big_m, big_v = ([d[n] for n in BIG] for d in (w, m, v))
    updated = None

    def finish_layer(l, started, after, updated):
        reduced = reduce_scatter_end(f"l{l}", chip_arr, started, after)
        return adamw_layer(f"l{l}_adamw", l, big_w, reduced, big_m, big_v, updated)

    started = None
    for l in reversed(range(depth)):
        dh, g = layer_bwd(f"l{l}", dh, saved[l], full[l], () if started is None else (started[-1],))
        if started is not None:
            updated = finish_layer(l + 1, started, (dh,), updated)
        small[l] = small_grads(g)
        shards_in = w_in.shape[-1]
        gl = [jnp.transpose(g["w_in"].reshape(d_model, N_CHIPS, shards_in), (1, 0, 2)),
              g["w_out"].reshape(N_CHIPS, -1, d_model), g["w_ff1"], g["w_ff2"].reshape(N_CHIPS, -1, d_model),
              g["w_ple_gate"].reshape(N_CHIPS, -1, d_model), g["w_ple_proj"]]
        started = reduce_scatter_begin(f"l{l}", core_arr, gl)

    grads, delta, new_m, new_v = {}, {}, {}, {}
    packed = _pack_rows([small[l][n] for l in range(depth) for n in SMALL])
    shapes = [small[l][n].shape for l in range(depth) for n in SMALL]
    pieces = _unpack_rows(small_allreduce("small_grads", packed), shapes)
    for i, n in enumerate(SMALL):
        grads[n] = jnp.stack([pieces[l * len(SMALL) + i] for l in range(depth)])
    grads["conv_w"] = lax.dynamic_slice(grads["conv_w"], (0, 0, chip * cw_cols), (depth, 3, cw_cols))
    for n in SMALL:
        shp = w[n].shape
        two_d = (-1, shp[-1]) if n != "sgu_w" else (-1, LANES)
        d, nm, nv = adamw(f"adamw_{n}", w[n].reshape(two_d), grads[n].reshape(two_d), m[n].reshape(two_d),
                          v[n].reshape(two_d))
        delta[n], new_m[n], new_v[n] = d.reshape(shp), nm.reshape(shp), nv.reshape(shp)

    updated = finish_layer(0, started, (dh, *[new_v[n] for n in SMALL]), updated)
    for k, d in enumerate((grads, delta, new_m, new_v)):
        d.update(zip(BIG, updated[k]))

    return (loss, dh[None], *[grads[n] for n in WEIGHTS], *[delta[n] for n in WEIGHTS], *[new_m[n] for n in WEIGHTS],
            *[new_v[n] for n in WEIGHTS])
```

```python
import functools
import math

import jax
import jax.numpy as jnp
from jax import lax
from jax.experimental import pallas as pl
from jax.experimental.pallas import tpu as pltpu

F32 = jnp.float32
BF = jnp.bfloat16
MESH = pl.DeviceIdType.MESH
HIGHEST = lax.Precision.HIGHEST

EPS = 1e-6
HEAD_DIM = 64
CONV_W = 256
ATTN_W = 512
SGU_W = 256
CHUNK = 128
N_CHIPS = 4
SCALE = HEAD_DIM ** -0.5
LANES = 128
VMEM_LIMIT = 56 * 1024 * 1024

ADAM_LR = 0.001
ADAM_B1 = 0.9
ADAM_B2 = 0.999
ADAM_EPS = 1e-08
ADAM_WD = 0.01
ADAM_STEP = 10

NT_DIMS = (((1,), (1,)), ((), ()))
TN_DIMS = (((0,), (0,)), ((), ()))


def _params(*sem):
    return pltpu.CompilerParams(dimension_semantics=sem if sem else None, vmem_limit_bytes=VMEM_LIMIT)


def _sds(shape, dtype):
    return jax.ShapeDtypeStruct(shape, dtype)


def _erf(x):
    return lax.erf(x)


def _gelu(x):
    return 0.5 * x * (1.0 + _erf(x * (2.0 ** -0.5)))


def _gelu_grad(x):
    return 0.5 * (1.0 + _erf(x * (2.0 ** -0.5))) + x * jnp.exp(-0.5 * x * x) * (1.0 / math.sqrt(2.0 * math.pi))


def _log_sigmoid(z):
    return jnp.minimum(z, 0.0) - jnp.log(1.0 + jnp.exp(-jnp.abs(z)))


def _head_mean_matrix(width):
    r = lax.broadcasted_iota(jnp.int32, (width, width), 0) // HEAD_DIM
    c = lax.broadcasted_iota(jnp.int32, (width, width), 1) // HEAD_DIM
    return (r == c).astype(BF)


def _head_mean(x, m):
    hi = x.astype(BF)
    lo = (x - hi.astype(F32)).astype(BF)
    return _dot2_stacked(hi, lo, m) * (1.0 / HEAD_DIM)


def rmsnorm(name, h, g, after=()):
    S, D = h.shape
    tm = min(S, 512)

    def body(h_ref, g_ref, *rest):
        x = h_ref[...]
        r = lax.rsqrt(jnp.mean(x * x, axis=-1, keepdims=True) + EPS)
        rest[-1][...] = ((x * r) * g_ref[...]).astype(BF)

    return pl.pallas_call(
        body, name=name, grid=(S // tm,),
        in_specs=[pl.BlockSpec((tm, D), lambda i: (i, 0)), pl.BlockSpec((1, D), lambda i: (0, 0))] +
                 [pl.BlockSpec(memory_space=pl.ANY)] * len(after),
        out_specs=pl.BlockSpec((tm, D), lambda i: (i, 0)),
        out_shape=_sds((S, D), BF), compiler_params=_params("parallel"),
    )(h, g.reshape(1, D), *after)


def mm_nn(name, x, w, *, extras=(), pro=None, epi=None, out_dtypes=None, tm=512, tn=512):
    S, K = x.shape
    if w.ndim == 3:
        J, _, tn = w.shape
        N = J * tn
        w_spec = pl.BlockSpec((None, K, tn), lambda n, m: (n, 0, 0))
    else:
        N = w.shape[1]
        tn = min(tn, N)
        w_spec = pl.BlockSpec((K, tn), lambda n, m: (0, n))
    tm = min(tm, S)
    out_dtypes = (BF,) if out_dtypes is None else out_dtypes
    n_ex, n_out = len(extras), len(out_dtypes)

    def body(x_ref, w_ref, *rest):
        xv = x_ref[...]
        if pro is not None:
            xv = pro(xv)
        acc = jnp.dot(xv.astype(BF), w_ref[...], preferred_element_type=F32)
        outs = (acc,) if epi is None else epi(acc, *[e[...] for e in rest[:n_ex]])
        for o_ref, o in zip(rest[n_ex:], outs):
            o_ref[...] = o.astype(o_ref.dtype)

    tile = pl.BlockSpec((tm, tn), lambda n, m: (m, n))
    out = pl.pallas_call(
        body, name=name, grid=(N // tn, S // tm),
        in_specs=[pl.BlockSpec((tm, K), lambda n, m: (m, 0)), w_spec] + [tile] * n_ex,
        out_specs=[tile] * n_out,
        out_shape=[_sds((S, N), d) for d in out_dtypes],
        compiler_params=_params("parallel", "parallel"),
    )(x, w, *extras)
    return out[0] if n_out == 1 else out


def mm_nt(name, dy, w, *, extras=(), epi=None, tm=256):
    S, N = dy.shape
    K = w.shape[0]
    tm = min(tm, S)
    n_ex = len(extras)

    def body(dy_ref, w_ref, *rest):
        acc = lax.dot_general(dy_ref[...].astype(BF), w_ref[...], NT_DIMS, preferred_element_type=F32)
        if epi is not None:
            acc = epi(acc, *[e[...] for e in rest[:n_ex]])
        rest[n_ex][...] = acc.astype(BF)

    row = pl.BlockSpec((tm, K), lambda i: (i, 0))
    return pl.pallas_call(
        body, name=name, grid=(S // tm,),
        in_specs=[pl.BlockSpec((tm, N), lambda i: (i, 0)), pl.BlockSpec((K, N), lambda i: (0, 0))] + [row] * n_ex,
        out_specs=row, out_shape=_sds((S, K), BF), compiler_params=_params("parallel"),
    )(dy, w, *extras)


def mm_nt_rmsbwd(name, dy, w, h, g, dres, *, tm=256):
    S, N = dy.shape
    D = h.shape[1]
    tm = min(tm, S)
    blocked = w.ndim == 3
    nj = w.shape[2] if blocked else N

    def body(dy_ref, w_ref, h_ref, g_ref, dres_ref, dh_ref, dg_ref):
        i = pl.program_id(0)
        if blocked:
            dyn = None
            for j in range(w.shape[0]):
                part = lax.dot_general(dy_ref[:, j * nj:(j + 1) * nj].astype(BF), w_ref[j], NT_DIMS,
                                       preferred_element_type=F32)
                dyn = part if dyn is None else dyn + part
        else:
            dyn = lax.dot_general(dy_ref[...].astype(BF), w_ref[...], NT_DIMS, preferred_element_type=F32)
        x = h_ref[...]
        r = lax.rsqrt(jnp.mean(x * x, axis=-1, keepdims=True) + EPS)
        t = dyn * g_ref[...]
        dh_ref[...] = dres_ref[...] + r * t - x * (r * r * r) * jnp.mean(t * x, axis=-1, keepdims=True)
        part = jnp.sum(dyn * (x * r), axis=0, keepdims=True)

        @pl.when(i == 0)
        def _():
            dg_ref[...] = part

        @pl.when(i > 0)
        def _():
            dg_ref[...] += part

    w_spec = pl.BlockSpec(w.shape, (lambda i: (0, 0, 0)) if blocked else (lambda i: (0, 0)))
    row = pl.BlockSpec((tm, D), lambda i: (i, 0))
    vec = pl.BlockSpec((1, D), lambda i: (0, 0))
    return pl.pallas_call(
        body, name=name, grid=(S // tm,),
        in_specs=[pl.BlockSpec((tm, N), lambda i: (i, 0)), w_spec, row, vec, row],
        out_specs=[row, vec], out_shape=[_sds((S, D), F32), _sds((1, D), F32)],
        compiler_params=_params("arbitrary"),
    )(dy, w, h, g.reshape(1, D), dres)


def mm_tn(name, x, dy, *, pro_x=None, col_blocks=None, tk=1024, tn=1024):
    S, K = x.shape
    N = dy.shape[1]
    tk = min(tk, K)
    if col_blocks is not None:
        tn = N // col_blocks
        out_shape = _sds((col_blocks, K, tn), BF)
        out_spec = pl.BlockSpec((None, tk, tn), lambda k, n: (n, k, 0))
    else:
        tn = min(tn, N)
        out_shape = _sds((K, N), BF)
        out_spec = pl.BlockSpec((tk, tn), lambda k, n: (k, n))

    def body(x_ref, dy_ref, o_ref):
        xv = x_ref[...]
        if pro_x is not None:
            xv = pro_x(xv)
        o_ref[...] = lax.dot_general(xv.astype(BF), dy_ref[...].astype(BF), TN_DIMS,
                                     preferred_element_type=F32).astype(BF)

    return pl.pallas_call(
        body, name=name, grid=(K // tk, N // tn),
        in_specs=[pl.BlockSpec((S, tk), lambda k, n: (0, k)), pl.BlockSpec((S, tn), lambda k, n: (0, n))],
        out_specs=out_spec, out_shape=out_shape, compiler_params=_params("parallel", "parallel"),
    )(x, dy)


def _conv_parts(ac_ref, ah_ref, cw):
    a_c = ac_ref[...].astype(F32)
    a_h = ah_ref[...].astype(F32)
    x = a_c * a_h
    row = lax.broadcasted_iota(jnp.int32, x.shape, 0)
    x1 = jnp.where(row >= 1, pltpu.roll(x, 1, 0), 0.0)
    x2 = jnp.where(row >= 2, pltpu.roll(x, 2, 0), 0.0)
    cv = cw[0:1] * x2 + cw[1:2] * x1 + cw[2:3] * x
    return a_c, a_h, x, x1, x2, cv, row


def conv_fwd(name, proj, cw):
    S = proj.shape[0]

    def body(ab_ref, ac_ref, ah_ref, cw_ref, o_ref):
        cv = _conv_parts(ac_ref, ah_ref, cw_ref[...])[5]
        o_ref[...] = (ab_ref[...].astype(F32) * cv).astype(BF)

    col = lambda j: pl.BlockSpec((S, CONV_W), lambda i, j=j: (0, j))
    return pl.pallas_call(
        body, name=name, grid=(1,),
        in_specs=[col(0), col(1), col(2), pl.BlockSpec((3, CONV_W), lambda i: (0, 0))],
        out_specs=pl.BlockSpec((S, CONV_W), lambda i: (0, 0)),
        out_shape=_sds((S, CONV_W), BF), compiler_params=_params("arbitrary"),
    )(proj, proj, proj, cw)


def conv_bwd(name, dy, proj, cw):
    S = proj.shape[0]

    def body(dy_ref, ab_ref, ac_ref, ah_ref, cw_ref, dab_ref, dac_ref, dah_ref, dcw_ref):
        w = cw_ref[...]
        a_c, a_h, x, x1, x2, cv, row = _conv_parts(ac_ref, ah_ref, w)
        d = dy_ref[...].astype(F32)
        dab_ref[...] = (d * cv).astype(BF)
        dcv = d * ab_ref[...].astype(F32)
        d1 = jnp.where(row < S - 1, pltpu.roll(dcv, S - 1, 0), 0.0)
        d2 = jnp.where(row < S - 2, pltpu.roll(dcv, S - 2, 0), 0.0)
        dx = w[2:3] * dcv + w[1:2] * d1 + w[0:1] * d2
        dac_ref[...] = (dx * a_h).astype(BF)
        dah_ref[...] = (dx * a_c).astype(BF)
        dcw_ref[0:1, :] = jnp.sum(dcv * x2, axis=0, keepdims=True)
        dcw_ref[1:2, :] = jnp.sum(dcv * x1, axis=0, keepdims=True)
        dcw_ref[2:3, :] = jnp.sum(dcv * x, axis=0, keepdims=True)

    col = lambda j: pl.BlockSpec((S, CONV_W), lambda i, j=j: (0, j))
    one = pl.BlockSpec((S, CONV_W), lambda i: (0, 0))
    small = pl.BlockSpec((3, CONV_W), lambda i: (0, 0))
    return pl.pallas_call(
        body, name=name, grid=(1,),
        in_specs=[col(0), col(0), col(1), col(2), small],
        out_specs=[one, one, one, small],
        out_shape=[_sds((S, CONV_W), BF)] * 3 + [_sds((3, CONV_W), F32)],
        compiler_params=_params("arbitrary"),
    )(dy, proj, proj, proj, cw)


SGU_HEADS = SGU_W // HEAD_DIM
CU_BLOCK = 2304 // SGU_W
CV_BLOCK = 2560 // SGU_W


def _sgu_common(cu_ref, cv_ref, gv_ref, tm):
    c_u = cu_ref[...].astype(F32)
    c_v = cv_ref[...].astype(F32)
    hm = _head_mean_matrix(SGU_W)
    u = _gelu(c_u)
    vg = _gelu(c_v)
    r = lax.rsqrt(_head_mean(vg * vg, hm) + EPS)
    vv = (vg * r) * gv_ref[...]
    head = lax.broadcasted_iota(jnp.int32, (CHUNK, SGU_W), 1) // HEAD_DIM
    tri = (lax.broadcasted_iota(jnp.int32, (CHUNK, CHUNK), 0) >=
           lax.broadcasted_iota(jnp.int32, (CHUNK, CHUNK), 1))
    return c_u, c_v, hm, u, vg, r, vv, head, tri


def _sgu_mix(w_ref, tri, head, vvc, bias):
    sv = bias
    for g in range(SGU_HEADS):
        wg = jnp.where(tri, w_ref[g], 0.0).astype(BF)
        sv = sv + jnp.where(head == g, jnp.dot(wg, vvc, preferred_element_type=F32), 0.0)
    return sv


def sgu_fwd(name, proj, gv, w, bias):
    S = proj.shape[0]
    tm = min(S, 512)

    def body(cu_ref, cv_ref, gv_ref, w_ref, b_ref, o_ref):
        _, _, _, u, _, _, vv, head, tri = _sgu_common(cu_ref, cv_ref, gv_ref, tm)
        vvb = vv.astype(BF)
        for ch in range(tm // CHUNK):
            rows = slice(ch * CHUNK, (ch + 1) * CHUNK)
            sv = _sgu_mix(w_ref, tri, head, vvb[rows], b_ref[...])
            o_ref[rows, :] = (u[rows] * sv).astype(BF)

    const = lambda shape: pl.BlockSpec(shape, lambda i: (0,) * len(shape))
    return pl.pallas_call(
        body, name=name, grid=(S // tm,),
        in_specs=[pl.BlockSpec((tm, SGU_W), lambda i: (i, CU_BLOCK)), pl.BlockSpec((tm, SGU_W), lambda i: (i, CV_BLOCK)),
                  const((1, SGU_W)), const((SGU_HEADS, CHUNK, CHUNK)), const((CHUNK, SGU_W))],
        out_specs=pl.BlockSpec((tm, SGU_W), lambda i: (i, 0)),
        out_shape=_sds((S, SGU_W), BF), compiler_params=_params("parallel"),
    )(proj, proj, gv, w, bias)


def sgu_bwd(name, dy, proj, gv, w, bias):
    S = proj.shape[0]
    tm = min(S, 512)

    def body(dy_ref, cu_ref, cv_ref, gv_ref, w_ref, b_ref, dcu_ref, dcv_ref, dw_ref, db_ref, dgv_ref, dvv_s):
        i = pl.program_id(0)
        c_u, c_v, hm, u, vg, r, vv, head, tri = _sgu_common(cu_ref, cv_ref, gv_ref, tm)
        vvb = vv.astype(BF)
        d = dy_ref[...].astype(F32)
        ind = (lax.broadcasted_iota(jnp.int32, (SGU_W, LANES), 0) // HEAD_DIM ==
               lax.broadcasted_iota(jnp.int32, (SGU_W, LANES), 1)).astype(BF)
        dw_acc = [jnp.zeros((CHUNK, CHUNK), F32) for _ in range(SGU_HEADS)]
        db_acc = jnp.zeros((CHUNK, LANES), F32)
        for ch in range(tm // CHUNK):
            rows = slice(ch * CHUNK, (ch + 1) * CHUNK)
            sv = _sgu_mix(w_ref, tri, head, vvb[rows], b_ref[...])
            dcu_ref[rows, :] = (d[rows] * sv * _gelu_grad(c_u[rows])).astype(BF)
            dsv = d[rows] * u[rows]
            dsv_hi = dsv.astype(BF)
            db_acc = db_acc + _dot2_stacked(dsv_hi, (dsv - dsv_hi.astype(F32)).astype(BF), ind)
            dvv = jnp.zeros((CHUNK, SGU_W), F32)
            for g in range(SGU_HEADS):
                dsv_g = jnp.where(head == g, dsv, 0.0).astype(BF)
                wg = jnp.where(tri, w_ref[g], 0.0).astype(BF)
                dvv = dvv + lax.dot_general(wg, dsv_g, TN_DIMS, preferred_element_type=F32)
                dw_acc[g] = dw_acc[g] + lax.dot_general(dsv_g, vvb[rows], NT_DIMS, preferred_element_type=F32)
            dvv_s[rows, :] = dvv
        dvv = dvv_s[...]
        gvv = gv_ref[...]
        t = dvv * gvv
        dvg = r * t - vg * (r * r * r) * _head_mean(t * vg, hm)
        dcv_ref[...] = (dvg * _gelu_grad(c_v)).astype(BF)
        dgv = jnp.sum(dvv * (vg * r), axis=0, keepdims=True)

        @pl.when(i == 0)
        def _():
            for g in range(SGU_HEADS):
                dw_ref[g] = jnp.where(tri, dw_acc[g], 0.0)
            db_ref[...] = db_acc
            dgv_ref[...] = dgv

        @pl.when(i > 0)
        def _():
            for g in range(SGU_HEADS):
                dw_ref[g] += jnp.where(tri, dw_acc[g], 0.0)
            db_ref[...] += db_acc
            dgv_ref[...] += dgv

    const = lambda shape: pl.BlockSpec(shape, lambda i: (0,) * len(shape))
    tile = pl.BlockSpec((tm, SGU_W), lambda i: (i, 0))
    return pl.pallas_call(
        body, name=name, grid=(S // tm,),
        in_specs=[pl.BlockSpec((tm, SGU_W), lambda i: (i, 3)),
                  pl.BlockSpec((tm, SGU_W), lambda i: (i, CU_BLOCK)), pl.BlockSpec((tm, SGU_W), lambda i: (i, CV_BLOCK)),
                  const((1, SGU_W)), const((SGU_HEADS, CHUNK, CHUNK)), const((CHUNK, SGU_W))],
        out_specs=[tile, tile, const((SGU_HEADS, CHUNK, CHUNK)), const((CHUNK, LANES)), const((1, SGU_W))],
        out_shape=[_sds((S, SGU_W), BF), _sds((S, SGU_W), BF), _sds((SGU_HEADS, CHUNK, CHUNK), F32),
                   _sds((CHUNK, LANES), F32), _sds((1, SGU_W), F32)],
        scratch_shapes=[pltpu.VMEM((tm, SGU_W), F32)],
        compiler_params=_params("arbitrary"),
    )(dy, proj, proj, gv, w, bias)


HEAD_PAIRS = ATTN_W // LANES
Q_BLOCK0 = 768 // LANES
K_BLOCK0 = 1280 // LANES
V_BLOCK0 = 1792 // LANES


def _attn_tile(S):
    return min(S, 256)


def _qk_norm(x, g, hm):
    r = lax.rsqrt(_head_mean(x * x, hm) + EPS)
    return r, (x * r) * g


MASKED = -1e30


def _logit_parts(z):
    lb = _log_sigmoid(z)
    lr = lb - z
    hi = lr.astype(BF)
    return lb, hi, (lr - hi.astype(F32)).astype(BF)


def _stack_heads(x, lane):
    return jnp.concatenate([jnp.where(lane < HEAD_DIM, x, 0.0), jnp.where(lane >= HEAD_DIM, x, 0.0)],
                           axis=0).astype(BF)


def _dot2_stacked(hi, lo, u):
    rows = hi.shape[0]
    both = jnp.dot(jnp.concatenate([hi, lo], axis=0), u, preferred_element_type=F32)
    return both[:rows] + both[rows:]


def attn_fwd(name, proj, gq, gk):
    S = proj.shape[0]
    T = _attn_tile(S)
    nq = S // T

    def body(q_ref, k_ref, v_ref, gq_ref, gk_ref, o_ref, tot_ref, kn_s, lb_s, hi_s, lo_s, z_s, a_s, o_s):
        qi = pl.program_id(1)
        hm = _head_mean_matrix(LANES)

        @pl.when(qi == 0)
        def _():
            kn_s[...] = _qk_norm(k_ref[...].astype(F32), gk_ref[...], hm)[1].astype(BF)

        qn = _qk_norm(q_ref[...].astype(F32), gq_ref[...], hm)[1]
        lane = lax.broadcasted_iota(jnp.int32, (T, LANES), 1)
        qst = _stack_heads(qn, lane)
        rowi = lax.broadcasted_iota(jnp.int32, (T, T), 0)
        coli = lax.broadcasted_iota(jnp.int32, (T, T), 1)
        u_excl = (rowi > coli).astype(BF)
        diagonal = jnp.where(coli < rowi, 0.0, MASKED)

        def logits(j):
            return lax.dot_general(qst, kn_s[pl.ds(pl.multiple_of(j * T, T), T), :], NT_DIMS,
                                   preferred_element_type=F32)

        def values(j):
            return v_ref[pl.ds(pl.multiple_of(j * T, T), T), :].astype(BF)

        def keep(slot, z):
            lb_s[slot], hi_s[...], lo_s[...] = _logit_parts(z)

        def step(it, carry):
            run = carry
            j = qi - it
            hi, lo = hi_s[...], lo_s[...]
            both = jnp.dot(jnp.concatenate([hi, lo], axis=0), u_excl, preferred_element_type=F32)
            o_s[...] += jnp.dot(a_s[...], values(jnp.minimum(j + 1, qi)), preferred_element_type=F32)
            z_after = logits(jnp.maximum(j - 2, 0))
            first = hi[:, 0:1].astype(F32) + lo[:, 0:1].astype(F32)
            keep((it + 1) % 2, z_s[...])
            later = both[:2 * T] + both[2 * T:]
            a_s[...] = jnp.exp(lb_s[it % 2] + later + run).astype(BF)
            z_s[...] = z_after
            return run + later[:, 0:1] + first

        keep(0, logits(qi) + jnp.concatenate([diagonal, diagonal], axis=0))
        z_s[...] = logits(jnp.maximum(qi - 1, 0))
        a_s[...] = jnp.zeros_like(a_s)
        o_s[...] = jnp.zeros_like(o_s)
        run = lax.fori_loop(0, qi + 1, step, jnp.zeros((2 * T, 1), F32))
        o = o_s[...] + jnp.dot(a_s[...], values(0), preferred_element_type=F32)
        o_ref[...] = jnp.where(lane < HEAD_DIM, o[:T], o[T:]).astype(BF)
        tot_ref[...] = jnp.where(lane < HEAD_DIM, run[:T], run[T:])

    gain = pl.BlockSpec((1, LANES), lambda hp, qi: (0, 0))
    full = lambda b0: pl.BlockSpec((S, LANES), lambda hp, qi, b0=b0: (0, b0 + hp))
    tile = pl.BlockSpec((T, LANES), lambda hp, qi: (qi, hp))
    return pl.pallas_call(
        body, name=name, grid=(HEAD_PAIRS, nq),
        in_specs=[pl.BlockSpec((T, LANES), lambda hp, qi: (qi, Q_BLOCK0 + hp)), full(K_BLOCK0), full(V_BLOCK0), gain, gain],
        out_specs=[tile, tile],
        out_shape=[_sds((S, ATTN_W), BF), _sds((S, ATTN_W), F32)],
        scratch_shapes=[pltpu.VMEM((S, LANES), BF), pltpu.VMEM((2, 2 * T, T), F32), pltpu.VMEM((2 * T, T), BF),
                        pltpu.VMEM((2 * T, T), BF), pltpu.VMEM((2 * T, T), F32), pltpu.VMEM((2 * T, T), BF),
                        pltpu.VMEM((2 * T, LANES), F32)],
        compiler_params=_params("arbitrary", "arbitrary"),
    )(proj, proj, proj, gq, gk)


def attn_bwd(name, dy, proj, tot, gq, gk):
    S = proj.shape[0]
    T = _attn_tile(S)
    nq = S // T

    def body(q_ref, k_ref, v_ref, tot_ref, do_ref, gq_ref, gk_ref,
             dq_ref, dk_ref, dv_ref, dgq_ref, dgk_ref, kn_s, dkn_s, dv_s,
             lb_s, z_s, g_s, dq_s, hi_s, lo_s, a_s, ghi_s, glo_s):
        hp = pl.program_id(0)
        qi = pl.program_id(1)
        hm = _head_mean_matrix(LANES)

        @pl.when(qi == 0)
        def _():
            kn_s[...] = _qk_norm(k_ref[...].astype(F32), gk_ref[...], hm)[1].astype(BF)
            dkn_s[...] = jnp.zeros_like(dkn_s)
            dv_s[...] = jnp.zeros_like(dv_s)

        q = q_ref[...].astype(F32)
        rq, qn = _qk_norm(q, gq_ref[...], hm)
        lane = lax.broadcasted_iota(jnp.int32, (T, LANES), 1)
        qst = _stack_heads(qn, lane)
        dost = _stack_heads(do_ref[...].astype(F32), lane)
        total = jnp.concatenate([tot_ref[:, 0:1], tot_ref[:, HEAD_DIM:HEAD_DIM + 1]], axis=0)
        rowi = lax.broadcasted_iota(jnp.int32, (T, T), 0)
        coli = lax.broadcasted_iota(jnp.int32, (T, T), 1)
        u_upto = (rowi <= coli).astype(BF)
        u_before = (rowi < coli).astype(BF)
        diagonal = jnp.where(coli < rowi, 0.0, MASKED)

        on_diagonal = jnp.concatenate([diagonal, diagonal], axis=0)

        def rows(b):
            return pl.ds(pl.multiple_of(jnp.clip(b, 0, qi) * T, T), T)

        def logits(b):
            return lax.dot_general(qst, kn_s[rows(b), :], NT_DIMS, preferred_element_type=F32)

        def keep(b, z):
            bias = jnp.where(b == qi, on_diagonal, jnp.where(b > qi, MASKED, 0.0))
            lb_s[b % 3], hi_s[...], lo_s[...] = _logit_parts(z + bias)

        def step(i, carry):
            run, grun = carry
            both_before = jnp.dot(jnp.concatenate([ghi_s[...], glo_s[...]], axis=0), u_before,
                                  preferred_element_type=F32)
            both_upto = jnp.dot(jnp.concatenate([hi_s[...], lo_s[...]], axis=0), u_upto, preferred_element_type=F32)
            da = lax.dot_general(dost, v_ref[rows(i), :].astype(BF), NT_DIMS, preferred_element_type=F32)
            dv_s[rows(i - 1), :] += lax.dot_general(a_s[...], dost, TN_DIMS, preferred_element_type=F32)
            z_after = logits(i + 2)

            keep(i + 1, z_s[...])

            g = g_s[...]
            before = both_before[:2 * T] + both_before[2 * T:]
            dz = (g - jnp.exp(lb_s[(i + 2) % 3]) * (g + (grun + before))).astype(BF)
            dq_s[...] += jnp.dot(dz, kn_s[rows(i - 1), :], preferred_element_type=F32)
            dkn_s[rows(i - 1), :] += lax.dot_general(dz, qst, TN_DIMS, preferred_element_type=F32)
            grun = grun + before[:, T - 1:T] + g[:, T - 1:T]

            upto = both_upto[:2 * T] + both_upto[2 * T:]
            a = jnp.exp(lb_s[i % 3] + (total - run - upto))
            g = da * a
            a_s[...] = a.astype(BF)
            g_s[...] = g
            ghi = g.astype(BF)
            ghi_s[...] = ghi
            glo_s[...] = (g - ghi.astype(F32)).astype(BF)
            z_s[...] = z_after
            return run + upto[:, T - 1:T], grun

        lb_s[...] = jnp.full(lb_s.shape, MASKED, F32)
        for ref in (a_s, g_s, ghi_s, glo_s, dq_s):
            ref[...] = jnp.zeros_like(ref)
        keep(0, logits(0))
        z_s[...] = logits(1)
        lax.fori_loop(0, qi + 2, step, (jnp.zeros((2 * T, 1), F32), jnp.zeros((2 * T, 1), F32)))
        dqn = jnp.where(lane < HEAD_DIM, dq_s[:T, :], dq_s[T:, :])
        gq_v = gq_ref[...]
        t = dqn * gq_v
        dq_ref[...] = (rq * t - q * (rq * rq * rq) * _head_mean(t * q, hm)).astype(BF)
        dgq = jnp.sum(dqn * (q * rq), axis=0, keepdims=True) * SCALE
        first = jnp.logical_and(hp == 0, qi == 0)

        @pl.when(first)
        def _():
            dgq_ref[...] = dgq

        @pl.when(jnp.logical_not(first))
        def _():
            dgq_ref[...] += dgq

        @pl.when(qi == nq - 1)
        def _():
            k = k_ref[...].astype(F32)
            rk = _qk_norm(k, gk_ref[...], hm)[0]
            dkn = dkn_s[...]
            tk = dkn * gk_ref[...]
            dk_ref[...] = (rk * tk - k * (rk * rk * rk) * _head_mean(tk * k, hm)).astype(BF)
            dgk = jnp.sum(dkn * (k * rk), axis=0, keepdims=True)
            dv_ref[...] = dv_s[...].astype(BF)

            @pl.when(hp == 0)
            def _():
                dgk_ref[...] = dgk

            @pl.when(hp > 0)
            def _():
                dgk_ref[...] += dgk

            @pl.when(hp == HEAD_PAIRS - 1)
            def _():
                fold = (lax.broadcasted_iota(jnp.int32, (LANES, LANES), 0) % HEAD_DIM ==
                        lax.broadcasted_iota(jnp.int32, (LANES, LANES), 1) % HEAD_DIM).astype(F32)
                dgq_ref[...] = jnp.dot(dgq_ref[...], fold, precision=HIGHEST, preferred_element_type=F32)
                dgk_ref[...] = jnp.dot(dgk_ref[...], fold, precision=HIGHEST, preferred_element_type=F32)

    gain = pl.BlockSpec((1, LANES), lambda hp, qi: (0, 0))
    full = lambda b0: pl.BlockSpec((S, LANES), lambda hp, qi, b0=b0: (0, b0 + hp))
    tile = pl.BlockSpec((T, LANES), lambda hp, qi: (qi, hp))
    col = pl.BlockSpec((S, LANES), lambda hp, qi: (0, hp))
    dgain = pl.BlockSpec((1, LANES), lambda hp, qi: (0, 0))
    return pl.pallas_call(
        body, name=name, grid=(HEAD_PAIRS, nq),
        in_specs=[pl.BlockSpec((T, LANES), lambda hp, qi: (qi, Q_BLOCK0 + hp)), full(K_BLOCK0), full(V_BLOCK0),
                  tile, pl.BlockSpec((T, LANES), lambda hp, qi: (qi, 2 + hp)), gain, gain],
        out_specs=[tile, col, col, dgain, dgain],
        out_shape=[_sds((S, ATTN_W), BF)] * 3 + [_sds((1, LANES), F32)] * 2,
        scratch_shapes=[pltpu.VMEM((S, LANES), BF), pltpu.VMEM((S, LANES), F32), pltpu.VMEM((S, LANES), F32),
                        pltpu.VMEM((3, 2 * T, T), F32), pltpu.VMEM((2 * T, T), F32), pltpu.VMEM((2 * T, T), F32),
                        pltpu.VMEM((2 * T, LANES), F32)] + [pltpu.VMEM((2 * T, T), BF)] * 5,
        compiler_params=_params("arbitrary", "arbitrary"),
    )(proj, proj, proj, tot, dy, gq, gk)


def ple_bwd_elem(name, dh, gp, pp, after=()):
    S, D = dh.shape
    tm = min(S, 512)

    def body(dh_ref, gp_ref, pp_ref, *rest):
        dgp_ref, dpp_ref = rest[-2:]
        d = dh_ref[...]
        gate = jax.nn.sigmoid(gp_ref[...].astype(F32))
        dpp_ref[...] = (d * gate).astype(BF)
        dgp_ref[...] = (d * pp_ref[...].astype(F32) * gate * (1.0 - gate)).astype(BF)

    tile = pl.BlockSpec((tm, D), lambda i: (i, 0))
    return pl.pallas_call(
        body, name=name, grid=(S // tm,), in_specs=[tile] * 3 + [pl.BlockSpec(memory_space=pl.ANY)] * len(after),
        out_specs=[tile] * 2, out_shape=[_sds((S, D), BF)] * 2, compiler_params=_params("parallel"),
    )(dh, gp, pp, *after)


def loss_head(name, h, target):
    S, D = h.shape
    tm = min(S, 512)

    def body(h_ref, t_ref, loss_ref, dh_ref):
        i = pl.program_id(0)
        e = h_ref[...] - t_ref[...]
        dh_ref[...] = e * (1.0 / D)
        part = jnp.zeros((8, LANES), F32) + 0.5 * jnp.sum(jnp.mean(e * e, axis=-1, keepdims=True))

        @pl.when(i == 0)
        def _():
            loss_ref[...] = part

        @pl.when(i > 0)
        def _():
            loss_ref[...] += part

    tile = pl.BlockSpec((tm, D), lambda i: (i, 0))
    return pl.pallas_call(
        body, name=name, grid=(S // tm,), in_specs=[tile, tile],
        out_specs=[pl.BlockSpec((8, LANES), lambda i: (0, 0)), tile],
        out_shape=[_sds((8, LANES), F32), _sds((S, D), F32)], compiler_params=_params("arbitrary"),
    )(h, target)


def _adamw_math(w, g, m, v):
    c1 = 1.0 - ADAM_B1 ** ADAM_STEP
    c2 = 1.0 - ADAM_B2 ** ADAM_STEP
    nm = ADAM_B1 * m + (1.0 - ADAM_B1) * g
    nv = ADAM_B2 * v + (1.0 - ADAM_B2) * (g * g)
    return -ADAM_LR * ((nm / c1) / (jnp.sqrt(nv / c2) + ADAM_EPS) + ADAM_WD * w), nm, nv


def adamw(name, w, g, m, v):
    R, C = w.shape
    tr = R
    for cand in (512, 256, 128, 64, 32, 16, 8):
        if R % cand == 0:
            tr = cand
            break

    def body(w_ref, g_ref, m_ref, v_ref, d_ref, nm_ref, nv_ref):
        d_ref[...], nm_ref[...], nv_ref[...] = _adamw_math(w_ref[...], g_ref[...], m_ref[...], v_ref[...])

    tile = pl.BlockSpec((tr, C), lambda i: (i, 0))
    return pl.pallas_call(
        body, name=name, grid=(R // tr,), in_specs=[tile] * 4, out_specs=[tile] * 3,
        out_shape=[_sds((R, C), F32)] * 3, compiler_params=_params("parallel"),
    )(w, g, m, v)


def adamw_layer(name, layer, ws, gs, ms, vs, prev):
    n = len(ws)
    steps = 8

    def body(*refs):
        ins, outs = refs[:4 * n + (0 if prev is None else 4 * n)], refs[-4 * n:]
        for i in range(n):
            w_ref, g_ref, m_ref, v_ref = (ins[k * n + i] for k in range(4))
            g = g_ref[...]
            outs[i][...] = g
            outs[n + i][...], outs[2 * n + i][...], outs[3 * n + i][...] = _adamw_math(w_ref[...], g, m_ref[...],
                                                                                        v_ref[...])

    def stacked(a):
        return pl.BlockSpec((None, a.shape[1] // steps, a.shape[2]), lambda t: (layer, t, 0))

    def flat(a):
        return pl.BlockSpec((a.shape[0] // steps, a.shape[1]), lambda t: (t, 0))

    in_specs = [stacked(a) for a in ws] + [flat(a) for a in gs] + [stacked(a) for a in ms] + [stacked(a) for a in vs]
    operands = [*ws, *gs, *ms, *vs]
    aliases = {}
    if prev is not None:
        flat_prev = [a for group in prev for a in group]
        in_specs += [pl.BlockSpec(memory_space=pl.ANY)] * len(flat_prev)
        aliases = {4 * n + i: i for i in range(4 * n)}
        operands += flat_prev
    out = pl.pallas_call(
        body, name=name, grid=(steps,), in_specs=in_specs, out_specs=[stacked(a) for a in ws] * 4,
        out_shape=[_sds(a.shape, F32) for a in ws] * 4, input_output_aliases=aliases,
        compiler_params=_params("parallel"),
    )(*operands)
    return [list(out[k * n:(k + 1) * n]) for k in range(4)]


def _relu2(u):
    r = jnp.maximum(u.astype(F32), 0.0)
    return r * r


def layer_fwd(tag, h0, p_bf, wt, after=()):
    hn1 = rmsnorm(f"{tag}_norm1", h0, wt["norm1_g"], after)
    proj = mm_nn(f"{tag}_proj", hn1, wt["w_in"], tn=1408)
    ya = conv_fwd(f"{tag}_conv", proj, wt["conv_w"])
    yb, yb_tot = attn_fwd(f"{tag}_attn", proj, wt["gq"], wt["gk"])
    yc = sgu_fwd(f"{tag}_sgu", proj, wt["gv"], wt["sgu_w"], wt["sgu_bias"])
    y = jnp.concatenate([ya, yb, yc], axis=-1)
    h1 = mm_nn(f"{tag}_out", y, wt["w_out"], extras=(h0,), epi=lambda acc, h: (h + acc,), out_dtypes=(F32,))
    hn2 = rmsnorm(f"{tag}_norm2", h1, wt["norm2_g"])
    uu = mm_nn(f"{tag}_ff1", hn2, wt["w_ff1"])
    h2 = mm_nn(f"{tag}_ff2", uu, wt["w_ff2"], pro=_relu2, extras=(h1,), epi=lambda acc, h: (h + acc,),
               out_dtypes=(F32,))
    hn3 = rmsnorm(f"{tag}_norm3", h2, wt["norm3_g"])
    gp = mm_nn(f"{tag}_gate", hn3, wt["w_ple_gate"])
    h3, pp = mm_nn(f"{tag}_ple", p_bf, wt["w_ple_proj"], extras=(gp, h2),
                   epi=lambda acc, g, h: (h + jax.nn.sigmoid(g.astype(F32)) * acc, acc), out_dtypes=(F32, BF))
    saved = dict(h0=h0, h1=h1, h2=h2, hn1=hn1, hn2=hn2, hn3=hn3, proj=proj, yb_tot=yb_tot, y=y, uu=uu, gp=gp, pp=pp,
                 p_bf=p_bf)
    return h3, saved


def layer_bwd(tag, dh3, sv, wt, after=()):
    dgp, dpp = ple_bwd_elem(f"{tag}_dple", dh3, sv["gp"], sv["pp"], after)
    g = {}
    g["w_ple_proj"] = mm_tn(f"{tag}_dwp", sv["p_bf"], dpp, col_blocks=N_CHIPS)
    g["w_ple_gate"] = mm_tn(f"{tag}_dwg", sv["hn3"], dgp)
    dh2, g["norm3_g"] = mm_nt_rmsbwd(f"{tag}_dnorm3", dgp, wt["w_ple_gate"], sv["h2"], wt["norm3_g"], dh3)

    duu = mm_nt(f"{tag}_dff2", dh2, wt["w_ff2"], extras=(sv["uu"],),
                epi=lambda acc, u: acc * (2.0 * jnp.maximum(u.astype(F32), 0.0)))
    g["w_ff2"] = mm_tn(f"{tag}_dw2", sv["uu"], dh2, pro_x=_relu2)
    g["w_ff1"] = mm_tn(f"{tag}_dw1", sv["hn2"], duu, col_blocks=N_CHIPS)
    dh1, g["norm2_g"] = mm_nt_rmsbwd(f"{tag}_dnorm2", duu, wt["w_ff1"], sv["h1"], wt["norm2_g"], dh2)

    dy = mm_nt(f"{tag}_dout", dh1, wt["w_out"])
    g["w_out"] = mm_tn(f"{tag}_dwo", sv["y"], dh1)
    dab, dac, dah, g["conv_w"] = conv_bwd(f"{tag}_dconv", dy, sv["proj"], wt["conv_w"])
    dq, dk, dv, g["gq"], g["gk"] = attn_bwd(f"{tag}_dattn", dy, sv["proj"], sv["yb_tot"], wt["gq"], wt["gk"])
    dcu, dcv, g["sgu_w"], g["sgu_bias"], g["gv"] = sgu_bwd(f"{tag}_dsgu", dy, sv["proj"], wt["gv"], wt["sgu_w"],
                                                           wt["sgu_bias"])
    dproj = jnp.concatenate([dab, dac, dah, dq, dk, dv, dcu, dcv], axis=-1)
    g["w_in"] = mm_tn(f"{tag}_dwi", sv["hn1"], dproj, tn=1408)
    dh0, g["norm1_g"] = mm_nt_rmsbwd(f"{tag}_dnorm1", dproj, wt["w_in"], sv["h0"], wt["norm1_g"], dh1)
    return dh0, g


def prep_small(norm1_g, q_norm_g, k_norm_g, sgu_norm_g, sgu_w, sgu_b, norm2_g, norm3_g, conv_w_full):
    return dict(
        norm1_g=norm1_g, norm2_g=norm2_g, norm3_g=norm3_g, conv_w=conv_w_full,
        gq=(jnp.tile(q_norm_g, 2) * SCALE).reshape(1, LANES), gk=jnp.tile(k_norm_g, 2).reshape(1, LANES),
        gv=sgu_norm_g.reshape(1, SGU_W), sgu_w=sgu_w, sgu_bias=jnp.repeat(sgu_b.T, HEAD_DIM, axis=1))


def small_grads(g):
    return dict(
        norm1_g=g["norm1_g"][0], norm2_g=g["norm2_g"][0], norm3_g=g["norm3_g"][0], conv_w=g["conv_w"],
        q_norm_g=g["gq"][0, :HEAD_DIM], k_norm_g=g["gk"][0, :HEAD_DIM], sgu_norm_g=g["gv"][0], sgu_w=g["sgu_w"],
        sgu_b=g["sgu_bias"][:, :SGU_HEADS].T)


HBM_SPEC = pl.BlockSpec(memory_space=pltpu.HBM)
BIG = ("w_in", "w_out", "w_ff1", "w_ff2", "w_ple_gate", "w_ple_proj")


def _mesh_pos():
    return lax.axis_index("x"), lax.axis_index("y"), lax.axis_index("c")


def _other_chips(x, y):
    return [(1 - x, y), (x, 1 - y), (1 - x, 1 - y)]


def _half(rows, core):
    h = rows // 2
    return pl.ds(pl.multiple_of(core * h, 16), h)


def _remote(src, dst, send_sems, recv_sems, k, to):
    return pltpu.make_async_remote_copy(src_ref=src, dst_ref=dst, send_sem=send_sems.at[k], recv_sem=recv_sems.at[k],
                                        device_id=to, device_id_type=MESH)


SEM_SPEC = pl.BlockSpec(memory_space=pltpu.SEMAPHORE)
ANY_SPEC = pl.BlockSpec(memory_space=pl.ANY)
SIDE_EFFECT = pltpu.SideEffectType.DATAFLOW_SIDE_EFFECTING


def _in_hbm(arrays):
    return [pltpu.with_memory_space_constraint(a, pltpu.HBM) for a in arrays]


def copies_start(name, srcs, lands, plan, after=()):
    n, na = len(srcs), len(after)

    def body(*refs):
        src_refs, land_refs = refs[:n], refs[n:2 * n]
        send_sem, recv_sem = refs[2 * n + na], refs[2 * n + na + 1]
        token = refs[-1]
        x, y, c = _mesh_pos()
        for i in range(n):
            for src, dst, dev in plan(i, src_refs[i], land_refs[i], x, y, c):
                pltpu.make_async_remote_copy(src_ref=src, dst_ref=dst, send_sem=send_sem, recv_sem=recv_sem,
                                             device_id=dev, device_id_type=MESH).start()
        token[...] = jnp.zeros_like(token)

    out = pl.pallas_call(
        body, name=name,
        in_specs=[HBM_SPEC] * (2 * n) + [ANY_SPEC] * na,
        out_specs=(SEM_SPEC, SEM_SPEC, *[HBM_SPEC] * (2 * n), pl.BlockSpec(memory_space=pltpu.VMEM)),
        out_shape=(pltpu.SemaphoreType.DMA(()), pltpu.SemaphoreType.DMA(()),
                   *[pltpu.HBM(a.shape, a.dtype) for a in (*srcs, *lands)], _sds((8, LANES), F32)),
        input_output_aliases={i: 2 + i for i in range(2 * n)},
        compiler_params=pltpu.CompilerParams(has_side_effects=SIDE_EFFECT),
    )(*_in_hbm(srcs), *_in_hbm(lands), *after)
    return out[0], out[1], list(out[2:2 + n]), list(out[2 + n:2 + 2 * n]), out[-1]


def copies_wait(name, started, plan, after=()):
    send_sem, recv_sem, srcs, lands, _ = started
    n, na = len(srcs), len(after)

    def body(*refs):
        src_refs, land_refs = refs[:n], refs[n:2 * n]
        send_sem, recv_sem = refs[2 * n], refs[2 * n + 1]
        x, y, c = _mesh_pos()
        for i in range(n):
            for src, dst, dev in plan(i, src_refs[i], land_refs[i], x, y, c):
                cp = pltpu.make_async_remote_copy(src_ref=src, dst_ref=dst, send_sem=send_sem, recv_sem=recv_sem,
                                                  device_id=dev, device_id_type=MESH)
                cp.wait_send()
                cp.wait_recv()

    out = pl.pallas_call(
        body, name=name,
        in_specs=[HBM_SPEC] * (2 * n) + [SEM_SPEC, SEM_SPEC] + [ANY_SPEC] * na,
        out_specs=[HBM_SPEC] * (2 * n),
        out_shape=[pltpu.HBM(a.shape, a.dtype) for a in (*srcs, *lands)],
        input_output_aliases={i: i for i in range(2 * n)},
        compiler_params=pltpu.CompilerParams(has_side_effects=SIDE_EFFECT),
    )(*srcs, *lands, send_sem, recv_sem, *after)
    return list(out[:n]), list(out[n:])


def _gather_plan(shapes):
    def plan(i, src, land, x, y, c):
        me = 2 * x + y
        rows = _half(shapes[i][0], c)
        return [(src.at[rows], land.at[me, rows], (*chip, c)) for chip in _other_chips(x, y)]
    return plan


def _gather_arrivals(shapes):
    def plan(i, src, land, x, y, c):
        rows = _half(shapes[i][0], c)
        return [(src.at[rows], land.at[2 * chip[0] + chip[1], rows], (*chip, c)) for chip in _other_chips(x, y)]
    return plan


def gather_finish(name, shards, lands):
    n = len(shards)

    def body(*refs):
        ins, outs = refs[:n], refs[2 * n:3 * n]
        send_sems, recv_sems = refs[3 * n:]
        x, y, c = _mesh_pos()
        me = 2 * x + y
        chips = _other_chips(x, y)
        sibling = (x, y, 1 - c)
        sends = []
        for i in range(n):
            rows = shards[i].shape[0]
            cp = _remote(ins[i], outs[i].at[me], send_sems, recv_sems, 4 * i + 3, sibling)
            cp.start()
            sends.append(cp)
            for k, chip in enumerate(chips):
                region = outs[i].at[2 * chip[0] + chip[1], _half(rows, c)]
                cp = _remote(region, region, send_sems, recv_sems, 4 * i + k, sibling)
                cp.start()
                sends.append(cp)
        for i in range(n):
            rows = shards[i].shape[0]
            _remote(ins[i], outs[i].at[me], send_sems, recv_sems, 4 * i + 3, sibling).wait_recv()
            for k, chip in enumerate(chips):
                region = outs[i].at[2 * chip[0] + chip[1], _half(rows, 1 - c)]
                _remote(region, region, send_sems, recv_sems, 4 * i + k, sibling).wait_recv()
        for cp in sends:
            cp.wait_send()

    return pl.pallas_call(
        body, name=name, in_specs=[HBM_SPEC] * (2 * n), out_specs=[HBM_SPEC] * n,
        out_shape=[_sds(a.shape, a.dtype) for a in lands],
        input_output_aliases={n + i: i for i in range(n)},
        scratch_shapes=[pltpu.SemaphoreType.DMA((4 * n,)), pltpu.SemaphoreType.DMA((4 * n,))],
    )(*shards, *lands)


def exchange_other_half(name, grads):
    n = len(grads)

    def body(*refs):
        ins, outs = refs[:n], refs[n:2 * n]
        send_sems, recv_sems = refs[2 * n:]
        x, y, c = _mesh_pos()
        copies = []
        for i in range(n):
            rows = grads[i].shape[1]
            cp = _remote(ins[i].at[:, _half(rows, 1 - c)], outs[i], send_sems, recv_sems, i, (x, y, 1 - c))
            cp.start()
            copies.append(cp)
        for cp in copies:
            cp.wait()

    return pl.pallas_call(
        body, name=name, in_specs=[HBM_SPEC] * n, out_specs=[HBM_SPEC] * n,
        out_shape=[_sds((N_CHIPS, g.shape[1] // 2, g.shape[2]), g.dtype) for g in grads],
        scratch_shapes=[pltpu.SemaphoreType.DMA((n,)), pltpu.SemaphoreType.DMA((n,))],
    )(*grads)


def add_own_half(name, core, grads, got):
    n = len(grads)

    def body(core_ref, *refs):
        for i in range(n):
            refs[2 * n + i][...] = (refs[i][...].astype(F32) + refs[n + i][...].astype(F32)).astype(BF)

    def spec(g, own):
        blk = (None, g.shape[1] // 2, g.shape[2])
        return pl.BlockSpec(blk, (lambda j, core_ref: (j, core_ref[0], 0)) if own else (lambda j, core_ref: (j, 0, 0)))

    return pl.pallas_call(
        body, name=name,
        grid_spec=pltpu.PrefetchScalarGridSpec(
            num_scalar_prefetch=1, grid=(N_CHIPS,),
            in_specs=[spec(g, True) for g in grads] + [spec(g, False) for g in grads],
            out_specs=[spec(g, False) for g in grads]),
        out_shape=[_sds(r.shape, BF) for r in got], compiler_params=_params("parallel"),
    )(core, *grads, *got)


def _chips_plan(i, src, land, x, y, c):
    return [(src.at[2 * chip[0] + chip[1]], land.at[k], (*chip, c)) for k, chip in enumerate(_other_chips(x, y))]


def sum_chips(name, place, parts, got):
    n = len(got)

    def body(place_ref, *refs):
        for i in range(n):
            acc = refs[i][...].astype(F32)
            for k in range(N_CHIPS - 1):
                acc = acc + refs[n + i][k].astype(F32)
            refs[2 * n + i][...] = acc

    steps = 2
    return pl.pallas_call(
        body, name=name,
        grid_spec=pltpu.PrefetchScalarGridSpec(
            num_scalar_prefetch=1, grid=(steps,),
            in_specs=[pl.BlockSpec((None, g.shape[1] // steps, g.shape[2]), lambda t, place_ref: (place_ref[0], t, 0))
                      for g in parts] +
                     [pl.BlockSpec((N_CHIPS - 1, g.shape[1] // steps, g.shape[2]), lambda t, place_ref: (0, t, 0))
                      for g in got],
            out_specs=[pl.BlockSpec((g.shape[1] // steps, g.shape[2]),
                                    lambda t, place_ref: (place_ref[1] * steps + t, 0)) for g in got]),
        out_shape=[_sds((2 * g.shape[1], g.shape[2]), F32) for g in got], compiler_params=_params("parallel"),
    )(place, *parts, *got)


def exchange_final_halves(name, shards):
    n = len(shards)

    def body(*refs):
        outs = refs[n:2 * n]
        send_sems, recv_sems = refs[2 * n:]
        x, y, c = _mesh_pos()
        copies = []
        for i in range(n):
            mine = outs[i].at[_half(shards[i].shape[0], c)]
            cp = _remote(mine, mine, send_sems, recv_sems, i, (x, y, 1 - c))
            cp.start()
            copies.append(cp)
        for i, cp in enumerate(copies):
            cp.wait_send()
            theirs = outs[i].at[_half(shards[i].shape[0], 1 - c)]
            _remote(theirs, theirs, send_sems, recv_sems, i, (x, y, 1 - c)).wait_recv()

    return pl.pallas_call(
        body, name=name, in_specs=[HBM_SPEC] * n, out_specs=[HBM_SPEC] * n,
        out_shape=[_sds(a.shape, a.dtype) for a in shards], input_output_aliases={i: i for i in range(n)},
        scratch_shapes=[pltpu.SemaphoreType.DMA((n,)), pltpu.SemaphoreType.DMA((n,))],
    )(*shards)


def reduce_scatter_begin(tag, core, grads):
    got = exchange_other_half(f"{tag}_rs_pair", grads)
    parts = add_own_half(f"{tag}_rs_add", core, grads, got)
    lands = [lax.empty((N_CHIPS - 1,) + p.shape[1:], p.dtype) for p in parts]
    return copies_start(f"{tag}_rs_start", parts, lands, _chips_plan)


def reduce_scatter_end(tag, place, started, after):
    parts, got = copies_wait(f"{tag}_rs_wait", started, _chips_plan, after)
    return exchange_final_halves(f"{tag}_rs_join", sum_chips(f"{tag}_rs_sum", place, parts, got))


def small_allreduce(name, x):
    R = x.shape[0]
    H = R // 2

    def body(x_ref, o_ref, pair_ref, chip_ref, send_sems, recv_sems):
        xx, yy, c = _mesh_pos()
        me = 2 * xx + yy
        chips = _other_chips(xx, yy)
        sibling = (xx, yy, 1 - c)
        mine = pl.ds(pl.multiple_of(c * H, 8), H)
        theirs = pl.ds(pl.multiple_of((1 - c) * H, 8), H)
        a = _remote(x_ref.at[theirs], pair_ref.at[theirs], send_sems, recv_sems, 0, sibling)
        a.start()
        a.wait_send()
        _remote(x_ref.at[mine], pair_ref.at[mine], send_sems, recv_sems, 0, sibling).wait_recv()
        chip_ref[me] = x_ref[mine, :] + pair_ref[mine, :]
        sends = []
        for k, chip in enumerate(chips):
            cp = _remote(chip_ref.at[me], chip_ref.at[me], send_sems, recv_sems, 1 + k, (*chip, c))
            cp.start()
            sends.append(cp)
        for k, chip in enumerate(chips):
            slot = chip_ref.at[2 * chip[0] + chip[1]]
            _remote(slot, slot, send_sems, recv_sems, 1 + k, (*chip, c)).wait_recv()
        o_ref[mine, :] = (chip_ref[0] + chip_ref[1]) + (chip_ref[2] + chip_ref[3])
        b = _remote(o_ref.at[mine], o_ref.at[mine], send_sems, recv_sems, 4, sibling)
        b.start()
        b.wait_send()
        _remote(o_ref.at[theirs], o_ref.at[theirs], send_sems, recv_sems, 4, sibling).wait_recv()
        for cp in sends:
            cp.wait_send()

    return pl.pallas_call(
        body, name=name,
        in_specs=[pl.BlockSpec(memory_space=pltpu.VMEM)], out_specs=pl.BlockSpec(memory_space=pltpu.VMEM),
        out_shape=_sds((R, LANES), F32),
        scratch_shapes=[pltpu.VMEM((R, LANES), F32), pltpu.VMEM((N_CHIPS, H, LANES), F32),
                        pltpu.SemaphoreType.DMA((5,)), pltpu.SemaphoreType.DMA((5,))],
        compiler_params=pltpu.CompilerParams(vmem_limit_bytes=VMEM_LIMIT),
    )(x)


WEIGHTS = ("norm1_g", "w_in", "conv_w", "q_norm_g", "k_norm_g", "sgu_norm_g", "sgu_w", "sgu_b", "w_out", "norm2_g",
           "w_ff1", "w_ff2", "norm3_g", "w_ple_gate", "w_ple_proj")
SMALL = ("norm1_g", "norm2_g", "norm3_g", "q_norm_g", "k_norm_g", "sgu_norm_g", "sgu_w", "sgu_b", "conv_w")


def _pack_rows(arrays):
    flat = []
    for a in arrays:
        v = a.reshape(-1)
        flat.append(jnp.pad(v, (0, (-v.shape[0]) % LANES)))
    v = jnp.concatenate(flat)
    v = jnp.pad(v, (0, (-v.shape[0]) % (16 * LANES)))
    return v.reshape(-1, LANES)


def _unpack_rows(packed, shapes):
    out, pos = [], 0
    flat = packed.reshape(-1)
    for shp in shapes:
        size = math.prod(shp)
        out.append(flat[pos:pos + size].reshape(shp))
        pos += size + (-size) % LANES
    return out


def kernel(x, p, norm1_g, w_in, conv_w, q_norm_g, k_norm_g, sgu_norm_g, sgu_w, sgu_b, w_out, norm2_g, w_ff1, w_ff2, norm3_g, w_ple_gate, w_ple_proj, loss_target, m_norm1_g, m_w_in, m_conv_w, m_q_norm_g, m_k_norm_g, m_sgu_norm_g, m_sgu_w, m_sgu_b, m_w_out, m_norm2_g, m_w_ff1, m_w_ff2, m_norm3_g, m_w_ple_gate, m_w_ple_proj, v_norm1_g, v_w_in, v_conv_w, v_q_norm_g, v_k_norm_g, v_sgu_norm_g, v_sgu_w, v_sgu_b, v_w_out, v_norm2_g, v_w_ff1, v_w_ff2, v_norm3_g, v_w_ple_gate, v_w_ple_proj):
    w = dict(norm1_g=norm1_g, w_in=w_in, conv_w=conv_w, q_norm_g=q_norm_g, k_norm_g=k_norm_g, sgu_norm_g=sgu_norm_g,
             sgu_w=sgu_w, sgu_b=sgu_b, w_out=w_out, norm2_g=norm2_g, w_ff1=w_ff1, w_ff2=w_ff2, norm3_g=norm3_g,
             w_ple_gate=w_ple_gate, w_ple_proj=w_ple_proj)
    m = dict(norm1_g=m_norm1_g, w_in=m_w_in, conv_w=m_conv_w, q_norm_g=m_q_norm_g, k_norm_g=m_k_norm_g,
             sgu_norm_g=m_sgu_norm_g, sgu_w=m_sgu_w, sgu_b=m_sgu_b, w_out=m_w_out, norm2_g=m_norm2_g, w_ff1=m_w_ff1,
             w_ff2=m_w_ff2, norm3_g=m_norm3_g, w_ple_gate=m_w_ple_gate, w_ple_proj=m_w_ple_proj)
    v = dict(norm1_g=v_norm1_g, w_in=v_w_in, conv_w=v_conv_w, q_norm_g=v_q_norm_g, k_norm_g=v_k_norm_g,
             sgu_norm_g=v_sgu_norm_g, sgu_w=v_sgu_w, sgu_b=v_sgu_b, w_out=v_w_out, norm2_g=v_norm2_g, w_ff1=v_w_ff1,
             w_ff2=v_w_ff2, norm3_g=v_norm3_g, w_ple_gate=v_w_ple_gate, w_ple_proj=v_w_ple_proj)
    depth = w_in.shape[0]
    d_model = x.shape[-1]
    chip = 2 * lax.axis_index("x") + lax.axis_index("y")
    core = lax.axis_index("c")
    core_arr = core.reshape(1).astype(jnp.int32)

    cw_cols = conv_w.shape[-1]
    placed = lax.dynamic_update_slice(jnp.zeros((depth, 3, CONV_W), F32), conv_w, (0, 0, chip * cw_cols))
    placed = jnp.where(core == 0, placed, 0.0)
    conv_full = _unpack_rows(small_allreduce("conv_w_gather", _pack_rows([placed])), [(depth, 3, CONV_W)])[0]

    h = x[0]
    p_bf = p[:, 0].astype(BF)
    saved, full = [], []
    shard_shapes = [w[n].shape[1:] for n in BIG]

    def gather_start(l, after):
        shards = [w[n][l].astype(BF) for n in BIG]
        lands = [lax.empty((N_CHIPS,) + s.shape, BF) for s in shards]
        return copies_start(f"l{l}_gather_start", shards, lands, _gather_plan(shard_shapes), after)

    started = gather_start(0, ())
    for l in range(depth):
        shards, lands = copies_wait(f"l{l}_gather_wait", started, _gather_arrivals(shard_shapes), (h,))
        g_in, g_out, g_ff1, g_ff2, g_gate, g_proj = gather_finish(f"l{l}_gather_finish", shards, lands)
        token = ()
        if l + 1 < depth:
            started = gather_start(l + 1, (g_proj,))
            token = (started[-1],)
        wt = prep_small(norm1_g[l], q_norm_g[l], k_norm_g[l], sgu_norm_g[l], sgu_w[l], sgu_b[l], norm2_g[l], norm3_g[l],
                        conv_full[l])
        wt["w_in"] = jnp.transpose(g_in, (1, 0, 2)).reshape(d_model, -1)
        wt["w_out"] = g_out.reshape(-1, d_model)
        wt["w_ff1"] = g_ff1
        wt["w_ff2"] = g_ff2.reshape(-1, d_model)
        wt["w_ple_gate"] = g_gate.reshape(-1, d_model)
        wt["w_ple_proj"] = g_proj
        h, sv = layer_fwd(f"l{l}", h, p_bf[l], wt, token)
        saved.append(sv)
        full.append(wt)

    loss_tile, dh = loss_head("loss", h, loss_target[0])
    loss = lax.psum(loss_tile[0, 0], ("x", "y", "c"))

    small = [None] * depth
    chip_arr = jnp.stack([chip, core]).astype(jnp.int32)
    big_w, big_m, big_v = ([d[n] for n in BIG] for d in (w, m, v))
    updated = None

    def finish_layer(l, started, after, updated):
        reduced = reduce_scatter_end(f"l{l}", chip_arr, started, after)
        return adamw_layer(f"l{l}_adamw", l, big_w, reduced, big_m, big_v, updated)

    started = None
    for l in reversed(range(depth)):
        dh, g = layer_bwd(f"l{l}", dh, saved[l], full[l], () if started is None else (started[-1],))
        if started is not None:
            updated = finish_layer(l + 1, started, (dh,), updated)
        small[l] = small_grads(g)
        shards_in = w_in.shape[-1]
        gl = [jnp.transpose(g["w_in"].reshape(d_model, N_CHIPS, shards_in), (1, 0, 2)),
              g["w_out"].reshape(N_CHIPS, -1, d_model), g["w_ff1"], g["w_ff2"].reshape(N_CHIPS, -1, d_model),
              g["w_ple_gate"].reshape(N_CHIPS, -1, d_model), g["w_ple_proj"]]
        started = reduce_scatter_begin(f"l{l}", core_arr, gl)

    grads, delta, new_m, new_v = {}, {}, {}, {}
    packed = _pack_rows([small[l][n] for l in range(depth) for n in SMALL])
    shapes = [small[l][n].shape for l in range(depth) for n in SMALL]
    pieces = _unpack_rows(small_allreduce("small_grads", packed), shapes)
    for i, n in enumerate(SMALL):
        grads[n] = jnp.stack([pieces[l * len(SMALL) + i] for l in range(depth)])
    grads["conv_w"] = lax.dynamic_slice(grads["conv_w"], (0, 0, chip * cw_cols), (depth, 3, cw_cols))
    for n in SMALL:
        shp = w[n].shape
        two_d = (-1, shp[-1]) if n != "sgu_w" else (-1, LANES)
        d, nm, nv = adamw(f"adamw_{n}", w[n].reshape(two_d), grads[n].reshape(two_d), m[n].reshape(two_d),
                          v[n].reshape(two_d))
        delta[n], new_m[n], new_v[n] = d.reshape(shp), nm.reshape(shp), nv.reshape(shp)

    updated = finish_layer(0, started, (dh, *[new_v[n] for n in SMALL]), updated)
    for k, d in enumerate((grads, delta, new_m, new_v)):
        d.update(zip(BIG, updated[k]))

    return (loss, dh[None], *[grads[n] for n in WEIGHTS], *[delta[n] for n in WEIGHTS], *[new_m[n] for n in WEIGHTS],
            *[new_v[n] for n in WEIGHTS])
```

```python
import functools
import math

import jax
import jax.numpy as jnp
from jax import lax
from jax.experimental import pallas as pl
from jax.experimental.pallas import tpu as pltpu

F32 = jnp.float32
BF = jnp.bfloat16
MESH = pl.DeviceIdType.MESH
HIGHEST = lax.Precision.HIGHEST

EPS = 1e-6
HEAD_DIM = 64
CONV_W = 256
ATTN_W = 512
SGU_W = 256
CHUNK = 128
N_CHIPS = 4
SCALE = HEAD_DIM ** -0.5
LANES = 128
VMEM_LIMIT = 56 * 1024 * 1024

ADAM_LR = 0.001
ADAM_B1 = 0.9
ADAM_B2 = 0.999
ADAM_EPS = 1e-08
ADAM_WD = 0.01
ADAM_STEP = 10

NT_DIMS = (((1,), (1,)), ((), ()))
TN_DIMS = (((0,), (0,)), ((), ()))


def _params(*sem):
    return pltpu.CompilerParams(dimension_semantics=sem if sem else None, vmem_limit_bytes=VMEM_LIMIT)


def _sds(shape, dtype):
    return jax.ShapeDtypeStruct(shape, dtype)


def _erf(x):
    return lax.erf(x)


def _gelu(x):
    return 0.5 * x * (1.0 + _erf(x * (2.0 ** -0.5)))


def _gelu_grad(x):
    return 0.5 * (1.0 + _erf(x * (2.0 ** -0.5))) + x * jnp.exp(-0.5 * x * x) * (1.0 / math.sqrt(2.0 * math.pi))


def _log_sigmoid(z):
    return jnp.minimum(z, 0.0) - jnp.log(1.0 + jnp.exp(-jnp.abs(z)))


def _head_mean_matrix(width):
    r = lax.broadcasted_iota(jnp.int32, (width, width), 0) // HEAD_DIM
    c = lax.broadcasted_iota(jnp.int32, (width, width), 1) // HEAD_DIM
    return (r == c).astype(BF)


def _head_mean(x, m):
    hi = x.astype(BF)
    lo = (x - hi.astype(F32)).astype(BF)
    return _dot2_stacked(hi, lo, m) * (1.0 / HEAD_DIM)


def rmsnorm(name, h, g, after=()):
    S, D = h.shape
    tm = min(S, 512)

    def body(h_ref, g_ref, *rest):
        x = h_ref[...]
        r = lax.rsqrt(jnp.mean(x * x, axis=-1, keepdims=True) + EPS)
        rest[-1][...] = ((x * r) * g_ref[...]).astype(BF)

    return pl.pallas_call(
        body, name=name, grid=(S // tm,),
        in_specs=[pl.BlockSpec((tm, D), lambda i: (i, 0)), pl.BlockSpec((1, D), lambda i: (0, 0))] +
                 [pl.BlockSpec(memory_space=pl.ANY)] * len(after),
        out_specs=pl.BlockSpec((tm, D), lambda i: (i, 0)),
        out_shape=_sds((S, D), BF), compiler_params=_params("parallel"),
    )(h, g.reshape(1, D), *after)


def mm_nn(name, x, w, *, extras=(), pro=None, epi=None, out_dtypes=None, tm=512, tn=512):
    S, K = x.shape
    if w.ndim == 3:
        J, _, tn = w.shape
        N = J * tn
        w_spec = pl.BlockSpec((None, K, tn), lambda n, m: (n, 0, 0))
    else:
        N = w.shape[1]
        tn = min(tn, N)
        w_spec = pl.BlockSpec((K, tn), lambda n, m: (0, n))
    tm = min(tm, S)
    out_dtypes = (BF,) if out_dtypes is None else out_dtypes
    n_ex, n_out = len(extras), len(out_dtypes)

    def body(x_ref, w_ref, *rest):
        xv = x_ref[...]
        if pro is not None:
            xv = pro(xv)
        acc = jnp.dot(xv.astype(BF), w_ref[...], preferred_element_type=F32)
        outs = (acc,) if epi is None else epi(acc, *[e[...] for e in rest[:n_ex]])
        for o_ref, o in zip(rest[n_ex:], outs):
            o_ref[...] = o.astype(o_ref.dtype)

    tile = pl.BlockSpec((tm, tn), lambda n, m: (m, n))
    out = pl.pallas_call(
        body, name=name, grid=(N // tn, S // tm),
        in_specs=[pl.BlockSpec((tm, K), lambda n, m: (m, 0)), w_spec] + [tile] * n_ex,
        out_specs=[tile] * n_out,
        out_shape=[_sds((S, N), d) for d in out_dtypes],
        compiler_params=_params("parallel", "parallel"),
    )(x, w, *extras)
    return out[0] if n_out == 1 else out


def mm_nt(name, dy, w, *, extras=(), epi=None, tm=256):
    S, N = dy.shape
    K = w.shape[0]
    tm = min(tm, S)
    n_ex = len(extras)

    def body(dy_ref, w_ref, *rest):
        acc = lax.dot_general(dy_ref[...].astype(BF), w_ref[...], NT_DIMS, preferred_element_type=F32)
        if epi is not None:
            acc = epi(acc, *[e[...] for e in rest[:n_ex]])
        rest[n_ex][...] = acc.astype(BF)

    row = pl.BlockSpec((tm, K), lambda i: (i, 0))
    return pl.pallas_call(
        body, name=name, grid=(S // tm,),
        in_specs=[pl.BlockSpec((tm, N), lambda i: (i, 0)), pl.BlockSpec((K, N), lambda i: (0, 0))] + [row] * n_ex,
        out_specs=row, out_shape=_sds((S, K), BF), compiler_params=_params("parallel"),
    )(dy, w, *extras)


def mm_nt_rmsbwd(name, dy, w, h, g, dres, *, tm=256):
    S, N = dy.shape
    D = h.shape[1]
    tm = min(tm, S)
    blocked = w.ndim == 3
    nj = w.shape[2] if blocked else N

    def body(dy_ref, w_ref, h_ref, g_ref, dres_ref, dh_ref, dg_ref):
        i = pl.program_id(0)
        if blocked:
            dyn = None
            for j in range(w.shape[0]):
                part = lax.dot_general(dy_ref[:, j * nj:(j + 1) * nj].astype(BF), w_ref[j], NT_DIMS,
                                       preferred_element_type=F32)
                dyn = part if dyn is None else dyn + part
        else:
            dyn = lax.dot_general(dy_ref[...].astype(BF), w_ref[...], NT_DIMS, preferred_element_type=F32)
        x = h_ref[...]
        r = lax.rsqrt(jnp.mean(x * x, axis=-1, keepdims=True) + EPS)
        t = dyn * g_ref[...]
        dh_ref[...] = dres_ref[...] + r * t - x * (r * r * r) * jnp.mean(t * x, axis=-1, keepdims=True)
        part = jnp.sum(dyn * (x * r), axis=0, keepdims=True)

        @pl.when(i == 0)
        def _():
            dg_ref[...] = part

        @pl.when(i > 0)
        def _():
            dg_ref[...] += part

    w_spec = pl.BlockSpec(w.shape, (lambda i: (0, 0, 0)) if blocked else (lambda i: (0, 0)))
    row = pl.BlockSpec((tm, D), lambda i: (i, 0))
    vec = pl.BlockSpec((1, D), lambda i: (0, 0))
    return pl.pallas_call(
        body, name=name, grid=(S // tm,),
        in_specs=[pl.BlockSpec((tm, N), lambda i: (i, 0)), w_spec, row, vec, row],
        out_specs=[row, vec], out_shape=[_sds((S, D), F32), _sds((1, D), F32)],
        compiler_params=_params("arbitrary"),
    )(dy, w, h, g.reshape(1, D), dres)


def mm_tn(name, x, dy, *, pro_x=None, col_blocks=None, tk=1024, tn=1024):
    S, K = x.shape
    N = dy.shape[1]
    tk = min(tk, K)
    if col_blocks is not None:
        tn = N // col_blocks
        out_shape = _sds((col_blocks, K, tn), BF)
        out_spec = pl.BlockSpec((None, tk, tn), lambda k, n: (n, k, 0))
    else:
        tn = min(tn, N)
        out_shape = _sds((K, N), BF)
        out_spec = pl.BlockSpec((tk, tn), lambda k, n: (k, n))

    def body(x_ref, dy_ref, o_ref):
        xv = x_ref[...]
        if pro_x is not None:
            xv = pro_x(xv)
        o_ref[...] = lax.dot_general(xv.astype(BF), dy_ref[...].astype(BF), TN_DIMS,
                                     preferred_element_type=F32).astype(BF)

    return pl.pallas_call(
        body, name=name, grid=(K // tk, N // tn),
        in_specs=[pl.BlockSpec((S, tk), lambda k, n: (0, k)), pl.BlockSpec((S, tn), lambda k, n: (0, n))],
        out_specs=out_spec, out_shape=out_shape, compiler_params=_params("parallel", "parallel"),
    )(x, dy)


def _conv_parts(ac_ref, ah_ref, cw):
    a_c = ac_ref[...].astype(F32)
    a_h = ah_ref[...].astype(F32)
    x = a_c * a_h
    row = lax.broadcasted_iota(jnp.int32, x.shape, 0)
    x1 = jnp.where(row >= 1, pltpu.roll(x, 1, 0), 0.0)
    x2 = jnp.where(row >= 2, pltpu.roll(x, 2, 0), 0.0)
    cv = cw[0:1] * x2 + cw[1:2] * x1 + cw[2:3] * x
    return a_c, a_h, x, x1, x2, cv, row


def conv_fwd(name, proj, cw):
    S = proj.shape[0]

    def body(ab_ref, ac_ref, ah_ref, cw_ref, o_ref):
        cv = _conv_parts(ac_ref, ah_ref, cw_ref[...])[5]
        o_ref[...] = (ab_ref[...].astype(F32) * cv).astype(BF)

    col = lambda j: pl.BlockSpec((S, CONV_W), lambda i, j=j: (0, j))
    return pl.pallas_call(
        body, name=name, grid=(1,),
        in_specs=[col(0), col(1), col(2), pl.BlockSpec((3, CONV_W), lambda i: (0, 0))],
        out_specs=pl.BlockSpec((S, CONV_W), lambda i: (0, 0)),
        out_shape=_sds((S, CONV_W), BF), compiler_params=_params("arbitrary"),
    )(proj, proj, proj, cw)


def conv_bwd(name, dy, proj, cw):
    S = proj.shape[0]

    def body(dy_ref, ab_ref, ac_ref, ah_ref, cw_ref, dab_ref, dac_ref, dah_ref, dcw_ref):
        w = cw_ref[...]
        a_c, a_h, x, x1, x2, cv, row = _conv_parts(ac_ref, ah_ref, w)
        d = dy_ref[...].astype(F32)
        dab_ref[...] = (d * cv).astype(BF)
        dcv = d * ab_ref[...].astype(F32)
        d1 = jnp.where(row < S - 1, pltpu.roll(dcv, S - 1, 0), 0.0)
        d2 = jnp.where(row < S - 2, pltpu.roll(dcv, S - 2, 0), 0.0)
        dx = w[2:3] * dcv + w[1:2] * d1 + w[0:1] * d2
        dac_ref[...] = (dx * a_h).astype(BF)
        dah_ref[...] = (dx * a_c).astype(BF)
        dcw_ref[0:1, :] = jnp.sum(dcv * x2, axis=0, keepdims=True)
        dcw_ref[1:2, :] = jnp.sum(dcv * x1, axis=0, keepdims=True)
        dcw_ref[2:3, :] = jnp.sum(dcv * x, axis=0, keepdims=True)

    col = lambda j: pl.BlockSpec((S, CONV_W), lambda i, j=j: (0, j))
    one = pl.BlockSpec((S, CONV_W), lambda i: (0, 0))
    small = pl.BlockSpec((3, CONV_W), lambda i: (0, 0))
    return pl.pallas_call(
        body, name=name, grid=(1,),
        in_specs=[col(0), col(0), col(1), col(2), small],
        out_specs=[one, one, one, small],
        out_shape=[_sds((S, CONV_W), BF)] * 3 + [_sds((3, CONV_W), F32)],
        compiler_params=_params("arbitrary"),
    )(dy, proj, proj, proj, cw)


SGU_HEADS = SGU_W // HEAD_DIM
CU_BLOCK = 2304 // SGU_W
CV_BLOCK = 2560 // SGU_W


def _sgu_common(cu_ref, cv_ref, gv_ref, tm):
    c_u = cu_ref[...].astype(F32)
    c_v = cv_ref[...].astype(F32)
    hm = _head_mean_matrix(SGU_W)
    u = _gelu(c_u)
    vg = _gelu(c_v)
    r = lax.rsqrt(_head_mean(vg * vg, hm) + EPS)
    vv = (vg * r) * gv_ref[...]
    head = lax.broadcasted_iota(jnp.int32, (CHUNK, SGU_W), 1) // HEAD_DIM
    tri = (lax.broadcasted_iota(jnp.int32, (CHUNK, CHUNK), 0) >=
           lax.broadcasted_iota(jnp.int32, (CHUNK, CHUNK), 1))
    return c_u, c_v, hm, u, vg, r, vv, head, tri


def _sgu_mix(w_ref, tri, head, vvc, bias):
    sv = bias
    for g in range(SGU_HEADS):
        wg = jnp.where(tri, w_ref[g], 0.0).astype(BF)
        sv = sv + jnp.where(head == g, jnp.dot(wg, vvc, preferred_element_type=F32), 0.0)
    return sv


def sgu_fwd(name, proj, gv, w, bias):
    S = proj.shape[0]
    tm = min(S, 512)

    def body(cu_ref, cv_ref, gv_ref, w_ref, b_ref, o_ref):
        _, _, _, u, _, _, vv, head, tri = _sgu_common(cu_ref, cv_ref, gv_ref, tm)
        vvb = vv.astype(BF)
        for ch in range(tm // CHUNK):
            rows = slice(ch * CHUNK, (ch + 1) * CHUNK)
            sv = _sgu_mix(w_ref, tri, head, vvb[rows], b_ref[...])
            o_ref[rows, :] = (u[rows] * sv).astype(BF)

    const = lambda shape: pl.BlockSpec(shape, lambda i: (0,) * len(shape))
    return pl.pallas_call(
        body, name=name, grid=(S // tm,),
        in_specs=[pl.BlockSpec((tm, SGU_W), lambda i: (i, CU_BLOCK)), pl.BlockSpec((tm, SGU_W), lambda i: (i, CV_BLOCK)),
                  const((1, SGU_W)), const((SGU_HEADS, CHUNK, CHUNK)), const((CHUNK, SGU_W))],
        out_specs=pl.BlockSpec((tm, SGU_W), lambda i: (i, 0)),
        out_shape=_sds((S, SGU_W), BF), compiler_params=_params("parallel"),
    )(proj, proj, gv, w, bias)


def sgu_bwd(name, dy, proj, gv, w, bias):
    S = proj.shape[0]
    tm = min(S, 512)

    def body(dy_ref, cu_ref, cv_ref, gv_ref, w_ref, b_ref, dcu_ref, dcv_ref, dw_ref, db_ref, dgv_ref, dvv_s):
        i = pl.program_id(0)
        c_u, c_v, hm, u, vg, r, vv, head, tri = _sgu_common(cu_ref, cv_ref, gv_ref, tm)
        vvb = vv.astype(BF)
        d = dy_ref[...].astype(F32)
        ind = (lax.broadcasted_iota(jnp.int32, (SGU_W, LANES), 0) // HEAD_DIM ==
               lax.broadcasted_iota(jnp.int32, (SGU_W, LANES), 1)).astype(BF)
        dw_acc = [jnp.zeros((CHUNK, CHUNK), F32) for _ in range(SGU_HEADS)]
        db_acc = jnp.zeros((CHUNK, LANES), F32)
        for ch in range(tm // CHUNK):
            rows = slice(ch * CHUNK, (ch + 1) * CHUNK)
            sv = _sgu_mix(w_ref, tri, head, vvb[rows], b_ref[...])
            dcu_ref[rows, :] = (d[rows] * sv * _gelu_grad(c_u[rows])).astype(BF)
            dsv = d[rows] * u[rows]
            dsv_hi = dsv.astype(BF)
            db_acc = db_acc + _dot2_stacked(dsv_hi, (dsv - dsv_hi.astype(F32)).astype(BF), ind)
            dvv = jnp.zeros((CHUNK, SGU_W), F32)
            for g in range(SGU_HEADS):
                dsv_g = jnp.where(head == g, dsv, 0.0).astype(BF)
                wg = jnp.where(tri, w_ref[g], 0.0).astype(BF)
                dvv = dvv + lax.dot_general(wg, dsv_g, TN_DIMS, preferred_element_type=F32)
                dw_acc[g] = dw_acc[g] + lax.dot_general(dsv_g, vvb[rows], NT_DIMS, preferred_element_type=F32)
            dvv_s[rows, :] = dvv
        dvv = dvv_s[...]
        gvv = gv_ref[...]
        t = dvv * gvv
        dvg = r * t - vg * (r * r * r) * _head_mean(t * vg, hm)
        dcv_ref[...] = (dvg * _gelu_grad(c_v)).astype(BF)
        dgv = jnp.sum(dvv * (vg * r), axis=0, keepdims=True)

        @pl.when(i == 0)
        def _():
            for g in range(SGU_HEADS):
                dw_ref[g] = jnp.where(tri, dw_acc[g], 0.0)
            db_ref[...] = db_acc
            dgv_ref[...] = dgv

        @pl.when(i > 0)
        def _():
            for g in range(SGU_HEADS):
                dw_ref[g] += jnp.where(tri, dw_acc[g], 0.0)
            db_ref[...] += db_acc
            dgv_ref[...] += dgv

    const = lambda shape: pl.BlockSpec(shape, lambda i: (0,) * len(shape))
    tile = pl.BlockSpec((tm, SGU_W), lambda i: (i, 0))
    return pl.pallas_call(
        body, name=name, grid=(S // tm,),
        in_specs=[pl.BlockSpec((tm, SGU_W), lambda i: (i, 3)),
                  pl.BlockSpec((tm, SGU_W), lambda i: (i, CU_BLOCK)), pl.BlockSpec((tm, SGU_W), lambda i: (i, CV_BLOCK)),
                  const((1, SGU_W)), const((SGU_HEADS, CHUNK, CHUNK)), const((CHUNK, SGU_W))],
        out_specs=[tile, tile, const((SGU_HEADS, CHUNK, CHUNK)), const((CHUNK, LANES)), const((1, SGU_W))],
        out_shape=[_sds((S, SGU_W), BF), _sds((S, SGU_W), BF), _sds((SGU_HEADS, CHUNK, CHUNK), F32),
                   _sds((CHUNK, LANES), F32), _sds((1, SGU_W), F32)],
        scratch_shapes=[pltpu.VMEM((tm, SGU_W), F32)],
        compiler_params=_params("arbitrary"),
    )(dy, proj, proj, gv, w, bias)


HEAD_PAIRS = ATTN_W // LANES
Q_BLOCK0 = 768 // LANES
K_BLOCK0 = 1280 // LANES
V_BLOCK0 = 1792 // LANES


def _attn_tile(S):
    return min(S, 256)


def _qk_norm(x, g, hm):
    r = lax.rsqrt(_head_mean(x * x, hm) + EPS)
    return r, (x * r) * g


MASKED = -1e30


def _logit_parts(z):
    lb = _log_sigmoid(z)
    lr = lb - z
    hi = lr.astype(BF)
    return lb, hi, (lr - hi.astype(F32)).astype(BF)


def _stack_heads(x, lane):
    return jnp.concatenate([jnp.where(lane < HEAD_DIM, x, 0.0), jnp.where(lane >= HEAD_DIM, x, 0.0)],
                           axis=0).astype(BF)


def _dot2_stacked(hi, lo, u):
    rows = hi.shape[0]
    both = jnp.dot(jnp.concatenate([hi, lo], axis=0), u, preferred_element_type=F32)
    return both[:rows] + both[rows:]


def attn_fwd(name, proj, gq, gk):
    S = proj.shape[0]
    T = _attn_tile(S)
    nq = S // T

    def body(q_ref, k_ref, v_ref, gq_ref, gk_ref, o_ref, tot_ref, kn_s, lb_s, hi_s, lo_s, z_s, a_s, o_s):
        qi = pl.program_id(1)
        hm = _head_mean_matrix(LANES)

        @pl.when(qi == 0)
        def _():
            kn_s[...] = _qk_norm(k_ref[...].astype(F32), gk_ref[...], hm)[1].astype(BF)

        qn = _qk_norm(q_ref[...].astype(F32), gq_ref[...], hm)[1]
        lane = lax.broadcasted_iota(jnp.int32, (T, LANES), 1)
        qst = _stack_heads(qn, lane)
        rowi = lax.broadcasted_iota(jnp.int32, (T, T), 0)
        coli = lax.broadcasted_iota(jnp.int32, (T, T), 1)
        u_excl = (rowi > coli).astype(BF)
        diagonal = jnp.where(coli < rowi, 0.0, MASKED)

        def logits(j):
            return lax.dot_general(qst, kn_s[pl.ds(pl.multiple_of(j * T, T), T), :], NT_DIMS,
                                   preferred_element_type=F32)

        def values(j):
            return v_ref[pl.ds(pl.multiple_of(j * T, T), T), :].astype(BF)

        def keep(slot, z):
            lb_s[slot], hi_s[...], lo_s[...] = _logit_parts(z)

        def step(it, carry):
            run = carry
            j = qi - it
            hi, lo = hi_s[...], lo_s[...]
            both = jnp.dot(jnp.concatenate([hi, lo], axis=0), u_excl, preferred_element_type=F32)
            o_s[...] += jnp.dot(a_s[...], values(jnp.minimum(j + 1, qi)), preferred_element_type=F32)
            z_after = logits(jnp.maximum(j - 2, 0))
            first = hi[:, 0:1].astype(F32) + lo[:, 0:1].astype(F32)
            keep((it + 1) % 2, z_s[...])
            later = both[:2 * T] + both[2 * T:]
            a_s[...] = jnp.exp(lb_s[it % 2] + later + run).astype(BF)
            z_s[...] = z_after
            return run + later[:, 0:1] + first

        keep(0, logits(qi) + jnp.concatenate([diagonal, diagonal], axis=0))
        z_s[...] = logits(jnp.maximum(qi - 1, 0))
        a_s[...] = jnp.zeros_like(a_s)
        o_s[...] = jnp.zeros_like(o_s)
        run = lax.fori_loop(0, qi + 1, step, jnp.zeros((2 * T, 1), F32))
        o = o_s[...] + jnp.dot(a_s[...], values(0), preferred_element_type=F32)
        o_ref[...] = jnp.where(lane < HEAD_DIM, o[:T], o[T:]).astype(BF)
        tot_ref[...] = jnp.where(lane < HEAD_DIM, run[:T], run[T:])

    gain = pl.BlockSpec((1, LANES), lambda hp, qi: (0, 0))
    full = lambda b0: pl.BlockSpec((S, LANES), lambda hp, qi, b0=b0: (0, b0 + hp))
    tile = pl.BlockSpec((T, LANES), lambda hp, qi: (qi, hp))
    return pl.pallas_call(
        body, name=name, grid=(HEAD_PAIRS, nq),
        in_specs=[pl.BlockSpec((T, LANES), lambda hp, qi: (qi, Q_BLOCK0 + hp)), full(K_BLOCK0), full(V_BLOCK0), gain, gain],
        out_specs=[tile, tile],
        out_shape=[_sds((S, ATTN_W), BF), _sds((S, ATTN_W), F32)],
        scratch_shapes=[pltpu.VMEM((S, LANES), BF), pltpu.VMEM((2, 2 * T, T), F32), pltpu.VMEM((2 * T, T), BF),
                        pltpu.VMEM((2 * T, T), BF), pltpu.VMEM((2 * T, T), F32), pltpu.VMEM((2 * T, T), BF),
                        pltpu.VMEM((2 * T, LANES), F32)],
        compiler_params=_params("arbitrary", "arbitrary"),
    )(proj, proj, proj, gq, gk)


def attn_bwd(name, dy, proj, tot, gq, gk):
    S = proj.shape[0]
    T = _attn_tile(S)
    nq = S // T

    def body(q_ref, k_ref, v_ref, tot_ref, do_ref, gq_ref, gk_ref,
             dq_ref, dk_ref, dv_ref, dgq_ref, dgk_ref, kn_s, dkn_s, dv_s,
             lb_s, z_s, g_s, dq_s, hi_s, lo_s, a_s, ghi_s, glo_s):
        hp = pl.program_id(0)
        qi = pl.program_id(1)
        hm = _head_mean_matrix(LANES)

        @pl.when(qi == 0)
        def _():
            kn_s[...] = _qk_norm(k_ref[...].astype(F32), gk_ref[...], hm)[1].astype(BF)
            dkn_s[...] = jnp.zeros_like(dkn_s)
            dv_s[...] = jnp.zeros_like(dv_s)

        q = q_ref[...].astype(F32)
        rq, qn = _qk_norm(q, gq_ref[...], hm)
        lane = lax.broadcasted_iota(jnp.int32, (T, LANES), 1)
        qst = _stack_heads(qn, lane)
        dost = _stack_heads(do_ref[...].astype(F32), lane)
        total = jnp.concatenate([tot_ref[:, 0:1], tot_ref[:, HEAD_DIM:HEAD_DIM + 1]], axis=0)
        rowi = lax.broadcasted_iota(jnp.int32, (T, T), 0)
        coli = lax.broadcasted_iota(jnp.int32, (T, T), 1)
        u_upto = (rowi <= coli).astype(BF)
        u_before = (rowi < coli).astype(BF)
        diagonal = jnp.where(coli < rowi, 0.0, MASKED)

        on_diagonal = jnp.concatenate([diagonal, diagonal], axis=0)

        def rows(b):
            return pl.ds(pl.multiple_of(jnp.clip(b, 0, qi) * T, T), T)

        def logits(b):
            return lax.dot_general(qst, kn_s[rows(b), :], NT_DIMS, preferred_element_type=F32)

        def keep(b, z):
            bias = jnp.where(b == qi, on_diagonal, jnp.where(b > qi, MASKED, 0.0))
            lb_s[b % 3], hi_s[...], lo_s[...] = _logit_parts(z + bias)

        def step(i, carry):
            run, grun = carry
            both_before = jnp.dot(jnp.concatenate([ghi_s[...], glo_s[...]], axis=0), u_before,
                                  preferred_element_type=F32)
            both_upto = jnp.dot(jnp.concatenate([hi_s[...], lo_s[...]], axis=0), u_upto, preferred_element_type=F32)
            da = lax.dot_general(dost, v_ref[rows(i), :].astype(BF), NT_DIMS, preferred_element_type=F32)
            dv_s[rows(i - 1), :] += lax.dot_general(a_s[...], dost, TN_DIMS, preferred_element_type=F32)
            z_after = logits(i + 2)

            keep(i + 1, z_s[...])

            g = g_s[...]
            before = both_before[:2 * T] + both_before[2 * T:]
            dz = (g - jnp.exp(lb_s[(i + 2) % 3]) * (g + (grun + before))).astype(BF)
            dq_s[...] += jnp.dot(dz, kn_s[rows(i - 1), :], preferred_element_type=F32)
            dkn_s[rows(i - 1), :] += lax.dot_general(dz, qst, TN_DIMS, preferred_element_type=F32)
            grun = grun + before[:, T - 1:T] + g[:, T - 1:T]

            upto = both_upto[:2 * T] + both_upto[2 * T:]
            a = jnp.exp(lb_s[i % 3] + (total - run - upto))
            g = da * a
            a_s[...] = a.astype(BF)
            g_s[...] = g
            ghi = g.astype(BF)
            ghi_s[...] = ghi
            glo_s[...] = (g - ghi.astype(F32)).astype(BF)
            z_s[...] = z_after
            return run + upto[:, T - 1:T], grun

        lb_s[...] = jnp.full(lb_s.shape, MASKED, F32)
        for ref in (a_s, g_s, ghi_s, glo_s, dq_s):
            ref[...] = jnp.zeros_like(ref)
        keep(0, logits(0))
        z_s[...] = logits(1)
        lax.fori_loop(0, qi + 2, step, (jnp.zeros((2 * T, 1), F32), jnp.zeros((2 * T, 1), F32)))
        dqn = jnp.where(lane < HEAD_DIM, dq_s[:T, :], dq_s[T:, :])
        gq_v = gq_ref[...]
        t = dqn * gq_v
        dq_ref[...] = (rq * t - q * (rq * rq * rq) * _head_mean(t * q, hm)).astype(BF)
        dgq = jnp.sum(dqn * (q * rq), axis=0, keepdims=True) * SCALE
        first = jnp.logical_and(hp == 0, qi == 0)

        @pl.when(first)
        def _():
            dgq_ref[...] = dgq

        @pl.when(jnp.logical_not(first))
        def _():
            dgq_ref[...] += dgq

        @pl.when(qi == nq - 1)
        def _():
            k = k_ref[...].astype(F32)
            rk = _qk_norm(k, gk_ref[...], hm)[0]
            dkn = dkn_s[...]
            tk = dkn * gk_ref[...]
            dk_ref[...] = (rk * tk - k * (rk * rk * rk) * _head_mean(tk * k, hm)).astype(BF)
            dgk = jnp.sum(dkn * (k * rk), axis=0, keepdims=True)
            dv_ref[...] = dv_s[...].astype(BF)

            @pl.when(hp == 0)
            def _():
                dgk_ref[...] = dgk

            @pl.when(hp > 0)
            def _():
                dgk_ref[...] += dgk

            @pl.when(hp == HEAD_PAIRS - 1)
            def _():
                fold = (lax.broadcasted_iota(jnp.int32, (LANES, LANES), 0) % HEAD_DIM ==
                        lax.broadcasted_iota(jnp.int32, (LANES, LANES), 1) % HEAD_DIM).astype(F32)
                dgq_ref[...] = jnp.dot(dgq_ref[...], fold, precision=HIGHEST, preferred_element_type=F32)
                dgk_ref[...] = jnp.dot(dgk_ref[...], fold, precision=HIGHEST, preferred_element_type=F32)

    gain = pl.BlockSpec((1, LANES), lambda hp, qi: (0, 0))
    full = lambda b0: pl.BlockSpec((S, LANES), lambda hp, qi, b0=b0: (0, b0 + hp))
    tile = pl.BlockSpec((T, LANES), lambda hp, qi: (qi, hp))
    col = pl.BlockSpec((S, LANES), lambda hp, qi: (0, hp))
    dgain = pl.BlockSpec((1, LANES), lambda hp, qi: (0, 0))
    return pl.pallas_call(
        body, name=name, grid=(HEAD_PAIRS, nq),
        in_specs=[pl.BlockSpec((T, LANES), lambda hp, qi: (qi, Q_BLOCK0 + hp)), full(K_BLOCK0), full(V_BLOCK0),
                  tile, pl.BlockSpec((T, LANES), lambda hp, qi: (qi, 2 + hp)), gain, gain],
        out_specs=[tile, col, col, dgain, dgain],
        out_shape=[_sds((S, ATTN_W), BF)] * 3 + [_sds((1, LANES), F32)] * 2,
        scratch_shapes=[pltpu.VMEM((S, LANES), BF), pltpu.VMEM((S, LANES), F32), pltpu.VMEM((S, LANES), F32),
                        pltpu.VMEM((3, 2 * T, T), F32), pltpu.VMEM((2 * T, T), F32), pltpu.VMEM((2 * T, T), F32),
                        pltpu.VMEM((2 * T, LANES), F32)] + [pltpu.VMEM((2 * T, T), BF)] * 5,
        compiler_params=_params("arbitrary", "arbitrary"),
    )(proj, proj, proj, tot, dy, gq, gk)


def ple_bwd_elem(name, dh, gp, pp, after=()):
    S, D = dh.shape
    tm = min(S, 512)

    def body(dh_ref, gp_ref, pp_ref, *rest):
        dgp_ref, dpp_ref = rest[-2:]
        d = dh_ref[...]
        gate = jax.nn.sigmoid(gp_ref[...].astype(F32))
        dpp_ref[...] = (d * gate).astype(BF)
        dgp_ref[...] = (d * pp_ref[...].astype(F32) * gate * (1.0 - gate)).astype(BF)

    tile = pl.BlockSpec((tm, D), lambda i: (i, 0))
    return pl.pallas_call(
        body, name=name, grid=(S // tm,), in_specs=[tile] * 3 + [pl.BlockSpec(memory_space=pl.ANY)] * len(after),
        out_specs=[tile] * 2, out_shape=[_sds((S, D), BF)] * 2, compiler_params=_params("parallel"),
    )(dh, gp, pp, *after)


def loss_head(name, h, target):
    S, D = h.shape
    tm = min(S, 512)

    def body(h_ref, t_ref, loss_ref, dh_ref):
        i = pl.program_id(0)
        e = h_ref[...] - t_ref[...]
        dh_ref[...] = e * (1.0 / D)
        part = jnp.zeros((8, LANES), F32) + 0.5 * jnp.sum(jnp.mean(e * e, axis=-1, keepdims=True))

        @pl.when(i == 0)
        def _():
            loss_ref[...] = part

        @pl.when(i > 0)
        def _():
            loss_ref[...] += part

    tile = pl.BlockSpec((tm, D), lambda i: (i, 0))
    return pl.pallas_call(
        body, name=name, grid=(S // tm,), in_specs=[tile, tile],
        out_specs=[pl.BlockSpec((8, LANES), lambda i: (0, 0)), tile],
        out_shape=[_sds((8, LANES), F32), _sds((S, D), F32)], compiler_params=_params("arbitrary"),
    )(h, target)


def _adamw_math(w, g, m, v):
    c1 = 1.0 - ADAM_B1 ** ADAM_STEP
    c2 = 1.0 - ADAM_B2 ** ADAM_STEP
    nm = ADAM_B1 * m + (1.0 - ADAM_B1) * g
    nv = ADAM_B2 * v + (1.0 - ADAM_B2) * (g * g)
    return -ADAM_LR * ((nm / c1) / (jnp.sqrt(nv / c2) + ADAM_EPS) + ADAM_WD * w), nm, nv


def adamw(name, w, g, m, v):
    R, C = w.shape
    tr = R
    for cand in (512, 256, 128, 64, 32, 16, 8):
        if R % cand == 0:
            tr = cand
            break

    def body(w_ref, g_ref, m_ref, v_ref, d_ref, nm_ref, nv_ref):
        d_ref[...], nm_ref[...], nv_ref[...] = _adamw_math(w_ref[...], g_ref[...], m_ref[...], v_ref[...])

    tile = pl.BlockSpec((tr, C), lambda i: (i, 0))
    return pl.pallas_call(
        body, name=name, grid=(R // tr,), in_specs=[tile] * 4, out_specs=[tile] * 3,
        out_shape=[_sds((R, C), F32)] * 3, compiler_params=_params("parallel"),
    )(w, g, m, v)


def adamw_layer(name, layer, ws, gs, ms, vs, prev, after=()):
    n = len(ws)
    steps = 8

    def body(*refs):
        ins, outs = refs[:4 * n], refs[-4 * n:]
        for i in range(n):
            w_ref, g_ref, m_ref, v_ref = (ins[k * n + i] for k in range(4))
            g = g_ref[...]
            outs[i][...] = g
            outs[n + i][...], outs[2 * n + i][...], outs[3 * n + i][...] = _adamw_math(w_ref[...], g, m_ref[...],
                                                                                        v_ref[...])

    def stacked(a):
        return pl.BlockSpec((None, a.shape[1] // steps, a.shape[2]), lambda t: (layer, t, 0))

    def flat(a):
        return pl.BlockSpec((a.shape[0] // steps, a.shape[1]), lambda t: (t, 0))

    in_specs = [stacked(a) for a in ws] + [flat(a) for a in gs] + [stacked(a) for a in ms] + [stacked(a) for a in vs]
    operands = [*ws, *gs, *ms, *vs]
    aliases = {}
    if prev is not None:
        flat_prev = [a for group in prev for a in group]
        in_specs += [pl.BlockSpec(memory_space=pl.ANY)] * len(flat_prev)
        aliases = {4 * n + i: i for i in range(4 * n)}
        operands += flat_prev
    in_specs += [pl.BlockSpec(memory_space=pl.ANY)] * len(after)
    operands += list(after)
    out = pl.pallas_call(
        body, name=name, grid=(steps,), in_specs=in_specs, out_specs=[stacked(a) for a in ws] * 4,
        out_shape=[_sds(a.shape, F32) for a in ws] * 4, input_output_aliases=aliases,
        compiler_params=_params("parallel"),
    )(*operands)
    return [list(out[k * n:(k + 1) * n]) for k in range(4)]


def _relu2(u):
    r = jnp.maximum(u.astype(F32), 0.0)
    return r * r


def layer_fwd(tag, h0, p_bf, wt, after=(), mid=None):
    hn1 = rmsnorm(f"{tag}_norm1", h0, wt["norm1_g"], after)
    proj = mm_nn(f"{tag}_proj", hn1, wt["w_in"], tn=1408)
    ya = conv_fwd(f"{tag}_conv", proj, wt["conv_w"])
    yb, yb_tot = attn_fwd(f"{tag}_attn", proj, wt["gq"], wt["gk"])
    yc = sgu_fwd(f"{tag}_sgu", proj, wt["gv"], wt["sgu_w"], wt["sgu_bias"])
    y = jnp.concatenate([ya, yb, yc], axis=-1)
    h1 = mm_nn(f"{tag}_out", y, wt["w_out"], extras=(h0,), epi=lambda acc, h: (h + acc,), out_dtypes=(F32,))
    hn2 = rmsnorm(f"{tag}_norm2", h1, wt["norm2_g"], () if mid is None else mid(yb))
    uu = mm_nn(f"{tag}_ff1", hn2, wt["w_ff1"])
    h2 = mm_nn(f"{tag}_ff2", uu, wt["w_ff2"], pro=_relu2, extras=(h1,), epi=lambda acc, h: (h + acc,),
               out_dtypes=(F32,))
    hn3 = rmsnorm(f"{tag}_norm3", h2, wt["norm3_g"])
    gp = mm_nn(f"{tag}_gate", hn3, wt["w_ple_gate"])
    h3, pp = mm_nn(f"{tag}_ple", p_bf, wt["w_ple_proj"], extras=(gp, h2),
                   epi=lambda acc, g, h: (h + jax.nn.sigmoid(g.astype(F32)) * acc, acc), out_dtypes=(F32, BF))
    saved = dict(h0=h0, h1=h1, h2=h2, hn1=hn1, hn2=hn2, hn3=hn3, proj=proj, yb_tot=yb_tot, y=y, uu=uu, gp=gp, pp=pp,
                 p_bf=p_bf)
    return h3, saved


def layer_bwd(tag, dh3, sv, wt, after=()):
    dgp, dpp = ple_bwd_elem(f"{tag}_dple", dh3, sv["gp"], sv["pp"], after)
    g = {}
    g["w_ple_proj"] = mm_tn(f"{tag}_dwp", sv["p_bf"], dpp, col_blocks=N_CHIPS)
    g["w_ple_gate"] = mm_tn(f"{tag}_dwg", sv["hn3"], dgp)
    dh2, g["norm3_g"] = mm_nt_rmsbwd(f"{tag}_dnorm3", dgp, wt["w_ple_gate"], sv["h2"], wt["norm3_g"], dh3)

    duu = mm_nt(f"{tag}_dff2", dh2, wt["w_ff2"], extras=(sv["uu"],),
                epi=lambda acc, u: acc * (2.0 * jnp.maximum(u.astype(F32), 0.0)))
    g["w_ff2"] = mm_tn(f"{tag}_dw2", sv["uu"], dh2, pro_x=_relu2)
    g["w_ff1"] = mm_tn(f"{tag}_dw1", sv["hn2"], duu, col_blocks=N_CHIPS)
    dh1, g["norm2_g"] = mm_nt_rmsbwd(f"{tag}_dnorm2", duu, wt["w_ff1"], sv["h1"], wt["norm2_g"], dh2)

    dy = mm_nt(f"{tag}_dout", dh1, wt["w_out"])
    g["w_out"] = mm_tn(f"{tag}_dwo", sv["y"], dh1)
    dab, dac, dah, g["conv_w"] = conv_bwd(f"{tag}_dconv", dy, sv["proj"], wt["conv_w"])
    dq, dk, dv, g["gq"], g["gk"] = attn_bwd(f"{tag}_dattn", dy, sv["proj"], sv["yb_tot"], wt["gq"], wt["gk"])
    dcu, dcv, g["sgu_w"], g["sgu_bias"], g["gv"] = sgu_bwd(f"{tag}_dsgu", dy, sv["proj"], wt["gv"], wt["sgu_w"],
                                                           wt["sgu_bias"])
    dproj = jnp.concatenate([dab, dac, dah, dq, dk, dv, dcu, dcv], axis=-1)
    g["w_in"] = mm_tn(f"{tag}_dwi", sv["hn1"], dproj, tn=1408)
    dh0, g["norm1_g"] = mm_nt_rmsbwd(f"{tag}_dnorm1", dproj, wt["w_in"], sv["h0"], wt["norm1_g"], dh1)
    return dh0, g


def prep_small(norm1_g, q_norm_g, k_norm_g, sgu_norm_g, sgu_w, sgu_b, norm2_g, norm3_g, conv_w_full):
    return dict(
        norm1_g=norm1_g, norm2_g=norm2_g, norm3_g=norm3_g, conv_w=conv_w_full,
        gq=(jnp.tile(q_norm_g, 2) * SCALE).reshape(1, LANES), gk=jnp.tile(k_norm_g, 2).reshape(1, LANES),
        gv=sgu_norm_g.reshape(1, SGU_W), sgu_w=sgu_w, sgu_bias=jnp.repeat(sgu_b.T, HEAD_DIM, axis=1))


def small_grads(g):
    return dict(
        norm1_g=g["norm1_g"][0], norm2_g=g["norm2_g"][0], norm3_g=g["norm3_g"][0], conv_w=g["conv_w"],
        q_norm_g=g["gq"][0, :HEAD_DIM], k_norm_g=g["gk"][0, :HEAD_DIM], sgu_norm_g=g["gv"][0], sgu_w=g["sgu_w"],
        sgu_b=g["sgu_bias"][:, :SGU_HEADS].T)


HBM_SPEC = pl.BlockSpec(memory_space=pltpu.HBM)
BIG = ("w_in", "w_out", "w_ff1", "w_ff2", "w_ple_gate", "w_ple_proj")


def _mesh_pos():
    return lax.axis_index("x"), lax.axis_index("y"), lax.axis_index("c")


def _other_chips(x, y):
    return [(1 - x, y), (x, 1 - y), (1 - x, 1 - y)]


def _half(rows, core):
    h = rows // 2
    return pl.ds(pl.multiple_of(core * h, 16), h)


def _remote(src, dst, send_sems, recv_sems, k, to):
    return pltpu.make_async_remote_copy(src_ref=src, dst_ref=dst, send_sem=send_sems.at[k], recv_sem=recv_sems.at[k],
                                        device_id=to, device_id_type=MESH)


SEM_SPEC = pl.BlockSpec(memory_space=pltpu.SEMAPHORE)
ANY_SPEC = pl.BlockSpec(memory_space=pl.ANY)
SIDE_EFFECT = pltpu.SideEffectType.DATAFLOW_SIDE_EFFECTING


def _in_hbm(arrays):
    return [pltpu.with_memory_space_constraint(a, pltpu.HBM) for a in arrays]


def copies_start(name, srcs, lands, plan, after=()):
    ns, nl, na = len(srcs), len(lands), len(after)

    def body(*refs):
        src_refs, land_refs = refs[:ns], refs[ns:ns + nl]
        send_sem, recv_sem = refs[ns + nl + na], refs[ns + nl + na + 1]
        token = refs[-1]
        for src, dst, dev in plan(src_refs, land_refs, *_mesh_pos()):
            pltpu.make_async_remote_copy(src_ref=src, dst_ref=dst, send_sem=send_sem, recv_sem=recv_sem,
                                         device_id=dev, device_id_type=MESH).start()
        token[...] = jnp.zeros_like(token)

    out = pl.pallas_call(
        body, name=name,
        in_specs=[HBM_SPEC] * (ns + nl) + [ANY_SPEC] * na,
        out_specs=(SEM_SPEC, SEM_SPEC, *[HBM_SPEC] * (ns + nl), pl.BlockSpec(memory_space=pltpu.VMEM)),
        out_shape=(pltpu.SemaphoreType.DMA(()), pltpu.SemaphoreType.DMA(()),
                   *[pltpu.HBM(a.shape, a.dtype) for a in (*srcs, *lands)], _sds((8, LANES), F32)),
        input_output_aliases={i: 2 + i for i in range(ns + nl)},
        compiler_params=pltpu.CompilerParams(has_side_effects=SIDE_EFFECT),
    )(*_in_hbm(srcs), *_in_hbm(lands), *after)
    return out[0], out[1], list(out[2:2 + ns]), list(out[2 + ns:2 + ns + nl]), out[-1]


def copies_wait(name, started, plan, after=()):
    send_sem, recv_sem, srcs, lands, _ = started
    ns, nl, na = len(srcs), len(lands), len(after)

    def body(*refs):
        src_refs, land_refs = refs[:ns], refs[ns:ns + nl]
        send_sem, recv_sem = refs[ns + nl], refs[ns + nl + 1]
        for src, dst, dev in plan(src_refs, land_refs, *_mesh_pos()):
            cp = pltpu.make_async_remote_copy(src_ref=src, dst_ref=dst, send_sem=send_sem, recv_sem=recv_sem,
                                              device_id=dev, device_id_type=MESH)
            cp.wait_send()
            cp.wait_recv()

    out = pl.pallas_call(
        body, name=name,
        in_specs=[HBM_SPEC] * (ns + nl) + [SEM_SPEC, SEM_SPEC] + [ANY_SPEC] * na,
        out_specs=[HBM_SPEC] * (ns + nl),
        out_shape=[pltpu.HBM(a.shape, a.dtype) for a in (*srcs, *lands)],
        input_output_aliases={i: i for i in range(ns + nl)},
        compiler_params=pltpu.CompilerParams(has_side_effects=SIDE_EFFECT),
    )(*srcs, *lands, send_sem, recv_sem, *after)
    return list(out[:ns]), list(out[ns:])


def _gather_plan(srcs, lands, x, y, c):
    me = 2 * x + y
    return [(src.at[_half(src.shape[0], c)], land.at[me, _half(src.shape[0], c)], (*chip, c))
            for src, land in zip(srcs, lands) for chip in _other_chips(x, y)]


def _gather_arrivals(srcs, lands, x, y, c):
    return [(src.at[_half(src.shape[0], c)], land.at[2 * chip[0] + chip[1], _half(src.shape[0], c)], (*chip, c))
            for src, land in zip(srcs, lands) for chip in _other_chips(x, y)]


def _forward_plan(srcs, lands, x, y, c):
    me, sibling = 2 * x + y, (x, y, 1 - c)
    out = []
    for src, land in zip(srcs, lands):
        out.append((src, land.at[me], sibling))
        for chip in _other_chips(x, y):
            region = land.at[2 * chip[0] + chip[1], _half(src.shape[0], c)]
            out.append((region, region, sibling))
    return out


def _forward_arrivals(srcs, lands, x, y, c):
    me, sibling = 2 * x + y, (x, y, 1 - c)
    out = []
    for src, land in zip(srcs, lands):
        out.append((src, land.at[me], sibling))
        for chip in _other_chips(x, y):
            slot = land.at[2 * chip[0] + chip[1]]
            out.append((slot.at[_half(src.shape[0], c)], slot.at[_half(src.shape[0], 1 - c)], sibling))
    return out


def _join_plan(srcs, lands, x, y, c):
    return [(land.at[_half(land.shape[0], c)], land.at[_half(land.shape[0], c)], (x, y, 1 - c)) for land in lands]


def _join_arrivals(srcs, lands, x, y, c):
    return [(land.at[_half(land.shape[0], c)], land.at[_half(land.shape[0], 1 - c)], (x, y, 1 - c)) for land in lands]


def exchange_other_half(name, grads):
    n = len(grads)

    def body(*refs):
        ins, outs = refs[:n], refs[n:2 * n]
        send_sems, recv_sems = refs[2 * n:]
        x, y, c = _mesh_pos()
        copies = []
        for i in range(n):
            rows = grads[i].shape[1]
            cp = _remote(ins[i].at[:, _half(rows, 1 - c)], outs[i], send_sems, recv_sems, i, (x, y, 1 - c))
            cp.start()
            copies.append(cp)
        for cp in copies:
            cp.wait()

    return pl.pallas_call(
        body, name=name, in_specs=[HBM_SPEC] * n, out_specs=[HBM_SPEC] * n,
        out_shape=[_sds((N_CHIPS, g.shape[1] // 2, g.shape[2]), g.dtype) for g in grads],
        scratch_shapes=[pltpu.SemaphoreType.DMA((n,)), pltpu.SemaphoreType.DMA((n,))],
    )(*grads)


def add_own_half(name, core, grads, got):
    n = len(grads)

    def body(core_ref, *refs):
        for i in range(n):
            refs[2 * n + i][...] = (refs[i][...].astype(F32) + refs[n + i][...].astype(F32)).astype(BF)

    def spec(g, own):
        blk = (None, g.shape[1] // 2, g.shape[2])
        return pl.BlockSpec(blk, (lambda j, core_ref: (j, core_ref[0], 0)) if own else (lambda j, core_ref: (j, 0, 0)))

    return pl.pallas_call(
        body, name=name,
        grid_spec=pltpu.PrefetchScalarGridSpec(
            num_scalar_prefetch=1, grid=(N_CHIPS,),
            in_specs=[spec(g, True) for g in grads] + [spec(g, False) for g in grads],
            out_specs=[spec(g, False) for g in grads]),
        out_shape=[_sds(r.shape, BF) for r in got], compiler_params=_params("parallel"),
    )(core, *grads, *got)


def _chips_plan(srcs, lands, x, y, c):
    return [(src.at[2 * chip[0] + chip[1]], land.at[k], (*chip, c))
            for src, land in zip(srcs, lands) for k, chip in enumerate(_other_chips(x, y))]


def sum_chips(name, place, parts, got):
    n = len(got)

    def body(place_ref, *refs):
        for i in range(n):
            acc = refs[i][...].astype(F32)
            for k in range(N_CHIPS - 1):
                acc = acc + refs[n + i][k].astype(F32)
            refs[2 * n + i][...] = acc

    steps = 2
    return pl.pallas_call(
        body, name=name,
        grid_spec=pltpu.PrefetchScalarGridSpec(
            num_scalar_prefetch=1, grid=(steps,),
            in_specs=[pl.BlockSpec((None, g.shape[1] // steps, g.shape[2]), lambda t, place_ref: (place_ref[0], t, 0))
                      for g in parts] +
                     [pl.BlockSpec((N_CHIPS - 1, g.shape[1] // steps, g.shape[2]), lambda t, place_ref: (0, t, 0))
                      for g in got],
            out_specs=[pl.BlockSpec((g.shape[1] // steps, g.shape[2]),
                                    lambda t, place_ref: (place_ref[1] * steps + t, 0)) for g in got]),
        out_shape=[_sds((2 * g.shape[1], g.shape[2]), F32) for g in got], compiler_params=_params("parallel"),
    )(place, *parts, *got)


def reduce_scatter_begin(tag, core, grads):
    got = exchange_other_half(f"{tag}_rs_pair", grads)
    parts = add_own_half(f"{tag}_rs_add", core, grads, got)
    lands = [lax.empty((N_CHIPS - 1,) + p.shape[1:], p.dtype) for p in parts]
    return copies_start(f"{tag}_rs_start", parts, lands, _chips_plan)


def reduce_scatter_sum(tag, place, started, after):
    parts, got = copies_wait(f"{tag}_rs_wait", started, _chips_plan, after)
    return copies_start(f"{tag}_rs_join_start", [], sum_chips(f"{tag}_rs_sum", place, parts, got), _join_plan)


def reduce_scatter_end(tag, joining, after):
    return copies_wait(f"{tag}_rs_join_wait", joining, _join_arrivals, after)[1]


def small_allreduce(name, x):
    R = x.shape[0]
    H = R // 2

    def body(x_ref, o_ref, pair_ref, chip_ref, send_sems, recv_sems):
        xx, yy, c = _mesh_pos()
        me = 2 * xx + yy
        chips = _other_chips(xx, yy)
        sibling = (xx, yy, 1 - c)
        mine = pl.ds(pl.multiple_of(c * H, 8), H)
        theirs = pl.ds(pl.multiple_of((1 - c) * H, 8), H)
        a = _remote(x_ref.at[theirs], pair_ref.at[theirs], send_sems, recv_sems, 0, sibling)
        a.start()
        a.wait_send()
        _remote(x_ref.at[mine], pair_ref.at[mine], send_sems, recv_sems, 0, sibling).wait_recv()
        chip_ref[me] = x_ref[mine, :] + pair_ref[mine, :]
        sends = []
        for k, chip in enumerate(chips):
            cp = _remote(chip_ref.at[me], chip_ref.at[me], send_sems, recv_sems, 1 + k, (*chip, c))
            cp.start()
            sends.append(cp)
        for k, chip in enumerate(chips):
            slot = chip_ref.at[2 * chip[0] + chip[1]]
            _remote(slot, slot, send_sems, recv_sems, 1 + k, (*chip, c)).wait_recv()
        o_ref[mine, :] = (chip_ref[0] + chip_ref[1]) + (chip_ref[2] + chip_ref[3])
        b = _remote(o_ref.at[mine], o_ref.at[mine], send_sems, recv_sems, 4, sibling)
        b.start()
        b.wait_send()
        _remote(o_ref.at[theirs], o_ref.at[theirs], send_sems, recv_sems, 4, sibling).wait_recv()
        for cp in sends:
            cp.wait_send()

    return pl.pallas_call(
        body, name=name,
        in_specs=[pl.BlockSpec(memory_space=pltpu.VMEM)], out_specs=pl.BlockSpec(memory_space=pltpu.VMEM),
        out_shape=_sds((R, LANES), F32),
        scratch_shapes=[pltpu.VMEM((R, LANES), F32), pltpu.VMEM((N_CHIPS, H, LANES), F32),
                        pltpu.SemaphoreType.DMA((5,)), pltpu.SemaphoreType.DMA((5,))],
        compiler_params=pltpu.CompilerParams(vmem_limit_bytes=VMEM_LIMIT),
    )(x)


WEIGHTS = ("norm1_g", "w_in", "conv_w", "q_norm_g", "k_norm_g", "sgu_norm_g", "sgu_w", "sgu_b", "w_out", "norm2_g",
           "w_ff1", "w_ff2", "norm3_g", "w_ple_gate", "w_ple_proj")
SMALL = ("norm1_g", "norm2_g", "norm3_g", "q_norm_g", "k_norm_g", "sgu_norm_g", "sgu_w", "sgu_b", "conv_w")


def _pack_rows(arrays):
    flat = []
    for a in arrays:
        v = a.reshape(-1)
        flat.append(jnp.pad(v, (0, (-v.shape[0]) % LANES)))
    v = jnp.concatenate(flat)
    v = jnp.pad(v, (0, (-v.shape[0]) % (16 * LANES)))
    return v.reshape(-1, LANES)


def _unpack_rows(packed, shapes):
    out, pos = [], 0
    flat = packed.reshape(-1)
    for shp in shapes:
        size = math.prod(shp)
        out.append(flat[pos:pos + size].reshape(shp))
        pos += size + (-size) % LANES
    return out


def kernel(x, p, norm1_g, w_in, conv_w, q_norm_g, k_norm_g, sgu_norm_g, sgu_w, sgu_b, w_out, norm2_g, w_ff1, w_ff2, norm3_g, w_ple_gate, w_ple_proj, loss_target, m_norm1_g, m_w_in, m_conv_w, m_q_norm_g, m_k_norm_g, m_sgu_norm_g, m_sgu_w, m_sgu_b, m_w_out, m_norm2_g, m_w_ff1, m_w_ff2, m_norm3_g, m_w_ple_gate, m_w_ple_proj, v_norm1_g, v_w_in, v_conv_w, v_q_norm_g, v_k_norm_g, v_sgu_norm_g, v_sgu_w, v_sgu_b, v_w_out, v_norm2_g, v_w_ff1, v_w_ff2, v_norm3_g, v_w_ple_gate, v_w_ple_proj):
    w = dict(norm1_g=norm1_g, w_in=w_in, conv_w=conv_w, q_norm_g=q_norm_g, k_norm_g=k_norm_g, sgu_norm_g=sgu_norm_g,
             sgu_w=sgu_w, sgu_b=sgu_b, w_out=w_out, norm2_g=norm2_g, w_ff1=w_ff1, w_ff2=w_ff2, norm3_g=norm3_g,
             w_ple_gate=w_ple_gate, w_ple_proj=w_ple_proj)
    m = dict(norm1_g=m_norm1_g, w_in=m_w_in, conv_w=m_conv_w, q_norm_g=m_q_norm_g, k_norm_g=m_k_norm_g,
             sgu_norm_g=m_sgu_norm_g, sgu_w=m_sgu_w, sgu_b=m_sgu_b, w_out=m_w_out, norm2_g=m_norm2_g, w_ff1=m_w_ff1,
             w_ff2=m_w_ff2, norm3_g=m_norm3_g, w_ple_gate=m_w_ple_gate, w_ple_proj=m_w_ple_proj)
    v = dict(norm1_g=v_norm1_g, w_in=v_w_in, conv_w=v_conv_w, q_norm_g=v_q_norm_g, k_norm_g=v_k_norm_g,
             sgu_norm_g=v_sgu_norm_g, sgu_w=v_sgu_w, sgu_b=v_sgu_b, w_out=v_w_out, norm2_g=v_norm2_g, w_ff1=v_w_ff1,
             w_ff2=v_w_ff2, norm3_g=v_norm3_g, w_ple_gate=v_w_ple_gate, w_ple_proj=v_w_ple_proj)
    depth = w_in.shape[0]
    d_model = x.shape[-1]
    chip = 2 * lax.axis_index("x") + lax.axis_index("y")
    core = lax.axis_index("c")
    core_arr = core.reshape(1).astype(jnp.int32)

    cw_cols = conv_w.shape[-1]
    placed = lax.dynamic_update_slice(jnp.zeros((depth, 3, CONV_W), F32), conv_w, (0, 0, chip * cw_cols))
    placed = jnp.where(core == 0, placed, 0.0)
    conv_full = _unpack_rows(small_allreduce("conv_w_gather", _pack_rows([placed])), [(depth, 3, CONV_W)])[0]

    h = x[0]
    p_bf = p[:, 0].astype(BF)
    saved, full = [], []

    def gather_start(l, after):
        shards = [w[n][l].astype(BF) for n in BIG]
        lands = [lax.empty((N_CHIPS,) + s.shape, BF) for s in shards]
        return copies_start(f"l{l}_gather_start", shards, lands, _gather_plan, after)

    def gather_forward(l, started, after):
        shards, lands = copies_wait(f"l{l}_gather_wait", started, _gather_arrivals, after)
        forwarding = copies_start(f"l{l}_forward_start", shards, lands, _forward_plan)
        return forwarding, gather_start(l + 1, (forwarding[-1],)) if l + 1 < depth else None

    forwarding, started = gather_forward(0, gather_start(0, ()), (h,))
    for l in range(depth):
        g_in, g_out, g_ff1, g_ff2, g_gate, g_proj = copies_wait(f"l{l}_forward_wait", forwarding, _forward_arrivals,
                                                                 (h,))[1]
        nxt = {}

        def mid(arr, l=l, started=started, nxt=nxt):
            nxt["forwarding"], nxt["started"] = gather_forward(l + 1, started, (arr,))
            return [t[-1] for t in nxt.values() if t is not None]

        wt = prep_small(norm1_g[l], q_norm_g[l], k_norm_g[l], sgu_norm_g[l], sgu_w[l], sgu_b[l], norm2_g[l], norm3_g[l],
                        conv_full[l])
        wt["w_in"] = jnp.transpose(g_in, (1, 0, 2)).reshape(d_model, -1)
        wt["w_out"] = g_out.reshape(-1, d_model)
        wt["w_ff1"] = g_ff1
        wt["w_ff2"] = g_ff2.reshape(-1, d_model)
        wt["w_ple_gate"] = g_gate.reshape(-1, d_model)
        wt["w_ple_proj"] = g_proj
        last = l + 1 == depth
        h, sv = layer_fwd(f"l{l}", h, p_bf[l], wt, () if last else (started[-1],), None if last else mid)
        if not last:
            forwarding, started = nxt["forwarding"], nxt["started"]
        saved.append(sv)
        full.append(wt)

    loss_tile, dh = loss_head("loss", h, loss_target[0])
    loss = lax.psum(loss_tile[0, 0], ("x", "y", "c"))

    small = [None] * depth
    chip_arr = jnp.stack([chip, core]).astype(jnp.int32)
    big_w, big_m, big_v = ([d[n] for n in BIG] for d in (w, m, v))
    started, joining, tokens = None, {}, ()
    for l in reversed(range(depth)):
        dh, g = layer_bwd(f"l{l}", dh, saved[l], full[l], tokens)
        tokens = ()
        if started is not None:
            joining[l + 1] = reduce_scatter_sum(f"l{l + 1}", chip_arr, started, (dh,))
            tokens = (joining[l + 1][-1],)
        small[l] = small_grads(g)
        shards_in = w_in.shape[-1]
        gl = [jnp.transpose(g["w_in"].reshape(d_model, N_CHIPS, shards_in), (1, 0, 2)),
              g["w_out"].reshape(N_CHIPS, -1, d_model), g["w_ff1"], g["w_ff2"].reshape(N_CHIPS, -1, d_model),
              g["w_ple_gate"].reshape(N_CHIPS, -1, d_model), g["w_ple_proj"]]
        started = reduce_scatter_begin(f"l{l}", core_arr, gl)
        tokens += (started[-1],)

    updated = None
    for l in reversed(range(1, depth)):
        reduced = reduce_scatter_end(f"l{l}", joining[l], (started[-1],))
        updated = adamw_layer(f"l{l}_adamw", l, big_w, reduced, big_m, big_v, updated)
    grads, delta, new_m, new_v = {}, {}, {}, {}
    packed = _pack_rows([small[l][n] for l in range(depth) for n in SMALL])
    shapes = [small[l][n].shape for l in range(depth) for n in SMALL]
    pieces = _unpack_rows(small_allreduce("small_grads", packed), shapes)
    for i, n in enumerate(SMALL):
        grads[n] = jnp.stack([pieces[l * len(SMALL) + i] for l in range(depth)])
    grads["conv_w"] = lax.dynamic_slice(grads["conv_w"], (0, 0, chip * cw_cols), (depth, 3, cw_cols))
    for n in SMALL:
        shp = w[n].shape
        two_d = (-1, shp[-1]) if n != "sgu_w" else (-1, LANES)
        d, nm, nv = adamw(f"adamw_{n}", w[n].reshape(two_d), grads[n].reshape(two_d), m[n].reshape(two_d),
                          v[n].reshape(two_d))
        delta[n], new_m[n], new_v[n] = d.reshape(shp), nm.reshape(shp), nv.reshape(shp)

    done = [new_v[n] for n in SMALL] + ([] if updated is None else [updated[3][0]])
    reduced = reduce_scatter_end("l0", reduce_scatter_sum("l0", chip_arr, started, (dh, *done)), ())
    updated = adamw_layer("l0_adamw", 0, big_w, reduced, big_m, big_v, updated)
    for k, d in enumerate((grads, delta, new_m, new_v)):
        d.update(zip(BIG, updated[k]))

    return (loss, dh[None], *[grads[n] for n in WEIGHTS], *[delta[n] for n in WEIGHTS], *[new_m[n] for n in WEIGHTS],
            *[new_v[n] for n in WEIGHTS])
```

```python
import functools
import math

import jax
import jax.numpy as jnp
from jax import lax
from jax.experimental import pallas as pl
from jax.experimental.pallas import tpu as pltpu

F32 = jnp.float32
BF = jnp.bfloat16
MESH = pl.DeviceIdType.MESH
HIGHEST = lax.Precision.HIGHEST

EPS = 1e-6
HEAD_DIM = 64
CONV_W = 256
ATTN_W = 512
SGU_W = 256
CHUNK = 128
N_CHIPS = 4
SCALE = HEAD_DIM ** -0.5
LANES = 128
VMEM_LIMIT = 56 * 1024 * 1024

ADAM_LR = 0.001
ADAM_B1 = 0.9
ADAM_B2 = 0.999
ADAM_EPS = 1e-08
ADAM_WD = 0.01
ADAM_STEP = 10

NT_DIMS = (((1,), (1,)), ((), ()))
TN_DIMS = (((0,), (0,)), ((), ()))


def _params(*sem):
    return pltpu.CompilerParams(dimension_semantics=sem if sem else None, vmem_limit_bytes=VMEM_LIMIT)


def _sds(shape, dtype):
    return jax.ShapeDtypeStruct(shape, dtype)


def _erf(x):
    return lax.erf(x)


def _gelu(x):
    return 0.5 * x * (1.0 + _erf(x * (2.0 ** -0.5)))


def _gelu_grad(x):
    return 0.5 * (1.0 + _erf(x * (2.0 ** -0.5))) + x * jnp.exp(-0.5 * x * x) * (1.0 / math.sqrt(2.0 * math.pi))


def _log_sigmoid(z):
    return jnp.minimum(z, 0.0) - jnp.log(1.0 + jnp.exp(-jnp.abs(z)))


def _head_mean_matrix(width):
    r = lax.broadcasted_iota(jnp.int32, (width, width), 0) // HEAD_DIM
    c = lax.broadcasted_iota(jnp.int32, (width, width), 1) // HEAD_DIM
    return (r == c).astype(BF)


def _head_mean(x, m):
    hi = x.astype(BF)
    lo = (x - hi.astype(F32)).astype(BF)
    return _dot2_stacked(hi, lo, m) * (1.0 / HEAD_DIM)


def rmsnorm(name, h, g, after=()):
    S, D = h.shape
    tm = min(S, 512)

    def body(h_ref, g_ref, *rest):
        x = h_ref[...]
        r = lax.rsqrt(jnp.mean(x * x, axis=-1, keepdims=True) + EPS)
        rest[-1][...] = ((x * r) * g_ref[...]).astype(BF)

    return pl.pallas_call(
        body, name=name, grid=(S // tm,),
        in_specs=[pl.BlockSpec((tm, D), lambda i: (i, 0)), pl.BlockSpec((1, D), lambda i: (0, 0))] +
                 [pl.BlockSpec(memory_space=pl.ANY)] * len(after),
        out_specs=pl.BlockSpec((tm, D), lambda i: (i, 0)),
        out_shape=_sds((S, D), BF), compiler_params=_params("parallel"),
    )(h, g.reshape(1, D), *after)


def mm_nn(name, x, w, *, extras=(), pro=None, epi=None, out_dtypes=None, tm=512, tn=512):
    S, K = x.shape
    if w.ndim == 3:
        J, _, tn = w.shape
        N = J * tn
        w_spec = pl.BlockSpec((None, K, tn), lambda n, m: (n, 0, 0))
    else:
        N = w.shape[1]
        tn = min(tn, N)
        w_spec = pl.BlockSpec((K, tn), lambda n, m: (0, n))
    tm = min(tm, S)
    out_dtypes = (BF,) if out_dtypes is None else out_dtypes
    n_ex, n_out = len(extras), len(out_dtypes)

    def body(x_ref, w_ref, *rest):
        xv = x_ref[...]
        if pro is not None:
            xv = pro(xv)
        acc = jnp.dot(xv.astype(BF), w_ref[...], preferred_element_type=F32)
        outs = (acc,) if epi is None else epi(acc, *[e[...] for e in rest[:n_ex]])
        for o_ref, o in zip(rest[n_ex:], outs):
            o_ref[...] = o.astype(o_ref.dtype)

    tile = pl.BlockSpec((tm, tn), lambda n, m: (m, n))
    out = pl.pallas_call(
        body, name=name, grid=(N // tn, S // tm),
        in_specs=[pl.BlockSpec((tm, K), lambda n, m: (m, 0)), w_spec] + [tile] * n_ex,
        out_specs=[tile] * n_out,
        out_shape=[_sds((S, N), d) for d in out_dtypes],
        compiler_params=_params("parallel", "parallel"),
    )(x, w, *extras)
    return out[0] if n_out == 1 else out


def mm_nt(name, dy, w, *, extras=(), epi=None, tm=512, after=()):
    S, N = dy.shape
    K = w.shape[0]
    tm = min(tm, S)
    n_ex = len(extras)

    def body(dy_ref, w_ref, *rest):
        acc = lax.dot_general(dy_ref[...].astype(BF), w_ref[...], NT_DIMS, preferred_element_type=F32)
        if epi is not None:
            acc = epi(acc, *[e[...] for e in rest[:n_ex]])
        rest[-1][...] = acc.astype(BF)

    row = pl.BlockSpec((tm, K), lambda i: (i, 0))
    return pl.pallas_call(
        body, name=name, grid=(S // tm,),
        in_specs=[pl.BlockSpec((tm, N), lambda i: (i, 0)), pl.BlockSpec((K, N), lambda i: (0, 0))] + [row] * n_ex +
                 [pl.BlockSpec(memory_space=pl.ANY)] * len(after),
        out_specs=row, out_shape=_sds((S, K), BF), compiler_params=_params("parallel"),
    )(dy, w, *extras, *after)


def mm_nt_rmsbwd(name, dy, w, h, g, dres, *, tm=512):
    S, N = dy.shape
    D = h.shape[1]
    tm = min(tm, S)
    blocked = w.ndim == 3
    nj = w.shape[2] if blocked else N

    def body(dy_ref, w_ref, h_ref, g_ref, dres_ref, dh_ref, dg_ref):
        i = pl.program_id(0)
        if blocked:
            dyn = None
            for j in range(w.shape[0]):
                part = lax.dot_general(dy_ref[:, j * nj:(j + 1) * nj].astype(BF), w_ref[j], NT_DIMS,
                                       preferred_element_type=F32)
                dyn = part if dyn is None else dyn + part
        else:
            dyn = lax.dot_general(dy_ref[...].astype(BF), w_ref[...], NT_DIMS, preferred_element_type=F32)
        x = h_ref[...]
        r = lax.rsqrt(jnp.mean(x * x, axis=-1, keepdims=True) + EPS)
        t = dyn * g_ref[...]
        dh_ref[...] = dres_ref[...] + r * t - x * (r * r * r) * jnp.mean(t * x, axis=-1, keepdims=True)
        part = jnp.sum(dyn * (x * r), axis=0, keepdims=True)

        @pl.when(i == 0)
        def _():
            dg_ref[...] = part

        @pl.when(i > 0)
        def _():
            dg_ref[...] += part

    w_spec = pl.BlockSpec(w.shape, (lambda i: (0, 0, 0)) if blocked else (lambda i: (0, 0)))
    row = pl.BlockSpec((tm, D), lambda i: (i, 0))
    vec = pl.BlockSpec((1, D), lambda i: (0, 0))
    return pl.pallas_call(
        body, name=name, grid=(S // tm,),
        in_specs=[pl.BlockSpec((tm, N), lambda i: (i, 0)), w_spec, row, vec, row],
        out_specs=[row, vec], out_shape=[_sds((S, D), F32), _sds((1, D), F32)],
        compiler_params=_params("arbitrary"),
    )(dy, w, h, g.reshape(1, D), dres)


def mm_tn(name, x, dy, *, pro_x=None, col_blocks=None, tk=1024, tn=1024):
    S, K = x.shape
    N = dy.shape[1]
    tk = min(tk, K)
    if col_blocks is not None:
        tn = N // col_blocks
        out_shape = _sds((col_blocks, K, tn), BF)
        out_spec = pl.BlockSpec((None, tk, tn), lambda k, n: (n, k, 0))
    else:
        tn = min(tn, N)
        out_shape = _sds((K, N), BF)
        out_spec = pl.BlockSpec((tk, tn), lambda k, n: (k, n))

    def body(x_ref, dy_ref, o_ref):
        xv = x_ref[...]
        if pro_x is not None:
            xv = pro_x(xv)
        o_ref[...] = lax.dot_general(xv.astype(BF), dy_ref[...].astype(BF), TN_DIMS,
                                     preferred_element_type=F32).astype(BF)

    return pl.pallas_call(
        body, name=name, grid=(K // tk, N // tn),
        in_specs=[pl.BlockSpec((S, tk), lambda k, n: (0, k)), pl.BlockSpec((S, tn), lambda k, n: (0, n))],
        out_specs=out_spec, out_shape=out_shape, compiler_params=_params("parallel", "parallel"),
    )(x, dy)


def _conv_parts(ac_ref, ah_ref, cw):
    a_c = ac_ref[...].astype(F32)
    a_h = ah_ref[...].astype(F32)
    x = a_c * a_h
    row = lax.broadcasted_iota(jnp.int32, x.shape, 0)
    x1 = jnp.where(row >= 1, pltpu.roll(x, 1, 0), 0.0)
    x2 = jnp.where(row >= 2, pltpu.roll(x, 2, 0), 0.0)
    cv = cw[0:1] * x2 + cw[1:2] * x1 + cw[2:3] * x
    return a_c, a_h, x, x1, x2, cv, row


def conv_fwd(name, proj, cw):
    S = proj.shape[0]

    def body(ab_ref, ac_ref, ah_ref, cw_ref, o_ref):
        cv = _conv_parts(ac_ref, ah_ref, cw_ref[...])[5]
        o_ref[...] = (ab_ref[...].astype(F32) * cv).astype(BF)

    col = lambda j: pl.BlockSpec((S, CONV_W), lambda i, j=j: (0, j))
    return pl.pallas_call(
        body, name=name, grid=(1,),
        in_specs=[col(0), col(1), col(2), pl.BlockSpec((3, CONV_W), lambda i: (0, 0))],
        out_specs=pl.BlockSpec((S, CONV_W), lambda i: (0, 0)),
        out_shape=_sds((S, CONV_W), BF), compiler_params=_params("arbitrary"),
    )(proj, proj, proj, cw)


def conv_bwd(name, dy, proj, cw):
    S = proj.shape[0]

    def body(dy_ref, ab_ref, ac_ref, ah_ref, cw_ref, dab_ref, dac_ref, dah_ref, dcw_ref):
        w = cw_ref[...]
        a_c, a_h, x, x1, x2, cv, row = _conv_parts(ac_ref, ah_ref, w)
        d = dy_ref[...].astype(F32)
        dab_ref[...] = (d * cv).astype(BF)
        dcv = d * ab_ref[...].astype(F32)
        d1 = jnp.where(row < S - 1, pltpu.roll(dcv, S - 1, 0), 0.0)
        d2 = jnp.where(row < S - 2, pltpu.roll(dcv, S - 2, 0), 0.0)
        dx = w[2:3] * dcv + w[1:2] * d1 + w[0:1] * d2
        dac_ref[...] = (dx * a_h).astype(BF)
        dah_ref[...] = (dx * a_c).astype(BF)
        dcw_ref[0:1, :] = jnp.sum(dcv * x2, axis=0, keepdims=True)
        dcw_ref[1:2, :] = jnp.sum(dcv * x1, axis=0, keepdims=True)
        dcw_ref[2:3, :] = jnp.sum(dcv * x, axis=0, keepdims=True)

    col = lambda j: pl.BlockSpec((S, CONV_W), lambda i, j=j: (0, j))
    one = pl.BlockSpec((S, CONV_W), lambda i: (0, 0))
    small = pl.BlockSpec((3, CONV_W), lambda i: (0, 0))
    return pl.pallas_call(
        body, name=name, grid=(1,),
        in_specs=[col(0), col(0), col(1), col(2), small],
        out_specs=[one, one, one, small],
        out_shape=[_sds((S, CONV_W), BF)] * 3 + [_sds((3, CONV_W), F32)],
        compiler_params=_params("arbitrary"),
    )(dy, proj, proj, proj, cw)


SGU_HEADS = SGU_W // HEAD_DIM
CU_BLOCK = 2304 // SGU_W
CV_BLOCK = 2560 // SGU_W


def _sgu_common(cu_ref, cv_ref, gv_ref, tm):
    c_u = cu_ref[...].astype(F32)
    c_v = cv_ref[...].astype(F32)
    hm = _head_mean_matrix(SGU_W)
    u = _gelu(c_u)
    vg = _gelu(c_v)
    r = lax.rsqrt(_head_mean(vg * vg, hm) + EPS)
    vv = (vg * r) * gv_ref[...]
    head = lax.broadcasted_iota(jnp.int32, (CHUNK, SGU_W), 1) // HEAD_DIM
    tri = (lax.broadcasted_iota(jnp.int32, (CHUNK, CHUNK), 0) >=
           lax.broadcasted_iota(jnp.int32, (CHUNK, CHUNK), 1))
    return c_u, c_v, hm, u, vg, r, vv, head, tri


def _sgu_mix(w_ref, tri, head, vvc, bias):
    sv = bias
    for g in range(SGU_HEADS):
        wg = jnp.where(tri, w_ref[g], 0.0).astype(BF)
        sv = sv + jnp.where(head == g, jnp.dot(wg, vvc, preferred_element_type=F32), 0.0)
    return sv


def sgu_fwd(name, proj, gv, w, bias):
    S = proj.shape[0]
    tm = min(S, 512)

    def body(cu_ref, cv_ref, gv_ref, w_ref, b_ref, o_ref):
        _, _, _, u, _, _, vv, head, tri = _sgu_common(cu_ref, cv_ref, gv_ref, tm)
        vvb = vv.astype(BF)
        for ch in range(tm // CHUNK):
            rows = slice(ch * CHUNK, (ch + 1) * CHUNK)
            sv = _sgu_mix(w_ref, tri, head, vvb[rows], b_ref[...])
            o_ref[rows, :] = (u[rows] * sv).astype(BF)

    const = lambda shape: pl.BlockSpec(shape, lambda i: (0,) * len(shape))
    return pl.pallas_call(
        body, name=name, grid=(S // tm,),
        in_specs=[pl.BlockSpec((tm, SGU_W), lambda i: (i, CU_BLOCK)), pl.BlockSpec((tm, SGU_W), lambda i: (i, CV_BLOCK)),
                  const((1, SGU_W)), const((SGU_HEADS, CHUNK, CHUNK)), const((CHUNK, SGU_W))],
        out_specs=pl.BlockSpec((tm, SGU_W), lambda i: (i, 0)),
        out_shape=_sds((S, SGU_W), BF), compiler_params=_params("parallel"),
    )(proj, proj, gv, w, bias)


def sgu_bwd(name, dy, proj, gv, w, bias):
    S = proj.shape[0]
    tm = min(S, 512)

    def body(dy_ref, cu_ref, cv_ref, gv_ref, w_ref, b_ref, dcu_ref, dcv_ref, dw_ref, db_ref, dgv_ref, dvv_s):
        i = pl.program_id(0)
        c_u, c_v, hm, u, vg, r, vv, head, tri = _sgu_common(cu_ref, cv_ref, gv_ref, tm)
        vvb = vv.astype(BF)
        d = dy_ref[...].astype(F32)
        ind = (lax.broadcasted_iota(jnp.int32, (SGU_W, LANES), 0) // HEAD_DIM ==
               lax.broadcasted_iota(jnp.int32, (SGU_W, LANES), 1)).astype(BF)
        dw_acc = [jnp.zeros((CHUNK, CHUNK), F32) for _ in range(SGU_HEADS)]
        db_acc = jnp.zeros((CHUNK, LANES), F32)
        for ch in range(tm // CHUNK):
            rows = slice(ch * CHUNK, (ch + 1) * CHUNK)
            sv = _sgu_mix(w_ref, tri, head, vvb[rows], b_ref[...])
            dcu_ref[rows, :] = (d[rows] * sv * _gelu_grad(c_u[rows])).astype(BF)
            dsv = d[rows] * u[rows]
            dsv_hi = dsv.astype(BF)
            db_acc = db_acc + _dot2_stacked(dsv_hi, (dsv - dsv_hi.astype(F32)).astype(BF), ind)
            dvv = jnp.zeros((CHUNK, SGU_W), F32)
            for g in range(SGU_HEADS):
                dsv_g = jnp.where(head == g, dsv, 0.0).astype(BF)
                wg = jnp.where(tri, w_ref[g], 0.0).astype(BF)
                dvv = dvv + lax.dot_general(wg, dsv_g, TN_DIMS, preferred_element_type=F32)
                dw_acc[g] = dw_acc[g] + lax.dot_general(dsv_g, vvb[rows], NT_DIMS, preferred_element_type=F32)
            dvv_s[rows, :] = dvv
        dvv = dvv_s[...]
        gvv = gv_ref[...]
        t = dvv * gvv
        dvg = r * t - vg * (r * r * r) * _head_mean(t * vg, hm)
        dcv_ref[...] = (dvg * _gelu_grad(c_v)).astype(BF)
        dgv = jnp.sum(dvv * (vg * r), axis=0, keepdims=True)

        @pl.when(i == 0)
        def _():
            for g in range(SGU_HEADS):
                dw_ref[g] = jnp.where(tri, dw_acc[g], 0.0)
            db_ref[...] = db_acc
            dgv_ref[...] = dgv

        @pl.when(i > 0)
        def _():
            for g in range(SGU_HEADS):
                dw_ref[g] += jnp.where(tri, dw_acc[g], 0.0)
            db_ref[...] += db_acc
            dgv_ref[...] += dgv

    const = lambda shape: pl.BlockSpec(shape, lambda i: (0,) * len(shape))
    tile = pl.BlockSpec((tm, SGU_W), lambda i: (i, 0))
    return pl.pallas_call(
        body, name=name, grid=(S // tm,),
        in_specs=[pl.BlockSpec((tm, SGU_W), lambda i: (i, 3)),
                  pl.BlockSpec((tm, SGU_W), lambda i: (i, CU_BLOCK)), pl.BlockSpec((tm, SGU_W), lambda i: (i, CV_BLOCK)),
                  const((1, SGU_W)), const((SGU_HEADS, CHUNK, CHUNK)), const((CHUNK, SGU_W))],
        out_specs=[tile, tile, const((SGU_HEADS, CHUNK, CHUNK)), const((CHUNK, LANES)), const((1, SGU_W))],
        out_shape=[_sds((S, SGU_W), BF), _sds((S, SGU_W), BF), _sds((SGU_HEADS, CHUNK, CHUNK), F32),
                   _sds((CHUNK, LANES), F32), _sds((1, SGU_W), F32)],
        scratch_shapes=[pltpu.VMEM((tm, SGU_W), F32)],
        compiler_params=_params("arbitrary"),
    )(dy, proj, proj, gv, w, bias)


HEAD_PAIRS = ATTN_W // LANES
Q_BLOCK0 = 768 // LANES
K_BLOCK0 = 1280 // LANES
V_BLOCK0 = 1792 // LANES


def _attn_tile(S):
    return min(S, 256)


def _qk_norm(x, g, hm):
    r = lax.rsqrt(_head_mean(x * x, hm) + EPS)
    return r, (x * r) * g


MASKED = -1e30


def _logit_parts(z):
    lb = _log_sigmoid(z)
    lr = lb - z
    hi = lr.astype(BF)
    return lb, hi, (lr - hi.astype(F32)).astype(BF)


def _stack_heads(x, lane):
    return jnp.concatenate([jnp.where(lane < HEAD_DIM, x, 0.0), jnp.where(lane >= HEAD_DIM, x, 0.0)],
                           axis=0).astype(BF)


def _dot2_stacked(hi, lo, u):
    rows = hi.shape[0]
    both = jnp.dot(jnp.concatenate([hi, lo], axis=0), u, preferred_element_type=F32)
    return both[:rows] + both[rows:]


def attn_fwd(name, proj, gq, gk):
    S = proj.shape[0]
    T = _attn_tile(S)
    nq = S // T

    def body(q_ref, k_ref, v_ref, gq_ref, gk_ref, o_ref, tot_ref, kn_s, lb_s, hi_s, lo_s, z_s, a_s, o_s):
        qi = pl.program_id(1)
        hm = _head_mean_matrix(LANES)

        @pl.when(qi == 0)
        def _():
            kn_s[...] = _qk_norm(k_ref[...].astype(F32), gk_ref[...], hm)[1].astype(BF)

        qn = _qk_norm(q_ref[...].astype(F32), gq_ref[...], hm)[1]
        lane = lax.broadcasted_iota(jnp.int32, (T, LANES), 1)
        qst = _stack_heads(qn, lane)
        rowi = lax.broadcasted_iota(jnp.int32, (T, T), 0)
        coli = lax.broadcasted_iota(jnp.int32, (T, T), 1)
        u_excl = (rowi > coli).astype(BF)
        diagonal = jnp.where(coli < rowi, 0.0, MASKED)

        def logits(j):
            return lax.dot_general(qst, kn_s[pl.ds(pl.multiple_of(j * T, T), T), :], NT_DIMS,
                                   preferred_element_type=F32)

        def values(j):
            return v_ref[pl.ds(pl.multiple_of(j * T, T), T), :].astype(BF)

        def keep(slot, z):
            lb_s[slot], hi_s[...], lo_s[...] = _logit_parts(z)

        def step(it, carry):
            run = carry
            j = qi - it
            hi, lo = hi_s[...], lo_s[...]
            both = jnp.dot(jnp.concatenate([hi, lo], axis=0), u_excl, preferred_element_type=F32)
            o_s[...] += jnp.dot(a_s[...], values(jnp.minimum(j + 1, qi)), preferred_element_type=F32)
            z_after = logits(jnp.maximum(j - 2, 0))
            first = hi[:, 0:1].astype(F32) + lo[:, 0:1].astype(F32)
            keep((it + 1) % 2, z_s[...])
            later = both[:2 * T] + both[2 * T:]
            a_s[...] = jnp.exp(lb_s[it % 2] + later + run).astype(BF)
            z_s[...] = z_after
            return run + later[:, 0:1] + first

        keep(0, logits(qi) + jnp.concatenate([diagonal, diagonal], axis=0))
        z_s[...] = logits(jnp.maximum(qi - 1, 0))
        a_s[...] = jnp.zeros_like(a_s)
        o_s[...] = jnp.zeros_like(o_s)
        run = lax.fori_loop(0, qi + 1, step, jnp.zeros((2 * T, 1), F32))
        o = o_s[...] + jnp.dot(a_s[...], values(0), preferred_element_type=F32)
        o_ref[...] = jnp.where(lane < HEAD_DIM, o[:T], o[T:]).astype(BF)
        tot_ref[...] = jnp.where(lane < HEAD_DIM, run[:T], run[T:])

    gain = pl.BlockSpec((1, LANES), lambda hp, qi: (0, 0))
    full = lambda b0: pl.BlockSpec((S, LANES), lambda hp, qi, b0=b0: (0, b0 + hp))
    tile = pl.BlockSpec((T, LANES), lambda hp, qi: (qi, hp))
    return pl.pallas_call(
        body, name=name, grid=(HEAD_PAIRS, nq),
        in_specs=[pl.BlockSpec((T, LANES), lambda hp, qi: (qi, Q_BLOCK0 + hp)), full(K_BLOCK0), full(V_BLOCK0), gain, gain],
        out_specs=[tile, tile],
        out_shape=[_sds((S, ATTN_W), BF), _sds((S, ATTN_W), F32)],
        scratch_shapes=[pltpu.VMEM((S, LANES), BF), pltpu.VMEM((2, 2 * T, T), F32), pltpu.VMEM((2 * T, T), BF),
                        pltpu.VMEM((2 * T, T), BF), pltpu.VMEM((2 * T, T), F32), pltpu.VMEM((2 * T, T), BF),
                        pltpu.VMEM((2 * T, LANES), F32)],
        compiler_params=_params("arbitrary", "arbitrary"),
    )(proj, proj, proj, gq, gk)


def attn_bwd(name, dy, proj, tot, gq, gk):
    S = proj.shape[0]
    T = _attn_tile(S)
    nq = S // T

    def body(q_ref, k_ref, v_ref, tot_ref, do_ref, gq_ref, gk_ref,
             dq_ref, dk_ref, dv_ref, dgq_ref, dgk_ref, kn_s, dkn_s, dv_s,
             lb_s, z_s, g_s, dq_s, hi_s, lo_s, a_s, ghi_s, glo_s):
        hp = pl.program_id(0)
        qi = pl.program_id(1)
        hm = _head_mean_matrix(LANES)

        @pl.when(qi == 0)
        def _():
            kn_s[...] = _qk_norm(k_ref[...].astype(F32), gk_ref[...], hm)[1].astype(BF)
            dkn_s[...] = jnp.zeros_like(dkn_s)
            dv_s[...] = jnp.zeros_like(dv_s)

        q = q_ref[...].astype(F32)
        rq, qn = _qk_norm(q, gq_ref[...], hm)
        lane = lax.broadcasted_iota(jnp.int32, (T, LANES), 1)
        qst = _stack_heads(qn, lane)
        dost = _stack_heads(do_ref[...].astype(F32), lane)
        total = jnp.concatenate([tot_ref[:, 0:1], tot_ref[:, HEAD_DIM:HEAD_DIM + 1]], axis=0)
        rowi = lax.broadcasted_iota(jnp.int32, (T, T), 0)
        coli = lax.broadcasted_iota(jnp.int32, (T, T), 1)
        u_upto = (rowi <= coli).astype(BF)
        u_before = (rowi < coli).astype(BF)
        diagonal = jnp.where(coli < rowi, 0.0, MASKED)

        on_diagonal = jnp.concatenate([diagonal, diagonal], axis=0)

        def rows(b):
            return pl.ds(pl.multiple_of(jnp.clip(b, 0, qi) * T, T), T)

        def logits(b):
            return lax.dot_general(qst, kn_s[rows(b), :], NT_DIMS, preferred_element_type=F32)

        def keep(b, z):
            bias = jnp.where(b == qi, on_diagonal, jnp.where(b > qi, MASKED, 0.0))
            lb_s[b % 3], hi_s[...], lo_s[...] = _logit_parts(z + bias)

        def step(i, carry):
            run, grun = carry
            both_before = jnp.dot(jnp.concatenate([ghi_s[...], glo_s[...]], axis=0), u_before,
                                  preferred_element_type=F32)
            both_upto = jnp.dot(jnp.concatenate([hi_s[...], lo_s[...]], axis=0), u_upto, preferred_element_type=F32)
            da = lax.dot_general(dost, v_ref[rows(i), :].astype(BF), NT_DIMS, preferred_element_type=F32)
            dv_s[rows(i - 1), :] += lax.dot_general(a_s[...], dost, TN_DIMS, preferred_element_type=F32)
            z_after = logits(i + 2)

            keep(i + 1, z_s[...])

            g = g_s[...]
            before = both_before[:2 * T] + both_before[2 * T:]
            dz = (g - jnp.exp(lb_s[(i + 2) % 3]) * (g + (grun + before))).astype(BF)
            dq_s[...] += jnp.dot(dz, kn_s[rows(i - 1), :], preferred_element_type=F32)
            dkn_s[rows(i - 1), :] += lax.dot_general(dz, qst, TN_DIMS, preferred_element_type=F32)
            grun = grun + before[:, T - 1:T] + g[:, T - 1:T]

            upto = both_upto[:2 * T] + both_upto[2 * T:]
            a = jnp.exp(lb_s[i % 3] + (total - run - upto))
            g = da * a
            a_s[...] = a.astype(BF)
            g_s[...] = g
            ghi = g.astype(BF)
            ghi_s[...] = ghi
            glo_s[...] = (g - ghi.astype(F32)).astype(BF)
            z_s[...] = z_after
            return run + upto[:, T - 1:T], grun

        lb_s[...] = jnp.full(lb_s.shape, MASKED, F32)
        for ref in (a_s, g_s, ghi_s, glo_s, dq_s):
            ref[...] = jnp.zeros_like(ref)
        keep(0, logits(0))
        z_s[...] = logits(1)
        lax.fori_loop(0, qi + 2, step, (jnp.zeros((2 * T, 1), F32), jnp.zeros((2 * T, 1), F32)))
        dqn = jnp.where(lane < HEAD_DIM, dq_s[:T, :], dq_s[T:, :])
        gq_v = gq_ref[...]
        t = dqn * gq_v
        dq_ref[...] = (rq * t - q * (rq * rq * rq) * _head_mean(t * q, hm)).astype(BF)
        dgq = jnp.sum(dqn * (q * rq), axis=0, keepdims=True) * SCALE
        first = jnp.logical_and(hp == 0, qi == 0)

        @pl.when(first)
        def _():
            dgq_ref[...] = dgq

        @pl.when(jnp.logical_not(first))
        def _():
            dgq_ref[...] += dgq

        @pl.when(qi == nq - 1)
        def _():
            k = k_ref[...].astype(F32)
            rk = _qk_norm(k, gk_ref[...], hm)[0]
            dkn = dkn_s[...]
            tk = dkn * gk_ref[...]
            dk_ref[...] = (rk * tk - k * (rk * rk * rk) * _head_mean(tk * k, hm)).astype(BF)
            dgk = jnp.sum(dkn * (k * rk), axis=0, keepdims=True)
            dv_ref[...] = dv_s[...].astype(BF)

            @pl.when(hp == 0)
            def _():
                dgk_ref[...] = dgk

            @pl.when(hp > 0)
            def _():
                dgk_ref[...] += dgk

            @pl.when(hp == HEAD_PAIRS - 1)
            def _():
                fold = (lax.broadcasted_iota(jnp.int32, (LANES, LANES), 0) % HEAD_DIM ==
                        lax.broadcasted_iota(jnp.int32, (LANES, LANES), 1) % HEAD_DIM).astype(F32)
                dgq_ref[...] = jnp.dot(dgq_ref[...], fold, precision=HIGHEST, preferred_element_type=F32)
                dgk_ref[...] = jnp.dot(dgk_ref[...], fold, precision=HIGHEST, preferred_element_type=F32)

    gain = pl.BlockSpec((1, LANES), lambda hp, qi: (0, 0))
    full = lambda b0: pl.BlockSpec((S, LANES), lambda hp, qi, b0=b0: (0, b0 + hp))
    tile = pl.BlockSpec((T, LANES), lambda hp, qi: (qi, hp))
    col = pl.BlockSpec((S, LANES), lambda hp, qi: (0, hp))
    dgain = pl.BlockSpec((1, LANES), lambda hp, qi: (0, 0))
    return pl.pallas_call(
        body, name=name, grid=(HEAD_PAIRS, nq),
        in_specs=[pl.BlockSpec((T, LANES), lambda hp, qi: (qi, Q_BLOCK0 + hp)), full(K_BLOCK0), full(V_BLOCK0),
                  tile, pl.BlockSpec((T, LANES), lambda hp, qi: (qi, 2 + hp)), gain, gain],
        out_specs=[tile, col, col, dgain, dgain],
        out_shape=[_sds((S, ATTN_W), BF)] * 3 + [_sds((1, LANES), F32)] * 2,
        scratch_shapes=[pltpu.VMEM((S, LANES), BF), pltpu.VMEM((S, LANES), F32), pltpu.VMEM((S, LANES), F32),
                        pltpu.VMEM((3, 2 * T, T), F32), pltpu.VMEM((2 * T, T), F32), pltpu.VMEM((2 * T, T), F32),
                        pltpu.VMEM((2 * T, LANES), F32)] + [pltpu.VMEM((2 * T, T), BF)] * 5,
        compiler_params=_params("arbitrary", "arbitrary"),
    )(proj, proj, proj, tot, dy, gq, gk)


def ple_bwd_elem(name, dh, gp, pp, after=()):
    S, D = dh.shape
    tm = min(S, 512)

    def body(dh_ref, gp_ref, pp_ref, *rest):
        dgp_ref, dpp_ref = rest[-2:]
        d = dh_ref[...]
        gate = jax.nn.sigmoid(gp_ref[...].astype(F32))
        dpp_ref[...] = (d * gate).astype(BF)
        dgp_ref[...] = (d * pp_ref[...].astype(F32) * gate * (1.0 - gate)).astype(BF)

    tile = pl.BlockSpec((tm, D), lambda i: (i, 0))
    return pl.pallas_call(
        body, name=name, grid=(S // tm,), in_specs=[tile] * 3 + [pl.BlockSpec(memory_space=pl.ANY)] * len(after),
        out_specs=[tile] * 2, out_shape=[_sds((S, D), BF)] * 2, compiler_params=_params("parallel"),
    )(dh, gp, pp, *after)


def loss_head(name, h, target):
    S, D = h.shape
    tm = min(S, 512)

    def body(h_ref, t_ref, loss_ref, dh_ref):
        i = pl.program_id(0)
        e = h_ref[...] - t_ref[...]
        dh_ref[...] = e * (1.0 / D)
        part = jnp.zeros((8, LANES), F32) + 0.5 * jnp.sum(jnp.mean(e * e, axis=-1, keepdims=True))

        @pl.when(i == 0)
        def _():
            loss_ref[...] = part

        @pl.when(i > 0)
        def _():
            loss_ref[...] += part

    tile = pl.BlockSpec((tm, D), lambda i: (i, 0))
    return pl.pallas_call(
        body, name=name, grid=(S // tm,), in_specs=[tile, tile],
        out_specs=[pl.BlockSpec((8, LANES), lambda i: (0, 0)), tile],
        out_shape=[_sds((8, LANES), F32), _sds((S, D), F32)], compiler_params=_params("arbitrary"),
    )(h, target)


def _adamw_math(w, g, m, v):
    c1 = 1.0 - ADAM_B1 ** ADAM_STEP
    c2 = 1.0 - ADAM_B2 ** ADAM_STEP
    nm = ADAM_B1 * m + (1.0 - ADAM_B1) * g
    nv = ADAM_B2 * v + (1.0 - ADAM_B2) * (g * g)
    return -ADAM_LR * ((nm / c1) / (jnp.sqrt(nv / c2) + ADAM_EPS) + ADAM_WD * w), nm, nv


def adamw(name, w, g, m, v):
    R, C = w.shape
    tr = R
    for cand in (512, 256, 128, 64, 32, 16, 8):
        if R % cand == 0:
            tr = cand
            break

    def body(w_ref, g_ref, m_ref, v_ref, d_ref, nm_ref, nv_ref):
        d_ref[...], nm_ref[...], nv_ref[...] = _adamw_math(w_ref[...], g_ref[...], m_ref[...], v_ref[...])

    tile = pl.BlockSpec((tr, C), lambda i: (i, 0))
    return pl.pallas_call(
        body, name=name, grid=(R // tr,), in_specs=[tile] * 4, out_specs=[tile] * 3,
        out_shape=[_sds((R, C), F32)] * 3, compiler_params=_params("parallel"),
    )(w, g, m, v)


def adamw_layer(name, layer, ws, gs, ms, vs, prev, after=()):
    n = len(ws)
    steps = 8

    def body(*refs):
        ins, outs = refs[:4 * n], refs[-4 * n:]
        for i in range(n):
            w_ref, g_ref, m_ref, v_ref = (ins[k * n + i] for k in range(4))
            g = g_ref[...]
            outs[i][...] = g
            outs[n + i][...], outs[2 * n + i][...], outs[3 * n + i][...] = _adamw_math(w_ref[...], g, m_ref[...],
                                                                                        v_ref[...])

    def stacked(a):
        return pl.BlockSpec((None, a.shape[1] // steps, a.shape[2]), lambda t: (layer, t, 0))

    def flat(a):
        return pl.BlockSpec((a.shape[0] // steps, a.shape[1]), lambda t: (t, 0))

    in_specs = [stacked(a) for a in ws] + [flat(a) for a in gs] + [stacked(a) for a in ms] + [stacked(a) for a in vs]
    operands = [*ws, *gs, *ms, *vs]
    aliases = {}
    if prev is not None:
        flat_prev = [a for group in prev for a in group]
        in_specs += [pl.BlockSpec(memory_space=pl.ANY)] * len(flat_prev)
        aliases = {4 * n + i: i for i in range(4 * n)}
        operands += flat_prev
    in_specs += [pl.BlockSpec(memory_space=pl.ANY)] * len(after)
    operands += list(after)
    out = pl.pallas_call(
        body, name=name, grid=(steps,), in_specs=in_specs, out_specs=[stacked(a) for a in ws] * 4,
        out_shape=[_sds(a.shape, F32) for a in ws] * 4, input_output_aliases=aliases,
        compiler_params=_params("parallel"),
    )(*operands)
    return [list(out[k * n:(k + 1) * n]) for k in range(4)]


def _relu2(u):
    r = jnp.maximum(u.astype(F32), 0.0)
    return r * r


def layer_fwd(tag, h0, p_bf, wt, after=(), mid=None):
    hn1 = rmsnorm(f"{tag}_norm1", h0, wt["norm1_g"], after)
    proj = mm_nn(f"{tag}_proj", hn1, wt["w_in"], tn=1408)
    ya = conv_fwd(f"{tag}_conv", proj, wt["conv_w"])
    yb, yb_tot = attn_fwd(f"{tag}_attn", proj, wt["gq"], wt["gk"])
    yc = sgu_fwd(f"{tag}_sgu", proj, wt["gv"], wt["sgu_w"], wt["sgu_bias"])
    y = jnp.concatenate([ya, yb, yc], axis=-1)
    h1 = mm_nn(f"{tag}_out", y, wt["w_out"], extras=(h0,), epi=lambda acc, h: (h + acc,), out_dtypes=(F32,))
    hn2 = rmsnorm(f"{tag}_norm2", h1, wt["norm2_g"], () if mid is None else mid(yb))
    uu = mm_nn(f"{tag}_ff1", hn2, wt["w_ff1"])
    h2 = mm_nn(f"{tag}_ff2", uu, wt["w_ff2"], pro=_relu2, extras=(h1,), epi=lambda acc, h: (h + acc,),
               out_dtypes=(F32,))
    hn3 = rmsnorm(f"{tag}_norm3", h2, wt["norm3_g"])
    gp = mm_nn(f"{tag}_gate", hn3, wt["w_ple_gate"])
    h3, pp = mm_nn(f"{tag}_ple", p_bf, wt["w_ple_proj"], extras=(gp, h2),
                   epi=lambda acc, g, h: (h + jax.nn.sigmoid(g.astype(F32)) * acc, acc), out_dtypes=(F32, BF))
    saved = dict(h0=h0, h1=h1, h2=h2, hn1=hn1, hn2=hn2, hn3=hn3, proj=proj, yb_tot=yb_tot, y=y, uu=uu, gp=gp, pp=pp,
                 p_bf=p_bf)
    return h3, saved


def layer_bwd(tag, dh3, sv, wt, after=(), mid=None):
    dgp, dpp = ple_bwd_elem(f"{tag}_dple", dh3, sv["gp"], sv["pp"], after)
    g = {}
    g["w_ple_proj"] = mm_tn(f"{tag}_dwp", sv["p_bf"], dpp, col_blocks=N_CHIPS)
    g["w_ple_gate"] = mm_tn(f"{tag}_dwg", sv["hn3"], dgp)
    dh2, g["norm3_g"] = mm_nt_rmsbwd(f"{tag}_dnorm3", dgp, wt["w_ple_gate"], sv["h2"], wt["norm3_g"], dh3)

    duu = mm_nt(f"{tag}_dff2", dh2, wt["w_ff2"], extras=(sv["uu"],),
                epi=lambda acc, u: acc * (2.0 * jnp.maximum(u.astype(F32), 0.0)))
    g["w_ff2"] = mm_tn(f"{tag}_dw2", sv["uu"], dh2, pro_x=_relu2)
    g["w_ff1"] = mm_tn(f"{tag}_dw1", sv["hn2"], duu, col_blocks=N_CHIPS)
    dh1, g["norm2_g"] = mm_nt_rmsbwd(f"{tag}_dnorm2", duu, wt["w_ff1"], sv["h1"], wt["norm2_g"], dh2)

    dy = mm_nt(f"{tag}_dout", dh1, wt["w_out"], after=() if mid is None else mid(dh1))
    g["w_out"] = mm_tn(f"{tag}_dwo", sv["y"], dh1)
    dab, dac, dah, g["conv_w"] = conv_bwd(f"{tag}_dconv", dy, sv["proj"], wt["conv_w"])
    dq, dk, dv, g["gq"], g["gk"] = attn_bwd(f"{tag}_dattn", dy, sv["proj"], sv["yb_tot"], wt["gq"], wt["gk"])
    dcu, dcv, g["sgu_w"], g["sgu_bias"], g["gv"] = sgu_bwd(f"{tag}_dsgu", dy, sv["proj"], wt["gv"], wt["sgu_w"],
                                                           wt["sgu_bias"])
    dproj = jnp.concatenate([dab, dac, dah, dq, dk, dv, dcu, dcv], axis=-1)
    g["w_in"] = mm_tn(f"{tag}_dwi", sv["hn1"], dproj, tn=1408)
    dh0, g["norm1_g"] = mm_nt_rmsbwd(f"{tag}_dnorm1", dproj, wt["w_in"], sv["h0"], wt["norm1_g"], dh1)
    return dh0, g


def prep_small(norm1_g, q_norm_g, k_norm_g, sgu_norm_g, sgu_w, sgu_b, norm2_g, norm3_g, conv_w_full):
    return dict(
        norm1_g=norm1_g, norm2_g=norm2_g, norm3_g=norm3_g, conv_w=conv_w_full,
        gq=(jnp.tile(q_norm_g, 2) * SCALE).reshape(1, LANES), gk=jnp.tile(k_norm_g, 2).reshape(1, LANES),
        gv=sgu_norm_g.reshape(1, SGU_W), sgu_w=sgu_w, sgu_bias=jnp.repeat(sgu_b.T, HEAD_DIM, axis=1))


def small_grads(g):
    return dict(
        norm1_g=g["norm1_g"][0], norm2_g=g["norm2_g"][0], norm3_g=g["norm3_g"][0], conv_w=g["conv_w"],
        q_norm_g=g["gq"][0, :HEAD_DIM], k_norm_g=g["gk"][0, :HEAD_DIM], sgu_norm_g=g["gv"][0], sgu_w=g["sgu_w"],
        sgu_b=g["sgu_bias"][:, :SGU_HEADS].T)


HBM_SPEC = pl.BlockSpec(memory_space=pltpu.HBM)
BIG = ("w_in", "w_out", "w_ff1", "w_ff2", "w_ple_gate", "w_ple_proj")


def _mesh_pos():
    return lax.axis_index("x"), lax.axis_index("y"), lax.axis_index("c")


def _other_chips(x, y):
    return [(1 - x, y), (x, 1 - y), (1 - x, 1 - y)]


def _half(rows, core):
    h = rows // 2
    return pl.ds(pl.multiple_of(core * h, 16), h)


def _remote(src, dst, send_sems, recv_sems, k, to):
    return pltpu.make_async_remote_copy(src_ref=src, dst_ref=dst, send_sem=send_sems.at[k], recv_sem=recv_sems.at[k],
                                        device_id=to, device_id_type=MESH)


SEM_SPEC = pl.BlockSpec(memory_space=pltpu.SEMAPHORE)
ANY_SPEC = pl.BlockSpec(memory_space=pl.ANY)
SIDE_EFFECT = pltpu.SideEffectType.DATAFLOW_SIDE_EFFECTING


def _in_hbm(arrays):
    return [pltpu.with_memory_space_constraint(a, pltpu.HBM) for a in arrays]


def copies_start(name, srcs, lands, plan, after=()):
    ns, nl, na = len(srcs), len(lands), len(after)

    def body(*refs):
        src_refs, land_refs = refs[:ns], refs[ns:ns + nl]
        send_sem, recv_sem = refs[ns + nl + na], refs[ns + nl + na + 1]
        token = refs[-1]
        for src, dst, dev in plan(src_refs, land_refs, *_mesh_pos()):
            pltpu.make_async_remote_copy(src_ref=src, dst_ref=dst, send_sem=send_sem, recv_sem=recv_sem,
                                         device_id=dev, device_id_type=MESH).start()
        token[...] = jnp.zeros_like(token)

    out = pl.pallas_call(
        body, name=name,
        in_specs=[HBM_SPEC] * (ns + nl) + [ANY_SPEC] * na,
        out_specs=(SEM_SPEC, SEM_SPEC, *[HBM_SPEC] * (ns + nl), pl.BlockSpec(memory_space=pltpu.VMEM)),
        out_shape=(pltpu.SemaphoreType.DMA(()), pltpu.SemaphoreType.DMA(()),
                   *[pltpu.HBM(a.shape, a.dtype) for a in (*srcs, *lands)], _sds((8, LANES), F32)),
        input_output_aliases={i: 2 + i for i in range(ns + nl)},
        compiler_params=pltpu.CompilerParams(has_side_effects=SIDE_EFFECT),
    )(*_in_hbm(srcs), *_in_hbm(lands), *after)
    return out[0], out[1], list(out[2:2 + ns]), list(out[2 + ns:2 + ns + nl]), out[-1]


def copies_wait(name, started, plan, after=()):
    send_sem, recv_sem, srcs, lands, _ = started
    ns, nl, na = len(srcs), len(lands), len(after)

    def body(*refs):
        src_refs, land_refs = refs[:ns], refs[ns:ns + nl]
        send_sem, recv_sem = refs[ns + nl], refs[ns + nl + 1]
        for src, dst, dev in plan(src_refs, land_refs, *_mesh_pos()):
            cp = pltpu.make_async_remote_copy(src_ref=src, dst_ref=dst, send_sem=send_sem, recv_sem=recv_sem,
                                              device_id=dev, device_id_type=MESH)
            cp.wait_send()
            cp.wait_recv()

    out = pl.pallas_call(
        body, name=name,
        in_specs=[HBM_SPEC] * (ns + nl) + [SEM_SPEC, SEM_SPEC] + [ANY_SPEC] * na,
        out_specs=[HBM_SPEC] * (ns + nl),
        out_shape=[pltpu.HBM(a.shape, a.dtype) for a in (*srcs, *lands)],
        input_output_aliases={i: i for i in range(ns + nl)},
        compiler_params=pltpu.CompilerParams(has_side_effects=SIDE_EFFECT),
    )(*srcs, *lands, send_sem, recv_sem, *after)
    return list(out[:ns]), list(out[ns:])


def _gather_plan(srcs, lands, x, y, c):
    me = 2 * x + y
    return [(src.at[_half(src.shape[0], c)], land.at[me, _half(src.shape[0], c)], (*chip, c))
            for src, land in zip(srcs, lands) for chip in _other_chips(x, y)]


def _gather_arrivals(srcs, lands, x, y, c):
    return [(src.at[_half(src.shape[0], c)], land.at[2 * chip[0] + chip[1], _half(src.shape[0], c)], (*chip, c))
            for src, land in zip(srcs, lands) for chip in _other_chips(x, y)]


def _forward_plan(srcs, lands, x, y, c):
    me, sibling = 2 * x + y, (x, y, 1 - c)
    out = []
    for src, land in zip(srcs, lands):
        out.append((src, land.at[me], sibling))
        for chip in _other_chips(x, y):
            region = land.at[2 * chip[0] + chip[1], _half(src.shape[0], c)]
            out.append((region, region, sibling))
    return out


def _forward_arrivals(srcs, lands, x, y, c):
    me, sibling = 2 * x + y, (x, y, 1 - c)
    out = []
    for src, land in zip(srcs, lands):
        out.append((src, land.at[me], sibling))
        for chip in _other_chips(x, y):
            slot = land.at[2 * chip[0] + chip[1]]
            out.append((slot.at[_half(src.shape[0], c)], slot.at[_half(src.shape[0], 1 - c)], sibling))
    return out


def _join_plan(srcs, lands, x, y, c):
    return [(land.at[_half(land.shape[0], c)], land.at[_half(land.shape[0], c)], (x, y, 1 - c)) for land in lands]


def _join_arrivals(srcs, lands, x, y, c):
    return [(land.at[_half(land.shape[0], c)], land.at[_half(land.shape[0], 1 - c)], (x, y, 1 - c)) for land in lands]


def _pair_plan(srcs, lands, x, y, c):
    return [(src.at[:, _half(src.shape[1], 1 - c)], land, (x, y, 1 - c)) for src, land in zip(srcs, lands)]


def add_own_half(name, core, grads, got):
    n = len(grads)

    def body(core_ref, *refs):
        for i in range(n):
            refs[2 * n + i][...] = (refs[i][...].astype(F32) + refs[n + i][...].astype(F32)).astype(BF)

    def spec(g, own):
        blk = (None, g.shape[1] // 2, g.shape[2])
        return pl.BlockSpec(blk, (lambda j, core_ref: (j, core_ref[0], 0)) if own else (lambda j, core_ref: (j, 0, 0)))

    return pl.pallas_call(
        body, name=name,
        grid_spec=pltpu.PrefetchScalarGridSpec(
            num_scalar_prefetch=1, grid=(N_CHIPS,),
            in_specs=[spec(g, True) for g in grads] + [spec(g, False) for g in grads],
            out_specs=[spec(g, False) for g in grads]),
        out_shape=[_sds(r.shape, BF) for r in got], compiler_params=_params("parallel"),
    )(core, *grads, *got)


def _chips_plan(srcs, lands, x, y, c):
    return [(src.at[2 * chip[0] + chip[1]], land.at[k], (*chip, c))
            for src, land in zip(srcs, lands) for k, chip in enumerate(_other_chips(x, y))]


def sum_chips(name, place, parts, got):
    n = len(got)

    def body(place_ref, *refs):
        for i in range(n):
            acc = refs[i][...].astype(F32)
            for k in range(N_CHIPS - 1):
                acc = acc + refs[n + i][k].astype(F32)
            refs[2 * n + i][...] = acc

    steps = 2
    return pl.pallas_call(
        body, name=name,
        grid_spec=pltpu.PrefetchScalarGridSpec(
            num_scalar_prefetch=1, grid=(steps,),
            in_specs=[pl.BlockSpec((None, g.shape[1] // steps, g.shape[2]), lambda t, place_ref: (place_ref[0], t, 0))
                      for g in parts] +
                     [pl.BlockSpec((N_CHIPS - 1, g.shape[1] // steps, g.shape[2]), lambda t, place_ref: (0, t, 0))
                      for g in got],
            out_specs=[pl.BlockSpec((g.shape[1] // steps, g.shape[2]),
                                    lambda t, place_ref: (place_ref[1] * steps + t, 0)) for g in got]),
        out_shape=[_sds((2 * g.shape[1], g.shape[2]), F32) for g in got], compiler_params=_params("parallel"),
    )(place, *parts, *got)


def reduce_scatter_pair(tag, grads):
    lands = [lax.empty((N_CHIPS, g.shape[1] // 2, g.shape[2]), g.dtype) for g in grads]
    return copies_start(f"{tag}_rs_pair_start", grads, lands, _pair_plan)


def reduce_scatter_begin(tag, core, pairing, after):
    grads, got = copies_wait(f"{tag}_rs_pair_wait", pairing, _pair_plan, after)
    parts = add_own_half(f"{tag}_rs_add", core, grads, got)
    lands = [lax.empty((N_CHIPS - 1,) + p.shape[1:], p.dtype) for p in parts]
    return copies_start(f"{tag}_rs_start", parts, lands, _chips_plan)


def reduce_scatter_sum(tag, place, started, after):
    parts, got = copies_wait(f"{tag}_rs_wait", started, _chips_plan, after)
    return copies_start(f"{tag}_rs_join_start", [], sum_chips(f"{tag}_rs_sum", place, parts, got), _join_plan)


def reduce_scatter_end(tag, joining, after):
    return copies_wait(f"{tag}_rs_join_wait", joining, _join_arrivals, after)[1]


def small_allreduce(name, x):
    R = x.shape[0]
    H = R // 2

    def body(x_ref, o_ref, pair_ref, chip_ref, send_sems, recv_sems):
        xx, yy, c = _mesh_pos()
        me = 2 * xx + yy
        chips = _other_chips(xx, yy)
        sibling = (xx, yy, 1 - c)
        mine = pl.ds(pl.multiple_of(c * H, 8), H)
        theirs = pl.ds(pl.multiple_of((1 - c) * H, 8), H)
        a = _remote(x_ref.at[theirs], pair_ref.at[theirs], send_sems, recv_sems, 0, sibling)
        a.start()
        a.wait_send()
        _remote(x_ref.at[mine], pair_ref.at[mine], send_sems, recv_sems, 0, sibling).wait_recv()
        chip_ref[me] = x_ref[mine, :] + pair_ref[mine, :]
        sends = []
        for k, chip in enumerate(chips):
            cp = _remote(chip_ref.at[me], chip_ref.at[me], send_sems, recv_sems, 1 + k, (*chip, c))
            cp.start()
            sends.append(cp)
        for k, chip in enumerate(chips):
            slot = chip_ref.at[2 * chip[0] + chip[1]]
            _remote(slot, slot, send_sems, recv_sems, 1 + k, (*chip, c)).wait_recv()
        o_ref[mine, :] = (chip_ref[0] + chip_ref[1]) + (chip_ref[2] + chip_ref[3])
        b = _remote(o_ref.at[mine], o_ref.at[mine], send_sems, recv_sems, 4, sibling)
        b.start()
        b.wait_send()
        _remote(o_ref.at[theirs], o_ref.at[theirs], send_sems, recv_sems, 4, sibling).wait_recv()
        for cp in sends:
            cp.wait_send()

    return pl.pallas_call(
        body, name=name,
        in_specs=[pl.BlockSpec(memory_space=pltpu.VMEM)], out_specs=pl.BlockSpec(memory_space=pltpu.VMEM),
        out_shape=_sds((R, LANES), F32),
        scratch_shapes=[pltpu.VMEM((R, LANES), F32), pltpu.VMEM((N_CHIPS, H, LANES), F32),
                        pltpu.SemaphoreType.DMA((5,)), pltpu.SemaphoreType.DMA((5,))],
        compiler_params=pltpu.CompilerParams(vmem_limit_bytes=VMEM_LIMIT),
    )(x)


WEIGHTS = ("norm1_g", "w_in", "conv_w", "q_norm_g", "k_norm_g", "sgu_norm_g", "sgu_w", "sgu_b", "w_out", "norm2_g",
           "w_ff1", "w_ff2", "norm3_g", "w_ple_gate", "w_ple_proj")
SMALL = ("norm1_g", "norm2_g", "norm3_g", "q_norm_g", "k_norm_g", "sgu_norm_g", "sgu_w", "sgu_b", "conv_w")


def _pack_rows(arrays):
    flat = []
    for a in arrays:
        v = a.reshape(-1)
        flat.append(jnp.pad(v, (0, (-v.shape[0]) % LANES)))
    v = jnp.concatenate(flat)
    v = jnp.pad(v, (0, (-v.shape[0]) % (16 * LANES)))
    return v.reshape(-1, LANES)


def _unpack_rows(packed, shapes):
    out, pos = [], 0
    flat = packed.reshape(-1)
    for shp in shapes:
        size = math.prod(shp)
        out.append(flat[pos:pos + size].reshape(shp))
        pos += size + (-size) % LANES
    return out


def kernel(x, p, norm1_g, w_in, conv_w, q_norm_g, k_norm_g, sgu_norm_g, sgu_w, sgu_b, w_out, norm2_g, w_ff1, w_ff2, norm3_g, w_ple_gate, w_ple_proj, loss_target, m_norm1_g, m_w_in, m_conv_w, m_q_norm_g, m_k_norm_g, m_sgu_norm_g, m_sgu_w, m_sgu_b, m_w_out, m_norm2_g, m_w_ff1, m_w_ff2, m_norm3_g, m_w_ple_gate, m_w_ple_proj, v_norm1_g, v_w_in, v_conv_w, v_q_norm_g, v_k_norm_g, v_sgu_norm_g, v_sgu_w, v_sgu_b, v_w_out, v_norm2_g, v_w_ff1, v_w_ff2, v_norm3_g, v_w_ple_gate, v_w_ple_proj):
    w = dict(norm1_g=norm1_g, w_in=w_in, conv_w=conv_w, q_norm_g=q_norm_g, k_norm_g=k_norm_g, sgu_norm_g=sgu_norm_g,
             sgu_w=sgu_w, sgu_b=sgu_b, w_out=w_out, norm2_g=norm2_g, w_ff1=w_ff1, w_ff2=w_ff2, norm3_g=norm3_g,
             w_ple_gate=w_ple_gate, w_ple_proj=w_ple_proj)
    m = dict(norm1_g=m_norm1_g, w_in=m_w_in, conv_w=m_conv_w, q_norm_g=m_q_norm_g, k_norm_g=m_k_norm_g,
             sgu_norm_g=m_sgu_norm_g, sgu_w=m_sgu_w, sgu_b=m_sgu_b, w_out=m_w_out, norm2_g=m_norm2_g, w_ff1=m_w_ff1,
             w_ff2=m_w_ff2, norm3_g=m_norm3_g, w_ple_gate=m_w_ple_gate, w_ple_proj=m_w_ple_proj)
    v = dict(norm1_g=v_norm1_g, w_in=v_w_in, conv_w=v_conv_w, q_norm_g=v_q_norm_g, k_norm_g=v_k_norm_g,
             sgu_norm_g=v_sgu_norm_g, sgu_w=v_sgu_w, sgu_b=v_sgu_b, w_out=v_w_out, norm2_g=v_norm2_g, w_ff1=v_w_ff1,
             w_ff2=v_w_ff2, norm3_g=v_norm3_g, w_ple_gate=v_w_ple_gate, w_ple_proj=v_w_ple_proj)
    depth = w_in.shape[0]
    d_model = x.shape[-1]
    chip = 2 * lax.axis_index("x") + lax.axis_index("y")
    core = lax.axis_index("c")
    core_arr = core.reshape(1).astype(jnp.int32)

    cw_cols = conv_w.shape[-1]
    placed = lax.dynamic_update_slice(jnp.zeros((depth, 3, CONV_W), F32), conv_w, (0, 0, chip * cw_cols))
    placed = jnp.where(core == 0, placed, 0.0)
    conv_full = _unpack_rows(small_allreduce("conv_w_gather", _pack_rows([placed])), [(depth, 3, CONV_W)])[0]

    h = x[0]
    p_bf = p[:, 0].astype(BF)
    saved, full = [], []

    def gather_start(l, after):
        shards = [w[n][l].astype(BF) for n in BIG]
        lands = [lax.empty((N_CHIPS,) + s.shape, BF) for s in shards]
        return copies_start(f"l{l}_gather_start", shards, lands, _gather_plan, after)

    def gather_forward(l, started, after):
        shards, lands = copies_wait(f"l{l}_gather_wait", started, _gather_arrivals, after)
        forwarding = copies_start(f"l{l}_forward_start", shards, lands, _forward_plan)
        return forwarding, gather_start(l + 1, (forwarding[-1],)) if l + 1 < depth else None

    forwarding, started = gather_forward(0, gather_start(0, ()), (h,))
    for l in range(depth):
        g_in, g_out, g_ff1, g_ff2, g_gate, g_proj = copies_wait(f"l{l}_forward_wait", forwarding, _forward_arrivals,
                                                                 (h,))[1]
        nxt = {}

        def mid(arr, l=l, started=started, nxt=nxt):
            nxt["forwarding"], nxt["started"] = gather_forward(l + 1, started, (arr,))
            return [t[-1] for t in nxt.values() if t is not None]

        wt = prep_small(norm1_g[l], q_norm_g[l], k_norm_g[l], sgu_norm_g[l], sgu_w[l], sgu_b[l], norm2_g[l], norm3_g[l],
                        conv_full[l])
        wt["w_in"] = jnp.transpose(g_in, (1, 0, 2)).reshape(d_model, -1)
        wt["w_out"] = g_out.reshape(-1, d_model)
        wt["w_ff1"] = g_ff1
        wt["w_ff2"] = g_ff2.reshape(-1, d_model)
        wt["w_ple_gate"] = g_gate.reshape(-1, d_model)
        wt["w_ple_proj"] = g_proj
        last = l + 1 == depth
        h, sv = layer_fwd(f"l{l}", h, p_bf[l], wt, () if last else (started[-1],), None if last else mid)
        if not last:
            forwarding, started = nxt["forwarding"], nxt["started"]
        saved.append(sv)
        full.append(wt)

    loss_tile, dh = loss_head("loss", h, loss_target[0])
    loss = lax.psum(loss_tile[0, 0], ("x", "y", "c"))

    small = [None] * depth
    chip_arr = jnp.stack([chip, core]).astype(jnp.int32)
    big_w, big_m, big_v = ([d[n] for n in BIG] for d in (w, m, v))
    pairing, joining, tokens = None, {}, ()
    for l in reversed(range(depth)):
        box = {}

        def mid(arr, l=l, pairing=pairing, box=box):
            box["started"] = reduce_scatter_begin(f"l{l + 1}", core_arr, pairing, (arr,))
            return (box["started"][-1],)

        dh, g = layer_bwd(f"l{l}", dh, saved[l], full[l], tokens, None if pairing is None else mid)
        tokens = ()
        if pairing is not None:
            joining[l + 1] = reduce_scatter_sum(f"l{l + 1}", chip_arr, box["started"], (dh,))
            tokens = (joining[l + 1][-1],)
        small[l] = small_grads(g)
        shards_in = w_in.shape[-1]
        gl = [jnp.transpose(g["w_in"].reshape(d_model, N_CHIPS, shards_in), (1, 0, 2)),
              g["w_out"].reshape(N_CHIPS, -1, d_model), g["w_ff1"], g["w_ff2"].reshape(N_CHIPS, -1, d_model),
              g["w_ple_gate"].reshape(N_CHIPS, -1, d_model), g["w_ple_proj"]]
        pairing = reduce_scatter_pair(f"l{l}", gl)
        tokens += (pairing[-1],)

    started = reduce_scatter_begin("l0", core_arr, pairing, (dh,))
    updated = None
    for l in reversed(range(1, depth)):
        reduced = reduce_scatter_end(f"l{l}", joining[l], (started[-1],))
        updated = adamw_layer(f"l{l}_adamw", l, big_w, reduced, big_m, big_v, updated)
    grads, delta, new_m, new_v = {}, {}, {}, {}
    packed = _pack_rows([small[l][n] for l in range(depth) for n in SMALL])
    shapes = [small[l][n].shape for l in range(depth) for n in SMALL]
    pieces = _unpack_rows(small_allreduce("small_grads", packed), shapes)
    for i, n in enumerate(SMALL):
        grads[n] = jnp.stack([pieces[l * len(SMALL) + i] for l in range(depth)])
    grads["conv_w"] = lax.dynamic_slice(grads["conv_w"], (0, 0, chip * cw_cols), (depth, 3, cw_cols))
    for n in SMALL:
        shp = w[n].shape
        two_d = (-1, shp[-1]) if n != "sgu_w" else (-1, LANES)
        d, nm, nv = adamw(f"adamw_{n}", w[n].reshape(two_d), grads[n].reshape(two_d), m[n].reshape(two_d),
                          v[n].reshape(two_d))
        delta[n], new_m[n], new_v[n] = d.reshape(shp), nm.reshape(shp), nv.reshape(shp)

    done = [new_v[n] for n in SMALL] + ([] if updated is None else [updated[3][0]])
    reduced = reduce_scatter_end("l0", reduce_scatter_sum("l0", chip_arr, started, (dh, *done)), ())
    updated = adamw_layer("l0_adamw", 0, big_w, reduced, big_m, big_v, updated)
    for k, d in enumerate((grads, delta, new_m, new_v)):
        d.update(zip(BIG, updated[k]))

    return (loss, dh[None], *[grads[n] for n in WEIGHTS], *[delta[n] for n in WEIGHTS], *[new_m[n] for n in WEIGHTS],
            *[new_v[n] for n in WEIGHTS])
```

```python
import functools
import math

import jax
import jax.numpy as jnp
from jax import lax
from jax.experimental import pallas as pl
from jax.experimental.pallas import tpu as pltpu

F32 = jnp.float32
BF = jnp.bfloat16
MESH = pl.DeviceIdType.MESH
HIGHEST = lax.Precision.HIGHEST

EPS = 1e-6
HEAD_DIM = 64
CONV_W = 256
ATTN_W = 512
SGU_W = 256
CHUNK = 128
N_CHIPS = 4
SCALE = HEAD_DIM ** -0.5
LANES = 128
VMEM_LIMIT = 56 * 1024 * 1024

ADAM_LR = 0.001
ADAM_B1 = 0.9
ADAM_B2 = 0.999
ADAM_EPS = 1e-08
ADAM_WD = 0.01
ADAM_STEP = 10

NT_DIMS = (((1,), (1,)), ((), ()))
TN_DIMS = (((0,), (0,)), ((), ()))


def _params(*sem):
    return pltpu.CompilerParams(dimension_semantics=sem if sem else None, vmem_limit_bytes=VMEM_LIMIT)


def _sds(shape, dtype):
    return jax.ShapeDtypeStruct(shape, dtype)


def _erf(x):
    return lax.erf(x)


def _gelu(x):
    return 0.5 * x * (1.0 + _erf(x * (2.0 ** -0.5)))


def _gelu_grad(x):
    return 0.5 * (1.0 + _erf(x * (2.0 ** -0.5))) + x * jnp.exp(-0.5 * x * x) * (1.0 / math.sqrt(2.0 * math.pi))


def _log_sigmoid(z):
    return jnp.minimum(z, 0.0) - jnp.log(1.0 + jnp.exp(-jnp.abs(z)))


def _head_mean_matrix(width):
    r = lax.broadcasted_iota(jnp.int32, (width, width), 0) // HEAD_DIM
    c = lax.broadcasted_iota(jnp.int32, (width, width), 1) // HEAD_DIM
    return (r == c).astype(BF)


def _head_mean(x, m):
    hi = x.astype(BF)
    lo = (x - hi.astype(F32)).astype(BF)
    return _dot2_stacked(hi, lo, m) * (1.0 / HEAD_DIM)


def rmsnorm(name, h, g, after=()):
    S, D = h.shape
    tm = min(S, 512)

    def body(h_ref, g_ref, *rest):
        x = h_ref[...]
        r = lax.rsqrt(jnp.mean(x * x, axis=-1, keepdims=True) + EPS)
        rest[-1][...] = ((x * r) * g_ref[...]).astype(BF)

    return pl.pallas_call(
        body, name=name, grid=(S // tm,),
        in_specs=[pl.BlockSpec((tm, D), lambda i: (i, 0)), pl.BlockSpec((1, D), lambda i: (0, 0))] +
                 [pl.BlockSpec(memory_space=pl.ANY)] * len(after),
        out_specs=pl.BlockSpec((tm, D), lambda i: (i, 0)),
        out_shape=_sds((S, D), BF), compiler_params=_params("parallel"),
    )(h, g.reshape(1, D), *after)


def mm_nn(name, x, w, *, extras=(), pro=None, epi=None, out_dtypes=None, tm=None, tn=512):
    S, K = x.shape
    if w.ndim == 3:
        J, _, tn = w.shape
        N = J * tn
        w_spec = pl.BlockSpec((None, K, tn), lambda n, m: (n, 0, 0))
    else:
        N = w.shape[1]
        tn = min(tn, N)
        w_spec = pl.BlockSpec((K, tn), lambda n, m: (0, n))
    tm = S if tm is None else min(tm, S)
    out_dtypes = (BF,) if out_dtypes is None else out_dtypes
    n_ex, n_out = len(extras), len(out_dtypes)

    def body(x_ref, w_ref, *rest):
        xv = x_ref[...]
        if pro is not None:
            xv = pro(xv)
        acc = jnp.dot(xv.astype(BF), w_ref[...], preferred_element_type=F32)
        outs = (acc,) if epi is None else epi(acc, *[e[...] for e in rest[:n_ex]])
        for o_ref, o in zip(rest[n_ex:], outs):
            o_ref[...] = o.astype(o_ref.dtype)

    tile = pl.BlockSpec((tm, tn), lambda n, m: (m, n))
    out = pl.pallas_call(
        body, name=name, grid=(N // tn, S // tm),
        in_specs=[pl.BlockSpec((tm, K), lambda n, m: (m, 0)), w_spec] + [tile] * n_ex,
        out_specs=[tile] * n_out,
        out_shape=[_sds((S, N), d) for d in out_dtypes],
        compiler_params=_params("parallel", "parallel"),
    )(x, w, *extras)
    return out[0] if n_out == 1 else out


def mm_nt(name, dy, w, *, extras=(), epi=None, tk=512, after=()):
    S, N = dy.shape
    K = w.shape[0]
    tk = min(tk, K)
    n_ex = len(extras)

    def body(dy_ref, w_ref, *rest):
        dyb = rest[-1]

        @pl.when(pl.program_id(0) == 0)
        def _():
            dyb[...] = dy_ref[...].astype(BF)

        acc = lax.dot_general(dyb[...], w_ref[...], NT_DIMS, preferred_element_type=F32)
        if epi is not None:
            acc = epi(acc, *[e[...] for e in rest[:n_ex]])
        rest[-2][...] = acc.astype(BF)

    col = pl.BlockSpec((S, tk), lambda k: (0, k))
    return pl.pallas_call(
        body, name=name, grid=(K // tk,),
        in_specs=[pl.BlockSpec((S, N), lambda k: (0, 0)), pl.BlockSpec((tk, N), lambda k: (k, 0))] + [col] * n_ex +
                 [pl.BlockSpec(memory_space=pl.ANY)] * len(after),
        out_specs=col, out_shape=_sds((S, K), BF), scratch_shapes=[pltpu.VMEM((S, N), BF)],
        compiler_params=_params("arbitrary"),
    )(dy, w, *extras, *after)


def mm_nt_rmsbwd(name, dy, w, h, g, dres, *, tm=256):
    S, N = dy.shape
    D = h.shape[1]
    tm = min(tm, S)
    blocked = w.ndim == 3
    nj = w.shape[2] if blocked else N

    def body(dy_ref, w_ref, h_ref, g_ref, dres_ref, dh_ref, dg_ref):
        i = pl.program_id(0)
        if blocked:
            dyn = None
            for j in range(w.shape[0]):
                part = lax.dot_general(dy_ref[:, j * nj:(j + 1) * nj].astype(BF), w_ref[j], NT_DIMS,
                                       preferred_element_type=F32)
                dyn = part if dyn is None else dyn + part
        else:
            dyn = lax.dot_general(dy_ref[...].astype(BF), w_ref[...], NT_DIMS, preferred_element_type=F32)
        x = h_ref[...]
        r = lax.rsqrt(jnp.mean(x * x, axis=-1, keepdims=True) + EPS)
        t = dyn * g_ref[...]
        dh_ref[...] = dres_ref[...] + r * t - x * (r * r * r) * jnp.mean(t * x, axis=-1, keepdims=True)
        part = jnp.sum(dyn * (x * r), axis=0, keepdims=True)

        @pl.when(i == 0)
        def _():
            dg_ref[...] = part

        @pl.when(i > 0)
        def _():
            dg_ref[...] += part

    w_spec = pl.BlockSpec(w.shape, (lambda i: (0, 0, 0)) if blocked else (lambda i: (0, 0)))
    row = pl.BlockSpec((tm, D), lambda i: (i, 0))
    vec = pl.BlockSpec((1, D), lambda i: (0, 0))
    return pl.pallas_call(
        body, name=name, grid=(S // tm,),
        in_specs=[pl.BlockSpec((tm, N), lambda i: (i, 0)), w_spec, row, vec, row],
        out_specs=[row, vec], out_shape=[_sds((S, D), F32), _sds((1, D), F32)],
        compiler_params=_params("arbitrary"),
    )(dy, w, h, g.reshape(1, D), dres)


def mm_tn(name, x, dy, *, pro_x=None, col_blocks=None, tk=1024, tn=1024):
    S, K = x.shape
    N = dy.shape[1]
    tk = min(tk, K)
    if col_blocks is not None:
        tn = N // col_blocks
        out_shape = _sds((col_blocks, K, tn), BF)
        out_spec = pl.BlockSpec((None, tk, tn), lambda k, n: (n, k, 0))
    else:
        tn = min(tn, N)
        out_shape = _sds((K, N), BF)
        out_spec = pl.BlockSpec((tk, tn), lambda k, n: (k, n))

    def body(x_ref, dy_ref, o_ref):
        xv = x_ref[...]
        if pro_x is not None:
            xv = pro_x(xv)
        o_ref[...] = lax.dot_general(xv.astype(BF), dy_ref[...].astype(BF), TN_DIMS,
                                     preferred_element_type=F32).astype(BF)

    return pl.pallas_call(
        body, name=name, grid=(K // tk, N // tn),
        in_specs=[pl.BlockSpec((S, tk), lambda k, n: (0, k)), pl.BlockSpec((S, tn), lambda k, n: (0, n))],
        out_specs=out_spec, out_shape=out_shape, compiler_params=_params("parallel", "parallel"),
    )(x, dy)


def _conv_parts(ac_ref, ah_ref, cw):
    a_c = ac_ref[...].astype(F32)
    a_h = ah_ref[...].astype(F32)
    x = a_c * a_h
    row = lax.broadcasted_iota(jnp.int32, x.shape, 0)
    x1 = jnp.where(row >= 1, pltpu.roll(x, 1, 0), 0.0)
    x2 = jnp.where(row >= 2, pltpu.roll(x, 2, 0), 0.0)
    cv = cw[0:1] * x2 + cw[1:2] * x1 + cw[2:3] * x
    return a_c, a_h, x, x1, x2, cv, row


def conv_fwd(name, proj, cw):
    S = proj.shape[0]

    def body(ab_ref, ac_ref, ah_ref, cw_ref, o_ref):
        cv = _conv_parts(ac_ref, ah_ref, cw_ref[...])[5]
        o_ref[...] = (ab_ref[...].astype(F32) * cv).astype(BF)

    col = lambda j: pl.BlockSpec((S, CONV_W), lambda i, j=j: (0, j))
    return pl.pallas_call(
        body, name=name, grid=(1,),
        in_specs=[col(0), col(1), col(2), pl.BlockSpec((3, CONV_W), lambda i: (0, 0))],
        out_specs=pl.BlockSpec((S, CONV_W), lambda i: (0, 0)),
        out_shape=_sds((S, CONV_W), BF), compiler_params=_params("arbitrary"),
    )(proj, proj, proj, cw)


def conv_bwd(name, dy, proj, cw):
    S = proj.shape[0]

    def body(dy_ref, ab_ref, ac_ref, ah_ref, cw_ref, dab_ref, dac_ref, dah_ref, dcw_ref):
        w = cw_ref[...]
        a_c, a_h, x, x1, x2, cv, row = _conv_parts(ac_ref, ah_ref, w)
        d = dy_ref[...].astype(F32)
        dab_ref[...] = (d * cv).astype(BF)
        dcv = d * ab_ref[...].astype(F32)
        d1 = jnp.where(row < S - 1, pltpu.roll(dcv, S - 1, 0), 0.0)
        d2 = jnp.where(row < S - 2, pltpu.roll(dcv, S - 2, 0), 0.0)
        dx = w[2:3] * dcv + w[1:2] * d1 + w[0:1] * d2
        dac_ref[...] = (dx * a_h).astype(BF)
        dah_ref[...] = (dx * a_c).astype(BF)
        dcw_ref[0:1, :] = jnp.sum(dcv * x2, axis=0, keepdims=True)
        dcw_ref[1:2, :] = jnp.sum(dcv * x1, axis=0, keepdims=True)
        dcw_ref[2:3, :] = jnp.sum(dcv * x, axis=0, keepdims=True)

    col = lambda j: pl.BlockSpec((S, CONV_W), lambda i, j=j: (0, j))
    one = pl.BlockSpec((S, CONV_W), lambda i: (0, 0))
    small = pl.BlockSpec((3, CONV_W), lambda i: (0, 0))
    return pl.pallas_call(
        body, name=name, grid=(1,),
        in_specs=[col(0), col(0), col(1), col(2), small],
        out_specs=[one, one, one, small],
        out_shape=[_sds((S, CONV_W), BF)] * 3 + [_sds((3, CONV_W), F32)],
        compiler_params=_params("arbitrary"),
    )(dy, proj, proj, proj, cw)


SGU_HEADS = SGU_W // HEAD_DIM
CU_BLOCK = 2304 // SGU_W
CV_BLOCK = 2560 // SGU_W


def _sgu_common(cu_ref, cv_ref, gv_ref, tm):
    c_u = cu_ref[...].astype(F32)
    c_v = cv_ref[...].astype(F32)
    hm = _head_mean_matrix(SGU_W)
    u = _gelu(c_u)
    vg = _gelu(c_v)
    r = lax.rsqrt(_head_mean(vg * vg, hm) + EPS)
    vv = (vg * r) * gv_ref[...]
    head = lax.broadcasted_iota(jnp.int32, (CHUNK, SGU_W), 1) // HEAD_DIM
    tri = (lax.broadcasted_iota(jnp.int32, (CHUNK, CHUNK), 0) >=
           lax.broadcasted_iota(jnp.int32, (CHUNK, CHUNK), 1))
    return c_u, c_v, hm, u, vg, r, vv, head, tri


def _sgu_mix(w_ref, tri, head, vvc, bias):
    sv = bias
    for g in range(SGU_HEADS):
        wg = jnp.where(tri, w_ref[g], 0.0).astype(BF)
        sv = sv + jnp.where(head == g, jnp.dot(wg, vvc, preferred_element_type=F32), 0.0)
    return sv


def sgu_fwd(name, proj, gv, w, bias):
    S = proj.shape[0]
    tm = min(S, 512)

    def body(cu_ref, cv_ref, gv_ref, w_ref, b_ref, o_ref):
        _, _, _, u, _, _, vv, head, tri = _sgu_common(cu_ref, cv_ref, gv_ref, tm)
        vvb = vv.astype(BF)
        for ch in range(tm // CHUNK):
            rows = slice(ch * CHUNK, (ch + 1) * CHUNK)
            sv = _sgu_mix(w_ref, tri, head, vvb[rows], b_ref[...])
            o_ref[rows, :] = (u[rows] * sv).astype(BF)

    const = lambda shape: pl.BlockSpec(shape, lambda i: (0,) * len(shape))
    return pl.pallas_call(
        body, name=name, grid=(S // tm,),
        in_specs=[pl.BlockSpec((tm, SGU_W), lambda i: (i, CU_BLOCK)), pl.BlockSpec((tm, SGU_W), lambda i: (i, CV_BLOCK)),
                  const((1, SGU_W)), const((SGU_HEADS, CHUNK, CHUNK)), const((CHUNK, SGU_W))],
        out_specs=pl.BlockSpec((tm, SGU_W), lambda i: (i, 0)),
        out_shape=_sds((S, SGU_W), BF), compiler_params=_params("parallel"),
    )(proj, proj, gv, w, bias)


def sgu_bwd(name, dy, proj, gv, w, bias):
    S = proj.shape[0]
    tm = min(S, 512)

    def body(dy_ref, cu_ref, cv_ref, gv_ref, w_ref, b_ref, dcu_ref, dcv_ref, dw_ref, db_ref, dgv_ref, dvv_s):
        i = pl.program_id(0)
        c_u, c_v, hm, u, vg, r, vv, head, tri = _sgu_common(cu_ref, cv_ref, gv_ref, tm)
        vvb = vv.astype(BF)
        d = dy_ref[...].astype(F32)
        ind = (lax.broadcasted_iota(jnp.int32, (SGU_W, LANES), 0) // HEAD_DIM ==
               lax.broadcasted_iota(jnp.int32, (SGU_W, LANES), 1)).astype(BF)
        dw_acc = [jnp.zeros((CHUNK, CHUNK), F32) for _ in range(SGU_HEADS)]
        db_acc = jnp.zeros((CHUNK, LANES), F32)
        for ch in range(tm // CHUNK):
            rows = slice(ch * CHUNK, (ch + 1) * CHUNK)
            sv = _sgu_mix(w_ref, tri, head, vvb[rows], b_ref[...])
            dcu_ref[rows, :] = (d[rows] * sv * _gelu_grad(c_u[rows])).astype(BF)
            dsv = d[rows] * u[rows]
            dsv_hi = dsv.astype(BF)
            db_acc = db_acc + _dot2_stacked(dsv_hi, (dsv - dsv_hi.astype(F32)).astype(BF), ind)
            dvv = jnp.zeros((CHUNK, SGU_W), F32)
            for g in range(SGU_HEADS):
                dsv_g = jnp.where(head == g, dsv, 0.0).astype(BF)
                wg = jnp.where(tri, w_ref[g], 0.0).astype(BF)
                dvv = dvv + lax.dot_general(wg, dsv_g, TN_DIMS, preferred_element_type=F32)
                dw_acc[g] = dw_acc[g] + lax.dot_general(dsv_g, vvb[rows], NT_DIMS, preferred_element_type=F32)
            dvv_s[rows, :] = dvv
        dvv = dvv_s[...]
        gvv = gv_ref[...]
        t = dvv * gvv
        dvg = r * t - vg * (r * r * r) * _head_mean(t * vg, hm)
        dcv_ref[...] = (dvg * _gelu_grad(c_v)).astype(BF)
        dgv = jnp.sum(dvv * (vg * r), axis=0, keepdims=True)

        @pl.when(i == 0)
        def _():
            for g in range(SGU_HEADS):
                dw_ref[g] = jnp.where(tri, dw_acc[g], 0.0)
            db_ref[...] = db_acc
            dgv_ref[...] = dgv

        @pl.when(i > 0)
        def _():
            for g in range(SGU_HEADS):
                dw_ref[g] += jnp.where(tri, dw_acc[g], 0.0)
            db_ref[...] += db_acc
            dgv_ref[...] += dgv

    const = lambda shape: pl.BlockSpec(shape, lambda i: (0,) * len(shape))
    tile = pl.BlockSpec((tm, SGU_W), lambda i: (i, 0))
    return pl.pallas_call(
        body, name=name, grid=(S // tm,),
        in_specs=[pl.BlockSpec((tm, SGU_W), lambda i: (i, 3)),
                  pl.BlockSpec((tm, SGU_W), lambda i: (i, CU_BLOCK)), pl.BlockSpec((tm, SGU_W), lambda i: (i, CV_BLOCK)),
                  const((1, SGU_W)), const((SGU_HEADS, CHUNK, CHUNK)), const((CHUNK, SGU_W))],
        out_specs=[tile, tile, const((SGU_HEADS, CHUNK, CHUNK)), const((CHUNK, LANES)), const((1, SGU_W))],
        out_shape=[_sds((S, SGU_W), BF), _sds((S, SGU_W), BF), _sds((SGU_HEADS, CHUNK, CHUNK), F32),
                   _sds((CHUNK, LANES), F32), _sds((1, SGU_W), F32)],
        scratch_shapes=[pltpu.VMEM((tm, SGU_W), F32)],
        compiler_params=_params("arbitrary"),
    )(dy, proj, proj, gv, w, bias)


HEAD_PAIRS = ATTN_W // LANES
Q_BLOCK0 = 768 // LANES
K_BLOCK0 = 1280 // LANES
V_BLOCK0 = 1792 // LANES


def _attn_tile(S):
    return min(S, 256)


def _qk_norm(x, g, hm):
    r = lax.rsqrt(_head_mean(x * x, hm) + EPS)
    return r, (x * r) * g


MASKED = -1e30


def _logit_parts(z):
    lb = _log_sigmoid(z)
    lr = lb - z
    hi = lr.astype(BF)
    return lb, hi, (lr - hi.astype(F32)).astype(BF)


def _stack_heads(x, lane):
    return jnp.concatenate([jnp.where(lane < HEAD_DIM, x, 0.0), jnp.where(lane >= HEAD_DIM, x, 0.0)],
                           axis=0).astype(BF)


def _dot2_stacked(hi, lo, u):
    rows = hi.shape[0]
    both = jnp.dot(jnp.concatenate([hi, lo], axis=0), u, preferred_element_type=F32)
    return both[:rows] + both[rows:]


def attn_fwd(name, proj, gq, gk):
    S = proj.shape[0]
    T = _attn_tile(S)
    nq = S // T

    def body(q_ref, k_ref, v_ref, gq_ref, gk_ref, o_ref, tot_ref, kn_s, lb_s, hi_s, lo_s, z_s, a_s, o_s):
        qi = pl.program_id(1)
        hm = _head_mean_matrix(LANES)

        @pl.when(qi == 0)
        def _():
            kn_s[...] = _qk_norm(k_ref[...].astype(F32), gk_ref[...], hm)[1].astype(BF)

        qn = _qk_norm(q_ref[...].astype(F32), gq_ref[...], hm)[1]
        lane = lax.broadcasted_iota(jnp.int32, (T, LANES), 1)
        qst = _stack_heads(qn, lane)
        rowi = lax.broadcasted_iota(jnp.int32, (T, T), 0)
        coli = lax.broadcasted_iota(jnp.int32, (T, T), 1)
        u_excl = (rowi > coli).astype(BF)
        diagonal = jnp.where(coli < rowi, 0.0, MASKED)

        def logits(j):
            return lax.dot_general(qst, kn_s[pl.ds(pl.multiple_of(j * T, T), T), :], NT_DIMS,
                                   preferred_element_type=F32)

        def values(j):
            return v_ref[pl.ds(pl.multiple_of(j * T, T), T), :].astype(BF)

        def keep(slot, z):
            lb_s[slot], hi_s[...], lo_s[...] = _logit_parts(z)

        def step(it, carry):
            run = carry
            j = qi - it
            hi, lo = hi_s[...], lo_s[...]
            both = jnp.dot(jnp.concatenate([hi, lo], axis=0), u_excl, preferred_element_type=F32)
            o_s[...] += jnp.dot(a_s[...], values(jnp.minimum(j + 1, qi)), preferred_element_type=F32)
            z_after = logits(jnp.maximum(j - 2, 0))
            first = hi[:, 0:1].astype(F32) + lo[:, 0:1].astype(F32)
            keep((it + 1) % 2, z_s[...])
            later = both[:2 * T] + both[2 * T:]
            a_s[...] = jnp.exp(lb_s[it % 2] + later + run).astype(BF)
            z_s[...] = z_after
            return run + later[:, 0:1] + first

        keep(0, logits(qi) + jnp.concatenate([diagonal, diagonal], axis=0))
        z_s[...] = logits(jnp.maximum(qi - 1, 0))
        a_s[...] = jnp.zeros_like(a_s)
        o_s[...] = jnp.zeros_like(o_s)
        run = lax.fori_loop(0, qi + 1, step, jnp.zeros((2 * T, 1), F32))
        o = o_s[...] + jnp.dot(a_s[...], values(0), preferred_element_type=F32)
        o_ref[...] = jnp.where(lane < HEAD_DIM, o[:T], o[T:]).astype(BF)
        tot_ref[...] = jnp.where(lane < HEAD_DIM, run[:T], run[T:])

    gain = pl.BlockSpec((1, LANES), lambda hp, qi: (0, 0))
    full = lambda b0: pl.BlockSpec((S, LANES), lambda hp, qi, b0=b0: (0, b0 + hp))
    tile = pl.BlockSpec((T, LANES), lambda hp, qi: (qi, hp))
    return pl.pallas_call(
        body, name=name, grid=(HEAD_PAIRS, nq),
        in_specs=[pl.BlockSpec((T, LANES), lambda hp, qi: (qi, Q_BLOCK0 + hp)), full(K_BLOCK0), full(V_BLOCK0), gain, gain],
        out_specs=[tile, tile],
        out_shape=[_sds((S, ATTN_W), BF), _sds((S, ATTN_W), F32)],
        scratch_shapes=[pltpu.VMEM((S, LANES), BF), pltpu.VMEM((2, 2 * T, T), F32), pltpu.VMEM((2 * T, T), BF),
                        pltpu.VMEM((2 * T, T), BF), pltpu.VMEM((2 * T, T), F32), pltpu.VMEM((2 * T, T), BF),
                        pltpu.VMEM((2 * T, LANES), F32)],
        compiler_params=_params("arbitrary", "arbitrary"),
    )(proj, proj, proj, gq, gk)


def attn_bwd(name, dy, proj, tot, gq, gk):
    S = proj.shape[0]
    T = _attn_tile(S)
    nq = S // T

    def body(q_ref, k_ref, v_ref, tot_ref, do_ref, gq_ref, gk_ref,
             dq_ref, dk_ref, dv_ref, dgq_ref, dgk_ref, kn_s, dkn_s, dv_s,
             lb_s, z_s, g_s, dq_s, hi_s, lo_s, a_s, ghi_s, glo_s):
        hp = pl.program_id(0)
        qi = pl.program_id(1)
        hm = _head_mean_matrix(LANES)

        @pl.when(qi == 0)
        def _():
            kn_s[...] = _qk_norm(k_ref[...].astype(F32), gk_ref[...], hm)[1].astype(BF)
            dkn_s[...] = jnp.zeros_like(dkn_s)
            dv_s[...] = jnp.zeros_like(dv_s)

        q = q_ref[...].astype(F32)
        rq, qn = _qk_norm(q, gq_ref[...], hm)
        lane = lax.broadcasted_iota(jnp.int32, (T, LANES), 1)
        qst = _stack_heads(qn, lane)
        dost = _stack_heads(do_ref[...].astype(F32), lane)
        total = jnp.concatenate([tot_ref[:, 0:1], tot_ref[:, HEAD_DIM:HEAD_DIM + 1]], axis=0)
        rowi = lax.broadcasted_iota(jnp.int32, (T, T), 0)
        coli = lax.broadcasted_iota(jnp.int32, (T, T), 1)
        u_upto = (rowi <= coli).astype(BF)
        u_before = (rowi < coli).astype(BF)
        diagonal = jnp.where(coli < rowi, 0.0, MASKED)

        on_diagonal = jnp.concatenate([diagonal, diagonal], axis=0)

        def rows(b):
            return pl.ds(pl.multiple_of(jnp.clip(b, 0, qi) * T, T), T)

        def logits(b):
            return lax.dot_general(qst, kn_s[rows(b), :], NT_DIMS, preferred_element_type=F32)

        def keep(b, z):
            bias = jnp.where(b == qi, on_diagonal, jnp.where(b > qi, MASKED, 0.0))
            lb_s[b % 3], hi_s[...], lo_s[...] = _logit_parts(z + bias)

        def step(i, carry):
            run, grun = carry
            both_before = jnp.dot(jnp.concatenate([ghi_s[...], glo_s[...]], axis=0), u_before,
                                  preferred_element_type=F32)
            both_upto = jnp.dot(jnp.concatenate([hi_s[...], lo_s[...]], axis=0), u_upto, preferred_element_type=F32)
            da = lax.dot_general(dost, v_ref[rows(i), :].astype(BF), NT_DIMS, preferred_element_type=F32)
            dv_s[rows(i - 1), :] += lax.dot_general(a_s[...], dost, TN_DIMS, preferred_element_type=F32)
            z_after = logits(i + 2)

            keep(i + 1, z_s[...])

            g = g_s[...]
            before = both_before[:2 * T] + both_before[2 * T:]
            dz = (g - jnp.exp(lb_s[(i + 2) % 3]) * (g + (grun + before))).astype(BF)
            dq_s[...] += jnp.dot(dz, kn_s[rows(i - 1), :], preferred_element_type=F32)
            dkn_s[rows(i - 1), :] += lax.dot_general(dz, qst, TN_DIMS, preferred_element_type=F32)
            grun = grun + before[:, T - 1:T] + g[:, T - 1:T]

            upto = both_upto[:2 * T] + both_upto[2 * T:]
            a = jnp.exp(lb_s[i % 3] + (total - run - upto))
            g = da * a
            a_s[...] = a.astype(BF)
            g_s[...] = g
            ghi = g.astype(BF)
            ghi_s[...] = ghi
            glo_s[...] = (g - ghi.astype(F32)).astype(BF)
            z_s[...] = z_after
            return run + upto[:, T - 1:T], grun

        lb_s[...] = jnp.full(lb_s.shape, MASKED, F32)
        for ref in (a_s, g_s, ghi_s, glo_s, dq_s):
            ref[...] = jnp.zeros_like(ref)
        keep(0, logits(0))
        z_s[...] = logits(1)
        lax.fori_loop(0, qi + 2, step, (jnp.zeros((2 * T, 1), F32), jnp.zeros((2 * T, 1), F32)))
        dqn = jnp.where(lane < HEAD_DIM, dq_s[:T, :], dq_s[T:, :])
        gq_v = gq_ref[...]
        t = dqn * gq_v
        dq_ref[...] = (rq * t - q * (rq * rq * rq) * _head_mean(t * q, hm)).astype(BF)
        dgq = jnp.sum(dqn * (q * rq), axis=0, keepdims=True) * SCALE
        first = jnp.logical_and(hp == 0, qi == 0)

        @pl.when(first)
        def _():
            dgq_ref[...] = dgq

        @pl.when(jnp.logical_not(first))
        def _():
            dgq_ref[...] += dgq

        @pl.when(qi == nq - 1)
        def _():
            k = k_ref[...].astype(F32)
            rk = _qk_norm(k, gk_ref[...], hm)[0]
            dkn = dkn_s[...]
            tk = dkn * gk_ref[...]
            dk_ref[...] = (rk * tk - k * (rk * rk * rk) * _head_mean(tk * k, hm)).astype(BF)
            dgk = jnp.sum(dkn * (k * rk), axis=0, keepdims=True)
            dv_ref[...] = dv_s[...].astype(BF)

            @pl.when(hp == 0)
            def _():
                dgk_ref[...] = dgk

            @pl.when(hp > 0)
            def _():
                dgk_ref[...] += dgk

            @pl.when(hp == HEAD_PAIRS - 1)
            def _():
                fold = (lax.broadcasted_iota(jnp.int32, (LANES, LANES), 0) % HEAD_DIM ==
                        lax.broadcasted_iota(jnp.int32, (LANES, LANES), 1) % HEAD_DIM).astype(F32)
                dgq_ref[...] = jnp.dot(dgq_ref[...], fold, precision=HIGHEST, preferred_element_type=F32)
                dgk_ref[...] = jnp.dot(dgk_ref[...], fold, precision=HIGHEST, preferred_element_type=F32)

    gain = pl.BlockSpec((1, LANES), lambda hp, qi: (0, 0))
    full = lambda b0: pl.BlockSpec((S, LANES), lambda hp, qi, b0=b0: (0, b0 + hp))
    tile = pl.BlockSpec((T, LANES), lambda hp, qi: (qi, hp))
    col = pl.BlockSpec((S, LANES), lambda hp, qi: (0, hp))
    dgain = pl.BlockSpec((1, LANES), lambda hp, qi: (0, 0))
    return pl.pallas_call(
        body, name=name, grid=(HEAD_PAIRS, nq),
        in_specs=[pl.BlockSpec((T, LANES), lambda hp, qi: (qi, Q_BLOCK0 + hp)), full(K_BLOCK0), full(V_BLOCK0),
                  tile, pl.BlockSpec((T, LANES), lambda hp, qi: (qi, 2 + hp)), gain, gain],
        out_specs=[tile, col, col, dgain, dgain],
        out_shape=[_sds((S, ATTN_W), BF)] * 3 + [_sds((1, LANES), F32)] * 2,
        scratch_shapes=[pltpu.VMEM((S, LANES), BF), pltpu.VMEM((S, LANES), F32), pltpu.VMEM((S, LANES), F32),
                        pltpu.VMEM((3, 2 * T, T), F32), pltpu.VMEM((2 * T, T), F32), pltpu.VMEM((2 * T, T), F32),
                        pltpu.VMEM((2 * T, LANES), F32)] + [pltpu.VMEM((2 * T, T), BF)] * 5,
        compiler_params=_params("arbitrary", "arbitrary"),
    )(proj, proj, proj, tot, dy, gq, gk)


def ple_bwd_elem(name, dh, gp, pp, after=()):
    S, D = dh.shape
    tm = min(S, 512)

    def body(dh_ref, gp_ref, pp_ref, *rest):
        dgp_ref, dpp_ref = rest[-2:]
        d = dh_ref[...]
        gate = jax.nn.sigmoid(gp_ref[...].astype(F32))
        dpp_ref[...] = (d * gate).astype(BF)
        dgp_ref[...] = (d * pp_ref[...].astype(F32) * gate * (1.0 - gate)).astype(BF)

    tile = pl.BlockSpec((tm, D), lambda i: (i, 0))
    return pl.pallas_call(
        body, name=name, grid=(S // tm,), in_specs=[tile] * 3 + [pl.BlockSpec(memory_space=pl.ANY)] * len(after),
        out_specs=[tile] * 2, out_shape=[_sds((S, D), BF)] * 2, compiler_params=_params("parallel"),
    )(dh, gp, pp, *after)


def loss_head(name, h, target):
    S, D = h.shape
    tm = min(S, 512)

    def body(h_ref, t_ref, loss_ref, dh_ref):
        i = pl.program_id(0)
        e = h_ref[...] - t_ref[...]
        dh_ref[...] = e * (1.0 / D)
        part = jnp.zeros((8, LANES), F32) + 0.5 * jnp.sum(jnp.mean(e * e, axis=-1, keepdims=True))

        @pl.when(i == 0)
        def _():
            loss_ref[...] = part

        @pl.when(i > 0)
        def _():
            loss_ref[...] += part

    tile = pl.BlockSpec((tm, D), lambda i: (i, 0))
    return pl.pallas_call(
        body, name=name, grid=(S // tm,), in_specs=[tile, tile],
        out_specs=[pl.BlockSpec((8, LANES), lambda i: (0, 0)), tile],
        out_shape=[_sds((8, LANES), F32), _sds((S, D), F32)], compiler_params=_params("arbitrary"),
    )(h, target)


def _adamw_math(w, g, m, v):
    c1 = 1.0 - ADAM_B1 ** ADAM_STEP
    c2 = 1.0 - ADAM_B2 ** ADAM_STEP
    nm = ADAM_B1 * m + (1.0 - ADAM_B1) * g
    nv = ADAM_B2 * v + (1.0 - ADAM_B2) * (g * g)
    return -ADAM_LR * ((nm / c1) / (jnp.sqrt(nv / c2) + ADAM_EPS) + ADAM_WD * w), nm, nv


def adamw(name, w, g, m, v):
    R, C = w.shape
    tr = R
    for cand in (512, 256, 128, 64, 32, 16, 8):
        if R % cand == 0:
            tr = cand
            break

    def body(w_ref, g_ref, m_ref, v_ref, d_ref, nm_ref, nv_ref):
        d_ref[...], nm_ref[...], nv_ref[...] = _adamw_math(w_ref[...], g_ref[...], m_ref[...], v_ref[...])

    tile = pl.BlockSpec((tr, C), lambda i: (i, 0))
    return pl.pallas_call(
        body, name=name, grid=(R // tr,), in_specs=[tile] * 4, out_specs=[tile] * 3,
        out_shape=[_sds((R, C), F32)] * 3, compiler_params=_params("parallel"),
    )(w, g, m, v)


def adamw_layer(name, layer, ws, gs, ms, vs, prev, after=()):
    n = len(ws)
    steps = 8

    def body(*refs):
        ins, outs = refs[:4 * n], refs[-4 * n:]
        for i in range(n):
            w_ref, g_ref, m_ref, v_ref = (ins[k * n + i] for k in range(4))
            g = g_ref[...]
            outs[i][...] = g
            outs[n + i][...], outs[2 * n + i][...], outs[3 * n + i][...] = _adamw_math(w_ref[...], g, m_ref[...],
                                                                                        v_ref[...])

    def stacked(a):
        return pl.BlockSpec((None, a.shape[1] // steps, a.shape[2]), lambda t: (layer, t, 0))

    def flat(a):
        return pl.BlockSpec((a.shape[0] // steps, a.shape[1]), lambda t: (t, 0))

    in_specs = [stacked(a) for a in ws] + [flat(a) for a in gs] + [stacked(a) for a in ms] + [stacked(a) for a in vs]
    operands = [*ws, *gs, *ms, *vs]
    aliases = {}
    if prev is not None:
        flat_prev = [a for group in prev for a in group]
        in_specs += [pl.BlockSpec(memory_space=pl.ANY)] * len(flat_prev)
        aliases = {4 * n + i: i for i in range(4 * n)}
        operands += flat_prev
    in_specs += [pl.BlockSpec(memory_space=pl.ANY)] * len(after)
    operands += list(after)
    out = pl.pallas_call(
        body, name=name, grid=(steps,), in_specs=in_specs, out_specs=[stacked(a) for a in ws] * 4,
        out_shape=[_sds(a.shape, F32) for a in ws] * 4, input_output_aliases=aliases,
        compiler_params=_params("parallel"),
    )(*operands)
    return [list(out[k * n:(k + 1) * n]) for k in range(4)]


def _relu2(u):
    r = jnp.maximum(u.astype(F32), 0.0)
    return r * r


def layer_fwd(tag, h0, p_bf, wt, after=(), mid=None):
    hn1 = rmsnorm(f"{tag}_norm1", h0, wt["norm1_g"], after)
    proj = mm_nn(f"{tag}_proj", hn1, wt["w_in"], tn=1408)
    ya = conv_fwd(f"{tag}_conv", proj, wt["conv_w"])
    yb, yb_tot = attn_fwd(f"{tag}_attn", proj, wt["gq"], wt["gk"])
    yc = sgu_fwd(f"{tag}_sgu", proj, wt["gv"], wt["sgu_w"], wt["sgu_bias"])
    y = jnp.concatenate([ya, yb, yc], axis=-1)
    h1 = mm_nn(f"{tag}_out", y, wt["w_out"], extras=(h0,), epi=lambda acc, h: (h + acc,), out_dtypes=(F32,))
    hn2 = rmsnorm(f"{tag}_norm2", h1, wt["norm2_g"], () if mid is None else mid(yb))
    uu = mm_nn(f"{tag}_ff1", hn2, wt["w_ff1"])
    h2 = mm_nn(f"{tag}_ff2", uu, wt["w_ff2"], pro=_relu2, extras=(h1,), epi=lambda acc, h: (h + acc,),
               out_dtypes=(F32,), tm=512)
    hn3 = rmsnorm(f"{tag}_norm3", h2, wt["norm3_g"])
    gp = mm_nn(f"{tag}_gate", hn3, wt["w_ple_gate"])
    h3, pp = mm_nn(f"{tag}_ple", p_bf, wt["w_ple_proj"], extras=(gp, h2),
                   epi=lambda acc, g, h: (h + jax.nn.sigmoid(g.astype(F32)) * acc, acc), out_dtypes=(F32, BF))
    saved = dict(h0=h0, h1=h1, h2=h2, hn1=hn1, hn2=hn2, hn3=hn3, proj=proj, yb_tot=yb_tot, y=y, uu=uu, gp=gp, pp=pp,
                 p_bf=p_bf)
    return h3, saved


def layer_bwd(tag, dh3, sv, wt, after=(), mid=None):
    dgp, dpp = ple_bwd_elem(f"{tag}_dple", dh3, sv["gp"], sv["pp"], after)
    g = {}
    g["w_ple_proj"] = mm_tn(f"{tag}_dwp", sv["p_bf"], dpp, col_blocks=N_CHIPS)
    g["w_ple_gate"] = mm_tn(f"{tag}_dwg", sv["hn3"], dgp)
    dh2, g["norm3_g"] = mm_nt_rmsbwd(f"{tag}_dnorm3", dgp, wt["w_ple_gate"], sv["h2"], wt["norm3_g"], dh3)

    duu = mm_nt(f"{tag}_dff2", dh2, wt["w_ff2"], extras=(sv["uu"],),
                epi=lambda acc, u: acc * (2.0 * jnp.maximum(u.astype(F32), 0.0)))
    g["w_ff2"] = mm_tn(f"{tag}_dw2", sv["uu"], dh2, pro_x=_relu2)
    g["w_ff1"] = mm_tn(f"{tag}_dw1", sv["hn2"], duu, col_blocks=N_CHIPS)
    dh1, g["norm2_g"] = mm_nt_rmsbwd(f"{tag}_dnorm2", duu, wt["w_ff1"], sv["h1"], wt["norm2_g"], dh2)

    dy = mm_nt(f"{tag}_dout", dh1, wt["w_out"], after=() if mid is None else mid(dh1))
    g["w_out"] = mm_tn(f"{tag}_dwo", sv["y"], dh1)
    dab, dac, dah, g["conv_w"] = conv_bwd(f"{tag}_dconv", dy, sv["proj"], wt["conv_w"])
    dq, dk, dv, g["gq"], g["gk"] = attn_bwd(f"{tag}_dattn", dy, sv["proj"], sv["yb_tot"], wt["gq"], wt["gk"])
    dcu, dcv, g["sgu_w"], g["sgu_bias"], g["gv"] = sgu_bwd(f"{tag}_dsgu", dy, sv["proj"], wt["gv"], wt["sgu_w"],
                                                           wt["sgu_bias"])
    dproj = jnp.concatenate([dab, dac, dah, dq, dk, dv, dcu, dcv], axis=-1)
    g["w_in"] = mm_tn(f"{tag}_dwi", sv["hn1"], dproj, tn=1408)
    dh0, g["norm1_g"] = mm_nt_rmsbwd(f"{tag}_dnorm1", dproj, wt["w_in"], sv["h0"], wt["norm1_g"], dh1)
    return dh0, g


def prep_small(norm1_g, q_norm_g, k_norm_g, sgu_norm_g, sgu_w, sgu_b, norm2_g, norm3_g, conv_w_full):
    return dict(
        norm1_g=norm1_g, norm2_g=norm2_g, norm3_g=norm3_g, conv_w=conv_w_full,
        gq=(jnp.tile(q_norm_g, 2) * SCALE).reshape(1, LANES), gk=jnp.tile(k_norm_g, 2).reshape(1, LANES),
        gv=sgu_norm_g.reshape(1, SGU_W), sgu_w=sgu_w, sgu_bias=jnp.repeat(sgu_b.T, HEAD_DIM, axis=1))


def small_grads(g):
    return dict(
        norm1_g=g["norm1_g"][0], norm2_g=g["norm2_g"][0], norm3_g=g["norm3_g"][0], conv_w=g["conv_w"],
        q_norm_g=g["gq"][0, :HEAD_DIM], k_norm_g=g["gk"][0, :HEAD_DIM], sgu_norm_g=g["gv"][0], sgu_w=g["sgu_w"],
        sgu_b=g["sgu_bias"][:, :SGU_HEADS].T)


HBM_SPEC = pl.BlockSpec(memory_space=pltpu.HBM)
BIG = ("w_in", "w_out", "w_ff1", "w_ff2", "w_ple_gate", "w_ple_proj")


def _mesh_pos():
    return lax.axis_index("x"), lax.axis_index("y"), lax.axis_index("c")


def _other_chips(x, y):
    return [(1 - x, y), (x, 1 - y), (1 - x, 1 - y)]


def _half(rows, core):
    h = rows // 2
    return pl.ds(pl.multiple_of(core * h, 16), h)


def _remote(src, dst, send_sems, recv_sems, k, to):
    return pltpu.make_async_remote_copy(src_ref=src, dst_ref=dst, send_sem=send_sems.at[k], recv_sem=recv_sems.at[k],
                                        device_id=to, device_id_type=MESH)


SEM_SPEC = pl.BlockSpec(memory_space=pltpu.SEMAPHORE)
ANY_SPEC = pl.BlockSpec(memory_space=pl.ANY)
SIDE_EFFECT = pltpu.SideEffectType.DATAFLOW_SIDE_EFFECTING


def _in_hbm(arrays):
    return [pltpu.with_memory_space_constraint(a, pltpu.HBM) for a in arrays]


def copies_start(name, srcs, lands, plan, after=()):
    ns, nl, na = len(srcs), len(lands), len(after)

    def body(*refs):
        src_refs, land_refs = refs[:ns], refs[ns:ns + nl]
        send_sem, recv_sem = refs[ns + nl + na], refs[ns + nl + na + 1]
        token = refs[-1]
        for src, dst, dev in plan(src_refs, land_refs, *_mesh_pos()):
            pltpu.make_async_remote_copy(src_ref=src, dst_ref=dst, send_sem=send_sem, recv_sem=recv_sem,
                                         device_id=dev, device_id_type=MESH).start()
        token[...] = jnp.zeros_like(token)

    out = pl.pallas_call(
        body, name=name,
        in_specs=[HBM_SPEC] * (ns + nl) + [ANY_SPEC] * na,
        out_specs=(SEM_SPEC, SEM_SPEC, *[HBM_SPEC] * (ns + nl), pl.BlockSpec(memory_space=pltpu.VMEM)),
        out_shape=(pltpu.SemaphoreType.DMA(()), pltpu.SemaphoreType.DMA(()),
                   *[pltpu.HBM(a.shape, a.dtype) for a in (*srcs, *lands)], _sds((8, LANES), F32)),
        input_output_aliases={i: 2 + i for i in range(ns + nl)},
        compiler_params=pltpu.CompilerParams(has_side_effects=SIDE_EFFECT),
    )(*_in_hbm(srcs), *_in_hbm(lands), *after)
    return out[0], out[1], list(out[2:2 + ns]), list(out[2 + ns:2 + ns + nl]), out[-1]


def copies_wait(name, started, plan, after=()):
    send_sem, recv_sem, srcs, lands, _ = started
    ns, nl, na = len(srcs), len(lands), len(after)

    def body(*refs):
        src_refs, land_refs = refs[:ns], refs[ns:ns + nl]
        send_sem, recv_sem = refs[ns + nl], refs[ns + nl + 1]
        for src, dst, dev in plan(src_refs, land_refs, *_mesh_pos()):
            cp = pltpu.make_async_remote_copy(src_ref=src, dst_ref=dst, send_sem=send_sem, recv_sem=recv_sem,
                                              device_id=dev, device_id_type=MESH)
            cp.wait_send()
            cp.wait_recv()

    out = pl.pallas_call(
        body, name=name,
        in_specs=[HBM_SPEC] * (ns + nl) + [SEM_SPEC, SEM_SPEC] + [ANY_SPEC] * na,
        out_specs=[HBM_SPEC] * (ns + nl),
        out_shape=[pltpu.HBM(a.shape, a.dtype) for a in (*srcs, *lands)],
        input_output_aliases={i: i for i in range(ns + nl)},
        compiler_params=pltpu.CompilerParams(has_side_effects=SIDE_EFFECT),
    )(*srcs, *lands, send_sem, recv_sem, *after)
    return list(out[:ns]), list(out[ns:])


def _gather_plan(srcs, lands, x, y, c):
    me = 2 * x + y
    return [(src.at[_half(src.shape[0], c)], land.at[me, _half(src.shape[0], c)], (*chip, c))
            for src, land in zip(srcs, lands) for chip in _other_chips(x, y)]


def _gather_arrivals(srcs, lands, x, y, c):
    return [(src.at[_half(src.shape[0], c)], land.at[2 * chip[0] + chip[1], _half(src.shape[0], c)], (*chip, c))
            for src, land in zip(srcs, lands) for chip in _other_chips(x, y)]


def _forward_plan(srcs, lands, x, y, c):
    me, sibling = 2 * x + y, (x, y, 1 - c)
    out = []
    for src, land in zip(srcs, lands):
        out.append((src, land.at[me], sibling))
        for chip in _other_chips(x, y):
            region = land.at[2 * chip[0] + chip[1], _half(src.shape[0], c)]
            out.append((region, region, sibling))
    return out


def _forward_arrivals(srcs, lands, x, y, c):
    me, sibling = 2 * x + y, (x, y, 1 - c)
    out = []
    for src, land in zip(srcs, lands):
        out.append((src, land.at[me], sibling))
        for chip in _other_chips(x, y):
            slot = land.at[2 * chip[0] + chip[1]]
            out.append((slot.at[_half(src.shape[0], c)], slot.at[_half(src.shape[0], 1 - c)], sibling))
    return out


def _join_plan(srcs, lands, x, y, c):
    return [(land.at[_half(land.shape[0], c)], land.at[_half(land.shape[0], c)], (x, y, 1 - c)) for land in lands]


def _join_arrivals(srcs, lands, x, y, c):
    return [(land.at[_half(land.shape[0], c)], land.at[_half(land.shape[0], 1 - c)], (x, y, 1 - c)) for land in lands]


def _pair_plan(srcs, lands, x, y, c):
    return [(src.at[:, _half(src.shape[1], 1 - c)], land, (x, y, 1 - c)) for src, land in zip(srcs, lands)]


def add_own_half(name, core, grads, got):
    n = len(grads)

    def body(core_ref, *refs):
        for i in range(n):
            refs[2 * n + i][...] = (refs[i][...].astype(F32) + refs[n + i][...].astype(F32)).astype(BF)

    def spec(g, own):
        blk = (None, g.shape[1] // 2, g.shape[2])
        return pl.BlockSpec(blk, (lambda j, core_ref: (j, core_ref[0], 0)) if own else (lambda j, core_ref: (j, 0, 0)))

    return pl.pallas_call(
        body, name=name,
        grid_spec=pltpu.PrefetchScalarGridSpec(
            num_scalar_prefetch=1, grid=(N_CHIPS,),
            in_specs=[spec(g, True) for g in grads] + [spec(g, False) for g in grads],
            out_specs=[spec(g, False) for g in grads]),
        out_shape=[_sds(r.shape, BF) for r in got], compiler_params=_params("parallel"),
    )(core, *grads, *got)


def _chips_plan(srcs, lands, x, y, c):
    return [(src.at[2 * chip[0] + chip[1]], land.at[k], (*chip, c))
            for src, land in zip(srcs, lands) for k, chip in enumerate(_other_chips(x, y))]


def sum_chips(name, place, parts, got):
    n = len(got)

    def body(place_ref, *refs):
        for i in range(n):
            acc = refs[i][...].astype(F32)
            for k in range(N_CHIPS - 1):
                acc = acc + refs[n + i][k].astype(F32)
            refs[2 * n + i][...] = acc

    steps = 2
    return pl.pallas_call(
        body, name=name,
        grid_spec=pltpu.PrefetchScalarGridSpec(
            num_scalar_prefetch=1, grid=(steps,),
            in_specs=[pl.BlockSpec((None, g.shape[1] // steps, g.shape[2]), lambda t, place_ref: (place_ref[0], t, 0))
                      for g in parts] +
                     [pl.BlockSpec((N_CHIPS - 1, g.shape[1] // steps, g.shape[2]), lambda t, place_ref: (0, t, 0))
                      for g in got],
            out_specs=[pl.BlockSpec((g.shape[1] // steps, g.shape[2]),
                                    lambda t, place_ref: (place_ref[1] * steps + t, 0)) for g in got]),
        out_shape=[_sds((2 * g.shape[1], g.shape[2]), F32) for g in got], compiler_params=_params("parallel"),
    )(place, *parts, *got)


def reduce_scatter_pair(tag, grads):
    lands = [lax.empty((N_CHIPS, g.shape[1] // 2, g.shape[2]), g.dtype) for g in grads]
    return copies_start(f"{tag}_rs_pair_start", grads, lands, _pair_plan)


def reduce_scatter_begin(tag, core, pairing, after):
    grads, got = copies_wait(f"{tag}_rs_pair_wait", pairing, _pair_plan, after)
    parts = add_own_half(f"{tag}_rs_add", core, grads, got)
    lands = [lax.empty((N_CHIPS - 1,) + p.shape[1:], p.dtype) for p in parts]
    return copies_start(f"{tag}_rs_start", parts, lands, _chips_plan)


def reduce_scatter_sum(tag, place, started, after):
    parts, got = copies_wait(f"{tag}_rs_wait", started, _chips_plan, after)
    return copies_start(f"{tag}_rs_join_start", [], sum_chips(f"{tag}_rs_sum", place, parts, got), _join_plan)


def reduce_scatter_end(tag, joining, after):
    return copies_wait(f"{tag}_rs_join_wait", joining, _join_arrivals, after)[1]


def small_allreduce(name, x):
    R = x.shape[0]
    H = R // 2

    def body(x_ref, o_ref, pair_ref, chip_ref, send_sems, recv_sems):
        xx, yy, c = _mesh_pos()
        me = 2 * xx + yy
        chips = _other_chips(xx, yy)
        sibling = (xx, yy, 1 - c)
        mine = pl.ds(pl.multiple_of(c * H, 8), H)
        theirs = pl.ds(pl.multiple_of((1 - c) * H, 8), H)
        a = _remote(x_ref.at[theirs], pair_ref.at[theirs], send_sems, recv_sems, 0, sibling)
        a.start()
        a.wait_send()
        _remote(x_ref.at[mine], pair_ref.at[mine], send_sems, recv_sems, 0, sibling).wait_recv()
        chip_ref[me] = x_ref[mine, :] + pair_ref[mine, :]
        sends = []
        for k, chip in enumerate(chips):
            cp = _remote(chip_ref.at[me], chip_ref.at[me], send_sems, recv_sems, 1 + k, (*chip, c))
            cp.start()
            sends.append(cp)
        for k, chip in enumerate(chips):
            slot = chip_ref.at[2 * chip[0] + chip[1]]
            _remote(slot, slot, send_sems, recv_sems, 1 + k, (*chip, c)).wait_recv()
        o_ref[mine, :] = (chip_ref[0] + chip_ref[1]) + (chip_ref[2] + chip_ref[3])
        b = _remote(o_ref.at[mine], o_ref.at[mine], send_sems, recv_sems, 4, sibling)
        b.start()
        b.wait_send()
        _remote(o_ref.at[theirs], o_ref.at[theirs], send_sems, recv_sems, 4, sibling).wait_recv()
        for cp in sends:
            cp.wait_send()

    return pl.pallas_call(
        body, name=name,
        in_specs=[pl.BlockSpec(memory_space=pltpu.VMEM)], out_specs=pl.BlockSpec(memory_space=pltpu.VMEM),
        out_shape=_sds((R, LANES), F32),
        scratch_shapes=[pltpu.VMEM((R, LANES), F32), pltpu.VMEM((N_CHIPS, H, LANES), F32),
                        pltpu.SemaphoreType.DMA((5,)), pltpu.SemaphoreType.DMA((5,))],
        compiler_params=pltpu.CompilerParams(vmem_limit_bytes=VMEM_LIMIT),
    )(x)


WEIGHTS = ("norm1_g", "w_in", "conv_w", "q_norm_g", "k_norm_g", "sgu_norm_g", "sgu_w", "sgu_b", "w_out", "norm2_g",
           "w_ff1", "w_ff2", "norm3_g", "w_ple_gate", "w_ple_proj")
SMALL = ("norm1_g", "norm2_g", "norm3_g", "q_norm_g", "k_norm_g", "sgu_norm_g", "sgu_w", "sgu_b", "conv_w")


def _pack_rows(arrays):
    flat = []
    for a in arrays:
        v = a.reshape(-1)
        flat.append(jnp.pad(v, (0, (-v.shape[0]) % LANES)))
    v = jnp.concatenate(flat)
    v = jnp.pad(v, (0, (-v.shape[0]) % (16 * LANES)))
    return v.reshape(-1, LANES)


def _unpack_rows(packed, shapes):
    out, pos = [], 0
    flat = packed.reshape(-1)
    for shp in shapes:
        size = math.prod(shp)
        out.append(flat[pos:pos + size].reshape(shp))
        pos += size + (-size) % LANES
    return out


def kernel(x, p, norm1_g, w_in, conv_w, q_norm_g, k_norm_g, sgu_norm_g, sgu_w, sgu_b, w_out, norm2_g, w_ff1, w_ff2, norm3_g, w_ple_gate, w_ple_proj, loss_target, m_norm1_g, m_w_in, m_conv_w, m_q_norm_g, m_k_norm_g, m_sgu_norm_g, m_sgu_w, m_sgu_b, m_w_out, m_norm2_g, m_w_ff1, m_w_ff2, m_norm3_g, m_w_ple_gate, m_w_ple_proj, v_norm1_g, v_w_in, v_conv_w, v_q_norm_g, v_k_norm_g, v_sgu_norm_g, v_sgu_w, v_sgu_b, v_w_out, v_norm2_g, v_w_ff1, v_w_ff2, v_norm3_g, v_w_ple_gate, v_w_ple_proj):
    w = dict(norm1_g=norm1_g, w_in=w_in, conv_w=conv_w, q_norm_g=q_norm_g, k_norm_g=k_norm_g, sgu_norm_g=sgu_norm_g,
             sgu_w=sgu_w, sgu_b=sgu_b, w_out=w_out, norm2_g=norm2_g, w_ff1=w_ff1, w_ff2=w_ff2, norm3_g=norm3_g,
             w_ple_gate=w_ple_gate, w_ple_proj=w_ple_proj)
    m = dict(norm1_g=m_norm1_g, w_in=m_w_in, conv_w=m_conv_w, q_norm_g=m_q_norm_g, k_norm_g=m_k_norm_g,
             sgu_norm_g=m_sgu_norm_g, sgu_w=m_sgu_w, sgu_b=m_sgu_b, w_out=m_w_out, norm2_g=m_norm2_g, w_ff1=m_w_ff1,
             w_ff2=m_w_ff2, norm3_g=m_norm3_g, w_ple_gate=m_w_ple_gate, w_ple_proj=m_w_ple_proj)
    v = dict(norm1_g=v_norm1_g, w_in=v_w_in, conv_w=v_conv_w, q_norm_g=v_q_norm_g, k_norm_g=v_k_norm_g,
             sgu_norm_g=v_sgu_norm_g, sgu_w=v_sgu_w, sgu_b=v_sgu_b, w_out=v_w_out, norm2_g=v_norm2_g, w_ff1=v_w_ff1,
             w_ff2=v_w_ff2, norm3_g=v_norm3_g, w_ple_gate=v_w_ple_gate, w_ple_proj=v_w_ple_proj)
    depth = w_in.shape[0]
    d_model = x.shape[-1]
    chip = 2 * lax.axis_index("x") + lax.axis_index("y")
    core = lax.axis_index("c")
    core_arr = core.reshape(1).astype(jnp.int32)

    cw_cols = conv_w.shape[-1]
    placed = lax.dynamic_update_slice(jnp.zeros((depth, 3, CONV_W), F32), conv_w, (0, 0, chip * cw_cols))
    placed = jnp.where(core == 0, placed, 0.0)
    conv_full = _unpack_rows(small_allreduce("conv_w_gather", _pack_rows([placed])), [(depth, 3, CONV_W)])[0]

    h = x[0]
    p_bf = p[:, 0].astype(BF)
    saved, full = [], []

    def gather_start(l, after):
        shards = [w[n][l].astype(BF) for n in BIG]
        lands = [lax.empty((N_CHIPS,) + s.shape, BF) for s in shards]
        return copies_start(f"l{l}_gather_start", shards, lands, _gather_plan, after)

    def gather_forward(l, started, after):
        shards, lands = copies_wait(f"l{l}_gather_wait", started, _gather_arrivals, after)
        forwarding = copies_start(f"l{l}_forward_start", shards, lands, _forward_plan)
        return forwarding, gather_start(l + 1, (forwarding[-1],)) if l + 1 < depth else None

    forwarding, started = gather_forward(0, gather_start(0, ()), (h,))
    for l in range(depth):
        g_in, g_out, g_ff1, g_ff2, g_gate, g_proj = copies_wait(f"l{l}_forward_wait", forwarding, _forward_arrivals,
                                                                 (h,))[1]
        nxt = {}

        def mid(arr, l=l, started=started, nxt=nxt):
            nxt["forwarding"], nxt["started"] = gather_forward(l + 1, started, (arr,))
            return [t[-1] for t in nxt.values() if t is not None]

        wt = prep_small(norm1_g[l], q_norm_g[l], k_norm_g[l], sgu_norm_g[l], sgu_w[l], sgu_b[l], norm2_g[l], norm3_g[l],
                        conv_full[l])
        wt["w_in"] = jnp.transpose(g_in, (1, 0, 2)).reshape(d_model, -1)
        wt["w_out"] = g_out.reshape(-1, d_model)
        wt["w_ff1"] = g_ff1
        wt["w_ff2"] = g_ff2.reshape(-1, d_model)
        wt["w_ple_gate"] = g_gate.reshape(-1, d_model)
        wt["w_ple_proj"] = g_proj
        last = l + 1 == depth
        h, sv = layer_fwd(f"l{l}", h, p_bf[l], wt, () if last else (started[-1],), None if last else mid)
        if not last:
            forwarding, started = nxt["forwarding"], nxt["started"]
        saved.append(sv)
        full.append(wt)

    loss_tile, dh = loss_head("loss", h, loss_target[0])
    loss = lax.psum(loss_tile[0, 0], ("x", "y", "c"))

    small = [None] * depth
    chip_arr = jnp.stack([chip, core]).astype(jnp.int32)
    big_w, big_m, big_v = ([d[n] for n in BIG] for d in (w, m, v))
    pairing, joining, tokens = None, {}, ()
    for l in reversed(range(depth)):
        box = {}

        def mid(arr, l=l, pairing=pairing, box=box):
            box["started"] = reduce_scatter_begin(f"l{l + 1}", core_arr, pairing, (arr,))
            return (box["started"][-1],)

        dh, g = layer_bwd(f"l{l}", dh, saved[l], full[l], tokens, None if pairing is None else mid)
        tokens = ()
        if pairing is not None:
            joining[l + 1] = reduce_scatter_sum(f"l{l + 1}", chip_arr, box["started"], (dh,))
            tokens = (joining[l + 1][-1],)
        small[l] = small_grads(g)
        shards_in = w_in.shape[-1]
        gl = [jnp.transpose(g["w_in"].reshape(d_model, N_CHIPS, shards_in), (1, 0, 2)),
              g["w_out"].reshape(N_CHIPS, -1, d_model), g["w_ff1"], g["w_ff2"].reshape(N_CHIPS, -1, d_model),
              g["w_ple_gate"].reshape(N_CHIPS, -1, d_model), g["w_ple_proj"]]
        pairing = reduce_scatter_pair(f"l{l}", gl)
        tokens += (pairing[-1],)

    started = reduce_scatter_begin("l0", core_arr, pairing, (dh,))
    updated = None
    for l in reversed(range(1, depth)):
        reduced = reduce_scatter_end(f"l{l}", joining[l], (started[-1],))
        updated = adamw_layer(f"l{l}_adamw", l, big_w, reduced, big_m, big_v, updated)
    grads, delta, new_m, new_v = {}, {}, {}, {}
    packed = _pack_rows([small[l][n] for l in range(depth) for n in SMALL])
    shapes = [small[l][n].shape for l in range(depth) for n in SMALL]
    pieces = _unpack_rows(small_allreduce("small_grads", packed), shapes)
    for i, n in enumerate(SMALL):
        grads[n] = jnp.stack([pieces[l * len(SMALL) + i] for l in range(depth)])
    grads["conv_w"] = lax.dynamic_slice(grads["conv_w"], (0, 0, chip * cw_cols), (depth, 3, cw_cols))
    for n in SMALL:
        shp = w[n].shape
        two_d = (-1, shp[-1]) if n != "sgu_w" else (-1, LANES)
        d, nm, nv = adamw(f"adamw_{n}", w[n].reshape(two_d), grads[n].reshape(two_d), m[n].reshape(two_d),
                          v[n].reshape(two_d))
        delta[n], new_m[n], new_v[n] = d.reshape(shp), nm.reshape(shp), nv.reshape(shp)

    done = [new_v[n] for n in SMALL] + ([] if updated is None else [updated[3][0]])
    reduced = reduce_scatter_end("l0", reduce_scatter_sum("l0", chip_arr, started, (dh, *done)), ())
    updated = adamw_layer("l0_adamw", 0, big_w, reduced, big_m, big_v, updated)
    for k, d in enumerate((grads, delta, new_m, new_v)):
        d.update(zip(BIG, updated[k]))

    return (loss, dh[None], *[grads[n] for n in WEIGHTS], *[delta[n] for n in WEIGHTS], *[new_m[n] for n in WEIGHTS],
            *[new_v[n] for n in WEIGHTS])
```

```python
import functools
import math

import jax
import jax.numpy as jnp
from jax import lax
from jax.experimental import pallas as pl
from jax.experimental.pallas import tpu as pltpu

F32 = jnp.float32
BF = jnp.bfloat16
MESH = pl.DeviceIdType.MESH
HIGHEST = lax.Precision.HIGHEST

EPS = 1e-6
HEAD_DIM = 64
CONV_W = 256
ATTN_W = 512
SGU_W = 256
D_MIX = CONV_W + ATTN_W + SGU_W
CHUNK = 128
N_CHIPS = 4
SCALE = HEAD_DIM ** -0.5
LANES = 128
VMEM_LIMIT = 56 * 1024 * 1024

ADAM_LR = 0.001
ADAM_B1 = 0.9
ADAM_B2 = 0.999
ADAM_EPS = 1e-08
ADAM_WD = 0.01
ADAM_STEP = 10

NT_DIMS = (((1,), (1,)), ((), ()))
TN_DIMS = (((0,), (0,)), ((), ()))


def _params(*sem):
    return pltpu.CompilerParams(dimension_semantics=sem if sem else None, vmem_limit_bytes=VMEM_LIMIT)


def _sds(shape, dtype):
    return jax.ShapeDtypeStruct(shape, dtype)


def _erf(x):
    return lax.erf(x)


def _gelu(x):
    return 0.5 * x * (1.0 + _erf(x * (2.0 ** -0.5)))


def _gelu_grad(x):
    return 0.5 * (1.0 + _erf(x * (2.0 ** -0.5))) + x * jnp.exp(-0.5 * x * x) * (1.0 / math.sqrt(2.0 * math.pi))


def _log_sigmoid(z):
    return jnp.minimum(z, 0.0) - jnp.log(1.0 + jnp.exp(-jnp.abs(z)))


def _head_mean_matrix(width):
    r = lax.broadcasted_iota(jnp.int32, (width, width), 0) // HEAD_DIM
    c = lax.broadcasted_iota(jnp.int32, (width, width), 1) // HEAD_DIM
    return (r == c).astype(BF)


def _head_mean(x, m):
    hi = x.astype(BF)
    lo = (x - hi.astype(F32)).astype(BF)
    return _dot2_stacked(hi, lo, m) * (1.0 / HEAD_DIM)


def rmsnorm(name, h, g, after=()):
    S, D = h.shape
    tm = min(S, 512)

    def body(h_ref, g_ref, *rest):
        x = h_ref[...]
        r = lax.rsqrt(jnp.mean(x * x, axis=-1, keepdims=True) + EPS)
        rest[-1][...] = ((x * r) * g_ref[...]).astype(BF)

    return pl.pallas_call(
        body, name=name, grid=(S // tm,),
        in_specs=[pl.BlockSpec((tm, D), lambda i: (i, 0)), pl.BlockSpec((1, D), lambda i: (0, 0))] +
                 [pl.BlockSpec(memory_space=pl.ANY)] * len(after),
        out_specs=pl.BlockSpec((tm, D), lambda i: (i, 0)),
        out_shape=_sds((S, D), BF), compiler_params=_params("parallel"),
    )(h, g.reshape(1, D), *after)


def mm_nn(name, x, w, *, extras=(), pro=None, epi=None, out_dtypes=None, tm=None, tn=512):
    S, K = x.shape
    if w.ndim == 3:
        J, _, tn = w.shape
        N = J * tn
        w_spec = pl.BlockSpec((None, K, tn), lambda n, m: (n, 0, 0))
    else:
        N = w.shape[1]
        tn = min(tn, N)
        w_spec = pl.BlockSpec((K, tn), lambda n, m: (0, n))
    tm = S if tm is None else min(tm, S)
    out_dtypes = (BF,) if out_dtypes is None else out_dtypes
    n_ex, n_out = len(extras), len(out_dtypes)

    def body(x_ref, w_ref, *rest):
        xv = x_ref[...]
        if pro is not None:
            xv = pro(xv)
        acc = jnp.dot(xv.astype(BF), w_ref[...], preferred_element_type=F32)
        outs = (acc,) if epi is None else epi(acc, *[e[...] for e in rest[:n_ex]])
        for o_ref, o in zip(rest[n_ex:], outs):
            o_ref[...] = o.astype(o_ref.dtype)

    tile = pl.BlockSpec((tm, tn), lambda n, m: (m, n))
    out = pl.pallas_call(
        body, name=name, grid=(N // tn, S // tm),
        in_specs=[pl.BlockSpec((tm, K), lambda n, m: (m, 0)), w_spec] + [tile] * n_ex,
        out_specs=[tile] * n_out,
        out_shape=[_sds((S, N), d) for d in out_dtypes],
        compiler_params=_params("parallel", "parallel"),
    )(x, w, *extras)
    return out[0] if n_out == 1 else out


def mm_nt(name, dy, w, *, extras=(), epi=None, tk=512, after=()):
    S, N = dy.shape
    K = w.shape[0]
    tk = min(tk, K)
    n_ex = len(extras)

    def body(dy_ref, w_ref, *rest):
        dyb = rest[-1]

        @pl.when(pl.program_id(0) == 0)
        def _():
            dyb[...] = dy_ref[...].astype(BF)

        acc = lax.dot_general(dyb[...], w_ref[...], NT_DIMS, preferred_element_type=F32)
        if epi is not None:
            acc = epi(acc, *[e[...] for e in rest[:n_ex]])
        rest[-2][...] = acc.astype(BF)

    col = pl.BlockSpec((S, tk), lambda k: (0, k))
    return pl.pallas_call(
        body, name=name, grid=(K // tk,),
        in_specs=[pl.BlockSpec((S, N), lambda k: (0, 0)), pl.BlockSpec((tk, N), lambda k: (k, 0))] + [col] * n_ex +
                 [pl.BlockSpec(memory_space=pl.ANY)] * len(after),
        out_specs=col, out_shape=_sds((S, K), BF), scratch_shapes=[pltpu.VMEM((S, N), BF)],
        compiler_params=_params("arbitrary"),
    )(dy, w, *extras, *after)


def mm_nt_rmsbwd(name, dy, w, h, g, dres, *, tm=256):
    S, N = dy.shape
    D = h.shape[1]
    tm = min(tm, S)
    blocked = w.ndim == 3
    nj = w.shape[2] if blocked else N

    def body(dy_ref, w_ref, h_ref, g_ref, dres_ref, dh_ref, dg_ref):
        i = pl.program_id(0)
        if blocked:
            dyn = None
            for j in range(w.shape[0]):
                part = lax.dot_general(dy_ref[:, j * nj:(j + 1) * nj].astype(BF), w_ref[j], NT_DIMS,
                                       preferred_element_type=F32)
                dyn = part if dyn is None else dyn + part
        else:
            dyn = lax.dot_general(dy_ref[...].astype(BF), w_ref[...], NT_DIMS, preferred_element_type=F32)
        x = h_ref[...]
        r = lax.rsqrt(jnp.mean(x * x, axis=-1, keepdims=True) + EPS)
        t = dyn * g_ref[...]
        dh_ref[...] = dres_ref[...] + r * t - x * (r * r * r) * jnp.mean(t * x, axis=-1, keepdims=True)
        part = jnp.sum(dyn * (x * r), axis=0, keepdims=True)

        @pl.when(i == 0)
        def _():
            dg_ref[...] = part

        @pl.when(i > 0)
        def _():
            dg_ref[...] += part

    w_spec = pl.BlockSpec(w.shape, (lambda i: (0, 0, 0)) if blocked else (lambda i: (0, 0)))
    row = pl.BlockSpec((tm, D), lambda i: (i, 0))
    vec = pl.BlockSpec((1, D), lambda i: (0, 0))
    return pl.pallas_call(
        body, name=name, grid=(S // tm,),
        in_specs=[pl.BlockSpec((tm, N), lambda i: (i, 0)), w_spec, row, vec, row],
        out_specs=[row, vec], out_shape=[_sds((S, D), F32), _sds((1, D), F32)],
        compiler_params=_params("arbitrary"),
    )(dy, w, h, g.reshape(1, D), dres)


def mm_tn(name, x, dy, *, pro_x=None, col_blocks=None, tk=1024, tn=1024):
    S, K = x.shape
    N = dy.shape[1]
    tk = min(tk, K)
    if col_blocks is not None:
        tn = N // col_blocks
        out_shape = _sds((col_blocks, K, tn), BF)
        out_spec = pl.BlockSpec((None, tk, tn), lambda k, n: (n, k, 0))
    else:
        tn = min(tn, N)
        out_shape = _sds((K, N), BF)
        out_spec = pl.BlockSpec((tk, tn), lambda k, n: (k, n))

    def body(x_ref, dy_ref, o_ref):
        xv = x_ref[...]
        if pro_x is not None:
            xv = pro_x(xv)
        o_ref[...] = lax.dot_general(xv.astype(BF), dy_ref[...].astype(BF), TN_DIMS,
                                     preferred_element_type=F32).astype(BF)

    return pl.pallas_call(
        body, name=name, grid=(K // tk, N // tn),
        in_specs=[pl.BlockSpec((S, tk), lambda k, n: (0, k)), pl.BlockSpec((S, tn), lambda k, n: (0, n))],
        out_specs=out_spec, out_shape=out_shape, compiler_params=_params("parallel", "parallel"),
    )(x, dy)


def _conv_parts(ac_ref, ah_ref, cw):
    a_c = ac_ref[...].astype(F32)
    a_h = ah_ref[...].astype(F32)
    x = a_c * a_h
    row = lax.broadcasted_iota(jnp.int32, x.shape, 0)
    x1 = jnp.where(row >= 1, pltpu.roll(x, 1, 0), 0.0)
    x2 = jnp.where(row >= 2, pltpu.roll(x, 2, 0), 0.0)
    cv = cw[0:1] * x2 + cw[1:2] * x1 + cw[2:3] * x
    return a_c, a_h, x, x1, x2, cv, row


def conv_fwd(name, proj, cw):
    S = proj.shape[0]

    def body(ab_ref, ac_ref, ah_ref, cw_ref, o_ref):
        cv = _conv_parts(ac_ref, ah_ref, cw_ref[...])[5]
        o_ref[...] = (ab_ref[...].astype(F32) * cv).astype(BF)

    col = lambda j: pl.BlockSpec((S, CONV_W), lambda i, j=j: (0, j))
    return pl.pallas_call(
        body, name=name, grid=(1,),
        in_specs=[col(0), col(1), col(2), pl.BlockSpec((3, CONV_W), lambda i: (0, 0))],
        out_specs=pl.BlockSpec((S, CONV_W), lambda i: (0, 0)),
        out_shape=_sds((S, D_MIX), BF), compiler_params=_params("arbitrary"),
    )(proj, proj, proj, cw)


def conv_bwd(name, dy, proj, cw):
    S = proj.shape[0]

    def body(dy_ref, ab_ref, ac_ref, ah_ref, cw_ref, dab_ref, dac_ref, dah_ref, dcw_ref):
        w = cw_ref[...]
        a_c, a_h, x, x1, x2, cv, row = _conv_parts(ac_ref, ah_ref, w)
        d = dy_ref[...].astype(F32)
        dab_ref[...] = (d * cv).astype(BF)
        dcv = d * ab_ref[...].astype(F32)
        d1 = jnp.where(row < S - 1, pltpu.roll(dcv, S - 1, 0), 0.0)
        d2 = jnp.where(row < S - 2, pltpu.roll(dcv, S - 2, 0), 0.0)
        dx = w[2:3] * dcv + w[1:2] * d1 + w[0:1] * d2
        dac_ref[...] = (dx * a_h).astype(BF)
        dah_ref[...] = (dx * a_c).astype(BF)
        dcw_ref[0:1, :] = jnp.sum(dcv * x2, axis=0, keepdims=True)
        dcw_ref[1:2, :] = jnp.sum(dcv * x1, axis=0, keepdims=True)
        dcw_ref[2:3, :] = jnp.sum(dcv * x, axis=0, keepdims=True)

    col = lambda j: pl.BlockSpec((S, CONV_W), lambda i, j=j: (0, j))
    one = pl.BlockSpec((S, CONV_W), lambda i: (0, 0))
    small = pl.BlockSpec((3, CONV_W), lambda i: (0, 0))
    return pl.pallas_call(
        body, name=name, grid=(1,),
        in_specs=[col(0), col(0), col(1), col(2), small],
        out_specs=[one, one, one, small],
        out_shape=[_sds((S, CONV_W), BF)] * 3 + [_sds((3, CONV_W), F32)],
        compiler_params=_params("arbitrary"),
    )(dy, proj, proj, proj, cw)


SGU_HEADS = SGU_W // HEAD_DIM
CU_BLOCK = 2304 // SGU_W
CV_BLOCK = 2560 // SGU_W


def _sgu_common(cu_ref, cv_ref, gv_ref, tm):
    c_u = cu_ref[...].astype(F32)
    c_v = cv_ref[...].astype(F32)
    hm = _head_mean_matrix(SGU_W)
    u = _gelu(c_u)
    vg = _gelu(c_v)
    r = lax.rsqrt(_head_mean(vg * vg, hm) + EPS)
    vv = (vg * r) * gv_ref[...]
    head = lax.broadcasted_iota(jnp.int32, (CHUNK, SGU_W), 1) // HEAD_DIM
    tri = (lax.broadcasted_iota(jnp.int32, (CHUNK, CHUNK), 0) >=
           lax.broadcasted_iota(jnp.int32, (CHUNK, CHUNK), 1))
    return c_u, c_v, hm, u, vg, r, vv, head, tri


def _sgu_mix(w_ref, tri, head, vvc, bias):
    sv = bias
    for g in range(SGU_HEADS):
        wg = jnp.where(tri, w_ref[g], 0.0).astype(BF)
        sv = sv + jnp.where(head == g, jnp.dot(wg, vvc, preferred_element_type=F32), 0.0)
    return sv


def sgu_fwd(name, proj, gv, w, bias, y):
    S = proj.shape[0]
    tm = min(S, 512)

    def body(cu_ref, cv_ref, gv_ref, w_ref, b_ref, y_ref, o_ref):
        _, _, _, u, _, _, vv, head, tri = _sgu_common(cu_ref, cv_ref, gv_ref, tm)
        vvb = vv.astype(BF)
        for ch in range(tm // CHUNK):
            rows = slice(ch * CHUNK, (ch + 1) * CHUNK)
            sv = _sgu_mix(w_ref, tri, head, vvb[rows], b_ref[...])
            o_ref[rows, :] = (u[rows] * sv).astype(BF)

    const = lambda shape: pl.BlockSpec(shape, lambda i: (0,) * len(shape))
    return pl.pallas_call(
        body, name=name, grid=(S // tm,),
        in_specs=[pl.BlockSpec((tm, SGU_W), lambda i: (i, CU_BLOCK)), pl.BlockSpec((tm, SGU_W), lambda i: (i, CV_BLOCK)),
                  const((1, SGU_W)), const((SGU_HEADS, CHUNK, CHUNK)), const((CHUNK, SGU_W)),
                  pl.BlockSpec(memory_space=pl.ANY)],
        out_specs=pl.BlockSpec((tm, SGU_W), lambda i: (i, (CONV_W + ATTN_W) // SGU_W)),
        out_shape=_sds(y.shape, BF), input_output_aliases={5: 0}, compiler_params=_params("parallel"),
    )(proj, proj, gv, w, bias, y)


def sgu_bwd(name, dy, proj, gv, w, bias):
    S = proj.shape[0]
    tm = min(S, 512)

    def body(dy_ref, cu_ref, cv_ref, gv_ref, w_ref, b_ref, dcu_ref, dcv_ref, dw_ref, db_ref, dgv_ref, dvv_s):
        i = pl.program_id(0)
        c_u, c_v, hm, u, vg, r, vv, head, tri = _sgu_common(cu_ref, cv_ref, gv_ref, tm)
        vvb = vv.astype(BF)
        d = dy_ref[...].astype(F32)
        ind = (lax.broadcasted_iota(jnp.int32, (SGU_W, LANES), 0) // HEAD_DIM ==
               lax.broadcasted_iota(jnp.int32, (SGU_W, LANES), 1)).astype(BF)
        dw_acc = [jnp.zeros((CHUNK, CHUNK), F32) for _ in range(SGU_HEADS)]
        db_acc = jnp.zeros((CHUNK, LANES), F32)
        for ch in range(tm // CHUNK):
            rows = slice(ch * CHUNK, (ch + 1) * CHUNK)
            sv = _sgu_mix(w_ref, tri, head, vvb[rows], b_ref[...])
            dcu_ref[rows, :] = (d[rows] * sv * _gelu_grad(c_u[rows])).astype(BF)
            dsv = d[rows] * u[rows]
            dsv_hi = dsv.astype(BF)
            db_acc = db_acc + _dot2_stacked(dsv_hi, (dsv - dsv_hi.astype(F32)).astype(BF), ind)
            dvv = jnp.zeros((CHUNK, SGU_W), F32)
            for g in range(SGU_HEADS):
                dsv_g = jnp.where(head == g, dsv, 0.0).astype(BF)
                wg = jnp.where(tri, w_ref[g], 0.0).astype(BF)
                dvv = dvv + lax.dot_general(wg, dsv_g, TN_DIMS, preferred_element_type=F32)
                dw_acc[g] = dw_acc[g] + lax.dot_general(dsv_g, vvb[rows], NT_DIMS, preferred_element_type=F32)
            dvv_s[rows, :] = dvv
        dvv = dvv_s[...]
        gvv = gv_ref[...]
        t = dvv * gvv
        dvg = r * t - vg * (r * r * r) * _head_mean(t * vg, hm)
        dcv_ref[...] = (dvg * _gelu_grad(c_v)).astype(BF)
        dgv = jnp.sum(dvv * (vg * r), axis=0, keepdims=True)

        @pl.when(i == 0)
        def _():
            for g in range(SGU_HEADS):
                dw_ref[g] = jnp.where(tri, dw_acc[g], 0.0)
            db_ref[...] = db_acc
            dgv_ref[...] = dgv

        @pl.when(i > 0)
        def _():
            for g in range(SGU_HEADS):
                dw_ref[g] += jnp.where(tri, dw_acc[g], 0.0)
            db_ref[...] += db_acc
            dgv_ref[...] += dgv

    const = lambda shape: pl.BlockSpec(shape, lambda i: (0,) * len(shape))
    tile = pl.BlockSpec((tm, SGU_W), lambda i: (i, 0))
    return pl.pallas_call(
        body, name=name, grid=(S // tm,),
        in_specs=[pl.BlockSpec((tm, SGU_W), lambda i: (i, 3)),
                  pl.BlockSpec((tm, SGU_W), lambda i: (i, CU_BLOCK)), pl.BlockSpec((tm, SGU_W), lambda i: (i, CV_BLOCK)),
                  const((1, SGU_W)), const((SGU_HEADS, CHUNK, CHUNK)), const((CHUNK, SGU_W))],
        out_specs=[tile, tile, const((SGU_HEADS, CHUNK, CHUNK)), const((CHUNK, LANES)), const((1, SGU_W))],
        out_shape=[_sds((S, SGU_W), BF), _sds((S, SGU_W), BF), _sds((SGU_HEADS, CHUNK, CHUNK), F32),
                   _sds((CHUNK, LANES), F32), _sds((1, SGU_W), F32)],
        scratch_shapes=[pltpu.VMEM((tm, SGU_W), F32)],
        compiler_params=_params("arbitrary"),
    )(dy, proj, proj, gv, w, bias)


HEAD_PAIRS = ATTN_W // LANES
Q_BLOCK0 = 768 // LANES
K_BLOCK0 = 1280 // LANES
V_BLOCK0 = 1792 // LANES


def _attn_tile(S):
    return min(S, 256)


def _qk_norm(x, g, hm):
    r = lax.rsqrt(_head_mean(x * x, hm) + EPS)
    return r, (x * r) * g


MASKED = -1e30


def _logit_parts(z):
    lb = _log_sigmoid(z)
    lr = lb - z
    hi = lr.astype(BF)
    return lb, hi, (lr - hi.astype(F32)).astype(BF)


def _stack_heads(x, lane):
    return jnp.concatenate([jnp.where(lane < HEAD_DIM, x, 0.0), jnp.where(lane >= HEAD_DIM, x, 0.0)],
                           axis=0).astype(BF)


def _dot2_stacked(hi, lo, u):
    rows = hi.shape[0]
    both = jnp.dot(jnp.concatenate([hi, lo], axis=0), u, preferred_element_type=F32)
    return both[:rows] + both[rows:]


def attn_fwd(name, proj, gq, gk, y):
    S = proj.shape[0]
    T = _attn_tile(S)
    nq = S // T

    def body(q_ref, k_ref, v_ref, gq_ref, gk_ref, y_ref, o_ref, tot_ref, kn_s, lb_s, hi_s, lo_s, z_s, a_s, o_s):
        qi = pl.program_id(1)
        hm = _head_mean_matrix(LANES)

        @pl.when(qi == 0)
        def _():
            kn_s[...] = _qk_norm(k_ref[...].astype(F32), gk_ref[...], hm)[1].astype(BF)

        qn = _qk_norm(q_ref[...].astype(F32), gq_ref[...], hm)[1]
        lane = lax.broadcasted_iota(jnp.int32, (T, LANES), 1)
        qst = _stack_heads(qn, lane)
        rowi = lax.broadcasted_iota(jnp.int32, (T, T), 0)
        coli = lax.broadcasted_iota(jnp.int32, (T, T), 1)
        u_excl = (rowi > coli).astype(BF)
        diagonal = jnp.where(coli < rowi, 0.0, MASKED)

        def logits(j):
            return lax.dot_general(qst, kn_s[pl.ds(pl.multiple_of(j * T, T), T), :], NT_DIMS,
                                   preferred_element_type=F32)

        def values(j):
            return v_ref[pl.ds(pl.multiple_of(j * T, T), T), :].astype(BF)

        def keep(slot, z):
            lb_s[slot], hi_s[...], lo_s[...] = _logit_parts(z)

        def step(it, carry):
            run = carry
            j = qi - it
            hi, lo = hi_s[...], lo_s[...]
            both = jnp.dot(jnp.concatenate([hi, lo], axis=0), u_excl, preferred_element_type=F32)
            o_s[...] += jnp.dot(a_s[...], values(jnp.minimum(j + 1, qi)), preferred_element_type=F32)
            z_after = logits(jnp.maximum(j - 2, 0))
            first = hi[:, 0:1].astype(F32) + lo[:, 0:1].astype(F32)
            keep((it + 1) % 2, z_s[...])
            later = both[:2 * T] + both[2 * T:]
            a_s[...] = jnp.exp(lb_s[it % 2] + later + run).astype(BF)
            z_s[...] = z_after
            return run + later[:, 0:1] + first

        keep(0, logits(qi) + jnp.concatenate([diagonal, diagonal], axis=0))
        z_s[...] = logits(jnp.maximum(qi - 1, 0))
        a_s[...] = jnp.zeros_like(a_s)
        o_s[...] = jnp.zeros_like(o_s)
        run = lax.fori_loop(0, qi + 1, step, jnp.zeros((2 * T, 1), F32))
        o = o_s[...] + jnp.dot(a_s[...], values(0), preferred_element_type=F32)
        o_ref[...] = jnp.where(lane < HEAD_DIM, o[:T], o[T:]).astype(BF)
        tot_ref[...] = jnp.where(lane < HEAD_DIM, run[:T], run[T:])

    gain = pl.BlockSpec((1, LANES), lambda hp, qi: (0, 0))
    full = lambda b0: pl.BlockSpec((S, LANES), lambda hp, qi, b0=b0: (0, b0 + hp))
    tile = pl.BlockSpec((T, LANES), lambda hp, qi: (qi, hp))
    return pl.pallas_call(
        body, name=name, grid=(HEAD_PAIRS, nq),
        in_specs=[pl.BlockSpec((T, LANES), lambda hp, qi: (qi, Q_BLOCK0 + hp)), full(K_BLOCK0), full(V_BLOCK0), gain, gain,
                  pl.BlockSpec(memory_space=pl.ANY)],
        out_specs=[pl.BlockSpec((T, LANES), lambda hp, qi: (qi, CONV_W // LANES + hp)), tile],
        out_shape=[_sds(y.shape, BF), _sds((S, ATTN_W), F32)], input_output_aliases={5: 0},
        scratch_shapes=[pltpu.VMEM((S, LANES), BF), pltpu.VMEM((2, 2 * T, T), F32), pltpu.VMEM((2 * T, T), BF),
                        pltpu.VMEM((2 * T, T), BF), pltpu.VMEM((2 * T, T), F32), pltpu.VMEM((2 * T, T), BF),
                        pltpu.VMEM((2 * T, LANES), F32)],
        compiler_params=_params("arbitrary", "arbitrary"),
    )(proj, proj, proj, gq, gk, y)


def attn_bwd(name, dy, proj, tot, gq, gk):
    S = proj.shape[0]
    T = _attn_tile(S)
    nq = S // T

    def body(q_ref, k_ref, v_ref, tot_ref, do_ref, gq_ref, gk_ref,
             dq_ref, dk_ref, dv_ref, dgq_ref, dgk_ref, kn_s, dkn_s, dv_s,
             lb_s, z_s, g_s, dq_s, hi_s, lo_s, a_s, ghi_s, glo_s):
        hp = pl.program_id(0)
        qi = pl.program_id(1)
        hm = _head_mean_matrix(LANES)

        @pl.when(qi == 0)
        def _():
            kn_s[...] = _qk_norm(k_ref[...].astype(F32), gk_ref[...], hm)[1].astype(BF)
            dkn_s[...] = jnp.zeros_like(dkn_s)
            dv_s[...] = jnp.zeros_like(dv_s)

        q = q_ref[...].astype(F32)
        rq, qn = _qk_norm(q, gq_ref[...], hm)
        lane = lax.broadcasted_iota(jnp.int32, (T, LANES), 1)
        qst = _stack_heads(qn, lane)
        dost = _stack_heads(do_ref[...].astype(F32), lane)
        total = jnp.concatenate([tot_ref[:, 0:1], tot_ref[:, HEAD_DIM:HEAD_DIM + 1]], axis=0)
        rowi = lax.broadcasted_iota(jnp.int32, (T, T), 0)
        coli = lax.broadcasted_iota(jnp.int32, (T, T), 1)
        u_upto = (rowi <= coli).astype(BF)
        u_before = (rowi < coli).astype(BF)
        diagonal = jnp.where(coli < rowi, 0.0, MASKED)

        on_diagonal = jnp.concatenate([diagonal, diagonal], axis=0)

        def rows(b):
            return pl.ds(pl.multiple_of(jnp.clip(b, 0, qi) * T, T), T)

        def logits(b):
            return lax.dot_general(qst, kn_s[rows(b), :], NT_DIMS, preferred_element_type=F32)

        def keep(b, z):
            bias = jnp.where(b == qi, on_diagonal, jnp.where(b > qi, MASKED, 0.0))
            lb_s[b % 3], hi_s[...], lo_s[...] = _logit_parts(z + bias)

        def step(i, carry):
            run, grun = carry
            both_before = jnp.dot(jnp.concatenate([ghi_s[...], glo_s[...]], axis=0), u_before,
                                  preferred_element_type=F32)
            both_upto = jnp.dot(jnp.concatenate([hi_s[...], lo_s[...]], axis=0), u_upto, preferred_element_type=F32)
            da = lax.dot_general(dost, v_ref[rows(i), :].astype(BF), NT_DIMS, preferred_element_type=F32)
            dv_s[rows(i - 1), :] += lax.dot_general(a_s[...], dost, TN_DIMS, preferred_element_type=F32)
            z_after = logits(i + 2)

            keep(i + 1, z_s[...])

            g = g_s[...]
            before = both_before[:2 * T] + both_before[2 * T:]
            dz = (g - jnp.exp(lb_s[(i + 2) % 3]) * (g + (grun + before))).astype(BF)
            dq_s[...] += jnp.dot(dz, kn_s[rows(i - 1), :], preferred_element_type=F32)
            dkn_s[rows(i - 1), :] += lax.dot_general(dz, qst, TN_DIMS, preferred_element_type=F32)
            grun = grun + before[:, T - 1:T] + g[:, T - 1:T]

            upto = both_upto[:2 * T] + both_upto[2 * T:]
            a = jnp.exp(lb_s[i % 3] + (total - run - upto))
            g = da * a
            a_s[...] = a.astype(BF)
            g_s[...] = g
            ghi = g.astype(BF)
            ghi_s[...] = ghi
            glo_s[...] = (g - ghi.astype(F32)).astype(BF)
            z_s[...] = z_after
            return run + upto[:, T - 1:T], grun

        lb_s[...] = jnp.full(lb_s.shape, MASKED, F32)
        for ref in (a_s, g_s, ghi_s, glo_s, dq_s):
            ref[...] = jnp.zeros_like(ref)
        keep(0, logits(0))
        z_s[...] = logits(1)
        lax.fori_loop(0, qi + 2, step, (jnp.zeros((2 * T, 1), F32), jnp.zeros((2 * T, 1), F32)))
        dqn = jnp.where(lane < HEAD_DIM, dq_s[:T, :], dq_s[T:, :])
        gq_v = gq_ref[...]
        t = dqn * gq_v
        dq_ref[...] = (rq * t - q * (rq * rq * rq) * _head_mean(t * q, hm)).astype(BF)
        dgq = jnp.sum(dqn * (q * rq), axis=0, keepdims=True) * SCALE
        first = jnp.logical_and(hp == 0, qi == 0)

        @pl.when(first)
        def _():
            dgq_ref[...] = dgq

        @pl.when(jnp.logical_not(first))
        def _():
            dgq_ref[...] += dgq

        @pl.when(qi == nq - 1)
        def _():
            k = k_ref[...].astype(F32)
            rk = _qk_norm(k, gk_ref[...], hm)[0]
            dkn = dkn_s[...]
            tk = dkn * gk_ref[...]
            dk_ref[...] = (rk * tk - k * (rk * rk * rk) * _head_mean(tk * k, hm)).astype(BF)
            dgk = jnp.sum(dkn * (k * rk), axis=0, keepdims=True)
            dv_ref[...] = dv_s[...].astype(BF)

            @pl.when(hp == 0)
            def _():
                dgk_ref[...] = dgk

            @pl.when(hp > 0)
            def _():
                dgk_ref[...] += dgk

            @pl.when(hp == HEAD_PAIRS - 1)
            def _():
                fold = (lax.broadcasted_iota(jnp.int32, (LANES, LANES), 0) % HEAD_DIM ==
                        lax.broadcasted_iota(jnp.int32, (LANES, LANES), 1) % HEAD_DIM).astype(F32)
                dgq_ref[...] = jnp.dot(dgq_ref[...], fold, precision=HIGHEST, preferred_element_type=F32)
                dgk_ref[...] = jnp.dot(dgk_ref[...], fold, precision=HIGHEST, preferred_element_type=F32)

    gain = pl.BlockSpec((1, LANES), lambda hp, qi: (0, 0))
    full = lambda b0: pl.BlockSpec((S, LANES), lambda hp, qi, b0=b0: (0, b0 + hp))
    tile = pl.BlockSpec((T, LANES), lambda hp, qi: (qi, hp))
    col = pl.BlockSpec((S, LANES), lambda hp, qi: (0, hp))
    dgain = pl.BlockSpec((1, LANES), lambda hp, qi: (0, 0))
    return pl.pallas_call(
        body, name=name, grid=(HEAD_PAIRS, nq),
        in_specs=[pl.BlockSpec((T, LANES), lambda hp, qi: (qi, Q_BLOCK0 + hp)), full(K_BLOCK0), full(V_BLOCK0),
                  tile, pl.BlockSpec((T, LANES), lambda hp, qi: (qi, 2 + hp)), gain, gain],
        out_specs=[tile, col, col, dgain, dgain],
        out_shape=[_sds((S, ATTN_W), BF)] * 3 + [_sds((1, LANES), F32)] * 2,
        scratch_shapes=[pltpu.VMEM((S, LANES), BF), pltpu.VMEM((S, LANES), F32), pltpu.VMEM((S, LANES), F32),
                        pltpu.VMEM((3, 2 * T, T), F32), pltpu.VMEM((2 * T, T), F32), pltpu.VMEM((2 * T, T), F32),
                        pltpu.VMEM((2 * T, LANES), F32)] + [pltpu.VMEM((2 * T, T), BF)] * 5,
        compiler_params=_params("arbitrary", "arbitrary"),
    )(proj, proj, proj, tot, dy, gq, gk)


def ple_bwd_elem(name, dh, gp, pp, after=()):
    S, D = dh.shape
    tm = min(S, 512)

    def body(dh_ref, gp_ref, pp_ref, *rest):
        dgp_ref, dpp_ref = rest[-2:]
        d = dh_ref[...]
        gate = jax.nn.sigmoid(gp_ref[...].astype(F32))
        dpp_ref[...] = (d * gate).astype(BF)
        dgp_ref[...] = (d * pp_ref[...].astype(F32) * gate * (1.0 - gate)).astype(BF)

    tile = pl.BlockSpec((tm, D), lambda i: (i, 0))
    return pl.pallas_call(
        body, name=name, grid=(S // tm,), in_specs=[tile] * 3 + [pl.BlockSpec(memory_space=pl.ANY)] * len(after),
        out_specs=[tile] * 2, out_shape=[_sds((S, D), BF)] * 2, compiler_params=_params("parallel"),
    )(dh, gp, pp, *after)


def loss_head(name, h, target):
    S, D = h.shape
    tm = min(S, 512)

    def body(h_ref, t_ref, loss_ref, dh_ref):
        i = pl.program_id(0)
        e = h_ref[...] - t_ref[...]
        dh_ref[...] = e * (1.0 / D)
        part = jnp.zeros((8, LANES), F32) + 0.5 * jnp.sum(jnp.mean(e * e, axis=-1, keepdims=True))

        @pl.when(i == 0)
        def _():
            loss_ref[...] = part

        @pl.when(i > 0)
        def _():
            loss_ref[...] += part

    tile = pl.BlockSpec((tm, D), lambda i: (i, 0))
    return pl.pallas_call(
        body, name=name, grid=(S // tm,), in_specs=[tile, tile],
        out_specs=[pl.BlockSpec((8, LANES), lambda i: (0, 0)), tile],
        out_shape=[_sds((8, LANES), F32), _sds((S, D), F32)], compiler_params=_params("arbitrary"),
    )(h, target)


def _adamw_math(w, g, m, v):
    c1 = 1.0 - ADAM_B1 ** ADAM_STEP
    c2 = 1.0 - ADAM_B2 ** ADAM_STEP
    nm = ADAM_B1 * m + (1.0 - ADAM_B1) * g
    nv = ADAM_B2 * v + (1.0 - ADAM_B2) * (g * g)
    return -ADAM_LR * ((nm / c1) / (jnp.sqrt(nv / c2) + ADAM_EPS) + ADAM_WD * w), nm, nv


def adamw(name, w, g, m, v):
    R, C = w.shape
    tr = R
    for cand in (512, 256, 128, 64, 32, 16, 8):
        if R % cand == 0:
            tr = cand
            break

    def body(w_ref, g_ref, m_ref, v_ref, d_ref, nm_ref, nv_ref):
        d_ref[...], nm_ref[...], nv_ref[...] = _adamw_math(w_ref[...], g_ref[...], m_ref[...], v_ref[...])

    tile = pl.BlockSpec((tr, C), lambda i: (i, 0))
    return pl.pallas_call(
        body, name=name, grid=(R // tr,), in_specs=[tile] * 4, out_specs=[tile] * 3,
        out_shape=[_sds((R, C), F32)] * 3, compiler_params=_params("parallel"),
    )(w, g, m, v)


def adamw_layer(name, layer, ws, gs, ms, vs, prev, after=()):
    n = len(ws)
    steps = 8

    def body(*refs):
        ins, outs = refs[:4 * n], refs[-4 * n:]
        for i in range(n):
            w_ref, g_ref, m_ref, v_ref = (ins[k * n + i] for k in range(4))
            g = g_ref[...]
            outs[i][...] = g
            outs[n + i][...], outs[2 * n + i][...], outs[3 * n + i][...] = _adamw_math(w_ref[...], g, m_ref[...],
                                                                                        v_ref[...])

    def stacked(a):
        return pl.BlockSpec((None, a.shape[1] // steps, a.shape[2]), lambda t: (layer, t, 0))

    def flat(a):
        return pl.BlockSpec((a.shape[0] // steps, a.shape[1]), lambda t: (t, 0))

    in_specs = [stacked(a) for a in ws] + [flat(a) for a in gs] + [stacked(a) for a in ms] + [stacked(a) for a in vs]
    operands = [*ws, *gs, *ms, *vs]
    aliases = {}
    if prev is not None:
        flat_prev = [a for group in prev for a in group]
        in_specs += [pl.BlockSpec(memory_space=pl.ANY)] * len(flat_prev)
        aliases = {4 * n + i: i for i in range(4 * n)}
        operands += flat_prev
    in_specs += [pl.BlockSpec(memory_space=pl.ANY)] * len(after)
    operands += list(after)
    out = pl.pallas_call(
        body, name=name, grid=(steps,), in_specs=in_specs, out_specs=[stacked(a) for a in ws] * 4,
        out_shape=[_sds(a.shape, F32) for a in ws] * 4, input_output_aliases=aliases,
        compiler_params=_params("parallel"),
    )(*operands)
    return [list(out[k * n:(k + 1) * n]) for k in range(4)]


def _relu2(u):
    r = jnp.maximum(u.astype(F32), 0.0)
    return r * r


def layer_fwd(tag, h0, p_bf, wt, after=(), mid=None):
    hn1 = rmsnorm(f"{tag}_norm1", h0, wt["norm1_g"], after)
    proj = mm_nn(f"{tag}_proj", hn1, wt["w_in"], tn=1408)
    y = conv_fwd(f"{tag}_conv", proj, wt["conv_w"])
    y, yb_tot = attn_fwd(f"{tag}_attn", proj, wt["gq"], wt["gk"], y)
    y = sgu_fwd(f"{tag}_sgu", proj, wt["gv"], wt["sgu_w"], wt["sgu_bias"], y)
    h1 = mm_nn(f"{tag}_out", y, wt["w_out"], extras=(h0,), epi=lambda acc, h: (h + acc,), out_dtypes=(F32,))
    hn2 = rmsnorm(f"{tag}_norm2", h1, wt["norm2_g"], () if mid is None else mid(yb_tot))
    uu = mm_nn(f"{tag}_ff1", hn2, wt["w_ff1"])
    h2 = mm_nn(f"{tag}_ff2", uu, wt["w_ff2"], pro=_relu2, extras=(h1,), epi=lambda acc, h: (h + acc,),
               out_dtypes=(F32,), tm=512)
    hn3 = rmsnorm(f"{tag}_norm3", h2, wt["norm3_g"])
    gp = mm_nn(f"{tag}_gate", hn3, wt["w_ple_gate"])
    h3, pp = mm_nn(f"{tag}_ple", p_bf, wt["w_ple_proj"], extras=(gp, h2),
                   epi=lambda acc, g, h: (h + jax.nn.sigmoid(g.astype(F32)) * acc, acc), out_dtypes=(F32, BF))
    saved = dict(h0=h0, h1=h1, h2=h2, hn1=hn1, hn2=hn2, hn3=hn3, proj=proj, yb_tot=yb_tot, y=y, uu=uu, gp=gp, pp=pp,
                 p_bf=p_bf)
    return h3, saved


def layer_bwd(tag, dh3, sv, wt, after=(), mid=None):
    dgp, dpp = ple_bwd_elem(f"{tag}_dple", dh3, sv["gp"], sv["pp"], after)
    g = {}
    g["w_ple_proj"] = mm_tn(f"{tag}_dwp", sv["p_bf"], dpp, col_blocks=N_CHIPS)
    g["w_ple_gate"] = mm_tn(f"{tag}_dwg", sv["hn3"], dgp)
    dh2, g["norm3_g"] = mm_nt_rmsbwd(f"{tag}_dnorm3", dgp, wt["w_ple_gate"], sv["h2"], wt["norm3_g"], dh3)

    duu = mm_nt(f"{tag}_dff2", dh2, wt["w_ff2"], extras=(sv["uu"],),
                epi=lambda acc, u: acc * (2.0 * jnp.maximum(u.astype(F32), 0.0)))
    g["w_ff2"] = mm_tn(f"{tag}_dw2", sv["uu"], dh2, pro_x=_relu2)
    g["w_ff1"] = mm_tn(f"{tag}_dw1", sv["hn2"], duu, col_blocks=N_CHIPS)
    dh1, g["norm2_g"] = mm_nt_rmsbwd(f"{tag}_dnorm2", duu, wt["w_ff1"], sv["h1"], wt["norm2_g"], dh2)

    dy = mm_nt(f"{tag}_dout", dh1, wt["w_out"], after=() if mid is None else mid(dh1))
    g["w_out"] = mm_tn(f"{tag}_dwo", sv["y"], dh1)
    dab, dac, dah, g["conv_w"] = conv_bwd(f"{tag}_dconv", dy, sv["proj"], wt["conv_w"])
    dq, dk, dv, g["gq"], g["gk"] = attn_bwd(f"{tag}_dattn", dy, sv["proj"], sv["yb_tot"], wt["gq"], wt["gk"])
    dcu, dcv, g["sgu_w"], g["sgu_bias"], g["gv"] = sgu_bwd(f"{tag}_dsgu", dy, sv["proj"], wt["gv"], wt["sgu_w"],
                                                           wt["sgu_bias"])
    dproj = jnp.concatenate([dab, dac, dah, dq, dk, dv, dcu, dcv], axis=-1)
    g["w_in"] = mm_tn(f"{tag}_dwi", sv["hn1"], dproj, tn=1408)
    dh0, g["norm1_g"] = mm_nt_rmsbwd(f"{tag}_dnorm1", dproj, wt["w_in"], sv["h0"], wt["norm1_g"], dh1)
    return dh0, g


def prep_small(norm1_g, q_norm_g, k_norm_g, sgu_norm_g, sgu_w, sgu_b, norm2_g, norm3_g, conv_w_full):
    return dict(
        norm1_g=norm1_g, norm2_g=norm2_g, norm3_g=norm3_g, conv_w=conv_w_full,
        gq=(jnp.tile(q_norm_g, 2) * SCALE).reshape(1, LANES), gk=jnp.tile(k_norm_g, 2).reshape(1, LANES),
        gv=sgu_norm_g.reshape(1, SGU_W), sgu_w=sgu_w, sgu_bias=jnp.repeat(sgu_b.T, HEAD_DIM, axis=1))


def small_grads(g):
    return dict(
        norm1_g=g["norm1_g"][0], norm2_g=g["norm2_g"][0], norm3_g=g["norm3_g"][0], conv_w=g["conv_w"],
        q_norm_g=g["gq"][0, :HEAD_DIM], k_norm_g=g["gk"][0, :HEAD_DIM], sgu_norm_g=g["gv"][0], sgu_w=g["sgu_w"],
        sgu_b=g["sgu_bias"][:, :SGU_HEADS].T)


HBM_SPEC = pl.BlockSpec(memory_space=pltpu.HBM)
BIG = ("w_in", "w_out", "w_ff1", "w_ff2", "w_ple_gate", "w_ple_proj")


def _mesh_pos():
    return lax.axis_index("x"), lax.axis_index("y"), lax.axis_index("c")


def _other_chips(x, y):
    return [(1 - x, y), (x, 1 - y), (1 - x, 1 - y)]


def _half(rows, core):
    h = rows // 2
    return pl.ds(pl.multiple_of(core * h, 16), h)


def _remote(src, dst, send_sems, recv_sems, k, to):
    return pltpu.make_async_remote_copy(src_ref=src, dst_ref=dst, send_sem=send_sems.at[k], recv_sem=recv_sems.at[k],
                                        device_id=to, device_id_type=MESH)


SEM_SPEC = pl.BlockSpec(memory_space=pltpu.SEMAPHORE)
ANY_SPEC = pl.BlockSpec(memory_space=pl.ANY)
SIDE_EFFECT = pltpu.SideEffectType.DATAFLOW_SIDE_EFFECTING


def _in_hbm(arrays):
    return [pltpu.with_memory_space_constraint(a, pltpu.HBM) for a in arrays]


def copies_start(name, srcs, lands, plan, after=()):
    ns, nl, na = len(srcs), len(lands), len(after)

    def body(*refs):
        src_refs, land_refs = refs[:ns], refs[ns:ns + nl]
        send_sem, recv_sem = refs[ns + nl + na], refs[ns + nl + na + 1]
        token = refs[-1]
        for src, dst, dev in plan(src_refs, land_refs, *_mesh_pos()):
            pltpu.make_async_remote_copy(src_ref=src, dst_ref=dst, send_sem=send_sem, recv_sem=recv_sem,
                                         device_id=dev, device_id_type=MESH).start()
        token[...] = jnp.zeros_like(token)

    out = pl.pallas_call(
        body, name=name,
        in_specs=[HBM_SPEC] * (ns + nl) + [ANY_SPEC] * na,
        out_specs=(SEM_SPEC, SEM_SPEC, *[HBM_SPEC] * (ns + nl), pl.BlockSpec(memory_space=pltpu.VMEM)),
        out_shape=(pltpu.SemaphoreType.DMA(()), pltpu.SemaphoreType.DMA(()),
                   *[pltpu.HBM(a.shape, a.dtype) for a in (*srcs, *lands)], _sds((8, LANES), F32)),
        input_output_aliases={i: 2 + i for i in range(ns + nl)},
        compiler_params=pltpu.CompilerParams(has_side_effects=SIDE_EFFECT),
    )(*_in_hbm(srcs), *_in_hbm(lands), *after)
    return out[0], out[1], list(out[2:2 + ns]), list(out[2 + ns:2 + ns + nl]), out[-1]


def copies_wait(name, started, plan, after=()):
    send_sem, recv_sem, srcs, lands, _ = started
    ns, nl, na = len(srcs), len(lands), len(after)

    def body(*refs):
        src_refs, land_refs = refs[:ns], refs[ns:ns + nl]
        send_sem, recv_sem = refs[ns + nl], refs[ns + nl + 1]
        for src, dst, dev in plan(src_refs, land_refs, *_mesh_pos()):
            cp = pltpu.make_async_remote_copy(src_ref=src, dst_ref=dst, send_sem=send_sem, recv_sem=recv_sem,
                                              device_id=dev, device_id_type=MESH)
            cp.wait_send()
            cp.wait_recv()

    out = pl.pallas_call(
        body, name=name,
        in_specs=[HBM_SPEC] * (ns + nl) + [SEM_SPEC, SEM_SPEC] + [ANY_SPEC] * na,
        out_specs=[HBM_SPEC] * (ns + nl),
        out_shape=[pltpu.HBM(a.shape, a.dtype) for a in (*srcs, *lands)],
        input_output_aliases={i: i for i in range(ns + nl)},
        compiler_params=pltpu.CompilerParams(has_side_effects=SIDE_EFFECT),
    )(*srcs, *lands, send_sem, recv_sem, *after)
    return list(out[:ns]), list(out[ns:])


def _gather_plan(srcs, lands, x, y, c):
    me = 2 * x + y
    return [(src.at[_half(src.shape[0], c)], land.at[me, _half(src.shape[0], c)], (*chip, c))
            for src, land in zip(srcs, lands) for chip in _other_chips(x, y)]


def _gather_arrivals(srcs, lands, x, y, c):
    return [(src.at[_half(src.shape[0], c)], land.at[2 * chip[0] + chip[1], _half(src.shape[0], c)], (*chip, c))
            for src, land in zip(srcs, lands) for chip in _other_chips(x, y)]


def _forward_plan(srcs, lands, x, y, c):
    me, sibling = 2 * x + y, (x, y, 1 - c)
    out = []
    for src, land in zip(srcs, lands):
        out.append((src, land.at[me], sibling))
        for chip in _other_chips(x, y):
            region = land.at[2 * chip[0] + chip[1], _half(src.shape[0], c)]
            out.append((region, region, sibling))
    return out


def _forward_arrivals(srcs, lands, x, y, c):
    me, sibling = 2 * x + y, (x, y, 1 - c)
    out = []
    for src, land in zip(srcs, lands):
        out.append((src, land.at[me], sibling))
        for chip in _other_chips(x, y):
            slot = land.at[2 * chip[0] + chip[1]]
            out.append((slot.at[_half(src.shape[0], c)], slot.at[_half(src.shape[0], 1 - c)], sibling))
    return out


def _join_plan(srcs, lands, x, y, c):
    return [(land.at[_half(land.shape[0], c)], land.at[_half(land.shape[0], c)], (x, y, 1 - c)) for land in lands]


def _join_arrivals(srcs, lands, x, y, c):
    return [(land.at[_half(land.shape[0], c)], land.at[_half(land.shape[0], 1 - c)], (x, y, 1 - c)) for land in lands]


def _pair_plan(srcs, lands, x, y, c):
    return [(src.at[:, _half(src.shape[1], 1 - c)], land, (x, y, 1 - c)) for src, land in zip(srcs, lands)]


def add_own_half(name, core, grads, got):
    n = len(grads)

    def body(core_ref, *refs):
        for i in range(n):
            refs[2 * n + i][...] = (refs[i][...].astype(F32) + refs[n + i][...].astype(F32)).astype(BF)

    def spec(g, own):
        blk = (None, g.shape[1] // 2, g.shape[2])
        return pl.BlockSpec(blk, (lambda j, core_ref: (j, core_ref[0], 0)) if own else (lambda j, core_ref: (j, 0, 0)))

    return pl.pallas_call(
        body, name=name,
        grid_spec=pltpu.PrefetchScalarGridSpec(
            num_scalar_prefetch=1, grid=(N_CHIPS,),
            in_specs=[spec(g, True) for g in grads] + [spec(g, False) for g in grads],
            out_specs=[spec(g, False) for g in grads]),
        out_shape=[_sds(r.shape, BF) for r in got], compiler_params=_params("parallel"),
    )(core, *grads, *got)


def _chips_plan(srcs, lands, x, y, c):
    return [(src.at[2 * chip[0] + chip[1]], land.at[k], (*chip, c))
            for src, land in zip(srcs, lands) for k, chip in enumerate(_other_chips(x, y))]


def sum_chips(name, place, parts, got):
    n = len(got)

    def body(place_ref, *refs):
        for i in range(n):
            acc = refs[i][...].astype(F32)
            for k in range(N_CHIPS - 1):
                acc = acc + refs[n + i][k].astype(F32)
            refs[2 * n + i][...] = acc

    steps = 2
    return pl.pallas_call(
        body, name=name,
        grid_spec=pltpu.PrefetchScalarGridSpec(
            num_scalar_prefetch=1, grid=(steps,),
            in_specs=[pl.BlockSpec((None, g.shape[1] // steps, g.shape[2]), lambda t, place_ref: (place_ref[0], t, 0))
                      for g in parts] +
                     [pl.BlockSpec((N_CHIPS - 1, g.shape[1] // steps, g.shape[2]), lambda t, place_ref: (0, t, 0))
                      for g in got],
            out_specs=[pl.BlockSpec((g.shape[1] // steps, g.shape[2]),
                                    lambda t, place_ref: (place_ref[1] * steps + t, 0)) for g in got]),
        out_shape=[_sds((2 * g.shape[1], g.shape[2]), F32) for g in got], compiler_params=_params("parallel"),
    )(place, *parts, *got)


def reduce_scatter_pair(tag, grads):
    lands = [lax.empty((N_CHIPS, g.shape[1] // 2, g.shape[2]), g.dtype) for g in grads]
    return copies_start(f"{tag}_rs_pair_start", grads, lands, _pair_plan)


def reduce_scatter_begin(tag, core, pairing, after):
    grads, got = copies_wait(f"{tag}_rs_pair_wait", pairing, _pair_plan, after)
    parts = add_own_half(f"{tag}_rs_add", core, grads, got)
    lands = [lax.empty((N_CHIPS - 1,) + p.shape[1:], p.dtype) for p in parts]
    return copies_start(f"{tag}_rs_start", parts, lands, _chips_plan)


def reduce_scatter_sum(tag, place, started, after):
    parts, got = copies_wait(f"{tag}_rs_wait", started, _chips_plan, after)
    return copies_start(f"{tag}_rs_join_start", [], sum_chips(f"{tag}_rs_sum", place, parts, got), _join_plan)


def reduce_scatter_end(tag, joining, after):
    return copies_wait(f"{tag}_rs_join_wait", joining, _join_arrivals, after)[1]


def small_allreduce(name, x):
    R = x.shape[0]
    H = R // 2

    def body(x_ref, o_ref, pair_ref, chip_ref, send_sems, recv_sems):
        xx, yy, c = _mesh_pos()
        me = 2 * xx + yy
        chips = _other_chips(xx, yy)
        sibling = (xx, yy, 1 - c)
        mine = pl.ds(pl.multiple_of(c * H, 8), H)
        theirs = pl.ds(pl.multiple_of((1 - c) * H, 8), H)
        a = _remote(x_ref.at[theirs], pair_ref.at[theirs], send_sems, recv_sems, 0, sibling)
        a.start()
        a.wait_send()
        _remote(x_ref.at[mine], pair_ref.at[mine], send_sems, recv_sems, 0, sibling).wait_recv()
        chip_ref[me] = x_ref[mine, :] + pair_ref[mine, :]
        sends = []
        for k, chip in enumerate(chips):
            cp = _remote(chip_ref.at[me], chip_ref.at[me], send_sems, recv_sems, 1 + k, (*chip, c))
            cp.start()
            sends.append(cp)
        for k, chip in enumerate(chips):
            slot = chip_ref.at[2 * chip[0] + chip[1]]
            _remote(slot, slot, send_sems, recv_sems, 1 + k, (*chip, c)).wait_recv()
        o_ref[mine, :] = (chip_ref[0] + chip_ref[1]) + (chip_ref[2] + chip_ref[3])
        b = _remote(o_ref.at[mine], o_ref.at[mine], send_sems, recv_sems, 4, sibling)
        b.start()
        b.wait_send()
        _remote(o_ref.at[theirs], o_ref.at[theirs], send_sems, recv_sems, 4, sibling).wait_recv()
        for cp in sends:
            cp.wait_send()

    return pl.pallas_call(
        body, name=name,
        in_specs=[pl.BlockSpec(memory_space=pltpu.VMEM)], out_specs=pl.BlockSpec(memory_space=pltpu.VMEM),
        out_shape=_sds((R, LANES), F32),
        scratch_shapes=[pltpu.VMEM((R, LANES), F32), pltpu.VMEM((N_CHIPS, H, LANES), F32),
                        pltpu.SemaphoreType.DMA((5,)), pltpu.SemaphoreType.DMA((5,))],
        compiler_params=pltpu.CompilerParams(vmem_limit_bytes=VMEM_LIMIT),
    )(x)


WEIGHTS = ("norm1_g", "w_in", "conv_w", "q_norm_g", "k_norm_g", "sgu_norm_g", "sgu_w", "sgu_b", "w_out", "norm2_g",
           "w_ff1", "w_ff2", "norm3_g", "w_ple_gate", "w_ple_proj")
SMALL = ("norm1_g", "norm2_g", "norm3_g", "q_norm_g", "k_norm_g", "sgu_norm_g", "sgu_w", "sgu_b", "conv_w")


PACK = 8 * LANES


def _pack_rows(arrays):
    blocks = []
    for a in arrays:
        v = a.reshape(-1)
        blocks.append(jnp.pad(v, (0, (-v.shape[0]) % PACK)).reshape(-1, LANES))
    rows = sum(b.shape[0] for b in blocks)
    if rows % 16:
        blocks.append(jnp.zeros((16 - rows % 16, LANES), F32))
    return jnp.concatenate(blocks, axis=0)


def _unpack_rows(packed, shapes):
    out, pos = [], 0
    flat = packed.reshape(-1)
    for shp in shapes:
        size = math.prod(shp)
        out.append(flat[pos:pos + size].reshape(shp))
        pos += size + (-size) % PACK
    return out


def kernel(x, p, norm1_g, w_in, conv_w, q_norm_g, k_norm_g, sgu_norm_g, sgu_w, sgu_b, w_out, norm2_g, w_ff1, w_ff2, norm3_g, w_ple_gate, w_ple_proj, loss_target, m_norm1_g, m_w_in, m_conv_w, m_q_norm_g, m_k_norm_g, m_sgu_norm_g, m_sgu_w, m_sgu_b, m_w_out, m_norm2_g, m_w_ff1, m_w_ff2, m_norm3_g, m_w_ple_gate, m_w_ple_proj, v_norm1_g, v_w_in, v_conv_w, v_q_norm_g, v_k_norm_g, v_sgu_norm_g, v_sgu_w, v_sgu_b, v_w_out, v_norm2_g, v_w_ff1, v_w_ff2, v_norm3_g, v_w_ple_gate, v_w_ple_proj):
    w = dict(norm1_g=norm1_g, w_in=w_in, conv_w=conv_w, q_norm_g=q_norm_g, k_norm_g=k_norm_g, sgu_norm_g=sgu_norm_g,
             sgu_w=sgu_w, sgu_b=sgu_b, w_out=w_out, norm2_g=norm2_g, w_ff1=w_ff1, w_ff2=w_ff2, norm3_g=norm3_g,
             w_ple_gate=w_ple_gate, w_ple_proj=w_ple_proj)
    m = dict(norm1_g=m_norm1_g, w_in=m_w_in, conv_w=m_conv_w, q_norm_g=m_q_norm_g, k_norm_g=m_k_norm_g,
             sgu_norm_g=m_sgu_norm_g, sgu_w=m_sgu_w, sgu_b=m_sgu_b, w_out=m_w_out, norm2_g=m_norm2_g, w_ff1=m_w_ff1,
             w_ff2=m_w_ff2, norm3_g=m_norm3_g, w_ple_gate=m_w_ple_gate, w_ple_proj=m_w_ple_proj)
    v = dict(norm1_g=v_norm1_g, w_in=v_w_in, conv_w=v_conv_w, q_norm_g=v_q_norm_g, k_norm_g=v_k_norm_g,
             sgu_norm_g=v_sgu_norm_g, sgu_w=v_sgu_w, sgu_b=v_sgu_b, w_out=v_w_out, norm2_g=v_norm2_g, w_ff1=v_w_ff1,
             w_ff2=v_w_ff2, norm3_g=v_norm3_g, w_ple_gate=v_w_ple_gate, w_ple_proj=v_w_ple_proj)
    depth = w_in.shape[0]
    d_model = x.shape[-1]
    chip = 2 * lax.axis_index("x") + lax.axis_index("y")
    core = lax.axis_index("c")
    core_arr = core.reshape(1).astype(jnp.int32)

    cw_cols = conv_w.shape[-1]
    placed = lax.dynamic_update_slice(jnp.zeros((depth, 3, CONV_W), F32), conv_w, (0, 0, chip * cw_cols))
    placed = jnp.where(core == 0, placed, 0.0)
    conv_full = _unpack_rows(small_allreduce("conv_w_gather", _pack_rows([placed])), [(depth, 3, CONV_W)])[0]

    h = x[0]
    p_bf = p[:, 0].astype(BF)
    saved, full = [], []

    def gather_start(l, after):
        shards = [w[n][l].astype(BF) for n in BIG]
        lands = [lax.empty((N_CHIPS,) + s.shape, BF) for s in shards]
        return copies_start(f"l{l}_gather_start", shards, lands, _gather_plan, after)

    def gather_forward(l, started, after):
        shards, lands = copies_wait(f"l{l}_gather_wait", started, _gather_arrivals, after)
        forwarding = copies_start(f"l{l}_forward_start", shards, lands, _forward_plan)
        return forwarding, gather_start(l + 1, (forwarding[-1],)) if l + 1 < depth else None

    forwarding, started = gather_forward(0, gather_start(0, ()), (h,))
    for l in range(depth):
        g_in, g_out, g_ff1, g_ff2, g_gate, g_proj = copies_wait(f"l{l}_forward_wait", forwarding, _forward_arrivals,
                                                                 (h,))[1]
        nxt = {}

        def mid(arr, l=l, started=started, nxt=nxt):
            nxt["forwarding"], nxt["started"] = gather_forward(l + 1, started, (arr,))
            return [t[-1] for t in nxt.values() if t is not None]

        wt = prep_small(norm1_g[l], q_norm_g[l], k_norm_g[l], sgu_norm_g[l], sgu_w[l], sgu_b[l], norm2_g[l], norm3_g[l],
                        conv_full[l])
        wt["w_in"] = jnp.transpose(g_in, (1, 0, 2)).reshape(d_model, -1)
        wt["w_out"] = g_out.reshape(-1, d_model)
        wt["w_ff1"] = g_ff1
        wt["w_ff2"] = g_ff2.reshape(-1, d_model)
        wt["w_ple_gate"] = g_gate.reshape(-1, d_model)
        wt["w_ple_proj"] = g_proj
        last = l + 1 == depth
        h, sv = layer_fwd(f"l{l}", h, p_bf[l], wt, () if last else (started[-1],), None if last else mid)
        if not last:
            forwarding, started = nxt["forwarding"], nxt["started"]
        saved.append(sv)
        full.append(wt)

    loss_tile, dh = loss_head("loss", h, loss_target[0])
    loss = lax.psum(loss_tile[0, 0], ("x", "y", "c"))

    small = [None] * depth
    chip_arr = jnp.stack([chip, core]).astype(jnp.int32)
    big_w, big_m, big_v = ([d[n] for n in BIG] for d in (w, m, v))
    pairing, joining, tokens = None, {}, ()
    for l in reversed(range(depth)):
        box = {}

        def mid(arr, l=l, pairing=pairing, box=box):
            box["started"] = reduce_scatter_begin(f"l{l + 1}", core_arr, pairing, (arr,))
            return (box["started"][-1],)

        dh, g = layer_bwd(f"l{l}", dh, saved[l], full[l], tokens, None if pairing is None else mid)
        tokens = ()
        if pairing is not None:
            joining[l + 1] = reduce_scatter_sum(f"l{l + 1}", chip_arr, box["started"], (dh,))
            tokens = (joining[l + 1][-1],)
        small[l] = small_grads(g)
        shards_in = w_in.shape[-1]
        gl = [jnp.transpose(g["w_in"].reshape(d_model, N_CHIPS, shards_in), (1, 0, 2)),
              g["w_out"].reshape(N_CHIPS, -1, d_model), g["w_ff1"], g["w_ff2"].reshape(N_CHIPS, -1, d_model),
              g["w_ple_gate"].reshape(N_CHIPS, -1, d_model), g["w_ple_proj"]]
        pairing = reduce_scatter_pair(f"l{l}", gl)
        tokens += (pairing[-1],)

    started = reduce_scatter_begin("l0", core_arr, pairing, (dh,))
    updated = None
    for l in reversed(range(1, depth)):
        reduced = reduce_scatter_end(f"l{l}", joining[l], (started[-1],))
        updated = adamw_layer(f"l{l}_adamw", l, big_w, reduced, big_m, big_v, updated)
    grads, delta, new_m, new_v = {}, {}, {}, {}
    packed = _pack_rows([small[l][n] for l in range(depth) for n in SMALL])
    shapes = [small[l][n].shape for l in range(depth) for n in SMALL]
    pieces = _unpack_rows(small_allreduce("small_grads", packed), shapes)
    for i, n in enumerate(SMALL):
        grads[n] = jnp.stack([pieces[l * len(SMALL) + i] for l in range(depth)])
    grads["conv_w"] = lax.dynamic_slice(grads["conv_w"], (0, 0, chip * cw_cols), (depth, 3, cw_cols))
    for n in SMALL:
        shp = w[n].shape
        two_d = (-1, shp[-1]) if n != "sgu_w" else (-1, LANES)
        d, nm, nv = adamw(f"adamw_{n}", w[n].reshape(two_d), grads[n].reshape(two_d), m[n].reshape(two_d),
                          v[n].reshape(two_d))
        delta[n], new_m[n], new_v[n] = d.reshape(shp), nm.reshape(shp), nv.reshape(shp)

    done = [new_v[n] for n in SMALL] + ([] if updated is None else [updated[3][0]])
    reduced = reduce_scatter_end("l0", reduce_scatter_sum("l0", chip_arr, started, (dh, *done)), ())
    updated = adamw_layer("l0_adamw", 0, big_w, reduced, big_m, big_v, updated)
    for k, d in enumerate((grads, delta, new_m, new_v)):
        d.update(zip(BIG, updated[k]))

    return (loss, dh[None], *[grads[n] for n in WEIGHTS], *[delta[n] for n in WEIGHTS], *[new_m[n] for n in WEIGHTS],
            *[new_v[n] for n in WEIGHTS])
```

```python
import functools
import math

import jax
import jax.numpy as jnp
from jax import lax
from jax.experimental import pallas as pl
from jax.experimental.pallas import tpu as pltpu

F32 = jnp.float32
BF = jnp.bfloat16
MESH = pl.DeviceIdType.MESH
HIGHEST = lax.Precision.HIGHEST

EPS = 1e-6
HEAD_DIM = 64
CONV_W = 256
ATTN_W = 512
SGU_W = 256
D_MIX = CONV_W + ATTN_W + SGU_W
CHUNK = 128
N_CHIPS = 4
SCALE = HEAD_DIM ** -0.5
LANES = 128
VMEM_LIMIT = 56 * 1024 * 1024

ADAM_LR = 0.001
ADAM_B1 = 0.9
ADAM_B2 = 0.999
ADAM_EPS = 1e-08
ADAM_WD = 0.01
ADAM_STEP = 10

NT_DIMS = (((1,), (1,)), ((), ()))
TN_DIMS = (((0,), (0,)), ((), ()))


def _params(*sem):
    return pltpu.CompilerParams(dimension_semantics=sem if sem else None, vmem_limit_bytes=VMEM_LIMIT)


def _sds(shape, dtype):
    return jax.ShapeDtypeStruct(shape, dtype)


def _erf(x):
    return lax.erf(x)


def _gelu(x):
    return 0.5 * x * (1.0 + _erf(x * (2.0 ** -0.5)))


def _gelu_grad(x):
    return 0.5 * (1.0 + _erf(x * (2.0 ** -0.5))) + x * jnp.exp(-0.5 * x * x) * (1.0 / math.sqrt(2.0 * math.pi))


def _log_sigmoid(z):
    return jnp.minimum(z, 0.0) - jnp.log(1.0 + jnp.exp(-jnp.abs(z)))


def _head_mean_matrix(width):
    r = lax.broadcasted_iota(jnp.int32, (width, width), 0) // HEAD_DIM
    c = lax.broadcasted_iota(jnp.int32, (width, width), 1) // HEAD_DIM
    return (r == c).astype(BF)


def _head_mean(x, m):
    hi = x.astype(BF)
    lo = (x - hi.astype(F32)).astype(BF)
    return _dot2_stacked(hi, lo, m) * (1.0 / HEAD_DIM)


def mm_nn(name, x, w, *, extras=(), pro=None, epi=None, out_dtypes=None, tm=None, tn=512):
    S, K = x.shape
    if w.ndim == 3:
        J, _, tn = w.shape
        N = J * tn
        w_spec = pl.BlockSpec((None, K, tn), lambda n, m: (n, 0, 0))
    else:
        N = w.shape[1]
        tn = min(tn, N)
        w_spec = pl.BlockSpec((K, tn), lambda n, m: (0, n))
    tm = S if tm is None else min(tm, S)
    out_dtypes = (BF,) if out_dtypes is None else out_dtypes
    n_ex, n_out = len(extras), len(out_dtypes)

    def body(x_ref, w_ref, *rest):
        xv = x_ref[...]
        if pro is not None:
            xv = pro(xv)
        acc = jnp.dot(xv.astype(BF), w_ref[...], preferred_element_type=F32)
        outs = (acc,) if epi is None else epi(acc, *[e[...] for e in rest[:n_ex]])
        for o_ref, o in zip(rest[n_ex:], outs):
            o_ref[...] = o.astype(o_ref.dtype)

    tile = pl.BlockSpec((tm, tn), lambda n, m: (m, n))
    out = pl.pallas_call(
        body, name=name, grid=(N // tn, S // tm),
        in_specs=[pl.BlockSpec((tm, K), lambda n, m: (m, 0)), w_spec] + [tile] * n_ex,
        out_specs=[tile] * n_out,
        out_shape=[_sds((S, N), d) for d in out_dtypes],
        compiler_params=_params("parallel", "parallel"),
    )(x, w, *extras)
    return out[0] if n_out == 1 else out


def norm_mm(name, h, g, w, *, tn=512, after=()):
    S, K = h.shape
    if w.ndim == 3:
        J, _, tn = w.shape
        N = J * tn
        w_spec = pl.BlockSpec((None, K, tn), lambda n: (n, 0, 0))
    else:
        N = w.shape[1]
        tn = min(tn, N)
        w_spec = pl.BlockSpec((K, tn), lambda n: (0, n))
    rows = min(S, 256)

    def body(h_ref, g_ref, w_ref, *rest):
        hn_ref, o_ref = rest[-2:]

        @pl.when(pl.program_id(0) == 0)
        def _():
            def chunk(i, _):
                r = pl.ds(pl.multiple_of(i * rows, rows), rows)
                x = h_ref[r, :]
                scale = lax.rsqrt(jnp.mean(x * x, axis=-1, keepdims=True) + EPS)
                hn_ref[r, :] = ((x * scale) * g_ref[...]).astype(BF)
                return 0

            lax.fori_loop(0, S // rows, chunk, 0)

        o_ref[...] = jnp.dot(hn_ref[...], w_ref[...], preferred_element_type=F32).astype(BF)

    whole = pl.BlockSpec((S, K), lambda n: (0, 0))
    return pl.pallas_call(
        body, name=name, grid=(N // tn,),
        in_specs=[pl.BlockSpec((S, K), lambda n: (0, 0), pipeline_mode=pl.Buffered(1)),
                  pl.BlockSpec((1, K), lambda n: (0, 0)), w_spec] + [pl.BlockSpec(memory_space=pl.ANY)] * len(after),
        out_specs=[whole, pl.BlockSpec((S, tn), lambda n: (0, n))],
        out_shape=[_sds((S, K), BF), _sds((S, N), BF)], compiler_params=_params("arbitrary"),
    )(h, g.reshape(1, K), w, *after)


def mm_nt(name, dy, w, *, extras=(), epi=None, tk=512, after=()):
    S, N = dy.shape
    K = w.shape[0]
    tk = min(tk, K)
    n_ex = len(extras)

    def body(dy_ref, w_ref, *rest):
        dyb = rest[-1]

        @pl.when(pl.program_id(0) == 0)
        def _():
            dyb[...] = dy_ref[...].astype(BF)

        acc = lax.dot_general(dyb[...], w_ref[...], NT_DIMS, preferred_element_type=F32)
        if epi is not None:
            acc = epi(acc, *[e[...] for e in rest[:n_ex]])
        rest[-2][...] = acc.astype(BF)

    col = pl.BlockSpec((S, tk), lambda k: (0, k))
    return pl.pallas_call(
        body, name=name, grid=(K // tk,),
        in_specs=[pl.BlockSpec((S, N), lambda k: (0, 0)), pl.BlockSpec((tk, N), lambda k: (k, 0))] + [col] * n_ex +
                 [pl.BlockSpec(memory_space=pl.ANY)] * len(after),
        out_specs=col, out_shape=_sds((S, K), BF), scratch_shapes=[pltpu.VMEM((S, N), BF)],
        compiler_params=_params("arbitrary"),
    )(dy, w, *extras, *after)


def mm_nt_rmsbwd(name, dy, w, h, g, dres, *, tm=256):
    S, N = dy.shape
    D = h.shape[1]
    tm = min(tm, S)
    blocked = w.ndim == 3
    nj = w.shape[2] if blocked else N

    def body(dy_ref, w_ref, h_ref, g_ref, dres_ref, dh_ref, dg_ref):
        i = pl.program_id(0)
        if blocked:
            dyn = None
            for j in range(w.shape[0]):
                part = lax.dot_general(dy_ref[:, j * nj:(j + 1) * nj].astype(BF), w_ref[j], NT_DIMS,
                                       preferred_element_type=F32)
                dyn = part if dyn is None else dyn + part
        else:
            dyn = lax.dot_general(dy_ref[...].astype(BF), w_ref[...], NT_DIMS, preferred_element_type=F32)
        x = h_ref[...]
        r = lax.rsqrt(jnp.mean(x * x, axis=-1, keepdims=True) + EPS)
        t = dyn * g_ref[...]
        dh_ref[...] = dres_ref[...] + r * t - x * (r * r * r) * jnp.mean(t * x, axis=-1, keepdims=True)
        part = jnp.sum(dyn * (x * r), axis=0, keepdims=True)

        @pl.when(i == 0)
        def _():
            dg_ref[...] = part

        @pl.when(i > 0)
        def _():
            dg_ref[...] += part

    w_spec = pl.BlockSpec(w.shape, (lambda i: (0, 0, 0)) if blocked else (lambda i: (0, 0)))
    row = pl.BlockSpec((tm, D), lambda i: (i, 0))
    vec = pl.BlockSpec((1, D), lambda i: (0, 0))
    return pl.pallas_call(
        body, name=name, grid=(S // tm,),
        in_specs=[pl.BlockSpec((tm, N), lambda i: (i, 0)), w_spec, row, vec, row],
        out_specs=[row, vec], out_shape=[_sds((S, D), F32), _sds((1, D), F32)],
        compiler_params=_params("arbitrary"),
    )(dy, w, h, g.reshape(1, D), dres)


def mm_tn(name, x, dy, *, pro_x=None, col_blocks=None, tk=1024, tn=1024):
    S, K = x.shape
    N = dy.shape[1]
    tk = min(tk, K)
    if col_blocks is not None:
        tn = N // col_blocks
        out_shape = _sds((col_blocks, K, tn), BF)
        out_spec = pl.BlockSpec((None, tk, tn), lambda k, n: (n, k, 0))
    else:
        tn = min(tn, N)
        out_shape = _sds((K, N), BF)
        out_spec = pl.BlockSpec((tk, tn), lambda k, n: (k, n))

    def body(x_ref, dy_ref, o_ref):
        xv = x_ref[...]
        if pro_x is not None:
            xv = pro_x(xv)
        o_ref[...] = lax.dot_general(xv.astype(BF), dy_ref[...].astype(BF), TN_DIMS,
                                     preferred_element_type=F32).astype(BF)

    return pl.pallas_call(
        body, name=name, grid=(K // tk, N // tn),
        in_specs=[pl.BlockSpec((S, tk), lambda k, n: (0, k)), pl.BlockSpec((S, tn), lambda k, n: (0, n))],
        out_specs=out_spec, out_shape=out_shape, compiler_params=_params("parallel", "parallel"),
    )(x, dy)


def _conv_parts(ac_ref, ah_ref, cw):
    a_c = ac_ref[...].astype(F32)
    a_h = ah_ref[...].astype(F32)
    x = a_c * a_h
    row = lax.broadcasted_iota(jnp.int32, x.shape, 0)
    x1 = jnp.where(row >= 1, pltpu.roll(x, 1, 0), 0.0)
    x2 = jnp.where(row >= 2, pltpu.roll(x, 2, 0), 0.0)
    cv = cw[0:1] * x2 + cw[1:2] * x1 + cw[2:3] * x
    return a_c, a_h, x, x1, x2, cv, row


def conv_fwd(name, proj, cw):
    S = proj.shape[0]

    def body(ab_ref, ac_ref, ah_ref, cw_ref, o_ref):
        cv = _conv_parts(ac_ref, ah_ref, cw_ref[...])[5]
        o_ref[...] = (ab_ref[...].astype(F32) * cv).astype(BF)

    col = lambda j: pl.BlockSpec((S, CONV_W), lambda i, j=j: (0, j))
    return pl.pallas_call(
        body, name=name, grid=(1,),
        in_specs=[col(0), col(1), col(2), pl.BlockSpec((3, CONV_W), lambda i: (0, 0))],
        out_specs=pl.BlockSpec((S, CONV_W), lambda i: (0, 0)),
        out_shape=_sds((S, D_MIX), BF), compiler_params=_params("arbitrary"),
    )(proj, proj, proj, cw)


def conv_bwd(name, dy, proj, cw):
    S = proj.shape[0]

    def body(dy_ref, ab_ref, ac_ref, ah_ref, cw_ref, dab_ref, dac_ref, dah_ref, dcw_ref):
        w = cw_ref[...]
        a_c, a_h, x, x1, x2, cv, row = _conv_parts(ac_ref, ah_ref, w)
        d = dy_ref[...].astype(F32)
        dab_ref[...] = (d * cv).astype(BF)
        dcv = d * ab_ref[...].astype(F32)
        d1 = jnp.where(row < S - 1, pltpu.roll(dcv, S - 1, 0), 0.0)
        d2 = jnp.where(row < S - 2, pltpu.roll(dcv, S - 2, 0), 0.0)
        dx = w[2:3] * dcv + w[1:2] * d1 + w[0:1] * d2
        dac_ref[...] = (dx * a_h).astype(BF)
        dah_ref[...] = (dx * a_c).astype(BF)
        dcw_ref[0:1, :] = jnp.sum(dcv * x2, axis=0, keepdims=True)
        dcw_ref[1:2, :] = jnp.sum(dcv * x1, axis=0, keepdims=True)
        dcw_ref[2:3, :] = jnp.sum(dcv * x, axis=0, keepdims=True)

    col = lambda j: pl.BlockSpec((S, CONV_W), lambda i, j=j: (0, j))
    one = pl.BlockSpec((S, CONV_W), lambda i: (0, 0))
    small = pl.BlockSpec((3, CONV_W), lambda i: (0, 0))
    return pl.pallas_call(
        body, name=name, grid=(1,),
        in_specs=[col(0), col(0), col(1), col(2), small],
        out_specs=[one, one, one, small],
        out_shape=[_sds((S, CONV_W), BF)] * 3 + [_sds((3, CONV_W), F32)],
        compiler_params=_params("arbitrary"),
    )(dy, proj, proj, proj, cw)


SGU_HEADS = SGU_W // HEAD_DIM
CU_BLOCK = 2304 // SGU_W
CV_BLOCK = 2560 // SGU_W


def _sgu_common(cu_ref, cv_ref, gv_ref, tm):
    c_u = cu_ref[...].astype(F32)
    c_v = cv_ref[...].astype(F32)
    hm = _head_mean_matrix(SGU_W)
    u = _gelu(c_u)
    vg = _gelu(c_v)
    r = lax.rsqrt(_head_mean(vg * vg, hm) + EPS)
    vv = (vg * r) * gv_ref[...]
    head = lax.broadcasted_iota(jnp.int32, (CHUNK, SGU_W), 1) // HEAD_DIM
    tri = (lax.broadcasted_iota(jnp.int32, (CHUNK, CHUNK), 0) >=
           lax.broadcasted_iota(jnp.int32, (CHUNK, CHUNK), 1))
    return c_u, c_v, hm, u, vg, r, vv, head, tri


def _sgu_mix(w_ref, tri, head, vvc, bias):
    sv = bias
    for g in range(SGU_HEADS):
        wg = jnp.where(tri, w_ref[g], 0.0).astype(BF)
        sv = sv + jnp.where(head == g, jnp.dot(wg, vvc, preferred_element_type=F32), 0.0)
    return sv


def sgu_fwd(name, proj, gv, w, bias, y):
    S = proj.shape[0]
    tm = min(S, 512)

    def body(cu_ref, cv_ref, gv_ref, w_ref, b_ref, y_ref, o_ref):
        _, _, _, u, _, _, vv, head, tri = _sgu_common(cu_ref, cv_ref, gv_ref, tm)
        vvb = vv.astype(BF)
        for ch in range(tm // CHUNK):
            rows = slice(ch * CHUNK, (ch + 1) * CHUNK)
            sv = _sgu_mix(w_ref, tri, head, vvb[rows], b_ref[...])
            o_ref[rows, :] = (u[rows] * sv).astype(BF)

    const = lambda shape: pl.BlockSpec(shape, lambda i: (0,) * len(shape))
    return pl.pallas_call(
        body, name=name, grid=(S // tm,),
        in_specs=[pl.BlockSpec((tm, SGU_W), lambda i: (i, CU_BLOCK)), pl.BlockSpec((tm, SGU_W), lambda i: (i, CV_BLOCK)),
                  const((1, SGU_W)), const((SGU_HEADS, CHUNK, CHUNK)), const((CHUNK, SGU_W)),
                  pl.BlockSpec(memory_space=pl.ANY)],
        out_specs=pl.BlockSpec((tm, SGU_W), lambda i: (i, (CONV_W + ATTN_W) // SGU_W)),
        out_shape=_sds(y.shape, BF), input_output_aliases={5: 0}, compiler_params=_params("parallel"),
    )(proj, proj, gv, w, bias, y)


def sgu_bwd(name, dy, proj, gv, w, bias):
    S = proj.shape[0]
    tm = min(S, 512)

    def body(dy_ref, cu_ref, cv_ref, gv_ref, w_ref, b_ref, dcu_ref, dcv_ref, dw_ref, db_ref, dgv_ref, dvv_s):
        i = pl.program_id(0)
        c_u, c_v, hm, u, vg, r, vv, head, tri = _sgu_common(cu_ref, cv_ref, gv_ref, tm)
        vvb = vv.astype(BF)
        d = dy_ref[...].astype(F32)
        ind = (lax.broadcasted_iota(jnp.int32, (SGU_W, LANES), 0) // HEAD_DIM ==
               lax.broadcasted_iota(jnp.int32, (SGU_W, LANES), 1)).astype(BF)
        dw_acc = [jnp.zeros((CHUNK, CHUNK), F32) for _ in range(SGU_HEADS)]
        db_acc = jnp.zeros((CHUNK, LANES), F32)
        for ch in range(tm // CHUNK):
            rows = slice(ch * CHUNK, (ch + 1) * CHUNK)
            sv = _sgu_mix(w_ref, tri, head, vvb[rows], b_ref[...])
            dcu_ref[rows, :] = (d[rows] * sv * _gelu_grad(c_u[rows])).astype(BF)
            dsv = d[rows] * u[rows]
            dsv_hi = dsv.astype(BF)
            db_acc = db_acc + _dot2_stacked(dsv_hi, (dsv - dsv_hi.astype(F32)).astype(BF), ind)
            dvv = jnp.zeros((CHUNK, SGU_W), F32)
            for g in range(SGU_HEADS):
                dsv_g = jnp.where(head == g, dsv, 0.0).astype(BF)
                wg = jnp.where(tri, w_ref[g], 0.0).astype(BF)
                dvv = dvv + lax.dot_general(wg, dsv_g, TN_DIMS, preferred_element_type=F32)
                dw_acc[g] = dw_acc[g] + lax.dot_general(dsv_g, vvb[rows], NT_DIMS, preferred_element_type=F32)
            dvv_s[rows, :] = dvv
        dvv = dvv_s[...]
        gvv = gv_ref[...]
        t = dvv * gvv
        dvg = r * t - vg * (r * r * r) * _head_mean(t * vg, hm)
        dcv_ref[...] = (dvg * _gelu_grad(c_v)).astype(BF)
        dgv = jnp.sum(dvv * (vg * r), axis=0, keepdims=True)

        @pl.when(i == 0)
        def _():
            for g in range(SGU_HEADS):
                dw_ref[g] = jnp.where(tri, dw_acc[g], 0.0)
            db_ref[...] = db_acc
            dgv_ref[...] = dgv

        @pl.when(i > 0)
        def _():
            for g in range(SGU_HEADS):
                dw_ref[g] += jnp.where(tri, dw_acc[g], 0.0)
            db_ref[...] += db_acc
            dgv_ref[...] += dgv

    const = lambda shape: pl.BlockSpec(shape, lambda i: (0,) * len(shape))
    tile = pl.BlockSpec((tm, SGU_W), lambda i: (i, 0))
    return pl.pallas_call(
        body, name=name, grid=(S // tm,),
        in_specs=[pl.BlockSpec((tm, SGU_W), lambda i: (i, 3)),
                  pl.BlockSpec((tm, SGU_W), lambda i: (i, CU_BLOCK)), pl.BlockSpec((tm, SGU_W), lambda i: (i, CV_BLOCK)),
                  const((1, SGU_W)), const((SGU_HEADS, CHUNK, CHUNK)), const((CHUNK, SGU_W))],
        out_specs=[tile, tile, const((SGU_HEADS, CHUNK, CHUNK)), const((CHUNK, LANES)), const((1, SGU_W))],
        out_shape=[_sds((S, SGU_W), BF), _sds((S, SGU_W), BF), _sds((SGU_HEADS, CHUNK, CHUNK), F32),
                   _sds((CHUNK, LANES), F32), _sds((1, SGU_W), F32)],
        scratch_shapes=[pltpu.VMEM((tm, SGU_W), F32)],
        compiler_params=_params("arbitrary"),
    )(dy, proj, proj, gv, w, bias)


HEAD_PAIRS = ATTN_W // LANES
Q_BLOCK0 = 768 // LANES
K_BLOCK0 = 1280 // LANES
V_BLOCK0 = 1792 // LANES


def _attn_tile(S):
    return min(S, 256)


def _qk_norm(x, g, hm):
    r = lax.rsqrt(_head_mean(x * x, hm) + EPS)
    return r, (x * r) * g


MASKED = -1e30


def _logit_parts(z):
    lb = _log_sigmoid(z)
    lr = lb - z
    hi = lr.astype(BF)
    return lb, hi, (lr - hi.astype(F32)).astype(BF)


def _stack_heads(x, lane):
    return jnp.concatenate([jnp.where(lane < HEAD_DIM, x, 0.0), jnp.where(lane >= HEAD_DIM, x, 0.0)],
                           axis=0).astype(BF)


def _dot2_stacked(hi, lo, u):
    rows = hi.shape[0]
    both = jnp.dot(jnp.concatenate([hi, lo], axis=0), u, preferred_element_type=F32)
    return both[:rows] + both[rows:]


def attn_fwd(name, proj, gq, gk, y):
    S = proj.shape[0]
    T = _attn_tile(S)
    nq = S // T

    def body(q_ref, k_ref, v_ref, gq_ref, gk_ref, y_ref, o_ref, tot_ref, kn_s, lb_s, hi_s, lo_s, z_s, a_s, o_s):
        qi = pl.program_id(1)
        hm = _head_mean_matrix(LANES)

        @pl.when(qi == 0)
        def _():
            kn_s[...] = _qk_norm(k_ref[...].astype(F32), gk_ref[...], hm)[1].astype(BF)

        qn = _qk_norm(q_ref[...].astype(F32), gq_ref[...], hm)[1]
        lane = lax.broadcasted_iota(jnp.int32, (T, LANES), 1)
        qst = _stack_heads(qn, lane)
        rowi = lax.broadcasted_iota(jnp.int32, (T, T), 0)
        coli = lax.broadcasted_iota(jnp.int32, (T, T), 1)
        u_excl = (rowi > coli).astype(BF)
        diagonal = jnp.where(coli < rowi, 0.0, MASKED)

        def logits(j):
            return lax.dot_general(qst, kn_s[pl.ds(pl.multiple_of(j * T, T), T), :], NT_DIMS,
                                   preferred_element_type=F32)

        def values(j):
            return v_ref[pl.ds(pl.multiple_of(j * T, T), T), :].astype(BF)

        def keep(slot, z):
            lb_s[slot], hi_s[...], lo_s[...] = _logit_parts(z)

        def step(it, carry):
            run = carry
            j = qi - it
            hi, lo = hi_s[...], lo_s[...]
            both = jnp.dot(jnp.concatenate([hi, lo], axis=0), u_excl, preferred_element_type=F32)
            o_s[...] += jnp.dot(a_s[...], values(jnp.minimum(j + 1, qi)), preferred_element_type=F32)
            z_after = logits(jnp.maximum(j - 2, 0))
            first = hi[:, 0:1].astype(F32) + lo[:, 0:1].astype(F32)
            keep((it + 1) % 2, z_s[...])
            later = both[:2 * T] + both[2 * T:]
            a_s[...] = jnp.exp(lb_s[it % 2] + later + run).astype(BF)
            z_s[...] = z_after
            return run + later[:, 0:1] + first

        keep(0, logits(qi) + jnp.concatenate([diagonal, diagonal], axis=0))
        z_s[...] = logits(jnp.maximum(qi - 1, 0))
        a_s[...] = jnp.zeros_like(a_s)
        o_s[...] = jnp.zeros_like(o_s)
        run = lax.fori_loop(0, qi + 1, step, jnp.zeros((2 * T, 1), F32))
        o = o_s[...] + jnp.dot(a_s[...], values(0), preferred_element_type=F32)
        o_ref[...] = jnp.where(lane < HEAD_DIM, o[:T], o[T:]).astype(BF)
        tot_ref[...] = jnp.where(lane < HEAD_DIM, run[:T], run[T:])

    gain = pl.BlockSpec((1, LANES), lambda hp, qi: (0, 0))
    full = lambda b0: pl.BlockSpec((S, LANES), lambda hp, qi, b0=b0: (0, b0 + hp))
    tile = pl.BlockSpec((T, LANES), lambda hp, qi: (qi, hp))
    return pl.pallas_call(
        body, name=name, grid=(HEAD_PAIRS, nq),
        in_specs=[pl.BlockSpec((T, LANES), lambda hp, qi: (qi, Q_BLOCK0 + hp)), full(K_BLOCK0), full(V_BLOCK0), gain, gain,
                  pl.BlockSpec(memory_space=pl.ANY)],
        out_specs=[pl.BlockSpec((T, LANES), lambda hp, qi: (qi, CONV_W // LANES + hp)), tile],
        out_shape=[_sds(y.shape, BF), _sds((S, ATTN_W), F32)], input_output_aliases={5: 0},
        scratch_shapes=[pltpu.VMEM((S, LANES), BF), pltpu.VMEM((2, 2 * T, T), F32), pltpu.VMEM((2 * T, T), BF),
                        pltpu.VMEM((2 * T, T), BF), pltpu.VMEM((2 * T, T), F32), pltpu.VMEM((2 * T, T), BF),
                        pltpu.VMEM((2 * T, LANES), F32)],
        compiler_params=_params("arbitrary", "arbitrary"),
    )(proj, proj, proj, gq, gk, y)


def attn_bwd(name, dy, proj, tot, gq, gk):
    S = proj.shape[0]
    T = _attn_tile(S)
    nq = S // T

    def body(q_ref, k_ref, v_ref, tot_ref, do_ref, gq_ref, gk_ref,
             dq_ref, dk_ref, dv_ref, dgq_ref, dgk_ref, kn_s, dkn_s, dv_s,
             lb_s, z_s, g_s, dq_s, hi_s, lo_s, a_s, ghi_s, glo_s):
        hp = pl.program_id(0)
        qi = pl.program_id(1)
        hm = _head_mean_matrix(LANES)

        @pl.when(qi == 0)
        def _():
            kn_s[...] = _qk_norm(k_ref[...].astype(F32), gk_ref[...], hm)[1].astype(BF)
            dkn_s[...] = jnp.zeros_like(dkn_s)
            dv_s[...] = jnp.zeros_like(dv_s)

        q = q_ref[...].astype(F32)
        rq, qn = _qk_norm(q, gq_ref[...], hm)
        lane = lax.broadcasted_iota(jnp.int32, (T, LANES), 1)
        qst = _stack_heads(qn, lane)
        dost = _stack_heads(do_ref[...].astype(F32), lane)
        total = jnp.concatenate([tot_ref[:, 0:1], tot_ref[:, HEAD_DIM:HEAD_DIM + 1]], axis=0)
        rowi = lax.broadcasted_iota(jnp.int32, (T, T), 0)
        coli = lax.broadcasted_iota(jnp.int32, (T, T), 1)
        u_upto = (rowi <= coli).astype(BF)
        u_before = (rowi < coli).astype(BF)
        diagonal = jnp.where(coli < rowi, 0.0, MASKED)

        on_diagonal = jnp.concatenate([diagonal, diagonal], axis=0)

        def rows(b):
            return pl.ds(pl.multiple_of(jnp.clip(b, 0, qi) * T, T), T)

        def logits(b):
            return lax.dot_general(qst, kn_s[rows(b), :], NT_DIMS, preferred_element_type=F32)

        def keep(b, z):
            bias = jnp.where(b == qi, on_diagonal, jnp.where(b > qi, MASKED, 0.0))
            lb_s[b % 3], hi_s[...], lo_s[...] = _logit_parts(z + bias)

        def step(i, carry):
            run, grun = carry
            both_before = jnp.dot(jnp.concatenate([ghi_s[...], glo_s[...]], axis=0), u_before,
                                  preferred_element_type=F32)
            both_upto = jnp.dot(jnp.concatenate([hi_s[...], lo_s[...]], axis=0), u_upto, preferred_element_type=F32)
            da = lax.dot_general(dost, v_ref[rows(i), :].astype(BF), NT_DIMS, preferred_element_type=F32)
            dv_s[rows(i - 1), :] += lax.dot_general(a_s[...], dost, TN_DIMS, preferred_element_type=F32)
            z_after = logits(i + 2)

            keep(i + 1, z_s[...])

            g = g_s[...]
            before = both_before[:2 * T] + both_before[2 * T:]
            dz = (g - jnp.exp(lb_s[(i + 2) % 3]) * (g + (grun + before))).astype(BF)
            dq_s[...] += jnp.dot(dz, kn_s[rows(i - 1), :], preferred_element_type=F32)
            dkn_s[rows(i - 1), :] += lax.dot_general(dz, qst, TN_DIMS, preferred_element_type=F32)
            grun = grun + before[:, T - 1:T] + g[:, T - 1:T]

            upto = both_upto[:2 * T] + both_upto[2 * T:]
            a = jnp.exp(lb_s[i % 3] + (total - run - upto))
            g = da * a
            a_s[...] = a.astype(BF)
            g_s[...] = g
            ghi = g.astype(BF)
            ghi_s[...] = ghi
            glo_s[...] = (g - ghi.astype(F32)).astype(BF)
            z_s[...] = z_after
            return run + upto[:, T - 1:T], grun

        lb_s[...] = jnp.full(lb_s.shape, MASKED, F32)
        for ref in (a_s, g_s, ghi_s, glo_s, dq_s):
            ref[...] = jnp.zeros_like(ref)
        keep(0, logits(0))
        z_s[...] = logits(1)
        lax.fori_loop(0, qi + 2, step, (jnp.zeros((2 * T, 1), F32), jnp.zeros((2 * T, 1), F32)))
        dqn = jnp.where(lane < HEAD_DIM, dq_s[:T, :], dq_s[T:, :])
        gq_v = gq_ref[...]
        t = dqn * gq_v
        dq_ref[...] = (rq * t - q * (rq * rq * rq) * _head_mean(t * q, hm)).astype(BF)
        dgq = jnp.sum(dqn * (q * rq), axis=0, keepdims=True) * SCALE
        first = jnp.logical_and(hp == 0, qi == 0)

        @pl.when(first)
        def _():
            dgq_ref[...] = dgq

        @pl.when(jnp.logical_not(first))
        def _():
            dgq_ref[...] += dgq

        @pl.when(qi == nq - 1)
        def _():
            k = k_ref[...].astype(F32)
            rk = _qk_norm(k, gk_ref[...], hm)[0]
            dkn = dkn_s[...]
            tk = dkn * gk_ref[...]
            dk_ref[...] = (rk * tk - k * (rk * rk * rk) * _head_mean(tk * k, hm)).astype(BF)
            dgk = jnp.sum(dkn * (k * rk), axis=0, keepdims=True)
            dv_ref[...] = dv_s[...].astype(BF)

            @pl.when(hp == 0)
            def _():
                dgk_ref[...] = dgk

            @pl.when(hp > 0)
            def _():
                dgk_ref[...] += dgk

            @pl.when(hp == HEAD_PAIRS - 1)
            def _():
                fold = (lax.broadcasted_iota(jnp.int32, (LANES, LANES), 0) % HEAD_DIM ==
                        lax.broadcasted_iota(jnp.int32, (LANES, LANES), 1) % HEAD_DIM).astype(F32)
                dgq_ref[...] = jnp.dot(dgq_ref[...], fold, precision=HIGHEST, preferred_element_type=F32)
                dgk_ref[...] = jnp.dot(dgk_ref[...], fold, precision=HIGHEST, preferred_element_type=F32)

    gain = pl.BlockSpec((1, LANES), lambda hp, qi: (0, 0))
    full = lambda b0: pl.BlockSpec((S, LANES), lambda hp, qi, b0=b0: (0, b0 + hp))
    tile = pl.BlockSpec((T, LANES), lambda hp, qi: (qi, hp))
    col = pl.BlockSpec((S, LANES), lambda hp, qi: (0, hp))
    dgain = pl.BlockSpec((1, LANES), lambda hp, qi: (0, 0))
    return pl.pallas_call(
        body, name=name, grid=(HEAD_PAIRS, nq),
        in_specs=[pl.BlockSpec((T, LANES), lambda hp, qi: (qi, Q_BLOCK0 + hp)), full(K_BLOCK0), full(V_BLOCK0),
                  tile, pl.BlockSpec((T, LANES), lambda hp, qi: (qi, 2 + hp)), gain, gain],
        out_specs=[tile, col, col, dgain, dgain],
        out_shape=[_sds((S, ATTN_W), BF)] * 3 + [_sds((1, LANES), F32)] * 2,
        scratch_shapes=[pltpu.VMEM((S, LANES), BF), pltpu.VMEM((S, LANES), F32), pltpu.VMEM((S, LANES), F32),
                        pltpu.VMEM((3, 2 * T, T), F32), pltpu.VMEM((2 * T, T), F32), pltpu.VMEM((2 * T, T), F32),
                        pltpu.VMEM((2 * T, LANES), F32)] + [pltpu.VMEM((2 * T, T), BF)] * 5,
        compiler_params=_params("arbitrary", "arbitrary"),
    )(proj, proj, proj, tot, dy, gq, gk)


def ple_bwd_elem(name, dh, gp, pp, after=()):
    S, D = dh.shape
    tm = min(S, 512)

    def body(dh_ref, gp_ref, pp_ref, *rest):
        dgp_ref, dpp_ref = rest[-2:]
        d = dh_ref[...]
        gate = jax.nn.sigmoid(gp_ref[...].astype(F32))
        dpp_ref[...] = (d * gate).astype(BF)
        dgp_ref[...] = (d * pp_ref[...].astype(F32) * gate * (1.0 - gate)).astype(BF)

    tile = pl.BlockSpec((tm, D), lambda i: (i, 0))
    return pl.pallas_call(
        body, name=name, grid=(S // tm,), in_specs=[tile] * 3 + [pl.BlockSpec(memory_space=pl.ANY)] * len(after),
        out_specs=[tile] * 2, out_shape=[_sds((S, D), BF)] * 2, compiler_params=_params("parallel"),
    )(dh, gp, pp, *after)


def loss_head(name, h, target):
    S, D = h.shape
    tm = min(S, 512)

    def body(h_ref, t_ref, loss_ref, dh_ref):
        i = pl.program_id(0)
        e = h_ref[...] - t_ref[...]
        dh_ref[...] = e * (1.0 / D)
        part = jnp.zeros((8, LANES), F32) + 0.5 * jnp.sum(jnp.mean(e * e, axis=-1, keepdims=True))

        @pl.when(i == 0)
        def _():
            loss_ref[...] = part

        @pl.when(i > 0)
        def _():
            loss_ref[...] += part

    tile = pl.BlockSpec((tm, D), lambda i: (i, 0))
    return pl.pallas_call(
        body, name=name, grid=(S // tm,), in_specs=[tile, tile],
        out_specs=[pl.BlockSpec((8, LANES), lambda i: (0, 0)), tile],
        out_shape=[_sds((8, LANES), F32), _sds((S, D), F32)], compiler_params=_params("arbitrary"),
    )(h, target)


def _adamw_math(w, g, m, v):
    c1 = 1.0 - ADAM_B1 ** ADAM_STEP
    c2 = 1.0 - ADAM_B2 ** ADAM_STEP
    nm = ADAM_B1 * m + (1.0 - ADAM_B1) * g
    nv = ADAM_B2 * v + (1.0 - ADAM_B2) * (g * g)
    return -ADAM_LR * ((nm / c1) / (jnp.sqrt(nv / c2) + ADAM_EPS) + ADAM_WD * w), nm, nv


def adamw(name, w, g, m, v):
    R, C = w.shape
    tr = R
    for cand in (512, 256, 128, 64, 32, 16, 8):
        if R % cand == 0:
            tr = cand
            break

    def body(w_ref, g_ref, m_ref, v_ref, d_ref, nm_ref, nv_ref):
        d_ref[...], nm_ref[...], nv_ref[...] = _adamw_math(w_ref[...], g_ref[...], m_ref[...], v_ref[...])

    tile = pl.BlockSpec((tr, C), lambda i: (i, 0))
    return pl.pallas_call(
        body, name=name, grid=(R // tr,), in_specs=[tile] * 4, out_specs=[tile] * 3,
        out_shape=[_sds((R, C), F32)] * 3, compiler_params=_params("parallel"),
    )(w, g, m, v)


def adamw_layer(name, layer, ws, gs, ms, vs, prev, after=()):
    n = len(ws)
    steps = 8

    def body(*refs):
        ins, outs = refs[:4 * n], refs[-4 * n:]
        for i in range(n):
            w_ref, g_ref, m_ref, v_ref = (ins[k * n + i] for k in range(4))
            g = g_ref[...]
            outs[i][...] = g
            outs[n + i][...], outs[2 * n + i][...], outs[3 * n + i][...] = _adamw_math(w_ref[...], g, m_ref[...],
                                                                                        v_ref[...])

    def stacked(a):
        return pl.BlockSpec((None, a.shape[1] // steps, a.shape[2]), lambda t: (layer, t, 0))

    def flat(a):
        return pl.BlockSpec((a.shape[0] // steps, a.shape[1]), lambda t: (t, 0))

    in_specs = [stacked(a) for a in ws] + [flat(a) for a in gs] + [stacked(a) for a in ms] + [stacked(a) for a in vs]
    operands = [*ws, *gs, *ms, *vs]
    aliases = {}
    if prev is not None:
        flat_prev = [a for group in prev for a in group]
        in_specs += [pl.BlockSpec(memory_space=pl.ANY)] * len(flat_prev)
        aliases = {4 * n + i: i for i in range(4 * n)}
        operands += flat_prev
    in_specs += [pl.BlockSpec(memory_space=pl.ANY)] * len(after)
    operands += list(after)
    out = pl.pallas_call(
        body, name=name, grid=(steps,), in_specs=in_specs, out_specs=[stacked(a) for a in ws] * 4,
        out_shape=[_sds(a.shape, F32) for a in ws] * 4, input_output_aliases=aliases,
        compiler_params=_params("parallel"),
    )(*operands)
    return [list(out[k * n:(k + 1) * n]) for k in range(4)]


def _relu2(u):
    r = jnp.maximum(u.astype(F32), 0.0)
    return r * r


def layer_fwd(tag, h0, p_bf, wt, after=(), mid=None):
    hn1, proj = norm_mm(f"{tag}_proj", h0, wt["norm1_g"], wt["w_in"], tn=256, after=after)
    y = conv_fwd(f"{tag}_conv", proj, wt["conv_w"])
    y, yb_tot = attn_fwd(f"{tag}_attn", proj, wt["gq"], wt["gk"], y)
    y = sgu_fwd(f"{tag}_sgu", proj, wt["gv"], wt["sgu_w"], wt["sgu_bias"], y)
    h1 = mm_nn(f"{tag}_out", y, wt["w_out"], extras=(h0,), epi=lambda acc, h: (h + acc,), out_dtypes=(F32,))
    hn2, uu = norm_mm(f"{tag}_ff1", h1, wt["norm2_g"], wt["w_ff1"], after=() if mid is None else mid(yb_tot))
    h2 = mm_nn(f"{tag}_ff2", uu, wt["w_ff2"], pro=_relu2, extras=(h1,), epi=lambda acc, h: (h + acc,),
               out_dtypes=(F32,), tm=512)
    hn3, gp = norm_mm(f"{tag}_gate", h2, wt["norm3_g"], wt["w_ple_gate"])
    h3, pp = mm_nn(f"{tag}_ple", p_bf, wt["w_ple_proj"], extras=(gp, h2),
                   epi=lambda acc, g, h: (h + jax.nn.sigmoid(g.astype(F32)) * acc, acc), out_dtypes=(F32, BF))
    saved = dict(h0=h0, h1=h1, h2=h2, hn1=hn1, hn2=hn2, hn3=hn3, proj=proj, yb_tot=yb_tot, y=y, uu=uu, gp=gp, pp=pp,
                 p_bf=p_bf)
    return h3, saved


def layer_bwd(tag, dh3, sv, wt, after=(), mid=None):
    dgp, dpp = ple_bwd_elem(f"{tag}_dple", dh3, sv["gp"], sv["pp"], after)
    g = {}
    g["w_ple_proj"] = mm_tn(f"{tag}_dwp", sv["p_bf"], dpp, col_blocks=N_CHIPS)
    g["w_ple_gate"] = mm_tn(f"{tag}_dwg", sv["hn3"], dgp)
    dh2, g["norm3_g"] = mm_nt_rmsbwd(f"{tag}_dnorm3", dgp, wt["w_ple_gate"], sv["h2"], wt["norm3_g"], dh3)

    duu = mm_nt(f"{tag}_dff2", dh2, wt["w_ff2"], extras=(sv["uu"],),
                epi=lambda acc, u: acc * (2.0 * jnp.maximum(u.astype(F32), 0.0)))
    g["w_ff2"] = mm_tn(f"{tag}_dw2", sv["uu"], dh2, pro_x=_relu2)
    g["w_ff1"] = mm_tn(f"{tag}_dw1", sv["hn2"], duu, col_blocks=N_CHIPS)
    dh1, g["norm2_g"] = mm_nt_rmsbwd(f"{tag}_dnorm2", duu, wt["w_ff1"], sv["h1"], wt["norm2_g"], dh2)

    dy = mm_nt(f"{tag}_dout", dh1, wt["w_out"], after=() if mid is None else mid(dh1))
    g["w_out"] = mm_tn(f"{tag}_dwo", sv["y"], dh1)
    dab, dac, dah, g["conv_w"] = conv_bwd(f"{tag}_dconv", dy, sv["proj"], wt["conv_w"])
    dq, dk, dv, g["gq"], g["gk"] = attn_bwd(f"{tag}_dattn", dy, sv["proj"], sv["yb_tot"], wt["gq"], wt["gk"])
    dcu, dcv, g["sgu_w"], g["sgu_bias"], g["gv"] = sgu_bwd(f"{tag}_dsgu", dy, sv["proj"], wt["gv"], wt["sgu_w"],
                                                           wt["sgu_bias"])
    dproj = jnp.concatenate([dab, dac, dah, dq, dk, dv, dcu, dcv], axis=-1)
    g["w_in"] = mm_tn(f"{tag}_dwi", sv["hn1"], dproj, tn=1408)
    dh0, g["norm1_g"] = mm_nt_rmsbwd(f"{tag}_dnorm1", dproj, wt["w_in"], sv["h0"], wt["norm1_g"], dh1)
    return dh0, g


def prep_small(norm1_g, q_norm_g, k_norm_g, sgu_norm_g, sgu_w, sgu_b, norm2_g, norm3_g, conv_w_full):
    return dict(
        norm1_g=norm1_g, norm2_g=norm2_g, norm3_g=norm3_g, conv_w=conv_w_full,
        gq=(jnp.tile(q_norm_g, 2) * SCALE).reshape(1, LANES), gk=jnp.tile(k_norm_g, 2).reshape(1, LANES),
        gv=sgu_norm_g.reshape(1, SGU_W), sgu_w=sgu_w, sgu_bias=jnp.repeat(sgu_b.T, HEAD_DIM, axis=1))


def small_grads(g):
    return dict(
        norm1_g=g["norm1_g"][0], norm2_g=g["norm2_g"][0], norm3_g=g["norm3_g"][0], conv_w=g["conv_w"],
        q_norm_g=g["gq"][0, :HEAD_DIM], k_norm_g=g["gk"][0, :HEAD_DIM], sgu_norm_g=g["gv"][0], sgu_w=g["sgu_w"],
        sgu_b=g["sgu_bias"][:, :SGU_HEADS].T)


HBM_SPEC = pl.BlockSpec(memory_space=pltpu.HBM)
BIG = ("w_in", "w_out", "w_ff1", "w_ff2", "w_ple_gate", "w_ple_proj")


def _mesh_pos():
    return lax.axis_index("x"), lax.axis_index("y"), lax.axis_index("c")


def _other_chips(x, y):
    return [(1 - x, y), (x, 1 - y), (1 - x, 1 - y)]


def _half(rows, core):
    h = rows // 2
    return pl.ds(pl.multiple_of(core * h, 16), h)


def _remote(src, dst, send_sems, recv_sems, k, to):
    return pltpu.make_async_remote_copy(src_ref=src, dst_ref=dst, send_sem=send_sems.at[k], recv_sem=recv_sems.at[k],
                                        device_id=to, device_id_type=MESH)


SEM_SPEC = pl.BlockSpec(memory_space=pltpu.SEMAPHORE)
ANY_SPEC = pl.BlockSpec(memory_space=pl.ANY)
SIDE_EFFECT = pltpu.SideEffectType.DATAFLOW_SIDE_EFFECTING


def _in_hbm(arrays):
    return [pltpu.with_memory_space_constraint(a, pltpu.HBM) for a in arrays]


def copies_start(name, srcs, lands, plan, after=()):
    ns, nl, na = len(srcs), len(lands), len(after)

    def body(*refs):
        src_refs, land_refs = refs[:ns], refs[ns:ns + nl]
        send_sem, recv_sem = refs[ns + nl + na], refs[ns + nl + na + 1]
        token = refs[-1]
        for src, dst, dev in plan(src_refs, land_refs, *_mesh_pos()):
            pltpu.make_async_remote_copy(src_ref=src, dst_ref=dst, send_sem=send_sem, recv_sem=recv_sem,
                                         device_id=dev, device_id_type=MESH).start()
        token[...] = jnp.zeros_like(token)

    out = pl.pallas_call(
        body, name=name,
        in_specs=[HBM_SPEC] * (ns + nl) + [ANY_SPEC] * na,
        out_specs=(SEM_SPEC, SEM_SPEC, *[HBM_SPEC] * (ns + nl), pl.BlockSpec(memory_space=pltpu.VMEM)),
        out_shape=(pltpu.SemaphoreType.DMA(()), pltpu.SemaphoreType.DMA(()),
                   *[pltpu.HBM(a.shape, a.dtype) for a in (*srcs, *lands)], _sds((8, LANES), F32)),
        input_output_aliases={i: 2 + i for i in range(ns + nl)},
        compiler_params=pltpu.CompilerParams(has_side_effects=SIDE_EFFECT),
    )(*_in_hbm(srcs), *_in_hbm(lands), *after)
    return out[0], out[1], list(out[2:2 + ns]), list(out[2 + ns:2 + ns + nl]), out[-1]


def copies_wait(name, started, plan, after=()):
    send_sem, recv_sem, srcs, lands, _ = started
    ns, nl, na = len(srcs), len(lands), len(after)

    def body(*refs):
        src_refs, land_refs = refs[:ns], refs[ns:ns + nl]
        send_sem, recv_sem = refs[ns + nl], refs[ns + nl + 1]
        for src, dst, dev in plan(src_refs, land_refs, *_mesh_pos()):
            cp = pltpu.make_async_remote_copy(src_ref=src, dst_ref=dst, send_sem=send_sem, recv_sem=recv_sem,
                                              device_id=dev, device_id_type=MESH)
            cp.wait_send()
            cp.wait_recv()

    out = pl.pallas_call(
        body, name=name,
        in_specs=[HBM_SPEC] * (ns + nl) + [SEM_SPEC, SEM_SPEC] + [ANY_SPEC] * na,
        out_specs=[HBM_SPEC] * (ns + nl),
        out_shape=[pltpu.HBM(a.shape, a.dtype) for a in (*srcs, *lands)],
        input_output_aliases={i: i for i in range(ns + nl)},
        compiler_params=pltpu.CompilerParams(has_side_effects=SIDE_EFFECT),
    )(*srcs, *lands, send_sem, recv_sem, *after)
    return list(out[:ns]), list(out[ns:])


def _gather_plan(srcs, lands, x, y, c):
    me = 2 * x + y
    return [(src.at[_half(src.shape[0], c)], land.at[me, _half(src.shape[0], c)], (*chip, c))
            for src, land in zip(srcs, lands) for chip in _other_chips(x, y)]


def _gather_arrivals(srcs, lands, x, y, c):
    return [(src.at[_half(src.shape[0], c)], land.at[2 * chip[0] + chip[1], _half(src.shape[0], c)], (*chip, c))
            for src, land in zip(srcs, lands) for chip in _other_chips(x, y)]


def _forward_plan(srcs, lands, x, y, c):
    me, sibling = 2 * x + y, (x, y, 1 - c)
    out = []
    for src, land in zip(srcs, lands):
        out.append((src, land.at[me], sibling))
        for chip in _other_chips(x, y):
            region = land.at[2 * chip[0] + chip[1], _half(src.shape[0], c)]
            out.append((region, region, sibling))
    return out


def _forward_arrivals(srcs, lands, x, y, c):
    me, sibling = 2 * x + y, (x, y, 1 - c)
    out = []
    for src, land in zip(srcs, lands):
        out.append((src, land.at[me], sibling))
        for chip in _other_chips(x, y):
            slot = land.at[2 * chip[0] + chip[1]]
            out.append((slot.at[_half(src.shape[0], c)], slot.at[_half(src.shape[0], 1 - c)], sibling))
    return out


def _join_plan(srcs, lands, x, y, c):
    return [(land.at[_half(land.shape[0], c)], land.at[_half(land.shape[0], c)], (x, y, 1 - c)) for land in lands]


def _join_arrivals(srcs, lands, x, y, c):
    return [(land.at[_half(land.shape[0], c)], land.at[_half(land.shape[0], 1 - c)], (x, y, 1 - c)) for land in lands]


def _pair_plan(srcs, lands, x, y, c):
    return [(src.at[:, _half(src.shape[1], 1 - c)], land, (x, y, 1 - c)) for src, land in zip(srcs, lands)]


def add_own_half(name, core, grads, got):
    n = len(grads)

    def body(core_ref, *refs):
        for i in range(n):
            refs[2 * n + i][...] = (refs[i][...].astype(F32) + refs[n + i][...].astype(F32)).astype(BF)

    def spec(g, own):
        blk = (None, g.shape[1] // 2, g.shape[2])
        return pl.BlockSpec(blk, (lambda j, core_ref: (j, core_ref[0], 0)) if own else (lambda j, core_ref: (j, 0, 0)))

    return pl.pallas_call(
        body, name=name,
        grid_spec=pltpu.PrefetchScalarGridSpec(
            num_scalar_prefetch=1, grid=(N_CHIPS,),
            in_specs=[spec(g, True) for g in grads] + [spec(g, False) for g in grads],
            out_specs=[spec(g, False) for g in grads]),
        out_shape=[_sds(r.shape, BF) for r in got], compiler_params=_params("parallel"),
    )(core, *grads, *got)


def _chips_plan(srcs, lands, x, y, c):
    return [(src.at[2 * chip[0] + chip[1]], land.at[k], (*chip, c))
            for src, land in zip(srcs, lands) for k, chip in enumerate(_other_chips(x, y))]


def sum_chips(name, place, parts, got):
    n = len(got)

    def body(place_ref, *refs):
        for i in range(n):
            acc = refs[i][...].astype(F32)
            for k in range(N_CHIPS - 1):
                acc = acc + refs[n + i][k].astype(F32)
            refs[2 * n + i][...] = acc

    steps = 2
    return pl.pallas_call(
        body, name=name,
        grid_spec=pltpu.PrefetchScalarGridSpec(
            num_scalar_prefetch=1, grid=(steps,),
            in_specs=[pl.BlockSpec((None, g.shape[1] // steps, g.shape[2]), lambda t, place_ref: (place_ref[0], t, 0))
                      for g in parts] +
                     [pl.BlockSpec((N_CHIPS - 1, g.shape[1] // steps, g.shape[2]), lambda t, place_ref: (0, t, 0))
                      for g in got],
            out_specs=[pl.BlockSpec((g.shape[1] // steps, g.shape[2]),
                                    lambda t, place_ref: (place_ref[1] * steps + t, 0)) for g in got]),
        out_shape=[_sds((2 * g.shape[1], g.shape[2]), F32) for g in got], compiler_params=_params("parallel"),
    )(place, *parts, *got)


def reduce_scatter_pair(tag, grads):
    lands = [lax.empty((N_CHIPS, g.shape[1] // 2, g.shape[2]), g.dtype) for g in grads]
    return copies_start(f"{tag}_rs_pair_start", grads, lands, _pair_plan)


def reduce_scatter_begin(tag, core, pairing, after):
    grads, got = copies_wait(f"{tag}_rs_pair_wait", pairing, _pair_plan, after)
    parts = add_own_half(f"{tag}_rs_add", core, grads, got)
    lands = [lax.empty((N_CHIPS - 1,) + p.shape[1:], p.dtype) for p in parts]
    return copies_start(f"{tag}_rs_start", parts, lands, _chips_plan)


def reduce_scatter_sum(tag, place, started, after):
    parts, got = copies_wait(f"{tag}_rs_wait", started, _chips_plan, after)
    return copies_start(f"{tag}_rs_join_start", [], sum_chips(f"{tag}_rs_sum", place, parts, got), _join_plan)


def reduce_scatter_end(tag, joining, after):
    return copies_wait(f"{tag}_rs_join_wait", joining, _join_arrivals, after)[1]


def small_allreduce(name, x):
    R = x.shape[0]
    H = R // 2

    def body(x_ref, o_ref, pair_ref, chip_ref, send_sems, recv_sems):
        xx, yy, c = _mesh_pos()
        me = 2 * xx + yy
        chips = _other_chips(xx, yy)
        sibling = (xx, yy, 1 - c)
        mine = pl.ds(pl.multiple_of(c * H, 8), H)
        theirs = pl.ds(pl.multiple_of((1 - c) * H, 8), H)
        a = _remote(x_ref.at[theirs], pair_ref.at[theirs], send_sems, recv_sems, 0, sibling)
        a.start()
        a.wait_send()
        _remote(x_ref.at[mine], pair_ref.at[mine], send_sems, recv_sems, 0, sibling).wait_recv()
        chip_ref[me] = x_ref[mine, :] + pair_ref[mine, :]
        sends = []
        for k, chip in enumerate(chips):
            cp = _remote(chip_ref.at[me], chip_ref.at[me], send_sems, recv_sems, 1 + k, (*chip, c))
            cp.start()
            sends.append(cp)
        for k, chip in enumerate(chips):
            slot = chip_ref.at[2 * chip[0] + chip[1]]
            _remote(slot, slot, send_sems, recv_sems, 1 + k, (*chip, c)).wait_recv()
        o_ref[mine, :] = (chip_ref[0] + chip_ref[1]) + (chip_ref[2] + chip_ref[3])
        b = _remote(o_ref.at[mine], o_ref.at[mine], send_sems, recv_sems, 4, sibling)
        b.start()
        b.wait_send()
        _remote(o_ref.at[theirs], o_ref.at[theirs], send_sems, recv_sems, 4, sibling).wait_recv()
        for cp in sends:
            cp.wait_send()

    return pl.pallas_call(
        body, name=name,
        in_specs=[pl.BlockSpec(memory_space=pltpu.VMEM)], out_specs=pl.BlockSpec(memory_space=pltpu.VMEM),
        out_shape=_sds((R, LANES), F32),
        scratch_shapes=[pltpu.VMEM((R, LANES), F32), pltpu.VMEM((N_CHIPS, H, LANES), F32),
                        pltpu.SemaphoreType.DMA((5,)), pltpu.SemaphoreType.DMA((5,))],
        compiler_params=pltpu.CompilerParams(vmem_limit_bytes=VMEM_LIMIT),
    )(x)


WEIGHTS = ("norm1_g", "w_in", "conv_w", "q_norm_g", "k_norm_g", "sgu_norm_g", "sgu_w", "sgu_b", "w_out", "norm2_g",
           "w_ff1", "w_ff2", "norm3_g", "w_ple_gate", "w_ple_proj")
SMALL = ("norm1_g", "norm2_g", "norm3_g", "q_norm_g", "k_norm_g", "sgu_norm_g", "sgu_w", "sgu_b", "conv_w")


PACK = 8 * LANES


def _pack_rows(arrays):
    blocks = []
    for a in arrays:
        v = a.reshape(-1)
        blocks.append(jnp.pad(v, (0, (-v.shape[0]) % PACK)).reshape(-1, LANES))
    rows = sum(b.shape[0] for b in blocks)
    if rows % 16:
        blocks.append(jnp.zeros((16 - rows % 16, LANES), F32))
    return jnp.concatenate(blocks, axis=0)


def _unpack_rows(packed, shapes):
    out, pos = [], 0
    flat = packed.reshape(-1)
    for shp in shapes:
        size = math.prod(shp)
        out.append(flat[pos:pos + size].reshape(shp))
        pos += size + (-size) % PACK
    return out


def kernel(x, p, norm1_g, w_in, conv_w, q_norm_g, k_norm_g, sgu_norm_g, sgu_w, sgu_b, w_out, norm2_g, w_ff1, w_ff2, norm3_g, w_ple_gate, w_ple_proj, loss_target, m_norm1_g, m_w_in, m_conv_w, m_q_norm_g, m_k_norm_g, m_sgu_norm_g, m_sgu_w, m_sgu_b, m_w_out, m_norm2_g, m_w_ff1, m_w_ff2, m_norm3_g, m_w_ple_gate, m_w_ple_proj, v_norm1_g, v_w_in, v_conv_w, v_q_norm_g, v_k_norm_g, v_sgu_norm_g, v_sgu_w, v_sgu_b, v_w_out, v_norm2_g, v_w_ff1, v_w_ff2, v_norm3_g, v_w_ple_gate, v_w_ple_proj):
    w = dict(norm1_g=norm1_g, w_in=w_in, conv_w=conv_w, q_norm_g=q_norm_g, k_norm_g=k_norm_g, sgu_norm_g=sgu_norm_g,
             sgu_w=sgu_w, sgu_b=sgu_b, w_out=w_out, norm2_g=norm2_g, w_ff1=w_ff1, w_ff2=w_ff2, norm3_g=norm3_g,
             w_ple_gate=w_ple_gate, w_ple_proj=w_ple_proj)
    m = dict(norm1_g=m_norm1_g, w_in=m_w_in, conv_w=m_conv_w, q_norm_g=m_q_norm_g, k_norm_g=m_k_norm_g,
             sgu_norm_g=m_sgu_norm_g, sgu_w=m_sgu_w, sgu_b=m_sgu_b, w_out=m_w_out, norm2_g=m_norm2_g, w_ff1=m_w_ff1,
             w_ff2=m_w_ff2, norm3_g=m_norm3_g, w_ple_gate=m_w_ple_gate, w_ple_proj=m_w_ple_proj)
    v = dict(norm1_g=v_norm1_g, w_in=v_w_in, conv_w=v_conv_w, q_norm_g=v_q_norm_g, k_norm_g=v_k_norm_g,
             sgu_norm_g=v_sgu_norm_g, sgu_w=v_sgu_w, sgu_b=v_sgu_b, w_out=v_w_out, norm2_g=v_norm2_g, w_ff1=v_w_ff1,
             w_ff2=v_w_ff2, norm3_g=v_norm3_g, w_ple_gate=v_w_ple_gate, w_ple_proj=v_w_ple_proj)
    depth = w_in.shape[0]
    d_model = x.shape[-1]
    chip = 2 * lax.axis_index("x") + lax.axis_index("y")
    core = lax.axis_index("c")
    core_arr = core.reshape(1).astype(jnp.int32)

    cw_cols = conv_w.shape[-1]
    placed = lax.dynamic_update_slice(jnp.zeros((depth, 3, CONV_W), F32), conv_w, (0, 0, chip * cw_cols))
    placed = jnp.where(core == 0, placed, 0.0)
    conv_full = _unpack_rows(small_allreduce("conv_w_gather", _pack_rows([placed])), [(depth, 3, CONV_W)])[0]

    h = x[0]
    p_bf = p[:, 0].astype(BF)
    saved, full = [], []

    def gather_start(l, after):
        shards = [w[n][l].astype(BF) for n in BIG]
        lands = [lax.empty((N_CHIPS,) + s.shape, BF) for s in shards]
        return copies_start(f"l{l}_gather_start", shards, lands, _gather_plan, after)

    def gather_forward(l, started, after):
        shards, lands = copies_wait(f"l{l}_gather_wait", started, _gather_arrivals, after)
        forwarding = copies_start(f"l{l}_forward_start", shards, lands, _forward_plan)
        return forwarding, gather_start(l + 1, (forwarding[-1],)) if l + 1 < depth else None

    forwarding, started = gather_forward(0, gather_start(0, ()), (h,))
    for l in range(depth):
        g_in, g_out, g_ff1, g_ff2, g_gate, g_proj = copies_wait(f"l{l}_forward_wait", forwarding, _forward_arrivals,
                                                                 (h,))[1]
        nxt = {}

        def mid(arr, l=l, started=started, nxt=nxt):
            nxt["forwarding"], nxt["started"] = gather_forward(l + 1, started, (arr,))
            return [t[-1] for t in nxt.values() if t is not None]

        wt = prep_small(norm1_g[l], q_norm_g[l], k_norm_g[l], sgu_norm_g[l], sgu_w[l], sgu_b[l], norm2_g[l], norm3_g[l],
                        conv_full[l])
        wt["w_in"] = jnp.transpose(g_in, (1, 0, 2)).reshape(d_model, -1)
        wt["w_out"] = g_out.reshape(-1, d_model)
        wt["w_ff1"] = g_ff1
        wt["w_ff2"] = g_ff2.reshape(-1, d_model)
        wt["w_ple_gate"] = g_gate.reshape(-1, d_model)
        wt["w_ple_proj"] = g_proj
        last = l + 1 == depth
        h, sv = layer_fwd(f"l{l}", h, p_bf[l], wt, () if last else (started[-1],), None if last else mid)
        if not last:
            forwarding, started = nxt["forwarding"], nxt["started"]
        saved.append(sv)
        full.append(wt)

    loss_tile, dh = loss_head("loss", h, loss_target[0])
    loss = lax.psum(loss_tile[0, 0], ("x", "y", "c"))

    small = [None] * depth
    chip_arr = jnp.stack([chip, core]).astype(jnp.int32)
    big_w, big_m, big_v = ([d[n] for n in BIG] for d in (w, m, v))
    pairing, joining, tokens = None, {}, ()
    for l in reversed(range(depth)):
        box = {}

        def mid(arr, l=l, pairing=pairing, box=box):
            box["started"] = reduce_scatter_begin(f"l{l + 1}", core_arr, pairing, (arr,))
            return (box["started"][-1],)

        dh, g = layer_bwd(f"l{l}", dh, saved[l], full[l], tokens, None if pairing is None else mid)
        tokens = ()
        if pairing is not None:
            joining[l + 1] = reduce_scatter_sum(f"l{l + 1}", chip_arr, box["started"], (dh,))
            tokens = (joining[l + 1][-1],)
        small[l] = small_grads(g)
        shards_in = w_in.shape[-1]
        gl = [jnp.transpose(g["w_in"].reshape(d_model, N_CHIPS, shards_in), (1, 0, 2)),
              g["w_out"].reshape(N_CHIPS, -1, d_model), g["w_ff1"], g["w_ff2"].reshape(N_CHIPS, -1, d_model),
              g["w_ple_gate"].reshape(N_CHIPS, -1, d_model), g["w_ple_proj"]]
        pairing = reduce_scatter_pair(f"l{l}", gl)
        tokens += (pairing[-1],)

    started = reduce_scatter_begin("l0", core_arr, pairing, (dh,))
    updated = None
    for l in reversed(range(1, depth)):
        reduced = reduce_scatter_end(f"l{l}", joining[l], (started[-1],))
        updated = adamw_layer(f"l{l}_adamw", l, big_w, reduced, big_m, big_v, updated)
    grads, delta, new_m, new_v = {}, {}, {}, {}
    packed = _pack_rows([small[l][n] for l in range(depth) for n in SMALL])
    shapes = [small[l][n].shape for l in range(depth) for n in SMALL]
    pieces = _unpack_rows(small_allreduce("small_grads", packed), shapes)
    for i, n in enumerate(SMALL):
        grads[n] = jnp.stack([pieces[l * len(SMALL) + i] for l in range(depth)])
    grads["conv_w"] = lax.dynamic_slice(grads["conv_w"], (0, 0, chip * cw_cols), (depth, 3, cw_cols))
    for n in SMALL:
        shp = w[n].shape
        two_d = (-1, shp[-1]) if n != "sgu_w" else (-1, LANES)
        d, nm, nv = adamw(f"adamw_{n}", w[n].reshape(two_d), grads[n].reshape(two_d), m[n].reshape(two_d),
                          v[n].reshape(two_d))
        delta[n], new_m[n], new_v[n] = d.reshape(shp), nm.reshape(shp), nv.reshape(shp)

    done = [new_v[n] for n in SMALL] + ([] if updated is None else [updated[3][0]])
    reduced = reduce_scatter_end("l0", reduce_scatter_sum("l0", chip_arr, started, (dh, *done)), ())
    updated = adamw_layer("l0_adamw", 0, big_w, reduced, big_m, big_v, updated)
    for k, d in enumerate((grads, delta, new_m, new_v)):
        d.update(zip(BIG, updated[k]))

    return (loss, dh[None], *[grads[n] for n in WEIGHTS], *[delta[n] for n in WEIGHTS], *[new_m[n] for n in WEIGHTS],
            *[new_v[n] for n in WEIGHTS])
```

```python
import functools
import math

import jax
import jax.numpy as jnp
from jax import lax
from jax.experimental import pallas as pl
from jax.experimental.pallas import tpu as pltpu

F32 = jnp.float32
BF = jnp.bfloat16
MESH = pl.DeviceIdType.MESH
HIGHEST = lax.Precision.HIGHEST

EPS = 1e-6
HEAD_DIM = 64
CONV_W = 256
ATTN_W = 512
SGU_W = 256
D_MIX = CONV_W + ATTN_W + SGU_W
CHUNK = 128
N_CHIPS = 4
SCALE = HEAD_DIM ** -0.5
LANES = 128
VMEM_LIMIT = 56 * 1024 * 1024

ADAM_LR = 0.001
ADAM_B1 = 0.9
ADAM_B2 = 0.999
ADAM_EPS = 1e-08
ADAM_WD = 0.01
ADAM_STEP = 10

NT_DIMS = (((1,), (1,)), ((), ()))
TN_DIMS = (((0,), (0,)), ((), ()))


def _params(*sem):
    return pltpu.CompilerParams(dimension_semantics=sem if sem else None, vmem_limit_bytes=VMEM_LIMIT)


def _sds(shape, dtype):
    return jax.ShapeDtypeStruct(shape, dtype)


def _erf(x):
    return lax.erf(x)


def _gelu(x):
    return 0.5 * x * (1.0 + _erf(x * (2.0 ** -0.5)))


def _gelu_grad(x):
    return 0.5 * (1.0 + _erf(x * (2.0 ** -0.5))) + x * jnp.exp(-0.5 * x * x) * (1.0 / math.sqrt(2.0 * math.pi))


def _log_sigmoid(z):
    return jnp.minimum(z, 0.0) - jnp.log(1.0 + jnp.exp(-jnp.abs(z)))


def _head_mean_matrix(width):
    r = lax.broadcasted_iota(jnp.int32, (width, width), 0) // HEAD_DIM
    c = lax.broadcasted_iota(jnp.int32, (width, width), 1) // HEAD_DIM
    return (r == c).astype(BF)


def _head_mean(x, m):
    hi = x.astype(BF)
    lo = (x - hi.astype(F32)).astype(BF)
    return _dot2_stacked(hi, lo, m) * (1.0 / HEAD_DIM)


def mm_nn(name, x, w, *, extras=(), pro=None, epi=None, out_dtypes=None, tm=None, tn=512):
    S, K = x.shape
    if w.ndim == 3:
        J, _, tn = w.shape
        N = J * tn
        w_spec = pl.BlockSpec((None, K, tn), lambda n, m: (n, 0, 0))
    else:
        N = w.shape[1]
        tn = min(tn, N)
        w_spec = pl.BlockSpec((K, tn), lambda n, m: (0, n))
    tm = S if tm is None else min(tm, S)
    out_dtypes = (BF,) if out_dtypes is None else out_dtypes
    n_ex, n_out = len(extras), len(out_dtypes)

    def body(x_ref, w_ref, *rest):
        xv = x_ref[...]
        if pro is not None:
            xv = pro(xv)
        acc = jnp.dot(xv.astype(BF), w_ref[...], preferred_element_type=F32)
        outs = (acc,) if epi is None else epi(acc, *[e[...] for e in rest[:n_ex]])
        for o_ref, o in zip(rest[n_ex:], outs):
            o_ref[...] = o.astype(o_ref.dtype)

    tile = pl.BlockSpec((tm, tn), lambda n, m: (m, n))
    out = pl.pallas_call(
        body, name=name, grid=(N // tn, S // tm),
        in_specs=[pl.BlockSpec((tm, K), lambda n, m: (m, 0)), w_spec] + [tile] * n_ex,
        out_specs=[tile] * n_out,
        out_shape=[_sds((S, N), d) for d in out_dtypes],
        compiler_params=_params("parallel", "parallel"),
    )(x, w, *extras)
    return out[0] if n_out == 1 else out


def norm_mm(name, h, g, w, *, tn=512, after=()):
    S, K = h.shape
    if w.ndim == 3:
        J, _, tn = w.shape
        N = J * tn
        w_spec = pl.BlockSpec((None, K, tn), lambda n: (n, 0, 0))
    else:
        N = w.shape[1]
        tn = min(tn, N)
        w_spec = pl.BlockSpec((K, tn), lambda n: (0, n))
    rows = min(S, 256)

    def body(h_ref, g_ref, w_ref, *rest):
        hn_ref, o_ref = rest[-2:]

        @pl.when(pl.program_id(0) == 0)
        def _():
            def chunk(i, _):
                r = pl.ds(pl.multiple_of(i * rows, rows), rows)
                x = h_ref[r, :]
                scale = lax.rsqrt(jnp.mean(x * x, axis=-1, keepdims=True) + EPS)
                hn_ref[r, :] = ((x * scale) * g_ref[...]).astype(BF)
                return 0

            lax.fori_loop(0, S // rows, chunk, 0)

        o_ref[...] = jnp.dot(hn_ref[...], w_ref[...], preferred_element_type=F32).astype(BF)

    whole = pl.BlockSpec((S, K), lambda n: (0, 0))
    return pl.pallas_call(
        body, name=name, grid=(N // tn,),
        in_specs=[pl.BlockSpec((S, K), lambda n: (0, 0), pipeline_mode=pl.Buffered(1)),
                  pl.BlockSpec((1, K), lambda n: (0, 0)), w_spec] + [pl.BlockSpec(memory_space=pl.ANY)] * len(after),
        out_specs=[whole, pl.BlockSpec((S, tn), lambda n: (0, n))],
        out_shape=[_sds((S, K), BF), _sds((S, N), BF)], compiler_params=_params("arbitrary"),
    )(h, g.reshape(1, K), w, *after)


def mm_nt(name, dy, w, *, extras=(), epi=None, tk=512, after=()):
    S, N = dy.shape
    K = w.shape[0]
    tk = min(tk, K)
    n_ex = len(extras)

    def body(dy_ref, w_ref, *rest):
        dyb = rest[-1]

        @pl.when(pl.program_id(0) == 0)
        def _():
            dyb[...] = dy_ref[...].astype(BF)

        acc = lax.dot_general(dyb[...], w_ref[...], NT_DIMS, preferred_element_type=F32)
        if epi is not None:
            acc = epi(acc, *[e[...] for e in rest[:n_ex]])
        rest[-2][...] = acc.astype(BF)

    col = pl.BlockSpec((S, tk), lambda k: (0, k))
    return pl.pallas_call(
        body, name=name, grid=(K // tk,),
        in_specs=[pl.BlockSpec((S, N), lambda k: (0, 0)), pl.BlockSpec((tk, N), lambda k: (k, 0))] + [col] * n_ex +
                 [pl.BlockSpec(memory_space=pl.ANY)] * len(after),
        out_specs=col, out_shape=_sds((S, K), BF), scratch_shapes=[pltpu.VMEM((S, N), BF)],
        compiler_params=_params("arbitrary"),
    )(dy, w, *extras, *after)


def mm_nt_rmsbwd(name, dy, w, h, g, dres, *, tm=256):
    S, N = dy.shape
    D = h.shape[1]
    tm = min(tm, S)
    blocked = w.ndim == 3
    nj = w.shape[2] if blocked else N

    def body(dy_ref, w_ref, h_ref, g_ref, dres_ref, dh_ref, dg_ref):
        i = pl.program_id(0)
        if blocked:
            dyn = None
            for j in range(w.shape[0]):
                part = lax.dot_general(dy_ref[:, j * nj:(j + 1) * nj].astype(BF), w_ref[j], NT_DIMS,
                                       preferred_element_type=F32)
                dyn = part if dyn is None else dyn + part
        else:
            dyn = lax.dot_general(dy_ref[...].astype(BF), w_ref[...], NT_DIMS, preferred_element_type=F32)
        x = h_ref[...]
        r = lax.rsqrt(jnp.mean(x * x, axis=-1, keepdims=True) + EPS)
        t = dyn * g_ref[...]
        dh_ref[...] = dres_ref[...] + r * t - x * (r * r * r) * jnp.mean(t * x, axis=-1, keepdims=True)
        part = jnp.sum(dyn * (x * r), axis=0, keepdims=True)

        @pl.when(i == 0)
        def _():
            dg_ref[...] = part

        @pl.when(i > 0)
        def _():
            dg_ref[...] += part

    w_spec = pl.BlockSpec(w.shape, (lambda i: (0, 0, 0)) if blocked else (lambda i: (0, 0)))
    row = pl.BlockSpec((tm, D), lambda i: (i, 0))
    vec = pl.BlockSpec((1, D), lambda i: (0, 0))
    return pl.pallas_call(
        body, name=name, grid=(S // tm,),
        in_specs=[pl.BlockSpec((tm, N), lambda i: (i, 0)), w_spec, row, vec, row],
        out_specs=[row, vec], out_shape=[_sds((S, D), F32), _sds((1, D), F32)],
        compiler_params=_params("arbitrary"),
    )(dy, w, h, g.reshape(1, D), dres)


def mm_tn(name, x, dy, *, pro_x=None, col_blocks=None, tk=1024, tn=1024):
    S, K = x.shape
    N = dy.shape[1]
    tk = min(tk, K)
    if col_blocks is not None:
        tn = N // col_blocks
        out_shape = _sds((col_blocks, K, tn), BF)
        out_spec = pl.BlockSpec((None, tk, tn), lambda k, n: (n, k, 0))
    else:
        tn = min(tn, N)
        out_shape = _sds((K, N), BF)
        out_spec = pl.BlockSpec((tk, tn), lambda k, n: (k, n))

    def body(x_ref, dy_ref, o_ref):
        xv = x_ref[...]
        if pro_x is not None:
            xv = pro_x(xv)
        o_ref[...] = lax.dot_general(xv.astype(BF), dy_ref[...].astype(BF), TN_DIMS,
                                     preferred_element_type=F32).astype(BF)

    return pl.pallas_call(
        body, name=name, grid=(K // tk, N // tn),
        in_specs=[pl.BlockSpec((S, tk), lambda k, n: (0, k)), pl.BlockSpec((S, tn), lambda k, n: (0, n))],
        out_specs=out_spec, out_shape=out_shape, compiler_params=_params("parallel", "parallel"),
    )(x, dy)


def _conv_parts(ac_ref, ah_ref, cw):
    a_c = ac_ref[...].astype(F32)
    a_h = ah_ref[...].astype(F32)
    x = a_c * a_h
    row = lax.broadcasted_iota(jnp.int32, x.shape, 0)
    x1 = jnp.where(row >= 1, pltpu.roll(x, 1, 0), 0.0)
    x2 = jnp.where(row >= 2, pltpu.roll(x, 2, 0), 0.0)
    cv = cw[0:1] * x2 + cw[1:2] * x1 + cw[2:3] * x
    return a_c, a_h, x, x1, x2, cv, row


def conv_fwd(name, proj, cw):
    S = proj.shape[0]

    def body(ab_ref, ac_ref, ah_ref, cw_ref, o_ref):
        cv = _conv_parts(ac_ref, ah_ref, cw_ref[...])[5]
        o_ref[...] = (ab_ref[...].astype(F32) * cv).astype(BF)

    col = lambda j: pl.BlockSpec((S, CONV_W), lambda i, j=j: (0, j))
    return pl.pallas_call(
        body, name=name, grid=(1,),
        in_specs=[col(0), col(1), col(2), pl.BlockSpec((3, CONV_W), lambda i: (0, 0))],
        out_specs=pl.BlockSpec((S, CONV_W), lambda i: (0, 0)),
        out_shape=_sds((S, D_MIX), BF), compiler_params=_params("arbitrary"),
    )(proj, proj, proj, cw)


def conv_bwd(name, dy, proj, cw):
    S = proj.shape[0]

    def body(dy_ref, ab_ref, ac_ref, ah_ref, cw_ref, dab_ref, dac_ref, dah_ref, dcw_ref):
        w = cw_ref[...]
        a_c, a_h, x, x1, x2, cv, row = _conv_parts(ac_ref, ah_ref, w)
        d = dy_ref[...].astype(F32)
        dab_ref[...] = (d * cv).astype(BF)
        dcv = d * ab_ref[...].astype(F32)
        d1 = jnp.where(row < S - 1, pltpu.roll(dcv, S - 1, 0), 0.0)
        d2 = jnp.where(row < S - 2, pltpu.roll(dcv, S - 2, 0), 0.0)
        dx = w[2:3] * dcv + w[1:2] * d1 + w[0:1] * d2
        dac_ref[...] = (dx * a_h).astype(BF)
        dah_ref[...] = (dx * a_c).astype(BF)
        dcw_ref[0:1, :] = jnp.sum(dcv * x2, axis=0, keepdims=True)
        dcw_ref[1:2, :] = jnp.sum(dcv * x1, axis=0, keepdims=True)
        dcw_ref[2:3, :] = jnp.sum(dcv * x, axis=0, keepdims=True)

    col = lambda j: pl.BlockSpec((S, CONV_W), lambda i, j=j: (0, j))
    one = pl.BlockSpec((S, CONV_W), lambda i: (0, 0))
    small = pl.BlockSpec((3, CONV_W), lambda i: (0, 0))
    return pl.pallas_call(
        body, name=name, grid=(1,),
        in_specs=[col(0), col(0), col(1), col(2), small],
        out_specs=[one, one, one, small],
        out_shape=[_sds((S, CONV_W), BF)] * 3 + [_sds((3, CONV_W), F32)],
        compiler_params=_params("arbitrary"),
    )(dy, proj, proj, proj, cw)


SGU_HEADS = SGU_W // HEAD_DIM
CU_BLOCK = 2304 // SGU_W
CV_BLOCK = 2560 // SGU_W


def _sgu_common(cu_ref, cv_ref, gv_ref, tm):
    c_u = cu_ref[...].astype(F32)
    c_v = cv_ref[...].astype(F32)
    hm = _head_mean_matrix(SGU_W)
    u = _gelu(c_u)
    vg = _gelu(c_v)
    r = lax.rsqrt(_head_mean(vg * vg, hm) + EPS)
    vv = (vg * r) * gv_ref[...]
    head = lax.broadcasted_iota(jnp.int32, (CHUNK, SGU_W), 1) // HEAD_DIM
    tri = (lax.broadcasted_iota(jnp.int32, (CHUNK, CHUNK), 0) >=
           lax.broadcasted_iota(jnp.int32, (CHUNK, CHUNK), 1))
    return c_u, c_v, hm, u, vg, r, vv, head, tri


def _sgu_mix(w_ref, tri, head, vvc, bias):
    sv = bias
    for g in range(SGU_HEADS):
        wg = jnp.where(tri, w_ref[g], 0.0).astype(BF)
        sv = sv + jnp.where(head == g, jnp.dot(wg, vvc, preferred_element_type=F32), 0.0)
    return sv


def sgu_fwd(name, proj, gv, w, bias, y):
    S = proj.shape[0]
    tm = min(S, 512)

    def body(cu_ref, cv_ref, gv_ref, w_ref, b_ref, y_ref, o_ref):
        _, _, _, u, _, _, vv, head, tri = _sgu_common(cu_ref, cv_ref, gv_ref, tm)
        vvb = vv.astype(BF)
        for ch in range(tm // CHUNK):
            rows = slice(ch * CHUNK, (ch + 1) * CHUNK)
            sv = _sgu_mix(w_ref, tri, head, vvb[rows], b_ref[...])
            o_ref[rows, :] = (u[rows] * sv).astype(BF)

    const = lambda shape: pl.BlockSpec(shape, lambda i: (0,) * len(shape))
    return pl.pallas_call(
        body, name=name, grid=(S // tm,),
        in_specs=[pl.BlockSpec((tm, SGU_W), lambda i: (i, CU_BLOCK)), pl.BlockSpec((tm, SGU_W), lambda i: (i, CV_BLOCK)),
                  const((1, SGU_W)), const((SGU_HEADS, CHUNK, CHUNK)), const((CHUNK, SGU_W)),
                  pl.BlockSpec(memory_space=pl.ANY)],
        out_specs=pl.BlockSpec((tm, SGU_W), lambda i: (i, (CONV_W + ATTN_W) // SGU_W)),
        out_shape=_sds(y.shape, BF), input_output_aliases={5: 0}, compiler_params=_params("parallel"),
    )(proj, proj, gv, w, bias, y)


def sgu_bwd(name, dy, proj, gv, w, bias):
    S = proj.shape[0]
    tm = min(S, 512)

    def body(dy_ref, cu_ref, cv_ref, gv_ref, w_ref, b_ref, dcu_ref, dcv_ref, dw_ref, db_ref, dgv_ref, dvv_s):
        i = pl.program_id(0)
        c_u, c_v, hm, u, vg, r, vv, head, tri = _sgu_common(cu_ref, cv_ref, gv_ref, tm)
        vvb = vv.astype(BF)
        d = dy_ref[...].astype(F32)
        ind = (lax.broadcasted_iota(jnp.int32, (SGU_W, LANES), 0) // HEAD_DIM ==
               lax.broadcasted_iota(jnp.int32, (SGU_W, LANES), 1)).astype(BF)
        dw_acc = [jnp.zeros((CHUNK, CHUNK), F32) for _ in range(SGU_HEADS)]
        db_acc = jnp.zeros((CHUNK, LANES), F32)
        for ch in range(tm // CHUNK):
            rows = slice(ch * CHUNK, (ch + 1) * CHUNK)
            sv = _sgu_mix(w_ref, tri, head, vvb[rows], b_ref[...])
            dcu_ref[rows, :] = (d[rows] * sv * _gelu_grad(c_u[rows])).astype(BF)
            dsv = d[rows] * u[rows]
            dsv_hi = dsv.astype(BF)
            db_acc = db_acc + _dot2_stacked(dsv_hi, (dsv - dsv_hi.astype(F32)).astype(BF), ind)
            dvv = jnp.zeros((CHUNK, SGU_W), F32)
            for g in range(SGU_HEADS):
                dsv_g = jnp.where(head == g, dsv, 0.0).astype(BF)
                wg = jnp.where(tri, w_ref[g], 0.0).astype(BF)
                dvv = dvv + lax.dot_general(wg, dsv_g, TN_DIMS, preferred_element_type=F32)
                dw_acc[g] = dw_acc[g] + lax.dot_general(dsv_g, vvb[rows], NT_DIMS, preferred_element_type=F32)
            dvv_s[rows, :] = dvv
        dvv = dvv_s[...]
        gvv = gv_ref[...]
        t = dvv * gvv
        dvg = r * t - vg * (r * r * r) * _head_mean(t * vg, hm)
        dcv_ref[...] = (dvg * _gelu_grad(c_v)).astype(BF)
        dgv = jnp.sum(dvv * (vg * r), axis=0, keepdims=True)

        @pl.when(i == 0)
        def _():
            for g in range(SGU_HEADS):
                dw_ref[g] = jnp.where(tri, dw_acc[g], 0.0)
            db_ref[...] = db_acc
            dgv_ref[...] = dgv

        @pl.when(i > 0)
        def _():
            for g in range(SGU_HEADS):
                dw_ref[g] += jnp.where(tri, dw_acc[g], 0.0)
            db_ref[...] += db_acc
            dgv_ref[...] += dgv

    const = lambda shape: pl.BlockSpec(shape, lambda i: (0,) * len(shape))
    tile = pl.BlockSpec((tm, SGU_W), lambda i: (i, 0))
    return pl.pallas_call(
        body, name=name, grid=(S // tm,),
        in_specs=[pl.BlockSpec((tm, SGU_W), lambda i: (i, 3)),
                  pl.BlockSpec((tm, SGU_W), lambda i: (i, CU_BLOCK)), pl.BlockSpec((tm, SGU_W), lambda i: (i, CV_BLOCK)),
                  const((1, SGU_W)), const((SGU_HEADS, CHUNK, CHUNK)), const((CHUNK, SGU_W))],
        out_specs=[tile, tile, const((SGU_HEADS, CHUNK, CHUNK)), const((CHUNK, LANES)), const((1, SGU_W))],
        out_shape=[_sds((S, SGU_W), BF), _sds((S, SGU_W), BF), _sds((SGU_HEADS, CHUNK, CHUNK), F32),
                   _sds((CHUNK, LANES), F32), _sds((1, SGU_W), F32)],
        scratch_shapes=[pltpu.VMEM((tm, SGU_W), F32)],
        compiler_params=_params("arbitrary"),
    )(dy, proj, proj, gv, w, bias)


HEAD_PAIRS = ATTN_W // LANES
Q_BLOCK0 = 768 // LANES
K_BLOCK0 = 1280 // LANES
V_BLOCK0 = 1792 // LANES


def _attn_tile(S):
    return min(S, 256)


def _qk_norm(x, g, hm):
    r = lax.rsqrt(_head_mean(x * x, hm) + EPS)
    return r, (x * r) * g


MASKED = -1e30


def _logit_parts(z):
    lb = _log_sigmoid(z)
    lr = lb - z
    hi = lr.astype(BF)
    return lb, hi, (lr - hi.astype(F32)).astype(BF)


def _stack_heads(x, lane):
    return jnp.concatenate([jnp.where(lane < HEAD_DIM, x, 0.0), jnp.where(lane >= HEAD_DIM, x, 0.0)],
                           axis=0).astype(BF)


def _dot2_stacked(hi, lo, u):
    rows = hi.shape[0]
    both = jnp.dot(jnp.concatenate([hi, lo], axis=0), u, preferred_element_type=F32)
    return both[:rows] + both[rows:]


def attn_fwd(name, proj, gq, gk, y):
    S = proj.shape[0]
    T = _attn_tile(S)
    nq = S // T

    def body(q_ref, k_ref, v_ref, gq_ref, gk_ref, y_ref, o_ref, tot_ref, kn_s, lb_s, hi_s, lo_s, z_s, a_s, o_s):
        qi = pl.program_id(1)
        hm = _head_mean_matrix(LANES)

        @pl.when(qi == 0)
        def _():
            kn_s[...] = _qk_norm(k_ref[...].astype(F32), gk_ref[...], hm)[1].astype(BF)

        qn = _qk_norm(q_ref[...].astype(F32), gq_ref[...], hm)[1]
        lane = lax.broadcasted_iota(jnp.int32, (T, LANES), 1)
        qst = _stack_heads(qn, lane)
        rowi = lax.broadcasted_iota(jnp.int32, (T, T), 0)
        coli = lax.broadcasted_iota(jnp.int32, (T, T), 1)
        u_excl = (rowi > coli).astype(BF)
        diagonal = jnp.where(coli < rowi, 0.0, MASKED)

        def logits(j):
            return lax.dot_general(qst, kn_s[pl.ds(pl.multiple_of(j * T, T), T), :], NT_DIMS,
                                   preferred_element_type=F32)

        def values(j):
            return v_ref[pl.ds(pl.multiple_of(j * T, T), T), :].astype(BF)

        def keep(slot, z):
            lb_s[slot], hi_s[...], lo_s[...] = _logit_parts(z)

        def step(it, carry):
            run = carry
            j = qi - it
            hi, lo = hi_s[...], lo_s[...]
            both = jnp.dot(jnp.concatenate([hi, lo], axis=0), u_excl, preferred_element_type=F32)
            o_s[...] += jnp.dot(a_s[...], values(jnp.minimum(j + 1, qi)), preferred_element_type=F32)
            z_after = logits(jnp.maximum(j - 2, 0))
            first = hi[:, 0:1].astype(F32) + lo[:, 0:1].astype(F32)
            keep((it + 1) % 2, z_s[...])
            later = both[:2 * T] + both[2 * T:]
            a_s[...] = jnp.exp(lb_s[it % 2] + later + run).astype(BF)
            z_s[...] = z_after
            return run + later[:, 0:1] + first

        keep(0, logits(qi) + jnp.concatenate([diagonal, diagonal], axis=0))
        z_s[...] = logits(jnp.maximum(qi - 1, 0))
        a_s[...] = jnp.zeros_like(a_s)
        o_s[...] = jnp.zeros_like(o_s)
        run = lax.fori_loop(0, qi + 1, step, jnp.zeros((2 * T, 1), F32))
        o = o_s[...] + jnp.dot(a_s[...], values(0), preferred_element_type=F32)
        o_ref[...] = jnp.where(lane < HEAD_DIM, o[:T], o[T:]).astype(BF)
        tot_ref[...] = jnp.where(lane < HEAD_DIM, run[:T], run[T:])

    gain = pl.BlockSpec((1, LANES), lambda hp, qi: (0, 0))
    full = lambda b0: pl.BlockSpec((S, LANES), lambda hp, qi, b0=b0: (0, b0 + hp))
    tile = pl.BlockSpec((T, LANES), lambda hp, qi: (qi, hp))
    return pl.pallas_call(
        body, name=name, grid=(HEAD_PAIRS, nq),
        in_specs=[pl.BlockSpec((T, LANES), lambda hp, qi: (qi, Q_BLOCK0 + hp)), full(K_BLOCK0), full(V_BLOCK0), gain, gain,
                  pl.BlockSpec(memory_space=pl.ANY)],
        out_specs=[pl.BlockSpec((T, LANES), lambda hp, qi: (qi, CONV_W // LANES + hp)), tile],
        out_shape=[_sds(y.shape, BF), _sds((S, ATTN_W), F32)], input_output_aliases={5: 0},
        scratch_shapes=[pltpu.VMEM((S, LANES), BF), pltpu.VMEM((2, 2 * T, T), F32), pltpu.VMEM((2 * T, T), BF),
                        pltpu.VMEM((2 * T, T), BF), pltpu.VMEM((2 * T, T), F32), pltpu.VMEM((2 * T, T), BF),
                        pltpu.VMEM((2 * T, LANES), F32)],
        compiler_params=_params("arbitrary", "arbitrary"),
    )(proj, proj, proj, gq, gk, y)


def attn_bwd(name, dy, proj, tot, gq, gk):
    S = proj.shape[0]
    T = _attn_tile(S)
    nq = S // T

    def body(q_ref, k_ref, v_ref, tot_ref, do_ref, gq_ref, gk_ref,
             dq_ref, dk_ref, dv_ref, dgq_ref, dgk_ref, kn_s, dkn_s, dv_s,
             lb_s, z_s, g_s, dq_s, hi_s, lo_s, a_s, ghi_s, glo_s):
        hp = pl.program_id(0)
        qi = pl.program_id(1)
        hm = _head_mean_matrix(LANES)

        @pl.when(qi == 0)
        def _():
            kn_s[...] = _qk_norm(k_ref[...].astype(F32), gk_ref[...], hm)[1].astype(BF)
            dkn_s[...] = jnp.zeros_like(dkn_s)
            dv_s[...] = jnp.zeros_like(dv_s)

        q = q_ref[...].astype(F32)
        rq, qn = _qk_norm(q, gq_ref[...], hm)
        lane = lax.broadcasted_iota(jnp.int32, (T, LANES), 1)
        qst = _stack_heads(qn, lane)
        dost = _stack_heads(do_ref[...].astype(F32), lane)
        total = jnp.concatenate([tot_ref[:, 0:1], tot_ref[:, HEAD_DIM:HEAD_DIM + 1]], axis=0)
        rowi = lax.broadcasted_iota(jnp.int32, (T, T), 0)
        coli = lax.broadcasted_iota(jnp.int32, (T, T), 1)
        u_upto = (rowi <= coli).astype(BF)
        u_before = (rowi < coli).astype(BF)
        diagonal = jnp.where(coli < rowi, 0.0, MASKED)

        on_diagonal = jnp.concatenate([diagonal, diagonal], axis=0)

        def rows(b):
            return pl.ds(pl.multiple_of(jnp.clip(b, 0, qi) * T, T), T)

        def logits(b):
            return lax.dot_general(qst, kn_s[rows(b), :], NT_DIMS, preferred_element_type=F32)

        def keep(b, z):
            bias = jnp.where(b == qi, on_diagonal, jnp.where(b > qi, MASKED, 0.0))
            lb_s[b % 3], hi_s[...], lo_s[...] = _logit_parts(z + bias)

        def step(i, carry):
            run, grun = carry
            both_before = jnp.dot(jnp.concatenate([ghi_s[...], glo_s[...]], axis=0), u_before,
                                  preferred_element_type=F32)
            both_upto = jnp.dot(jnp.concatenate([hi_s[...], lo_s[...]], axis=0), u_upto, preferred_element_type=F32)
            da = lax.dot_general(dost, v_ref[rows(i), :].astype(BF), NT_DIMS, preferred_element_type=F32)
            dv_s[rows(i - 1), :] += lax.dot_general(a_s[...], dost, TN_DIMS, preferred_element_type=F32)
            z_after = logits(i + 2)

            keep(i + 1, z_s[...])

            g = g_s[...]
            before = both_before[:2 * T] + both_before[2 * T:]
            dz = (g - jnp.exp(lb_s[(i + 2) % 3]) * (g + (grun + before))).astype(BF)
            dq_s[...] += jnp.dot(dz, kn_s[rows(i - 1), :], preferred_element_type=F32)
            dkn_s[rows(i - 1), :] += lax.dot_general(dz, qst, TN_DIMS, preferred_element_type=F32)
            grun = grun + before[:, T - 1:T] + g[:, T - 1:T]

            upto = both_upto[:2 * T] + both_upto[2 * T:]
            a = jnp.exp(lb_s[i % 3] + (total - run - upto))
            g = da * a
            a_s[...] = a.astype(BF)
            g_s[...] = g
            ghi = g.astype(BF)
            ghi_s[...] = ghi
            glo_s[...] = (g - ghi.astype(F32)).astype(BF)
            z_s[...] = z_after
            return run + upto[:, T - 1:T], grun

        lb_s[...] = jnp.full(lb_s.shape, MASKED, F32)
        for ref in (a_s, g_s, ghi_s, glo_s, dq_s):
            ref[...] = jnp.zeros_like(ref)
        keep(0, logits(0))
        z_s[...] = logits(1)
        lax.fori_loop(0, qi + 2, step, (jnp.zeros((2 * T, 1), F32), jnp.zeros((2 * T, 1), F32)))
        dqn = jnp.where(lane < HEAD_DIM, dq_s[:T, :], dq_s[T:, :])
        gq_v = gq_ref[...]
        t = dqn * gq_v
        dq_ref[...] = (rq * t - q * (rq * rq * rq) * _head_mean(t * q, hm)).astype(BF)
        dgq = jnp.sum(dqn * (q * rq), axis=0, keepdims=True) * SCALE
        first = jnp.logical_and(hp == 0, qi == 0)

        @pl.when(first)
        def _():
            dgq_ref[...] = dgq

        @pl.when(jnp.logical_not(first))
        def _():
            dgq_ref[...] += dgq

        @pl.when(qi == nq - 1)
        def _():
            k = k_ref[...].astype(F32)
            rk = _qk_norm(k, gk_ref[...], hm)[0]
            dkn = dkn_s[...]
            tk = dkn * gk_ref[...]
            dk_ref[...] = (rk * tk - k * (rk * rk * rk) * _head_mean(tk * k, hm)).astype(BF)
            dgk = jnp.sum(dkn * (k * rk), axis=0, keepdims=True)
            dv_ref[...] = dv_s[...].astype(BF)

            @pl.when(hp == 0)
            def _():
                dgk_ref[...] = dgk

            @pl.when(hp > 0)
            def _():
                dgk_ref[...] += dgk

            @pl.when(hp == HEAD_PAIRS - 1)
            def _():
                fold = (lax.broadcasted_iota(jnp.int32, (LANES, LANES), 0) % HEAD_DIM ==
                        lax.broadcasted_iota(jnp.int32, (LANES, LANES), 1) % HEAD_DIM).astype(F32)
                dgq_ref[...] = jnp.dot(dgq_ref[...], fold, precision=HIGHEST, preferred_element_type=F32)
                dgk_ref[...] = jnp.dot(dgk_ref[...], fold, precision=HIGHEST, preferred_element_type=F32)

    gain = pl.BlockSpec((1, LANES), lambda hp, qi: (0, 0))
    full = lambda b0: pl.BlockSpec((S, LANES), lambda hp, qi, b0=b0: (0, b0 + hp))
    tile = pl.BlockSpec((T, LANES), lambda hp, qi: (qi, hp))
    col = pl.BlockSpec((S, LANES), lambda hp, qi: (0, hp))
    dgain = pl.BlockSpec((1, LANES), lambda hp, qi: (0, 0))
    return pl.pallas_call(
        body, name=name, grid=(HEAD_PAIRS, nq),
        in_specs=[pl.BlockSpec((T, LANES), lambda hp, qi: (qi, Q_BLOCK0 + hp)), full(K_BLOCK0), full(V_BLOCK0),
                  tile, pl.BlockSpec((T, LANES), lambda hp, qi: (qi, 2 + hp)), gain, gain],
        out_specs=[tile, col, col, dgain, dgain],
        out_shape=[_sds((S, ATTN_W), BF)] * 3 + [_sds((1, LANES), F32)] * 2,
        scratch_shapes=[pltpu.VMEM((S, LANES), BF), pltpu.VMEM((S, LANES), F32), pltpu.VMEM((S, LANES), F32),
                        pltpu.VMEM((3, 2 * T, T), F32), pltpu.VMEM((2 * T, T), F32), pltpu.VMEM((2 * T, T), F32),
                        pltpu.VMEM((2 * T, LANES), F32)] + [pltpu.VMEM((2 * T, T), BF)] * 5,
        compiler_params=_params("arbitrary", "arbitrary"),
    )(proj, proj, proj, tot, dy, gq, gk)


def ple_bwd(name, dh, gp, pp, w, h, g, after=()):
    S, D = dh.shape
    tm = min(S, 256)

    def body(dh_ref, gp_ref, pp_ref, w_ref, h_ref, g_ref, *rest):
        dgp_ref, dpp_ref, dh2_ref, dg_ref = rest[-4:]
        i = pl.program_id(0)
        d = dh_ref[...]
        gate = jax.nn.sigmoid(gp_ref[...].astype(F32))
        dpp_ref[...] = (d * gate).astype(BF)
        dgp = (d * pp_ref[...].astype(F32) * gate * (1.0 - gate)).astype(BF)
        dgp_ref[...] = dgp
        dyn = lax.dot_general(dgp, w_ref[...], NT_DIMS, preferred_element_type=F32)
        x = h_ref[...]
        r = lax.rsqrt(jnp.mean(x * x, axis=-1, keepdims=True) + EPS)
        t = dyn * g_ref[...]
        dh2_ref[...] = d + r * t - x * (r * r * r) * jnp.mean(t * x, axis=-1, keepdims=True)
        part = jnp.sum(dyn * (x * r), axis=0, keepdims=True)

        @pl.when(i == 0)
        def _():
            dg_ref[...] = part

        @pl.when(i > 0)
        def _():
            dg_ref[...] += part

    tile = pl.BlockSpec((tm, D), lambda i: (i, 0))
    vec = pl.BlockSpec((1, D), lambda i: (0, 0))
    return pl.pallas_call(
        body, name=name, grid=(S // tm,),
        in_specs=[tile, tile, tile, pl.BlockSpec((D, D), lambda i: (0, 0)), tile, vec] +
                 [pl.BlockSpec(memory_space=pl.ANY)] * len(after),
        out_specs=[tile, tile, tile, vec],
        out_shape=[_sds((S, D), BF), _sds((S, D), BF), _sds((S, D), F32), _sds((1, D), F32)],
        compiler_params=_params("arbitrary"),
    )(dh, gp, pp, w, h, g.reshape(1, D), *after)


def loss_head(name, h, target):
    S, D = h.shape
    tm = min(S, 512)

    def body(h_ref, t_ref, loss_ref, dh_ref):
        i = pl.program_id(0)
        e = h_ref[...] - t_ref[...]
        dh_ref[...] = e * (1.0 / D)
        part = jnp.zeros((8, LANES), F32) + 0.5 * jnp.sum(jnp.mean(e * e, axis=-1, keepdims=True))

        @pl.when(i == 0)
        def _():
            loss_ref[...] = part

        @pl.when(i > 0)
        def _():
            loss_ref[...] += part

    tile = pl.BlockSpec((tm, D), lambda i: (i, 0))
    return pl.pallas_call(
        body, name=name, grid=(S // tm,), in_specs=[tile, tile],
        out_specs=[pl.BlockSpec((8, LANES), lambda i: (0, 0)), tile],
        out_shape=[_sds((8, LANES), F32), _sds((S, D), F32)], compiler_params=_params("arbitrary"),
    )(h, target)


def _adamw_math(w, g, m, v):
    c1 = 1.0 - ADAM_B1 ** ADAM_STEP
    c2 = 1.0 - ADAM_B2 ** ADAM_STEP
    nm = ADAM_B1 * m + (1.0 - ADAM_B1) * g
    nv = ADAM_B2 * v + (1.0 - ADAM_B2) * (g * g)
    return -ADAM_LR * ((nm / c1) / (jnp.sqrt(nv / c2) + ADAM_EPS) + ADAM_WD * w), nm, nv


def adamw(name, w, g, m, v):
    R, C = w.shape
    tr = R
    for cand in (512, 256, 128, 64, 32, 16, 8):
        if R % cand == 0:
            tr = cand
            break

    def body(w_ref, g_ref, m_ref, v_ref, d_ref, nm_ref, nv_ref):
        d_ref[...], nm_ref[...], nv_ref[...] = _adamw_math(w_ref[...], g_ref[...], m_ref[...], v_ref[...])

    tile = pl.BlockSpec((tr, C), lambda i: (i, 0))
    return pl.pallas_call(
        body, name=name, grid=(R // tr,), in_specs=[tile] * 4, out_specs=[tile] * 3,
        out_shape=[_sds((R, C), F32)] * 3, compiler_params=_params("parallel"),
    )(w, g, m, v)


def adamw_layer(name, layer, ws, gs, ms, vs, prev, after=()):
    n = len(ws)
    steps = 8

    def body(*refs):
        ins, outs = refs[:4 * n], refs[-4 * n:]
        for i in range(n):
            w_ref, g_ref, m_ref, v_ref = (ins[k * n + i] for k in range(4))
            g = g_ref[...]
            outs[i][...] = g
            outs[n + i][...], outs[2 * n + i][...], outs[3 * n + i][...] = _adamw_math(w_ref[...], g, m_ref[...],
                                                                                        v_ref[...])

    def stacked(a):
        return pl.BlockSpec((None, a.shape[1] // steps, a.shape[2]), lambda t: (layer, t, 0))

    def flat(a):
        return pl.BlockSpec((a.shape[0] // steps, a.shape[1]), lambda t: (t, 0))

    in_specs = [stacked(a) for a in ws] + [flat(a) for a in gs] + [stacked(a) for a in ms] + [stacked(a) for a in vs]
    operands = [*ws, *gs, *ms, *vs]
    aliases = {}
    if prev is not None:
        flat_prev = [a for group in prev for a in group]
        in_specs += [pl.BlockSpec(memory_space=pl.ANY)] * len(flat_prev)
        aliases = {4 * n + i: i for i in range(4 * n)}
        operands += flat_prev
    in_specs += [pl.BlockSpec(memory_space=pl.ANY)] * len(after)
    operands += list(after)
    out = pl.pallas_call(
        body, name=name, grid=(steps,), in_specs=in_specs, out_specs=[stacked(a) for a in ws] * 4,
        out_shape=[_sds(a.shape, F32) for a in ws] * 4, input_output_aliases=aliases,
        compiler_params=_params("parallel"),
    )(*operands)
    return [list(out[k * n:(k + 1) * n]) for k in range(4)]


def _relu2(u):
    r = jnp.maximum(u.astype(F32), 0.0)
    return r * r


def layer_fwd(tag, h0, p_bf, wt, after=(), mid=None):
    hn1, proj = norm_mm(f"{tag}_proj", h0, wt["norm1_g"], wt["w_in"], tn=256, after=after)
    y = conv_fwd(f"{tag}_conv", proj, wt["conv_w"])
    y, yb_tot = attn_fwd(f"{tag}_attn", proj, wt["gq"], wt["gk"], y)
    y = sgu_fwd(f"{tag}_sgu", proj, wt["gv"], wt["sgu_w"], wt["sgu_bias"], y)
    h1 = mm_nn(f"{tag}_out", y, wt["w_out"], extras=(h0,), epi=lambda acc, h: (h + acc,), out_dtypes=(F32,))
    hn2, uu = norm_mm(f"{tag}_ff1", h1, wt["norm2_g"], wt["w_ff1"], after=() if mid is None else mid(yb_tot))
    h2 = mm_nn(f"{tag}_ff2", uu, wt["w_ff2"], pro=_relu2, extras=(h1,), epi=lambda acc, h: (h + acc,),
               out_dtypes=(F32,), tm=512)
    hn3, gp = norm_mm(f"{tag}_gate", h2, wt["norm3_g"], wt["w_ple_gate"])
    h3, pp = mm_nn(f"{tag}_ple", p_bf, wt["w_ple_proj"], extras=(gp, h2),
                   epi=lambda acc, g, h: (h + jax.nn.sigmoid(g.astype(F32)) * acc, acc), out_dtypes=(F32, BF))
    saved = dict(h0=h0, h1=h1, h2=h2, hn1=hn1, hn2=hn2, hn3=hn3, proj=proj, yb_tot=yb_tot, y=y, uu=uu, gp=gp, pp=pp,
                 p_bf=p_bf)
    return h3, saved


def layer_bwd(tag, dh3, sv, wt, after=(), mid=None):
    g = {}
    dgp, dpp, dh2, g["norm3_g"] = ple_bwd(f"{tag}_dple", dh3, sv["gp"], sv["pp"], wt["w_ple_gate"], sv["h2"],
                                          wt["norm3_g"], after)
    g["w_ple_proj"] = mm_tn(f"{tag}_dwp", sv["p_bf"], dpp, col_blocks=N_CHIPS)
    g["w_ple_gate"] = mm_tn(f"{tag}_dwg", sv["hn3"], dgp)

    duu = mm_nt(f"{tag}_dff2", dh2, wt["w_ff2"], extras=(sv["uu"],),
                epi=lambda acc, u: acc * (2.0 * jnp.maximum(u.astype(F32), 0.0)))
    g["w_ff2"] = mm_tn(f"{tag}_dw2", sv["uu"], dh2, pro_x=_relu2)
    g["w_ff1"] = mm_tn(f"{tag}_dw1", sv["hn2"], duu, col_blocks=N_CHIPS)
    dh1, g["norm2_g"] = mm_nt_rmsbwd(f"{tag}_dnorm2", duu, wt["w_ff1"], sv["h1"], wt["norm2_g"], dh2)

    dy = mm_nt(f"{tag}_dout", dh1, wt["w_out"], after=() if mid is None else mid(dh1))
    g["w_out"] = mm_tn(f"{tag}_dwo", sv["y"], dh1)
    dab, dac, dah, g["conv_w"] = conv_bwd(f"{tag}_dconv", dy, sv["proj"], wt["conv_w"])
    dq, dk, dv, g["gq"], g["gk"] = attn_bwd(f"{tag}_dattn", dy, sv["proj"], sv["yb_tot"], wt["gq"], wt["gk"])
    dcu, dcv, g["sgu_w"], g["sgu_bias"], g["gv"] = sgu_bwd(f"{tag}_dsgu", dy, sv["proj"], wt["gv"], wt["sgu_w"],
                                                           wt["sgu_bias"])
    dproj = jnp.concatenate([dab, dac, dah, dq, dk, dv, dcu, dcv], axis=-1)
    g["w_in"] = mm_tn(f"{tag}_dwi", sv["hn1"], dproj, tn=1408)
    dh0, g["norm1_g"] = mm_nt_rmsbwd(f"{tag}_dnorm1", dproj, wt["w_in"], sv["h0"], wt["norm1_g"], dh1)
    return dh0, g


def prep_small(norm1_g, q_norm_g, k_norm_g, sgu_norm_g, sgu_w, sgu_b, norm2_g, norm3_g, conv_w_full):
    return dict(
        norm1_g=norm1_g, norm2_g=norm2_g, norm3_g=norm3_g, conv_w=conv_w_full,
        gq=(jnp.tile(q_norm_g, 2) * SCALE).reshape(1, LANES), gk=jnp.tile(k_norm_g, 2).reshape(1, LANES),
        gv=sgu_norm_g.reshape(1, SGU_W), sgu_w=sgu_w, sgu_bias=jnp.repeat(sgu_b.T, HEAD_DIM, axis=1))


def small_grads(g):
    return dict(
        norm1_g=g["norm1_g"][0], norm2_g=g["norm2_g"][0], norm3_g=g["norm3_g"][0], conv_w=g["conv_w"],
        q_norm_g=g["gq"][0, :HEAD_DIM], k_norm_g=g["gk"][0, :HEAD_DIM], sgu_norm_g=g["gv"][0], sgu_w=g["sgu_w"],
        sgu_b=g["sgu_bias"][:, :SGU_HEADS].T)


HBM_SPEC = pl.BlockSpec(memory_space=pltpu.HBM)
BIG = ("w_in", "w_out", "w_ff1", "w_ff2", "w_ple_gate", "w_ple_proj")


def _mesh_pos():
    return lax.axis_index("x"), lax.axis_index("y"), lax.axis_index("c")


def _other_chips(x, y):
    return [(1 - x, y), (x, 1 - y), (1 - x, 1 - y)]


def _half(rows, core):
    h = rows // 2
    return pl.ds(pl.multiple_of(core * h, 16), h)


def _remote(src, dst, send_sems, recv_sems, k, to):
    return pltpu.make_async_remote_copy(src_ref=src, dst_ref=dst, send_sem=send_sems.at[k], recv_sem=recv_sems.at[k],
                                        device_id=to, device_id_type=MESH)


SEM_SPEC = pl.BlockSpec(memory_space=pltpu.SEMAPHORE)
ANY_SPEC = pl.BlockSpec(memory_space=pl.ANY)
SIDE_EFFECT = pltpu.SideEffectType.DATAFLOW_SIDE_EFFECTING


def _in_hbm(arrays):
    return [pltpu.with_memory_space_constraint(a, pltpu.HBM) for a in arrays]


def copies_start(name, srcs, lands, plan, after=()):
    ns, nl, na = len(srcs), len(lands), len(after)

    def body(*refs):
        src_refs, land_refs = refs[:ns], refs[ns:ns + nl]
        send_sem, recv_sem = refs[ns + nl + na], refs[ns + nl + na + 1]
        token = refs[-1]
        for src, dst, dev in plan(src_refs, land_refs, *_mesh_pos()):
            pltpu.make_async_remote_copy(src_ref=src, dst_ref=dst, send_sem=send_sem, recv_sem=recv_sem,
                                         device_id=dev, device_id_type=MESH).start()
        token[...] = jnp.zeros_like(token)

    out = pl.pallas_call(
        body, name=name,
        in_specs=[HBM_SPEC] * (ns + nl) + [ANY_SPEC] * na,
        out_specs=(SEM_SPEC, SEM_SPEC, *[HBM_SPEC] * (ns + nl), pl.BlockSpec(memory_space=pltpu.VMEM)),
        out_shape=(pltpu.SemaphoreType.DMA(()), pltpu.SemaphoreType.DMA(()),
                   *[pltpu.HBM(a.shape, a.dtype) for a in (*srcs, *lands)], _sds((8, LANES), F32)),
        input_output_aliases={i: 2 + i for i in range(ns + nl)},
        compiler_params=pltpu.CompilerParams(has_side_effects=SIDE_EFFECT),
    )(*_in_hbm(srcs), *_in_hbm(lands), *after)
    return out[0], out[1], list(out[2:2 + ns]), list(out[2 + ns:2 + ns + nl]), out[-1]


def copies_wait(name, started, plan, after=()):
    send_sem, recv_sem, srcs, lands, _ = started
    ns, nl, na = len(srcs), len(lands), len(after)

    def body(*refs):
        src_refs, land_refs = refs[:ns], refs[ns:ns + nl]
        send_sem, recv_sem = refs[ns + nl], refs[ns + nl + 1]
        for src, dst, dev in plan(src_refs, land_refs, *_mesh_pos()):
            cp = pltpu.make_async_remote_copy(src_ref=src, dst_ref=dst, send_sem=send_sem, recv_sem=recv_sem,
                                              device_id=dev, device_id_type=MESH)
            cp.wait_send()
            cp.wait_recv()

    out = pl.pallas_call(
        body, name=name,
        in_specs=[HBM_SPEC] * (ns + nl) + [SEM_SPEC, SEM_SPEC] + [ANY_SPEC] * na,
        out_specs=[HBM_SPEC] * (ns + nl),
        out_shape=[pltpu.HBM(a.shape, a.dtype) for a in (*srcs, *lands)],
        input_output_aliases={i: i for i in range(ns + nl)},
        compiler_params=pltpu.CompilerParams(has_side_effects=SIDE_EFFECT),
    )(*srcs, *lands, send_sem, recv_sem, *after)
    return list(out[:ns]), list(out[ns:])


def _gather_plan(srcs, lands, x, y, c):
    me = 2 * x + y
    return [(src.at[_half(src.shape[0], c)], land.at[me, _half(src.shape[0], c)], (*chip, c))
            for src, land in zip(srcs, lands) for chip in _other_chips(x, y)]


def _gather_arrivals(srcs, lands, x, y, c):
    return [(src.at[_half(src.shape[0], c)], land.at[2 * chip[0] + chip[1], _half(src.shape[0], c)], (*chip, c))
            for src, land in zip(srcs, lands) for chip in _other_chips(x, y)]


def _forward_plan(srcs, lands, x, y, c):
    me, sibling = 2 * x + y, (x, y, 1 - c)
    out = []
    for src, land in zip(srcs, lands):
        out.append((src, land.at[me], sibling))
        for chip in _other_chips(x, y):
            region = land.at[2 * chip[0] + chip[1], _half(src.shape[0], c)]
            out.append((region, region, sibling))
    return out


def _forward_arrivals(srcs, lands, x, y, c):
    me, sibling = 2 * x + y, (x, y, 1 - c)
    out = []
    for src, land in zip(srcs, lands):
        out.append((src, land.at[me], sibling))
        for chip in _other_chips(x, y):
            slot = land.at[2 * chip[0] + chip[1]]
            out.append((slot.at[_half(src.shape[0], c)], slot.at[_half(src.shape[0], 1 - c)], sibling))
    return out


def _join_plan(srcs, lands, x, y, c):
    return [(land.at[_half(land.shape[0], c)], land.at[_half(land.shape[0], c)], (x, y, 1 - c)) for land in lands]


def _join_arrivals(srcs, lands, x, y, c):
    return [(land.at[_half(land.shape[0], c)], land.at[_half(land.shape[0], 1 - c)], (x, y, 1 - c)) for land in lands]


def _pair_plan(srcs, lands, x, y, c):
    return [(src.at[:, _half(src.shape[1], 1 - c)], land, (x, y, 1 - c)) for src, land in zip(srcs, lands)]


def add_own_half(name, core, grads, got):
    n = len(grads)

    def body(core_ref, *refs):
        for i in range(n):
            refs[2 * n + i][...] = (refs[i][...].astype(F32) + refs[n + i][...].astype(F32)).astype(BF)

    def spec(g, own):
        blk = (None, g.shape[1] // 2, g.shape[2])
        return pl.BlockSpec(blk, (lambda j, core_ref: (j, core_ref[0], 0)) if own else (lambda j, core_ref: (j, 0, 0)))

    return pl.pallas_call(
        body, name=name,
        grid_spec=pltpu.PrefetchScalarGridSpec(
            num_scalar_prefetch=1, grid=(N_CHIPS,),
            in_specs=[spec(g, True) for g in grads] + [spec(g, False) for g in grads],
            out_specs=[spec(g, False) for g in grads]),
        out_shape=[_sds(r.shape, BF) for r in got], compiler_params=_params("parallel"),
    )(core, *grads, *got)


def _chips_plan(srcs, lands, x, y, c):
    return [(src.at[2 * chip[0] + chip[1]], land.at[k], (*chip, c))
            for src, land in zip(srcs, lands) for k, chip in enumerate(_other_chips(x, y))]


def sum_chips(name, place, parts, got):
    n = len(got)

    def body(place_ref, *refs):
        for i in range(n):
            acc = refs[i][...].astype(F32)
            for k in range(N_CHIPS - 1):
                acc = acc + refs[n + i][k].astype(F32)
            refs[2 * n + i][...] = acc

    steps = 2
    return pl.pallas_call(
        body, name=name,
        grid_spec=pltpu.PrefetchScalarGridSpec(
            num_scalar_prefetch=1, grid=(steps,),
            in_specs=[pl.BlockSpec((None, g.shape[1] // steps, g.shape[2]), lambda t, place_ref: (place_ref[0], t, 0))
                      for g in parts] +
                     [pl.BlockSpec((N_CHIPS - 1, g.shape[1] // steps, g.shape[2]), lambda t, place_ref: (0, t, 0))
                      for g in got],
            out_specs=[pl.BlockSpec((g.shape[1] // steps, g.shape[2]),
                                    lambda t, place_ref: (place_ref[1] * steps + t, 0)) for g in got]),
        out_shape=[_sds((2 * g.shape[1], g.shape[2]), F32) for g in got], compiler_params=_params("parallel"),
    )(place, *parts, *got)


def reduce_scatter_pair(tag, grads):
    lands = [lax.empty((N_CHIPS, g.shape[1] // 2, g.shape[2]), g.dtype) for g in grads]
    return copies_start(f"{tag}_rs_pair_start", grads, lands, _pair_plan)


def reduce_scatter_begin(tag, core, pairing, after):
    grads, got = copies_wait(f"{tag}_rs_pair_wait", pairing, _pair_plan, after)
    parts = add_own_half(f"{tag}_rs_add", core, grads, got)
    lands = [lax.empty((N_CHIPS - 1,) + p.shape[1:], p.dtype) for p in parts]
    return copies_start(f"{tag}_rs_start", parts, lands, _chips_plan)


def reduce_scatter_sum(tag, place, started, after):
    parts, got = copies_wait(f"{tag}_rs_wait", started, _chips_plan, after)
    return copies_start(f"{tag}_rs_join_start", [], sum_chips(f"{tag}_rs_sum", place, parts, got), _join_plan)


def reduce_scatter_end(tag, joining, after):
    return copies_wait(f"{tag}_rs_join_wait", joining, _join_arrivals, after)[1]


def small_allreduce(name, x):
    R = x.shape[0]
    H = R // 2

    def body(x_ref, o_ref, pair_ref, chip_ref, send_sems, recv_sems):
        xx, yy, c = _mesh_pos()
        me = 2 * xx + yy
        chips = _other_chips(xx, yy)
        sibling = (xx, yy, 1 - c)
        mine = pl.ds(pl.multiple_of(c * H, 8), H)
        theirs = pl.ds(pl.multiple_of((1 - c) * H, 8), H)
        a = _remote(x_ref.at[theirs], pair_ref.at[theirs], send_sems, recv_sems, 0, sibling)
        a.start()
        a.wait_send()
        _remote(x_ref.at[mine], pair_ref.at[mine], send_sems, recv_sems, 0, sibling).wait_recv()
        chip_ref[me] = x_ref[mine, :] + pair_ref[mine, :]
        sends = []
        for k, chip in enumerate(chips):
            cp = _remote(chip_ref.at[me], chip_ref.at[me], send_sems, recv_sems, 1 + k, (*chip, c))
            cp.start()
            sends.append(cp)
        for k, chip in enumerate(chips):
            slot = chip_ref.at[2 * chip[0] + chip[1]]
            _remote(slot, slot, send_sems, recv_sems, 1 + k, (*chip, c)).wait_recv()
        o_ref[mine, :] = (chip_ref[0] + chip_ref[1]) + (chip_ref[2] + chip_ref[3])
        b = _remote(o_ref.at[mine], o_ref.at[mine], send_sems, recv_sems, 4, sibling)
        b.start()
        b.wait_send()
        _remote(o_ref.at[theirs], o_ref.at[theirs], send_sems, recv_sems, 4, sibling).wait_recv()
        for cp in sends:
            cp.wait_send()

    return pl.pallas_call(
        body, name=name,
        in_specs=[pl.BlockSpec(memory_space=pltpu.VMEM)], out_specs=pl.BlockSpec(memory_space=pltpu.VMEM),
        out_shape=_sds((R, LANES), F32),
        scratch_shapes=[pltpu.VMEM((R, LANES), F32), pltpu.VMEM((N_CHIPS, H, LANES), F32),
                        pltpu.SemaphoreType.DMA((5,)), pltpu.SemaphoreType.DMA((5,))],
        compiler_params=pltpu.CompilerParams(vmem_limit_bytes=VMEM_LIMIT),
    )(x)


WEIGHTS = ("norm1_g", "w_in", "conv_w", "q_norm_g", "k_norm_g", "sgu_norm_g", "sgu_w", "sgu_b", "w_out", "norm2_g",
           "w_ff1", "w_ff2", "norm3_g", "w_ple_gate", "w_ple_proj")
SMALL = ("norm1_g", "norm2_g", "norm3_g", "q_norm_g", "k_norm_g", "sgu_norm_g", "sgu_w", "sgu_b", "conv_w")


PACK = 8 * LANES


def _pack_rows(arrays):
    blocks = []
    for a in arrays:
        v = a.reshape(-1)
        blocks.append(jnp.pad(v, (0, (-v.shape[0]) % PACK)).reshape(-1, LANES))
    rows = sum(b.shape[0] for b in blocks)
    if rows % 16:
        blocks.append(jnp.zeros((16 - rows % 16, LANES), F32))
    return jnp.concatenate(blocks, axis=0)


def _unpack_rows(packed, shapes):
    out, pos = [], 0
    flat = packed.reshape(-1)
    for shp in shapes:
        size = math.prod(shp)
        out.append(flat[pos:pos + size].reshape(shp))
        pos += size + (-size) % PACK
    return out


def kernel(x, p, norm1_g, w_in, conv_w, q_norm_g, k_norm_g, sgu_norm_g, sgu_w, sgu_b, w_out, norm2_g, w_ff1, w_ff2, norm3_g, w_ple_gate, w_ple_proj, loss_target, m_norm1_g, m_w_in, m_conv_w, m_q_norm_g, m_k_norm_g, m_sgu_norm_g, m_sgu_w, m_sgu_b, m_w_out, m_norm2_g, m_w_ff1, m_w_ff2, m_norm3_g, m_w_ple_gate, m_w_ple_proj, v_norm1_g, v_w_in, v_conv_w, v_q_norm_g, v_k_norm_g, v_sgu_norm_g, v_sgu_w, v_sgu_b, v_w_out, v_norm2_g, v_w_ff1, v_w_ff2, v_norm3_g, v_w_ple_gate, v_w_ple_proj):
    w = dict(norm1_g=norm1_g, w_in=w_in, conv_w=conv_w, q_norm_g=q_norm_g, k_norm_g=k_norm_g, sgu_norm_g=sgu_norm_g,
             sgu_w=sgu_w, sgu_b=sgu_b, w_out=w_out, norm2_g=norm2_g, w_ff1=w_ff1, w_ff2=w_ff2, norm3_g=norm3_g,
             w_ple_gate=w_ple_gate, w_ple_proj=w_ple_proj)
    m = dict(norm1_g=m_norm1_g, w_in=m_w_in, conv_w=m_conv_w, q_norm_g=m_q_norm_g, k_norm_g=m_k_norm_g,
             sgu_norm_g=m_sgu_norm_g, sgu_w=m_sgu_w, sgu_b=m_sgu_b, w_out=m_w_out, norm2_g=m_norm2_g, w_ff1=m_w_ff1,
             w_ff2=m_w_ff2, norm3_g=m_norm3_g, w_ple_gate=m_w_ple_gate, w_ple_proj=m_w_ple_proj)
    v = dict(norm1_g=v_norm1_g, w_in=v_w_in, conv_w=v_conv_w, q_norm_g=v_q_norm_g, k_norm_g=v_k_norm_g,
             sgu_norm_g=v_sgu_norm_g, sgu_w=v_sgu_w, sgu_b=v_sgu_b, w_out=v_w_out, norm2_g=v_norm2_g, w_ff1=v_w_ff1,
             w_ff2=v_w_ff2, norm3_g=v_norm3_g, w_ple_gate=v_w_ple_gate, w_ple_proj=v_w_ple_proj)
    depth = w_in.shape[0]
    d_model = x.shape[-1]
    chip = 2 * lax.axis_index("x") + lax.axis_index("y")
    core = lax.axis_index("c")
    core_arr = core.reshape(1).astype(jnp.int32)

    cw_cols = conv_w.shape[-1]
    placed = lax.dynamic_update_slice(jnp.zeros((depth, 3, CONV_W), F32), conv_w, (0, 0, chip * cw_cols))
    placed = jnp.where(core == 0, placed, 0.0)
    conv_full = _unpack_rows(small_allreduce("conv_w_gather", _pack_rows([placed])), [(depth, 3, CONV_W)])[0]

    h = x[0]
    p_bf = p[:, 0].astype(BF)
    saved, full = [], []

    def gather_start(l, after):
        shards = [w[n][l].astype(BF) for n in BIG]
        lands = [lax.empty((N_CHIPS,) + s.shape, BF) for s in shards]
        return copies_start(f"l{l}_gather_start", shards, lands, _gather_plan, after)

    def gather_forward(l, started, after):
        shards, lands = copies_wait(f"l{l}_gather_wait", started, _gather_arrivals, after)
        forwarding = copies_start(f"l{l}_forward_start", shards, lands, _forward_plan)
        return forwarding, gather_start(l + 1, (forwarding[-1],)) if l + 1 < depth else None

    forwarding, started = gather_forward(0, gather_start(0, ()), (h,))
    for l in range(depth):
        g_in, g_out, g_ff1, g_ff2, g_gate, g_proj = copies_wait(f"l{l}_forward_wait", forwarding, _forward_arrivals,
                                                                 (h,))[1]
        nxt = {}

        def mid(arr, l=l, started=started, nxt=nxt):
            nxt["forwarding"], nxt["started"] = gather_forward(l + 1, started, (arr,))
            return [t[-1] for t in nxt.values() if t is not None]

        wt = prep_small(norm1_g[l], q_norm_g[l], k_norm_g[l], sgu_norm_g[l], sgu_w[l], sgu_b[l], norm2_g[l], norm3_g[l],
                        conv_full[l])
        wt["w_in"] = jnp.transpose(g_in, (1, 0, 2)).reshape(d_model, -1)
        wt["w_out"] = g_out.reshape(-1, d_model)
        wt["w_ff1"] = g_ff1
        wt["w_ff2"] = g_ff2.reshape(-1, d_model)
        wt["w_ple_gate"] = g_gate.reshape(-1, d_model)
        wt["w_ple_proj"] = g_proj
        last = l + 1 == depth
        h, sv = layer_fwd(f"l{l}", h, p_bf[l], wt, () if last else (started[-1],), None if last else mid)
        if not last:
            forwarding, started = nxt["forwarding"], nxt["started"]
        saved.append(sv)
        full.append(wt)

    loss_tile, dh = loss_head("loss", h, loss_target[0])
    loss = lax.psum(loss_tile[0, 0], ("x", "y", "c"))

    small = [None] * depth
    chip_arr = jnp.stack([chip, core]).astype(jnp.int32)
    big_w, big_m, big_v = ([d[n] for n in BIG] for d in (w, m, v))
    pairing, joining, tokens = None, {}, ()
    for l in reversed(range(depth)):
        box = {}

        def mid(arr, l=l, pairing=pairing, box=box):
            box["started"] = reduce_scatter_begin(f"l{l + 1}", core_arr, pairing, (arr,))
            return (box["started"][-1],)

        dh, g = layer_bwd(f"l{l}", dh, saved[l], full[l], tokens, None if pairing is None else mid)
        tokens = ()
        if pairing is not None:
            joining[l + 1] = reduce_scatter_sum(f"l{l + 1}", chip_arr, box["started"], (dh,))
            tokens = (joining[l + 1][-1],)
        small[l] = small_grads(g)
        shards_in = w_in.shape[-1]
        gl = [jnp.transpose(g["w_in"].reshape(d_model, N_CHIPS, shards_in), (1, 0, 2)),
              g["w_out"].reshape(N_CHIPS, -1, d_model), g["w_ff1"], g["w_ff2"].reshape(N_CHIPS, -1, d_model),
              g["w_ple_gate"].reshape(N_CHIPS, -1, d_model), g["w_ple_proj"]]
        pairing = reduce_scatter_pair(f"l{l}", gl)
        tokens += (pairing[-1],)

    started = reduce_scatter_begin("l0", core_arr, pairing, (dh,))
    updated = None
    for l in reversed(range(1, depth)):
        reduced = reduce_scatter_end(f"l{l}", joining[l], (started[-1],))
        updated = adamw_layer(f"l{l}_adamw", l, big_w, reduced, big_m, big_v, updated)
    grads, delta, new_m, new_v = {}, {}, {}, {}
    packed = _pack_rows([small[l][n] for l in range(depth) for n in SMALL])
    shapes = [small[l][n].shape for l in range(depth) for n in SMALL]
    pieces = _unpack_rows(small_allreduce("small_grads", packed), shapes)
    for i, n in enumerate(SMALL):
        grads[n] = jnp.stack([pieces[l * len(SMALL) + i] for l in range(depth)])
    grads["conv_w"] = lax.dynamic_slice(grads["conv_w"], (0, 0, chip * cw_cols), (depth, 3, cw_cols))
    for n in SMALL:
        shp = w[n].shape
        two_d = (-1, shp[-1]) if n != "sgu_w" else (-1, LANES)
        d, nm, nv = adamw(f"adamw_{n}", w[n].reshape(two_d), grads[n].reshape(two_d), m[n].reshape(two_d),
                          v[n].reshape(two_d))
        delta[n], new_m[n], new_v[n] = d.reshape(shp), nm.reshape(shp), nv.reshape(shp)

    done = [new_v[n] for n in SMALL] + ([] if updated is None else [updated[3][0]])
    reduced = reduce_scatter_end("l0", reduce_scatter_sum("l0", chip_arr, started, (dh, *done)), ())
    updated = adamw_layer("l0_adamw", 0, big_w, reduced, big_m, big_v, updated)
    for k, d in enumerate((grads, delta, new_m, new_v)):
        d.update(zip(BIG, updated[k]))

    return (loss, dh[None], *[grads[n] for n in WEIGHTS], *[delta[n] for n in WEIGHTS], *[new_m[n] for n in WEIGHTS],
            *[new_v[n] for n in WEIGHTS])
```

```python
import math

import jax
import jax.numpy as jnp
from jax import lax
from jax.experimental import pallas as pl
from jax.experimental.pallas import tpu as pltpu

F32 = jnp.float32
BF = jnp.bfloat16
MESH = pl.DeviceIdType.MESH
HIGHEST = lax.Precision.HIGHEST

EPS = 1e-6
HEAD_DIM = 64
CONV_W = 256
ATTN_W = 512
SGU_W = 256
D_MIX = CONV_W + ATTN_W + SGU_W
CHUNK = 128
N_CHIPS = 4
SCALE = HEAD_DIM ** -0.5
LANES = 128
VMEM_LIMIT = 56 * 1024 * 1024

ADAM_LR = 0.001
ADAM_B1 = 0.9
ADAM_B2 = 0.999
ADAM_EPS = 1e-08
ADAM_WD = 0.01
ADAM_STEP = 10

NT_DIMS = (((1,), (1,)), ((), ()))
TN_DIMS = (((0,), (0,)), ((), ()))


def _params(*sem):
    return pltpu.CompilerParams(dimension_semantics=sem if sem else None, vmem_limit_bytes=VMEM_LIMIT)


def _sds(shape, dtype):
    return jax.ShapeDtypeStruct(shape, dtype)


def _erf(x):
    return lax.erf(x)


def _gelu(x):
    return 0.5 * x * (1.0 + _erf(x * (2.0 ** -0.5)))


def _gelu_grad(x):
    return 0.5 * (1.0 + _erf(x * (2.0 ** -0.5))) + x * jnp.exp(-0.5 * x * x) * (1.0 / math.sqrt(2.0 * math.pi))


def _log_sigmoid(z):
    return jnp.minimum(z, 0.0) - jnp.log(1.0 + jnp.exp(-jnp.abs(z)))


def _head_mean_matrix(width):
    r = lax.broadcasted_iota(jnp.int32, (width, width), 0) // HEAD_DIM
    c = lax.broadcasted_iota(jnp.int32, (width, width), 1) // HEAD_DIM
    return (r == c).astype(BF)


def _head_mean(x, m):
    hi = x.astype(BF)
    lo = (x - hi.astype(F32)).astype(BF)
    return _dot2_stacked(hi, lo, m) * (1.0 / HEAD_DIM)


def mm_nn(name, x, w, *, extras=(), pro=None, epi=None, out_dtypes=None, tm=None, tn=512):
    S, K = x.shape
    if w.ndim == 3:
        J, _, tn = w.shape
        N = J * tn
        w_spec = pl.BlockSpec((None, K, tn), lambda n, m: (n, 0, 0))
    else:
        N = w.shape[1]
        tn = min(tn, N)
        w_spec = pl.BlockSpec((K, tn), lambda n, m: (0, n))
    tm = S if tm is None else min(tm, S)
    out_dtypes = (BF,) if out_dtypes is None else out_dtypes
    n_ex, n_out = len(extras), len(out_dtypes)

    def body(x_ref, w_ref, *rest):
        xv = x_ref[...]
        if pro is not None:
            xv = pro(xv)
        acc = jnp.dot(xv.astype(BF), w_ref[...], preferred_element_type=F32)
        outs = (acc,) if epi is None else epi(acc, *[e[...] for e in rest[:n_ex]])
        for o_ref, o in zip(rest[n_ex:], outs):
            o_ref[...] = o.astype(o_ref.dtype)

    tile = pl.BlockSpec((tm, tn), lambda n, m: (m, n))
    out = pl.pallas_call(
        body, name=name, grid=(N // tn, S // tm),
        in_specs=[pl.BlockSpec((tm, K), lambda n, m: (m, 0)), w_spec] + [tile] * n_ex,
        out_specs=[tile] * n_out,
        out_shape=[_sds((S, N), d) for d in out_dtypes],
        compiler_params=_params("parallel", "parallel"),
    )(x, w, *extras)
    return out[0] if n_out == 1 else out


def norm_mm(name, h, g, w, *, tn=512, after=()):
    S, K = h.shape
    if w.ndim == 3:
        J, _, tn = w.shape
        N = J * tn
        w_spec = pl.BlockSpec((None, K, tn), lambda n: (n, 0, 0))
    else:
        N = w.shape[1]
        tn = min(tn, N)
        w_spec = pl.BlockSpec((K, tn), lambda n: (0, n))
    rows = min(S, 256)

    def body(h_ref, g_ref, w_ref, *rest):
        hn_ref, o_ref = rest[-2:]

        @pl.when(pl.program_id(0) == 0)
        def _():
            def chunk(i, _):
                r = pl.ds(pl.multiple_of(i * rows, rows), rows)
                x = h_ref[r, :]
                scale = lax.rsqrt(jnp.mean(x * x, axis=-1, keepdims=True) + EPS)
                hn_ref[r, :] = ((x * scale) * g_ref[...]).astype(BF)
                return 0

            lax.fori_loop(0, S // rows, chunk, 0)

        o_ref[...] = jnp.dot(hn_ref[...], w_ref[...], preferred_element_type=F32).astype(BF)

    whole = pl.BlockSpec((S, K), lambda n: (0, 0))
    return pl.pallas_call(
        body, name=name, grid=(N // tn,),
        in_specs=[pl.BlockSpec((S, K), lambda n: (0, 0), pipeline_mode=pl.Buffered(1)),
                  pl.BlockSpec((1, K), lambda n: (0, 0)), w_spec] + [pl.BlockSpec(memory_space=pl.ANY)] * len(after),
        out_specs=[whole, pl.BlockSpec((S, tn), lambda n: (0, n))],
        out_shape=[_sds((S, K), BF), _sds((S, N), BF)], compiler_params=_params("arbitrary"),
    )(h, g.reshape(1, K), w, *after)


def mm_nt(name, dy, w, *, extras=(), epi=None, tk=512, after=()):
    S, N = dy.shape
    K = w.shape[0]
    tk = min(tk, K)
    n_ex = len(extras)

    def body(dy_ref, w_ref, *rest):
        dyb = rest[-1]

        @pl.when(pl.program_id(0) == 0)
        def _():
            dyb[...] = dy_ref[...].astype(BF)

        acc = lax.dot_general(dyb[...], w_ref[...], NT_DIMS, preferred_element_type=F32)
        if epi is not None:
            acc = epi(acc, *[e[...] for e in rest[:n_ex]])
        rest[-2][...] = acc.astype(BF)

    col = pl.BlockSpec((S, tk), lambda k: (0, k))
    return pl.pallas_call(
        body, name=name, grid=(K // tk,),
        in_specs=[pl.BlockSpec((S, N), lambda k: (0, 0)), pl.BlockSpec((tk, N), lambda k: (k, 0))] + [col] * n_ex +
                 [pl.BlockSpec(memory_space=pl.ANY)] * len(after),
        out_specs=col, out_shape=_sds((S, K), BF), scratch_shapes=[pltpu.VMEM((S, N), BF)],
        compiler_params=_params("arbitrary"),
    )(dy, w, *extras, *after)


def mm_nt_rmsbwd(name, dy, w, h, g, dres, *, tm=256):
    S, N = dy.shape
    D = h.shape[1]
    tm = min(tm, S)
    blocked = w.ndim == 3
    nj = w.shape[2] if blocked else N

    def body(dy_ref, w_ref, h_ref, g_ref, dres_ref, dh_ref, dg_ref):
        i = pl.program_id(0)
        if blocked:
            dyn = None
            for j in range(w.shape[0]):
                part = lax.dot_general(dy_ref[:, j * nj:(j + 1) * nj].astype(BF), w_ref[j], NT_DIMS,
                                       preferred_element_type=F32)
                dyn = part if dyn is None else dyn + part
        else:
            dyn = lax.dot_general(dy_ref[...].astype(BF), w_ref[...], NT_DIMS, preferred_element_type=F32)
        x = h_ref[...]
        r = lax.rsqrt(jnp.mean(x * x, axis=-1, keepdims=True) + EPS)
        t = dyn * g_ref[...]
        dh_ref[...] = dres_ref[...] + r * t - x * (r * r * r) * jnp.mean(t * x, axis=-1, keepdims=True)
        part = jnp.sum(dyn * (x * r), axis=0, keepdims=True)

        @pl.when(i == 0)
        def _():
            dg_ref[...] = part

        @pl.when(i > 0)
        def _():
            dg_ref[...] += part

    w_spec = pl.BlockSpec(w.shape, (lambda i: (0, 0, 0)) if blocked else (lambda i: (0, 0)))
    row = pl.BlockSpec((tm, D), lambda i: (i, 0))
    vec = pl.BlockSpec((1, D), lambda i: (0, 0))
    return pl.pallas_call(
        body, name=name, grid=(S // tm,),
        in_specs=[pl.BlockSpec((tm, N), lambda i: (i, 0)), w_spec, row, vec, row],
        out_specs=[row, vec], out_shape=[_sds((S, D), F32), _sds((1, D), F32)],
        compiler_params=_params("arbitrary"),
    )(dy, w, h, g.reshape(1, D), dres)


def mm_tn(name, x, dy, *, pro_x=None, col_blocks=None, tk=1024, tn=1024):
    S, K = x.shape
    N = dy.shape[1]
    tk = min(tk, K)
    if col_blocks is not None:
        tn = N // col_blocks
        out_shape = _sds((col_blocks, K, tn), BF)
        out_spec = pl.BlockSpec((None, tk, tn), lambda k, n: (n, k, 0))
    else:
        tn = min(tn, N)
        out_shape = _sds((K, N), BF)
        out_spec = pl.BlockSpec((tk, tn), lambda k, n: (k, n))

    def body(x_ref, dy_ref, o_ref):
        xv = x_ref[...]
        if pro_x is not None:
            xv = pro_x(xv)
        o_ref[...] = lax.dot_general(xv.astype(BF), dy_ref[...].astype(BF), TN_DIMS,
                                     preferred_element_type=F32).astype(BF)

    return pl.pallas_call(
        body, name=name, grid=(K // tk, N // tn),
        in_specs=[pl.BlockSpec((S, tk), lambda k, n: (0, k)), pl.BlockSpec((S, tn), lambda k, n: (0, n))],
        out_specs=out_spec, out_shape=out_shape, compiler_params=_params("parallel", "parallel"),
    )(x, dy)


def _conv_parts(ac_ref, ah_ref, cw):
    a_c = ac_ref[...].astype(F32)
    a_h = ah_ref[...].astype(F32)
    x = a_c * a_h
    row = lax.broadcasted_iota(jnp.int32, x.shape, 0)
    x1 = jnp.where(row >= 1, pltpu.roll(x, 1, 0), 0.0)
    x2 = jnp.where(row >= 2, pltpu.roll(x, 2, 0), 0.0)
    cv = cw[0:1] * x2 + cw[1:2] * x1 + cw[2:3] * x
    return a_c, a_h, x, x1, x2, cv, row


def conv_fwd(name, proj, cw):
    S = proj.shape[0]

    def body(ab_ref, ac_ref, ah_ref, cw_ref, o_ref):
        cv = _conv_parts(ac_ref, ah_ref, cw_ref[...])[5]
        o_ref[...] = (ab_ref[...].astype(F32) * cv).astype(BF)

    col = lambda j: pl.BlockSpec((S, CONV_W), lambda i, j=j: (0, j))
    return pl.pallas_call(
        body, name=name, grid=(1,),
        in_specs=[col(0), col(1), col(2), pl.BlockSpec((3, CONV_W), lambda i: (0, 0))],
        out_specs=pl.BlockSpec((S, CONV_W), lambda i: (0, 0)),
        out_shape=_sds((S, D_MIX), BF), compiler_params=_params("arbitrary"),
    )(proj, proj, proj, cw)


def conv_bwd(name, dy, proj, cw):
    S = proj.shape[0]

    def body(dy_ref, ab_ref, ac_ref, ah_ref, cw_ref, dab_ref, dac_ref, dah_ref, dcw_ref):
        w = cw_ref[...]
        a_c, a_h, x, x1, x2, cv, row = _conv_parts(ac_ref, ah_ref, w)
        d = dy_ref[...].astype(F32)
        dab_ref[...] = (d * cv).astype(BF)
        dcv = d * ab_ref[...].astype(F32)
        d1 = jnp.where(row < S - 1, pltpu.roll(dcv, S - 1, 0), 0.0)
        d2 = jnp.where(row < S - 2, pltpu.roll(dcv, S - 2, 0), 0.0)
        dx = w[2:3] * dcv + w[1:2] * d1 + w[0:1] * d2
        dac_ref[...] = (dx * a_h).astype(BF)
        dah_ref[...] = (dx * a_c).astype(BF)
        dcw_ref[0:1, :] = jnp.sum(dcv * x2, axis=0, keepdims=True)
        dcw_ref[1:2, :] = jnp.sum(dcv * x1, axis=0, keepdims=True)
        dcw_ref[2:3, :] = jnp.sum(dcv * x, axis=0, keepdims=True)

    col = lambda j: pl.BlockSpec((S, CONV_W), lambda i, j=j: (0, j))
    one = pl.BlockSpec((S, CONV_W), lambda i: (0, 0))
    small = pl.BlockSpec((3, CONV_W), lambda i: (0, 0))
    return pl.pallas_call(
        body, name=name, grid=(1,),
        in_specs=[col(0), col(0), col(1), col(2), small],
        out_specs=[one, one, one, small],
        out_shape=[_sds((S, CONV_W), BF)] * 3 + [_sds((3, CONV_W), F32)],
        compiler_params=_params("arbitrary"),
    )(dy, proj, proj, proj, cw)


SGU_HEADS = SGU_W // HEAD_DIM
CU_BLOCK = 2304 // SGU_W
CV_BLOCK = 2560 // SGU_W


def _sgu_common(cu_ref, cv_ref, gv_ref):
    c_u = cu_ref[...].astype(F32)
    c_v = cv_ref[...].astype(F32)
    hm = _head_mean_matrix(SGU_W)
    u = _gelu(c_u)
    vg = _gelu(c_v)
    r = lax.rsqrt(_head_mean(vg * vg, hm) + EPS)
    vv = (vg * r) * gv_ref[...]
    head = lax.broadcasted_iota(jnp.int32, (CHUNK, SGU_W), 1) // HEAD_DIM
    tri = (lax.broadcasted_iota(jnp.int32, (CHUNK, CHUNK), 0) >=
           lax.broadcasted_iota(jnp.int32, (CHUNK, CHUNK), 1))
    return c_u, c_v, hm, u, vg, r, vv, head, tri


def _sgu_mix(w_ref, tri, head, vvc, bias):
    sv = bias
    for g in range(SGU_HEADS):
        wg = jnp.where(tri, w_ref[g], 0.0).astype(BF)
        sv = sv + jnp.where(head == g, jnp.dot(wg, vvc, preferred_element_type=F32), 0.0)
    return sv


def sgu_fwd(name, proj, gv, w, bias, y):
    S = proj.shape[0]
    tm = min(S, 512)

    def body(cu_ref, cv_ref, gv_ref, w_ref, b_ref, y_ref, o_ref):
        _, _, _, u, _, _, vv, head, tri = _sgu_common(cu_ref, cv_ref, gv_ref)
        vvb = vv.astype(BF)
        for ch in range(tm // CHUNK):
            rows = slice(ch * CHUNK, (ch + 1) * CHUNK)
            sv = _sgu_mix(w_ref, tri, head, vvb[rows], b_ref[...])
            o_ref[rows, :] = (u[rows] * sv).astype(BF)

    const = lambda shape: pl.BlockSpec(shape, lambda i: (0,) * len(shape))
    return pl.pallas_call(
        body, name=name, grid=(S // tm,),
        in_specs=[pl.BlockSpec((tm, SGU_W), lambda i: (i, CU_BLOCK)), pl.BlockSpec((tm, SGU_W), lambda i: (i, CV_BLOCK)),
                  const((1, SGU_W)), const((SGU_HEADS, CHUNK, CHUNK)), const((CHUNK, SGU_W)),
                  pl.BlockSpec(memory_space=pl.ANY)],
        out_specs=pl.BlockSpec((tm, SGU_W), lambda i: (i, (CONV_W + ATTN_W) // SGU_W)),
        out_shape=_sds(y.shape, BF), input_output_aliases={5: 0}, compiler_params=_params("parallel"),
    )(proj, proj, gv, w, bias, y)


def sgu_bwd(name, dy, proj, gv, w, bias):
    S = proj.shape[0]
    tm = min(S, 512)

    def body(dy_ref, cu_ref, cv_ref, gv_ref, w_ref, b_ref, dcu_ref, dcv_ref, dw_ref, db_ref, dgv_ref, dvv_s):
        i = pl.program_id(0)
        c_u, c_v, hm, u, vg, r, vv, head, tri = _sgu_common(cu_ref, cv_ref, gv_ref)
        vvb = vv.astype(BF)
        d = dy_ref[...].astype(F32)
        ind = (lax.broadcasted_iota(jnp.int32, (SGU_W, LANES), 0) // HEAD_DIM ==
               lax.broadcasted_iota(jnp.int32, (SGU_W, LANES), 1)).astype(BF)
        dw_acc = [jnp.zeros((CHUNK, CHUNK), F32) for _ in range(SGU_HEADS)]
        db_acc = jnp.zeros((CHUNK, LANES), F32)
        for ch in range(tm // CHUNK):
            rows = slice(ch * CHUNK, (ch + 1) * CHUNK)
            sv = _sgu_mix(w_ref, tri, head, vvb[rows], b_ref[...])
            dcu_ref[rows, :] = (d[rows] * sv * _gelu_grad(c_u[rows])).astype(BF)
            dsv = d[rows] * u[rows]
            dsv_hi = dsv.astype(BF)
            db_acc = db_acc + _dot2_stacked(dsv_hi, (dsv - dsv_hi.astype(F32)).astype(BF), ind)
            dvv = jnp.zeros((CHUNK, SGU_W), F32)
            for g in range(SGU_HEADS):
                dsv_g = jnp.where(head == g, dsv, 0.0).astype(BF)
                wg = jnp.where(tri, w_ref[g], 0.0).astype(BF)
                dvv = dvv + lax.dot_general(wg, dsv_g, TN_DIMS, preferred_element_type=F32)
                dw_acc[g] = dw_acc[g] + lax.dot_general(dsv_g, vvb[rows], NT_DIMS, preferred_element_type=F32)
            dvv_s[rows, :] = dvv
        dvv = dvv_s[...]
        gvv = gv_ref[...]
        t = dvv * gvv
        dvg = r * t - vg * (r * r * r) * _head_mean(t * vg, hm)
        dcv_ref[...] = (dvg * _gelu_grad(c_v)).astype(BF)
        dgv = jnp.sum(dvv * (vg * r), axis=0, keepdims=True)

        @pl.when(i == 0)
        def _():
            for g in range(SGU_HEADS):
                dw_ref[g] = jnp.where(tri, dw_acc[g], 0.0)
            db_ref[...] = db_acc
            dgv_ref[...] = dgv

        @pl.when(i > 0)
        def _():
            for g in range(SGU_HEADS):
                dw_ref[g] += jnp.where(tri, dw_acc[g], 0.0)
            db_ref[...] += db_acc
            dgv_ref[...] += dgv

    const = lambda shape: pl.BlockSpec(shape, lambda i: (0,) * len(shape))
    tile = pl.BlockSpec((tm, SGU_W), lambda i: (i, 0))
    return pl.pallas_call(
        body, name=name, grid=(S // tm,),
        in_specs=[pl.BlockSpec((tm, SGU_W), lambda i: (i, 3)),
                  pl.BlockSpec((tm, SGU_W), lambda i: (i, CU_BLOCK)), pl.BlockSpec((tm, SGU_W), lambda i: (i, CV_BLOCK)),
                  const((1, SGU_W)), const((SGU_HEADS, CHUNK, CHUNK)), const((CHUNK, SGU_W))],
        out_specs=[tile, tile, const((SGU_HEADS, CHUNK, CHUNK)), const((CHUNK, LANES)), const((1, SGU_W))],
        out_shape=[_sds((S, SGU_W), BF), _sds((S, SGU_W), BF), _sds((SGU_HEADS, CHUNK, CHUNK), F32),
                   _sds((CHUNK, LANES), F32), _sds((1, SGU_W), F32)],
        scratch_shapes=[pltpu.VMEM((tm, SGU_W), F32)],
        compiler_params=_params("arbitrary"),
    )(dy, proj, proj, gv, w, bias)


HEAD_PAIRS = ATTN_W // LANES
Q_BLOCK0 = 768 // LANES
K_BLOCK0 = 1280 // LANES
V_BLOCK0 = 1792 // LANES


def _attn_tile(S):
    return min(S, 256)


def _qk_norm(x, g, hm):
    r = lax.rsqrt(_head_mean(x * x, hm) + EPS)
    return r, (x * r) * g


MASKED = -1e30


def _logit_parts(z):
    lb = _log_sigmoid(z)
    lr = lb - z
    hi = lr.astype(BF)
    return lb, hi, (lr - hi.astype(F32)).astype(BF)


def _stack_heads(x, lane):
    return jnp.concatenate([jnp.where(lane < HEAD_DIM, x, 0.0), jnp.where(lane >= HEAD_DIM, x, 0.0)],
                           axis=0).astype(BF)


def _dot2_stacked(hi, lo, u):
    rows = hi.shape[0]
    both = jnp.dot(jnp.concatenate([hi, lo], axis=0), u, preferred_element_type=F32)
    return both[:rows] + both[rows:]


def attn_fwd(name, proj, gq, gk, y):
    S = proj.shape[0]
    T = _attn_tile(S)
    nq = S // T

    def body(q_ref, k_ref, v_ref, gq_ref, gk_ref, y_ref, o_ref, tot_ref, kn_s, lb_s, hi_s, lo_s, z_s, a_s, o_s):
        qi = pl.program_id(1)
        hm = _head_mean_matrix(LANES)

        @pl.when(qi == 0)
        def _():
            kn_s[...] = _qk_norm(k_ref[...].astype(F32), gk_ref[...], hm)[1].astype(BF)

        qn = _qk_norm(q_ref[...].astype(F32), gq_ref[...], hm)[1]
        lane = lax.broadcasted_iota(jnp.int32, (T, LANES), 1)
        qst = _stack_heads(qn, lane)
        rowi = lax.broadcasted_iota(jnp.int32, (T, T), 0)
        coli = lax.broadcasted_iota(jnp.int32, (T, T), 1)
        u_excl = (rowi > coli).astype(BF)
        diagonal = jnp.where(coli < rowi, 0.0, MASKED)

        def logits(j):
            return lax.dot_general(qst, kn_s[pl.ds(pl.multiple_of(j * T, T), T), :], NT_DIMS,
                                   preferred_element_type=F32)

        def values(j):
            return v_ref[pl.ds(pl.multiple_of(j * T, T), T), :].astype(BF)

        def keep(slot, z):
            lb_s[slot], hi_s[...], lo_s[...] = _logit_parts(z)

        def step(it, carry):
            run = carry
            j = qi - it
            hi, lo = hi_s[...], lo_s[...]
            both = jnp.dot(jnp.concatenate([hi, lo], axis=0), u_excl, preferred_element_type=F32)
            o_s[...] += jnp.dot(a_s[...], values(jnp.minimum(j + 1, qi)), preferred_element_type=F32)
            z_after = logits(jnp.maximum(j - 2, 0))
            first = hi[:, 0:1].astype(F32) + lo[:, 0:1].astype(F32)
            keep((it + 1) % 2, z_s[...])
            later = both[:2 * T] + both[2 * T:]
            a_s[...] = jnp.exp(lb_s[it % 2] + later + run).astype(BF)
            z_s[...] = z_after
            return run + later[:, 0:1] + first

        keep(0, logits(qi) + jnp.concatenate([diagonal, diagonal], axis=0))
        z_s[...] = logits(jnp.maximum(qi - 1, 0))
        a_s[...] = jnp.zeros_like(a_s)
        o_s[...] = jnp.zeros_like(o_s)
        run = lax.fori_loop(0, qi, step, jnp.zeros((2 * T, 1), F32))
        hi, lo = hi_s[...], lo_s[...]
        both = jnp.dot(jnp.concatenate([hi, lo], axis=0), u_excl, preferred_element_type=F32)
        o = o_s[...] + jnp.dot(a_s[...], values(jnp.minimum(1, qi)), preferred_element_type=F32)
        later = both[:2 * T] + both[2 * T:]
        a = jnp.exp(lb_s[qi % 2] + later + run).astype(BF)
        run = run + later[:, 0:1] + (hi[:, 0:1].astype(F32) + lo[:, 0:1].astype(F32))
        o = o + jnp.dot(a, values(0), preferred_element_type=F32)
        o_ref[...] = jnp.where(lane < HEAD_DIM, o[:T], o[T:]).astype(BF)
        tot_ref[...] = jnp.where(lane < HEAD_DIM, run[:T], run[T:])

    gain = pl.BlockSpec((1, LANES), lambda hp, qi: (0, 0))
    full = lambda b0: pl.BlockSpec((S, LANES), lambda hp, qi, b0=b0: (0, b0 + hp))
    tile = pl.BlockSpec((T, LANES), lambda hp, qi: (qi, hp))
    return pl.pallas_call(
        body, name=name, grid=(HEAD_PAIRS, nq),
        in_specs=[pl.BlockSpec((T, LANES), lambda hp, qi: (qi, Q_BLOCK0 + hp)), full(K_BLOCK0), full(V_BLOCK0), gain, gain,
                  pl.BlockSpec(memory_space=pl.ANY)],
        out_specs=[pl.BlockSpec((T, LANES), lambda hp, qi: (qi, CONV_W // LANES + hp)), tile],
        out_shape=[_sds(y.shape, BF), _sds((S, ATTN_W), F32)], input_output_aliases={5: 0},
        scratch_shapes=[pltpu.VMEM((S, LANES), BF), pltpu.VMEM((2, 2 * T, T), F32), pltpu.VMEM((2 * T, T), BF),
                        pltpu.VMEM((2 * T, T), BF), pltpu.VMEM((2 * T, T), F32), pltpu.VMEM((2 * T, T), BF),
                        pltpu.VMEM((2 * T, LANES), F32)],
        compiler_params=_params("arbitrary", "arbitrary"),
    )(proj, proj, proj, gq, gk, y)


def attn_bwd(name, dy, proj, tot, gq, gk):
    S = proj.shape[0]
    T = _attn_tile(S)
    nq = S // T

    def body(q_ref, k_ref, v_ref, tot_ref, do_ref, gq_ref, gk_ref,
             dq_ref, dk_ref, dv_ref, dgq_ref, dgk_ref, kn_s, dkn_s, dv_s,
             lb_s, z_s, g_s, dq_s, hi_s, lo_s, a_s, ghi_s, glo_s):
        hp = pl.program_id(0)
        qi = pl.program_id(1)
        hm = _head_mean_matrix(LANES)

        @pl.when(qi == 0)
        def _():
            kn_s[...] = _qk_norm(k_ref[...].astype(F32), gk_ref[...], hm)[1].astype(BF)
            dkn_s[...] = jnp.zeros_like(dkn_s)
            dv_s[...] = jnp.zeros_like(dv_s)

        q = q_ref[...].astype(F32)
        rq, qn = _qk_norm(q, gq_ref[...], hm)
        lane = lax.broadcasted_iota(jnp.int32, (T, LANES), 1)
        qst = _stack_heads(qn, lane)
        dost = _stack_heads(do_ref[...].astype(F32), lane)
        total = jnp.concatenate([tot_ref[:, 0:1], tot_ref[:, HEAD_DIM:HEAD_DIM + 1]], axis=0)
        rowi = lax.broadcasted_iota(jnp.int32, (T, T), 0)
        coli = lax.broadcasted_iota(jnp.int32, (T, T), 1)
        u_upto = (rowi <= coli).astype(BF)
        u_before = (rowi < coli).astype(BF)
        diagonal = jnp.where(coli < rowi, 0.0, MASKED)

        on_diagonal = jnp.concatenate([diagonal, diagonal], axis=0)

        def rows(b):
            return pl.ds(pl.multiple_of(jnp.clip(b, 0, qi) * T, T), T)

        def logits(b):
            return lax.dot_general(qst, kn_s[rows(b), :], NT_DIMS, preferred_element_type=F32)

        def keep(b, z):
            bias = jnp.where(b == qi, on_diagonal, jnp.where(b > qi, MASKED, 0.0))
            lb_s[b % 3], hi_s[...], lo_s[...] = _logit_parts(z + bias)

        def step(i, carry):
            run, grun = carry
            both_before = jnp.dot(jnp.concatenate([ghi_s[...], glo_s[...]], axis=0), u_before,
                                  preferred_element_type=F32)
            both_upto = jnp.dot(jnp.concatenate([hi_s[...], lo_s[...]], axis=0), u_upto, preferred_element_type=F32)
            da = lax.dot_general(dost, v_ref[rows(i), :].astype(BF), NT_DIMS, preferred_element_type=F32)
            dv_s[rows(i - 1), :] += lax.dot_general(a_s[...], dost, TN_DIMS, preferred_element_type=F32)
            z_after = logits(i + 2)

            keep(i + 1, z_s[...])

            g = g_s[...]
            before = both_before[:2 * T] + both_before[2 * T:]
            dz = (g - jnp.exp(lb_s[(i + 2) % 3]) * (g + (grun + before))).astype(BF)
            dq_s[...] += jnp.dot(dz, kn_s[rows(i - 1), :], preferred_element_type=F32)
            dkn_s[rows(i - 1), :] += lax.dot_general(dz, qst, TN_DIMS, preferred_element_type=F32)
            grun = grun + before[:, T - 1:T] + g[:, T - 1:T]

            upto = both_upto[:2 * T] + both_upto[2 * T:]
            a = jnp.exp(lb_s[i % 3] + (total - run - upto))
            g = da * a
            a_s[...] = a.astype(BF)
            g_s[...] = g
            ghi = g.astype(BF)
            ghi_s[...] = ghi
            glo_s[...] = (g - ghi.astype(F32)).astype(BF)
            z_s[...] = z_after
            return run + upto[:, T - 1:T], grun

        lb_s[...] = jnp.full(lb_s.shape, MASKED, F32)
        for ref in (a_s, g_s, ghi_s, glo_s, dq_s):
            ref[...] = jnp.zeros_like(ref)
        keep(0, logits(0))
        z_s[...] = logits(1)
        _, grun = lax.fori_loop(0, qi + 1, step, (jnp.zeros((2 * T, 1), F32), jnp.zeros((2 * T, 1), F32)))
        both_before = jnp.dot(jnp.concatenate([ghi_s[...], glo_s[...]], axis=0), u_before, preferred_element_type=F32)
        dv_s[rows(qi), :] += lax.dot_general(a_s[...], dost, TN_DIMS, preferred_element_type=F32)
        g = g_s[...]
        before = both_before[:2 * T] + both_before[2 * T:]
        dz = (g - jnp.exp(lb_s[qi % 3]) * (g + (grun + before))).astype(BF)
        dkn_s[rows(qi), :] += lax.dot_general(dz, qst, TN_DIMS, preferred_element_type=F32)
        dq_all = dq_s[...] + jnp.dot(dz, kn_s[rows(qi), :], preferred_element_type=F32)
        dqn = jnp.where(lane < HEAD_DIM, dq_all[:T], dq_all[T:])
        gq_v = gq_ref[...]
        t = dqn * gq_v
        dq_ref[...] = (rq * t - q * (rq * rq * rq) * _head_mean(t * q, hm)).astype(BF)
        dgq = jnp.sum(dqn * (q * rq), axis=0, keepdims=True) * SCALE
        first = jnp.logical_and(hp == 0, qi == 0)

        @pl.when(first)
        def _():
            dgq_ref[...] = dgq

        @pl.when(jnp.logical_not(first))
        def _():
            dgq_ref[...] += dgq

        @pl.when(qi == nq - 1)
        def _():
            k = k_ref[...].astype(F32)
            rk = _qk_norm(k, gk_ref[...], hm)[0]
            dkn = dkn_s[...]
            tk = dkn * gk_ref[...]
            dk_ref[...] = (rk * tk - k * (rk * rk * rk) * _head_mean(tk * k, hm)).astype(BF)
            dgk = jnp.sum(dkn * (k * rk), axis=0, keepdims=True)
            dv_ref[...] = dv_s[...].astype(BF)

            @pl.when(hp == 0)
            def _():
                dgk_ref[...] = dgk

            @pl.when(hp > 0)
            def _():
                dgk_ref[...] += dgk

            @pl.when(hp == HEAD_PAIRS - 1)
            def _():
                fold = (lax.broadcasted_iota(jnp.int32, (LANES, LANES), 0) % HEAD_DIM ==
                        lax.broadcasted_iota(jnp.int32, (LANES, LANES), 1) % HEAD_DIM).astype(F32)
                dgq_ref[...] = jnp.dot(dgq_ref[...], fold, precision=HIGHEST, preferred_element_type=F32)
                dgk_ref[...] = jnp.dot(dgk_ref[...], fold, precision=HIGHEST, preferred_element_type=F32)

    gain = pl.BlockSpec((1, LANES), lambda hp, qi: (0, 0))
    full = lambda b0: pl.BlockSpec((S, LANES), lambda hp, qi, b0=b0: (0, b0 + hp))
    tile = pl.BlockSpec((T, LANES), lambda hp, qi: (qi, hp))
    col = pl.BlockSpec((S, LANES), lambda hp, qi: (0, hp))
    dgain = pl.BlockSpec((1, LANES), lambda hp, qi: (0, 0))
    return pl.pallas_call(
        body, name=name, grid=(HEAD_PAIRS, nq),
        in_specs=[pl.BlockSpec((T, LANES), lambda hp, qi: (qi, Q_BLOCK0 + hp)), full(K_BLOCK0), full(V_BLOCK0),
                  tile, pl.BlockSpec((T, LANES), lambda hp, qi: (qi, 2 + hp)), gain, gain],
        out_specs=[tile, col, col, dgain, dgain],
        out_shape=[_sds((S, ATTN_W), BF)] * 3 + [_sds((1, LANES), F32)] * 2,
        scratch_shapes=[pltpu.VMEM((S, LANES), BF), pltpu.VMEM((S, LANES), F32), pltpu.VMEM((S, LANES), F32),
                        pltpu.VMEM((3, 2 * T, T), F32), pltpu.VMEM((2 * T, T), F32), pltpu.VMEM((2 * T, T), F32),
                        pltpu.VMEM((2 * T, LANES), F32)] + [pltpu.VMEM((2 * T, T), BF)] * 5,
        compiler_params=_params("arbitrary", "arbitrary"),
    )(proj, proj, proj, tot, dy, gq, gk)


def ple_bwd(name, dh, gp, pp, w, h, g, after=()):
    S, D = dh.shape
    tm = min(S, 256)

    def body(dh_ref, gp_ref, pp_ref, w_ref, h_ref, g_ref, *rest):
        dgp_ref, dpp_ref, dh2_ref, dg_ref = rest[-4:]
        i = pl.program_id(0)
        d = dh_ref[...]
        gate = jax.nn.sigmoid(gp_ref[...].astype(F32))
        dpp_ref[...] = (d * gate).astype(BF)
        dgp = (d * pp_ref[...].astype(F32) * gate * (1.0 - gate)).astype(BF)
        dgp_ref[...] = dgp
        dyn = lax.dot_general(dgp, w_ref[...], NT_DIMS, preferred_element_type=F32)
        x = h_ref[...]
        r = lax.rsqrt(jnp.mean(x * x, axis=-1, keepdims=True) + EPS)
        t = dyn * g_ref[...]
        dh2_ref[...] = d + r * t - x * (r * r * r) * jnp.mean(t * x, axis=-1, keepdims=True)
        part = jnp.sum(dyn * (x * r), axis=0, keepdims=True)

        @pl.when(i == 0)
        def _():
            dg_ref[...] = part

        @pl.when(i > 0)
        def _():
            dg_ref[...] += part

    tile = pl.BlockSpec((tm, D), lambda i: (i, 0))
    vec = pl.BlockSpec((1, D), lambda i: (0, 0))
    return pl.pallas_call(
        body, name=name, grid=(S // tm,),
        in_specs=[tile, tile, tile, pl.BlockSpec((D, D), lambda i: (0, 0)), tile, vec] +
                 [pl.BlockSpec(memory_space=pl.ANY)] * len(after),
        out_specs=[tile, tile, tile, vec],
        out_shape=[_sds((S, D), BF), _sds((S, D), BF), _sds((S, D), F32), _sds((1, D), F32)],
        compiler_params=_params("arbitrary"),
    )(dh, gp, pp, w, h, g.reshape(1, D), *after)


def loss_head(name, h, target):
    S, D = h.shape
    tm = min(S, 512)

    def body(h_ref, t_ref, loss_ref, dh_ref):
        i = pl.program_id(0)
        e = h_ref[...] - t_ref[...]
        dh_ref[...] = e * (1.0 / D)
        part = jnp.zeros((8, LANES), F32) + 0.5 * jnp.sum(jnp.mean(e * e, axis=-1, keepdims=True))

        @pl.when(i == 0)
        def _():
            loss_ref[...] = part

        @pl.when(i > 0)
        def _():
            loss_ref[...] += part

    tile = pl.BlockSpec((tm, D), lambda i: (i, 0))
    return pl.pallas_call(
        body, name=name, grid=(S // tm,), in_specs=[tile, tile],
        out_specs=[pl.BlockSpec((8, LANES), lambda i: (0, 0)), tile],
        out_shape=[_sds((8, LANES), F32), _sds((S, D), F32)], compiler_params=_params("arbitrary"),
    )(h, target)


def _adamw_math(w, g, m, v):
    c1 = 1.0 - ADAM_B1 ** ADAM_STEP
    c2 = 1.0 - ADAM_B2 ** ADAM_STEP
    nm = ADAM_B1 * m + (1.0 - ADAM_B1) * g
    nv = ADAM_B2 * v + (1.0 - ADAM_B2) * (g * g)
    return -ADAM_LR * ((nm / c1) / (jnp.sqrt(nv / c2) + ADAM_EPS) + ADAM_WD * w), nm, nv


def adamw(name, w, g, m, v):
    R, C = w.shape
    tr = R
    for cand in (512, 256, 128, 64, 32, 16, 8):
        if R % cand == 0:
            tr = cand
            break

    def body(w_ref, g_ref, m_ref, v_ref, d_ref, nm_ref, nv_ref):
        d_ref[...], nm_ref[...], nv_ref[...] = _adamw_math(w_ref[...], g_ref[...], m_ref[...], v_ref[...])

    tile = pl.BlockSpec((tr, C), lambda i: (i, 0))
    return pl.pallas_call(
        body, name=name, grid=(R // tr,), in_specs=[tile] * 4, out_specs=[tile] * 3,
        out_shape=[_sds((R, C), F32)] * 3, compiler_params=_params("parallel"),
    )(w, g, m, v)


def adamw_layer(name, layer, ws, gs, ms, vs, prev, after=()):
    n = len(ws)
    steps = 8

    def body(*refs):
        ins, outs = refs[:4 * n], refs[-4 * n:]
        for i in range(n):
            w_ref, g_ref, m_ref, v_ref = (ins[k * n + i] for k in range(4))
            g = g_ref[...]
            outs[i][...] = g
            outs[n + i][...], outs[2 * n + i][...], outs[3 * n + i][...] = _adamw_math(w_ref[...], g, m_ref[...],
                                                                                        v_ref[...])

    def stacked(a):
        return pl.BlockSpec((None, a.shape[1] // steps, a.shape[2]), lambda t: (layer, t, 0))

    def flat(a):
        return pl.BlockSpec((a.shape[0] // steps, a.shape[1]), lambda t: (t, 0))

    in_specs = [stacked(a) for a in ws] + [flat(a) for a in gs] + [stacked(a) for a in ms] + [stacked(a) for a in vs]
    operands = [*ws, *gs, *ms, *vs]
    aliases = {}
    if prev is not None:
        flat_prev = [a for group in prev for a in group]
        in_specs += [pl.BlockSpec(memory_space=pl.ANY)] * len(flat_prev)
        aliases = {4 * n + i: i for i in range(4 * n)}
        operands += flat_prev
    in_specs += [pl.BlockSpec(memory_space=pl.ANY)] * len(after)
    operands += list(after)
    out = pl.pallas_call(
        body, name=name, grid=(steps,), in_specs=in_specs, out_specs=[stacked(a) for a in ws] * 4,
        out_shape=[_sds(a.shape, F32) for a in ws] * 4, input_output_aliases=aliases,
        compiler_params=_params("parallel"),
    )(*operands)
    return [list(out[k * n:(k + 1) * n]) for k in range(4)]


def _relu2(u):
    r = jnp.maximum(u.astype(F32), 0.0)
    return r * r


def layer_fwd(tag, h0, p_bf, wt, after=(), mid=None):
    hn1, proj = norm_mm(f"{tag}_proj", h0, wt["norm1_g"], wt["w_in"], tn=256, after=after)
    y = conv_fwd(f"{tag}_conv", proj, wt["conv_w"])
    y, yb_tot = attn_fwd(f"{tag}_attn", proj, wt["gq"], wt["gk"], y)
    y = sgu_fwd(f"{tag}_sgu", proj, wt["gv"], wt["sgu_w"], wt["sgu_bias"], y)
    h1 = mm_nn(f"{tag}_out", y, wt["w_out"], extras=(h0,), epi=lambda acc, h: (h + acc,), out_dtypes=(F32,))
    hn2, uu = norm_mm(f"{tag}_ff1", h1, wt["norm2_g"], wt["w_ff1"], after=() if mid is None else mid(yb_tot))
    h2 = mm_nn(f"{tag}_ff2", uu, wt["w_ff2"], pro=_relu2, extras=(h1,), epi=lambda acc, h: (h + acc,),
               out_dtypes=(F32,), tm=512)
    hn3, gp = norm_mm(f"{tag}_gate", h2, wt["norm3_g"], wt["w_ple_gate"])
    h3, pp = mm_nn(f"{tag}_ple", p_bf, wt["w_ple_proj"], extras=(gp, h2),
                   epi=lambda acc, g, h: (h + jax.nn.sigmoid(g.astype(F32)) * acc, acc), out_dtypes=(F32, BF))
    saved = dict(h0=h0, h1=h1, h2=h2, hn1=hn1, hn2=hn2, hn3=hn3, proj=proj, yb_tot=yb_tot, y=y, uu=uu, gp=gp, pp=pp,
                 p_bf=p_bf)
    return h3, saved


def layer_bwd(tag, dh3, sv, wt, after=(), mid=None):
    g = {}
    dgp, dpp, dh2, g["norm3_g"] = ple_bwd(f"{tag}_dple", dh3, sv["gp"], sv["pp"], wt["w_ple_gate"], sv["h2"],
                                          wt["norm3_g"], after)
    g["w_ple_proj"] = mm_tn(f"{tag}_dwp", sv["p_bf"], dpp, col_blocks=N_CHIPS)
    g["w_ple_gate"] = mm_tn(f"{tag}_dwg", sv["hn3"], dgp)

    duu = mm_nt(f"{tag}_dff2", dh2, wt["w_ff2"], extras=(sv["uu"],),
                epi=lambda acc, u: acc * (2.0 * jnp.maximum(u.astype(F32), 0.0)))
    g["w_ff2"] = mm_tn(f"{tag}_dw2", sv["uu"], dh2, pro_x=_relu2)
    g["w_ff1"] = mm_tn(f"{tag}_dw1", sv["hn2"], duu, col_blocks=N_CHIPS)
    dh1, g["norm2_g"] = mm_nt_rmsbwd(f"{tag}_dnorm2", duu, wt["w_ff1"], sv["h1"], wt["norm2_g"], dh2)

    dy = mm_nt(f"{tag}_dout", dh1, wt["w_out"], after=() if mid is None else mid(dh1))
    g["w_out"] = mm_tn(f"{tag}_dwo", sv["y"], dh1)
    dab, dac, dah, g["conv_w"] = conv_bwd(f"{tag}_dconv", dy, sv["proj"], wt["conv_w"])
    dq, dk, dv, g["gq"], g["gk"] = attn_bwd(f"{tag}_dattn", dy, sv["proj"], sv["yb_tot"], wt["gq"], wt["gk"])
    dcu, dcv, g["sgu_w"], g["sgu_bias"], g["gv"] = sgu_bwd(f"{tag}_dsgu", dy, sv["proj"], wt["gv"], wt["sgu_w"],
                                                           wt["sgu_bias"])
    dproj = jnp.concatenate([dab, dac, dah, dq, dk, dv, dcu, dcv], axis=-1)
    g["w_in"] = mm_tn(f"{tag}_dwi", sv["hn1"], dproj, tn=1408)
    dh0, g["norm1_g"] = mm_nt_rmsbwd(f"{tag}_dnorm1", dproj, wt["w_in"], sv["h0"], wt["norm1_g"], dh1)
    return dh0, g


def prep_small(norm1_g, q_norm_g, k_norm_g, sgu_norm_g, sgu_w, sgu_b, norm2_g, norm3_g, conv_w_full):
    return dict(
        norm1_g=norm1_g, norm2_g=norm2_g, norm3_g=norm3_g, conv_w=conv_w_full,
        gq=(jnp.tile(q_norm_g, 2) * SCALE).reshape(1, LANES), gk=jnp.tile(k_norm_g, 2).reshape(1, LANES),
        gv=sgu_norm_g.reshape(1, SGU_W), sgu_w=sgu_w, sgu_bias=jnp.repeat(sgu_b.T, HEAD_DIM, axis=1))


def small_grads(g):
    return dict(
        norm1_g=g["norm1_g"][0], norm2_g=g["norm2_g"][0], norm3_g=g["norm3_g"][0], conv_w=g["conv_w"],
        q_norm_g=g["gq"][0, :HEAD_DIM], k_norm_g=g["gk"][0, :HEAD_DIM], sgu_norm_g=g["gv"][0], sgu_w=g["sgu_w"],
        sgu_b=g["sgu_bias"][:, :SGU_HEADS].T)


HBM_SPEC = pl.BlockSpec(memory_space=pltpu.HBM)
BIG = ("w_in", "w_out", "w_ff1", "w_ff2", "w_ple_gate", "w_ple_proj")


def _mesh_pos():
    return lax.axis_index("x"), lax.axis_index("y"), lax.axis_index("c")


def _other_chips(x, y):
    return [(1 - x, y), (x, 1 - y), (1 - x, 1 - y)]


def _half(rows, core):
    h = rows // 2
    return pl.ds(pl.multiple_of(core * h, 16), h)


def _remote(src, dst, send_sems, recv_sems, k, to):
    return pltpu.make_async_remote_copy(src_ref=src, dst_ref=dst, send_sem=send_sems.at[k], recv_sem=recv_sems.at[k],
                                        device_id=to, device_id_type=MESH)


SEM_SPEC = pl.BlockSpec(memory_space=pltpu.SEMAPHORE)
ANY_SPEC = pl.BlockSpec(memory_space=pl.ANY)
SIDE_EFFECT = pltpu.SideEffectType.DATAFLOW_SIDE_EFFECTING


def _in_hbm(arrays):
    return [pltpu.with_memory_space_constraint(a, pltpu.HBM) for a in arrays]


def copies_start(name, srcs, lands, plan, after=()):
    ns, nl, na = len(srcs), len(lands), len(after)

    def body(*refs):
        src_refs, land_refs = refs[:ns], refs[ns:ns + nl]
        send_sem, recv_sem = refs[ns + nl + na], refs[ns + nl + na + 1]
        token = refs[-1]
        for src, dst, dev in plan(src_refs, land_refs, *_mesh_pos()):
            pltpu.make_async_remote_copy(src_ref=src, dst_ref=dst, send_sem=send_sem, recv_sem=recv_sem,
                                         device_id=dev, device_id_type=MESH).start()
        token[...] = jnp.zeros_like(token)

    out = pl.pallas_call(
        body, name=name,
        in_specs=[HBM_SPEC] * (ns + nl) + [ANY_SPEC] * na,
        out_specs=(SEM_SPEC, SEM_SPEC, *[HBM_SPEC] * (ns + nl), pl.BlockSpec(memory_space=pltpu.VMEM)),
        out_shape=(pltpu.SemaphoreType.DMA(()), pltpu.SemaphoreType.DMA(()),
                   *[pltpu.HBM(a.shape, a.dtype) for a in (*srcs, *lands)], _sds((8, LANES), F32)),
        input_output_aliases={i: 2 + i for i in range(ns + nl)},
        compiler_params=pltpu.CompilerParams(has_side_effects=SIDE_EFFECT),
    )(*_in_hbm(srcs), *_in_hbm(lands), *after)
    return out[0], out[1], list(out[2:2 + ns]), list(out[2 + ns:2 + ns + nl]), out[-1]


def copies_wait(name, started, plan, after=()):
    send_sem, recv_sem, srcs, lands, _ = started
    ns, nl, na = len(srcs), len(lands), len(after)

    def body(*refs):
        src_refs, land_refs = refs[:ns], refs[ns:ns + nl]
        send_sem, recv_sem = refs[ns + nl], refs[ns + nl + 1]
        for src, dst, dev in plan(src_refs, land_refs, *_mesh_pos()):
            cp = pltpu.make_async_remote_copy(src_ref=src, dst_ref=dst, send_sem=send_sem, recv_sem=recv_sem,
                                              device_id=dev, device_id_type=MESH)
            cp.wait_send()
            cp.wait_recv()

    out = pl.pallas_call(
        body, name=name,
        in_specs=[HBM_SPEC] * (ns + nl) + [SEM_SPEC, SEM_SPEC] + [ANY_SPEC] * na,
        out_specs=[HBM_SPEC] * (ns + nl),
        out_shape=[pltpu.HBM(a.shape, a.dtype) for a in (*srcs, *lands)],
        input_output_aliases={i: i for i in range(ns + nl)},
        compiler_params=pltpu.CompilerParams(has_side_effects=SIDE_EFFECT),
    )(*srcs, *lands, send_sem, recv_sem, *after)
    return list(out[:ns]), list(out[ns:])


def _gather_plan(srcs, lands, x, y, c):
    me = 2 * x + y
    return [(src.at[_half(src.shape[0], c)], land.at[me, _half(src.shape[0], c)], (*chip, c))
            for src, land in zip(srcs, lands) for chip in _other_chips(x, y)]


def _gather_arrivals(srcs, lands, x, y, c):
    return [(src.at[_half(src.shape[0], c)], land.at[2 * chip[0] + chip[1], _half(src.shape[0], c)], (*chip, c))
            for src, land in zip(srcs, lands) for chip in _other_chips(x, y)]


def _forward_plan(srcs, lands, x, y, c):
    me, sibling = 2 * x + y, (x, y, 1 - c)
    out = []
    for src, land in zip(srcs, lands):
        out.append((src, land.at[me], sibling))
        for chip in _other_chips(x, y):
            region = land.at[2 * chip[0] + chip[1], _half(src.shape[0], c)]
            out.append((region, region, sibling))
    return out


def _forward_arrivals(srcs, lands, x, y, c):
    me, sibling = 2 * x + y, (x, y, 1 - c)
    out = []
    for src, land in zip(srcs, lands):
        out.append((src, land.at[me], sibling))
        for chip in _other_chips(x, y):
            slot = land.at[2 * chip[0] + chip[1]]
            out.append((slot.at[_half(src.shape[0], c)], slot.at[_half(src.shape[0], 1 - c)], sibling))
    return out


def _join_plan(srcs, lands, x, y, c):
    return [(land.at[_half(land.shape[0], c)], land.at[_half(land.shape[0], c)], (x, y, 1 - c)) for land in lands]


def _join_arrivals(srcs, lands, x, y, c):
    return [(land.at[_half(land.shape[0], c)], land.at[_half(land.shape[0], 1 - c)], (x, y, 1 - c)) for land in lands]


def _pair_plan(srcs, lands, x, y, c):
    return [(src.at[:, _half(src.shape[1], 1 - c)], land, (x, y, 1 - c)) for src, land in zip(srcs, lands)]


def add_own_half(name, core, grads, got):
    n = len(grads)

    def body(core_ref, *refs):
        for i in range(n):
            refs[2 * n + i][...] = (refs[i][...].astype(F32) + refs[n + i][...].astype(F32)).astype(BF)

    def spec(g, own):
        blk = (None, g.shape[1] // 2, g.shape[2])
        return pl.BlockSpec(blk, (lambda j, core_ref: (j, core_ref[0], 0)) if own else (lambda j, core_ref: (j, 0, 0)))

    return pl.pallas_call(
        body, name=name,
        grid_spec=pltpu.PrefetchScalarGridSpec(
            num_scalar_prefetch=1, grid=(N_CHIPS,),
            in_specs=[spec(g, True) for g in grads] + [spec(g, False) for g in grads],
            out_specs=[spec(g, False) for g in grads]),
        out_shape=[_sds(r.shape, BF) for r in got], compiler_params=_params("parallel"),
    )(core, *grads, *got)


def _chips_plan(srcs, lands, x, y, c):
    return [(src.at[2 * chip[0] + chip[1]], land.at[k], (*chip, c))
            for src, land in zip(srcs, lands) for k, chip in enumerate(_other_chips(x, y))]


def sum_chips(name, place, parts, got):
    n = len(got)

    def body(place_ref, *refs):
        for i in range(n):
            acc = refs[i][...].astype(F32)
            for k in range(N_CHIPS - 1):
                acc = acc + refs[n + i][k].astype(F32)
            refs[2 * n + i][...] = acc

    steps = 2
    return pl.pallas_call(
        body, name=name,
        grid_spec=pltpu.PrefetchScalarGridSpec(
            num_scalar_prefetch=1, grid=(steps,),
            in_specs=[pl.BlockSpec((None, g.shape[1] // steps, g.shape[2]), lambda t, place_ref: (place_ref[0], t, 0))
                      for g in parts] +
                     [pl.BlockSpec((N_CHIPS - 1, g.shape[1] // steps, g.shape[2]), lambda t, place_ref: (0, t, 0))
                      for g in got],
            out_specs=[pl.BlockSpec((g.shape[1] // steps, g.shape[2]),
                                    lambda t, place_ref: (place_ref[1] * steps + t, 0)) for g in got]),
        out_shape=[_sds((2 * g.shape[1], g.shape[2]), F32) for g in got], compiler_params=_params("parallel"),
    )(place, *parts, *got)


def reduce_scatter_pair(tag, grads):
    lands = [lax.empty((N_CHIPS, g.shape[1] // 2, g.shape[2]), g.dtype) for g in grads]
    return copies_start(f"{tag}_rs_pair_start", grads, lands, _pair_plan)


def reduce_scatter_begin(tag, core, pairing, after):
    grads, got = copies_wait(f"{tag}_rs_pair_wait", pairing, _pair_plan, after)
    parts = add_own_half(f"{tag}_rs_add", core, grads, got)
    lands = [lax.empty((N_CHIPS - 1,) + p.shape[1:], p.dtype) for p in parts]
    return copies_start(f"{tag}_rs_start", parts, lands, _chips_plan)


def reduce_scatter_sum(tag, place, started, after):
    parts, got = copies_wait(f"{tag}_rs_wait", started, _chips_plan, after)
    return copies_start(f"{tag}_rs_join_start", [], sum_chips(f"{tag}_rs_sum", place, parts, got), _join_plan)


def reduce_scatter_end(tag, joining, after):
    return copies_wait(f"{tag}_rs_join_wait", joining, _join_arrivals, after)[1]


def small_allreduce(name, x):
    R = x.shape[0]
    H = R // 2

    def body(x_ref, o_ref, pair_ref, chip_ref, send_sems, recv_sems):
        xx, yy, c = _mesh_pos()
        me = 2 * xx + yy
        chips = _other_chips(xx, yy)
        sibling = (xx, yy, 1 - c)
        mine = pl.ds(pl.multiple_of(c * H, 8), H)
        theirs = pl.ds(pl.multiple_of((1 - c) * H, 8), H)
        a = _remote(x_ref.at[theirs], pair_ref.at[theirs], send_sems, recv_sems, 0, sibling)
        a.start()
        a.wait_send()
        _remote(x_ref.at[mine], pair_ref.at[mine], send_sems, recv_sems, 0, sibling).wait_recv()
        chip_ref[me] = x_ref[mine, :] + pair_ref[mine, :]
        sends = []
        for k, chip in enumerate(chips):
            cp = _remote(chip_ref.at[me], chip_ref.at[me], send_sems, recv_sems, 1 + k, (*chip, c))
            cp.start()
            sends.append(cp)
        for k, chip in enumerate(chips):
            slot = chip_ref.at[2 * chip[0] + chip[1]]
            _remote(slot, slot, send_sems, recv_sems, 1 + k, (*chip, c)).wait_recv()
        o_ref[mine, :] = (chip_ref[0] + chip_ref[1]) + (chip_ref[2] + chip_ref[3])
        b = _remote(o_ref.at[mine], o_ref.at[mine], send_sems, recv_sems, 4, sibling)
        b.start()
        b.wait_send()
        _remote(o_ref.at[theirs], o_ref.at[theirs], send_sems, recv_sems, 4, sibling).wait_recv()
        for cp in sends:
            cp.wait_send()

    return pl.pallas_call(
        body, name=name,
        in_specs=[pl.BlockSpec(memory_space=pltpu.VMEM)], out_specs=pl.BlockSpec(memory_space=pltpu.VMEM),
        out_shape=_sds((R, LANES), F32),
        scratch_shapes=[pltpu.VMEM((R, LANES), F32), pltpu.VMEM((N_CHIPS, H, LANES), F32),
                        pltpu.SemaphoreType.DMA((5,)), pltpu.SemaphoreType.DMA((5,))],
        compiler_params=pltpu.CompilerParams(vmem_limit_bytes=VMEM_LIMIT),
    )(x)


WEIGHTS = ("norm1_g", "w_in", "conv_w", "q_norm_g", "k_norm_g", "sgu_norm_g", "sgu_w", "sgu_b", "w_out", "norm2_g",
           "w_ff1", "w_ff2", "norm3_g", "w_ple_gate", "w_ple_proj")
SMALL = ("norm1_g", "norm2_g", "norm3_g", "q_norm_g", "k_norm_g", "sgu_norm_g", "sgu_w", "sgu_b", "conv_w")


PACK = 8 * LANES


def _pack_rows(arrays):
    blocks = []
    for a in arrays:
        v = a.reshape(-1)
        blocks.append(jnp.pad(v, (0, (-v.shape[0]) % PACK)).reshape(-1, LANES))
    rows = sum(b.shape[0] for b in blocks)
    if rows % 16:
        blocks.append(jnp.zeros((16 - rows % 16, LANES), F32))
    return jnp.concatenate(blocks, axis=0)


def _unpack_rows(packed, shapes):
    out, pos = [], 0
    flat = packed.reshape(-1)
    for shp in shapes:
        size = math.prod(shp)
        out.append(flat[pos:pos + size].reshape(shp))
        pos += size + (-size) % PACK
    return out


def kernel(x, p, norm1_g, w_in, conv_w, q_norm_g, k_norm_g, sgu_norm_g, sgu_w, sgu_b, w_out, norm2_g, w_ff1, w_ff2, norm3_g, w_ple_gate, w_ple_proj, loss_target, m_norm1_g, m_w_in, m_conv_w, m_q_norm_g, m_k_norm_g, m_sgu_norm_g, m_sgu_w, m_sgu_b, m_w_out, m_norm2_g, m_w_ff1, m_w_ff2, m_norm3_g, m_w_ple_gate, m_w_ple_proj, v_norm1_g, v_w_in, v_conv_w, v_q_norm_g, v_k_norm_g, v_sgu_norm_g, v_sgu_w, v_sgu_b, v_w_out, v_norm2_g, v_w_ff1, v_w_ff2, v_norm3_g, v_w_ple_gate, v_w_ple_proj):
    w = dict(norm1_g=norm1_g, w_in=w_in, conv_w=conv_w, q_norm_g=q_norm_g, k_norm_g=k_norm_g, sgu_norm_g=sgu_norm_g,
             sgu_w=sgu_w, sgu_b=sgu_b, w_out=w_out, norm2_g=norm2_g, w_ff1=w_ff1, w_ff2=w_ff2, norm3_g=norm3_g,
             w_ple_gate=w_ple_gate, w_ple_proj=w_ple_proj)
    m = dict(norm1_g=m_norm1_g, w_in=m_w_in, conv_w=m_conv_w, q_norm_g=m_q_norm_g, k_norm_g=m_k_norm_g,
             sgu_norm_g=m_sgu_norm_g, sgu_w=m_sgu_w, sgu_b=m_sgu_b, w_out=m_w_out, norm2_g=m_norm2_g, w_ff1=m_w_ff1,
             w_ff2=m_w_ff2, norm3_g=m_norm3_g, w_ple_gate=m_w_ple_gate, w_ple_proj=m_w_ple_proj)
    v = dict(norm1_g=v_norm1_g, w_in=v_w_in, conv_w=v_conv_w, q_norm_g=v_q_norm_g, k_norm_g=v_k_norm_g,
             sgu_norm_g=v_sgu_norm_g, sgu_w=v_sgu_w, sgu_b=v_sgu_b, w_out=v_w_out, norm2_g=v_norm2_g, w_ff1=v_w_ff1,
             w_ff2=v_w_ff2, norm3_g=v_norm3_g, w_ple_gate=v_w_ple_gate, w_ple_proj=v_w_ple_proj)
    depth = w_in.shape[0]
    d_model = x.shape[-1]
    chip = 2 * lax.axis_index("x") + lax.axis_index("y")
    core = lax.axis_index("c")
    core_arr = core.reshape(1).astype(jnp.int32)

    cw_cols = conv_w.shape[-1]
    placed = lax.dynamic_update_slice(jnp.zeros((depth, 3, CONV_W), F32), conv_w, (0, 0, chip * cw_cols))
    placed = jnp.where(core == 0, placed, 0.0)
    conv_full = _unpack_rows(small_allreduce("conv_w_gather", _pack_rows([placed])), [(depth, 3, CONV_W)])[0]

    h = x[0]
    p_bf = p[:, 0].astype(BF)
    saved, full = [], []

    def gather_start(l, after):
        shards = [w[n][l].astype(BF) for n in BIG]
        lands = [lax.empty((N_CHIPS,) + s.shape, BF) for s in shards]
        return copies_start(f"l{l}_gather_start", shards, lands, _gather_plan, after)

    def gather_forward(l, started, after):
        shards, lands = copies_wait(f"l{l}_gather_wait", started, _gather_arrivals, after)
        forwarding = copies_start(f"l{l}_forward_start", shards, lands, _forward_plan)
        return forwarding, gather_start(l + 1, (forwarding[-1],)) if l + 1 < depth else None

    forwarding, started = gather_forward(0, gather_start(0, ()), (h,))
    for l in range(depth):
        g_in, g_out, g_ff1, g_ff2, g_gate, g_proj = copies_wait(f"l{l}_forward_wait", forwarding, _forward_arrivals,
                                                                 (h,))[1]
        nxt = {}

        def mid(arr, l=l, started=started, nxt=nxt):
            nxt["forwarding"], nxt["started"] = gather_forward(l + 1, started, (arr,))
            return [t[-1] for t in nxt.values() if t is not None]

        wt = prep_small(norm1_g[l], q_norm_g[l], k_norm_g[l], sgu_norm_g[l], sgu_w[l], sgu_b[l], norm2_g[l], norm3_g[l],
                        conv_full[l])
        wt["w_in"] = jnp.transpose(g_in, (1, 0, 2)).reshape(d_model, -1)
        wt["w_out"] = g_out.reshape(-1, d_model)
        wt["w_ff1"] = g_ff1
        wt["w_ff2"] = g_ff2.reshape(-1, d_model)
        wt["w_ple_gate"] = g_gate.reshape(-1, d_model)
        wt["w_ple_proj"] = g_proj
        last = l + 1 == depth
        h, sv = layer_fwd(f"l{l}", h, p_bf[l], wt, () if last else (started[-1],), None if last else mid)
        if not last:
            forwarding, started = nxt["forwarding"], nxt["started"]
        saved.append(sv)
        full.append(wt)

    loss_tile, dh = loss_head("loss", h, loss_target[0])
    loss = lax.psum(loss_tile[0, 0], ("x", "y", "c"))

    small = [None] * depth
    chip_arr = jnp.stack([chip, core]).astype(jnp.int32)
    big_w, big_m, big_v = ([d[n] for n in BIG] for d in (w, m, v))
    pairing, joining, tokens = None, {}, ()
    for l in reversed(range(depth)):
        box = {}

        def mid(arr, l=l, pairing=pairing, box=box):
            box["started"] = reduce_scatter_begin(f"l{l + 1}", core_arr, pairing, (arr,))
            return (box["started"][-1],)

        dh, g = layer_bwd(f"l{l}", dh, saved[l], full[l], tokens, None if pairing is None else mid)
        tokens = ()
        if pairing is not None:
            joining[l + 1] = reduce_scatter_sum(f"l{l + 1}", chip_arr, box["started"], (dh,))
            tokens = (joining[l + 1][-1],)
        small[l] = small_grads(g)
        shards_in = w_in.shape[-1]
        gl = [jnp.transpose(g["w_in"].reshape(d_model, N_CHIPS, shards_in), (1, 0, 2)),
              g["w_out"].reshape(N_CHIPS, -1, d_model), g["w_ff1"], g["w_ff2"].reshape(N_CHIPS, -1, d_model),
              g["w_ple_gate"].reshape(N_CHIPS, -1, d_model), g["w_ple_proj"]]
        pairing = reduce_scatter_pair(f"l{l}", gl)
        tokens += (pairing[-1],)

    started = reduce_scatter_begin("l0", core_arr, pairing, (dh,))
    updated = None
    for l in reversed(range(1, depth)):
        reduced = reduce_scatter_end(f"l{l}", joining[l], (started[-1],))
        updated = adamw_layer(f"l{l}_adamw", l, big_w, reduced, big_m, big_v, updated)
    grads, delta, new_m, new_v = {}, {}, {}, {}
    packed = _pack_rows([small[l][n] for l in range(depth) for n in SMALL])
    shapes = [small[l][n].shape for l in range(depth) for n in SMALL]
    pieces = _unpack_rows(small_allreduce("small_grads", packed), shapes)
    for i, n in enumerate(SMALL):
        grads[n] = jnp.stack([pieces[l * len(SMALL) + i] for l in range(depth)])
    grads["conv_w"] = lax.dynamic_slice(grads["conv_w"], (0, 0, chip * cw_cols), (depth, 3, cw_cols))
    for n in SMALL:
        shp = w[n].shape
        two_d = (-1, shp[-1]) if n != "sgu_w" else (-1, LANES)
        d, nm, nv = adamw(f"adamw_{n}", w[n].reshape(two_d), grads[n].reshape(two_d), m[n].reshape(two_d),
                          v[n].reshape(two_d))
        delta[n], new_m[n], new_v[n] = d.reshape(shp), nm.reshape(shp), nv.reshape(shp)

    done = [new_v[n] for n in SMALL] + ([] if updated is None else [updated[3][0]])
    reduced = reduce_scatter_end("l0", reduce_scatter_sum("l0", chip_arr, started, (dh, *done)), ())
    updated = adamw_layer("l0_adamw", 0, big_w, reduced, big_m, big_v, updated)
    for k, d in enumerate((grads, delta, new_m, new_v)):
        d.update(zip(BIG, updated[k]))

    return (loss, dh[None], *[grads[n] for n in WEIGHTS], *[delta[n] for n in WEIGHTS], *[new_m[n] for n in WEIGHTS],
            *[new_v[n] for n in WEIGHTS])
```

```python
import math

import jax
import jax.numpy as jnp
from jax import lax
from jax.experimental import pallas as pl
from jax.experimental.pallas import tpu as pltpu

F32 = jnp.float32
BF = jnp.bfloat16
MESH = pl.DeviceIdType.MESH
HIGHEST = lax.Precision.HIGHEST

EPS = 1e-6
HEAD_DIM = 64
CONV_W = 256
ATTN_W = 512
SGU_W = 256
D_MIX = CONV_W + ATTN_W + SGU_W
CHUNK = 128
N_CHIPS = 4
SCALE = HEAD_DIM ** -0.5
LANES = 128
VMEM_LIMIT = 56 * 1024 * 1024

ADAM_LR = 0.001
ADAM_B1 = 0.9
ADAM_B2 = 0.999
ADAM_EPS = 1e-08
ADAM_WD = 0.01
ADAM_STEP = 10

NT_DIMS = (((1,), (1,)), ((), ()))
TN_DIMS = (((0,), (0,)), ((), ()))


def _params(*sem):
    return pltpu.CompilerParams(dimension_semantics=sem if sem else None, vmem_limit_bytes=VMEM_LIMIT)


def _sds(shape, dtype):
    return jax.ShapeDtypeStruct(shape, dtype)


def _erf(x):
    return lax.erf(x)


def _gelu(x):
    return 0.5 * x * (1.0 + _erf(x * (2.0 ** -0.5)))


def _gelu_grad(x):
    return 0.5 * (1.0 + _erf(x * (2.0 ** -0.5))) + x * jnp.exp(-0.5 * x * x) * (1.0 / math.sqrt(2.0 * math.pi))


def _log_sigmoid(z):
    return jnp.minimum(z, 0.0) - jnp.log(1.0 + jnp.exp(-jnp.abs(z)))


def _head_mean_matrix(width):
    r = lax.broadcasted_iota(jnp.int32, (width, width), 0) // HEAD_DIM
    c = lax.broadcasted_iota(jnp.int32, (width, width), 1) // HEAD_DIM
    return (r == c).astype(BF)


def _head_mean(x, m):
    hi = x.astype(BF)
    lo = (x - hi.astype(F32)).astype(BF)
    return _dot2_stacked(hi, lo, m) * (1.0 / HEAD_DIM)


def mm_nn(name, x, w, *, extras=(), pro=None, epi=None, out_dtypes=None, tm=None, tn=512):
    S, K = x.shape
    if w.ndim == 3:
        J, _, tn = w.shape
        N = J * tn
        w_spec = pl.BlockSpec((None, K, tn), lambda n, m: (n, 0, 0))
    else:
        N = w.shape[1]
        tn = min(tn, N)
        w_spec = pl.BlockSpec((K, tn), lambda n, m: (0, n))
    tm = S if tm is None else min(tm, S)
    out_dtypes = (BF,) if out_dtypes is None else out_dtypes
    n_ex, n_out = len(extras), len(out_dtypes)

    def body(x_ref, w_ref, *rest):
        xv = x_ref[...]
        if pro is not None:
            xv = pro(xv)
        acc = jnp.dot(xv.astype(BF), w_ref[...], preferred_element_type=F32)
        outs = (acc,) if epi is None else epi(acc, *[e[...] for e in rest[:n_ex]])
        for o_ref, o in zip(rest[n_ex:], outs):
            o_ref[...] = o.astype(o_ref.dtype)

    tile = pl.BlockSpec((tm, tn), lambda n, m: (m, n))
    out = pl.pallas_call(
        body, name=name, grid=(N // tn, S // tm),
        in_specs=[pl.BlockSpec((tm, K), lambda n, m: (m, 0)), w_spec] + [tile] * n_ex,
        out_specs=[tile] * n_out,
        out_shape=[_sds((S, N), d) for d in out_dtypes],
        compiler_params=_params("parallel", "parallel"),
    )(x, w, *extras)
    return out[0] if n_out == 1 else out


def norm_mm(name, h, g, w, *, tn=512, after=()):
    S, K = h.shape
    if w.ndim == 3:
        J, _, tn = w.shape
        N = J * tn
        w_spec = pl.BlockSpec((None, K, tn), lambda n: (n, 0, 0))
    else:
        N = w.shape[1]
        tn = min(tn, N)
        w_spec = pl.BlockSpec((K, tn), lambda n: (0, n))
    rows = min(S, 256)

    def body(h_ref, g_ref, w_ref, *rest):
        hn_ref, o_ref = rest[-2:]

        @pl.when(pl.program_id(0) == 0)
        def _():
            def chunk(i, _):
                r = pl.ds(pl.multiple_of(i * rows, rows), rows)
                x = h_ref[r, :]
                scale = lax.rsqrt(jnp.mean(x * x, axis=-1, keepdims=True) + EPS)
                hn_ref[r, :] = ((x * scale) * g_ref[...]).astype(BF)
                return 0

            lax.fori_loop(0, S // rows, chunk, 0)

        o_ref[...] = jnp.dot(hn_ref[...], w_ref[...], preferred_element_type=F32).astype(BF)

    whole = pl.BlockSpec((S, K), lambda n: (0, 0))
    return pl.pallas_call(
        body, name=name, grid=(N // tn,),
        in_specs=[pl.BlockSpec((S, K), lambda n: (0, 0), pipeline_mode=pl.Buffered(1)),
                  pl.BlockSpec((1, K), lambda n: (0, 0)), w_spec] + [pl.BlockSpec(memory_space=pl.ANY)] * len(after),
        out_specs=[whole, pl.BlockSpec((S, tn), lambda n: (0, n))],
        out_shape=[_sds((S, K), BF), _sds((S, N), BF)], compiler_params=_params("arbitrary"),
    )(h, g.reshape(1, K), w, *after)


def mm_nt(name, dy, w, *, extras=(), epi=None, tk=512, after=()):
    S, N = dy.shape
    K = w.shape[0]
    tk = min(tk, K)
    n_ex = len(extras)

    def body(dy_ref, w_ref, *rest):
        dyb = rest[-1]

        @pl.when(pl.program_id(0) == 0)
        def _():
            dyb[...] = dy_ref[...].astype(BF)

        acc = lax.dot_general(dyb[...], w_ref[...], NT_DIMS, preferred_element_type=F32)
        if epi is not None:
            acc = epi(acc, *[e[...] for e in rest[:n_ex]])
        rest[-2][...] = acc.astype(BF)

    col = pl.BlockSpec((S, tk), lambda k: (0, k))
    return pl.pallas_call(
        body, name=name, grid=(K // tk,),
        in_specs=[pl.BlockSpec((S, N), lambda k: (0, 0)), pl.BlockSpec((tk, N), lambda k: (k, 0))] + [col] * n_ex +
                 [pl.BlockSpec(memory_space=pl.ANY)] * len(after),
        out_specs=col, out_shape=_sds((S, K), BF), scratch_shapes=[pltpu.VMEM((S, N), BF)],
        compiler_params=_params("arbitrary"),
    )(dy, w, *extras, *after)


def mm_nt_rmsbwd(name, dy, w, h, g, dres, *, tm=256):
    S, N = dy.shape
    D = h.shape[1]
    tm = min(tm, S)
    blocked = w.ndim == 3
    nj = w.shape[2] if blocked else N

    def body(dy_ref, w_ref, h_ref, g_ref, dres_ref, dh_ref, dg_ref):
        i = pl.program_id(0)
        if blocked:
            dyn = None
            for j in range(w.shape[0]):
                part = lax.dot_general(dy_ref[:, j * nj:(j + 1) * nj].astype(BF), w_ref[j], NT_DIMS,
                                       preferred_element_type=F32)
                dyn = part if dyn is None else dyn + part
        else:
            dyn = lax.dot_general(dy_ref[...].astype(BF), w_ref[...], NT_DIMS, preferred_element_type=F32)
        x = h_ref[...]
        r = lax.rsqrt(jnp.mean(x * x, axis=-1, keepdims=True) + EPS)
        t = dyn * g_ref[...]
        dh_ref[...] = dres_ref[...] + r * t - x * (r * r * r) * jnp.mean(t * x, axis=-1, keepdims=True)
        part = jnp.sum(dyn * (x * r), axis=0, keepdims=True)

        @pl.when(i == 0)
        def _():
            dg_ref[...] = part

        @pl.when(i > 0)
        def _():
            dg_ref[...] += part

    w_spec = pl.BlockSpec(w.shape, (lambda i: (0, 0, 0)) if blocked else (lambda i: (0, 0)))
    row = pl.BlockSpec((tm, D), lambda i: (i, 0))
    vec = pl.BlockSpec((1, D), lambda i: (0, 0))
    return pl.pallas_call(
        body, name=name, grid=(S // tm,),
        in_specs=[pl.BlockSpec((tm, N), lambda i: (i, 0)), w_spec, row, vec, row],
        out_specs=[row, vec], out_shape=[_sds((S, D), F32), _sds((1, D), F32)],
        compiler_params=_params("arbitrary"),
    )(dy, w, h, g.reshape(1, D), dres)


def mm_tn(name, x, dy, *, pro_x=None, col_blocks=None, tk=1024, tn=1024):
    S, K = x.shape
    N = dy.shape[1]
    tk = min(tk, K)
    if col_blocks is not None:
        tn = N // col_blocks
        out_shape = _sds((col_blocks, K, tn), BF)
        out_spec = pl.BlockSpec((None, tk, tn), lambda k, n: (n, k, 0))
    else:
        tn = min(tn, N)
        out_shape = _sds((K, N), BF)
        out_spec = pl.BlockSpec((tk, tn), lambda k, n: (k, n))

    def body(x_ref, dy_ref, o_ref):
        xv = x_ref[...]
        if pro_x is not None:
            xv = pro_x(xv)
        o_ref[...] = lax.dot_general(xv.astype(BF), dy_ref[...].astype(BF), TN_DIMS,
                                     preferred_element_type=F32).astype(BF)

    return pl.pallas_call(
        body, name=name, grid=(K // tk, N // tn),
        in_specs=[pl.BlockSpec((S, tk), lambda k, n: (0, k)), pl.BlockSpec((S, tn), lambda k, n: (0, n))],
        out_specs=out_spec, out_shape=out_shape, compiler_params=_params("parallel", "parallel"),
    )(x, dy)


def _conv_parts(ac_ref, ah_ref, cw):
    a_c = ac_ref[...].astype(F32)
    a_h = ah_ref[...].astype(F32)
    x = a_c * a_h
    row = lax.broadcasted_iota(jnp.int32, x.shape, 0)
    x1 = jnp.where(row >= 1, pltpu.roll(x, 1, 0), 0.0)
    x2 = jnp.where(row >= 2, pltpu.roll(x, 2, 0), 0.0)
    cv = cw[0:1] * x2 + cw[1:2] * x1 + cw[2:3] * x
    return a_c, a_h, x, x1, x2, cv, row


def conv_fwd(name, proj, cw):
    S = proj.shape[0]

    def body(ab_ref, ac_ref, ah_ref, cw_ref, o_ref):
        cv = _conv_parts(ac_ref, ah_ref, cw_ref[...])[5]
        o_ref[...] = (ab_ref[...].astype(F32) * cv).astype(BF)

    col = lambda j: pl.BlockSpec((S, CONV_W), lambda i, j=j: (0, j))
    return pl.pallas_call(
        body, name=name, grid=(1,),
        in_specs=[col(0), col(1), col(2), pl.BlockSpec((3, CONV_W), lambda i: (0, 0))],
        out_specs=pl.BlockSpec((S, CONV_W), lambda i: (0, 0)),
        out_shape=_sds((S, D_MIX), BF), compiler_params=_params("arbitrary"),
    )(proj, proj, proj, cw)


def conv_bwd(name, dy, proj, cw):
    S = proj.shape[0]

    def body(dy_ref, ab_ref, ac_ref, ah_ref, cw_ref, dab_ref, dac_ref, dah_ref, dcw_ref):
        w = cw_ref[...]
        a_c, a_h, x, x1, x2, cv, row = _conv_parts(ac_ref, ah_ref, w)
        d = dy_ref[...].astype(F32)
        dab_ref[...] = (d * cv).astype(BF)
        dcv = d * ab_ref[...].astype(F32)
        d1 = jnp.where(row < S - 1, pltpu.roll(dcv, S - 1, 0), 0.0)
        d2 = jnp.where(row < S - 2, pltpu.roll(dcv, S - 2, 0), 0.0)
        dx = w[2:3] * dcv + w[1:2] * d1 + w[0:1] * d2
        dac_ref[...] = (dx * a_h).astype(BF)
        dah_ref[...] = (dx * a_c).astype(BF)
        dcw_ref[0:1, :] = jnp.sum(dcv * x2, axis=0, keepdims=True)
        dcw_ref[1:2, :] = jnp.sum(dcv * x1, axis=0, keepdims=True)
        dcw_ref[2:3, :] = jnp.sum(dcv * x, axis=0, keepdims=True)

    col = lambda j: pl.BlockSpec((S, CONV_W), lambda i, j=j: (0, j))
    one = pl.BlockSpec((S, CONV_W), lambda i: (0, 0))
    small = pl.BlockSpec((3, CONV_W), lambda i: (0, 0))
    return pl.pallas_call(
        body, name=name, grid=(1,),
        in_specs=[col(0), col(0), col(1), col(2), small],
        out_specs=[one, one, one, small],
        out_shape=[_sds((S, CONV_W), BF)] * 3 + [_sds((3, CONV_W), F32)],
        compiler_params=_params("arbitrary"),
    )(dy, proj, proj, proj, cw)


SGU_HEADS = SGU_W // HEAD_DIM
CU_BLOCK = 2304 // SGU_W
CV_BLOCK = 2560 // SGU_W


def _sgu_common(cu_ref, cv_ref, gv_ref):
    c_u = cu_ref[...].astype(F32)
    c_v = cv_ref[...].astype(F32)
    hm = _head_mean_matrix(SGU_W)
    u = _gelu(c_u)
    vg = _gelu(c_v)
    r = lax.rsqrt(_head_mean(vg * vg, hm) + EPS)
    vv = (vg * r) * gv_ref[...]
    head = lax.broadcasted_iota(jnp.int32, (CHUNK, SGU_W), 1) // HEAD_DIM
    tri = (lax.broadcasted_iota(jnp.int32, (CHUNK, CHUNK), 0) >=
           lax.broadcasted_iota(jnp.int32, (CHUNK, CHUNK), 1))
    return c_u, c_v, hm, u, vg, r, vv, head, tri


def _sgu_mix(w_ref, tri, head, vvc, bias):
    sv = bias
    for g in range(SGU_HEADS):
        wg = jnp.where(tri, w_ref[g], 0.0).astype(BF)
        sv = sv + jnp.where(head == g, jnp.dot(wg, vvc, preferred_element_type=F32), 0.0)
    return sv


def sgu_fwd(name, proj, gv, w, bias, y):
    S = proj.shape[0]
    tm = min(S, 512)

    def body(cu_ref, cv_ref, gv_ref, w_ref, b_ref, y_ref, o_ref):
        _, _, _, u, _, _, vv, head, tri = _sgu_common(cu_ref, cv_ref, gv_ref)
        vvb = vv.astype(BF)
        for ch in range(tm // CHUNK):
            rows = slice(ch * CHUNK, (ch + 1) * CHUNK)
            sv = _sgu_mix(w_ref, tri, head, vvb[rows], b_ref[...])
            o_ref[rows, :] = (u[rows] * sv).astype(BF)

    const = lambda shape: pl.BlockSpec(shape, lambda i: (0,) * len(shape))
    return pl.pallas_call(
        body, name=name, grid=(S // tm,),
        in_specs=[pl.BlockSpec((tm, SGU_W), lambda i: (i, CU_BLOCK)), pl.BlockSpec((tm, SGU_W), lambda i: (i, CV_BLOCK)),
                  const((1, SGU_W)), const((SGU_HEADS, CHUNK, CHUNK)), const((CHUNK, SGU_W)),
                  pl.BlockSpec(memory_space=pl.ANY)],
        out_specs=pl.BlockSpec((tm, SGU_W), lambda i: (i, (CONV_W + ATTN_W) // SGU_W)),
        out_shape=_sds(y.shape, BF), input_output_aliases={5: 0}, compiler_params=_params("parallel"),
    )(proj, proj, gv, w, bias, y)


def sgu_bwd(name, dy, proj, gv, w, bias):
    S = proj.shape[0]
    tm = min(S, 512)

    def body(dy_ref, cu_ref, cv_ref, gv_ref, w_ref, b_ref, dcu_ref, dcv_ref, dw_ref, db_ref, dgv_ref, dvv_s):
        i = pl.program_id(0)
        c_u, c_v, hm, u, vg, r, vv, head, tri = _sgu_common(cu_ref, cv_ref, gv_ref)
        vvb = vv.astype(BF)
        d = dy_ref[...].astype(F32)
        ind = (lax.broadcasted_iota(jnp.int32, (SGU_W, LANES), 0) // HEAD_DIM ==
               lax.broadcasted_iota(jnp.int32, (SGU_W, LANES), 1)).astype(BF)
        dw_acc = [jnp.zeros((CHUNK, CHUNK), F32) for _ in range(SGU_HEADS)]
        db_acc = jnp.zeros((CHUNK, LANES), F32)
        for ch in range(tm // CHUNK):
            rows = slice(ch * CHUNK, (ch + 1) * CHUNK)
            sv = _sgu_mix(w_ref, tri, head, vvb[rows], b_ref[...])
            dcu_ref[rows, :] = (d[rows] * sv * _gelu_grad(c_u[rows])).astype(BF)
            dsv = d[rows] * u[rows]
            dsv_hi = dsv.astype(BF)
            db_acc = db_acc + _dot2_stacked(dsv_hi, (dsv - dsv_hi.astype(F32)).astype(BF), ind)
            dvv = jnp.zeros((CHUNK, SGU_W), F32)
            for g in range(SGU_HEADS):
                dsv_g = jnp.where(head == g, dsv, 0.0).astype(BF)
                wg = jnp.where(tri, w_ref[g], 0.0).astype(BF)
                dvv = dvv + lax.dot_general(wg, dsv_g, TN_DIMS, preferred_element_type=F32)
                dw_acc[g] = dw_acc[g] + lax.dot_general(dsv_g, vvb[rows], NT_DIMS, preferred_element_type=F32)
            dvv_s[rows, :] = dvv
        dvv = dvv_s[...]
        gvv = gv_ref[...]
        t = dvv * gvv
        dvg = r * t - vg * (r * r * r) * _head_mean(t * vg, hm)
        dcv_ref[...] = (dvg * _gelu_grad(c_v)).astype(BF)
        dgv = jnp.sum(dvv * (vg * r), axis=0, keepdims=True)

        @pl.when(i == 0)
        def _():
            for g in range(SGU_HEADS):
                dw_ref[g] = jnp.where(tri, dw_acc[g], 0.0)
            db_ref[...] = db_acc
            dgv_ref[...] = dgv

        @pl.when(i > 0)
        def _():
            for g in range(SGU_HEADS):
                dw_ref[g] += jnp.where(tri, dw_acc[g], 0.0)
            db_ref[...] += db_acc
            dgv_ref[...] += dgv

    const = lambda shape: pl.BlockSpec(shape, lambda i: (0,) * len(shape))
    tile = pl.BlockSpec((tm, SGU_W), lambda i: (i, 0))
    return pl.pallas_call(
        body, name=name, grid=(S // tm,),
        in_specs=[pl.BlockSpec((tm, SGU_W), lambda i: (i, 3)),
                  pl.BlockSpec((tm, SGU_W), lambda i: (i, CU_BLOCK)), pl.BlockSpec((tm, SGU_W), lambda i: (i, CV_BLOCK)),
                  const((1, SGU_W)), const((SGU_HEADS, CHUNK, CHUNK)), const((CHUNK, SGU_W))],
        out_specs=[tile, tile, const((SGU_HEADS, CHUNK, CHUNK)), const((CHUNK, LANES)), const((1, SGU_W))],
        out_shape=[_sds((S, SGU_W), BF), _sds((S, SGU_W), BF), _sds((SGU_HEADS, CHUNK, CHUNK), F32),
                   _sds((CHUNK, LANES), F32), _sds((1, SGU_W), F32)],
        scratch_shapes=[pltpu.VMEM((tm, SGU_W), F32)],
        compiler_params=_params("arbitrary"),
    )(dy, proj, proj, gv, w, bias)


HEAD_PAIRS = ATTN_W // LANES
Q_BLOCK0 = 768 // LANES
K_BLOCK0 = 1280 // LANES
V_BLOCK0 = 1792 // LANES


def _attn_tile(S):
    return min(S, 256)


def _qk_norm(x, g, hm):
    r = lax.rsqrt(_head_mean(x * x, hm) + EPS)
    return r, (x * r) * g


MASKED = -1e30


def _logit_parts(z):
    lb = _log_sigmoid(z)
    lr = lb - z
    hi = lr.astype(BF)
    return lb, hi, (lr - hi.astype(F32)).astype(BF)


def _stack_heads(x, lane):
    return jnp.concatenate([jnp.where(lane < HEAD_DIM, x, 0.0), jnp.where(lane >= HEAD_DIM, x, 0.0)],
                           axis=0).astype(BF)


def _dot2_stacked(hi, lo, u):
    rows = hi.shape[0]
    both = jnp.dot(jnp.concatenate([hi, lo], axis=0), u, preferred_element_type=F32)
    return both[:rows] + both[rows:]


def attn_fwd(name, proj, gq, gk, y):
    S = proj.shape[0]
    T = _attn_tile(S)
    nq = S // T

    def body(q_ref, k_ref, v_ref, gq_ref, gk_ref, y_ref, o_ref, tot_ref, kn_s, lb_s, hi_s, lo_s, z_s, a_s, o_s):
        qi = pl.program_id(1)
        hm = _head_mean_matrix(LANES)

        @pl.when(qi == 0)
        def _():
            kn_s[...] = _qk_norm(k_ref[...].astype(F32), gk_ref[...], hm)[1].astype(BF)

        qn = _qk_norm(q_ref[...].astype(F32), gq_ref[...], hm)[1]
        lane = lax.broadcasted_iota(jnp.int32, (T, LANES), 1)
        qst = _stack_heads(qn, lane)
        rowi = lax.broadcasted_iota(jnp.int32, (T, T), 0)
        coli = lax.broadcasted_iota(jnp.int32, (T, T), 1)
        u_excl = (rowi > coli).astype(BF)
        diagonal = jnp.where(coli < rowi, 0.0, MASKED)

        def logits(j):
            return lax.dot_general(qst, kn_s[pl.ds(pl.multiple_of(j * T, T), T), :], NT_DIMS,
                                   preferred_element_type=F32)

        def values(j):
            return v_ref[pl.ds(pl.multiple_of(j * T, T), T), :].astype(BF)

        def keep(slot, z):
            lb_s[slot], hi_s[...], lo_s[...] = _logit_parts(z)

        def step(it, carry):
            run = carry
            j = qi - it
            hi, lo = hi_s[...], lo_s[...]
            both = jnp.dot(jnp.concatenate([hi, lo], axis=0), u_excl, preferred_element_type=F32)
            o_s[...] += jnp.dot(a_s[...], values(jnp.minimum(j + 1, qi)), preferred_element_type=F32)
            z_after = logits(jnp.maximum(j - 2, 0))
            first = hi[:, 0:1].astype(F32) + lo[:, 0:1].astype(F32)
            keep((it + 1) % 2, z_s[...])
            later = both[:2 * T] + both[2 * T:]
            a_s[...] = jnp.exp(lb_s[it % 2] + later + run).astype(BF)
            z_s[...] = z_after
            return run + later[:, 0:1] + first

        keep(0, logits(qi) + jnp.concatenate([diagonal, diagonal], axis=0))
        z_s[...] = logits(jnp.maximum(qi - 1, 0))
        a_s[...] = jnp.zeros_like(a_s)
        o_s[...] = jnp.zeros_like(o_s)
        run = lax.fori_loop(0, qi, step, jnp.zeros((2 * T, 1), F32))
        hi, lo = hi_s[...], lo_s[...]
        both = jnp.dot(jnp.concatenate([hi, lo], axis=0), u_excl, preferred_element_type=F32)
        o = o_s[...] + jnp.dot(a_s[...], values(jnp.minimum(1, qi)), preferred_element_type=F32)
        later = both[:2 * T] + both[2 * T:]
        a = jnp.exp(lb_s[qi % 2] + later + run).astype(BF)
        run = run + later[:, 0:1] + (hi[:, 0:1].astype(F32) + lo[:, 0:1].astype(F32))
        o = o + jnp.dot(a, values(0), preferred_element_type=F32)
        o_ref[...] = jnp.where(lane < HEAD_DIM, o[:T], o[T:]).astype(BF)
        tot_ref[...] = jnp.where(lane < HEAD_DIM, run[:T], run[T:])

    gain = pl.BlockSpec((1, LANES), lambda hp, qi: (0, 0))
    full = lambda b0: pl.BlockSpec((S, LANES), lambda hp, qi, b0=b0: (0, b0 + hp))
    tile = pl.BlockSpec((T, LANES), lambda hp, qi: (qi, hp))
    return pl.pallas_call(
        body, name=name, grid=(HEAD_PAIRS, nq),
        in_specs=[pl.BlockSpec((T, LANES), lambda hp, qi: (qi, Q_BLOCK0 + hp)), full(K_BLOCK0), full(V_BLOCK0), gain, gain,
                  pl.BlockSpec(memory_space=pl.ANY)],
        out_specs=[pl.BlockSpec((T, LANES), lambda hp, qi: (qi, CONV_W // LANES + hp)), tile],
        out_shape=[_sds(y.shape, BF), _sds((S, ATTN_W), F32)], input_output_aliases={5: 0},
        scratch_shapes=[pltpu.VMEM((S, LANES), BF), pltpu.VMEM((2, 2 * T, T), F32), pltpu.VMEM((2 * T, T), BF),
                        pltpu.VMEM((2 * T, T), BF), pltpu.VMEM((2 * T, T), F32), pltpu.VMEM((2 * T, T), BF),
                        pltpu.VMEM((2 * T, LANES), F32)],
        compiler_params=_params("arbitrary", "arbitrary"),
    )(proj, proj, proj, gq, gk, y)


def attn_bwd(name, dy, proj, tot, gq, gk):
    S = proj.shape[0]
    T = _attn_tile(S)
    nq = S // T

    def body(q_ref, k_ref, v_ref, tot_ref, do_ref, gq_ref, gk_ref,
             dq_ref, dk_ref, dv_ref, dgq_ref, dgk_ref, kn_s, dkn_s, dv_s,
             lb_s, z_s, g_s, dq_s, hi_s, lo_s, a_s, ghi_s, glo_s):
        hp = pl.program_id(0)
        qi = pl.program_id(1)
        hm = _head_mean_matrix(LANES)

        @pl.when(qi == 0)
        def _():
            kn_s[...] = _qk_norm(k_ref[...].astype(F32), gk_ref[...], hm)[1].astype(BF)
            dkn_s[...] = jnp.zeros_like(dkn_s)
            dv_s[...] = jnp.zeros_like(dv_s)

        q = q_ref[...].astype(F32)
        rq, qn = _qk_norm(q, gq_ref[...], hm)
        lane = lax.broadcasted_iota(jnp.int32, (T, LANES), 1)
        qst = _stack_heads(qn, lane)
        dost = _stack_heads(do_ref[...].astype(F32), lane)
        total = jnp.concatenate([tot_ref[:, 0:1], tot_ref[:, HEAD_DIM:HEAD_DIM + 1]], axis=0)
        rowi = lax.broadcasted_iota(jnp.int32, (T, T), 0)
        coli = lax.broadcasted_iota(jnp.int32, (T, T), 1)
        u_upto = (rowi <= coli).astype(BF)
        u_before = (rowi < coli).astype(BF)
        diagonal = jnp.where(coli < rowi, 0.0, MASKED)

        on_diagonal = jnp.concatenate([diagonal, diagonal], axis=0)

        def rows(b):
            return pl.ds(pl.multiple_of(jnp.clip(b, 0, qi) * T, T), T)

        def logits(b):
            return lax.dot_general(qst, kn_s[rows(b), :], NT_DIMS, preferred_element_type=F32)

        def keep(b, z):
            bias = jnp.where(b == qi, on_diagonal, jnp.where(b > qi, MASKED, 0.0))
            lb_s[b % 3], hi_s[...], lo_s[...] = _logit_parts(z + bias)

        def step(i, carry):
            run, grun = carry
            both_before = jnp.dot(jnp.concatenate([ghi_s[...], glo_s[...]], axis=0), u_before,
                                  preferred_element_type=F32)
            both_upto = jnp.dot(jnp.concatenate([hi_s[...], lo_s[...]], axis=0), u_upto, preferred_element_type=F32)
            da = lax.dot_general(dost, v_ref[rows(i), :].astype(BF), NT_DIMS, preferred_element_type=F32)
            dv_s[rows(i - 1), :] += lax.dot_general(a_s[...], dost, TN_DIMS, preferred_element_type=F32)
            z_after = logits(i + 2)

            keep(i + 1, z_s[...])

            g = g_s[...]
            before = both_before[:2 * T] + both_before[2 * T:]
            dz = (g - jnp.exp(lb_s[(i + 2) % 3]) * (g + (grun + before))).astype(BF)
            dq_s[...] += jnp.dot(dz, kn_s[rows(i - 1), :], preferred_element_type=F32)
            dkn_s[rows(i - 1), :] += lax.dot_general(dz, qst, TN_DIMS, preferred_element_type=F32)
            grun = grun + before[:, T - 1:T] + g[:, T - 1:T]

            upto = both_upto[:2 * T] + both_upto[2 * T:]
            a = jnp.exp(lb_s[i % 3] + (total - run - upto))
            g = da * a
            a_s[...] = a.astype(BF)
            g_s[...] = g
            ghi = g.astype(BF)
            ghi_s[...] = ghi
            glo_s[...] = (g - ghi.astype(F32)).astype(BF)
            z_s[...] = z_after
            return run + upto[:, T - 1:T], grun

        lb_s[...] = jnp.full(lb_s.shape, MASKED, F32)
        for ref in (a_s, g_s, ghi_s, glo_s, dq_s):
            ref[...] = jnp.zeros_like(ref)
        keep(0, logits(0))
        z_s[...] = logits(1)
        _, grun = lax.fori_loop(0, qi + 1, step, (jnp.zeros((2 * T, 1), F32), jnp.zeros((2 * T, 1), F32)))
        both_before = jnp.dot(jnp.concatenate([ghi_s[...], glo_s[...]], axis=0), u_before, preferred_element_type=F32)
        dv_s[rows(qi), :] += lax.dot_general(a_s[...], dost, TN_DIMS, preferred_element_type=F32)
        g = g_s[...]
        before = both_before[:2 * T] + both_before[2 * T:]
        dz = (g - jnp.exp(lb_s[qi % 3]) * (g + (grun + before))).astype(BF)
        dkn_s[rows(qi), :] += lax.dot_general(dz, qst, TN_DIMS, preferred_element_type=F32)
        dq_all = dq_s[...] + jnp.dot(dz, kn_s[rows(qi), :], preferred_element_type=F32)
        dqn = jnp.where(lane < HEAD_DIM, dq_all[:T], dq_all[T:])
        gq_v = gq_ref[...]
        t = dqn * gq_v
        dq_ref[...] = (rq * t - q * (rq * rq * rq) * _head_mean(t * q, hm)).astype(BF)
        dgq = jnp.sum(dqn * (q * rq), axis=0, keepdims=True) * SCALE
        first = jnp.logical_and(hp == 0, qi == 0)

        @pl.when(first)
        def _():
            dgq_ref[...] = dgq

        @pl.when(jnp.logical_not(first))
        def _():
            dgq_ref[...] += dgq

        @pl.when(qi == nq - 1)
        def _():
            k = k_ref[...].astype(F32)
            rk = _qk_norm(k, gk_ref[...], hm)[0]
            dkn = dkn_s[...]
            tk = dkn * gk_ref[...]
            dk_ref[...] = (rk * tk - k * (rk * rk * rk) * _head_mean(tk * k, hm)).astype(BF)
            dgk = jnp.sum(dkn * (k * rk), axis=0, keepdims=True)
            dv_ref[...] = dv_s[...].astype(BF)

            @pl.when(hp == 0)
            def _():
                dgk_ref[...] = dgk

            @pl.when(hp > 0)
            def _():
                dgk_ref[...] += dgk

            @pl.when(hp == HEAD_PAIRS - 1)
            def _():
                fold = (lax.broadcasted_iota(jnp.int32, (LANES, LANES), 0) % HEAD_DIM ==
                        lax.broadcasted_iota(jnp.int32, (LANES, LANES), 1) % HEAD_DIM).astype(F32)
                dgq_ref[...] = jnp.dot(dgq_ref[...], fold, precision=HIGHEST, preferred_element_type=F32)
                dgk_ref[...] = jnp.dot(dgk_ref[...], fold, precision=HIGHEST, preferred_element_type=F32)

    gain = pl.BlockSpec((1, LANES), lambda hp, qi: (0, 0))
    full = lambda b0: pl.BlockSpec((S, LANES), lambda hp, qi, b0=b0: (0, b0 + hp))
    tile = pl.BlockSpec((T, LANES), lambda hp, qi: (qi, hp))
    col = pl.BlockSpec((S, LANES), lambda hp, qi: (0, hp))
    dgain = pl.BlockSpec((1, LANES), lambda hp, qi: (0, 0))
    return pl.pallas_call(
        body, name=name, grid=(HEAD_PAIRS, nq),
        in_specs=[pl.BlockSpec((T, LANES), lambda hp, qi: (qi, Q_BLOCK0 + hp)), full(K_BLOCK0), full(V_BLOCK0),
                  tile, pl.BlockSpec((T, LANES), lambda hp, qi: (qi, 2 + hp)), gain, gain],
        out_specs=[tile, col, col, dgain, dgain],
        out_shape=[_sds((S, ATTN_W), BF)] * 3 + [_sds((1, LANES), F32)] * 2,
        scratch_shapes=[pltpu.VMEM((S, LANES), BF), pltpu.VMEM((S, LANES), F32), pltpu.VMEM((S, LANES), F32),
                        pltpu.VMEM((3, 2 * T, T), F32), pltpu.VMEM((2 * T, T), F32), pltpu.VMEM((2 * T, T), F32),
                        pltpu.VMEM((2 * T, LANES), F32)] + [pltpu.VMEM((2 * T, T), BF)] * 5,
        compiler_params=_params("arbitrary", "arbitrary"),
    )(proj, proj, proj, tot, dy, gq, gk)


def ple_bwd(name, dh, gp, pp, w, h, g, after=()):
    S, D = dh.shape
    tm = min(S, 256)

    def body(dh_ref, gp_ref, pp_ref, w_ref, h_ref, g_ref, *rest):
        dgp_ref, dpp_ref, dh2_ref, dg_ref = rest[-4:]
        i = pl.program_id(0)
        d = dh_ref[...]
        gate = jax.nn.sigmoid(gp_ref[...].astype(F32))
        dpp_ref[...] = (d * gate).astype(BF)
        dgp = (d * pp_ref[...].astype(F32) * gate * (1.0 - gate)).astype(BF)
        dgp_ref[...] = dgp
        dyn = lax.dot_general(dgp, w_ref[...], NT_DIMS, preferred_element_type=F32)
        x = h_ref[...]
        r = lax.rsqrt(jnp.mean(x * x, axis=-1, keepdims=True) + EPS)
        t = dyn * g_ref[...]
        dh2_ref[...] = d + r * t - x * (r * r * r) * jnp.mean(t * x, axis=-1, keepdims=True)
        part = jnp.sum(dyn * (x * r), axis=0, keepdims=True)

        @pl.when(i == 0)
        def _():
            dg_ref[...] = part

        @pl.when(i > 0)
        def _():
            dg_ref[...] += part

    tile = pl.BlockSpec((tm, D), lambda i: (i, 0))
    vec = pl.BlockSpec((1, D), lambda i: (0, 0))
    return pl.pallas_call(
        body, name=name, grid=(S // tm,),
        in_specs=[tile, tile, tile, pl.BlockSpec((D, D), lambda i: (0, 0)), tile, vec] +
                 [pl.BlockSpec(memory_space=pl.ANY)] * len(after),
        out_specs=[tile, tile, tile, vec],
        out_shape=[_sds((S, D), BF), _sds((S, D), BF), _sds((S, D), F32), _sds((1, D), F32)],
        compiler_params=_params("arbitrary"),
    )(dh, gp, pp, w, h, g.reshape(1, D), *after)


def loss_head(name, h, target):
    S, D = h.shape
    tm = min(S, 512)

    def body(h_ref, t_ref, loss_ref, dh_ref):
        i = pl.program_id(0)
        e = h_ref[...] - t_ref[...]
        dh_ref[...] = e * (1.0 / D)
        part = jnp.zeros((8, LANES), F32) + 0.5 * jnp.sum(jnp.mean(e * e, axis=-1, keepdims=True))

        @pl.when(i == 0)
        def _():
            loss_ref[...] = part

        @pl.when(i > 0)
        def _():
            loss_ref[...] += part

    tile = pl.BlockSpec((tm, D), lambda i: (i, 0))
    return pl.pallas_call(
        body, name=name, grid=(S // tm,), in_specs=[tile, tile],
        out_specs=[pl.BlockSpec((8, LANES), lambda i: (0, 0)), tile],
        out_shape=[_sds((8, LANES), F32), _sds((S, D), F32)], compiler_params=_params("arbitrary"),
    )(h, target)


def _adamw_math(w, g, m, v):
    c1 = 1.0 - ADAM_B1 ** ADAM_STEP
    c2 = 1.0 - ADAM_B2 ** ADAM_STEP
    nm = ADAM_B1 * m + (1.0 - ADAM_B1) * g
    nv = ADAM_B2 * v + (1.0 - ADAM_B2) * (g * g)
    return -ADAM_LR * ((nm / c1) / (jnp.sqrt(nv / c2) + ADAM_EPS) + ADAM_WD * w), nm, nv


def adamw(name, w, g, m, v):
    R, C = w.shape
    tr = R
    for cand in (512, 256, 128, 64, 32, 16, 8):
        if R % cand == 0:
            tr = cand
            break

    def body(w_ref, g_ref, m_ref, v_ref, d_ref, nm_ref, nv_ref):
        d_ref[...], nm_ref[...], nv_ref[...] = _adamw_math(w_ref[...], g_ref[...], m_ref[...], v_ref[...])

    tile = pl.BlockSpec((tr, C), lambda i: (i, 0))
    return pl.pallas_call(
        body, name=name, grid=(R // tr,), in_specs=[tile] * 4, out_specs=[tile] * 3,
        out_shape=[_sds((R, C), F32)] * 3, compiler_params=_params("parallel"),
    )(w, g, m, v)


def adamw_layer(name, layer, ws, gs, ms, vs, prev, after=()):
    n = len(ws)
    steps = 8

    def body(*refs):
        ins, outs = refs[:4 * n], refs[-4 * n:]
        for i in range(n):
            w_ref, g_ref, m_ref, v_ref = (ins[k * n + i] for k in range(4))
            g = g_ref[...]
            outs[i][...] = g
            outs[n + i][...], outs[2 * n + i][...], outs[3 * n + i][...] = _adamw_math(w_ref[...], g, m_ref[...],
                                                                                        v_ref[...])

    def stacked(a):
        return pl.BlockSpec((None, a.shape[1] // steps, a.shape[2]), lambda t: (layer, t, 0))

    def flat(a):
        return pl.BlockSpec((a.shape[0] // steps, a.shape[1]), lambda t: (t, 0))

    in_specs = [stacked(a) for a in ws] + [flat(a) for a in gs] + [stacked(a) for a in ms] + [stacked(a) for a in vs]
    operands = [*ws, *gs, *ms, *vs]
    aliases = {}
    if prev is not None:
        flat_prev = [a for group in prev for a in group]
        in_specs += [pl.BlockSpec(memory_space=pl.ANY)] * len(flat_prev)
        aliases = {4 * n + i: i for i in range(4 * n)}
        operands += flat_prev
    in_specs += [pl.BlockSpec(memory_space=pl.ANY)] * len(after)
    operands += list(after)
    out = pl.pallas_call(
        body, name=name, grid=(steps,), in_specs=in_specs, out_specs=[stacked(a) for a in ws] * 4,
        out_shape=[_sds(a.shape, F32) for a in ws] * 4, input_output_aliases=aliases,
        compiler_params=_params("parallel"),
    )(*operands)
    return [list(out[k * n:(k + 1) * n]) for k in range(4)]


def _relu2(u):
    r = jnp.maximum(u.astype(F32), 0.0)
    return r * r


def layer_fwd(tag, h0, p_bf, wt, after=(), mid=None):
    hn1, proj = norm_mm(f"{tag}_proj", h0, wt["norm1_g"], wt["w_in"], tn=256, after=after)
    y = conv_fwd(f"{tag}_conv", proj, wt["conv_w"])
    y, yb_tot = attn_fwd(f"{tag}_attn", proj, wt["gq"], wt["gk"], y)
    y = sgu_fwd(f"{tag}_sgu", proj, wt["gv"], wt["sgu_w"], wt["sgu_bias"], y)
    h1 = mm_nn(f"{tag}_out", y, wt["w_out"], extras=(h0,), epi=lambda acc, h: (h + acc,), out_dtypes=(F32,))
    hn2, uu = norm_mm(f"{tag}_ff1", h1, wt["norm2_g"], wt["w_ff1"], after=() if mid is None else mid(yb_tot))
    h2 = mm_nn(f"{tag}_ff2", uu, wt["w_ff2"], pro=_relu2, extras=(h1,), epi=lambda acc, h: (h + acc,),
               out_dtypes=(F32,), tm=512)
    hn3, gp = norm_mm(f"{tag}_gate", h2, wt["norm3_g"], wt["w_ple_gate"])
    h3, pp = mm_nn(f"{tag}_ple", p_bf, wt["w_ple_proj"], extras=(gp, h2),
                   epi=lambda acc, g, h: (h + jax.nn.sigmoid(g.astype(F32)) * acc, acc), out_dtypes=(F32, BF))
    saved = dict(h0=h0, h1=h1, h2=h2, hn1=hn1, hn2=hn2, hn3=hn3, proj=proj, yb_tot=yb_tot, y=y, uu=uu, gp=gp, pp=pp,
                 p_bf=p_bf)
    return h3, saved


def layer_bwd(tag, dh3, sv, wt, after=(), mid=None):
    g = {}
    dgp, dpp, dh2, g["norm3_g"] = ple_bwd(f"{tag}_dple", dh3, sv["gp"], sv["pp"], wt["w_ple_gate"], sv["h2"],
                                          wt["norm3_g"], after)
    g["w_ple_proj"] = mm_tn(f"{tag}_dwp", sv["p_bf"], dpp, col_blocks=N_CHIPS)
    g["w_ple_gate"] = mm_tn(f"{tag}_dwg", sv["hn3"], dgp, tn=256)

    duu = mm_nt(f"{tag}_dff2", dh2, wt["w_ff2"], extras=(sv["uu"],),
                epi=lambda acc, u: acc * (2.0 * jnp.maximum(u.astype(F32), 0.0)))
    g["w_ff2"] = mm_tn(f"{tag}_dw2", sv["uu"], dh2, pro_x=_relu2)
    g["w_ff1"] = mm_tn(f"{tag}_dw1", sv["hn2"], duu, col_blocks=N_CHIPS)
    dh1, g["norm2_g"] = mm_nt_rmsbwd(f"{tag}_dnorm2", duu, wt["w_ff1"], sv["h1"], wt["norm2_g"], dh2)

    dy = mm_nt(f"{tag}_dout", dh1, wt["w_out"], after=() if mid is None else mid(dh1))
    g["w_out"] = mm_tn(f"{tag}_dwo", sv["y"], dh1, tn=256)
    dab, dac, dah, g["conv_w"] = conv_bwd(f"{tag}_dconv", dy, sv["proj"], wt["conv_w"])
    dq, dk, dv, g["gq"], g["gk"] = attn_bwd(f"{tag}_dattn", dy, sv["proj"], sv["yb_tot"], wt["gq"], wt["gk"])
    dcu, dcv, g["sgu_w"], g["sgu_bias"], g["gv"] = sgu_bwd(f"{tag}_dsgu", dy, sv["proj"], wt["gv"], wt["sgu_w"],
                                                           wt["sgu_bias"])
    dproj = jnp.concatenate([dab, dac, dah, dq, dk, dv, dcu, dcv], axis=-1)
    g["w_in"] = mm_tn(f"{tag}_dwi", sv["hn1"], dproj, tn=1408)
    dh0, g["norm1_g"] = mm_nt_rmsbwd(f"{tag}_dnorm1", dproj, wt["w_in"], sv["h0"], wt["norm1_g"], dh1)
    return dh0, g


def prep_small(norm1_g, q_norm_g, k_norm_g, sgu_norm_g, sgu_w, sgu_b, norm2_g, norm3_g, conv_w_full):
    return dict(
        norm1_g=norm1_g, norm2_g=norm2_g, norm3_g=norm3_g, conv_w=conv_w_full,
        gq=(jnp.tile(q_norm_g, 2) * SCALE).reshape(1, LANES), gk=jnp.tile(k_norm_g, 2).reshape(1, LANES),
        gv=sgu_norm_g.reshape(1, SGU_W), sgu_w=sgu_w, sgu_bias=jnp.repeat(sgu_b.T, HEAD_DIM, axis=1))


def small_grads(g):
    return dict(
        norm1_g=g["norm1_g"][0], norm2_g=g["norm2_g"][0], norm3_g=g["norm3_g"][0], conv_w=g["conv_w"],
        q_norm_g=g["gq"][0, :HEAD_DIM], k_norm_g=g["gk"][0, :HEAD_DIM], sgu_norm_g=g["gv"][0], sgu_w=g["sgu_w"],
        sgu_b=g["sgu_bias"][:, :SGU_HEADS].T)


HBM_SPEC = pl.BlockSpec(memory_space=pltpu.HBM)
BIG = ("w_in", "w_out", "w_ff1", "w_ff2", "w_ple_gate", "w_ple_proj")


def _mesh_pos():
    return lax.axis_index("x"), lax.axis_index("y"), lax.axis_index("c")


def _other_chips(x, y):
    return [(1 - x, y), (x, 1 - y), (1 - x, 1 - y)]


def _half(rows, core):
    h = rows // 2
    return pl.ds(pl.multiple_of(core * h, 16), h)


def _remote(src, dst, send_sems, recv_sems, k, to):
    return pltpu.make_async_remote_copy(src_ref=src, dst_ref=dst, send_sem=send_sems.at[k], recv_sem=recv_sems.at[k],
                                        device_id=to, device_id_type=MESH)


SEM_SPEC = pl.BlockSpec(memory_space=pltpu.SEMAPHORE)
ANY_SPEC = pl.BlockSpec(memory_space=pl.ANY)
SIDE_EFFECT = pltpu.SideEffectType.DATAFLOW_SIDE_EFFECTING


def _in_hbm(arrays):
    return [pltpu.with_memory_space_constraint(a, pltpu.HBM) for a in arrays]


def copies_start(name, srcs, lands, plan, after=()):
    ns, nl, na = len(srcs), len(lands), len(after)

    def body(*refs):
        src_refs, land_refs = refs[:ns], refs[ns:ns + nl]
        send_sem, recv_sem = refs[ns + nl + na], refs[ns + nl + na + 1]
        token = refs[-1]
        for src, dst, dev in plan(src_refs, land_refs, *_mesh_pos()):
            pltpu.make_async_remote_copy(src_ref=src, dst_ref=dst, send_sem=send_sem, recv_sem=recv_sem,
                                         device_id=dev, device_id_type=MESH).start()
        token[...] = jnp.zeros_like(token)

    out = pl.pallas_call(
        body, name=name,
        in_specs=[HBM_SPEC] * (ns + nl) + [ANY_SPEC] * na,
        out_specs=(SEM_SPEC, SEM_SPEC, *[HBM_SPEC] * (ns + nl), pl.BlockSpec(memory_space=pltpu.VMEM)),
        out_shape=(pltpu.SemaphoreType.DMA(()), pltpu.SemaphoreType.DMA(()),
                   *[pltpu.HBM(a.shape, a.dtype) for a in (*srcs, *lands)], _sds((8, LANES), F32)),
        input_output_aliases={i: 2 + i for i in range(ns + nl)},
        compiler_params=pltpu.CompilerParams(has_side_effects=SIDE_EFFECT),
    )(*_in_hbm(srcs), *_in_hbm(lands), *after)
    return out[0], out[1], list(out[2:2 + ns]), list(out[2 + ns:2 + ns + nl]), out[-1]


def copies_wait(name, started, plan, after=()):
    send_sem, recv_sem, srcs, lands, _ = started
    ns, nl, na = len(srcs), len(lands), len(after)

    def body(*refs):
        src_refs, land_refs = refs[:ns], refs[ns:ns + nl]
        send_sem, recv_sem = refs[ns + nl], refs[ns + nl + 1]
        for src, dst, dev in plan(src_refs, land_refs, *_mesh_pos()):
            cp = pltpu.make_async_remote_copy(src_ref=src, dst_ref=dst, send_sem=send_sem, recv_sem=recv_sem,
                                              device_id=dev, device_id_type=MESH)
            cp.wait_send()
            cp.wait_recv()

    out = pl.pallas_call(
        body, name=name,
        in_specs=[HBM_SPEC] * (ns + nl) + [SEM_SPEC, SEM_SPEC] + [ANY_SPEC] * na,
        out_specs=[HBM_SPEC] * (ns + nl),
        out_shape=[pltpu.HBM(a.shape, a.dtype) for a in (*srcs, *lands)],
        input_output_aliases={i: i for i in range(ns + nl)},
        compiler_params=pltpu.CompilerParams(has_side_effects=SIDE_EFFECT),
    )(*srcs, *lands, send_sem, recv_sem, *after)
    return list(out[:ns]), list(out[ns:])


def _gather_plan(srcs, lands, x, y, c):
    me = 2 * x + y
    return [(src.at[_half(src.shape[0], c)], land.at[me, _half(src.shape[0], c)], (*chip, c))
            for src, land in zip(srcs, lands) for chip in _other_chips(x, y)]


def _gather_arrivals(srcs, lands, x, y, c):
    return [(src.at[_half(src.shape[0], c)], land.at[2 * chip[0] + chip[1], _half(src.shape[0], c)], (*chip, c))
            for src, land in zip(srcs, lands) for chip in _other_chips(x, y)]


def _forward_plan(srcs, lands, x, y, c):
    me, sibling = 2 * x + y, (x, y, 1 - c)
    out = []
    for src, land in zip(srcs, lands):
        out.append((src, land.at[me], sibling))
        for chip in _other_chips(x, y):
            region = land.at[2 * chip[0] + chip[1], _half(src.shape[0], c)]
            out.append((region, region, sibling))
    return out


def _forward_arrivals(srcs, lands, x, y, c):
    me, sibling = 2 * x + y, (x, y, 1 - c)
    out = []
    for src, land in zip(srcs, lands):
        out.append((src, land.at[me], sibling))
        for chip in _other_chips(x, y):
            slot = land.at[2 * chip[0] + chip[1]]
            out.append((slot.at[_half(src.shape[0], c)], slot.at[_half(src.shape[0], 1 - c)], sibling))
    return out


def _join_plan(srcs, lands, x, y, c):
    return [(land.at[_half(land.shape[0], c)], land.at[_half(land.shape[0], c)], (x, y, 1 - c)) for land in lands]


def _join_arrivals(srcs, lands, x, y, c):
    return [(land.at[_half(land.shape[0], c)], land.at[_half(land.shape[0], 1 - c)], (x, y, 1 - c)) for land in lands]


def _pair_plan(srcs, lands, x, y, c):
    return [(src.at[:, _half(src.shape[1], 1 - c)], land, (x, y, 1 - c)) for src, land in zip(srcs, lands)]


def add_own_half(name, core, grads, got):
    n = len(grads)

    def body(core_ref, *refs):
        for i in range(n):
            refs[2 * n + i][...] = (refs[i][...].astype(F32) + refs[n + i][...].astype(F32)).astype(BF)

    def spec(g, own):
        blk = (None, g.shape[1] // 2, g.shape[2])
        return pl.BlockSpec(blk, (lambda j, core_ref: (j, core_ref[0], 0)) if own else (lambda j, core_ref: (j, 0, 0)))

    return pl.pallas_call(
        body, name=name,
        grid_spec=pltpu.PrefetchScalarGridSpec(
            num_scalar_prefetch=1, grid=(N_CHIPS,),
            in_specs=[spec(g, True) for g in grads] + [spec(g, False) for g in grads],
            out_specs=[spec(g, False) for g in grads]),
        out_shape=[_sds(r.shape, BF) for r in got], compiler_params=_params("parallel"),
    )(core, *grads, *got)


def _chips_plan(srcs, lands, x, y, c):
    return [(src.at[2 * chip[0] + chip[1]], land.at[k], (*chip, c))
            for src, land in zip(srcs, lands) for k, chip in enumerate(_other_chips(x, y))]


def sum_chips(name, place, parts, got):
    n = len(got)

    def body(place_ref, *refs):
        for i in range(n):
            acc = refs[i][...].astype(F32)
            for k in range(N_CHIPS - 1):
                acc = acc + refs[n + i][k].astype(F32)
            refs[2 * n + i][...] = acc

    steps = 2
    return pl.pallas_call(
        body, name=name,
        grid_spec=pltpu.PrefetchScalarGridSpec(
            num_scalar_prefetch=1, grid=(steps,),
            in_specs=[pl.BlockSpec((None, g.shape[1] // steps, g.shape[2]), lambda t, place_ref: (place_ref[0], t, 0))
                      for g in parts] +
                     [pl.BlockSpec((N_CHIPS - 1, g.shape[1] // steps, g.shape[2]), lambda t, place_ref: (0, t, 0))
                      for g in got],
            out_specs=[pl.BlockSpec((g.shape[1] // steps, g.shape[2]),
                                    lambda t, place_ref: (place_ref[1] * steps + t, 0)) for g in got]),
        out_shape=[_sds((2 * g.shape[1], g.shape[2]), F32) for g in got], compiler_params=_params("parallel"),
    )(place, *parts, *got)


def reduce_scatter_pair(tag, grads):
    lands = [lax.empty((N_CHIPS, g.shape[1] // 2, g.shape[2]), g.dtype) for g in grads]
    return copies_start(f"{tag}_rs_pair_start", grads, lands, _pair_plan)


def reduce_scatter_begin(tag, core, pairing, after):
    grads, got = copies_wait(f"{tag}_rs_pair_wait", pairing, _pair_plan, after)
    parts = add_own_half(f"{tag}_rs_add", core, grads, got)
    lands = [lax.empty((N_CHIPS - 1,) + p.shape[1:], p.dtype) for p in parts]
    return copies_start(f"{tag}_rs_start", parts, lands, _chips_plan)


def reduce_scatter_sum(tag, place, started, after):
    parts, got = copies_wait(f"{tag}_rs_wait", started, _chips_plan, after)
    return copies_start(f"{tag}_rs_join_start", [], sum_chips(f"{tag}_rs_sum", place, parts, got), _join_plan)


def reduce_scatter_end(tag, joining, after):
    return copies_wait(f"{tag}_rs_join_wait", joining, _join_arrivals, after)[1]


def small_allreduce(name, x):
    R = x.shape[0]
    H = R // 2

    def body(x_ref, o_ref, pair_ref, chip_ref, send_sems, recv_sems):
        xx, yy, c = _mesh_pos()
        me = 2 * xx + yy
        chips = _other_chips(xx, yy)
        sibling = (xx, yy, 1 - c)
        mine = pl.ds(pl.multiple_of(c * H, 8), H)
        theirs = pl.ds(pl.multiple_of((1 - c) * H, 8), H)
        a = _remote(x_ref.at[theirs], pair_ref.at[theirs], send_sems, recv_sems, 0, sibling)
        a.start()
        a.wait_send()
        _remote(x_ref.at[mine], pair_ref.at[mine], send_sems, recv_sems, 0, sibling).wait_recv()
        chip_ref[me] = x_ref[mine, :] + pair_ref[mine, :]
        sends = []
        for k, chip in enumerate(chips):
            cp = _remote(chip_ref.at[me], chip_ref.at[me], send_sems, recv_sems, 1 + k, (*chip, c))
            cp.start()
            sends.append(cp)
        for k, chip in enumerate(chips):
            slot = chip_ref.at[2 * chip[0] + chip[1]]
            _remote(slot, slot, send_sems, recv_sems, 1 + k, (*chip, c)).wait_recv()
        o_ref[mine, :] = (chip_ref[0] + chip_ref[1]) + (chip_ref[2] + chip_ref[3])
        b = _remote(o_ref.at[mine], o_ref.at[mine], send_sems, recv_sems, 4, sibling)
        b.start()
        b.wait_send()
        _remote(o_ref.at[theirs], o_ref.at[theirs], send_sems, recv_sems, 4, sibling).wait_recv()
        for cp in sends:
            cp.wait_send()

    return pl.pallas_call(
        body, name=name,
        in_specs=[pl.BlockSpec(memory_space=pltpu.VMEM)], out_specs=pl.BlockSpec(memory_space=pltpu.VMEM),
        out_shape=_sds((R, LANES), F32),
        scratch_shapes=[pltpu.VMEM((R, LANES), F32), pltpu.VMEM((N_CHIPS, H, LANES), F32),
                        pltpu.SemaphoreType.DMA((5,)), pltpu.SemaphoreType.DMA((5,))],
        compiler_params=pltpu.CompilerParams(vmem_limit_bytes=VMEM_LIMIT),
    )(x)


WEIGHTS = ("norm1_g", "w_in", "conv_w", "q_norm_g", "k_norm_g", "sgu_norm_g", "sgu_w", "sgu_b", "w_out", "norm2_g",
           "w_ff1", "w_ff2", "norm3_g", "w_ple_gate", "w_ple_proj")
SMALL = ("norm1_g", "norm2_g", "norm3_g", "q_norm_g", "k_norm_g", "sgu_norm_g", "sgu_w", "sgu_b", "conv_w")


PACK = 8 * LANES


def _pack_rows(arrays):
    blocks = []
    for a in arrays:
        v = a.reshape(-1)
        blocks.append(jnp.pad(v, (0, (-v.shape[0]) % PACK)).reshape(-1, LANES))
    rows = sum(b.shape[0] for b in blocks)
    if rows % 16:
        blocks.append(jnp.zeros((16 - rows % 16, LANES), F32))
    return jnp.concatenate(blocks, axis=0)


def _unpack_rows(packed, shapes):
    out, pos = [], 0
    flat = packed.reshape(-1)
    for shp in shapes:
        size = math.prod(shp)
        out.append(flat[pos:pos + size].reshape(shp))
        pos += size + (-size) % PACK
    return out


def kernel(x, p, norm1_g, w_in, conv_w, q_norm_g, k_norm_g, sgu_norm_g, sgu_w, sgu_b, w_out, norm2_g, w_ff1, w_ff2, norm3_g, w_ple_gate, w_ple_proj, loss_target, m_norm1_g, m_w_in, m_conv_w, m_q_norm_g, m_k_norm_g, m_sgu_norm_g, m_sgu_w, m_sgu_b, m_w_out, m_norm2_g, m_w_ff1, m_w_ff2, m_norm3_g, m_w_ple_gate, m_w_ple_proj, v_norm1_g, v_w_in, v_conv_w, v_q_norm_g, v_k_norm_g, v_sgu_norm_g, v_sgu_w, v_sgu_b, v_w_out, v_norm2_g, v_w_ff1, v_w_ff2, v_norm3_g, v_w_ple_gate, v_w_ple_proj):
    w = dict(norm1_g=norm1_g, w_in=w_in, conv_w=conv_w, q_norm_g=q_norm_g, k_norm_g=k_norm_g, sgu_norm_g=sgu_norm_g,
             sgu_w=sgu_w, sgu_b=sgu_b, w_out=w_out, norm2_g=norm2_g, w_ff1=w_ff1, w_ff2=w_ff2, norm3_g=norm3_g,
             w_ple_gate=w_ple_gate, w_ple_proj=w_ple_proj)
    m = dict(norm1_g=m_norm1_g, w_in=m_w_in, conv_w=m_conv_w, q_norm_g=m_q_norm_g, k_norm_g=m_k_norm_g,
             sgu_norm_g=m_sgu_norm_g, sgu_w=m_sgu_w, sgu_b=m_sgu_b, w_out=m_w_out, norm2_g=m_norm2_g, w_ff1=m_w_ff1,
             w_ff2=m_w_ff2, norm3_g=m_norm3_g, w_ple_gate=m_w_ple_gate, w_ple_proj=m_w_ple_proj)
    v = dict(norm1_g=v_norm1_g, w_in=v_w_in, conv_w=v_conv_w, q_norm_g=v_q_norm_g, k_norm_g=v_k_norm_g,
             sgu_norm_g=v_sgu_norm_g, sgu_w=v_sgu_w, sgu_b=v_sgu_b, w_out=v_w_out, norm2_g=v_norm2_g, w_ff1=v_w_ff1,
             w_ff2=v_w_ff2, norm3_g=v_norm3_g, w_ple_gate=v_w_ple_gate, w_ple_proj=v_w_ple_proj)
    depth = w_in.shape[0]
    d_model = x.shape[-1]
    chip = 2 * lax.axis_index("x") + lax.axis_index("y")
    core = lax.axis_index("c")
    core_arr = core.reshape(1).astype(jnp.int32)

    cw_cols = conv_w.shape[-1]
    placed = lax.dynamic_update_slice(jnp.zeros((depth, 3, CONV_W), F32), conv_w, (0, 0, chip * cw_cols))
    placed = jnp.where(core == 0, placed, 0.0)
    conv_full = _unpack_rows(small_allreduce("conv_w_gather", _pack_rows([placed])), [(depth, 3, CONV_W)])[0]

    h = x[0]
    p_bf = p[:, 0].astype(BF)
    saved, full = [], []

    def gather_start(l, after):
        shards = [w[n][l].astype(BF) for n in BIG]
        lands = [lax.empty((N_CHIPS,) + s.shape, BF) for s in shards]
        return copies_start(f"l{l}_gather_start", shards, lands, _gather_plan, after)

    def gather_forward(l, started, after):
        shards, lands = copies_wait(f"l{l}_gather_wait", started, _gather_arrivals, after)
        forwarding = copies_start(f"l{l}_forward_start", shards, lands, _forward_plan)
        return forwarding, gather_start(l + 1, (forwarding[-1],)) if l + 1 < depth else None

    forwarding, started = gather_forward(0, gather_start(0, ()), (h,))
    for l in range(depth):
        g_in, g_out, g_ff1, g_ff2, g_gate, g_proj = copies_wait(f"l{l}_forward_wait", forwarding, _forward_arrivals,
                                                                 (h,))[1]
        nxt = {}

        def mid(arr, l=l, started=started, nxt=nxt):
            nxt["forwarding"], nxt["started"] = gather_forward(l + 1, started, (arr,))
            return [t[-1] for t in nxt.values() if t is not None]

        wt = prep_small(norm1_g[l], q_norm_g[l], k_norm_g[l], sgu_norm_g[l], sgu_w[l], sgu_b[l], norm2_g[l], norm3_g[l],
                        conv_full[l])
        wt["w_in"] = jnp.transpose(g_in, (1, 0, 2)).reshape(d_model, -1)
        wt["w_out"] = g_out.reshape(-1, d_model)
        wt["w_ff1"] = g_ff1
        wt["w_ff2"] = g_ff2.reshape(-1, d_model)
        wt["w_ple_gate"] = g_gate.reshape(-1, d_model)
        wt["w_ple_proj"] = g_proj
        last = l + 1 == depth
        h, sv = layer_fwd(f"l{l}", h, p_bf[l], wt, () if last else (started[-1],), None if last else mid)
        if not last:
            forwarding, started = nxt["forwarding"], nxt["started"]
        saved.append(sv)
        full.append(wt)

    loss_tile, dh = loss_head("loss", h, loss_target[0])
    loss = lax.psum(loss_tile[0, 0], ("x", "y", "c"))

    small = [None] * depth
    chip_arr = jnp.stack([chip, core]).astype(jnp.int32)
    big_w, big_m, big_v = ([d[n] for n in BIG] for d in (w, m, v))
    pairing, joining, tokens = None, {}, ()
    for l in reversed(range(depth)):
        box = {}

        def mid(arr, l=l, pairing=pairing, box=box):
            box["started"] = reduce_scatter_begin(f"l{l + 1}", core_arr, pairing, (arr,))
            return (box["started"][-1],)

        dh, g = layer_bwd(f"l{l}", dh, saved[l], full[l], tokens, None if pairing is None else mid)
        tokens = ()
        if pairing is not None:
            joining[l + 1] = reduce_scatter_sum(f"l{l + 1}", chip_arr, box["started"], (dh,))
            tokens = (joining[l + 1][-1],)
        small[l] = small_grads(g)
        shards_in = w_in.shape[-1]
        gl = [jnp.transpose(g["w_in"].reshape(d_model, N_CHIPS, shards_in), (1, 0, 2)),
              g["w_out"].reshape(N_CHIPS, -1, d_model), g["w_ff1"], g["w_ff2"].reshape(N_CHIPS, -1, d_model),
              g["w_ple_gate"].reshape(N_CHIPS, -1, d_model), g["w_ple_proj"]]
        pairing = reduce_scatter_pair(f"l{l}", gl)
        tokens += (pairing[-1],)

    started = reduce_scatter_begin("l0", core_arr, pairing, (dh,))
    updated = None
    for l in reversed(range(1, depth)):
        reduced = reduce_scatter_end(f"l{l}", joining[l], (started[-1],))
        updated = adamw_layer(f"l{l}_adamw", l, big_w, reduced, big_m, big_v, updated)
    grads, delta, new_m, new_v = {}, {}, {}, {}
    packed = _pack_rows([small[l][n] for l in range(depth) for n in SMALL])
    shapes = [small[l][n].shape for l in range(depth) for n in SMALL]
    pieces = _unpack_rows(small_allreduce("small_grads", packed), shapes)
    for i, n in enumerate(SMALL):
        grads[n] = jnp.stack([pieces[l * len(SMALL) + i] for l in range(depth)])
    grads["conv_w"] = lax.dynamic_slice(grads["conv_w"], (0, 0, chip * cw_cols), (depth, 3, cw_cols))
    for n in SMALL:
        shp = w[n].shape
        two_d = (-1, shp[-1]) if n != "sgu_w" else (-1, LANES)
        d, nm, nv = adamw(f"adamw_{n}", w[n].reshape(two_d), grads[n].reshape(two_d), m[n].reshape(two_d),
                          v[n].reshape(two_d))
        delta[n], new_m[n], new_v[n] = d.reshape(shp), nm.reshape(shp), nv.reshape(shp)

    done = [new_v[n] for n in SMALL] + ([] if updated is None else [updated[3][0]])
    reduced = reduce_scatter_end("l0", reduce_scatter_sum("l0", chip_arr, started, (dh, *done)), ())
    updated = adamw_layer("l0_adamw", 0, big_w, reduced, big_m, big_v, updated)
    for k, d in enumerate((grads, delta, new_m, new_v)):
        d.update(zip(BIG, updated[k]))

    return (loss, dh[None], *[grads[n] for n in WEIGHTS], *[delta[n] for n in WEIGHTS], *[new_m[n] for n in WEIGHTS],
            *[new_v[n] for n in WEIGHTS])
```

```python
import math

import jax
import jax.numpy as jnp
from jax import lax
from jax.experimental import pallas as pl
from jax.experimental.pallas import tpu as pltpu

F32 = jnp.float32
BF = jnp.bfloat16
MESH = pl.DeviceIdType.MESH
HIGHEST = lax.Precision.HIGHEST

EPS = 1e-6
HEAD_DIM = 64
CONV_W = 256
ATTN_W = 512
SGU_W = 256
D_MIX = CONV_W + ATTN_W + SGU_W
CHUNK = 128
N_CHIPS = 4
SCALE = HEAD_DIM ** -0.5
LANES = 128
VMEM_LIMIT = 56 * 1024 * 1024

ADAM_LR = 0.001
ADAM_B1 = 0.9
ADAM_B2 = 0.999
ADAM_EPS = 1e-08
ADAM_WD = 0.01
ADAM_STEP = 10

NT_DIMS = (((1,), (1,)), ((), ()))
TN_DIMS = (((0,), (0,)), ((), ()))


def _params(*sem):
    return pltpu.CompilerParams(dimension_semantics=sem if sem else None, vmem_limit_bytes=VMEM_LIMIT)


def _sds(shape, dtype):
    return jax.ShapeDtypeStruct(shape, dtype)


def _erf(x):
    return lax.erf(x)


def _gelu(x):
    return 0.5 * x * (1.0 + _erf(x * (2.0 ** -0.5)))


def _gelu_grad(x):
    return 0.5 * (1.0 + _erf(x * (2.0 ** -0.5))) + x * jnp.exp(-0.5 * x * x) * (1.0 / math.sqrt(2.0 * math.pi))


def _log_sigmoid(z):
    return jnp.minimum(z, 0.0) - jnp.log(1.0 + jnp.exp(-jnp.abs(z)))


def _head_mean_matrix(width):
    r = lax.broadcasted_iota(jnp.int32, (width, width), 0) // HEAD_DIM
    c = lax.broadcasted_iota(jnp.int32, (width, width), 1) // HEAD_DIM
    return (r == c).astype(BF)


def _head_mean(x, m):
    hi = x.astype(BF)
    lo = (x - hi.astype(F32)).astype(BF)
    return _dot2_stacked(hi, lo, m) * (1.0 / HEAD_DIM)


def mm_nn(name, x, w, *, extras=(), pro=None, epi=None, out_dtypes=None, tm=None, tn=512):
    S, K = x.shape
    if w.ndim == 3:
        J, _, tn = w.shape
        N = J * tn
        w_spec = pl.BlockSpec((None, K, tn), lambda n, m: (n, 0, 0))
    else:
        N = w.shape[1]
        tn = min(tn, N)
        w_spec = pl.BlockSpec((K, tn), lambda n, m: (0, n))
    tm = S if tm is None else min(tm, S)
    out_dtypes = (BF,) if out_dtypes is None else out_dtypes
    n_ex, n_out = len(extras), len(out_dtypes)

    def body(x_ref, w_ref, *rest):
        xv = x_ref[...]
        if pro is not None:
            xv = pro(xv)
        acc = jnp.dot(xv.astype(BF), w_ref[...], preferred_element_type=F32)
        outs = (acc,) if epi is None else epi(acc, *[e[...] for e in rest[:n_ex]])
        for o_ref, o in zip(rest[n_ex:], outs):
            o_ref[...] = o.astype(o_ref.dtype)

    tile = pl.BlockSpec((tm, tn), lambda n, m: (m, n))
    out = pl.pallas_call(
        body, name=name, grid=(N // tn, S // tm),
        in_specs=[pl.BlockSpec((tm, K), lambda n, m: (m, 0)), w_spec] + [tile] * n_ex,
        out_specs=[tile] * n_out,
        out_shape=[_sds((S, N), d) for d in out_dtypes],
        compiler_params=_params("parallel", "parallel"),
    )(x, w, *extras)
    return out[0] if n_out == 1 else out


def norm_mm(name, h, g, w, *, tn=512, after=()):
    S, K = h.shape
    if w.ndim == 3:
        J, _, tn = w.shape
        N = J * tn
        w_spec = pl.BlockSpec((None, K, tn), lambda n: (n, 0, 0))
    else:
        N = w.shape[1]
        tn = min(tn, N)
        w_spec = pl.BlockSpec((K, tn), lambda n: (0, n))
    rows = min(S, 256)

    def body(h_ref, g_ref, w_ref, *rest):
        hn_ref, o_ref = rest[-2:]

        @pl.when(pl.program_id(0) == 0)
        def _():
            def chunk(i, _):
                r = pl.ds(pl.multiple_of(i * rows, rows), rows)
                x = h_ref[r, :]
                scale = lax.rsqrt(jnp.mean(x * x, axis=-1, keepdims=True) + EPS)
                hn_ref[r, :] = ((x * scale) * g_ref[...]).astype(BF)
                return 0

            lax.fori_loop(0, S // rows, chunk, 0)

        o_ref[...] = jnp.dot(hn_ref[...], w_ref[...], preferred_element_type=F32).astype(BF)

    whole = pl.BlockSpec((S, K), lambda n: (0, 0))
    return pl.pallas_call(
        body, name=name, grid=(N // tn,),
        in_specs=[pl.BlockSpec((S, K), lambda n: (0, 0), pipeline_mode=pl.Buffered(1)),
                  pl.BlockSpec((1, K), lambda n: (0, 0)), w_spec] + [pl.BlockSpec(memory_space=pl.ANY)] * len(after),
        out_specs=[whole, pl.BlockSpec((S, tn), lambda n: (0, n))],
        out_shape=[_sds((S, K), BF), _sds((S, N), BF)], compiler_params=_params("arbitrary"),
    )(h, g.reshape(1, K), w, *after)


def mm_nt(name, dy, w, *, extras=(), epi=None, tk=512, after=()):
    S, N = dy.shape
    K = w.shape[0]
    tk = min(tk, K)
    n_ex = len(extras)

    def body(dy_ref, w_ref, *rest):
        dyb = rest[-1]

        @pl.when(pl.program_id(0) == 0)
        def _():
            dyb[...] = dy_ref[...].astype(BF)

        acc = lax.dot_general(dyb[...], w_ref[...], NT_DIMS, preferred_element_type=F32)
        if epi is not None:
            acc = epi(acc, *[e[...] for e in rest[:n_ex]])
        rest[-2][...] = acc.astype(BF)

    col = pl.BlockSpec((S, tk), lambda k: (0, k))
    return pl.pallas_call(
        body, name=name, grid=(K // tk,),
        in_specs=[pl.BlockSpec((S, N), lambda k: (0, 0)), pl.BlockSpec((tk, N), lambda k: (k, 0))] + [col] * n_ex +
                 [pl.BlockSpec(memory_space=pl.ANY)] * len(after),
        out_specs=col, out_shape=_sds((S, K), BF), scratch_shapes=[pltpu.VMEM((S, N), BF)],
        compiler_params=_params("arbitrary"),
    )(dy, w, *extras, *after)


def mm_nt_rmsbwd(name, dy, w, h, g, dres, *, tm=256):
    S, N = dy.shape
    D = h.shape[1]
    tm = min(tm, S)
    blocked = w.ndim == 3
    nj = w.shape[2] if blocked else N

    def body(dy_ref, w_ref, h_ref, g_ref, dres_ref, dh_ref, dg_ref):
        i = pl.program_id(0)
        if blocked:
            dyn = None
            for j in range(w.shape[0]):
                part = lax.dot_general(dy_ref[:, j * nj:(j + 1) * nj].astype(BF), w_ref[j], NT_DIMS,
                                       preferred_element_type=F32)
                dyn = part if dyn is None else dyn + part
        else:
            dyn = lax.dot_general(dy_ref[...].astype(BF), w_ref[...], NT_DIMS, preferred_element_type=F32)
        x = h_ref[...]
        r = lax.rsqrt(jnp.mean(x * x, axis=-1, keepdims=True) + EPS)
        t = dyn * g_ref[...]
        dh_ref[...] = dres_ref[...] + r * t - x * (r * r * r) * jnp.mean(t * x, axis=-1, keepdims=True)
        part = jnp.sum(dyn * (x * r), axis=0, keepdims=True)

        @pl.when(i == 0)
        def _():
            dg_ref[...] = part

        @pl.when(i > 0)
        def _():
            dg_ref[...] += part

    w_spec = pl.BlockSpec(w.shape, (lambda i: (0, 0, 0)) if blocked else (lambda i: (0, 0)))
    row = pl.BlockSpec((tm, D), lambda i: (i, 0))
    vec = pl.BlockSpec((1, D), lambda i: (0, 0))
    return pl.pallas_call(
        body, name=name, grid=(S // tm,),
        in_specs=[pl.BlockSpec((tm, N), lambda i: (i, 0)), w_spec, row, vec, row],
        out_specs=[row, vec], out_shape=[_sds((S, D), F32), _sds((1, D), F32)],
        compiler_params=_params("arbitrary"),
    )(dy, w, h, g.reshape(1, D), dres)


def mm_tn(name, x, dy, *, pro_x=None, col_blocks=None, tk=1024, tn=1024):
    S, K = x.shape
    N = dy.shape[1]
    tk = min(tk, K)
    if col_blocks is not None:
        tn = N // col_blocks
        out_shape = _sds((col_blocks, K, tn), BF)
        out_spec = pl.BlockSpec((None, tk, tn), lambda k, n: (n, k, 0))
    else:
        tn = min(tn, N)
        out_shape = _sds((K, N), BF)
        out_spec = pl.BlockSpec((tk, tn), lambda k, n: (k, n))

    def body(x_ref, dy_ref, o_ref):
        xv = x_ref[...]
        if pro_x is not None:
            xv = pro_x(xv)
        o_ref[...] = lax.dot_general(xv.astype(BF), dy_ref[...].astype(BF), TN_DIMS,
                                     preferred_element_type=F32).astype(BF)

    return pl.pallas_call(
        body, name=name, grid=(K // tk, N // tn),
        in_specs=[pl.BlockSpec((S, tk), lambda k, n: (0, k)), pl.BlockSpec((S, tn), lambda k, n: (0, n))],
        out_specs=out_spec, out_shape=out_shape, compiler_params=_params("parallel", "parallel"),
    )(x, dy)


def _conv_parts(ac_ref, ah_ref, cw):
    a_c = ac_ref[...].astype(F32)
    a_h = ah_ref[...].astype(F32)
    x = a_c * a_h
    row = lax.broadcasted_iota(jnp.int32, x.shape, 0)
    x1 = jnp.where(row >= 1, pltpu.roll(x, 1, 0), 0.0)
    x2 = jnp.where(row >= 2, pltpu.roll(x, 2, 0), 0.0)
    cv = cw[0:1] * x2 + cw[1:2] * x1 + cw[2:3] * x
    return a_c, a_h, x, x1, x2, cv, row


def conv_fwd(name, proj, cw):
    S = proj.shape[0]

    def body(ab_ref, ac_ref, ah_ref, cw_ref, o_ref):
        cv = _conv_parts(ac_ref, ah_ref, cw_ref[...])[5]
        o_ref[...] = (ab_ref[...].astype(F32) * cv).astype(BF)

    col = lambda j: pl.BlockSpec((S, CONV_W), lambda i, j=j: (0, j))
    return pl.pallas_call(
        body, name=name, grid=(1,),
        in_specs=[col(0), col(1), col(2), pl.BlockSpec((3, CONV_W), lambda i: (0, 0))],
        out_specs=pl.BlockSpec((S, CONV_W), lambda i: (0, 0)),
        out_shape=_sds((S, D_MIX), BF), compiler_params=_params("arbitrary"),
    )(proj, proj, proj, cw)


def conv_bwd(name, dy, proj, cw):
    S = proj.shape[0]

    def body(dy_ref, ab_ref, ac_ref, ah_ref, cw_ref, dab_ref, dac_ref, dah_ref, dcw_ref):
        w = cw_ref[...]
        a_c, a_h, x, x1, x2, cv, row = _conv_parts(ac_ref, ah_ref, w)
        d = dy_ref[...].astype(F32)
        dab_ref[...] = (d * cv).astype(BF)
        dcv = d * ab_ref[...].astype(F32)
        d1 = jnp.where(row < S - 1, pltpu.roll(dcv, S - 1, 0), 0.0)
        d2 = jnp.where(row < S - 2, pltpu.roll(dcv, S - 2, 0), 0.0)
        dx = w[2:3] * dcv + w[1:2] * d1 + w[0:1] * d2
        dac_ref[...] = (dx * a_h).astype(BF)
        dah_ref[...] = (dx * a_c).astype(BF)
        dcw_ref[0:1, :] = jnp.sum(dcv * x2, axis=0, keepdims=True)
        dcw_ref[1:2, :] = jnp.sum(dcv * x1, axis=0, keepdims=True)
        dcw_ref[2:3, :] = jnp.sum(dcv * x, axis=0, keepdims=True)

    col = lambda j: pl.BlockSpec((S, CONV_W), lambda i, j=j: (0, j))
    one = pl.BlockSpec((S, CONV_W), lambda i: (0, 0))
    small = pl.BlockSpec((3, CONV_W), lambda i: (0, 0))
    return pl.pallas_call(
        body, name=name, grid=(1,),
        in_specs=[col(0), col(0), col(1), col(2), small],
        out_specs=[one, one, one, small],
        out_shape=[_sds((S, CONV_W), BF)] * 3 + [_sds((3, CONV_W), F32)],
        compiler_params=_params("arbitrary"),
    )(dy, proj, proj, proj, cw)


SGU_HEADS = SGU_W // HEAD_DIM
CU_BLOCK = 2304 // SGU_W
CV_BLOCK = 2560 // SGU_W


def _sgu_common(cu_ref, cv_ref, gv_ref):
    c_u = cu_ref[...].astype(F32)
    c_v = cv_ref[...].astype(F32)
    hm = _head_mean_matrix(SGU_W)
    u = _gelu(c_u)
    vg = _gelu(c_v)
    r = lax.rsqrt(_head_mean(vg * vg, hm) + EPS)
    vv = (vg * r) * gv_ref[...]
    head = lax.broadcasted_iota(jnp.int32, (CHUNK, SGU_W), 1) // HEAD_DIM
    tri = (lax.broadcasted_iota(jnp.int32, (CHUNK, CHUNK), 0) >=
           lax.broadcasted_iota(jnp.int32, (CHUNK, CHUNK), 1))
    return c_u, c_v, hm, u, vg, r, vv, head, tri


def _sgu_mix(w_ref, tri, head, vvc, bias):
    sv = bias
    for g in range(SGU_HEADS):
        wg = jnp.where(tri, w_ref[g], 0.0).astype(BF)
        sv = sv + jnp.where(head == g, jnp.dot(wg, vvc, preferred_element_type=F32), 0.0)
    return sv


def sgu_fwd(name, proj, gv, w, bias, y):
    S = proj.shape[0]
    tm = min(S, 512)

    def body(cu_ref, cv_ref, gv_ref, w_ref, b_ref, y_ref, o_ref):
        _, _, _, u, _, _, vv, head, tri = _sgu_common(cu_ref, cv_ref, gv_ref)
        vvb = vv.astype(BF)
        for ch in range(tm // CHUNK):
            rows = slice(ch * CHUNK, (ch + 1) * CHUNK)
            sv = _sgu_mix(w_ref, tri, head, vvb[rows], b_ref[...])
            o_ref[rows, :] = (u[rows] * sv).astype(BF)

    const = lambda shape: pl.BlockSpec(shape, lambda i: (0,) * len(shape))
    return pl.pallas_call(
        body, name=name, grid=(S // tm,),
        in_specs=[pl.BlockSpec((tm, SGU_W), lambda i: (i, CU_BLOCK)), pl.BlockSpec((tm, SGU_W), lambda i: (i, CV_BLOCK)),
                  const((1, SGU_W)), const((SGU_HEADS, CHUNK, CHUNK)), const((CHUNK, SGU_W)),
                  pl.BlockSpec(memory_space=pl.ANY)],
        out_specs=pl.BlockSpec((tm, SGU_W), lambda i: (i, (CONV_W + ATTN_W) // SGU_W)),
        out_shape=_sds(y.shape, BF), input_output_aliases={5: 0}, compiler_params=_params("parallel"),
    )(proj, proj, gv, w, bias, y)


def sgu_bwd(name, dy, proj, gv, w, bias):
    S = proj.shape[0]
    tm = min(S, 512)

    def body(dy_ref, cu_ref, cv_ref, gv_ref, w_ref, b_ref, dcu_ref, dcv_ref, dw_ref, db_ref, dgv_ref, dvv_s):
        i = pl.program_id(0)
        c_u, c_v, hm, u, vg, r, vv, head, tri = _sgu_common(cu_ref, cv_ref, gv_ref)
        vvb = vv.astype(BF)
        d = dy_ref[...].astype(F32)
        ind = (lax.broadcasted_iota(jnp.int32, (SGU_W, LANES), 0) // HEAD_DIM ==
               lax.broadcasted_iota(jnp.int32, (SGU_W, LANES), 1)).astype(BF)
        dw_acc = [jnp.zeros((CHUNK, CHUNK), F32) for _ in range(SGU_HEADS)]
        db_acc = jnp.zeros((CHUNK, LANES), F32)
        for ch in range(tm // CHUNK):
            rows = slice(ch * CHUNK, (ch + 1) * CHUNK)
            sv = _sgu_mix(w_ref, tri, head, vvb[rows], b_ref[...])
            dcu_ref[rows, :] = (d[rows] * sv * _gelu_grad(c_u[rows])).astype(BF)
            dsv = d[rows] * u[rows]
            dsv_hi = dsv.astype(BF)
            db_acc = db_acc + _dot2_stacked(dsv_hi, (dsv - dsv_hi.astype(F32)).astype(BF), ind)
            dvv = jnp.zeros((CHUNK, SGU_W), F32)
            for g in range(SGU_HEADS):
                dsv_g = jnp.where(head == g, dsv, 0.0).astype(BF)
                wg = jnp.where(tri, w_ref[g], 0.0).astype(BF)
                dvv = dvv + lax.dot_general(wg, dsv_g, TN_DIMS, preferred_element_type=F32)
                dw_acc[g] = dw_acc[g] + lax.dot_general(dsv_g, vvb[rows], NT_DIMS, preferred_element_type=F32)
            dvv_s[rows, :] = dvv
        dvv = dvv_s[...]
        gvv = gv_ref[...]
        t = dvv * gvv
        dvg = r * t - vg * (r * r * r) * _head_mean(t * vg, hm)
        dcv_ref[...] = (dvg * _gelu_grad(c_v)).astype(BF)
        dgv = jnp.sum(dvv * (vg * r), axis=0, keepdims=True)

        @pl.when(i == 0)
        def _():
            for g in range(SGU_HEADS):
                dw_ref[g] = jnp.where(tri, dw_acc[g], 0.0)
            db_ref[...] = db_acc
            dgv_ref[...] = dgv

        @pl.when(i > 0)
        def _():
            for g in range(SGU_HEADS):
                dw_ref[g] += jnp.where(tri, dw_acc[g], 0.0)
            db_ref[...] += db_acc
            dgv_ref[...] += dgv

    const = lambda shape: pl.BlockSpec(shape, lambda i: (0,) * len(shape))
    tile = pl.BlockSpec((tm, SGU_W), lambda i: (i, 0))
    return pl.pallas_call(
        body, name=name, grid=(S // tm,),
        in_specs=[pl.BlockSpec((tm, SGU_W), lambda i: (i, 3)),
                  pl.BlockSpec((tm, SGU_W), lambda i: (i, CU_BLOCK)), pl.BlockSpec((tm, SGU_W), lambda i: (i, CV_BLOCK)),
                  const((1, SGU_W)), const((SGU_HEADS, CHUNK, CHUNK)), const((CHUNK, SGU_W))],
        out_specs=[tile, tile, const((SGU_HEADS, CHUNK, CHUNK)), const((CHUNK, LANES)), const((1, SGU_W))],
        out_shape=[_sds((S, SGU_W), BF), _sds((S, SGU_W), BF), _sds((SGU_HEADS, CHUNK, CHUNK), F32),
                   _sds((CHUNK, LANES), F32), _sds((1, SGU_W), F32)],
        scratch_shapes=[pltpu.VMEM((tm, SGU_W), F32)],
        compiler_params=_params("arbitrary"),
    )(dy, proj, proj, gv, w, bias)


HEAD_PAIRS = ATTN_W // LANES
Q_BLOCK0 = 768 // LANES
K_BLOCK0 = 1280 // LANES
V_BLOCK0 = 1792 // LANES


def _attn_tile(S):
    return min(S, 256)


def _qk_norm(x, g, hm):
    r = lax.rsqrt(_head_mean(x * x, hm) + EPS)
    return r, (x * r) * g


MASKED = -1e30


def _logit_parts(z):
    lb = _log_sigmoid(z)
    lr = lb - z
    hi = lr.astype(BF)
    return lb, hi, (lr - hi.astype(F32)).astype(BF)


def _stack_heads(x, lane):
    return jnp.concatenate([jnp.where(lane < HEAD_DIM, x, 0.0), jnp.where(lane >= HEAD_DIM, x, 0.0)],
                           axis=0).astype(BF)


def _dot2_stacked(hi, lo, u):
    rows = hi.shape[0]
    both = jnp.dot(jnp.concatenate([hi, lo], axis=0), u, preferred_element_type=F32)
    return both[:rows] + both[rows:]


def attn_fwd(name, proj, gq, gk, y):
    S = proj.shape[0]
    T = _attn_tile(S)
    nq = S // T

    def body(q_ref, k_ref, v_ref, gq_ref, gk_ref, y_ref, o_ref, tot_ref, kn_s, lb_s, hi_s, lo_s, z_s, a_s, o_s):
        qi = pl.program_id(1)
        hm = _head_mean_matrix(LANES)

        @pl.when(qi == 0)
        def _():
            kn_s[...] = _qk_norm(k_ref[...].astype(F32), gk_ref[...], hm)[1].astype(BF)

        qn = _qk_norm(q_ref[...].astype(F32), gq_ref[...], hm)[1]
        lane = lax.broadcasted_iota(jnp.int32, (T, LANES), 1)
        qst = _stack_heads(qn, lane)
        rowi = lax.broadcasted_iota(jnp.int32, (T, T), 0)
        coli = lax.broadcasted_iota(jnp.int32, (T, T), 1)
        u_excl = (rowi > coli).astype(BF)
        diagonal = jnp.where(coli < rowi, 0.0, MASKED)

        def logits(j):
            return lax.dot_general(qst, kn_s[pl.ds(pl.multiple_of(j * T, T), T), :], NT_DIMS,
                                   preferred_element_type=F32)

        def values(j):
            return v_ref[pl.ds(pl.multiple_of(j * T, T), T), :].astype(BF)

        def keep(slot, z):
            lb_s[slot], hi_s[...], lo_s[...] = _logit_parts(z)

        def step(it, carry):
            run = carry
            j = qi - it
            hi, lo = hi_s[...], lo_s[...]
            both = jnp.dot(jnp.concatenate([hi, lo], axis=0), u_excl, preferred_element_type=F32)
            o_s[...] += jnp.dot(a_s[...], values(jnp.minimum(j + 1, qi)), preferred_element_type=F32)
            z_after = logits(jnp.maximum(j - 2, 0))
            first = hi[:, 0:1].astype(F32) + lo[:, 0:1].astype(F32)
            keep((it + 1) % 2, z_s[...])
            later = both[:2 * T] + both[2 * T:]
            a_s[...] = jnp.exp(lb_s[it % 2] + later + run).astype(BF)
            z_s[...] = z_after
            return run + later[:, 0:1] + first

        keep(0, logits(qi) + jnp.concatenate([diagonal, diagonal], axis=0))
        z_s[...] = logits(jnp.maximum(qi - 1, 0))
        a_s[...] = jnp.zeros_like(a_s)
        o_s[...] = jnp.zeros_like(o_s)
        run = lax.fori_loop(0, qi, step, jnp.zeros((2 * T, 1), F32))
        hi, lo = hi_s[...], lo_s[...]
        both = jnp.dot(jnp.concatenate([hi, lo], axis=0), u_excl, preferred_element_type=F32)
        o = o_s[...] + jnp.dot(a_s[...], values(jnp.minimum(1, qi)), preferred_element_type=F32)
        later = both[:2 * T] + both[2 * T:]
        a = jnp.exp(lb_s[qi % 2] + later + run).astype(BF)
        run = run + later[:, 0:1] + (hi[:, 0:1].astype(F32) + lo[:, 0:1].astype(F32))
        o = o + jnp.dot(a, values(0), preferred_element_type=F32)
        o_ref[...] = jnp.where(lane < HEAD_DIM, o[:T], o[T:]).astype(BF)
        tot_ref[...] = jnp.where(lane < HEAD_DIM, run[:T], run[T:])

    gain = pl.BlockSpec((1, LANES), lambda hp, qi: (0, 0))
    full = lambda b0: pl.BlockSpec((S, LANES), lambda hp, qi, b0=b0: (0, b0 + hp))
    tile = pl.BlockSpec((T, LANES), lambda hp, qi: (qi, hp))
    return pl.pallas_call(
        body, name=name, grid=(HEAD_PAIRS, nq),
        in_specs=[pl.BlockSpec((T, LANES), lambda hp, qi: (qi, Q_BLOCK0 + hp)), full(K_BLOCK0), full(V_BLOCK0), gain, gain,
                  pl.BlockSpec(memory_space=pl.ANY)],
        out_specs=[pl.BlockSpec((T, LANES), lambda hp, qi: (qi, CONV_W // LANES + hp)), tile],
        out_shape=[_sds(y.shape, BF), _sds((S, ATTN_W), F32)], input_output_aliases={5: 0},
        scratch_shapes=[pltpu.VMEM((S, LANES), BF), pltpu.VMEM((2, 2 * T, T), F32), pltpu.VMEM((2 * T, T), BF),
                        pltpu.VMEM((2 * T, T), BF), pltpu.VMEM((2 * T, T), F32), pltpu.VMEM((2 * T, T), BF),
                        pltpu.VMEM((2 * T, LANES), F32)],
        compiler_params=_params("arbitrary", "arbitrary"),
    )(proj, proj, proj, gq, gk, y)


def attn_bwd(name, dy, proj, tot, gq, gk):
    S = proj.shape[0]
    T = _attn_tile(S)
    nq = S // T

    def body(q_ref, k_ref, v_ref, tot_ref, do_ref, gq_ref, gk_ref,
             dq_ref, dk_ref, dv_ref, dgq_ref, dgk_ref, kn_s, dkn_s, dv_s,
             lb_s, z_s, g_s, dq_s, hi_s, lo_s, a_s, ghi_s, glo_s):
        hp = pl.program_id(0)
        qi = pl.program_id(1)
        hm = _head_mean_matrix(LANES)

        @pl.when(qi == 0)
        def _():
            kn_s[...] = _qk_norm(k_ref[...].astype(F32), gk_ref[...], hm)[1].astype(BF)
            dkn_s[...] = jnp.zeros_like(dkn_s)
            dv_s[...] = jnp.zeros_like(dv_s)

        q = q_ref[...].astype(F32)
        rq, qn = _qk_norm(q, gq_ref[...], hm)
        lane = lax.broadcasted_iota(jnp.int32, (T, LANES), 1)
        qst = _stack_heads(qn, lane)
        dost = _stack_heads(do_ref[...].astype(F32), lane)
        total = jnp.concatenate([tot_ref[:, 0:1], tot_ref[:, HEAD_DIM:HEAD_DIM + 1]], axis=0)
        rowi = lax.broadcasted_iota(jnp.int32, (T, T), 0)
        coli = lax.broadcasted_iota(jnp.int32, (T, T), 1)
        u_upto = (rowi <= coli).astype(BF)
        u_before = (rowi < coli).astype(BF)
        diagonal = jnp.where(coli < rowi, 0.0, MASKED)

        on_diagonal = jnp.concatenate([diagonal, diagonal], axis=0)

        def rows(b):
            return pl.ds(pl.multiple_of(jnp.clip(b, 0, qi) * T, T), T)

        def logits(b):
            return lax.dot_general(qst, kn_s[rows(b), :], NT_DIMS, preferred_element_type=F32)

        def keep(b, z):
            bias = jnp.where(b == qi, on_diagonal, jnp.where(b > qi, MASKED, 0.0))
            lb_s[b % 3], hi_s[...], lo_s[...] = _logit_parts(z + bias)

        def step(i, carry):
            run, grun = carry
            both_before = jnp.dot(jnp.concatenate([ghi_s[...], glo_s[...]], axis=0), u_before,
                                  preferred_element_type=F32)
            both_upto = jnp.dot(jnp.concatenate([hi_s[...], lo_s[...]], axis=0), u_upto, preferred_element_type=F32)
            da = lax.dot_general(dost, v_ref[rows(i), :].astype(BF), NT_DIMS, preferred_element_type=F32)
            dv_s[rows(i - 1), :] += lax.dot_general(a_s[...], dost, TN_DIMS, preferred_element_type=F32)
            z_after = logits(i + 2)

            keep(i + 1, z_s[...])

            g = g_s[...]
            before = both_before[:2 * T] + both_before[2 * T:]
            dz = (g - jnp.exp(lb_s[(i + 2) % 3]) * (g + (grun + before))).astype(BF)
            dq_s[...] += jnp.dot(dz, kn_s[rows(i - 1), :], preferred_element_type=F32)
            dkn_s[rows(i - 1), :] += lax.dot_general(dz, qst, TN_DIMS, preferred_element_type=F32)
            grun = grun + before[:, T - 1:T] + g[:, T - 1:T]

            upto = both_upto[:2 * T] + both_upto[2 * T:]
            a = jnp.exp(lb_s[i % 3] + (total - run - upto))
            g = da * a
            a_s[...] = a.astype(BF)
            g_s[...] = g
            ghi = g.astype(BF)
            ghi_s[...] = ghi
            glo_s[...] = (g - ghi.astype(F32)).astype(BF)
            z_s[...] = z_after
            return run + upto[:, T - 1:T], grun

        lb_s[...] = jnp.full(lb_s.shape, MASKED, F32)
        for ref in (a_s, g_s, ghi_s, glo_s, dq_s):
            ref[...] = jnp.zeros_like(ref)
        keep(0, logits(0))
        z_s[...] = logits(1)
        _, grun = lax.fori_loop(0, qi + 1, step, (jnp.zeros((2 * T, 1), F32), jnp.zeros((2 * T, 1), F32)))
        both_before = jnp.dot(jnp.concatenate([ghi_s[...], glo_s[...]], axis=0), u_before, preferred_element_type=F32)
        dv_s[rows(qi), :] += lax.dot_general(a_s[...], dost, TN_DIMS, preferred_element_type=F32)
        g = g_s[...]
        before = both_before[:2 * T] + both_before[2 * T:]
        dz = (g - jnp.exp(lb_s[qi % 3]) * (g + (grun + before))).astype(BF)
        dkn_s[rows(qi), :] += lax.dot_general(dz, qst, TN_DIMS, preferred_element_type=F32)
        dq_all = dq_s[...] + jnp.dot(dz, kn_s[rows(qi), :], preferred_element_type=F32)
        dqn = jnp.where(lane < HEAD_DIM, dq_all[:T], dq_all[T:])
        gq_v = gq_ref[...]
        t = dqn * gq_v
        dq_ref[...] = (rq * t - q * (rq * rq * rq) * _head_mean(t * q, hm)).astype(BF)
        dgq = jnp.sum(dqn * (q * rq), axis=0, keepdims=True) * SCALE
        first = jnp.logical_and(hp == 0, qi == 0)

        @pl.when(first)
        def _():
            dgq_ref[...] = dgq

        @pl.when(jnp.logical_not(first))
        def _():
            dgq_ref[...] += dgq

        @pl.when(qi == nq - 1)
        def _():
            k = k_ref[...].astype(F32)
            rk = _qk_norm(k, gk_ref[...], hm)[0]
            dkn = dkn_s[...]
            tk = dkn * gk_ref[...]
            dk_ref[...] = (rk * tk - k * (rk * rk * rk) * _head_mean(tk * k, hm)).astype(BF)
            dgk = jnp.sum(dkn * (k * rk), axis=0, keepdims=True)
            dv_ref[...] = dv_s[...].astype(BF)

            @pl.when(hp == 0)
            def _():
                dgk_ref[...] = dgk

            @pl.when(hp > 0)
            def _():
                dgk_ref[...] += dgk

            @pl.when(hp == HEAD_PAIRS - 1)
            def _():
                fold = (lax.broadcasted_iota(jnp.int32, (LANES, LANES), 0) % HEAD_DIM ==
                        lax.broadcasted_iota(jnp.int32, (LANES, LANES), 1) % HEAD_DIM).astype(F32)
                dgq_ref[...] = jnp.dot(dgq_ref[...], fold, precision=HIGHEST, preferred_element_type=F32)
                dgk_ref[...] = jnp.dot(dgk_ref[...], fold, precision=HIGHEST, preferred_element_type=F32)

    gain = pl.BlockSpec((1, LANES), lambda hp, qi: (0, 0))
    full = lambda b0: pl.BlockSpec((S, LANES), lambda hp, qi, b0=b0: (0, b0 + hp))
    tile = pl.BlockSpec((T, LANES), lambda hp, qi: (qi, hp))
    col = pl.BlockSpec((S, LANES), lambda hp, qi: (0, hp))
    dgain = pl.BlockSpec((1, LANES), lambda hp, qi: (0, 0))
    return pl.pallas_call(
        body, name=name, grid=(HEAD_PAIRS, nq),
        in_specs=[pl.BlockSpec((T, LANES), lambda hp, qi: (qi, Q_BLOCK0 + hp)), full(K_BLOCK0), full(V_BLOCK0),
                  tile, pl.BlockSpec((T, LANES), lambda hp, qi: (qi, 2 + hp)), gain, gain],
        out_specs=[tile, col, col, dgain, dgain],
        out_shape=[_sds((S, ATTN_W), BF)] * 3 + [_sds((1, LANES), F32)] * 2,
        scratch_shapes=[pltpu.VMEM((S, LANES), BF), pltpu.VMEM((S, LANES), F32), pltpu.VMEM((S, LANES), F32),
                        pltpu.VMEM((3, 2 * T, T), F32), pltpu.VMEM((2 * T, T), F32), pltpu.VMEM((2 * T, T), F32),
                        pltpu.VMEM((2 * T, LANES), F32)] + [pltpu.VMEM((2 * T, T), BF)] * 5,
        compiler_params=_params("arbitrary", "arbitrary"),
    )(proj, proj, proj, tot, dy, gq, gk)


def ple_bwd(name, dh, gp, pp, w, h, g, after=()):
    S, D = dh.shape
    tm = min(S, 256)

    def body(dh_ref, gp_ref, pp_ref, w_ref, h_ref, g_ref, *rest):
        dgp_ref, dpp_ref, dh2_ref, dg_ref = rest[-4:]
        i = pl.program_id(0)
        d = dh_ref[...]
        gate = jax.nn.sigmoid(gp_ref[...].astype(F32))
        dpp_ref[...] = (d * gate).astype(BF)
        dgp = (d * pp_ref[...].astype(F32) * gate * (1.0 - gate)).astype(BF)
        dgp_ref[...] = dgp
        dyn = lax.dot_general(dgp, w_ref[...], NT_DIMS, preferred_element_type=F32)
        x = h_ref[...]
        r = lax.rsqrt(jnp.mean(x * x, axis=-1, keepdims=True) + EPS)
        t = dyn * g_ref[...]
        dh2_ref[...] = d + r * t - x * (r * r * r) * jnp.mean(t * x, axis=-1, keepdims=True)
        part = jnp.sum(dyn * (x * r), axis=0, keepdims=True)

        @pl.when(i == 0)
        def _():
            dg_ref[...] = part

        @pl.when(i > 0)
        def _():
            dg_ref[...] += part

    tile = pl.BlockSpec((tm, D), lambda i: (i, 0))
    vec = pl.BlockSpec((1, D), lambda i: (0, 0))
    return pl.pallas_call(
        body, name=name, grid=(S // tm,),
        in_specs=[tile, tile, tile, pl.BlockSpec((D, D), lambda i: (0, 0)), tile, vec] +
                 [pl.BlockSpec(memory_space=pl.ANY)] * len(after),
        out_specs=[tile, tile, tile, vec],
        out_shape=[_sds((S, D), BF), _sds((S, D), BF), _sds((S, D), F32), _sds((1, D), F32)],
        compiler_params=_params("arbitrary"),
    )(dh, gp, pp, w, h, g.reshape(1, D), *after)


def loss_head(name, h, target):
    S, D = h.shape
    tm = min(S, 512)

    def body(h_ref, t_ref, loss_ref, dh_ref):
        i = pl.program_id(0)
        e = h_ref[...] - t_ref[...]
        dh_ref[...] = e * (1.0 / D)
        part = jnp.zeros((8, LANES), F32) + 0.5 * jnp.sum(jnp.mean(e * e, axis=-1, keepdims=True))

        @pl.when(i == 0)
        def _():
            loss_ref[...] = part

        @pl.when(i > 0)
        def _():
            loss_ref[...] += part

    tile = pl.BlockSpec((tm, D), lambda i: (i, 0))
    return pl.pallas_call(
        body, name=name, grid=(S // tm,), in_specs=[tile, tile],
        out_specs=[pl.BlockSpec((8, LANES), lambda i: (0, 0)), tile],
        out_shape=[_sds((8, LANES), F32), _sds((S, D), F32)], compiler_params=_params("arbitrary"),
    )(h, target)


def _adamw_math(w, g, m, v):
    c1 = 1.0 - ADAM_B1 ** ADAM_STEP
    c2 = 1.0 - ADAM_B2 ** ADAM_STEP
    nm = ADAM_B1 * m + (1.0 - ADAM_B1) * g
    nv = ADAM_B2 * v + (1.0 - ADAM_B2) * (g * g)
    return -ADAM_LR * ((nm / c1) / (jnp.sqrt(nv / c2) + ADAM_EPS) + ADAM_WD * w), nm, nv


def adamw(name, w, g, m, v):
    R, C = w.shape
    tr = R
    for cand in (512, 256, 128, 64, 32, 16, 8):
        if R % cand == 0:
            tr = cand
            break

    def body(w_ref, g_ref, m_ref, v_ref, d_ref, nm_ref, nv_ref):
        d_ref[...], nm_ref[...], nv_ref[...] = _adamw_math(w_ref[...], g_ref[...], m_ref[...], v_ref[...])

    tile = pl.BlockSpec((tr, C), lambda i: (i, 0))
    return pl.pallas_call(
        body, name=name, grid=(R // tr,), in_specs=[tile] * 4, out_specs=[tile] * 3,
        out_shape=[_sds((R, C), F32)] * 3, compiler_params=_params("parallel"),
    )(w, g, m, v)


def adamw_layer(name, layer, ws, gs, ms, vs, prev, after=()):
    n = len(ws)
    steps = 8

    def body(*refs):
        ins, outs = refs[:4 * n], refs[-4 * n:]
        for i in range(n):
            w_ref, g_ref, m_ref, v_ref = (ins[k * n + i] for k in range(4))
            g = g_ref[...]
            outs[i][...] = g
            outs[n + i][...], outs[2 * n + i][...], outs[3 * n + i][...] = _adamw_math(w_ref[...], g, m_ref[...],
                                                                                        v_ref[...])

    def stacked(a):
        return pl.BlockSpec((None, a.shape[1] // steps, a.shape[2]), lambda t: (layer, t, 0))

    def flat(a):
        return pl.BlockSpec((a.shape[0] // steps, a.shape[1]), lambda t: (t, 0))

    in_specs = [stacked(a) for a in ws] + [flat(a) for a in gs] + [stacked(a) for a in ms] + [stacked(a) for a in vs]
    operands = [*ws, *gs, *ms, *vs]
    aliases = {}
    if prev is not None:
        flat_prev = [a for group in prev for a in group]
        in_specs += [pl.BlockSpec(memory_space=pl.ANY)] * len(flat_prev)
        aliases = {4 * n + i: i for i in range(4 * n)}
        operands += flat_prev
    in_specs += [pl.BlockSpec(memory_space=pl.ANY)] * len(after)
    operands += list(after)
    out = pl.pallas_call(
        body, name=name, grid=(steps,), in_specs=in_specs, out_specs=[stacked(a) for a in ws] * 4,
        out_shape=[_sds(a.shape, F32) for a in ws] * 4, input_output_aliases=aliases,
        compiler_params=_params("parallel"),
    )(*operands)
    return [list(out[k * n:(k + 1) * n]) for k in range(4)]


def _relu2(u):
    r = jnp.maximum(u.astype(F32), 0.0)
    return r * r


def layer_fwd(tag, h0, p_bf, wt, after=(), mid=None):
    hn1, proj = norm_mm(f"{tag}_proj", h0, wt["norm1_g"], wt["w_in"], tn=256, after=after)
    y = conv_fwd(f"{tag}_conv", proj, wt["conv_w"])
    y, yb_tot = attn_fwd(f"{tag}_attn", proj, wt["gq"], wt["gk"], y)
    y = sgu_fwd(f"{tag}_sgu", proj, wt["gv"], wt["sgu_w"], wt["sgu_bias"], y)
    h1 = mm_nn(f"{tag}_out", y, wt["w_out"], extras=(h0,), epi=lambda acc, h: (h + acc,), out_dtypes=(F32,))
    hn2, uu = norm_mm(f"{tag}_ff1", h1, wt["norm2_g"], wt["w_ff1"], after=() if mid is None else mid(yb_tot))
    h2 = mm_nn(f"{tag}_ff2", uu, wt["w_ff2"], pro=_relu2, extras=(h1,), epi=lambda acc, h: (h + acc,),
               out_dtypes=(F32,), tm=512)
    hn3, gp = norm_mm(f"{tag}_gate", h2, wt["norm3_g"], wt["w_ple_gate"])
    h3, pp = mm_nn(f"{tag}_ple", p_bf, wt["w_ple_proj"], extras=(gp, h2),
                   epi=lambda acc, g, h: (h + jax.nn.sigmoid(g.astype(F32)) * acc, acc), out_dtypes=(F32, BF))
    saved = dict(h0=h0, h1=h1, h2=h2, hn1=hn1, hn2=hn2, hn3=hn3, proj=proj, yb_tot=yb_tot, y=y, uu=uu, gp=gp, pp=pp,
                 p_bf=p_bf)
    return h3, saved


def layer_bwd(tag, dh3, sv, wt, after=(), mid=None):
    g = {}
    dgp, dpp, dh2, g["norm3_g"] = ple_bwd(f"{tag}_dple", dh3, sv["gp"], sv["pp"], wt["w_ple_gate"], sv["h2"],
                                          wt["norm3_g"], after)
    g["w_ple_proj"] = mm_tn(f"{tag}_dwp", sv["p_bf"], dpp, col_blocks=N_CHIPS)
    g["w_ple_gate"] = mm_tn(f"{tag}_dwg", sv["hn3"], dgp)

    duu = mm_nt(f"{tag}_dff2", dh2, wt["w_ff2"], extras=(sv["uu"],),
                epi=lambda acc, u: acc * (2.0 * jnp.maximum(u.astype(F32), 0.0)))
    g["w_ff2"] = mm_tn(f"{tag}_dw2", sv["uu"], dh2, pro_x=_relu2)
    g["w_ff1"] = mm_tn(f"{tag}_dw1", sv["hn2"], duu, col_blocks=N_CHIPS)
    dh1, g["norm2_g"] = mm_nt_rmsbwd(f"{tag}_dnorm2", duu, wt["w_ff1"], sv["h1"], wt["norm2_g"], dh2)

    dy = mm_nt(f"{tag}_dout", dh1, wt["w_out"], after=() if mid is None else mid(dh1))
    g["w_out"] = mm_tn(f"{tag}_dwo", sv["y"], dh1)
    dab, dac, dah, g["conv_w"] = conv_bwd(f"{tag}_dconv", dy, sv["proj"], wt["conv_w"])
    dq, dk, dv, g["gq"], g["gk"] = attn_bwd(f"{tag}_dattn", dy, sv["proj"], sv["yb_tot"], wt["gq"], wt["gk"])
    dcu, dcv, g["sgu_w"], g["sgu_bias"], g["gv"] = sgu_bwd(f"{tag}_dsgu", dy, sv["proj"], wt["gv"], wt["sgu_w"],
                                                           wt["sgu_bias"])
    dproj = jnp.concatenate([dab, dac, dah, dq, dk, dv, dcu, dcv], axis=-1)
    g["w_in"] = mm_tn(f"{tag}_dwi", sv["hn1"], dproj, tn=1408)
    dh0, g["norm1_g"] = mm_nt_rmsbwd(f"{tag}_dnorm1", dproj, wt["w_in"], sv["h0"], wt["norm1_g"], dh1)
    return dh0, g


def prep_small(norm1_g, q_norm_g, k_norm_g, sgu_norm_g, sgu_w, sgu_b, norm2_g, norm3_g, conv_w_full):
    return dict(
        norm1_g=norm1_g, norm2_g=norm2_g, norm3_g=norm3_g, conv_w=conv_w_full,
        gq=(jnp.tile(q_norm_g, 2) * SCALE).reshape(1, LANES), gk=jnp.tile(k_norm_g, 2).reshape(1, LANES),
        gv=sgu_norm_g.reshape(1, SGU_W), sgu_w=sgu_w, sgu_bias=jnp.repeat(sgu_b.T, HEAD_DIM, axis=1))


def small_grads(g):
    return dict(
        norm1_g=g["norm1_g"][0], norm2_g=g["norm2_g"][0], norm3_g=g["norm3_g"][0], conv_w=g["conv_w"],
        q_norm_g=g["gq"][0, :HEAD_DIM], k_norm_g=g["gk"][0, :HEAD_DIM], sgu_norm_g=g["gv"][0], sgu_w=g["sgu_w"],
        sgu_b=g["sgu_bias"][:, :SGU_HEADS].T)


HBM_SPEC = pl.BlockSpec(memory_space=pltpu.HBM)
BIG = ("w_in", "w_out", "w_ff1", "w_ff2", "w_ple_gate", "w_ple_proj")


def _mesh_pos():
    return lax.axis_index("x"), lax.axis_index("y"), lax.axis_index("c")


def _other_chips(x, y):
    return [(1 - x, y), (x, 1 - y), (1 - x, 1 - y)]


def _half(rows, core):
    h = rows // 2
    return pl.ds(pl.multiple_of(core * h, 16), h)


def _remote(src, dst, send_sems, recv_sems, k, to):
    return pltpu.make_async_remote_copy(src_ref=src, dst_ref=dst, send_sem=send_sems.at[k], recv_sem=recv_sems.at[k],
                                        device_id=to, device_id_type=MESH)


SEM_SPEC = pl.BlockSpec(memory_space=pltpu.SEMAPHORE)
ANY_SPEC = pl.BlockSpec(memory_space=pl.ANY)
SIDE_EFFECT = pltpu.SideEffectType.DATAFLOW_SIDE_EFFECTING


def _in_hbm(arrays):
    return [pltpu.with_memory_space_constraint(a, pltpu.HBM) for a in arrays]


def copies_start(name, srcs, lands, plan, after=()):
    ns, nl, na = len(srcs), len(lands), len(after)

    def body(*refs):
        src_refs, land_refs = refs[:ns], refs[ns:ns + nl]
        send_sem, recv_sem = refs[ns + nl + na], refs[ns + nl + na + 1]
        token = refs[-1]
        for src, dst, dev in plan(src_refs, land_refs, *_mesh_pos()):
            pltpu.make_async_remote_copy(src_ref=src, dst_ref=dst, send_sem=send_sem, recv_sem=recv_sem,
                                         device_id=dev, device_id_type=MESH).start()
        token[...] = jnp.zeros_like(token)

    out = pl.pallas_call(
        body, name=name,
        in_specs=[HBM_SPEC] * (ns + nl) + [ANY_SPEC] * na,
        out_specs=(SEM_SPEC, SEM_SPEC, *[HBM_SPEC] * (ns + nl), pl.BlockSpec(memory_space=pltpu.VMEM)),
        out_shape=(pltpu.SemaphoreType.DMA(()), pltpu.SemaphoreType.DMA(()),
                   *[pltpu.HBM(a.shape, a.dtype) for a in (*srcs, *lands)], _sds((8, LANES), F32)),
        input_output_aliases={i: 2 + i for i in range(ns + nl)},
        compiler_params=pltpu.CompilerParams(has_side_effects=SIDE_EFFECT),
    )(*_in_hbm(srcs), *_in_hbm(lands), *after)
    return out[0], out[1], list(out[2:2 + ns]), list(out[2 + ns:2 + ns + nl]), out[-1]


def copies_wait(name, started, plan, after=()):
    send_sem, recv_sem, srcs, lands, _ = started
    ns, nl, na = len(srcs), len(lands), len(after)

    def body(*refs):
        src_refs, land_refs = refs[:ns], refs[ns:ns + nl]
        send_sem, recv_sem = refs[ns + nl], refs[ns + nl + 1]
        for src, dst, dev in plan(src_refs, land_refs, *_mesh_pos()):
            cp = pltpu.make_async_remote_copy(src_ref=src, dst_ref=dst, send_sem=send_sem, recv_sem=recv_sem,
                                              device_id=dev, device_id_type=MESH)
            cp.wait_send()
            cp.wait_recv()

    out = pl.pallas_call(
        body, name=name,
        in_specs=[HBM_SPEC] * (ns + nl) + [SEM_SPEC, SEM_SPEC] + [ANY_SPEC] * na,
        out_specs=[HBM_SPEC] * (ns + nl),
        out_shape=[pltpu.HBM(a.shape, a.dtype) for a in (*srcs, *lands)],
        input_output_aliases={i: i for i in range(ns + nl)},
        compiler_params=pltpu.CompilerParams(has_side_effects=SIDE_EFFECT),
    )(*srcs, *lands, send_sem, recv_sem, *after)
    return list(out[:ns]), list(out[ns:])


def _gather_plan(srcs, lands, x, y, c):
    me = 2 * x + y
    return [(src.at[_half(src.shape[0], c)], land.at[me, _half(src.shape[0], c)], (*chip, c))
            for src, land in zip(srcs, lands) for chip in _other_chips(x, y)]


def _gather_arrivals(srcs, lands, x, y, c):
    return [(src.at[_half(src.shape[0], c)], land.at[2 * chip[0] + chip[1], _half(src.shape[0], c)], (*chip, c))
            for src, land in zip(srcs, lands) for chip in _other_chips(x, y)]


def _forward_plan(srcs, lands, x, y, c):
    me, sibling = 2 * x + y, (x, y, 1 - c)
    out = []
    for src, land in zip(srcs, lands):
        out.append((src, land.at[me], sibling))
        for chip in _other_chips(x, y):
            region = land.at[2 * chip[0] + chip[1], _half(src.shape[0], c)]
            out.append((region, region, sibling))
    return out


def _forward_arrivals(srcs, lands, x, y, c):
    me, sibling = 2 * x + y, (x, y, 1 - c)
    out = []
    for src, land in zip(srcs, lands):
        out.append((src, land.at[me], sibling))
        for chip in _other_chips(x, y):
            slot = land.at[2 * chip[0] + chip[1]]
            out.append((slot.at[_half(src.shape[0], c)], slot.at[_half(src.shape[0], 1 - c)], sibling))
    return out


def _join_plan(srcs, lands, x, y, c):
    return [(land.at[_half(land.shape[0], c)], land.at[_half(land.shape[0], c)], (x, y, 1 - c)) for land in lands]


def _join_arrivals(srcs, lands, x, y, c):
    return [(land.at[_half(land.shape[0], c)], land.at[_half(land.shape[0], 1 - c)], (x, y, 1 - c)) for land in lands]


def _pair_plan(srcs, lands, x, y, c):
    return [(src.at[:, _half(src.shape[1], 1 - c)], land, (x, y, 1 - c)) for src, land in zip(srcs, lands)]


def add_own_half(name, core, grads, got):
    n = len(grads)

    def body(core_ref, *refs):
        for i in range(n):
            refs[2 * n + i][...] = (refs[i][...].astype(F32) + refs[n + i][...].astype(F32)).astype(BF)

    def spec(g, own):
        blk = (None, g.shape[1] // 2, g.shape[2])
        return pl.BlockSpec(blk, (lambda j, core_ref: (j, core_ref[0], 0)) if own else (lambda j, core_ref: (j, 0, 0)))

    return pl.pallas_call(
        body, name=name,
        grid_spec=pltpu.PrefetchScalarGridSpec(
            num_scalar_prefetch=1, grid=(N_CHIPS,),
            in_specs=[spec(g, True) for g in grads] + [spec(g, False) for g in grads],
            out_specs=[spec(g, False) for g in grads]),
        out_shape=[_sds(r.shape, BF) for r in got], compiler_params=_params("parallel"),
    )(core, *grads, *got)


def _chips_plan(srcs, lands, x, y, c):
    return [(src.at[2 * chip[0] + chip[1]], land.at[k], (*chip, c))
            for src, land in zip(srcs, lands) for k, chip in enumerate(_other_chips(x, y))]


def sum_chips(name, place, parts, got):
    n = len(got)

    def body(place_ref, *refs):
        for i in range(n):
            acc = refs[i][...].astype(F32)
            for k in range(N_CHIPS - 1):
                acc = acc + refs[n + i][k].astype(F32)
            refs[2 * n + i][...] = acc

    steps = 2
    return pl.pallas_call(
        body, name=name,
        grid_spec=pltpu.PrefetchScalarGridSpec(
            num_scalar_prefetch=1, grid=(steps,),
            in_specs=[pl.BlockSpec((None, g.shape[1] // steps, g.shape[2]), lambda t, place_ref: (place_ref[0], t, 0))
                      for g in parts] +
                     [pl.BlockSpec((N_CHIPS - 1, g.shape[1] // steps, g.shape[2]), lambda t, place_ref: (0, t, 0))
                      for g in got],
            out_specs=[pl.BlockSpec((g.shape[1] // steps, g.shape[2]),
                                    lambda t, place_ref: (place_ref[1] * steps + t, 0)) for g in got]),
        out_shape=[_sds((2 * g.shape[1], g.shape[2]), F32) for g in got], compiler_params=_params("parallel"),
    )(place, *parts, *got)


def reduce_scatter_pair(tag, grads):
    lands = [lax.empty((N_CHIPS, g.shape[1] // 2, g.shape[2]), g.dtype) for g in grads]
    return copies_start(f"{tag}_rs_pair_start", grads, lands, _pair_plan)


def reduce_scatter_begin(tag, core, pairing, after):
    grads, got = copies_wait(f"{tag}_rs_pair_wait", pairing, _pair_plan, after)
    parts = add_own_half(f"{tag}_rs_add", core, grads, got)
    lands = [lax.empty((N_CHIPS - 1,) + p.shape[1:], p.dtype) for p in parts]
    return copies_start(f"{tag}_rs_start", parts, lands, _chips_plan)


def reduce_scatter_sum(tag, place, started, after):
    parts, got = copies_wait(f"{tag}_rs_wait", started, _chips_plan, after)
    return copies_start(f"{tag}_rs_join_start", [], sum_chips(f"{tag}_rs_sum", place, parts, got), _join_plan)


def reduce_scatter_end(tag, joining, after):
    return copies_wait(f"{tag}_rs_join_wait", joining, _join_arrivals, after)[1]


def small_allreduce(name, x, after=()):
    R = x.shape[0]
    H = R // 2

    def body(x_ref, *rest):
        o_ref, pair_ref, chip_ref, send_sems, recv_sems = rest[-5:]
        xx, yy, c = _mesh_pos()
        me = 2 * xx + yy
        chips = _other_chips(xx, yy)
        sibling = (xx, yy, 1 - c)
        mine = pl.ds(pl.multiple_of(c * H, 8), H)
        theirs = pl.ds(pl.multiple_of((1 - c) * H, 8), H)
        a = _remote(x_ref.at[theirs], pair_ref.at[theirs], send_sems, recv_sems, 0, sibling)
        a.start()
        a.wait_send()
        _remote(x_ref.at[mine], pair_ref.at[mine], send_sems, recv_sems, 0, sibling).wait_recv()
        chip_ref[me] = x_ref[mine, :] + pair_ref[mine, :]
        sends = []
        for k, chip in enumerate(chips):
            cp = _remote(chip_ref.at[me], chip_ref.at[me], send_sems, recv_sems, 1 + k, (*chip, c))
            cp.start()
            sends.append(cp)
        for k, chip in enumerate(chips):
            slot = chip_ref.at[2 * chip[0] + chip[1]]
            _remote(slot, slot, send_sems, recv_sems, 1 + k, (*chip, c)).wait_recv()
        o_ref[mine, :] = (chip_ref[0] + chip_ref[1]) + (chip_ref[2] + chip_ref[3])
        b = _remote(o_ref.at[mine], o_ref.at[mine], send_sems, recv_sems, 4, sibling)
        b.start()
        b.wait_send()
        _remote(o_ref.at[theirs], o_ref.at[theirs], send_sems, recv_sems, 4, sibling).wait_recv()
        for cp in sends:
            cp.wait_send()

    return pl.pallas_call(
        body, name=name,
        in_specs=[pl.BlockSpec(memory_space=pltpu.VMEM)] + [pl.BlockSpec(memory_space=pl.ANY)] * len(after),
        out_specs=pl.BlockSpec(memory_space=pltpu.VMEM), out_shape=_sds((R, LANES), F32),
        scratch_shapes=[pltpu.VMEM((R, LANES), F32), pltpu.VMEM((N_CHIPS, H, LANES), F32),
                        pltpu.SemaphoreType.DMA((5,)), pltpu.SemaphoreType.DMA((5,))],
        compiler_params=pltpu.CompilerParams(vmem_limit_bytes=VMEM_LIMIT),
    )(x, *after)


WEIGHTS = ("norm1_g", "w_in", "conv_w", "q_norm_g", "k_norm_g", "sgu_norm_g", "sgu_w", "sgu_b", "w_out", "norm2_g",
           "w_ff1", "w_ff2", "norm3_g", "w_ple_gate", "w_ple_proj")
SMALL = ("norm1_g", "norm2_g", "norm3_g", "q_norm_g", "k_norm_g", "sgu_norm_g", "sgu_w", "sgu_b", "conv_w")


PACK = 8 * LANES


def _pack_rows(arrays):
    blocks = []
    for a in arrays:
        v = a.reshape(-1)
        blocks.append(jnp.pad(v, (0, (-v.shape[0]) % PACK)).reshape(-1, LANES))
    rows = sum(b.shape[0] for b in blocks)
    if rows % 16:
        blocks.append(jnp.zeros((16 - rows % 16, LANES), F32))
    return jnp.concatenate(blocks, axis=0)


def _unpack_rows(packed, shapes):
    out, pos = [], 0
    flat = packed.reshape(-1)
    for shp in shapes:
        size = math.prod(shp)
        out.append(flat[pos:pos + size].reshape(shp))
        pos += size + (-size) % PACK
    return out


def kernel(x, p, norm1_g, w_in, conv_w, q_norm_g, k_norm_g, sgu_norm_g, sgu_w, sgu_b, w_out, norm2_g, w_ff1, w_ff2, norm3_g, w_ple_gate, w_ple_proj, loss_target, m_norm1_g, m_w_in, m_conv_w, m_q_norm_g, m_k_norm_g, m_sgu_norm_g, m_sgu_w, m_sgu_b, m_w_out, m_norm2_g, m_w_ff1, m_w_ff2, m_norm3_g, m_w_ple_gate, m_w_ple_proj, v_norm1_g, v_w_in, v_conv_w, v_q_norm_g, v_k_norm_g, v_sgu_norm_g, v_sgu_w, v_sgu_b, v_w_out, v_norm2_g, v_w_ff1, v_w_ff2, v_norm3_g, v_w_ple_gate, v_w_ple_proj):
    w = dict(norm1_g=norm1_g, w_in=w_in, conv_w=conv_w, q_norm_g=q_norm_g, k_norm_g=k_norm_g, sgu_norm_g=sgu_norm_g,
             sgu_w=sgu_w, sgu_b=sgu_b, w_out=w_out, norm2_g=norm2_g, w_ff1=w_ff1, w_ff2=w_ff2, norm3_g=norm3_g,
             w_ple_gate=w_ple_gate, w_ple_proj=w_ple_proj)
    m = dict(norm1_g=m_norm1_g, w_in=m_w_in, conv_w=m_conv_w, q_norm_g=m_q_norm_g, k_norm_g=m_k_norm_g,
             sgu_norm_g=m_sgu_norm_g, sgu_w=m_sgu_w, sgu_b=m_sgu_b, w_out=m_w_out, norm2_g=m_norm2_g, w_ff1=m_w_ff1,
             w_ff2=m_w_ff2, norm3_g=m_norm3_g, w_ple_gate=m_w_ple_gate, w_ple_proj=m_w_ple_proj)
    v = dict(norm1_g=v_norm1_g, w_in=v_w_in, conv_w=v_conv_w, q_norm_g=v_q_norm_g, k_norm_g=v_k_norm_g,
             sgu_norm_g=v_sgu_norm_g, sgu_w=v_sgu_w, sgu_b=v_sgu_b, w_out=v_w_out, norm2_g=v_norm2_g, w_ff1=v_w_ff1,
             w_ff2=v_w_ff2, norm3_g=v_norm3_g, w_ple_gate=v_w_ple_gate, w_ple_proj=v_w_ple_proj)
    depth = w_in.shape[0]
    d_model = x.shape[-1]
    chip = 2 * lax.axis_index("x") + lax.axis_index("y")
    core = lax.axis_index("c")
    core_arr = core.reshape(1).astype(jnp.int32)

    cw_cols = conv_w.shape[-1]

    h = x[0]
    p_bf = p[:, 0].astype(BF)
    saved, full = [], []

    def gather_start(l, after):
        shards = [w[n][l].astype(BF) for n in BIG]
        lands = [lax.empty((N_CHIPS,) + s.shape, BF) for s in shards]
        return copies_start(f"l{l}_gather_start", shards, lands, _gather_plan, after)

    def gather_forward(l, started, after):
        shards, lands = copies_wait(f"l{l}_gather_wait", started, _gather_arrivals, after)
        forwarding = copies_start(f"l{l}_forward_start", shards, lands, _forward_plan)
        return forwarding, gather_start(l + 1, (forwarding[-1],)) if l + 1 < depth else None

    first = gather_start(0, ())
    placed = lax.dynamic_update_slice(jnp.zeros((depth, 3, CONV_W), F32), conv_w, (0, 0, chip * cw_cols))
    placed = jnp.where(core == 0, placed, 0.0)
    conv_full = _unpack_rows(small_allreduce("conv_w_gather", _pack_rows([placed]), (first[-1],)),
                             [(depth, 3, CONV_W)])[0]
    forwarding, started = gather_forward(0, first, (h, conv_full))
    for l in range(depth):
        g_in, g_out, g_ff1, g_ff2, g_gate, g_proj = copies_wait(f"l{l}_forward_wait", forwarding, _forward_arrivals,
                                                                 (h,))[1]
        nxt = {}

        def mid(arr, l=l, started=started, nxt=nxt):
            nxt["forwarding"], nxt["started"] = gather_forward(l + 1, started, (arr,))
            return [t[-1] for t in nxt.values() if t is not None]

        wt = prep_small(norm1_g[l], q_norm_g[l], k_norm_g[l], sgu_norm_g[l], sgu_w[l], sgu_b[l], norm2_g[l], norm3_g[l],
                        conv_full[l])
        wt["w_in"] = jnp.transpose(g_in, (1, 0, 2)).reshape(d_model, -1)
        wt["w_out"] = g_out.reshape(-1, d_model)
        wt["w_ff1"] = g_ff1
        wt["w_ff2"] = g_ff2.reshape(-1, d_model)
        wt["w_ple_gate"] = g_gate.reshape(-1, d_model)
        wt["w_ple_proj"] = g_proj
        last = l + 1 == depth
        h, sv = layer_fwd(f"l{l}", h, p_bf[l], wt, () if last else (started[-1],), None if last else mid)
        if not last:
            forwarding, started = nxt["forwarding"], nxt["started"]
        saved.append(sv)
        full.append(wt)

    loss_tile, dh = loss_head("loss", h, loss_target[0])
    loss = lax.psum(loss_tile[0, 0], ("x", "y", "c"))

    small = [None] * depth
    chip_arr = jnp.stack([chip, core]).astype(jnp.int32)
    big_w, big_m, big_v = ([d[n] for n in BIG] for d in (w, m, v))
    pairing, joining, tokens = None, {}, ()
    for l in reversed(range(depth)):
        box = {}

        def mid(arr, l=l, pairing=pairing, box=box):
            box["started"] = reduce_scatter_begin(f"l{l + 1}", core_arr, pairing, (arr,))
            return (box["started"][-1],)

        dh, g = layer_bwd(f"l{l}", dh, saved[l], full[l], tokens, None if pairing is None else mid)
        tokens = ()
        if pairing is not None:
            joining[l + 1] = reduce_scatter_sum(f"l{l + 1}", chip_arr, box["started"], (dh,))
            tokens = (joining[l + 1][-1],)
        small[l] = small_grads(g)
        shards_in = w_in.shape[-1]
        gl = [jnp.transpose(g["w_in"].reshape(d_model, N_CHIPS, shards_in), (1, 0, 2)),
              g["w_out"].reshape(N_CHIPS, -1, d_model), g["w_ff1"], g["w_ff2"].reshape(N_CHIPS, -1, d_model),
              g["w_ple_gate"].reshape(N_CHIPS, -1, d_model), g["w_ple_proj"]]
        pairing = reduce_scatter_pair(f"l{l}", gl)
        tokens += (pairing[-1],)

    started = reduce_scatter_begin("l0", core_arr, pairing, (dh,))
    updated = None
    for l in reversed(range(1, depth)):
        reduced = reduce_scatter_end(f"l{l}", joining[l], (started[-1],))
        updated = adamw_layer(f"l{l}_adamw", l, big_w, reduced, big_m, big_v, updated)
    grads, delta, new_m, new_v = {}, {}, {}, {}
    packed = _pack_rows([small[l][n] for l in range(depth) for n in SMALL])
    shapes = [small[l][n].shape for l in range(depth) for n in SMALL]
    pieces = _unpack_rows(small_allreduce("small_grads", packed), shapes)
    for i, n in enumerate(SMALL):
        grads[n] = jnp.stack([pieces[l * len(SMALL) + i] for l in range(depth)])
    grads["conv_w"] = lax.dynamic_slice(grads["conv_w"], (0, 0, chip * cw_cols), (depth, 3, cw_cols))
    for n in SMALL:
        shp = w[n].shape
        two_d = (-1, shp[-1]) if n != "sgu_w" else (-1, LANES)
        d, nm, nv = adamw(f"adamw_{n}", w[n].reshape(two_d), grads[n].reshape(two_d), m[n].reshape(two_d),
                          v[n].reshape(two_d))
        delta[n], new_m[n], new_v[n] = d.reshape(shp), nm.reshape(shp), nv.reshape(shp)

    done = [new_v[n] for n in SMALL] + ([] if updated is None else [updated[3][0]])
    reduced = reduce_scatter_end("l0", reduce_scatter_sum("l0", chip_arr, started, (dh, *done)), ())
    updated = adamw_layer("l0_adamw", 0, big_w, reduced, big_m, big_v, updated)
    for k, d in enumerate((grads, delta, new_m, new_v)):
        d.update(zip(BIG, updated[k]))

    return (loss, dh[None], *[grads[n] for n in WEIGHTS], *[delta[n] for n in WEIGHTS], *[new_m[n] for n in WEIGHTS],
            *[new_v[n] for n in WEIGHTS])
```
